```python
import math
import jax, jax.numpy as jnp
from jax import lax
import numpy as np

D_MODEL = 1024
BATCH = 8
SEQ = 8192
DEPTH = 1

ATTN_HEADS = 8
ATTN_HEAD_DIM = 64
ATTN_WIDTH = ATTN_HEADS * ATTN_HEAD_DIM
DILATED_PATTERNS = ((128, 1), (512, 4), (2048, 16))
ATTN_BLOCK = 128
N_BUCKETS = 32
MAX_DISTANCE = 2048
NEG_INF = -1e30
HGRN_HEADS = 8
HGRN_KEY_DIM = 128
HGRN_VAL_DIM = 128
HGRN_FDIM = HGRN_HEADS * HGRN_KEY_DIM
HGRN_WIDTH = HGRN_HEADS * HGRN_VAL_DIM
HGRN_CHUNK = 64
IN_WIDTH = 4 * ATTN_WIDTH + 2 * HGRN_FDIM + 2 * HGRN_WIDTH + 2 * D_MODEL
EPS = 1e-6

kernel_name = "hybrid_dilated_attn_hgrn2_gated_merge"


def rmsnorm(x, g):
    xf = x.astype(jnp.float32)
    y = xf * lax.rsqrt(jnp.mean(xf * xf, axis=-1, keepdims=True) + EPS)
    return (y * g.astype(jnp.float32)).astype(x.dtype)


def t5_bucket(dist):
    max_exact = N_BUCKETS // 2
    n = dist.astype(jnp.float32)
    large = max_exact + (jnp.log(jnp.maximum(n, 1.0) / max_exact)
                         / math.log(MAX_DISTANCE / max_exact)
                         * (N_BUCKETS - max_exact)).astype(jnp.int32)
    large = jnp.minimum(large, N_BUCKETS - 1)
    return jnp.where(dist < max_exact, dist, large)


def dilated_pattern(q, k, v, rel_bias, window, dilation):
    B, S, H, E = q.shape
    L = S // dilation
    span = window // dilation
    nb = -(-L // ATTN_BLOCK)
    Lp = nb * ATTN_BLOCK

    def to_sub(t):
        t = t.reshape(B, L, dilation, H, E).transpose(0, 2, 3, 1, 4)
        return jnp.pad(t, ((0, 0), (0, 0), (0, 0), (0, Lp - L), (0, 0)))

    def kv_blocks(t):
        tp = jnp.pad(to_sub(t), ((0, 0), (0, 0), (0, 0), (ATTN_BLOCK, 0), (0, 0)))
        prev = tp[:, :, :, :Lp].reshape(B, dilation, H, nb, ATTN_BLOCK, E)
        cur = tp[:, :, :, ATTN_BLOCK:].reshape(B, dilation, H, nb, ATTN_BLOCK, E)
        return jnp.concatenate([prev, cur], axis=4)

    qs = to_sub(q).reshape(B, dilation, H, nb, ATTN_BLOCK, E)
    ks, vs = kv_blocks(k), kv_blocks(v)

    qi = jnp.arange(ATTN_BLOCK)[:, None]
    kj = jnp.arange(2 * ATTN_BLOCK)[None, :]
    delta = qi + ATTN_BLOCK - kj
    band = (delta >= 0) & (delta <= span)
    key_pos = jnp.arange(nb)[:, None, None] * ATTN_BLOCK + kj[None] - ATTN_BLOCK
    mask = band[None] & (key_pos >= 0)
    bucket = t5_bucket(jnp.clip(delta, 0, None) * dilation)
    bias = rel_bias.astype(jnp.float32)[bucket].transpose(2, 0, 1)

    s = jnp.einsum('bdhnqe,bdhnke->bdhnqk', qs, ks) * (E ** -0.5) + bias[None, None, :, None]
    s = jnp.where(mask, s, NEG_INF)
    m = jnp.max(s, axis=-1, keepdims=True)
    p = jnp.exp(s - m)
    den = jnp.sum(p, axis=-1, keepdims=True)
    o = jnp.einsum('bdhnqk,bdhnke->bdhnqe', p, vs) / den
    lse = (m + jnp.log(den))[..., 0]

    o = o.reshape(B, dilation, H, Lp, E)[:, :, :, :L].transpose(0, 3, 1, 2, 4).reshape(B, S, H, E)
    lse = lse.reshape(B, dilation, H, Lp)[..., :L].transpose(0, 3, 1, 2).reshape(B, S, H)
    return o, lse


def dilated_attention(q, k, v, rel_bias):
    outs, lses = [], []
    for window, dilation in DILATED_PATTERNS:
        o, lse = dilated_pattern(q, k, v, rel_bias, window, dilation)
        outs.append(o)
        lses.append(lse)
    w = jax.nn.softmax(jnp.stack(lses, 0), axis=0)
    return jnp.einsum('gbsh,gbshe->bshe', w, jnp.stack(outs, 0))


def hgrn2_recurrence(q, f_raw, i, lb):
    B, S, H, DK = q.shape
    DV = i.shape[-1]
    C = HGRN_CHUNK
    nc = S // C
    f = lb + (1.0 - lb) * jax.nn.sigmoid(f_raw)
    g = jnp.log(f)
    k = 1.0 - f

    def chunks(t):
        return t.reshape(B, nc, C, H, t.shape[-1]).transpose(1, 0, 3, 2, 4)

    causal = jnp.tril(jnp.ones((C, C), dtype=bool))

    def step(state, inp):
        qc, kc, vc, gc = inp
        b = jnp.cumsum(gc, axis=2)
        o_inter = jnp.einsum('bhtk,bhkv->bhtv', qc * jnp.exp(b), state)
        diff = b[:, :, :, None, :] - b[:, :, None, :, :]
        decay = jnp.exp(jnp.where(causal[:, :, None], diff, -jnp.inf))
        a = jnp.einsum('bhtk,bhsk,bhtsk->bhts', qc, kc, decay)
        o_intra = jnp.einsum('bhts,bhsv->bhtv', a, vc)
        b_last = b[:, :, -1:, :]
        new_state = (jnp.exp(b_last[:, :, 0, :])[..., None] * state
                     + jnp.einsum('bhsk,bhsv->bhkv', kc * jnp.exp(b_last - b), vc))
        return new_state, o_inter + o_intra

    s0 = jnp.zeros((B, H, DK, DV), jnp.float32)
    _, o = lax.scan(step, s0, (chunks(q), chunks(k), chunks(i), chunks(g)))
    return o.transpose(1, 0, 3, 2, 4).reshape(B, S, H, DV)


def _fwd_setup_inputs(seed: int = 0) -> dict:
    key = jax.random.key(seed)
    ks = jax.random.split(key, 13)
    D = D_MODEL
    nrm = lambda k, shape, fan_in: jax.random.normal(k, shape, jnp.float32) * fan_in ** -0.5
    return {
        "x": jax.random.normal(ks[0], (BATCH, SEQ, D), jnp.float32),
        "c": jax.random.normal(ks[1], (BATCH, D), jnp.float32),
        "w_ada": nrm(ks[2], (DEPTH, D, 3 * D), D),
        "b_ada": 0.02 * jax.random.normal(ks[3], (DEPTH, 3 * D), jnp.float32),
        "norm_g": 1.0 + 0.05 * jax.random.normal(ks[4], (DEPTH, D), jnp.float32),
        "w_in": nrm(ks[5], (DEPTH, D, IN_WIDTH), D),
        "hgrn_onorm_g": 1.0 + 0.05 * jax.random.normal(ks[6], (DEPTH, HGRN_VAL_DIM), jnp.float32),
        "w_branch_a": nrm(ks[7], (DEPTH, ATTN_WIDTH, D), ATTN_WIDTH),
        "w_branch_b": nrm(ks[8], (DEPTH, HGRN_WIDTH, D), HGRN_WIDTH),
        "w_out": nrm(ks[9], (DEPTH, D, D), D),
        "rel_bias": 0.5 * jax.random.normal(ks[10], (N_BUCKETS, ATTN_HEADS), jnp.float32),
        "hgrn_lb": 0.5 * jax.random.normal(ks[11], (DEPTH + 1, HGRN_FDIM), jnp.float32),
        "final_g": 1.0 + 0.05 * jax.random.normal(ks[12], (D,), jnp.float32),
    }


def _fwd_reference(x, c, w_ada, b_ada, norm_g, w_in, hgrn_onorm_g, w_branch_a, w_branch_b,
              w_out, rel_bias, hgrn_lb, final_g):
    B, S, D = x.shape
    sizes = [ATTN_WIDTH] * 4 + [HGRN_FDIM, HGRN_FDIM, HGRN_WIDTH, HGRN_WIDTH, D_MODEL, D_MODEL]
    cuts = [int(v) for v in np.cumsum(sizes)[:-1]]
    lower_bounds = jnp.cumsum(jax.nn.softmax(hgrn_lb.astype(jnp.float32), axis=0), axis=0)
    for l in range(DEPTH):
        mod = jax.nn.silu(c) @ w_ada[l] + b_ada[l]
        shift, scale, gate = jnp.split(mod, 3, axis=-1)
        h = rmsnorm(x, norm_g[l]) * (1.0 + scale[:, None]) + shift[:, None]
        proj = h @ w_in[l]
        q_a, k_a, v_a, z_a, q_b, f_b, i_b, z_b, g_a, g_b = jnp.split(proj, cuts, axis=-1)

        heads_a = lambda t: t.astype(jnp.float32).reshape(B, S, ATTN_HEADS, ATTN_HEAD_DIM)
        o_a = dilated_attention(heads_a(q_a), heads_a(k_a), heads_a(v_a), rel_bias)
        o_a = o_a.reshape(B, S, ATTN_WIDTH).astype(x.dtype) * jax.nn.silu(z_a)

        heads_b = lambda t, e: t.astype(jnp.float32).reshape(B, S, HGRN_HEADS, e)
        lb = lower_bounds[l].reshape(HGRN_HEADS, HGRN_KEY_DIM)
        o_b = hgrn2_recurrence(jax.nn.silu(heads_b(q_b, HGRN_KEY_DIM)), heads_b(f_b, HGRN_KEY_DIM),
                               heads_b(i_b, HGRN_VAL_DIM), lb)
        o_b = rmsnorm(o_b, hgrn_onorm_g[l]).reshape(B, S, HGRN_WIDTH).astype(x.dtype) * jax.nn.silu(z_b)

        y = jax.nn.sigmoid(g_a) * (o_a @ w_branch_a[l]) + jax.nn.sigmoid(g_b) * (o_b @ w_branch_b[l])
        x = x + gate[:, None] * (y @ w_out[l])
    return rmsnorm(x, final_g)


import jax as _jax
import jax.numpy as _jnp

TWIN_FORMAT = 'train_step'
FWD_PARAMS = ['x', 'c', 'w_ada', 'b_ada', 'norm_g', 'w_in', 'hgrn_onorm_g', 'w_branch_a', 'w_branch_b', 'w_out', 'rel_bias', 'hgrn_lb', 'final_g']
TWIN_WEIGHTS = ['w_ada', 'b_ada', 'norm_g', 'w_in', 'hgrn_onorm_g', 'w_branch_a', 'w_branch_b', 'w_out', 'rel_bias', 'hgrn_lb', 'final_g']
TWIN_DIFF_INPUT = 'x'
TWIN_INPUTS = ['x', 'c', 'w_ada', 'b_ada', 'norm_g', 'w_in', 'hgrn_onorm_g', 'w_branch_a', 'w_branch_b', 'w_out', 'rel_bias', 'hgrn_lb', 'final_g', 'loss_target', 'm_w_ada', 'm_b_ada', 'm_norm_g', 'm_w_in', 'm_hgrn_onorm_g', 'm_w_branch_a', 'm_w_branch_b', 'm_w_out', 'm_rel_bias', 'm_hgrn_lb', 'm_final_g', 'v_w_ada', 'v_b_ada', 'v_norm_g', 'v_w_in', 'v_hgrn_onorm_g', 'v_w_branch_a', 'v_w_branch_b', 'v_w_out', 'v_rel_bias', 'v_hgrn_lb', 'v_final_g']
TWIN_OUTPUTS = ['loss', 'grad_x', 'grad_w_ada', 'grad_b_ada', 'grad_norm_g', 'grad_w_in', 'grad_hgrn_onorm_g', 'grad_w_branch_a', 'grad_w_branch_b', 'grad_w_out', 'grad_rel_bias', 'grad_hgrn_lb', 'grad_final_g', 'delta_w_ada', 'delta_b_ada', 'delta_norm_g', 'delta_w_in', 'delta_hgrn_onorm_g', 'delta_w_branch_a', 'delta_w_branch_b', 'delta_w_out', 'delta_rel_bias', 'delta_hgrn_lb', 'delta_final_g', 'new_m_w_ada', 'new_m_b_ada', 'new_m_norm_g', 'new_m_w_in', 'new_m_hgrn_onorm_g', 'new_m_w_branch_a', 'new_m_w_branch_b', 'new_m_w_out', 'new_m_rel_bias', 'new_m_hgrn_lb', 'new_m_final_g', 'new_v_w_ada', 'new_v_b_ada', 'new_v_norm_g', 'new_v_w_in', 'new_v_hgrn_onorm_g', 'new_v_w_branch_a', 'new_v_w_branch_b', 'new_v_w_out', 'new_v_rel_bias', 'new_v_hgrn_lb', 'new_v_final_g']
TWIN_LEAF_KINDS = {'loss': 'loss', 'grad_x': 'grad_x', 'grad_w_ada': 'grad_w', 'grad_b_ada': 'grad_w', 'grad_norm_g': 'grad_w', 'grad_w_in': 'grad_w', 'grad_hgrn_onorm_g': 'grad_w', 'grad_w_branch_a': 'grad_w', 'grad_w_branch_b': 'grad_w', 'grad_w_out': 'grad_w', 'grad_rel_bias': 'grad_w', 'grad_hgrn_lb': 'grad_w', 'grad_final_g': 'grad_w', 'delta_w_ada': 'delta_w', 'delta_b_ada': 'delta_w', 'delta_norm_g': 'delta_w', 'delta_w_in': 'delta_w', 'delta_hgrn_onorm_g': 'delta_w', 'delta_w_branch_a': 'delta_w', 'delta_w_branch_b': 'delta_w', 'delta_w_out': 'delta_w', 'delta_rel_bias': 'delta_w', 'delta_hgrn_lb': 'delta_w', 'delta_final_g': 'delta_w', 'new_m_w_ada': 'new_m', 'new_m_b_ada': 'new_m', 'new_m_norm_g': 'new_m', 'new_m_w_in': 'new_m', 'new_m_hgrn_onorm_g': 'new_m', 'new_m_w_branch_a': 'new_m', 'new_m_w_branch_b': 'new_m', 'new_m_w_out': 'new_m', 'new_m_rel_bias': 'new_m', 'new_m_hgrn_lb': 'new_m', 'new_m_final_g': 'new_m', 'new_v_w_ada': 'new_v', 'new_v_b_ada': 'new_v', 'new_v_norm_g': 'new_v', 'new_v_w_in': 'new_v', 'new_v_hgrn_onorm_g': 'new_v', 'new_v_w_branch_a': 'new_v', 'new_v_w_branch_b': 'new_v', 'new_v_w_out': 'new_v', 'new_v_rel_bias': 'new_v', 'new_v_hgrn_lb': 'new_v', 'new_v_final_g': 'new_v'}


def _forward(args):
    return _fwd_reference(*[args[k] for k in FWD_PARAMS])


def _output_shape():
    def fwd():
        inp = _fwd_setup_inputs(0)
        return _fwd_reference(*[inp[k] for k in FWD_PARAMS])
    out = _jax.eval_shape(fwd)
    return out.shape, out.dtype

N_MICROBATCH = 1
ADAM_LR = 0.001
ADAM_B1 = 0.9
ADAM_B2 = 0.999
ADAM_EPS = 1e-08
ADAM_WD = 0.01
ADAM_STEP = 10
PER_EXAMPLE_BATCH_AXIS = {'x': 0, 'c': 0, 'loss_target': 0}
SHARED_INPUTS = []
_WEIGHT_DTYPES = {'w_ada': _jnp.float32, 'b_ada': _jnp.float32, 'norm_g': _jnp.float32, 'w_in': _jnp.float32, 'hgrn_onorm_g': _jnp.float32, 'w_branch_a': _jnp.float32, 'w_branch_b': _jnp.float32, 'w_out': _jnp.float32, 'rel_bias': _jnp.float32, 'hgrn_lb': _jnp.float32, 'final_g': _jnp.float32}
MOMENT_SCALE = {'w_ada': 1.107501e-01, 'b_ada': 2.252262e-01, 'norm_g': 1.009967e-01, 'w_in': 4.736295e-02, 'hgrn_onorm_g': 2.020526e-01, 'w_branch_a': 4.962047e-02, 'w_branch_b': 7.444263e-02, 'w_out': 8.851070e-02, 'rel_bias': 3.369410e-02, 'hgrn_lb': 3.793836e-03, 'final_g': 6.428724e+01}


def _to_microbatches(a, axis):
    t = _jnp.moveaxis(a, axis, 0)
    t = t.reshape((N_MICROBATCH, t.shape[0] // N_MICROBATCH) + t.shape[1:])
    return _jnp.moveaxis(t, 1, axis + 1)


def setup_inputs(seed: int = 0) -> dict:
    inp = _fwd_setup_inputs(seed)
    key = _jax.random.fold_in(_jax.random.key(seed), 7919)
    shape, _ = _output_shape()
    out = dict(inp)
    out["loss_target"] = _jax.random.normal(_jax.random.fold_in(key, 0), shape, _jnp.float32)
    for i, name in enumerate(TWIN_WEIGHTS):
        w = inp[name].astype(_jnp.float32)
        if MOMENT_SCALE is None:
            s = _jnp.sqrt(_jnp.mean(_jnp.square(w)) + 1e-30)
        else:
            s = MOMENT_SCALE[name]
        km, kv = _jax.random.split(_jax.random.fold_in(key, i + 1))
        out[name] = w
        out["m_" + name] = s * _jax.random.normal(km, w.shape, _jnp.float32)
        out["v_" + name] = (s * s) * _jax.random.uniform(kv, w.shape, _jnp.float32, 0.5, 1.5)
    if N_MICROBATCH > 1:
        for name, axis in PER_EXAMPLE_BATCH_AXIS.items():
            out[name] = _to_microbatches(out[name], axis)
    return {'x': out['x'], 'c': out['c'], 'w_ada': out['w_ada'], 'b_ada': out['b_ada'], 'norm_g': out['norm_g'], 'w_in': out['w_in'], 'hgrn_onorm_g': out['hgrn_onorm_g'], 'w_branch_a': out['w_branch_a'], 'w_branch_b': out['w_branch_b'], 'w_out': out['w_out'], 'rel_bias': out['rel_bias'], 'hgrn_lb': out['hgrn_lb'], 'final_g': out['final_g'], 'loss_target': out['loss_target'], 'm_w_ada': out['m_w_ada'], 'm_b_ada': out['m_b_ada'], 'm_norm_g': out['m_norm_g'], 'm_w_in': out['m_w_in'], 'm_hgrn_onorm_g': out['m_hgrn_onorm_g'], 'm_w_branch_a': out['m_w_branch_a'], 'm_w_branch_b': out['m_w_branch_b'], 'm_w_out': out['m_w_out'], 'm_rel_bias': out['m_rel_bias'], 'm_hgrn_lb': out['m_hgrn_lb'], 'm_final_g': out['m_final_g'], 'v_w_ada': out['v_w_ada'], 'v_b_ada': out['v_b_ada'], 'v_norm_g': out['v_norm_g'], 'v_w_in': out['v_w_in'], 'v_hgrn_onorm_g': out['v_hgrn_onorm_g'], 'v_w_branch_a': out['v_w_branch_a'], 'v_w_branch_b': out['v_w_branch_b'], 'v_w_out': out['v_w_out'], 'v_rel_bias': out['v_rel_bias'], 'v_hgrn_lb': out['v_hgrn_lb'], 'v_final_g': out['v_final_g']}


def _loss(weights, diff, rest, loss_target):
    with _jax.named_scope("forward"):
        args = {**rest, TWIN_DIFF_INPUT: diff, **{k: w.astype(_WEIGHT_DTYPES[k]) for k, w in weights.items()}}
        y = _forward(args)
    with _jax.named_scope("loss_head"):
        err = _jnp.square(y.astype(_jnp.float32) - loss_target)
        return 0.5 * _jnp.sum(_jnp.mean(err, axis=-1)) if err.ndim else 0.5 * err


def _adamw(w, g, m, v):
    m = ADAM_B1 * m + (1.0 - ADAM_B1) * g
    v = ADAM_B2 * v + (1.0 - ADAM_B2) * _jnp.square(g)
    m_hat = m / (1.0 - ADAM_B1 ** ADAM_STEP)
    v_hat = v / (1.0 - ADAM_B2 ** ADAM_STEP)
    delta = -ADAM_LR * (m_hat / (_jnp.sqrt(v_hat) + ADAM_EPS) + ADAM_WD * w)
    return delta, m, v


def reference(x, c, w_ada, b_ada, norm_g, w_in, hgrn_onorm_g, w_branch_a, w_branch_b, w_out, rel_bias, hgrn_lb, final_g, loss_target, m_w_ada, m_b_ada, m_norm_g, m_w_in, m_hgrn_onorm_g, m_w_branch_a, m_w_branch_b, m_w_out, m_rel_bias, m_hgrn_lb, m_final_g, v_w_ada, v_b_ada, v_norm_g, v_w_in, v_hgrn_onorm_g, v_w_branch_a, v_w_branch_b, v_w_out, v_rel_bias, v_hgrn_lb, v_final_g):
    given = dict(x=x, c=c, w_ada=w_ada, b_ada=b_ada, norm_g=norm_g, w_in=w_in, hgrn_onorm_g=hgrn_onorm_g, w_branch_a=w_branch_a, w_branch_b=w_branch_b, w_out=w_out, rel_bias=rel_bias, hgrn_lb=hgrn_lb, final_g=final_g, loss_target=loss_target, m_w_ada=m_w_ada, m_b_ada=m_b_ada, m_norm_g=m_norm_g, m_w_in=m_w_in, m_hgrn_onorm_g=m_hgrn_onorm_g, m_w_branch_a=m_w_branch_a, m_w_branch_b=m_w_branch_b, m_w_out=m_w_out, m_rel_bias=m_rel_bias, m_hgrn_lb=m_hgrn_lb, m_final_g=m_final_g, v_w_ada=v_w_ada, v_b_ada=v_b_ada, v_norm_g=v_norm_g, v_w_in=v_w_in, v_hgrn_onorm_g=v_hgrn_onorm_g, v_w_branch_a=v_w_branch_a, v_w_branch_b=v_w_branch_b, v_w_out=v_w_out, v_rel_bias=v_rel_bias, v_hgrn_lb=v_hgrn_lb, v_final_g=v_final_g)
    weights = {n: given[n] for n in TWIN_WEIGHTS}
    shared = {n: given[n] for n in SHARED_INPUTS}
    per_example = {n: given[n] for n in ['x', 'c']}
    grad_fn = _jax.value_and_grad(_loss, argnums=(0, 1))

    def one_microbatch(ex, loss_target):
        ex = dict(ex)
        diff = ex.pop(TWIN_DIFF_INPUT)
        return grad_fn(weights, diff, {**shared, **ex}, loss_target)

    if N_MICROBATCH == 1:
        loss, (grad_w, grad_x) = one_microbatch(per_example, given["loss_target"])
    else:
        def body(carry, xs):
            loss_sum, grad_sum = carry
            l_k, (gw_k, gx_k) = one_microbatch(xs[0], xs[1])
            with _jax.named_scope("update"):
                return (loss_sum + l_k, _jax.tree.map(_jnp.add, grad_sum, gw_k)), gx_k

        init = (_jnp.zeros((), _jnp.float32), _jax.tree.map(_jnp.zeros_like, weights))
        (loss, grad_w), grad_x = _jax.lax.scan(body, init, (per_example, given["loss_target"]))
    with _jax.named_scope("update"):
        delta_w, new_m, new_v = {}, {}, {}
        for n in TWIN_WEIGHTS:
            delta_w[n], new_m[n], new_v[n] = _adamw(weights[n], grad_w[n], given["m_" + n], given["v_" + n])
    return (loss, grad_x, *[grad_w[n] for n in TWIN_WEIGHTS], *[delta_w[n] for n in TWIN_WEIGHTS],
            *[new_m[n] for n in TWIN_WEIGHTS], *[new_v[n] for n in TWIN_WEIGHTS])
```

```python
import functools
import math

import numpy as np
import jax
import jax.numpy as jnp
from jax import lax
from jax.experimental import pallas as pl
from jax.experimental.pallas import tpu as pltpu

F32 = jnp.float32
BF16 = jnp.bfloat16
MXU_DTYPE = jnp.bfloat16

N_DEV = 8
D = 1024
A_HEADS = 8
A_HD = 64
A_W = A_HEADS * A_HD
A_BLK = 128
PATTERNS = ((128, 1), (512, 4), (2048, 16))
N_BUCKETS = 32
MAX_DISTANCE = 2048
NEG = -1e30
G_HEADS = 8
G_DK = 128
G_W = G_HEADS * G_DK
IN_W = 8 * D
EPS = 1e-6
ADAM_LR = 0.001
ADAM_B1 = 0.9
ADAM_B2 = 0.999
ADAM_EPS = 1e-08
ADAM_WD = 0.01
ADAM_STEP = 10

G_CHUNK = 128
G_SUB = 16
VMEM_LIMIT = 56 * 1024 * 1024

NN = (((1,), (0,)), ((), ()))
NT = (((1,), (1,)), ((), ()))
TN = (((0,), (0,)), ((), ()))
MESH = pl.DeviceIdType.MESH


def _mm(a, b, dims=NN):
    return lax.dot_general(a.astype(MXU_DTYPE), b.astype(MXU_DTYPE), dims,
                           preferred_element_type=F32)


def _mm_exact(a, b):
    return lax.dot_general(a, b, NN, precision=lax.Precision.HIGHEST,
                           preferred_element_type=F32)


def _sigmoid(x):
    return 1.0 / (1.0 + jnp.exp(-x))


def _params(sem=None):
    return pltpu.CompilerParams(dimension_semantics=sem, vmem_limit_bytes=VMEM_LIMIT)


def _all_gather(xs, name):
    n = len(xs)

    def body(*refs):
        ins, outs = refs[:n], refs[n:2 * n]
        send_sems, recv_sems, local_sems = refs[2 * n:]
        x, y, c = lax.axis_index("x"), lax.axis_index("y"), lax.axis_index("c")
        me, sibling = (x, y, c), (x, y, 1 - c)
        chips = [(1 - x, y), (x, 1 - y), (1 - x, 1 - y)]

        def slot(ref, dev):
            return ref.at[4 * dev[0] + 2 * dev[1] + dev[2]]

        def copy(a, k, block, to, src=None):
            return pltpu.make_async_remote_copy(
                src_ref=slot(outs[a], block) if src is None else src,
                dst_ref=slot(outs[a], block),
                send_sem=send_sems.at[a, k], recv_sem=recv_sems.at[a, k],
                device_id=to, device_id_type=MESH)

        mine, first, passed = [], [], []
        for a in range(n):
            cp = pltpu.make_async_copy(ins[a], slot(outs[a], me), local_sems.at[a])
            cp.start()
            mine.append(cp)
            first.append(copy(a, 0, me, sibling, src=ins[a]))
            for j, chip in enumerate(chips):
                first.append(copy(a, 1 + j, me, (*chip, c), src=ins[a]))
        for cp in first:
            cp.start()
        for j, chip in enumerate(chips):
            for a in range(n):
                copy(a, 1 + j, (*chip, c), me).wait_recv()
                cp = copy(a, 4 + j, (*chip, c), sibling)
                cp.start()
                passed.append(cp)
        for a in range(n):
            copy(a, 0, sibling, me).wait_recv()
            for j, chip in enumerate(chips):
                copy(a, 4 + j, (*chip, 1 - c), me).wait_recv()
        for cp in first + passed:
            cp.wait_send()
        for cp in mine:
            cp.wait()

    any_spec = pl.BlockSpec(memory_space=pl.ANY)
    return pl.pallas_call(
        body, name=name,
        out_shape=[jax.ShapeDtypeStruct((N_DEV,) + v.shape, v.dtype) for v in xs],
        in_specs=[any_spec] * n, out_specs=[any_spec] * n,
        scratch_shapes=[pltpu.SemaphoreType.DMA((n, 7)), pltpu.SemaphoreType.DMA((n, 7)),
                        pltpu.SemaphoreType.DMA((n,))],
    )(*xs)


def _all_to_all(xs, name):
    n = len(xs)

    def body(*refs):
        ins, outs = refs[:n], refs[n:2 * n]
        send_sems, recv_sems, local_sems = refs[2 * n:]
        x, y, c = lax.axis_index("x"), lax.axis_index("y"), lax.axis_index("c")
        me = 4 * x + 2 * y + c
        peers = []
        for m in range(1, N_DEV):
            px = 1 - x if m & 4 else x
            py = 1 - y if m & 2 else y
            pc_ = 1 - c if m & 1 else c
            peers.append((px, py, pc_))

        def copy(a, k):
            peer = peers[k]
            pid = 4 * peer[0] + 2 * peer[1] + peer[2]
            return pltpu.make_async_remote_copy(
                src_ref=ins[a].at[pid], dst_ref=outs[a].at[me],
                send_sem=send_sems.at[a, k], recv_sem=recv_sems.at[a, k],
                device_id=peer, device_id_type=MESH)

        def landed(a, k):
            peer = peers[k]
            pid = 4 * peer[0] + 2 * peer[1] + peer[2]
            return pltpu.make_async_remote_copy(
                src_ref=ins[a].at[pid], dst_ref=outs[a].at[pid],
                send_sem=send_sems.at[a, k], recv_sem=recv_sems.at[a, k],
                device_id=peer, device_id_type=MESH)

        mine, sent = [], []
        for a in range(n):
            cp = pltpu.make_async_copy(ins[a].at[me], outs[a].at[me], local_sems.at[a])
            cp.start()
            mine.append(cp)
        for k in range(N_DEV - 1):
            for a in range(n):
                cp = copy(a, k)
                cp.start()
                sent.append(cp)
        for k in range(N_DEV - 1):
            for a in range(n):
                landed(a, k).wait_recv()
        for cp in sent:
            cp.wait_send()
        for cp in mine:
            cp.wait()

    any_spec = pl.BlockSpec(memory_space=pl.ANY)
    return pl.pallas_call(
        body, name=name,
        out_shape=[jax.ShapeDtypeStruct(v.shape, v.dtype) for v in xs],
        in_specs=[any_spec] * n, out_specs=[any_spec] * n,
        scratch_shapes=[pltpu.SemaphoreType.DMA((n, 7)), pltpu.SemaphoreType.DMA((n, 7)),
                        pltpu.SemaphoreType.DMA((n,))],
    )(*xs)


def _mod_fwd(c64, w_ada, b_loc):
    def body(c_ref, w_ref, b_ref, o_ref):
        cv = c_ref[...]
        sc = cv * _sigmoid(cv)
        o_ref[...] = _mm(sc, w_ref[...]) + b_ref[...]

    return pl.pallas_call(
        body, name="mod_fwd",
        out_shape=jax.ShapeDtypeStruct((c64.shape[0], w_ada.shape[1]), F32),
        compiler_params=_params(),
    )(c64, w_ada, b_loc)


def _inproj(x, mod3, norm_g, w_in_g, tm=512):
    S = x.shape[0]

    def body(x_ref, mod_ref, g_ref, w_ref, proj_ref, h_ref, hs):
        @pl.when(pl.program_id(1) == 0)
        def _():
            xv = x_ref[...]
            r = lax.rsqrt(jnp.mean(xv * xv, axis=-1, keepdims=True) + EPS)
            h = (xv * r * g_ref[...]) * (1.0 + mod_ref[1:2, :]) + mod_ref[0:1, :]
            hs[...] = h.astype(MXU_DTYPE)
            h_ref[...] = h.astype(MXU_DTYPE)
        proj_ref[...] = _mm(hs[...], w_ref[0])

    return pl.pallas_call(
        body, name="inproj",
        grid=(S // tm, N_DEV),
        in_specs=[pl.BlockSpec((tm, D), lambda i, j: (i, 0)),
                  pl.BlockSpec((8, D), lambda i, j: (0, 0)),
                  pl.BlockSpec((1, D), lambda i, j: (0, 0)),
                  pl.BlockSpec((1, D, D), lambda i, j: (j, 0, 0))],
        out_specs=[pl.BlockSpec((tm, D), lambda i, j: (i, j)),
                   pl.BlockSpec((tm, D), lambda i, j: (i, 0))],
        out_shape=[jax.ShapeDtypeStruct((S, IN_W), F32), jax.ShapeDtypeStruct((S, D), MXU_DTYPE)],
        scratch_shapes=[pltpu.VMEM((tm, D), MXU_DTYPE)],
        compiler_params=_params(("parallel", "arbitrary")),
    )(x, mod3, norm_g, w_in_g)


def _bucket_tables():
    qi = np.arange(A_BLK)[:, None]
    kj = np.arange(2 * A_BLK)[None, :]
    delta = qi + A_BLK - kj
    out = []
    for window, dil in PATTERNS:
        span = window // dil
        band = (delta >= 0) & (delta <= span)
        dist = np.clip(delta, 0, None) * dil
        max_exact = N_BUCKETS // 2
        nf = dist.astype(np.float32)
        large = max_exact + (np.log(np.maximum(nf, np.float32(1.0)) / np.float32(max_exact))
                             / np.float32(math.log(MAX_DISTANCE / max_exact))
                             * np.float32(N_BUCKETS - max_exact)).astype(np.int32)
        large = np.minimum(large, N_BUCKETS - 1)
        bucket = np.where(dist < max_exact, dist, large)
        out.append(np.where(band, bucket, -1).astype(np.int32))
    return np.stack(out)


def _bias_tables(rel_bias, buckets):
    def body(rb_ref, bk_ref, o_ref):
        h = pl.program_id(1)
        bk = bk_ref[0]
        acc = jnp.full(bk.shape, NEG, F32)
        for b in range(N_BUCKETS):
            acc = jnp.where(bk == b, rb_ref[b, h], acc)
        o_ref[0, 0] = acc

    return pl.pallas_call(
        body, name="bias_tables",
        grid=(3, A_HEADS),
        in_specs=[pl.BlockSpec(memory_space=pltpu.SMEM),
                  pl.BlockSpec((1, A_BLK, 2 * A_BLK), lambda p, h: (p, 0, 0))],
        out_specs=pl.BlockSpec((1, 1, A_BLK, 2 * A_BLK), lambda p, h: (p, h, 0, 0)),
        out_shape=jax.ShapeDtypeStruct((3, A_HEADS, A_BLK, 2 * A_BLK), F32),
        compiler_params=_params(("arbitrary", "arbitrary")),
    )(rel_bias, buckets)


def _attn_in_specs(d, nb):
    cb = IN_W // A_W

    def cur(col):
        return pl.BlockSpec((A_BLK, A_W), lambda r, n: (jnp.minimum(n, nb - 1), r * cb + col))

    def prev(col):
        return pl.BlockSpec((A_BLK, A_W),
                            lambda r, n: (jnp.maximum(jnp.minimum(n, nb - 1) - 1, 0), r * cb + col))

    return [cur(0), prev(1), cur(1), prev(2), cur(2)]


def _attn_scores(q_ref, kp_ref, kc_ref, b_ref, h, first):
    sl = slice(A_HD * h, A_HD * (h + 1))
    k = jnp.concatenate([kp_ref[:, sl], kc_ref[:, sl]], axis=0)
    s = _mm(q_ref[:, sl], k, NT) * (A_HD ** -0.5) + b_ref[h] + first
    return s, k


def _attn_fwd(proj, bias_p, d, name):
    S = proj.shape[0]
    L = S // d
    nb = L // A_BLK
    pv = proj.reshape(L, d * IN_W)

    def body(q_ref, kp_ref, kc_ref, vp_ref, vc_ref, b_ref, o_ref, l_ref):
        n = pl.program_id(1)
        kj = lax.broadcasted_iota(jnp.int32, (A_BLK, 2 * A_BLK), 1)
        first = jnp.where((n == 0) & (kj < A_BLK), NEG, 0.0).astype(F32)
        for h in range(A_HEADS):
            sl = slice(A_HD * h, A_HD * (h + 1))
            s, _ = _attn_scores(q_ref, kp_ref, kc_ref, b_ref, h, first)
            v = jnp.concatenate([vp_ref[:, sl], vc_ref[:, sl]], axis=0)
            m = jnp.max(s, axis=-1, keepdims=True)
            p = jnp.exp(s - m)
            den = jnp.sum(p, axis=-1, keepdims=True)
            o_ref[:, sl] = _mm(p, v) / den
            l_ref[:, sl] = jnp.broadcast_to(m + jnp.log(den), (A_BLK, A_HD))

    o, l = pl.pallas_call(
        body, name=name,
        grid=(d, nb),
        in_specs=_attn_in_specs(d, nb) + [pl.BlockSpec((A_HEADS, A_BLK, 2 * A_BLK), lambda r, n: (0, 0, 0))],
        out_specs=[pl.BlockSpec((A_BLK, A_W), lambda r, n: (n, r))] * 2,
        out_shape=[jax.ShapeDtypeStruct((L, d * A_W), F32)] * 2,
        compiler_params=_params(("parallel", "parallel")),
    )(pv, pv, pv, pv, pv, bias_p)
    return o.reshape(S, A_W), l.reshape(S, A_W)


def _attn_combine(os, ls, proj, tm=512):
    S = proj.shape[0]

    def body(o1, o2, o3, l1, l2, l3, z_ref, ao_ref, lt_ref, oa_ref):
        a1, a2, a3 = l1[...], l2[...], l3[...]
        m = jnp.maximum(jnp.maximum(a1, a2), a3)
        e1, e2, e3 = jnp.exp(a1 - m), jnp.exp(a2 - m), jnp.exp(a3 - m)
        den = e1 + e2 + e3
        ao = (e1 * o1[...] + e2 * o2[...] + e3 * o3[...]) / den
        z = z_ref[...]
        ao_ref[...] = ao
        lt_ref[...] = m + jnp.log(den)
        oa_ref[...] = (ao * (z * _sigmoid(z))).astype(MXU_DTYPE)

    spec = pl.BlockSpec((tm, A_W), lambda i: (i, 0))
    return pl.pallas_call(
        body, name="attn_combine",
        grid=(S // tm,),
        in_specs=[spec] * 6 + [pl.BlockSpec((tm, A_W), lambda i: (i, 3))],
        out_specs=[spec] * 3,
        out_shape=[jax.ShapeDtypeStruct((S, A_W), F32), jax.ShapeDtypeStruct((S, A_W), F32),
                   jax.ShapeDtypeStruct((S, A_W), MXU_DTYPE)],
        compiler_params=_params(("parallel",)),
    )(*os, *ls, proj)


def _attn_pre_bwd(doa, ao, proj, tm=512):
    S = proj.shape[0]

    def body(doa_ref, ao_ref, z_ref, do_ref, dz_ref, dl_ref):
        z = z_ref[...]
        sg = _sigmoid(z)
        g = doa_ref[...]
        ao_v = ao_ref[...]
        do = g * (z * sg)
        do_ref[...] = do
        dz_ref[...] = (g * ao_v * (sg * (1.0 + z * (1.0 - sg)))).astype(MXU_DTYPE)
        prod = do * ao_v
        for h in range(A_HEADS):
            sl = slice(A_HD * h, A_HD * (h + 1))
            dl_ref[:, sl] = jnp.broadcast_to(jnp.sum(prod[:, sl], axis=-1, keepdims=True), (tm, A_HD))

    spec = pl.BlockSpec((tm, A_W), lambda i: (i, 0))
    return pl.pallas_call(
        body, name="attn_pre_bwd",
        grid=(S // tm,),
        in_specs=[spec, spec, pl.BlockSpec((tm, A_W), lambda i: (i, 3))],
        out_specs=[spec] * 3,
        out_shape=[jax.ShapeDtypeStruct((S, A_W), F32), jax.ShapeDtypeStruct((S, A_W), MXU_DTYPE),
                   jax.ShapeDtypeStruct((S, A_W), F32)],
        compiler_params=_params(("parallel",)),
    )(doa, ao, proj)


def _attn_bwd(proj, do, lt, delta, bias_p, d, name):
    S = proj.shape[0]
    L = S // d
    nb = L // A_BLK
    pv = proj.reshape(L, d * IN_W)
    view = lambda t: t.reshape(L, d * A_W)

    def body(q_ref, kp_ref, kc_ref, vp_ref, vc_ref, do_ref, lt_ref, dl_ref, b_ref,
             dq_ref, dk_ref, dv_ref, db_ref, ck, cv):
        r, n = pl.program_id(0), pl.program_id(1)

        @pl.when((r == 0) & (n == 0))
        def _():
            db_ref[...] = jnp.zeros_like(db_ref)

        @pl.when(n == 0)
        def _():
            ck[...] = jnp.zeros_like(ck)
            cv[...] = jnp.zeros_like(cv)

        @pl.when(n < nb)
        def _():
            kj = lax.broadcasted_iota(jnp.int32, (A_BLK, 2 * A_BLK), 1)
            first = jnp.where((n == 0) & (kj < A_BLK), NEG, 0.0).astype(F32)
            for h in range(A_HEADS):
                sl = slice(A_HD * h, A_HD * (h + 1))
                s, k = _attn_scores(q_ref, kp_ref, kc_ref, b_ref, h, first)
                v = jnp.concatenate([vp_ref[:, sl], vc_ref[:, sl]], axis=0)
                p = jnp.exp(s - lt_ref[:, A_HD * h:A_HD * h + 1])
                do_h = do_ref[:, sl]
                dp = _mm(do_h, v, NT)
                ds = p * (dp - dl_ref[:, A_HD * h:A_HD * h + 1])
                db_ref[h] += ds
                dq_ref[:, sl] = _mm(ds, k) * (A_HD ** -0.5)
                dk = _mm(ds, q_ref[:, sl], TN) * (A_HD ** -0.5)
                dv = _mm(p, do_h, TN)
                dk_ref[:, sl] = ck[:, sl] + dk[:A_BLK]
                dv_ref[:, sl] = cv[:, sl] + dv[:A_BLK]
                ck[:, sl] = dk[A_BLK:]
                cv[:, sl] = dv[A_BLK:]

        @pl.when(n == nb)
        def _():
            dk_ref[...] = ck[...]
            dv_ref[...] = cv[...]

    row = pl.BlockSpec((A_BLK, A_W), lambda r, n: (jnp.minimum(n, nb - 1), r))
    lag = pl.BlockSpec((A_BLK, A_W), lambda r, n: (jnp.maximum(n - 1, 0), r))
    dq, dk, dv, db = pl.pallas_call(
        body, name=name,
        grid=(d, nb + 1),
        in_specs=_attn_in_specs(d, nb) + [row, row, row,
                                          pl.BlockSpec((A_HEADS, A_BLK, 2 * A_BLK), lambda r, n: (0, 0, 0))],
        out_specs=[row, lag, lag, pl.BlockSpec((A_HEADS, A_BLK, 2 * A_BLK), lambda r, n: (0, 0, 0))],
        out_shape=[jax.ShapeDtypeStruct((L, d * A_W), F32)] * 3
                  + [jax.ShapeDtypeStruct((A_HEADS, A_BLK, 2 * A_BLK), F32)],
        scratch_shapes=[pltpu.VMEM((A_BLK, A_W), F32), pltpu.VMEM((A_BLK, A_W), F32)],
        compiler_params=_params(("arbitrary", "arbitrary")),
    )(pv, pv, pv, pv, pv, view(do), view(lt), view(delta), bias_p)
    return dq.reshape(S, A_W), dk.reshape(S, A_W), dv.reshape(S, A_W), db


def _attn_assemble(dqs, dks, dvs, dz, tm=512):
    S = dz.shape[0]

    def body(q1, q2, q3, k1, k2, k3, v1, v2, v3, z_ref, p0_ref, p1_ref):
        p0_ref[:, :A_W] = (q1[...] + q2[...] + q3[...]).astype(MXU_DTYPE)
        p0_ref[:, A_W:] = (k1[...] + k2[...] + k3[...]).astype(MXU_DTYPE)
        p1_ref[:, :A_W] = (v1[...] + v2[...] + v3[...]).astype(MXU_DTYPE)
        p1_ref[:, A_W:] = z_ref[...]

    spec = pl.BlockSpec((tm, A_W), lambda i: (i, 0))
    wide = pl.BlockSpec((tm, 2 * A_W), lambda i: (i, 0))
    return pl.pallas_call(
        body, name="attn_assemble",
        grid=(S // tm,),
        in_specs=[spec] * 10,
        out_specs=[wide, wide],
        out_shape=[jax.ShapeDtypeStruct((S, 2 * A_W), MXU_DTYPE)] * 2,
        compiler_params=_params(("parallel",)),
    )(*dqs, *dks, *dvs, dz)


def _rel_bias_grad(dbs, buckets):
    def body(d1, d2, d3, bk_ref, o_ref):
        row = lax.broadcasted_iota(jnp.int32, (A_HEADS, 128), 0)
        lane = lax.broadcasted_iota(jnp.int32, (A_HEADS, 128), 1)
        acc = jnp.zeros((A_HEADS, 128), F32)
        for p, dref in enumerate((d1, d2, d3)):
            bk = bk_ref[p]
            for h in range(A_HEADS):
                ds = dref[h]
                for b in range(N_BUCKETS):
                    s = jnp.sum(jnp.where(bk == b, ds, 0.0), keepdims=True)
                    acc = acc + jnp.where((row == h) & (lane == b), s, 0.0)
        o_ref[...] = acc

    return pl.pallas_call(
        body, name="rel_bias_grad",
        out_shape=jax.ShapeDtypeStruct((A_HEADS, 128), F32),
        compiler_params=_params(),
    )(*dbs, buckets)


def _tri(c):
    t = np.tril(np.ones((c, c), np.float32))
    return jnp.asarray(t), jnp.asarray(t.T.copy())


def _roll(x, shift):
    return x if shift == 0 else pltpu.roll(x, shift, 0)


def _hgrn_gates(q_ref, f_ref, lbp_ref, tri_ref):
    qraw = q_ref[...]
    sq = _sigmoid(qraw)
    q = qraw * sq
    sg = _sigmoid(f_ref[...])
    lb = _sigmoid(lbp_ref[0:1, :] - lbp_ref[1:2, :])
    f = lb + (1.0 - lb) * sg
    k = 1.0 - f
    b = _mm_exact(tri_ref[...], jnp.log(f))
    return qraw, sq, q, sg, lb, f, k, b


def _hgrn_col(C, base, idx):
    return pl.BlockSpec((C, G_DK), lambda h, n: (idx(n), base * G_HEADS + h))


def _hgrn_levels(C):
    out, m = [], G_SUB
    while 2 * m <= C:
        out.append(m)
        m *= 2
    return out


def _hgrn_level(b, C, m):
    zeros = jnp.zeros((m, G_DK), F32)
    refq, refk = [], []
    for blk in range(C // m):
        if blk % 2 == 1:
            refq.append(jnp.broadcast_to(b[blk * m:blk * m + 1], (m, G_DK)))
            refk.append(zeros)
        else:
            refq.append(zeros)
            refk.append(jnp.broadcast_to(b[(blk + 1) * m:(blk + 1) * m + 1], (m, G_DK)))
    refq = jnp.concatenate(refq, axis=0)
    refk = jnp.concatenate(refk, axis=0)
    right = (lax.broadcasted_iota(jnp.int32, (C, 1), 0) // m) % 2 == 1
    eq = jnp.exp(jnp.where(right, b - refq, NEG))
    ek = jnp.exp(jnp.where(right, NEG, refk - b))
    ti = lax.broadcasted_iota(jnp.int32, (C, C), 0)
    si = lax.broadcasted_iota(jnp.int32, (C, C), 1)
    return eq, ek, (ti // (2 * m) == si // (2 * m)) & (ti - si >= G_SUB)


def _hgrn_fwd(proj, hgrn_lb, onorm_g, C=G_CHUNK):
    S = proj.shape[0]
    nc = S // C
    tri, _ = _tri(C)

    def body(q_ref, f_ref, i_ref, z_ref, lbp_ref, go_ref, tri_ref, o_ref, ob_ref, st_ref, St):
        n = pl.program_id(1)

        @pl.when(n == 0)
        def _():
            St[...] = jnp.zeros_like(St)

        _, _, q, _, _, _, k, b = _hgrn_gates(q_ref, f_ref, lbp_ref, tri_ref)
        v = i_ref[...]
        bC = b[C - 1:C, :]
        S0 = St[...]
        o = _mm(q * jnp.exp(b), S0, NT)
        row = lax.broadcasted_iota(jnp.int32, (C, 1), 0)
        for l in range(G_SUB):
            e = jnp.exp(jnp.where(row >= l, b - _roll(b, l), NEG))
            a = jnp.sum(q * _roll(k, l) * e, axis=-1, keepdims=True)
            o = o + a * _roll(v, l)
        a_off = jnp.zeros((C, C), F32)
        for m in _hgrn_levels(C):
            eq, ek, pm = _hgrn_level(b, C, m)
            a_off = a_off + jnp.where(pm, _mm(q * eq, k * ek, NT), 0.0)
        o = o + _mm(a_off, v)
        S1 = S0 * jnp.exp(bC) + _mm(v, k * jnp.exp(bC - b), TN)
        St[...] = S1
        st_ref[0, 0] = S1
        o_ref[...] = o
        r = lax.rsqrt(jnp.mean(o * o, axis=-1, keepdims=True) + EPS)
        z = z_ref[...]
        ob_ref[...] = (o * r * go_ref[...] * (z * _sigmoid(z))).astype(MXU_DTYPE)

    ident = lambda n: n
    out = pl.BlockSpec((C, G_DK), lambda h, n: (n, h))
    return pl.pallas_call(
        body, name="hgrn_fwd",
        grid=(G_HEADS, nc),
        in_specs=[_hgrn_col(C, 2, ident), _hgrn_col(C, 3, ident), _hgrn_col(C, 4, ident), _hgrn_col(C, 5, ident),
                  pl.BlockSpec((2, G_DK), lambda h, n: (0, h)),
                  pl.BlockSpec((1, G_DK), lambda h, n: (0, 0)),
                  pl.BlockSpec((C, C), lambda h, n: (0, 0))],
        out_specs=[out, out, pl.BlockSpec((1, 1, G_DK, G_DK), lambda h, n: (n, h, 0, 0))],
        out_shape=[jax.ShapeDtypeStruct((S, G_W), F32), jax.ShapeDtypeStruct((S, G_W), MXU_DTYPE),
                   jax.ShapeDtypeStruct((nc, G_HEADS, G_DK, G_DK), F32)],
        scratch_shapes=[pltpu.VMEM((G_DK, G_DK), F32)],
        compiler_params=_params(("parallel", "arbitrary")),
    )(proj, proj, proj, proj, hgrn_lb, onorm_g, tri)


def _hgrn_bwd(proj, o_raw, dob, states, hgrn_lb, onorm_g, C=G_CHUNK):
    S = proj.shape[0]
    nc = S // C
    tri, triu = _tri(C)

    def body(q_ref, f_ref, i_ref, z_ref, o_ref, dob_ref, s0_ref, s1_ref, lbp_ref, go_ref, tri_ref, triu_ref,
             dq_ref, df_ref, di_ref, dz_ref, dlb_ref, dgo_ref, dSt):
        n = pl.program_id(1)
        cn = nc - 1 - n

        @pl.when(n == 0)
        def _():
            dSt[...] = jnp.zeros_like(dSt)
            dlb_ref[...] = jnp.zeros_like(dlb_ref)
            dgo_ref[...] = jnp.zeros_like(dgo_ref)

        qraw, sq, q, sg, lb, f, k, b = _hgrn_gates(q_ref, f_ref, lbp_ref, tri_ref)
        v = i_ref[...]
        bC = b[C - 1:C, :]
        eb = jnp.exp(b)
        ecb = jnp.exp(bC - b)
        o = o_ref[...]
        z = z_ref[...]
        sz = _sigmoid(z)
        go = go_ref[...]
        g_ob = dob_ref[...]
        r = lax.rsqrt(jnp.mean(o * o, axis=-1, keepdims=True) + EPS)
        nh = o * r
        dnrm = g_ob * (z * sz)
        dz_ref[...] = (g_ob * (nh * go) * (sz * (1.0 + z * (1.0 - sz)))).astype(MXU_DTYPE)
        dgo_ref[0:1, :] += jnp.sum(dnrm * nh, axis=0, keepdims=True)
        dn = dnrm * go
        do = r * (dn - nh * jnp.mean(dn * nh, axis=-1, keepdims=True))

        S0 = jnp.where(cn == 0, 0.0, s0_ref[0, 0])
        S1 = s1_ref[0, 0]
        dS1 = dSt[...]
        dq = eb * _mm(do, S0)
        dk = ecb * _mm(v, dS1)
        dv = _mm(k * ecb, dS1, NT)
        bnd = jnp.sum(dS1 * S1, axis=0, keepdims=True)
        dSt[...] = dS1 * jnp.exp(bC) + _mm(do, q * eb, TN)

        row = lax.broadcasted_iota(jnp.int32, (C, 1), 0)
        for l in range(G_SUB):
            e = jnp.exp(jnp.where(row >= l, b - _roll(b, l), NEG))
            da =jnp.sum(do * _roll(v, l), axis=-1, keepdims=True)
            dq = dq + da * (_roll(k, l) * e)
            up = (C - l) % C
            e2 = jnp.exp(jnp.where(row + l < C, _roll(b, up) - b, NEG))
            qe = _roll(q, up) * e2
            dou = _roll(do, up)
            da2 = jnp.sum(dou * v, axis=-1, keepdims=True)
            a2 = jnp.sum(qe * k, axis=-1, keepdims=True)
            dk = dk + da2 * qe
            dv = dv + a2 * dou

        da_all = _mm(do, v, NT)
        a_off = jnp.zeros((C, C), F32)
        for m in _hgrn_levels(C):
            eq, ek, pm = _hgrn_level(b, C, m)
            qt, kt = q * eq, k * ek
            da_m = jnp.where(pm, da_all, 0.0)
            a_off = a_off + jnp.where(pm, _mm(qt, kt, NT), 0.0)
            dq = dq + _mm(da_m, kt) * eq
            dk = dk + _mm(da_m, qt, TN) * ek
        dv = dv + _mm(a_off, do, TN)

        row = lax.broadcasted_iota(jnp.int32, (C, 1), 0)
        db = q * dq - k * dk + jnp.where(row == C - 1, bnd, 0.0)
        dg = _mm_exact(triu_ref[...], db)
        df = dg / f - dk
        df_ref[...] = (df * (1.0 - lb) * (sg * (1.0 - sg))).astype(MXU_DTYPE)
        dlb_ref[0:1, :] += jnp.sum(df * (1.0 - sg), axis=0, keepdims=True)
        dq_ref[...] = (dq * (sq * (1.0 + qraw * (1.0 - sq)))).astype(MXU_DTYPE)
        di_ref[...] = dv.astype(MXU_DTYPE)

    rev = lambda n: nc - 1 - n
    blk = pl.BlockSpec((C, G_DK), lambda h, n: (nc - 1 - n, h))
    return pl.pallas_call(
        body, name="hgrn_bwd",
        grid=(G_HEADS, nc),
        in_specs=[_hgrn_col(C, 2, rev), _hgrn_col(C, 3, rev), _hgrn_col(C, 4, rev), _hgrn_col(C, 5, rev), blk, blk,
                  pl.BlockSpec((1, 1, G_DK, G_DK), lambda h, n: (jnp.maximum(nc - 2 - n, 0), h, 0, 0)),
                  pl.BlockSpec((1, 1, G_DK, G_DK), lambda h, n: (nc - 1 - n, h, 0, 0)),
                  pl.BlockSpec((2, G_DK), lambda h, n: (0, h)),
                  pl.BlockSpec((1, G_DK), lambda h, n: (0, 0)),
                  pl.BlockSpec((C, C), lambda h, n: (0, 0)),
                  pl.BlockSpec((C, C), lambda h, n: (0, 0))],
        out_specs=[blk, blk, blk, blk,
                   pl.BlockSpec((8, G_DK), lambda h, n: (0, h)),
                   pl.BlockSpec((8, G_DK), lambda h, n: (h, 0))],
        out_shape=[jax.ShapeDtypeStruct((S, G_W), MXU_DTYPE)] * 4
                  + [jax.ShapeDtypeStruct((8, G_W), F32), jax.ShapeDtypeStruct((8 * G_HEADS, G_DK), F32)],
        scratch_shapes=[pltpu.VMEM((G_DK, G_DK), F32)],
        compiler_params=_params(("parallel", "arbitrary")),
    )(proj, proj, proj, proj, o_raw, dob, states, states, hgrn_lb, onorm_g, tri, triu)


def _tail(x, target, oa, ob, proj, mod3, final_g, wa, wb, wo, tm=256):
    S = x.shape[0]
    nt = S // tm

    def body(x_ref, t_ref, oa_ref, ob_ref, ga_ref, gb_ref, mod_ref, fg_ref, wa_ref, wb_ref, wo_ref,
             dx2_ref, doa_ref, dob_ref, dga_ref, dgb_ref, sums_ref, gwa_ref, gwb_ref, gwo_ref,
             acc_a, acc_b, acc_o):
        i = pl.program_id(0)

        @pl.when(i == 0)
        def _():
            sums_ref[...] = jnp.zeros_like(sums_ref)
            acc_a[...] = jnp.zeros_like(acc_a)
            acc_b[...] = jnp.zeros_like(acc_b)
            acc_o[...] = jnp.zeros_like(acc_o)

        oa_v, ob_v = oa_ref[...], ob_ref[...]
        pa = _mm(oa_v, wa_ref[...])
        pb = _mm(ob_v, wb_ref[...])
        sa, sb = _sigmoid(ga_ref[...]), _sigmoid(gb_ref[...])
        ym = sa * pa + sb * pb
        u = _mm(ym, wo_ref[...])
        gate = mod_ref[2:3, :]
        fg = fg_ref[...]
        x2 = x_ref[...] + gate * u
        r2 = lax.rsqrt(jnp.mean(x2 * x2, axis=-1, keepdims=True) + EPS)
        xn2 = x2 * r2
        e = xn2 * fg - t_ref[...]
        dy = e * (1.0 / D)
        dn = dy * fg
        dx2 = r2 * (dn - xn2 * jnp.mean(dn * xn2, axis=-1, keepdims=True))
        dx2_ref[...] = dx2
        sums_ref[0:1, :] += jnp.sum(dy * xn2, axis=0, keepdims=True)
        sums_ref[1:2, :] += jnp.sum(dx2 * u, axis=0, keepdims=True)
        sums_ref[2:3, :] += (0.5 / D) * jnp.sum(e * e, axis=0, keepdims=True)
        du = dx2 * gate
        dym = _mm(du, wo_ref[...], NT)
        acc_o[...] += _mm(ym, du, TN)
        dpa, dpb = dym * sa, dym * sb
        dga_ref[...] = (dym * pa * (sa * (1.0 - sa))).astype(MXU_DTYPE)
        dgb_ref[...] = (dym * pb * (sb * (1.0 - sb))).astype(MXU_DTYPE)
        doa_ref[...] = _mm(dpa, wa_ref[...], NT)
        dob_ref[...] = _mm(dpb, wb_ref[...], NT)
        acc_a[...] += _mm(oa_v, dpa, TN)
        acc_b[...] += _mm(ob_v, dpb, TN)

        @pl.when(i == nt - 1)
        def _():
            pltpu.sync_copy(acc_a, gwa_ref)
            pltpu.sync_copy(acc_b, gwb_ref)
            pltpu.sync_copy(acc_o, gwo_ref)

    row = lambda w: pl.BlockSpec((tm, w), lambda i: (i, 0))
    full = lambda a, b: pl.BlockSpec((a, b), lambda i: (0, 0))
    any_spec = pl.BlockSpec(memory_space=pl.ANY)
    return pl.pallas_call(
        body, name="tail",
        grid=(nt,),
        in_specs=[row(D), row(D), row(A_W), row(D),
                  pl.BlockSpec((tm, D), lambda i: (i, 6)), pl.BlockSpec((tm, D), lambda i: (i, 7)),
                  full(8, D), full(1, D), full(A_W, D), full(D, D), full(D, D)],
        out_specs=[row(D), row(A_W), row(D), row(D), row(D), full(8, D), any_spec, any_spec, any_spec],
        out_shape=[jax.ShapeDtypeStruct((S, D), F32), jax.ShapeDtypeStruct((S, A_W), F32),
                   jax.ShapeDtypeStruct((S, D), F32), jax.ShapeDtypeStruct((S, D), MXU_DTYPE),
                   jax.ShapeDtypeStruct((S, D), MXU_DTYPE), jax.ShapeDtypeStruct((8, D), F32),
                   jax.ShapeDtypeStruct((A_W, D), F32), jax.ShapeDtypeStruct((D, D), F32),
                   jax.ShapeDtypeStruct((D, D), F32)],
        scratch_shapes=[pltpu.VMEM((A_W, D), F32), pltpu.VMEM((D, D), F32), pltpu.VMEM((D, D), F32)],
        compiler_params=_params(("arbitrary",)),
    )(x, target, oa, ob, proj, proj, mod3, final_g, wa, wb, wo)


def _dh(dproj, w_in_g, x, dx2, mod3, norm_g, tm=512):
    S = x.shape[0]

    def body(p_ref, w_ref, x_ref, dx2_ref, mod_ref, g_ref, gx_ref, sums_ref, acc):
        i, j = pl.program_id(0), pl.program_id(1)

        @pl.when((i == 0) & (j == 0))
        def _():
            sums_ref[...] = jnp.zeros_like(sums_ref)

        @pl.when(j == 0)
        def _():
            acc[...] = jnp.zeros_like(acc)

        acc[...] += _mm(p_ref[0], w_ref[0], NT)

        @pl.when(j == N_DEV - 1)
        def _():
            dh = acc[...]
            xv = x_ref[...]
            g = g_ref[...]
            sc1 = 1.0 + mod_ref[1:2, :]
            r = lax.rsqrt(jnp.mean(xv * xv, axis=-1, keepdims=True) + EPS)
            xn = xv * r
            sums_ref[0:1, :] += jnp.sum(dh, axis=0, keepdims=True)
            sums_ref[1:2, :] += jnp.sum(dh * (xn * g), axis=0, keepdims=True)
            sums_ref[2:3, :] += jnp.sum(dh * sc1 * xn, axis=0, keepdims=True)
            dxn = dh * sc1 * g
            gx_ref[...] = dx2_ref[...] + r * (dxn - xn * jnp.mean(dxn * xn, axis=-1, keepdims=True))

    row = pl.BlockSpec((tm, D), lambda i, j: (i, 0))
    return pl.pallas_call(
        body, name="dh",
        grid=(S // tm, N_DEV),
        in_specs=[pl.BlockSpec((1, tm, D), lambda i, j: (j, i, 0)),
                  pl.BlockSpec((1, D, D), lambda i, j: (j, 0, 0)),
                  row, row,
                  pl.BlockSpec((8, D), lambda i, j: (0, 0)),
                  pl.BlockSpec((1, D), lambda i, j: (0, 0))],
        out_specs=[row, pl.BlockSpec((8, D), lambda i, j: (0, 0))],
        out_shape=[jax.ShapeDtypeStruct((S, D), F32), jax.ShapeDtypeStruct((8, D), F32)],
        scratch_shapes=[pltpu.VMEM((tm, D), F32)],
        compiler_params=_params(("arbitrary", "arbitrary")),
    )(dproj, w_in_g, x, dx2, mod3, norm_g)


def _gw_in(h, dproj, tm=512):
    S = h.shape[0]

    def body(h_ref, p_ref, o_ref):
        @pl.when(pl.program_id(1) == 0)
        def _():
            o_ref[...] = jnp.zeros_like(o_ref)
        o_ref[0] += _mm(h_ref[...], p_ref[0], TN)

    return pl.pallas_call(
        body, name="gw_in",
        grid=(N_DEV, S // tm),
        in_specs=[pl.BlockSpec((tm, D), lambda j, i: (i, 0)),
                  pl.BlockSpec((1, tm, D), lambda j, i: (j, i, 0))],
        out_specs=pl.BlockSpec((1, D, D), lambda j, i: (j, 0, 0)),
        out_shape=jax.ShapeDtypeStruct((N_DEV, D, D), F32),
        compiler_params=_params(("parallel", "arbitrary")),
    )(h, dproj)


def _adamw_math(w, g, m, v):
    m = ADAM_B1 * m + (1.0 - ADAM_B1) * g
    v = ADAM_B2 * v + (1.0 - ADAM_B2) * (g * g)
    m_hat = m / (1.0 - ADAM_B1 ** ADAM_STEP)
    v_hat = v / (1.0 - ADAM_B2 ** ADAM_STEP)
    delta = -ADAM_LR * (m_hat / (jnp.sqrt(v_hat) + ADAM_EPS) + ADAM_WD * w)
    return delta, m, v


def _adamw_big(recv, w, m, v, name, tr=128):
    M, N = w.shape
    tr = min(tr, M)

    def body(r_ref, w_ref, m_ref, v_ref, g_ref, d_ref, nm_ref, nv_ref):
        g = r_ref[0]
        for j in range(1, N_DEV):
            g = g + r_ref[j]
        g_ref[...] = g
        d_ref[...], nm_ref[...], nv_ref[...] = _adamw_math(w_ref[...], g, m_ref[...], v_ref[...])

    blk = pl.BlockSpec((tr, N), lambda i: (i, 0))
    return pl.pallas_call(
        body, name=name,
        grid=(M // tr,),
        in_specs=[pl.BlockSpec((N_DEV, tr, N), lambda i: (0, i, 0)), blk, blk, blk],
        out_specs=[blk] * 4,
        out_shape=[jax.ShapeDtypeStruct((M, N), F32)] * 4,
        compiler_params=_params(("parallel",)),
    )(recv, w, m, v)


def _adamw_w_ada(c64, dmod64, w, m, v):
    def body(c_ref, dm_ref, w_ref, m_ref, v_ref, g_ref, d_ref, nm_ref, nv_ref):
        cv = c_ref[...]
        g = _mm(cv * _sigmoid(cv), dm_ref[...], TN)
        g_ref[...] = g
        d_ref[...], nm_ref[...], nv_ref[...] = _adamw_math(w_ref[...], g, m_ref[...], v_ref[...])

    return pl.pallas_call(
        body, name="adamw_w_ada",
        out_shape=[jax.ShapeDtypeStruct(w.shape, F32)] * 4,
        compiler_params=_params(),
    )(c64, dmod64, w, m, v)


P_MOD, P_NORM, P_ONORM, P_RELB, P_LB, P_FINAL, P_LOSS, P_END = (0, 3 * D, 4 * D, 5 * D, 6 * D, 7 * D, 8 * D, 9 * D)


def _adamw_small(packed, b_ada, norm_g, onorm_g, relb, hgrn_lb, final_g, ms, vs):
    def body(pk_ref, b_ref, ng_ref, og_ref, rb_ref, lb_ref, fg_ref,
             mb, mn, mo, mr, ml, mf, vb, vn, vo, vr, vl, vf,
             loss_ref, gb, gn, go, gr, gl, gf, db, dn, do, dr, dl, df,
             nmb, nmn, nmo, nmr, nml, nmf, nvb, nvn, nvo, nvr, nvl, nvf):
        tot = pk_ref[0:1, :]
        for j in range(1, N_DEV):
            tot = tot + pk_ref[8 * j:8 * j + 1, :]
        loss_ref[...] = jnp.broadcast_to(jnp.sum(tot[:, P_LOSS:P_END], axis=-1, keepdims=True), (8, 128))

        def upd(g, w_ref, m_ref, v_ref, g_out, d_out, m_out, v_out):
            g_out[...] = g
            d_out[...], m_out[...], v_out[...] = _adamw_math(w_ref[...], g, m_ref[...], v_ref[...])

        upd(tot[:, P_MOD:P_NORM], b_ref, mb, vb, gb, db, nmb, nvb)
        upd(tot[:, P_NORM:P_ONORM], ng_ref, mn, vn, gn, dn, nmn, nvn)
        g_on = tot[:, P_ONORM:P_ONORM + G_DK]
        for h in range(1, G_HEADS):
            g_on = g_on + tot[:, P_ONORM + G_DK * h:P_ONORM + G_DK * (h + 1)]
        upd(g_on, og_ref, mo, vo, go, do, nmo, nvo)
        upd(tot[:, P_RELB:P_LB], rb_ref, mr, vr, gr, dr, nmr, nvr)
        a = lb_ref[...]
        lb = _sigmoid(a[0:1, :] - a[1:2, :])
        g0 = tot[:, P_LB:P_FINAL] * lb * (1.0 - lb)
        row = lax.broadcasted_iota(jnp.int32, (2, D), 0)
        upd(jnp.where(row == 0, g0, -g0), lb_ref, ml, vl, gl, dl, nml, nvl)
        upd(tot[:, P_FINAL:P_LOSS], fg_ref, mf, vf, gf, df, nmf, nvf)

    shapes = [b_ada.shape, norm_g.shape, onorm_g.shape, relb.shape, hgrn_lb.shape, final_g.shape]
    outs = [jax.ShapeDtypeStruct((8, 128), F32)] + [jax.ShapeDtypeStruct(s, F32) for s in shapes] * 4
    return pl.pallas_call(
        body, name="adamw_small",
        out_shape=outs,
        compiler_params=_params(),
    )(packed, b_ada, norm_g, onorm_g, relb, hgrn_lb, final_g, *ms, *vs)


def _local_step(x, target, mod3, norm_g, w_in_g, onorm_g, wa, wb, wo, rel_bias, hgrn_lb, final_g):
    buckets = jnp.asarray(_bucket_tables())
    bias = _bias_tables(rel_bias, buckets)
    proj, h = _inproj(x, mod3, norm_g, w_in_g)
    os, ls = [], []
    for p, (_, d) in enumerate(PATTERNS):
        o, l = _attn_fwd(proj, bias[p], d, "attn_fwd_d%d" % d)
        os.append(o)
        ls.append(l)
    ao, lt, oa = _attn_combine(os, ls, proj)
    o_raw, ob, states = _hgrn_fwd(proj, hgrn_lb, onorm_g)
    dx2, doa, dob, dga, dgb, tsums, gwa, gwb, gwo = _tail(x, target, oa, ob, proj, mod3, final_g, wa, wb, wo)
    do, dza, delta = _attn_pre_bwd(doa, ao, proj)
    dqs, dks, dvs, dbs = [], [], [], []
    for p, (_, d) in enumerate(PATTERNS):
        dq, dk, dv, db = _attn_bwd(proj, do, lt, delta, bias[p], d, "attn_bwd_d%d" % d)
        dqs.append(dq)
        dks.append(dk)
        dvs.append(dv)
        dbs.append(db)
    p0, p1 = _attn_assemble(dqs, dks, dvs, dza)
    g_relb = _rel_bias_grad(dbs, buckets)
    dqb, dfb, dib, dzb, dlb, dgo = _hgrn_bwd(proj, o_raw, dob, states, hgrn_lb, onorm_g)
    dproj = jnp.stack([p0, p1, dqb, dfb, dib, dzb, dga, dgb])
    gx, hsums = _dh(dproj, w_in_g, x, dx2, mod3, norm_g)
    gw_in = _gw_in(h, dproj)
    row = jnp.concatenate([
        hsums[0], hsums[1], tsums[1],
        hsums[2],
        dgo.reshape(G_HEADS, 8, G_DK)[:, 0].reshape(-1),
        g_relb.reshape(-1),
        dlb[0],
        tsums[0],
        tsums[2],
    ])
    return gx, gw_in, gwa, gwb, gwo, row


def kernel(x, c, w_ada, b_ada, norm_g, w_in, hgrn_onorm_g, w_branch_a, w_branch_b, w_out, rel_bias, hgrn_lb, final_g, loss_target, m_w_ada, m_b_ada, m_norm_g, m_w_in, m_hgrn_onorm_g, m_w_branch_a, m_w_branch_b, m_w_out, m_rel_bias, m_hgrn_lb, m_final_g, v_w_ada, v_b_ada, v_norm_g, v_w_in, v_hgrn_onorm_g, v_w_branch_a, v_w_branch_b, v_w_out, v_rel_bias, v_hgrn_lb, v_final_g):
    me = 4 * lax.axis_index("x") + 2 * lax.axis_index("y") + lax.axis_index("c")
    n_ada = w_ada.shape[2]

    w_in_g, wa_g, wb_g, wo_g = _all_gather(
        [w_in[0].astype(MXU_DTYPE), w_branch_a[0].astype(MXU_DTYPE),
         w_branch_b[0].astype(MXU_DTYPE), w_out[0].astype(MXU_DTYPE)], "gather_weights")
    wa = wa_g.transpose(1, 0, 2).reshape(A_W, D)
    wb = wb_g.reshape(D, D)
    wo = wo_g.reshape(D, D)

    (c_all,) = _all_gather([jnp.broadcast_to(c, (8, D))], "gather_c")
    c64 = c_all.reshape(8 * N_DEV, D)
    b_loc = lax.dynamic_slice(b_ada, (0, me * n_ada), (1, n_ada))
    mod_part = _mod_fwd(c64, w_ada[0], b_loc)[::8]
    (mod_all,) = _all_gather([mod_part], "gather_mod")
    mod = lax.dynamic_slice(mod_all, (0, me, 0), (N_DEV, 1, n_ada)).reshape(3, D)
    mod3 = jnp.concatenate([mod, jnp.zeros((5, D), F32)], axis=0)

    onorm_t = hgrn_onorm_g
    gx, gw_in, gwa, gwb, gwo, row = _local_step(
        x[0], loss_target[0], mod3, norm_g, w_in_g, onorm_t, wa, wb, wo, rel_bias, hgrn_lb,
        final_g.reshape(1, D))

    r_in, r_a, r_b, r_o = _all_to_all(
        [gw_in, gwa.reshape(A_W, N_DEV, D // N_DEV).transpose(1, 0, 2),
         gwb.reshape(N_DEV, D // N_DEV, D), gwo.reshape(N_DEV, D // N_DEV, D)], "scatter_grads")
    packed8 = jnp.concatenate([row[None, :], jnp.zeros((7, P_END), F32)], axis=0)
    (packed,) = _all_gather([packed8], "gather_small")
    packed = packed.reshape(8 * N_DEV, P_END)

    g_in, d_in, nm_in, nv_in = _adamw_big(r_in, w_in[0], m_w_in[0], v_w_in[0], "adamw_w_in")
    g_a, d_a, nm_a, nv_a = _adamw_big(r_a, w_branch_a[0], m_w_branch_a[0], v_w_branch_a[0], "adamw_w_branch_a")
    g_b, d_b, nm_b, nv_b = _adamw_big(r_b, w_branch_b[0], m_w_branch_b[0], v_w_branch_b[0], "adamw_w_branch_b")
    g_o, d_o, nm_o, nv_o = _adamw_big(r_o, w_out[0], m_w_out[0], v_w_out[0], "adamw_w_out")

    dmod64 = lax.dynamic_slice(packed, (0, P_MOD + me * n_ada), (8 * N_DEV, n_ada))
    g_ada, d_ada, nm_ada, nv_ada = _adamw_w_ada(c64, dmod64, w_ada[0], m_w_ada[0], v_w_ada[0])

    def flat_relb(t):
        return jnp.pad(t.T, ((0, 0), (0, 128 - N_BUCKETS))).reshape(1, A_HEADS * 128)

    def unflat_relb(t):
        return t.reshape(A_HEADS, 128)[:, :N_BUCKETS].T

    fg2 = lambda t: t.reshape(1, D)
    smalls = _adamw_small(
        packed, b_ada, norm_g, hgrn_onorm_g, flat_relb(rel_bias), hgrn_lb, fg2(final_g),
        [m_b_ada, m_norm_g, m_hgrn_onorm_g, flat_relb(m_rel_bias), m_hgrn_lb, fg2(m_final_g)],
        [v_b_ada, v_norm_g, v_hgrn_onorm_g, flat_relb(v_rel_bias), v_hgrn_lb, fg2(v_final_g)])
    loss = smalls[0][0, 0]

    def small(kind):
        s = smalls[1 + 6 * kind:7 + 6 * kind]
        return s[0], s[1], s[2], unflat_relb(s[3]), s[4], s[5].reshape(D)

    def leaves(ada, sm, w_in_, wa_, wb_, wo_):
        b_, n_, o_, r_, l_, f_ = sm
        return (ada[None], b_, n_, w_in_[None], o_, wa_[None], wb_[None], wo_[None], r_, l_, f_)

    return (loss, gx[None],
            *leaves(g_ada, small(0), g_in, g_a, g_b, g_o),
            *leaves(d_ada, small(1), d_in, d_a, d_b, d_o),
            *leaves(nm_ada, small(2), nm_in, nm_a, nm_b, nm_o),
            *leaves(nv_ada, small(3), nv_in, nv_a, nv_b, nv_o))
```

```python
import functools
import math

import numpy as np
import jax
import jax.numpy as jnp
from jax import lax
from jax.experimental import pallas as pl
from jax.experimental.pallas import tpu as pltpu

F32 = jnp.float32
BF16 = jnp.bfloat16
MXU_DTYPE = jnp.bfloat16
XCHG_DTYPE = jnp.bfloat16

N_DEV = 8
D = 1024
A_HEADS = 8
A_HD = 64
A_W = A_HEADS * A_HD
A_BLK = 128
PATTERNS = ((128, 1), (512, 4), (2048, 16))
N_BUCKETS = 32
MAX_DISTANCE = 2048
NEG = -1e30
G_HEADS = 8
G_DK = 128
G_W = G_HEADS * G_DK
IN_W = 8 * D
EPS = 1e-6
ADAM_LR = 0.001
ADAM_B1 = 0.9
ADAM_B2 = 0.999
ADAM_EPS = 1e-08
ADAM_WD = 0.01
ADAM_STEP = 10

G_CHUNK = 128
G_SUB = 16
VMEM_LIMIT = 56 * 1024 * 1024

NN = (((1,), (0,)), ((), ()))
NT = (((1,), (1,)), ((), ()))
TN = (((0,), (0,)), ((), ()))
MESH = pl.DeviceIdType.MESH


def _mm(a, b, dims=NN):
    return lax.dot_general(a.astype(MXU_DTYPE), b.astype(MXU_DTYPE), dims,
                           preferred_element_type=F32)


def _mm_exact(a, b):
    return lax.dot_general(a, b, NN, precision=lax.Precision.HIGHEST,
                           preferred_element_type=F32)


def _sigmoid(x):
    return 1.0 / (1.0 + jnp.exp(-x))


def _params(sem=None):
    return pltpu.CompilerParams(dimension_semantics=sem, vmem_limit_bytes=VMEM_LIMIT)


def _all_gather(xs, name):
    n = len(xs)

    def body(*refs):
        ins, outs = refs[:n], refs[n:2 * n]
        send_sems, recv_sems, local_sems = refs[2 * n:]
        x, y, c = lax.axis_index("x"), lax.axis_index("y"), lax.axis_index("c")
        me, sibling = (x, y, c), (x, y, 1 - c)
        chips = [(1 - x, y), (x, 1 - y), (1 - x, 1 - y)]

        def slot(ref, dev):
            return ref.at[4 * dev[0] + 2 * dev[1] + dev[2]]

        def copy(a, k, block, to, src=None):
            return pltpu.make_async_remote_copy(
                src_ref=slot(outs[a], block) if src is None else src,
                dst_ref=slot(outs[a], block),
                send_sem=send_sems.at[a, k], recv_sem=recv_sems.at[a, k],
                device_id=to, device_id_type=MESH)

        mine, first, passed = [], [], []
        for a in range(n):
            cp = pltpu.make_async_copy(ins[a], slot(outs[a], me), local_sems.at[a])
            cp.start()
            mine.append(cp)
            first.append(copy(a, 0, me, sibling, src=ins[a]))
            for j, chip in enumerate(chips):
                first.append(copy(a, 1 + j, me, (*chip, c), src=ins[a]))
        for cp in first:
            cp.start()
        for j, chip in enumerate(chips):
            for a in range(n):
                copy(a, 1 + j, (*chip, c), me).wait_recv()
                cp = copy(a, 4 + j, (*chip, c), sibling)
                cp.start()
                passed.append(cp)
        for a in range(n):
            copy(a, 0, sibling, me).wait_recv()
            for j, chip in enumerate(chips):
                copy(a, 4 + j, (*chip, 1 - c), me).wait_recv()
        for cp in first + passed:
            cp.wait_send()
        for cp in mine:
            cp.wait()

    any_spec = pl.BlockSpec(memory_space=pl.ANY)
    return pl.pallas_call(
        body, name=name,
        out_shape=[jax.ShapeDtypeStruct((N_DEV,) + v.shape, v.dtype) for v in xs],
        in_specs=[any_spec] * n, out_specs=[any_spec] * n,
        scratch_shapes=[pltpu.SemaphoreType.DMA((n, 7)), pltpu.SemaphoreType.DMA((n, 7)),
                        pltpu.SemaphoreType.DMA((n,))],
    )(*xs)


def _all_to_all(xs, name):
    n = len(xs)

    def body(*refs):
        ins, outs = refs[:n], refs[n:2 * n]
        send_sems, recv_sems, local_sems = refs[2 * n:]
        x, y, c = lax.axis_index("x"), lax.axis_index("y"), lax.axis_index("c")
        me = 4 * x + 2 * y + c
        peers = []
        for m in range(1, N_DEV):
            px = 1 - x if m & 4 else x
            py = 1 - y if m & 2 else y
            pc_ = 1 - c if m & 1 else c
            peers.append((px, py, pc_))

        def copy(a, k):
            peer = peers[k]
            pid = 4 * peer[0] + 2 * peer[1] + peer[2]
            return pltpu.make_async_remote_copy(
                src_ref=ins[a].at[pid], dst_ref=outs[a].at[me],
                send_sem=send_sems.at[a, k], recv_sem=recv_sems.at[a, k],
                device_id=peer, device_id_type=MESH)

        def landed(a, k):
            peer = peers[k]
            pid = 4 * peer[0] + 2 * peer[1] + peer[2]
            return pltpu.make_async_remote_copy(
                src_ref=ins[a].at[pid], dst_ref=outs[a].at[pid],
                send_sem=send_sems.at[a, k], recv_sem=recv_sems.at[a, k],
                device_id=peer, device_id_type=MESH)

        mine, sent = [], []
        for a in range(n):
            cp = pltpu.make_async_copy(ins[a].at[me], outs[a].at[me], local_sems.at[a])
            cp.start()
            mine.append(cp)
        for k in range(N_DEV - 1):
            for a in range(n):
                cp = copy(a, k)
                cp.start()
                sent.append(cp)
        for k in range(N_DEV - 1):
            for a in range(n):
                landed(a, k).wait_recv()
        for cp in sent:
            cp.wait_send()
        for cp in mine:
            cp.wait()

    any_spec = pl.BlockSpec(memory_space=pl.ANY)
    return pl.pallas_call(
        body, name=name,
        out_shape=[jax.ShapeDtypeStruct(v.shape, v.dtype) for v in xs],
        in_specs=[any_spec] * n, out_specs=[any_spec] * n,
        scratch_shapes=[pltpu.SemaphoreType.DMA((n, 7)), pltpu.SemaphoreType.DMA((n, 7)),
                        pltpu.SemaphoreType.DMA((n,))],
    )(*xs)


def _mod_fwd(c64, w_ada, b_loc):
    def body(c_ref, w_ref, b_ref, o_ref):
        cv = c_ref[...]
        sc = cv * _sigmoid(cv)
        o_ref[...] = _mm(sc, w_ref[...]) + b_ref[...]

    return pl.pallas_call(
        body, name="mod_fwd",
        out_shape=jax.ShapeDtypeStruct((c64.shape[0], w_ada.shape[1]), F32),
        compiler_params=_params(),
    )(c64, w_ada, b_loc)


def _inproj(x, mod3, norm_g, w_in_g, tm=512):
    S = x.shape[0]

    def body(x_ref, mod_ref, g_ref, w_ref, proj_ref, h_ref, hs):
        @pl.when(pl.program_id(1) == 0)
        def _():
            xv = x_ref[...]
            r = lax.rsqrt(jnp.mean(xv * xv, axis=-1, keepdims=True) + EPS)
            h = (xv * r * g_ref[...]) * (1.0 + mod_ref[1:2, :]) + mod_ref[0:1, :]
            hs[...] = h.astype(MXU_DTYPE)
            h_ref[...] = h.astype(MXU_DTYPE)
        proj_ref[...] = _mm(hs[...], w_ref[0])

    return pl.pallas_call(
        body, name="inproj",
        grid=(S // tm, N_DEV),
        in_specs=[pl.BlockSpec((tm, D), lambda i, j: (i, 0)),
                  pl.BlockSpec((8, D), lambda i, j: (0, 0)),
                  pl.BlockSpec((1, D), lambda i, j: (0, 0)),
                  pl.BlockSpec((1, D, D), lambda i, j: (j, 0, 0))],
        out_specs=[pl.BlockSpec((tm, D), lambda i, j: (i, j)),
                   pl.BlockSpec((tm, D), lambda i, j: (i, 0))],
        out_shape=[jax.ShapeDtypeStruct((S, IN_W), F32), jax.ShapeDtypeStruct((S, D), MXU_DTYPE)],
        scratch_shapes=[pltpu.VMEM((tm, D), MXU_DTYPE)],
        compiler_params=_params(("parallel", "arbitrary")),
    )(x, mod3, norm_g, w_in_g)


def _bucket_tables():
    qi = np.arange(A_BLK)[:, None]
    kj = np.arange(2 * A_BLK)[None, :]
    delta = qi + A_BLK - kj
    out = []
    for window, dil in PATTERNS:
        span = window // dil
        band = (delta >= 0) & (delta <= span)
        dist = np.clip(delta, 0, None) * dil
        max_exact = N_BUCKETS // 2
        nf = dist.astype(np.float32)
        large = max_exact + (np.log(np.maximum(nf, np.float32(1.0)) / np.float32(max_exact))
                             / np.float32(math.log(MAX_DISTANCE / max_exact))
                             * np.float32(N_BUCKETS - max_exact)).astype(np.int32)
        large = np.minimum(large, N_BUCKETS - 1)
        bucket = np.where(dist < max_exact, dist, large)
        out.append(np.where(band, bucket, -1).astype(np.int32))
    return np.stack(out)


def _bias_tables(rel_bias, buckets):
    def body(rb_ref, bk_ref, o_ref):
        h = pl.program_id(1)
        bk = bk_ref[0]
        acc = jnp.full(bk.shape, NEG, F32)
        for b in range(N_BUCKETS):
            acc = jnp.where(bk == b, rb_ref[b, h], acc)
        o_ref[0, 0] = acc

    return pl.pallas_call(
        body, name="bias_tables",
        grid=(3, A_HEADS),
        in_specs=[pl.BlockSpec(memory_space=pltpu.SMEM),
                  pl.BlockSpec((1, A_BLK, 2 * A_BLK), lambda p, h: (p, 0, 0))],
        out_specs=pl.BlockSpec((1, 1, A_BLK, 2 * A_BLK), lambda p, h: (p, h, 0, 0)),
        out_shape=jax.ShapeDtypeStruct((3, A_HEADS, A_BLK, 2 * A_BLK), F32),
        compiler_params=_params(("arbitrary", "arbitrary")),
    )(rel_bias, buckets)


A_PAIR = 2 * A_HD


def _attn_in_specs(sb, nsb):
    def cur(col):
        return pl.BlockSpec((sb, A_PAIR), lambda hp, n: (jnp.minimum(n, nsb - 1), 4 * col + hp))

    def prev(col):
        return pl.BlockSpec((sb, A_PAIR), lambda hp, n: (jnp.maximum(jnp.minimum(n, nsb - 1) - 1, 0), 4 * col + hp))

    return [cur(0), prev(1), cur(1), prev(2), cur(2)]


def _rows(r, d):
    return pl.ds(r, A_BLK) if d == 1 else pl.ds(r, A_BLK, stride=d)


def _for_residues(d, fn):
    if d == 1:
        fn(0)
    else:
        lax.fori_loop(0, d, lambda r, c: (fn(r), c)[1], 0)


def _attn_scores(q, kp, kc, bias, hh, first):
    sl = slice(A_HD * hh, A_HD * (hh + 1))
    k = jnp.concatenate([kp[:, sl], kc[:, sl]], axis=0)
    s = _mm(q[:, sl], k, NT) * (A_HD ** -0.5) + bias + first
    return s, k


def _attn_fwd(proj, bias_p, d, name):
    S = proj.shape[0]
    sb = A_BLK * d
    nsb = S // sb

    def body(q_ref, kp_ref, kc_ref, vp_ref, vc_ref, b_ref, o_ref, l_ref):
        n = pl.program_id(1)
        kj = lax.broadcasted_iota(jnp.int32, (A_BLK, 2 * A_BLK), 1)
        first = jnp.where((n == 0) & (kj < A_BLK), NEG, 0.0).astype(F32)

        def residue(r):
            rows = _rows(r, d)
            q, kp, kc, vp, vc = q_ref[rows, :], kp_ref[rows, :], kc_ref[rows, :], vp_ref[rows, :], vc_ref[rows, :]
            os, ls = [], []
            for hh in range(2):
                sl = slice(A_HD * hh, A_HD * (hh + 1))
                s, _ = _attn_scores(q, kp, kc, b_ref[hh], hh, first)
                v = jnp.concatenate([vp[:, sl], vc[:, sl]], axis=0)
                m = jnp.max(s, axis=-1, keepdims=True)
                p = jnp.exp(s - m)
                den = jnp.sum(p, axis=-1, keepdims=True)
                os.append(_mm(p, v) / den)
                ls.append(jnp.broadcast_to(m + jnp.log(den), (A_BLK, A_HD)))
            o_ref[rows, :] = jnp.concatenate(os, axis=1)
            l_ref[rows, :] = jnp.concatenate(ls, axis=1)

        _for_residues(d, residue)

    out = pl.BlockSpec((sb, A_PAIR), lambda hp, n: (n, hp))
    return pl.pallas_call(
        body, name=name,
        grid=(A_HEADS // 2, nsb),
        in_specs=_attn_in_specs(sb, nsb) + [pl.BlockSpec((2, A_BLK, 2 * A_BLK), lambda hp, n: (hp, 0, 0))],
        out_specs=[out, out],
        out_shape=[jax.ShapeDtypeStruct((S, A_W), F32)] * 2,
        compiler_params=_params(("parallel", "parallel")),
    )(proj, proj, proj, proj, proj, bias_p)


def _attn_combine(os, ls, proj, tm=512):
    S = proj.shape[0]

    def body(o1, o2, o3, l1, l2, l3, z_ref, ao_ref, lt_ref, oa_ref):
        a1, a2, a3 = l1[...], l2[...], l3[...]
        m = jnp.maximum(jnp.maximum(a1, a2), a3)
        e1, e2, e3 = jnp.exp(a1 - m), jnp.exp(a2 - m), jnp.exp(a3 - m)
        den = e1 + e2 + e3
        ao = (e1 * o1[...] + e2 * o2[...] + e3 * o3[...]) / den
        z = z_ref[...]
        ao_ref[...] = ao
        lt_ref[...] = m + jnp.log(den)
        oa_ref[...] = (ao * (z * _sigmoid(z))).astype(MXU_DTYPE)

    spec = pl.BlockSpec((tm, A_W), lambda i: (i, 0))
    return pl.pallas_call(
        body, name="attn_combine",
        grid=(S // tm,),
        in_specs=[spec] * 6 + [pl.BlockSpec((tm, A_W), lambda i: (i, 3))],
        out_specs=[spec] * 3,
        out_shape=[jax.ShapeDtypeStruct((S, A_W), F32), jax.ShapeDtypeStruct((S, A_W), F32),
                   jax.ShapeDtypeStruct((S, A_W), MXU_DTYPE)],
        compiler_params=_params(("parallel",)),
    )(*os, *ls, proj)


def _attn_pre_bwd(doa, ao, proj, tm=512):
    S = proj.shape[0]

    def body(doa_ref, ao_ref, z_ref, do_ref, dz_ref, dl_ref):
        z = z_ref[...]
        sg = _sigmoid(z)
        g = doa_ref[...]
        ao_v = ao_ref[...]
        do = g * (z * sg)
        do_ref[...] = do
        dz_ref[...] = (g * ao_v * (sg * (1.0 + z * (1.0 - sg)))).astype(MXU_DTYPE)
        prod = do * ao_v
        for h in range(A_HEADS):
            sl = slice(A_HD * h, A_HD * (h + 1))
            dl_ref[:, sl] = jnp.broadcast_to(jnp.sum(prod[:, sl], axis=-1, keepdims=True), (tm, A_HD))

    spec = pl.BlockSpec((tm, A_W), lambda i: (i, 0))
    return pl.pallas_call(
        body, name="attn_pre_bwd",
        grid=(S // tm,),
        in_specs=[spec, spec, pl.BlockSpec((tm, A_W), lambda i: (i, 3))],
        out_specs=[spec] * 3,
        out_shape=[jax.ShapeDtypeStruct((S, A_W), F32), jax.ShapeDtypeStruct((S, A_W), MXU_DTYPE),
                   jax.ShapeDtypeStruct((S, A_W), F32)],
        compiler_params=_params(("parallel",)),
    )(doa, ao, proj)


def _attn_bwd(proj, do, lt, delta, bias_p, d, name):
    S = proj.shape[0]
    sb = A_BLK * d
    nsb = S // sb

    def body(q_ref, kp_ref, kc_ref, vp_ref, vc_ref, do_ref, lt_ref, dl_ref, b_ref,
             dq_ref, dk_ref, dv_ref, db_ref, ck, cv):
        n = pl.program_id(1)

        @pl.when(n == 0)
        def _():
            db_ref[...] = jnp.zeros_like(db_ref)
            ck[...] = jnp.zeros_like(ck)
            cv[...] = jnp.zeros_like(cv)

        @pl.when(n < nsb)
        def _():
            kj = lax.broadcasted_iota(jnp.int32, (A_BLK, 2 * A_BLK), 1)
            first = jnp.where((n == 0) & (kj < A_BLK), NEG, 0.0).astype(F32)

            def residue(r):
                rows = _rows(r, d)
                q, kp, kc, vp, vc = q_ref[rows, :], kp_ref[rows, :], kc_ref[rows, :], vp_ref[rows, :], vc_ref[rows, :]
                do_r, lt_r, dl_r = do_ref[rows, :], lt_ref[rows, :], dl_ref[rows, :]
                dqs, dks, dvs = [], [], []
                for hh in range(2):
                    sl = slice(A_HD * hh, A_HD * (hh + 1))
                    s, k = _attn_scores(q, kp, kc, b_ref[hh], hh, first)
                    v = jnp.concatenate([vp[:, sl], vc[:, sl]], axis=0)
                    p = jnp.exp(s - lt_r[:, A_HD * hh:A_HD * hh + 1])
                    do_h = do_r[:, sl]
                    ds = p * (_mm(do_h, v, NT) - dl_r[:, A_HD * hh:A_HD * hh + 1])
                    db_ref[hh] += ds
                    dqs.append(_mm(ds, k) * (A_HD ** -0.5))
                    dks.append(_mm(ds, q[:, sl], TN) * (A_HD ** -0.5))
                    dvs.append(_mm(p, do_h, TN))
                dk = jnp.concatenate(dks, axis=1)
                dv = jnp.concatenate(dvs, axis=1)
                dq_ref[rows, :] = jnp.concatenate(dqs, axis=1)
                dk_ref[rows, :] = ck[rows, :] + dk[:A_BLK]
                dv_ref[rows, :] = cv[rows, :] + dv[:A_BLK]
                ck[rows, :] = dk[A_BLK:]
                cv[rows, :] = dv[A_BLK:]

            _for_residues(d, residue)

        @pl.when(n == nsb)
        def _():
            dk_ref[...] = ck[...]
            dv_ref[...] = cv[...]

    row = pl.BlockSpec((sb, A_PAIR), lambda hp, n: (jnp.minimum(n, nsb - 1), hp))
    lag = pl.BlockSpec((sb, A_PAIR), lambda hp, n: (jnp.maximum(n - 1, 0), hp))
    tab = pl.BlockSpec((2, A_BLK, 2 * A_BLK), lambda hp, n: (hp, 0, 0))
    return pl.pallas_call(
        body, name=name,
        grid=(A_HEADS // 2, nsb + 1),
        in_specs=_attn_in_specs(sb, nsb) + [row, row, row, tab],
        out_specs=[row, lag, lag, tab],
        out_shape=[jax.ShapeDtypeStruct((S, A_W), F32)] * 3
                  + [jax.ShapeDtypeStruct((A_HEADS, A_BLK, 2 * A_BLK), F32)],
        scratch_shapes=[pltpu.VMEM((sb, A_PAIR), F32), pltpu.VMEM((sb, A_PAIR), F32)],
        compiler_params=_params(("parallel", "arbitrary")),
    )(proj, proj, proj, proj, proj, do, lt, delta, bias_p)


def _attn_assemble(dqs, dks, dvs, dz, tm=512):
    S = dz.shape[0]

    def body(q1, q2, q3, k1, k2, k3, v1, v2, v3, z_ref, p0_ref, p1_ref):
        p0_ref[:, :A_W] = (q1[...] + q2[...] + q3[...]).astype(MXU_DTYPE)
        p0_ref[:, A_W:] = (k1[...] + k2[...] + k3[...]).astype(MXU_DTYPE)
        p1_ref[:, :A_W] = (v1[...] + v2[...] + v3[...]).astype(MXU_DTYPE)
        p1_ref[:, A_W:] = z_ref[...]

    spec = pl.BlockSpec((tm, A_W), lambda i: (i, 0))
    wide = pl.BlockSpec((tm, 2 * A_W), lambda i: (i, 0))
    return pl.pallas_call(
        body, name="attn_assemble",
        grid=(S // tm,),
        in_specs=[spec] * 10,
        out_specs=[wide, wide],
        out_shape=[jax.ShapeDtypeStruct((S, 2 * A_W), MXU_DTYPE)] * 2,
        compiler_params=_params(("parallel",)),
    )(*dqs, *dks, *dvs, dz)


def _rel_bias_grad(dbs, buckets):
    def body(d1, d2, d3, bk_ref, o_ref):
        row = lax.broadcasted_iota(jnp.int32, (A_HEADS, 128), 0)
        lane = lax.broadcasted_iota(jnp.int32, (A_HEADS, 128), 1)
        acc = jnp.zeros((A_HEADS, 128), F32)
        for p, dref in enumerate((d1, d2, d3)):
            bk = bk_ref[p]
            for h in range(A_HEADS):
                ds = dref[h]
                for b in range(N_BUCKETS):
                    s = jnp.sum(jnp.where(bk == b, ds, 0.0), keepdims=True)
                    acc = acc + jnp.where((row == h) & (lane == b), s, 0.0)
        o_ref[...] = acc

    return pl.pallas_call(
        body, name="rel_bias_grad",
        out_shape=jax.ShapeDtypeStruct((A_HEADS, 128), F32),
        compiler_params=_params(),
    )(*dbs, buckets)


def _tri(c):
    t = np.tril(np.ones((c, c), np.float32))
    return jnp.asarray(t), jnp.asarray(t.T.copy())


def _roll(x, shift):
    return x if shift == 0 else pltpu.roll(x, shift, 0)


def _hgrn_gates(q_ref, f_ref, lbp_ref, tri_ref):
    qraw = q_ref[...]
    sq = _sigmoid(qraw)
    q = qraw * sq
    sg = _sigmoid(f_ref[...])
    lb = _sigmoid(lbp_ref[0:1, :] - lbp_ref[1:2, :])
    f = lb + (1.0 - lb) * sg
    k = 1.0 - f
    b = _mm_exact(tri_ref[...], jnp.log(f))
    return qraw, sq, q, sg, lb, f, k, b


def _hgrn_col(C, base, idx):
    return pl.BlockSpec((C, G_DK), lambda h, n: (idx(n), base * G_HEADS + h))


def _hgrn_levels(C):
    out, m = [], G_SUB
    while 2 * m <= C:
        out.append(m)
        m *= 2
    return out


def _hgrn_level(b, C, m):
    zeros = jnp.zeros((m, G_DK), F32)
    refq, refk = [], []
    for blk in range(C // m):
        if blk % 2 == 1:
            refq.append(jnp.broadcast_to(b[blk * m:blk * m + 1], (m, G_DK)))
            refk.append(zeros)
        else:
            refq.append(zeros)
            refk.append(jnp.broadcast_to(b[(blk + 1) * m:(blk + 1) * m + 1], (m, G_DK)))
    refq = jnp.concatenate(refq, axis=0)
    refk = jnp.concatenate(refk, axis=0)
    right = (lax.broadcasted_iota(jnp.int32, (C, 1), 0) // m) % 2 == 1
    eq = jnp.exp(jnp.where(right, b - refq, NEG))
    ek = jnp.exp(jnp.where(right, NEG, refk - b))
    ti = lax.broadcasted_iota(jnp.int32, (C, C), 0)
    si = lax.broadcasted_iota(jnp.int32, (C, C), 1)
    return eq, ek, (ti // (2 * m) == si // (2 * m)) & (ti - si >= G_SUB)


def _hgrn_fwd(proj, hgrn_lb, onorm_g, C=G_CHUNK):
    S = proj.shape[0]
    nc = S // C
    tri, _ = _tri(C)

    def body(q_ref, f_ref, i_ref, z_ref, lbp_ref, go_ref, tri_ref, o_ref, ob_ref, st_ref, St):
        n = pl.program_id(1)

        @pl.when(n == 0)
        def _():
            St[...] = jnp.zeros_like(St)

        _, _, q, _, _, _, k, b = _hgrn_gates(q_ref, f_ref, lbp_ref, tri_ref)
        v = i_ref[...]
        bC = b[C - 1:C, :]
        S0 = St[...]
        o = _mm(q * jnp.exp(b), S0, NT)
        row = lax.broadcasted_iota(jnp.int32, (C, 1), 0)
        for l in range(G_SUB):
            e = jnp.exp(jnp.where(row >= l, b - _roll(b, l), NEG))
            a = jnp.sum(q * _roll(k, l) * e, axis=-1, keepdims=True)
            o = o + a * _roll(v, l)
        a_off = jnp.zeros((C, C), F32)
        for m in _hgrn_levels(C):
            eq, ek, pm = _hgrn_level(b, C, m)
            a_off = a_off + jnp.where(pm, _mm(q * eq, k * ek, NT), 0.0)
        o = o + _mm(a_off, v)
        S1 = S0 * jnp.exp(bC) + _mm(v, k * jnp.exp(bC - b), TN)
        St[...] = S1
        st_ref[0, 0] = S1
        o_ref[...] = o
        r = lax.rsqrt(jnp.mean(o * o, axis=-1, keepdims=True) + EPS)
        z = z_ref[...]
        ob_ref[...] = (o * r * go_ref[...] * (z * _sigmoid(z))).astype(MXU_DTYPE)

    ident = lambda n: n
    out = pl.BlockSpec((C, G_DK), lambda h, n: (n, h))
    return pl.pallas_call(
        body, name="hgrn_fwd",
        grid=(G_HEADS, nc),
        in_specs=[_hgrn_col(C, 2, ident), _hgrn_col(C, 3, ident), _hgrn_col(C, 4, ident), _hgrn_col(C, 5, ident),
                  pl.BlockSpec((2, G_DK), lambda h, n: (0, h)),
                  pl.BlockSpec((1, G_DK), lambda h, n: (0, 0)),
                  pl.BlockSpec((C, C), lambda h, n: (0, 0))],
        out_specs=[out, out, pl.BlockSpec((1, 1, G_DK, G_DK), lambda h, n: (n, h, 0, 0))],
        out_shape=[jax.ShapeDtypeStruct((S, G_W), F32), jax.ShapeDtypeStruct((S, G_W), MXU_DTYPE),
                   jax.ShapeDtypeStruct((nc, G_HEADS, G_DK, G_DK), F32)],
        scratch_shapes=[pltpu.VMEM((G_DK, G_DK), F32)],
        compiler_params=_params(("parallel", "arbitrary")),
    )(proj, proj, proj, proj, hgrn_lb, onorm_g, tri)


def _hgrn_bwd(proj, o_raw, dob, states, hgrn_lb, onorm_g, C=G_CHUNK):
    S = proj.shape[0]
    nc = S // C
    tri, triu = _tri(C)

    def body(q_ref, f_ref, i_ref, z_ref, o_ref, dob_ref, s0_ref, s1_ref, lbp_ref, go_ref, tri_ref, triu_ref,
             dq_ref, df_ref, di_ref, dz_ref, dlb_ref, dgo_ref, dSt):
        n = pl.program_id(1)
        cn = nc - 1 - n

        @pl.when(n == 0)
        def _():
            dSt[...] = jnp.zeros_like(dSt)
            dlb_ref[...] = jnp.zeros_like(dlb_ref)
            dgo_ref[...] = jnp.zeros_like(dgo_ref)

        qraw, sq, q, sg, lb, f, k, b = _hgrn_gates(q_ref, f_ref, lbp_ref, tri_ref)
        v = i_ref[...]
        bC = b[C - 1:C, :]
        eb = jnp.exp(b)
        ecb = jnp.exp(bC - b)
        o = o_ref[...]
        z = z_ref[...]
        sz = _sigmoid(z)
        go = go_ref[...]
        g_ob = dob_ref[...]
        r = lax.rsqrt(jnp.mean(o * o, axis=-1, keepdims=True) + EPS)
        nh = o * r
        dnrm = g_ob * (z * sz)
        dz_ref[...] = (g_ob * (nh * go) * (sz * (1.0 + z * (1.0 - sz)))).astype(MXU_DTYPE)
        dgo_ref[0:1, :] += jnp.sum(dnrm * nh, axis=0, keepdims=True)
        dn = dnrm * go
        do = r * (dn - nh * jnp.mean(dn * nh, axis=-1, keepdims=True))

        S0 = jnp.where(cn == 0, 0.0, s0_ref[0, 0])
        S1 = s1_ref[0, 0]
        dS1 = dSt[...]
        dq = eb * _mm(do, S0)
        dk = ecb * _mm(v, dS1)
        dv = _mm(k * ecb, dS1, NT)
        bnd = jnp.sum(dS1 * S1, axis=0, keepdims=True)
        dSt[...] = dS1 * jnp.exp(bC) + _mm(do, q * eb, TN)

        row = lax.broadcasted_iota(jnp.int32, (C, 1), 0)
        for l in range(G_SUB):
            e = jnp.exp(jnp.where(row >= l, b - _roll(b, l), NEG))
            da =jnp.sum(do * _roll(v, l), axis=-1, keepdims=True)
            dq = dq + da * (_roll(k, l) * e)
            up = (C - l) % C
            e2 = jnp.exp(jnp.where(row + l < C, _roll(b, up) - b, NEG))
            qe = _roll(q, up) * e2
            dou = _roll(do, up)
            da2 = jnp.sum(dou * v, axis=-1, keepdims=True)
            a2 = jnp.sum(qe * k, axis=-1, keepdims=True)
            dk = dk + da2 * qe
            dv = dv + a2 * dou

        da_all = _mm(do, v, NT)
        a_off = jnp.zeros((C, C), F32)
        for m in _hgrn_levels(C):
            eq, ek, pm = _hgrn_level(b, C, m)
            qt, kt = q * eq, k * ek
            da_m = jnp.where(pm, da_all, 0.0)
            a_off = a_off + jnp.where(pm, _mm(qt, kt, NT), 0.0)
            dq = dq + _mm(da_m, kt) * eq
            dk = dk + _mm(da_m, qt, TN) * ek
        dv = dv + _mm(a_off, do, TN)

        row = lax.broadcasted_iota(jnp.int32, (C, 1), 0)
        db = q * dq - k * dk + jnp.where(row == C - 1, bnd, 0.0)
        dg = _mm_exact(triu_ref[...], db)
        df = dg / f - dk
        df_ref[...] = (df * (1.0 - lb) * (sg * (1.0 - sg))).astype(MXU_DTYPE)
        dlb_ref[0:1, :] += jnp.sum(df * (1.0 - sg), axis=0, keepdims=True)
        dq_ref[...] = (dq * (sq * (1.0 + qraw * (1.0 - sq)))).astype(MXU_DTYPE)
        di_ref[...] = dv.astype(MXU_DTYPE)

    rev = lambda n: nc - 1 - n
    blk = pl.BlockSpec((C, G_DK), lambda h, n: (nc - 1 - n, h))
    return pl.pallas_call(
        body, name="hgrn_bwd",
        grid=(G_HEADS, nc),
        in_specs=[_hgrn_col(C, 2, rev), _hgrn_col(C, 3, rev), _hgrn_col(C, 4, rev), _hgrn_col(C, 5, rev), blk, blk,
                  pl.BlockSpec((1, 1, G_DK, G_DK), lambda h, n: (jnp.maximum(nc - 2 - n, 0), h, 0, 0)),
                  pl.BlockSpec((1, 1, G_DK, G_DK), lambda h, n: (nc - 1 - n, h, 0, 0)),
                  pl.BlockSpec((2, G_DK), lambda h, n: (0, h)),
                  pl.BlockSpec((1, G_DK), lambda h, n: (0, 0)),
                  pl.BlockSpec((C, C), lambda h, n: (0, 0)),
                  pl.BlockSpec((C, C), lambda h, n: (0, 0))],
        out_specs=[blk, blk, blk, blk,
                   pl.BlockSpec((8, G_DK), lambda h, n: (0, h)),
                   pl.BlockSpec((8, G_DK), lambda h, n: (h, 0))],
        out_shape=[jax.ShapeDtypeStruct((S, G_W), MXU_DTYPE)] * 4
                  + [jax.ShapeDtypeStruct((8, G_W), F32), jax.ShapeDtypeStruct((8 * G_HEADS, G_DK), F32)],
        scratch_shapes=[pltpu.VMEM((G_DK, G_DK), F32)],
        compiler_params=_params(("parallel", "arbitrary")),
    )(proj, proj, proj, proj, o_raw, dob, states, states, hgrn_lb, onorm_g, tri, triu)


def _tail(x, target, oa, ob, proj, mod3, final_g, wa, wb, wo, tm=256):
    S = x.shape[0]
    nt = S // tm

    def body(x_ref, t_ref, oa_ref, ob_ref, ga_ref, gb_ref, mod_ref, fg_ref, wa_ref, wb_ref, wo_ref,
             dx2_ref, doa_ref, dob_ref, dga_ref, dgb_ref, sums_ref, gwa_ref, gwb_ref, gwo_ref,
             acc_a, acc_b, acc_o):
        i = pl.program_id(0)

        @pl.when(i == 0)
        def _():
            sums_ref[...] = jnp.zeros_like(sums_ref)
            acc_a[...] = jnp.zeros_like(acc_a)
            acc_b[...] = jnp.zeros_like(acc_b)
            acc_o[...] = jnp.zeros_like(acc_o)

        oa_v, ob_v = oa_ref[...], ob_ref[...]
        pa = _mm(oa_v, wa_ref[...])
        pb = _mm(ob_v, wb_ref[...])
        sa, sb = _sigmoid(ga_ref[...]), _sigmoid(gb_ref[...])
        ym = sa * pa + sb * pb
        u = _mm(ym, wo_ref[...])
        gate = mod_ref[2:3, :]
        fg = fg_ref[...]
        x2 = x_ref[...] + gate * u
        r2 = lax.rsqrt(jnp.mean(x2 * x2, axis=-1, keepdims=True) + EPS)
        xn2 = x2 * r2
        e = xn2 * fg - t_ref[...]
        dy = e * (1.0 / D)
        dn = dy * fg
        dx2 = r2 * (dn - xn2 * jnp.mean(dn * xn2, axis=-1, keepdims=True))
        dx2_ref[...] = dx2
        sums_ref[0:1, :] += jnp.sum(dy * xn2, axis=0, keepdims=True)
        sums_ref[1:2, :] += jnp.sum(dx2 * u, axis=0, keepdims=True)
        sums_ref[2:3, :] += (0.5 / D) * jnp.sum(e * e, axis=0, keepdims=True)
        du = dx2 * gate
        dym = _mm(du, wo_ref[...], NT)
        acc_o[...] += _mm(ym, du, TN)
        dpa, dpb = dym * sa, dym * sb
        dga_ref[...] = (dym * pa * (sa * (1.0 - sa))).astype(MXU_DTYPE)
        dgb_ref[...] = (dym * pb * (sb * (1.0 - sb))).astype(MXU_DTYPE)
        doa_ref[...] = _mm(dpa, wa_ref[...], NT)
        dob_ref[...] = _mm(dpb, wb_ref[...], NT)
        acc_a[...] += _mm(oa_v, dpa, TN)
        acc_b[...] += _mm(ob_v, dpb, TN)

        @pl.when(i == nt - 1)
        def _():
            pltpu.sync_copy(acc_a, gwa_ref)
            pltpu.sync_copy(acc_b, gwb_ref)
            pltpu.sync_copy(acc_o, gwo_ref)

    row = lambda w: pl.BlockSpec((tm, w), lambda i: (i, 0))
    full = lambda a, b: pl.BlockSpec((a, b), lambda i: (0, 0))
    any_spec = pl.BlockSpec(memory_space=pl.ANY)
    return pl.pallas_call(
        body, name="tail",
        grid=(nt,),
        in_specs=[row(D), row(D), row(A_W), row(D),
                  pl.BlockSpec((tm, D), lambda i: (i, 6)), pl.BlockSpec((tm, D), lambda i: (i, 7)),
                  full(8, D), full(1, D), full(A_W, D), full(D, D), full(D, D)],
        out_specs=[row(D), row(A_W), row(D), row(D), row(D), full(8, D), any_spec, any_spec, any_spec],
        out_shape=[jax.ShapeDtypeStruct((S, D), F32), jax.ShapeDtypeStruct((S, A_W), F32),
                   jax.ShapeDtypeStruct((S, D), F32), jax.ShapeDtypeStruct((S, D), MXU_DTYPE),
                   jax.ShapeDtypeStruct((S, D), MXU_DTYPE), jax.ShapeDtypeStruct((8, D), F32),
                   jax.ShapeDtypeStruct((A_W, D), F32), jax.ShapeDtypeStruct((D, D), F32),
                   jax.ShapeDtypeStruct((D, D), F32)],
        scratch_shapes=[pltpu.VMEM((A_W, D), F32), pltpu.VMEM((D, D), F32), pltpu.VMEM((D, D), F32)],
        compiler_params=_params(("arbitrary",)),
    )(x, target, oa, ob, proj, proj, mod3, final_g, wa, wb, wo)


def _dh(pieces, w_in_g, x, dx2, mod3, norm_g, tm=512):
    S = x.shape[0]

    def body(*refs):
        p_refs = refs[:N_DEV]
        w_ref, x_ref, dx2_ref, mod_ref, g_ref, gx_ref, sums_ref, acc = refs[N_DEV:]
        i, j = pl.program_id(0), pl.program_id(1)

        @pl.when((i == 0) & (j == 0))
        def _():
            sums_ref[...] = jnp.zeros_like(sums_ref)

        @pl.when(j == 0)
        def _():
            acc[...] = jnp.zeros_like(acc)

        for k in range(N_DEV):
            @pl.when(j == k)
            def _(k=k):
                acc[...] += _mm(p_refs[k][...], w_ref[0], NT)

        @pl.when(j == N_DEV - 1)
        def _():
            dh = acc[...]
            xv = x_ref[...]
            g = g_ref[...]
            sc1 = 1.0 + mod_ref[1:2, :]
            r = lax.rsqrt(jnp.mean(xv * xv, axis=-1, keepdims=True) + EPS)
            xn = xv * r
            sums_ref[0:1, :] += jnp.sum(dh, axis=0, keepdims=True)
            sums_ref[1:2, :] += jnp.sum(dh * (xn * g), axis=0, keepdims=True)
            sums_ref[2:3, :] += jnp.sum(dh * sc1 * xn, axis=0, keepdims=True)
            dxn = dh * sc1 * g
            gx_ref[...] = dx2_ref[...] + r * (dxn - xn * jnp.mean(dxn * xn, axis=-1, keepdims=True))

    row = pl.BlockSpec((tm, D), lambda i, j: (i, 0))
    return pl.pallas_call(
        body, name="dh",
        grid=(S // tm, N_DEV),
        in_specs=[row] * N_DEV
                 + [pl.BlockSpec((1, D, D), lambda i, j: (j, 0, 0)),
                    row, row,
                    pl.BlockSpec((8, D), lambda i, j: (0, 0)),
                    pl.BlockSpec((1, D), lambda i, j: (0, 0))],
        out_specs=[row, pl.BlockSpec((8, D), lambda i, j: (0, 0))],
        out_shape=[jax.ShapeDtypeStruct((S, D), F32), jax.ShapeDtypeStruct((8, D), F32)],
        scratch_shapes=[pltpu.VMEM((tm, D), F32)],
        compiler_params=_params(("arbitrary", "arbitrary")),
    )(*pieces, w_in_g, x, dx2, mod3, norm_g)


def _gw_in(h, pieces, tm=512):
    S = h.shape[0]
    nt = S // tm

    def body(*refs):
        h_ref, p_refs, o_ref, acc = refs[0], refs[1:1 + N_DEV], refs[1 + N_DEV], refs[2 + N_DEV]
        j, i = pl.program_id(0), pl.program_id(1)

        @pl.when(i == 0)
        def _():
            acc[...] = jnp.zeros_like(acc)

        for k in range(N_DEV):
            @pl.when(j == k)
            def _(k=k):
                acc[...] += _mm(h_ref[...], p_refs[k][...], TN)

        @pl.when(i == nt - 1)
        def _():
            o_ref[0] = acc[...].astype(XCHG_DTYPE)

    def piece(k):
        return pl.BlockSpec((tm, D), lambda j, i: (jnp.where(j == k, i, 0), 0))

    return pl.pallas_call(
        body, name="gw_in",
        grid=(N_DEV, nt),
        in_specs=[pl.BlockSpec((tm, D), lambda j, i: (i, 0))] + [piece(k) for k in range(N_DEV)],
        out_specs=pl.BlockSpec((1, D, D), lambda j, i: (j, 0, 0)),
        out_shape=jax.ShapeDtypeStruct((N_DEV, D, D), XCHG_DTYPE),
        scratch_shapes=[pltpu.VMEM((D, D), F32)],
        compiler_params=_params(("parallel", "arbitrary")),
    )(h, *pieces)


def _adamw_math(w, g, m, v):
    m = ADAM_B1 * m + (1.0 - ADAM_B1) * g
    v = ADAM_B2 * v + (1.0 - ADAM_B2) * (g * g)
    m_hat = m / (1.0 - ADAM_B1 ** ADAM_STEP)
    v_hat = v / (1.0 - ADAM_B2 ** ADAM_STEP)
    delta = -ADAM_LR * (m_hat / (jnp.sqrt(v_hat) + ADAM_EPS) + ADAM_WD * w)
    return delta, m, v


def _adamw_big(recv, w, m, v, name, tr=128):
    M, N = w.shape
    tr = min(tr, M)

    def body(r_ref, w_ref, m_ref, v_ref, g_ref, d_ref, nm_ref, nv_ref):
        g = r_ref[0].astype(F32)
        for j in range(1, N_DEV):
            g = g + r_ref[j].astype(F32)
        g_ref[...] = g
        d_ref[...], nm_ref[...], nv_ref[...] = _adamw_math(w_ref[...], g, m_ref[...], v_ref[...])

    blk = pl.BlockSpec((tr, N), lambda i: (i, 0))
    return pl.pallas_call(
        body, name=name,
        grid=(M // tr,),
        in_specs=[pl.BlockSpec((N_DEV, tr, N), lambda i: (0, i, 0)), blk, blk, blk],
        out_specs=[blk] * 4,
        out_shape=[jax.ShapeDtypeStruct((M, N), F32)] * 4,
        compiler_params=_params(("parallel",)),
    )(recv, w, m, v)


def _adamw_w_ada(c64, dmod64, w, m, v):
    def body(c_ref, dm_ref, w_ref, m_ref, v_ref, g_ref, d_ref, nm_ref, nv_ref):
        cv = c_ref[...]
        g = _mm(cv * _sigmoid(cv), dm_ref[...], TN)
        g_ref[...] = g
        d_ref[...], nm_ref[...], nv_ref[...] = _adamw_math(w_ref[...], g, m_ref[...], v_ref[...])

    return pl.pallas_call(
        body, name="adamw_w_ada",
        out_shape=[jax.ShapeDtypeStruct(w.shape, F32)] * 4,
        compiler_params=_params(),
    )(c64, dmod64, w, m, v)


P_MOD, P_NORM, P_ONORM, P_RELB, P_LB, P_FINAL, P_LOSS, P_END = (0, 3 * D, 4 * D, 5 * D, 6 * D, 7 * D, 8 * D, 9 * D)


def _adamw_small(packed, b_ada, norm_g, onorm_g, relb, hgrn_lb, final_g, ms, vs):
    def body(pk_ref, b_ref, ng_ref, og_ref, rb_ref, lb_ref, fg_ref,
             mb, mn, mo, mr, ml, mf, vb, vn, vo, vr, vl, vf,
             loss_ref, gb, gn, go, gr, gl, gf, db, dn, do, dr, dl, df,
             nmb, nmn, nmo, nmr, nml, nmf, nvb, nvn, nvo, nvr, nvl, nvf):
        tot = pk_ref[0:1, :]
        for j in range(1, N_DEV):
            tot = tot + pk_ref[8 * j:8 * j + 1, :]
        loss_ref[...] = jnp.broadcast_to(jnp.sum(tot[:, P_LOSS:P_END], axis=-1, keepdims=True), (8, 128))

        def upd(g, w_ref, m_ref, v_ref, g_out, d_out, m_out, v_out):
            g_out[...] = g
            d_out[...], m_out[...], v_out[...] = _adamw_math(w_ref[...], g, m_ref[...], v_ref[...])

        upd(tot[:, P_MOD:P_NORM], b_ref, mb, vb, gb, db, nmb, nvb)
        upd(tot[:, P_NORM:P_ONORM], ng_ref, mn, vn, gn, dn, nmn, nvn)
        g_on = tot[:, P_ONORM:P_ONORM + G_DK]
        for h in range(1, G_HEADS):
            g_on = g_on + tot[:, P_ONORM + G_DK * h:P_ONORM + G_DK * (h + 1)]
        upd(g_on, og_ref, mo, vo, go, do, nmo, nvo)
        upd(tot[:, P_RELB:P_LB], rb_ref, mr, vr, gr, dr, nmr, nvr)
        a = lb_ref[...]
        lb = _sigmoid(a[0:1, :] - a[1:2, :])
        g0 = tot[:, P_LB:P_FINAL] * lb * (1.0 - lb)
        row = lax.broadcasted_iota(jnp.int32, (2, D), 0)
        upd(jnp.where(row == 0, g0, -g0), lb_ref, ml, vl, gl, dl, nml, nvl)
        upd(tot[:, P_FINAL:P_LOSS], fg_ref, mf, vf, gf, df, nmf, nvf)

    shapes = [b_ada.shape, norm_g.shape, onorm_g.shape, relb.shape, hgrn_lb.shape, final_g.shape]
    outs = [jax.ShapeDtypeStruct((8, 128), F32)] + [jax.ShapeDtypeStruct(s, F32) for s in shapes] * 4
    return pl.pallas_call(
        body, name="adamw_small",
        out_shape=outs,
        compiler_params=_params(),
    )(packed, b_ada, norm_g, onorm_g, relb, hgrn_lb, final_g, *ms, *vs)


def _local_step(x, target, mod3, norm_g, w_in_g, onorm_g, wa, wb, wo, rel_bias, hgrn_lb, final_g):
    buckets = jnp.asarray(_bucket_tables())
    bias = _bias_tables(rel_bias, buckets)
    proj, h = _inproj(x, mod3, norm_g, w_in_g)
    os, ls = [], []
    for p, (_, d) in enumerate(PATTERNS):
        o, l = _attn_fwd(proj, bias[p], d, "attn_fwd_d%d" % d)
        os.append(o)
        ls.append(l)
    ao, lt, oa = _attn_combine(os, ls, proj)
    o_raw, ob, states = _hgrn_fwd(proj, hgrn_lb, onorm_g)
    dx2, doa, dob, dga, dgb, tsums, gwa, gwb, gwo = _tail(x, target, oa, ob, proj, mod3, final_g, wa, wb, wo)
    do, dza, delta = _attn_pre_bwd(doa, ao, proj)
    dqs, dks, dvs, dbs = [], [], [], []
    for p, (_, d) in enumerate(PATTERNS):
        dq, dk, dv, db = _attn_bwd(proj, do, lt, delta, bias[p], d, "attn_bwd_d%d" % d)
        dqs.append(dq)
        dks.append(dk)
        dvs.append(dv)
        dbs.append(db)
    p0, p1 = _attn_assemble(dqs, dks, dvs, dza)
    g_relb = _rel_bias_grad(dbs, buckets)
    dqb, dfb, dib, dzb, dlb, dgo = _hgrn_bwd(proj, o_raw, dob, states, hgrn_lb, onorm_g)
    pieces = [p0, p1, dqb, dfb, dib, dzb, dga, dgb]
    gw_in = _gw_in(h, pieces)
    gx, hsums = _dh(pieces, w_in_g, x, dx2, mod3, norm_g)
    row = jnp.concatenate([
        hsums[0], hsums[1], tsums[1],
        hsums[2],
        dgo.reshape(G_HEADS, 8, G_DK)[:, 0].reshape(-1),
        g_relb.reshape(-1),
        dlb[0],
        tsums[0],
        tsums[2],
    ])
    return gx, gw_in, gwa, gwb, gwo, row


def kernel(x, c, w_ada, b_ada, norm_g, w_in, hgrn_onorm_g, w_branch_a, w_branch_b, w_out, rel_bias, hgrn_lb, final_g, loss_target, m_w_ada, m_b_ada, m_norm_g, m_w_in, m_hgrn_onorm_g, m_w_branch_a, m_w_branch_b, m_w_out, m_rel_bias, m_hgrn_lb, m_final_g, v_w_ada, v_b_ada, v_norm_g, v_w_in, v_hgrn_onorm_g, v_w_branch_a, v_w_branch_b, v_w_out, v_rel_bias, v_hgrn_lb, v_final_g):
    me = 4 * lax.axis_index("x") + 2 * lax.axis_index("y") + lax.axis_index("c")
    n_ada = w_ada.shape[2]

    w_in_g, wa_g, wb_g, wo_g = _all_gather(
        [w_in[0].astype(MXU_DTYPE), w_branch_a[0].astype(MXU_DTYPE),
         w_branch_b[0].astype(MXU_DTYPE), w_out[0].astype(MXU_DTYPE)], "gather_weights")
    wa = wa_g.transpose(1, 0, 2).reshape(A_W, D)
    wb = wb_g.reshape(D, D)
    wo = wo_g.reshape(D, D)

    (c_all,) = _all_gather([jnp.broadcast_to(c, (8, D))], "gather_c")
    c64 = c_all.reshape(8 * N_DEV, D)
    b_loc = lax.dynamic_slice(b_ada, (0, me * n_ada), (1, n_ada))
    mod_part = _mod_fwd(c64, w_ada[0], b_loc)[::8]
    (mod_all,) = _all_gather([mod_part], "gather_mod")
    mod = lax.dynamic_slice(mod_all, (0, me, 0), (N_DEV, 1, n_ada)).reshape(3, D)
    mod3 = jnp.concatenate([mod, jnp.zeros((5, D), F32)], axis=0)

    onorm_t = hgrn_onorm_g
    gx, gw_in, gwa, gwb, gwo, row = _local_step(
        x[0], loss_target[0], mod3, norm_g, w_in_g, onorm_t, wa, wb, wo, rel_bias, hgrn_lb,
        final_g.reshape(1, D))

    r_in, r_a, r_b, r_o = _all_to_all(
        [gw_in, gwa.astype(XCHG_DTYPE).reshape(A_W, N_DEV, D // N_DEV).transpose(1, 0, 2),
         gwb.astype(XCHG_DTYPE).reshape(N_DEV, D // N_DEV, D),
         gwo.astype(XCHG_DTYPE).reshape(N_DEV, D // N_DEV, D)], "scatter_grads")
    packed8 = jnp.concatenate([row[None, :], jnp.zeros((7, P_END), F32)], axis=0)
    (packed,) = _all_gather([packed8], "gather_small")
    packed = packed.reshape(8 * N_DEV, P_END)

    g_in, d_in, nm_in, nv_in = _adamw_big(r_in, w_in[0], m_w_in[0], v_w_in[0], "adamw_w_in")
    g_a, d_a, nm_a, nv_a = _adamw_big(r_a, w_branch_a[0], m_w_branch_a[0], v_w_branch_a[0], "adamw_w_branch_a")
    g_b, d_b, nm_b, nv_b = _adamw_big(r_b, w_branch_b[0], m_w_branch_b[0], v_w_branch_b[0], "adamw_w_branch_b")
    g_o, d_o, nm_o, nv_o = _adamw_big(r_o, w_out[0], m_w_out[0], v_w_out[0], "adamw_w_out")

    dmod64 = lax.dynamic_slice(packed, (0, P_MOD + me * n_ada), (8 * N_DEV, n_ada))
    g_ada, d_ada, nm_ada, nv_ada = _adamw_w_ada(c64, dmod64, w_ada[0], m_w_ada[0], v_w_ada[0])

    def flat_relb(t):
        return jnp.pad(t.T, ((0, 0), (0, 128 - N_BUCKETS))).reshape(1, A_HEADS * 128)

    def unflat_relb(t):
        return t.reshape(A_HEADS, 128)[:, :N_BUCKETS].T

    fg2 = lambda t: t.reshape(1, D)
    smalls = _adamw_small(
        packed, b_ada, norm_g, hgrn_onorm_g, flat_relb(rel_bias), hgrn_lb, fg2(final_g),
        [m_b_ada, m_norm_g, m_hgrn_onorm_g, flat_relb(m_rel_bias), m_hgrn_lb, fg2(m_final_g)],
        [v_b_ada, v_norm_g, v_hgrn_onorm_g, flat_relb(v_rel_bias), v_hgrn_lb, fg2(v_final_g)])
    loss = smalls[0][0, 0]

    def small(kind):
        s = smalls[1 + 6 * kind:7 + 6 * kind]
        return s[0], s[1], s[2], unflat_relb(s[3]), s[4], s[5].reshape(D)

    def leaves(ada, sm, w_in_, wa_, wb_, wo_):
        b_, n_, o_, r_, l_, f_ = sm
        return (ada[None], b_, n_, w_in_[None], o_, wa_[None], wb_[None], wo_[None], r_, l_, f_)

    return (loss, gx[None],
            *leaves(g_ada, small(0), g_in, g_a, g_b, g_o),
            *leaves(d_ada, small(1), d_in, d_a, d_b, d_o),
            *leaves(nm_ada, small(2), nm_in, nm_a, nm_b, nm_o),
            *leaves(nv_ada, small(3), nv_in, nv_a, nv_b, nv_o))
```

```python
import functools
import math

import numpy as np
import jax
import jax.numpy as jnp
from jax import lax
from jax.experimental import pallas as pl
from jax.experimental.pallas import tpu as pltpu

F32 = jnp.float32
BF16 = jnp.bfloat16
MXU_DTYPE = jnp.bfloat16
XCHG_DTYPE = jnp.bfloat16

N_DEV = 8
D = 1024
A_HEADS = 8
A_HD = 64
A_W = A_HEADS * A_HD
A_BLK = 128
PATTERNS = ((128, 1), (512, 4), (2048, 16))
N_BUCKETS = 32
MAX_DISTANCE = 2048
NEG = -1e30
G_HEADS = 8
G_DK = 128
G_W = G_HEADS * G_DK
IN_W = 8 * D
EPS = 1e-6
ADAM_LR = 0.001
ADAM_B1 = 0.9
ADAM_B2 = 0.999
ADAM_EPS = 1e-08
ADAM_WD = 0.01
ADAM_STEP = 10

G_CHUNK = 128
G_SUB = 16
VMEM_LIMIT = 56 * 1024 * 1024

NN = (((1,), (0,)), ((), ()))
NT = (((1,), (1,)), ((), ()))
TN = (((0,), (0,)), ((), ()))
MESH = pl.DeviceIdType.MESH


def _mm(a, b, dims=NN):
    return lax.dot_general(a.astype(MXU_DTYPE), b.astype(MXU_DTYPE), dims,
                           preferred_element_type=F32)


def _mm_exact(a, b):
    return lax.dot_general(a, b, NN, precision=lax.Precision.HIGHEST,
                           preferred_element_type=F32)


def _sigmoid(x):
    return 1.0 / (1.0 + jnp.exp(-x))


def _params(sem=None):
    return pltpu.CompilerParams(dimension_semantics=sem, vmem_limit_bytes=VMEM_LIMIT)


def _all_gather(xs, name):
    n = len(xs)

    def body(*refs):
        ins, outs = refs[:n], refs[n:2 * n]
        send_sems, recv_sems, local_sems = refs[2 * n:]
        x, y, c = lax.axis_index("x"), lax.axis_index("y"), lax.axis_index("c")
        me, sibling = (x, y, c), (x, y, 1 - c)
        chips = [(1 - x, y), (x, 1 - y), (1 - x, 1 - y)]

        def slot(ref, dev):
            return ref.at[4 * dev[0] + 2 * dev[1] + dev[2]]

        def copy(a, k, block, to, src=None):
            return pltpu.make_async_remote_copy(
                src_ref=slot(outs[a], block) if src is None else src,
                dst_ref=slot(outs[a], block),
                send_sem=send_sems.at[a, k], recv_sem=recv_sems.at[a, k],
                device_id=to, device_id_type=MESH)

        mine, first, passed = [], [], []
        for a in range(n):
            cp = pltpu.make_async_copy(ins[a], slot(outs[a], me), local_sems.at[a])
            cp.start()
            mine.append(cp)
            first.append(copy(a, 0, me, sibling, src=ins[a]))
            for j, chip in enumerate(chips):
                first.append(copy(a, 1 + j, me, (*chip, c), src=ins[a]))
        for cp in first:
            cp.start()
        for j, chip in enumerate(chips):
            for a in range(n):
                copy(a, 1 + j, (*chip, c), me).wait_recv()
                cp = copy(a, 4 + j, (*chip, c), sibling)
                cp.start()
                passed.append(cp)
        for a in range(n):
            copy(a, 0, sibling, me).wait_recv()
            for j, chip in enumerate(chips):
                copy(a, 4 + j, (*chip, 1 - c), me).wait_recv()
        for cp in first + passed:
            cp.wait_send()
        for cp in mine:
            cp.wait()

    any_spec = pl.BlockSpec(memory_space=pl.ANY)
    return pl.pallas_call(
        body, name=name,
        out_shape=[jax.ShapeDtypeStruct((N_DEV,) + v.shape, v.dtype) for v in xs],
        in_specs=[any_spec] * n, out_specs=[any_spec] * n,
        scratch_shapes=[pltpu.SemaphoreType.DMA((n, 7)), pltpu.SemaphoreType.DMA((n, 7)),
                        pltpu.SemaphoreType.DMA((n,))],
    )(*xs)


def _all_to_all_copies(ins, outs, send_sems, recv_sems, local_sems):
    n = len(ins)
    x, y, c = lax.axis_index("x"), lax.axis_index("y"), lax.axis_index("c")
    me = 4 * x + 2 * y + c
    peers = []
    for m in range(1, N_DEV):
        peers.append((1 - x if m & 4 else x, 1 - y if m & 2 else y, 1 - c if m & 1 else c))

    def copy(a, k, landing):
        peer = peers[k]
        pid = 4 * peer[0] + 2 * peer[1] + peer[2]
        return pltpu.make_async_remote_copy(
            src_ref=ins[a].at[pid], dst_ref=outs[a].at[pid if landing else me],
            send_sem=send_sems.at[a, k], recv_sem=recv_sems.at[a, k],
            device_id=peer, device_id_type=MESH)

    def local(a):
        return pltpu.make_async_copy(ins[a].at[me], outs[a].at[me], local_sems.at[a])

    def start():
        for a in range(n):
            local(a).start()
        for k in range(N_DEV - 1):
            for a in range(n):
                copy(a, k, False).start()

    def wait():
        for k in range(N_DEV - 1):
            for a in range(n):
                copy(a, k, True).wait_recv()
        for k in range(N_DEV - 1):
            for a in range(n):
                copy(a, k, False).wait_send()
        for a in range(n):
            local(a).wait()

    return start, wait


def _mod_fwd(c64, w_ada, b_loc):
    def body(c_ref, w_ref, b_ref, o_ref):
        cv = c_ref[...]
        sc = cv * _sigmoid(cv)
        o_ref[...] = _mm(sc, w_ref[...]) + b_ref[...]

    return pl.pallas_call(
        body, name="mod_fwd",
        out_shape=jax.ShapeDtypeStruct((c64.shape[0], w_ada.shape[1]), F32),
        compiler_params=_params(),
    )(c64, w_ada, b_loc)


def _inproj(x, mod3, norm_g, w_in_g, tm=1024):
    S = x.shape[0]

    def body(x_ref, mod_ref, g_ref, w_ref, proj_ref, h_ref, hs):
        @pl.when(pl.program_id(1) == 0)
        def _():
            xv = x_ref[...]
            r = lax.rsqrt(jnp.mean(xv * xv, axis=-1, keepdims=True) + EPS)
            h = (xv * r * g_ref[...]) * (1.0 + mod_ref[1:2, :]) + mod_ref[0:1, :]
            hs[...] = h.astype(MXU_DTYPE)
            h_ref[...] = h.astype(MXU_DTYPE)
        proj_ref[...] = _mm(hs[...], w_ref[0])

    return pl.pallas_call(
        body, name="inproj",
        grid=(S // tm, N_DEV),
        in_specs=[pl.BlockSpec((tm, D), lambda i, j: (i, 0)),
                  pl.BlockSpec((8, D), lambda i, j: (0, 0)),
                  pl.BlockSpec((1, D), lambda i, j: (0, 0)),
                  pl.BlockSpec((1, D, D), lambda i, j: (j, 0, 0))],
        out_specs=[pl.BlockSpec((tm, D), lambda i, j: (i, j)),
                   pl.BlockSpec((tm, D), lambda i, j: (i, 0))],
        out_shape=[jax.ShapeDtypeStruct((S, IN_W), F32), jax.ShapeDtypeStruct((S, D), MXU_DTYPE)],
        scratch_shapes=[pltpu.VMEM((tm, D), MXU_DTYPE)],
        compiler_params=_params(("parallel", "arbitrary")),
    )(x, mod3, norm_g, w_in_g)


def _bucket_tables():
    qi = np.arange(A_BLK)[:, None]
    kj = np.arange(2 * A_BLK)[None, :]
    delta = qi + A_BLK - kj
    out = []
    for window, dil in PATTERNS:
        span = window // dil
        band = (delta >= 0) & (delta <= span)
        dist = np.clip(delta, 0, None) * dil
        max_exact = N_BUCKETS // 2
        nf = dist.astype(np.float32)
        large = max_exact + (np.log(np.maximum(nf, np.float32(1.0)) / np.float32(max_exact))
                             / np.float32(math.log(MAX_DISTANCE / max_exact))
                             * np.float32(N_BUCKETS - max_exact)).astype(np.int32)
        large = np.minimum(large, N_BUCKETS - 1)
        bucket = np.where(dist < max_exact, dist, large)
        out.append(np.where(band, bucket, -1).astype(np.int32))
    return np.stack(out)


def _bias_tables(rel_bias, buckets):
    def body(rb_ref, bk_ref, o_ref):
        h = pl.program_id(1)
        bk = bk_ref[0]
        acc = jnp.full(bk.shape, NEG, F32)
        for b in range(N_BUCKETS):
            acc = jnp.where(bk == b, rb_ref[b, h], acc)
        o_ref[0, 0] = acc

    return pl.pallas_call(
        body, name="bias_tables",
        grid=(3, A_HEADS),
        in_specs=[pl.BlockSpec(memory_space=pltpu.SMEM),
                  pl.BlockSpec((1, A_BLK, 2 * A_BLK), lambda p, h: (p, 0, 0))],
        out_specs=pl.BlockSpec((1, 1, A_BLK, 2 * A_BLK), lambda p, h: (p, h, 0, 0)),
        out_shape=jax.ShapeDtypeStruct((3, A_HEADS, A_BLK, 2 * A_BLK), F32),
        compiler_params=_params(("arbitrary", "arbitrary")),
    )(rel_bias, buckets)


A_TILES = 8


def _attn_heads_per_step(d):
    return A_HEADS if d == 1 else 2


def _attn_in_specs(sb, nsb, hw):
    w = A_HD * hw
    per = A_W // w

    def cur(col):
        return pl.BlockSpec((sb, w), lambda hp, n: (jnp.minimum(n, nsb - 1), per * col + hp))

    def prev(col):
        return pl.BlockSpec((sb, w), lambda hp, n: (jnp.maximum(jnp.minimum(n, nsb - 1) - 1, 0), per * col + hp))

    return [cur(0), prev(1), cur(1), prev(2), cur(2)]


def _rows(r, d):
    return pl.ds(r, A_BLK) if d == 1 else pl.ds(r, A_BLK, stride=d)


def _for_residues(d, hw, fn):
    unroll = min(d, max(1, A_TILES // hw))
    if d == unroll:
        for r in range(d):
            fn(r)
    else:
        def group(g, c):
            for u in range(unroll):
                fn(g * unroll + u)
            return c
        lax.fori_loop(0, d // unroll, group, 0)


def _attn_scores(q, kp, kc, bias, hh, first):
    sl = slice(A_HD * hh, A_HD * (hh + 1))
    k = jnp.concatenate([kp[:, sl], kc[:, sl]], axis=0)
    s = _mm(q[:, sl], k, NT) * (A_HD ** -0.5) + bias + first
    return s, k


def _attn_fwd(proj, bias_p, d, name):
    S = proj.shape[0]
    sb = A_BLK * d
    nsb = S // sb
    hw = _attn_heads_per_step(d)

    def body(q_ref, kp_ref, kc_ref, vp_ref, vc_ref, b_ref, o_ref, l_ref):
        n = pl.program_id(1)
        kj = lax.broadcasted_iota(jnp.int32, (A_BLK, 2 * A_BLK), 1)
        first = jnp.where((n == 0) & (kj < A_BLK), NEG, 0.0).astype(F32)

        def residue(r):
            rows = _rows(r, d)
            for pp in range(hw // 2):
                lanes = pl.ds(2 * A_HD * pp, 2 * A_HD)
                q, kp, kc, vp, vc = (ref[rows, lanes] for ref in (q_ref, kp_ref, kc_ref, vp_ref, vc_ref))
                os, ls = [], []
                for hh in range(2):
                    sl = slice(A_HD * hh, A_HD * (hh + 1))
                    s, _ = _attn_scores(q, kp, kc, b_ref[2 * pp + hh], hh, first)
                    v = jnp.concatenate([vp[:, sl], vc[:, sl]], axis=0)
                    m = jnp.max(s, axis=-1, keepdims=True)
                    p = jnp.exp(s - m)
                    den = jnp.sum(p, axis=-1, keepdims=True)
                    os.append(_mm(p, v) / den)
                    ls.append(jnp.broadcast_to(m + jnp.log(den), (A_BLK, A_HD)))
                o_ref[rows, lanes] = jnp.concatenate(os, axis=1)
                l_ref[rows, lanes] = jnp.concatenate(ls, axis=1)

        _for_residues(d, hw, residue)

    out = pl.BlockSpec((sb, A_HD * hw), lambda hp, n: (n, hp))
    return pl.pallas_call(
        body, name=name,
        grid=(A_HEADS // hw, nsb),
        in_specs=_attn_in_specs(sb, nsb, hw) + [pl.BlockSpec((hw, A_BLK, 2 * A_BLK), lambda hp, n: (hp, 0, 0))],
        out_specs=[out, out],
        out_shape=[jax.ShapeDtypeStruct((S, A_W), F32)] * 2,
        compiler_params=_params(("parallel", "parallel")),
    )(proj, proj, proj, proj, proj, bias_p)


def _attn_combine(os, ls, proj, tm=512):
    S = proj.shape[0]

    def body(o1, o2, o3, l1, l2, l3, z_ref, ao_ref, lt_ref, oa_ref):
        a1, a2, a3 = l1[...], l2[...], l3[...]
        m = jnp.maximum(jnp.maximum(a1, a2), a3)
        e1, e2, e3 = jnp.exp(a1 - m), jnp.exp(a2 - m), jnp.exp(a3 - m)
        den = e1 + e2 + e3
        ao = (e1 * o1[...] + e2 * o2[...] + e3 * o3[...]) / den
        z = z_ref[...]
        ao_ref[...] = ao
        lt_ref[...] = m + jnp.log(den)
        oa_ref[...] = (ao * (z * _sigmoid(z))).astype(MXU_DTYPE)

    spec = pl.BlockSpec((tm, A_W), lambda i: (i, 0))
    return pl.pallas_call(
        body, name="attn_combine",
        grid=(S // tm,),
        in_specs=[spec] * 6 + [pl.BlockSpec((tm, A_W), lambda i: (i, 3))],
        out_specs=[spec] * 3,
        out_shape=[jax.ShapeDtypeStruct((S, A_W), F32), jax.ShapeDtypeStruct((S, A_W), F32),
                   jax.ShapeDtypeStruct((S, A_W), MXU_DTYPE)],
        compiler_params=_params(("parallel",)),
    )(*os, *ls, proj)


def _attn_pre_bwd(doa, ao, proj, tm=512):
    S = proj.shape[0]

    def body(doa_ref, ao_ref, z_ref, do_ref, dz_ref, dl_ref):
        z = z_ref[...]
        sg = _sigmoid(z)
        g = doa_ref[...]
        ao_v = ao_ref[...]
        do = g * (z * sg)
        do_ref[...] = do
        dz_ref[...] = (g * ao_v * (sg * (1.0 + z * (1.0 - sg)))).astype(MXU_DTYPE)
        prod = do * ao_v
        for h in range(A_HEADS):
            sl = slice(A_HD * h, A_HD * (h + 1))
            dl_ref[:, sl] = jnp.broadcast_to(jnp.sum(prod[:, sl], axis=-1, keepdims=True), (tm, A_HD))

    spec = pl.BlockSpec((tm, A_W), lambda i: (i, 0))
    return pl.pallas_call(
        body, name="attn_pre_bwd",
        grid=(S // tm,),
        in_specs=[spec, spec, pl.BlockSpec((tm, A_W), lambda i: (i, 3))],
        out_specs=[spec] * 3,
        out_shape=[jax.ShapeDtypeStruct((S, A_W), F32), jax.ShapeDtypeStruct((S, A_W), MXU_DTYPE),
                   jax.ShapeDtypeStruct((S, A_W), F32)],
        compiler_params=_params(("parallel",)),
    )(doa, ao, proj)


def _attn_bwd(proj, do, lt, delta, bias_p, d, name):
    S = proj.shape[0]
    sb = A_BLK * d
    nsb = S // sb
    hw = _attn_heads_per_step(d)

    def body(q_ref, kp_ref, kc_ref, vp_ref, vc_ref, do_ref, lt_ref, dl_ref, b_ref,
             dq_ref, dk_ref, dv_ref, db_ref, ck, cv):
        n = pl.program_id(1)

        @pl.when(n == 0)
        def _():
            db_ref[...] = jnp.zeros_like(db_ref)
            ck[...] = jnp.zeros_like(ck)
            cv[...] = jnp.zeros_like(cv)

        @pl.when(n < nsb)
        def _():
            kj = lax.broadcasted_iota(jnp.int32, (A_BLK, 2 * A_BLK), 1)
            first = jnp.where((n == 0) & (kj < A_BLK), NEG, 0.0).astype(F32)

            def residue(r):
                rows = _rows(r, d)
                for pp in range(hw // 2):
                    lanes = pl.ds(2 * A_HD * pp, 2 * A_HD)
                    q, kp, kc, vp, vc, do_r, lt_r, dl_r = (
                        ref[rows, lanes] for ref in (q_ref, kp_ref, kc_ref, vp_ref, vc_ref, do_ref, lt_ref, dl_ref))
                    dqs, dks, dvs = [], [], []
                    for hh in range(2):
                        sl = slice(A_HD * hh, A_HD * (hh + 1))
                        s, k = _attn_scores(q, kp, kc, b_ref[2 * pp + hh], hh, first)
                        v = jnp.concatenate([vp[:, sl], vc[:, sl]], axis=0)
                        p = jnp.exp(s - lt_r[:, A_HD * hh:A_HD * hh + 1])
                        do_h = do_r[:, sl]
                        ds = p * (_mm(do_h, v, NT) - dl_r[:, A_HD * hh:A_HD * hh + 1])
                        db_ref[2 * pp + hh] += ds
                        dqs.append(_mm(ds, k) * (A_HD ** -0.5))
                        dks.append(_mm(ds, q[:, sl], TN) * (A_HD ** -0.5))
                        dvs.append(_mm(p, do_h, TN))
                    dk = jnp.concatenate(dks, axis=1)
                    dv = jnp.concatenate(dvs, axis=1)
                    dq_ref[rows, lanes] = jnp.concatenate(dqs, axis=1)
                    dk_ref[rows, lanes] = ck[rows, lanes] + dk[:A_BLK]
                    dv_ref[rows, lanes] = cv[rows, lanes] + dv[:A_BLK]
                    ck[rows, lanes] = dk[A_BLK:]
                    cv[rows, lanes] = dv[A_BLK:]

            _for_residues(d, hw, residue)

        @pl.when(n == nsb)
        def _():
            dk_ref[...] = ck[...]
            dv_ref[...] = cv[...]

    w = A_HD * hw
    row = pl.BlockSpec((sb, w), lambda hp, n: (jnp.minimum(n, nsb - 1), hp))
    lag = pl.BlockSpec((sb, w), lambda hp, n: (jnp.maximum(n - 1, 0), hp))
    tab = pl.BlockSpec((hw, A_BLK, 2 * A_BLK), lambda hp, n: (hp, 0, 0))
    return pl.pallas_call(
        body, name=name,
        grid=(A_HEADS // hw, nsb + 1),
        in_specs=_attn_in_specs(sb, nsb, hw) + [row, row, row, tab],
        out_specs=[row, lag, lag, tab],
        out_shape=[jax.ShapeDtypeStruct((S, A_W), F32)] * 3
                  + [jax.ShapeDtypeStruct((A_HEADS, A_BLK, 2 * A_BLK), F32)],
        scratch_shapes=[pltpu.VMEM((sb, w), F32), pltpu.VMEM((sb, w), F32)],
        compiler_params=_params(("parallel", "arbitrary")),
    )(proj, proj, proj, proj, proj, do, lt, delta, bias_p)


def _attn_assemble(dqs, dks, dvs, dz, tm=512):
    S = dz.shape[0]

    def body(q1, q2, q3, k1, k2, k3, v1, v2, v3, z_ref, p0_ref, p1_ref):
        p0_ref[:, :A_W] = (q1[...] + q2[...] + q3[...]).astype(MXU_DTYPE)
        p0_ref[:, A_W:] = (k1[...] + k2[...] + k3[...]).astype(MXU_DTYPE)
        p1_ref[:, :A_W] = (v1[...] + v2[...] + v3[...]).astype(MXU_DTYPE)
        p1_ref[:, A_W:] = z_ref[...]

    spec = pl.BlockSpec((tm, A_W), lambda i: (i, 0))
    wide = pl.BlockSpec((tm, 2 * A_W), lambda i: (i, 0))
    return pl.pallas_call(
        body, name="attn_assemble",
        grid=(S // tm,),
        in_specs=[spec] * 10,
        out_specs=[wide, wide],
        out_shape=[jax.ShapeDtypeStruct((S, 2 * A_W), MXU_DTYPE)] * 2,
        compiler_params=_params(("parallel",)),
    )(*dqs, *dks, *dvs, dz)


def _rel_bias_grad(dbs, buckets):
    def body(d1, d2, d3, bk_ref, o_ref):
        row = lax.broadcasted_iota(jnp.int32, (A_HEADS, 128), 0)
        lane = lax.broadcasted_iota(jnp.int32, (A_HEADS, 128), 1)
        acc = jnp.zeros((A_HEADS, 128), F32)
        for p, dref in enumerate((d1, d2, d3)):
            bk = bk_ref[p]
            for h in range(A_HEADS):
                ds = dref[h]
                for b in range(N_BUCKETS):
                    s = jnp.sum(jnp.where(bk == b, ds, 0.0), keepdims=True)
                    acc = acc + jnp.where((row == h) & (lane == b), s, 0.0)
        o_ref[...] = acc

    return pl.pallas_call(
        body, name="rel_bias_grad",
        out_shape=jax.ShapeDtypeStruct((A_HEADS, 128), F32),
        compiler_params=_params(),
    )(*dbs, buckets)


def _tri(c):
    t = np.tril(np.ones((c, c), np.float32))
    return jnp.asarray(t), jnp.asarray(t.T.copy())


def _roll(x, shift):
    return x if shift == 0 else pltpu.roll(x, shift, 0)


def _hgrn_gates(q_ref, f_ref, lbp_ref, tri_ref):
    qraw = q_ref[...]
    sq = _sigmoid(qraw)
    q = qraw * sq
    sg = _sigmoid(f_ref[...])
    lb = _sigmoid(lbp_ref[0:1, :] - lbp_ref[1:2, :])
    f = lb + (1.0 - lb) * sg
    k = 1.0 - f
    b = _mm_exact(tri_ref[...], jnp.log(f))
    return qraw, sq, q, sg, lb, f, k, b


def _hgrn_col(C, base, idx):
    return pl.BlockSpec((C, G_DK), lambda h, n: (idx(n), base * G_HEADS + h))


def _hgrn_levels(C):
    out, m = [], G_SUB
    while 2 * m <= C:
        out.append(m)
        m *= 2
    return out


def _hgrn_level(b, C, m):
    zeros = jnp.zeros((m, G_DK), F32)
    refq, refk = [], []
    for blk in range(C // m):
        if blk % 2 == 1:
            refq.append(jnp.broadcast_to(b[blk * m:blk * m + 1], (m, G_DK)))
            refk.append(zeros)
        else:
            refq.append(zeros)
            refk.append(jnp.broadcast_to(b[(blk + 1) * m:(blk + 1) * m + 1], (m, G_DK)))
    refq = jnp.concatenate(refq, axis=0)
    refk = jnp.concatenate(refk, axis=0)
    right = (lax.broadcasted_iota(jnp.int32, (C, 1), 0) // m) % 2 == 1
    eq = jnp.exp(jnp.where(right, b - refq, NEG))
    ek = jnp.exp(jnp.where(right, NEG, refk - b))
    ti = lax.broadcasted_iota(jnp.int32, (C, C), 0)
    si = lax.broadcasted_iota(jnp.int32, (C, C), 1)
    return eq, ek, (ti // (2 * m) == si // (2 * m)) & (ti - si >= G_SUB)


def _hgrn_fwd(proj, hgrn_lb, onorm_g, C=G_CHUNK):
    S = proj.shape[0]
    nc = S // C
    tri, _ = _tri(C)

    def body(q_ref, f_ref, i_ref, z_ref, lbp_ref, go_ref, tri_ref, o_ref, ob_ref, st_ref, St):
        n = pl.program_id(1)

        @pl.when(n == 0)
        def _():
            St[...] = jnp.zeros_like(St)

        _, _, q, _, _, _, k, b = _hgrn_gates(q_ref, f_ref, lbp_ref, tri_ref)
        v = i_ref[...]
        bC = b[C - 1:C, :]
        S0 = St[...]
        o = _mm(q * jnp.exp(b), S0, NT)
        row = lax.broadcasted_iota(jnp.int32, (C, 1), 0)
        for l in range(G_SUB):
            e = jnp.exp(jnp.where(row >= l, b - _roll(b, l), NEG))
            a = jnp.sum(q * _roll(k, l) * e, axis=-1, keepdims=True)
            o = o + a * _roll(v, l)
        a_off = jnp.zeros((C, C), F32)
        for m in _hgrn_levels(C):
            eq, ek, pm = _hgrn_level(b, C, m)
            a_off = a_off + jnp.where(pm, _mm(q * eq, k * ek, NT), 0.0)
        o = o + _mm(a_off, v)
        S1 = S0 * jnp.exp(bC) + _mm(v, k * jnp.exp(bC - b), TN)
        St[...] = S1
        st_ref[0, 0] = S1
        o_ref[...] = o
        r = lax.rsqrt(jnp.mean(o * o, axis=-1, keepdims=True) + EPS)
        z = z_ref[...]
        ob_ref[...] = (o * r * go_ref[...] * (z * _sigmoid(z))).astype(MXU_DTYPE)

    ident = lambda n: n
    out = pl.BlockSpec((C, G_DK), lambda h, n: (n, h))
    return pl.pallas_call(
        body, name="hgrn_fwd",
        grid=(G_HEADS, nc),
        in_specs=[_hgrn_col(C, 2, ident), _hgrn_col(C, 3, ident), _hgrn_col(C, 4, ident), _hgrn_col(C, 5, ident),
                  pl.BlockSpec((2, G_DK), lambda h, n: (0, h)),
                  pl.BlockSpec((1, G_DK), lambda h, n: (0, 0)),
                  pl.BlockSpec((C, C), lambda h, n: (0, 0))],
        out_specs=[out, out, pl.BlockSpec((1, 1, G_DK, G_DK), lambda h, n: (n, h, 0, 0))],
        out_shape=[jax.ShapeDtypeStruct((S, G_W), F32), jax.ShapeDtypeStruct((S, G_W), MXU_DTYPE),
                   jax.ShapeDtypeStruct((nc, G_HEADS, G_DK, G_DK), F32)],
        scratch_shapes=[pltpu.VMEM((G_DK, G_DK), F32)],
        compiler_params=_params(("parallel", "arbitrary")),
    )(proj, proj, proj, proj, hgrn_lb, onorm_g, tri)


def _hgrn_bwd(proj, o_raw, dob, states, hgrn_lb, onorm_g, C=G_CHUNK):
    S = proj.shape[0]
    nc = S // C
    tri, triu = _tri(C)

    def body(q_ref, f_ref, i_ref, z_ref, o_ref, dob_ref, s0_ref, s1_ref, lbp_ref, go_ref, tri_ref, triu_ref,
             dq_ref, df_ref, di_ref, dz_ref, dlb_ref, dgo_ref, dSt):
        n = pl.program_id(1)
        cn = nc - 1 - n

        @pl.when(n == 0)
        def _():
            dSt[...] = jnp.zeros_like(dSt)
            dlb_ref[...] = jnp.zeros_like(dlb_ref)
            dgo_ref[...] = jnp.zeros_like(dgo_ref)

        qraw, sq, q, sg, lb, f, k, b = _hgrn_gates(q_ref, f_ref, lbp_ref, tri_ref)
        v = i_ref[...]
        bC = b[C - 1:C, :]
        eb = jnp.exp(b)
        ecb = jnp.exp(bC - b)
        o = o_ref[...]
        z = z_ref[...]
        sz = _sigmoid(z)
        go = go_ref[...]
        g_ob = dob_ref[...]
        r = lax.rsqrt(jnp.mean(o * o, axis=-1, keepdims=True) + EPS)
        nh = o * r
        dnrm = g_ob * (z * sz)
        dz_ref[...] = (g_ob * (nh * go) * (sz * (1.0 + z * (1.0 - sz)))).astype(MXU_DTYPE)
        dgo_ref[0:1, :] += jnp.sum(dnrm * nh, axis=0, keepdims=True)
        dn = dnrm * go
        do = r * (dn - nh * jnp.mean(dn * nh, axis=-1, keepdims=True))

        S0 = jnp.where(cn == 0, 0.0, s0_ref[0, 0])
        S1 = s1_ref[0, 0]
        dS1 = dSt[...]
        dq = eb * _mm(do, S0)
        dk = ecb * _mm(v, dS1)
        dv = _mm(k * ecb, dS1, NT)
        bnd = jnp.sum(dS1 * S1, axis=0, keepdims=True)
        dSt[...] = dS1 * jnp.exp(bC) + _mm(do, q * eb, TN)

        row = lax.broadcasted_iota(jnp.int32, (C, 1), 0)
        for l in range(G_SUB):
            e = jnp.exp(jnp.where(row >= l, b - _roll(b, l), NEG))
            da =jnp.sum(do * _roll(v, l), axis=-1, keepdims=True)
            dq = dq + da * (_roll(k, l) * e)
            up = (C - l) % C
            e2 = jnp.exp(jnp.where(row + l < C, _roll(b, up) - b, NEG))
            qe = _roll(q, up) * e2
            dou = _roll(do, up)
            da2 = jnp.sum(dou * v, axis=-1, keepdims=True)
            a2 = jnp.sum(qe * k, axis=-1, keepdims=True)
            dk = dk + da2 * qe
            dv = dv + a2 * dou

        da_all = _mm(do, v, NT)
        a_off = jnp.zeros((C, C), F32)
        for m in _hgrn_levels(C):
            eq, ek, pm = _hgrn_level(b, C, m)
            qt, kt = q * eq, k * ek
            da_m = jnp.where(pm, da_all, 0.0)
            a_off = a_off + jnp.where(pm, _mm(qt, kt, NT), 0.0)
            dq = dq + _mm(da_m, kt) * eq
            dk = dk + _mm(da_m, qt, TN) * ek
        dv = dv + _mm(a_off, do, TN)

        row = lax.broadcasted_iota(jnp.int32, (C, 1), 0)
        db = q * dq - k * dk + jnp.where(row == C - 1, bnd, 0.0)
        dg = _mm_exact(triu_ref[...], db)
        df = dg / f - dk
        df_ref[...] = (df * (1.0 - lb) * (sg * (1.0 - sg))).astype(MXU_DTYPE)
        dlb_ref[0:1, :] += jnp.sum(df * (1.0 - sg), axis=0, keepdims=True)
        dq_ref[...] = (dq * (sq * (1.0 + qraw * (1.0 - sq)))).astype(MXU_DTYPE)
        di_ref[...] = dv.astype(MXU_DTYPE)

    rev = lambda n: nc - 1 - n
    blk = pl.BlockSpec((C, G_DK), lambda h, n: (nc - 1 - n, h))
    return pl.pallas_call(
        body, name="hgrn_bwd",
        grid=(G_HEADS, nc),
        in_specs=[_hgrn_col(C, 2, rev), _hgrn_col(C, 3, rev), _hgrn_col(C, 4, rev), _hgrn_col(C, 5, rev), blk, blk,
                  pl.BlockSpec((1, 1, G_DK, G_DK), lambda h, n: (jnp.maximum(nc - 2 - n, 0), h, 0, 0)),
                  pl.BlockSpec((1, 1, G_DK, G_DK), lambda h, n: (nc - 1 - n, h, 0, 0)),
                  pl.BlockSpec((2, G_DK), lambda h, n: (0, h)),
                  pl.BlockSpec((1, G_DK), lambda h, n: (0, 0)),
                  pl.BlockSpec((C, C), lambda h, n: (0, 0)),
                  pl.BlockSpec((C, C), lambda h, n: (0, 0))],
        out_specs=[blk, blk, blk, blk,
                   pl.BlockSpec((8, G_DK), lambda h, n: (0, h)),
                   pl.BlockSpec((8, G_DK), lambda h, n: (h, 0))],
        out_shape=[jax.ShapeDtypeStruct((S, G_W), MXU_DTYPE)] * 4
                  + [jax.ShapeDtypeStruct((8, G_W), F32), jax.ShapeDtypeStruct((8 * G_HEADS, G_DK), F32)],
        scratch_shapes=[pltpu.VMEM((G_DK, G_DK), F32)],
        compiler_params=_params(("parallel", "arbitrary")),
    )(proj, proj, proj, proj, o_raw, dob, states, states, hgrn_lb, onorm_g, tri, triu)


def _tail(x, target, oa, ob, proj, mod3, final_g, wa, wb, wo, tm=256):
    S = x.shape[0]
    nt = S // tm

    def body(x_ref, t_ref, oa_ref, ob_ref, ga_ref, gb_ref, mod_ref, fg_ref, wa_ref, wb_ref, wo_ref,
             dx2_ref, doa_ref, dob_ref, dga_ref, dgb_ref, sums_ref, gwa_ref, gwb_ref, gwo_ref,
             acc_a, acc_b, acc_o):
        i = pl.program_id(0)

        @pl.when(i == 0)
        def _():
            sums_ref[...] = jnp.zeros_like(sums_ref)
            acc_a[...] = jnp.zeros_like(acc_a)
            acc_b[...] = jnp.zeros_like(acc_b)
            acc_o[...] = jnp.zeros_like(acc_o)

        oa_v, ob_v = oa_ref[...], ob_ref[...]
        pa = _mm(oa_v, wa_ref[...])
        pb = _mm(ob_v, wb_ref[...])
        sa, sb = _sigmoid(ga_ref[...]), _sigmoid(gb_ref[...])
        ym = sa * pa + sb * pb
        u = _mm(ym, wo_ref[...])
        gate = mod_ref[2:3, :]
        fg = fg_ref[...]
        x2 = x_ref[...] + gate * u
        r2 = lax.rsqrt(jnp.mean(x2 * x2, axis=-1, keepdims=True) + EPS)
        xn2 = x2 * r2
        e = xn2 * fg - t_ref[...]
        dy = e * (1.0 / D)
        dn = dy * fg
        dx2 = r2 * (dn - xn2 * jnp.mean(dn * xn2, axis=-1, keepdims=True))
        dx2_ref[...] = dx2
        sums_ref[0:1, :] += jnp.sum(dy * xn2, axis=0, keepdims=True)
        sums_ref[1:2, :] += jnp.sum(dx2 * u, axis=0, keepdims=True)
        sums_ref[2:3, :] += (0.5 / D) * jnp.sum(e * e, axis=0, keepdims=True)
        du = dx2 * gate
        dym = _mm(du, wo_ref[...], NT)
        acc_o[...] += _mm(ym, du, TN)
        dpa, dpb = dym * sa, dym * sb
        dga_ref[...] = (dym * pa * (sa * (1.0 - sa))).astype(MXU_DTYPE)
        dgb_ref[...] = (dym * pb * (sb * (1.0 - sb))).astype(MXU_DTYPE)
        doa_ref[...] = _mm(dpa, wa_ref[...], NT)
        dob_ref[...] = _mm(dpb, wb_ref[...], NT)
        acc_a[...] += _mm(oa_v, dpa, TN)
        acc_b[...] += _mm(ob_v, dpb, TN)

        @pl.when(i == nt - 1)
        def _():
            pltpu.sync_copy(acc_a, gwa_ref)
            pltpu.sync_copy(acc_b, gwb_ref)
            pltpu.sync_copy(acc_o, gwo_ref)

    row = lambda w: pl.BlockSpec((tm, w), lambda i: (i, 0))
    full = lambda a, b: pl.BlockSpec((a, b), lambda i: (0, 0))
    any_spec = pl.BlockSpec(memory_space=pl.ANY)
    return pl.pallas_call(
        body, name="tail",
        grid=(nt,),
        in_specs=[row(D), row(D), row(A_W), row(D),
                  pl.BlockSpec((tm, D), lambda i: (i, 6)), pl.BlockSpec((tm, D), lambda i: (i, 7)),
                  full(8, D), full(1, D), full(A_W, D), full(D, D), full(D, D)],
        out_specs=[row(D), row(A_W), row(D), row(D), row(D), full(8, D), any_spec, any_spec, any_spec],
        out_shape=[jax.ShapeDtypeStruct((S, D), F32), jax.ShapeDtypeStruct((S, A_W), F32),
                   jax.ShapeDtypeStruct((S, D), F32), jax.ShapeDtypeStruct((S, D), MXU_DTYPE),
                   jax.ShapeDtypeStruct((S, D), MXU_DTYPE), jax.ShapeDtypeStruct((8, D), F32),
                   jax.ShapeDtypeStruct((A_W, D), F32), jax.ShapeDtypeStruct((D, D), F32),
                   jax.ShapeDtypeStruct((D, D), F32)],
        scratch_shapes=[pltpu.VMEM((A_W, D), F32), pltpu.VMEM((D, D), F32), pltpu.VMEM((D, D), F32)],
        compiler_params=_params(("arbitrary",)),
    )(x, target, oa, ob, proj, proj, mod3, final_g, wa, wb, wo)


def _dh(pieces, w_in_g, x, dx2, mod3, norm_g, grads, tm=512):
    S = x.shape[0]
    ni = S // tm
    ng = len(grads)

    def body(*refs):
        p_refs = refs[:N_DEV]
        w_ref, x_ref, dx2_ref, mod_ref, g_ref = refs[N_DEV:N_DEV + 5]
        g_ins = refs[N_DEV + 5:N_DEV + 5 + ng]
        gx_ref, sums_ref = refs[N_DEV + 5 + ng:N_DEV + 7 + ng]
        g_outs = refs[N_DEV + 7 + ng:N_DEV + 7 + 2 * ng]
        acc, send_sems, recv_sems, local_sems = refs[N_DEV + 7 + 2 * ng:]
        i, j = pl.program_id(0), pl.program_id(1)
        start, wait = _all_to_all_copies(g_ins, g_outs, send_sems, recv_sems, local_sems)

        @pl.when((i == 0) & (j == 0))
        def _():
            start()
            sums_ref[...] = jnp.zeros_like(sums_ref)

        @pl.when(j == 0)
        def _():
            acc[...] = jnp.zeros_like(acc)

        for k in range(N_DEV):
            @pl.when(j == k)
            def _(k=k):
                acc[...] += _mm(p_refs[k][...], w_ref[0], NT)

        @pl.when(j == N_DEV - 1)
        def _():
            dh = acc[...]
            xv = x_ref[...]
            g = g_ref[...]
            sc1 = 1.0 + mod_ref[1:2, :]
            r = lax.rsqrt(jnp.mean(xv * xv, axis=-1, keepdims=True) + EPS)
            xn = xv * r
            sums_ref[0:1, :] += jnp.sum(dh, axis=0, keepdims=True)
            sums_ref[1:2, :] += jnp.sum(dh * (xn * g), axis=0, keepdims=True)
            sums_ref[2:3, :] += jnp.sum(dh * sc1 * xn, axis=0, keepdims=True)
            dxn = dh * sc1 * g
            gx_ref[...] = dx2_ref[...] + r * (dxn - xn * jnp.mean(dxn * xn, axis=-1, keepdims=True))

        @pl.when((i == ni - 1) & (j == N_DEV - 1))
        def _():
            wait()

    row = pl.BlockSpec((tm, D), lambda i, j: (i, 0))
    any_spec = pl.BlockSpec(memory_space=pl.ANY)
    return pl.pallas_call(
        body, name="dh_scatter",
        grid=(ni, N_DEV),
        in_specs=[row] * N_DEV
                 + [pl.BlockSpec((1, D, D), lambda i, j: (j, 0, 0)),
                    row, row,
                    pl.BlockSpec((8, D), lambda i, j: (0, 0)),
                    pl.BlockSpec((1, D), lambda i, j: (0, 0))]
                 + [any_spec] * ng,
        out_specs=[row, pl.BlockSpec((8, D), lambda i, j: (0, 0))] + [any_spec] * ng,
        out_shape=[jax.ShapeDtypeStruct((S, D), F32), jax.ShapeDtypeStruct((8, D), F32)]
                  + [jax.ShapeDtypeStruct(g.shape, g.dtype) for g in grads],
        scratch_shapes=[pltpu.VMEM((tm, D), F32),
                        pltpu.SemaphoreType.DMA((ng, N_DEV - 1)), pltpu.SemaphoreType.DMA((ng, N_DEV - 1)),
                        pltpu.SemaphoreType.DMA((ng,))],
        compiler_params=_params(("arbitrary", "arbitrary")),
    )(*pieces, w_in_g, x, dx2, mod3, norm_g, *grads)


def _gw_in(h, pieces, tm=512):
    S = h.shape[0]
    nt = S // tm

    def body(*refs):
        h_ref, p_refs, o_ref, acc = refs[0], refs[1:1 + N_DEV], refs[1 + N_DEV], refs[2 + N_DEV]
        j, i = pl.program_id(0), pl.program_id(1)

        @pl.when(i == 0)
        def _():
            acc[...] = jnp.zeros_like(acc)

        for k in range(N_DEV):
            @pl.when(j == k)
            def _(k=k):
                acc[...] += _mm(h_ref[...], p_refs[k][...], TN)

        @pl.when(i == nt - 1)
        def _():
            o_ref[0] = acc[...].astype(XCHG_DTYPE)

    def piece(k):
        return pl.BlockSpec((tm, D), lambda j, i: (jnp.where(j == k, i, 0), 0))

    return pl.pallas_call(
        body, name="gw_in",
        grid=(N_DEV, nt),
        in_specs=[pl.BlockSpec((tm, D), lambda j, i: (i, 0))] + [piece(k) for k in range(N_DEV)],
        out_specs=pl.BlockSpec((1, D, D), lambda j, i: (j, 0, 0)),
        out_shape=jax.ShapeDtypeStruct((N_DEV, D, D), XCHG_DTYPE),
        scratch_shapes=[pltpu.VMEM((D, D), F32)],
        compiler_params=_params(("parallel", "arbitrary")),
    )(h, *pieces)


def _adamw_math(w, g, m, v):
    m = ADAM_B1 * m + (1.0 - ADAM_B1) * g
    v = ADAM_B2 * v + (1.0 - ADAM_B2) * (g * g)
    m_hat = m / (1.0 - ADAM_B1 ** ADAM_STEP)
    v_hat = v / (1.0 - ADAM_B2 ** ADAM_STEP)
    delta = -ADAM_LR * (m_hat / (jnp.sqrt(v_hat) + ADAM_EPS) + ADAM_WD * w)
    return delta, m, v


def _adamw_big(recv, w, m, v, name, tr=128):
    M, N = w.shape
    tr = min(tr, M)

    def body(r_ref, w_ref, m_ref, v_ref, g_ref, d_ref, nm_ref, nv_ref):
        g = r_ref[0].astype(F32)
        for j in range(1, N_DEV):
            g = g + r_ref[j].astype(F32)
        g_ref[...] = g
        d_ref[...], nm_ref[...], nv_ref[...] = _adamw_math(w_ref[...], g, m_ref[...], v_ref[...])

    blk = pl.BlockSpec((tr, N), lambda i: (i, 0))
    return pl.pallas_call(
        body, name=name,
        grid=(M // tr,),
        in_specs=[pl.BlockSpec((N_DEV, tr, N), lambda i: (0, i, 0)), blk, blk, blk],
        out_specs=[blk] * 4,
        out_shape=[jax.ShapeDtypeStruct((M, N), F32)] * 4,
        compiler_params=_params(("parallel",)),
    )(recv, w, m, v)


def _adamw_w_ada(c64, dmod64, w, m, v):
    def body(c_ref, dm_ref, w_ref, m_ref, v_ref, g_ref, d_ref, nm_ref, nv_ref):
        cv = c_ref[...]
        g = _mm(cv * _sigmoid(cv), dm_ref[...], TN)
        g_ref[...] = g
        d_ref[...], nm_ref[...], nv_ref[...] = _adamw_math(w_ref[...], g, m_ref[...], v_ref[...])

    return pl.pallas_call(
        body, name="adamw_w_ada",
        out_shape=[jax.ShapeDtypeStruct(w.shape, F32)] * 4,
        compiler_params=_params(),
    )(c64, dmod64, w, m, v)


P_MOD, P_NORM, P_ONORM, P_RELB, P_LB, P_FINAL, P_LOSS, P_END = (0, 3 * D, 4 * D, 5 * D, 6 * D, 7 * D, 8 * D, 9 * D)


def _adamw_small(packed, b_ada, norm_g, onorm_g, relb, hgrn_lb, final_g, ms, vs):
    def body(pk_ref, b_ref, ng_ref, og_ref, rb_ref, lb_ref, fg_ref,
             mb, mn, mo, mr, ml, mf, vb, vn, vo, vr, vl, vf,
             loss_ref, gb, gn, go, gr, gl, gf, db, dn, do, dr, dl, df,
             nmb, nmn, nmo, nmr, nml, nmf, nvb, nvn, nvo, nvr, nvl, nvf):
        tot = pk_ref[0:1, :]
        for j in range(1, N_DEV):
            tot = tot + pk_ref[8 * j:8 * j + 1, :]
        loss_ref[...] = jnp.broadcast_to(jnp.sum(tot[:, P_LOSS:P_END], axis=-1, keepdims=True), (8, 128))

        def upd(g, w_ref, m_ref, v_ref, g_out, d_out, m_out, v_out):
            g_out[...] = g
            d_out[...], m_out[...], v_out[...] = _adamw_math(w_ref[...], g, m_ref[...], v_ref[...])

        upd(tot[:, P_MOD:P_NORM], b_ref, mb, vb, gb, db, nmb, nvb)
        upd(tot[:, P_NORM:P_ONORM], ng_ref, mn, vn, gn, dn, nmn, nvn)
        g_on = tot[:, P_ONORM:P_ONORM + G_DK]
        for h in range(1, G_HEADS):
            g_on = g_on + tot[:, P_ONORM + G_DK * h:P_ONORM + G_DK * (h + 1)]
        upd(g_on, og_ref, mo, vo, go, do, nmo, nvo)
        upd(tot[:, P_RELB:P_LB], rb_ref, mr, vr, gr, dr, nmr, nvr)
        a = lb_ref[...]
        lb = _sigmoid(a[0:1, :] - a[1:2, :])
        g0 = tot[:, P_LB:P_FINAL] * lb * (1.0 - lb)
        row = lax.broadcasted_iota(jnp.int32, (2, D), 0)
        upd(jnp.where(row == 0, g0, -g0), lb_ref, ml, vl, gl, dl, nml, nvl)
        upd(tot[:, P_FINAL:P_LOSS], fg_ref, mf, vf, gf, df, nmf, nvf)

    shapes = [b_ada.shape, norm_g.shape, onorm_g.shape, relb.shape, hgrn_lb.shape, final_g.shape]
    outs = [jax.ShapeDtypeStruct((8, 128), F32)] + [jax.ShapeDtypeStruct(s, F32) for s in shapes] * 4
    return pl.pallas_call(
        body, name="adamw_small",
        out_shape=outs,
        compiler_params=_params(),
    )(packed, b_ada, norm_g, onorm_g, relb, hgrn_lb, final_g, *ms, *vs)


def _local_step(x, target, mod3, norm_g, w_in_g, onorm_g, wa, wb, wo, rel_bias, hgrn_lb, final_g):
    buckets = jnp.asarray(_bucket_tables())
    bias = _bias_tables(rel_bias, buckets)
    proj, h = _inproj(x, mod3, norm_g, w_in_g)
    os, ls = [], []
    for p, (_, d) in enumerate(PATTERNS):
        o, l = _attn_fwd(proj, bias[p], d, "attn_fwd_d%d" % d)
        os.append(o)
        ls.append(l)
    ao, lt, oa = _attn_combine(os, ls, proj)
    o_raw, ob, states = _hgrn_fwd(proj, hgrn_lb, onorm_g)
    dx2, doa, dob, dga, dgb, tsums, gwa, gwb, gwo = _tail(x, target, oa, ob, proj, mod3, final_g, wa, wb, wo)
    do, dza, delta = _attn_pre_bwd(doa, ao, proj)
    dqs, dks, dvs, dbs = [], [], [], []
    for p, (_, d) in enumerate(PATTERNS):
        dq, dk, dv, db = _attn_bwd(proj, do, lt, delta, bias[p], d, "attn_bwd_d%d" % d)
        dqs.append(dq)
        dks.append(dk)
        dvs.append(dv)
        dbs.append(db)
    p0, p1 = _attn_assemble(dqs, dks, dvs, dza)
    g_relb = _rel_bias_grad(dbs, buckets)
    dqb, dfb, dib, dzb, dlb, dgo = _hgrn_bwd(proj, o_raw, dob, states, hgrn_lb, onorm_g)
    pieces = [p0, p1, dqb, dfb, dib, dzb, dga, dgb]
    grads = [_gw_in(h, pieces),
             gwa.astype(XCHG_DTYPE).reshape(A_W, N_DEV, D // N_DEV).transpose(1, 0, 2),
             gwb.astype(XCHG_DTYPE).reshape(N_DEV, D // N_DEV, D),
             gwo.astype(XCHG_DTYPE).reshape(N_DEV, D // N_DEV, D)]
    gx, hsums, *received = _dh(pieces, w_in_g, x, dx2, mod3, norm_g, grads)
    row = jnp.concatenate([
        hsums[0], hsums[1], tsums[1],
        hsums[2],
        dgo.reshape(G_HEADS, 8, G_DK)[:, 0].reshape(-1),
        g_relb.reshape(-1),
        dlb[0],
        tsums[0],
        tsums[2],
    ])
    return gx, received, row


def kernel(x, c, w_ada, b_ada, norm_g, w_in, hgrn_onorm_g, w_branch_a, w_branch_b, w_out, rel_bias, hgrn_lb, final_g, loss_target, m_w_ada, m_b_ada, m_norm_g, m_w_in, m_hgrn_onorm_g, m_w_branch_a, m_w_branch_b, m_w_out, m_rel_bias, m_hgrn_lb, m_final_g, v_w_ada, v_b_ada, v_norm_g, v_w_in, v_hgrn_onorm_g, v_w_branch_a, v_w_branch_b, v_w_out, v_rel_bias, v_hgrn_lb, v_final_g):
    me = 4 * lax.axis_index("x") + 2 * lax.axis_index("y") + lax.axis_index("c")
    n_ada = w_ada.shape[2]

    w_in_g, wa_g, wb_g, wo_g = _all_gather(
        [w_in[0].astype(MXU_DTYPE), w_branch_a[0].astype(MXU_DTYPE),
         w_branch_b[0].astype(MXU_DTYPE), w_out[0].astype(MXU_DTYPE)], "gather_weights")
    wa = wa_g.transpose(1, 0, 2).reshape(A_W, D)
    wb = wb_g.reshape(D, D)
    wo = wo_g.reshape(D, D)

    (c_all,) = _all_gather([jnp.broadcast_to(c, (8, D))], "gather_c")
    c64 = c_all.reshape(8 * N_DEV, D)
    b_loc = lax.dynamic_slice(b_ada, (0, me * n_ada), (1, n_ada))
    mod_part = _mod_fwd(c64, w_ada[0], b_loc)[::8]
    (mod_all,) = _all_gather([mod_part], "gather_mod")
    mod = lax.dynamic_slice(mod_all, (0, me, 0), (N_DEV, 1, n_ada)).reshape(3, D)
    mod3 = jnp.concatenate([mod, jnp.zeros((5, D), F32)], axis=0)

    onorm_t = hgrn_onorm_g
    gx, (r_in, r_a, r_b, r_o), row = _local_step(
        x[0], loss_target[0], mod3, norm_g, w_in_g, onorm_t, wa, wb, wo, rel_bias, hgrn_lb,
        final_g.reshape(1, D))
    packed8 = jnp.concatenate([row[None, :], jnp.zeros((7, P_END), F32)], axis=0)
    (packed,) = _all_gather([packed8], "gather_small")
    packed = packed.reshape(8 * N_DEV, P_END)

    g_in, d_in, nm_in, nv_in = _adamw_big(r_in, w_in[0], m_w_in[0], v_w_in[0], "adamw_w_in")
    g_a, d_a, nm_a, nv_a = _adamw_big(r_a, w_branch_a[0], m_w_branch_a[0], v_w_branch_a[0], "adamw_w_branch_a")
    g_b, d_b, nm_b, nv_b = _adamw_big(r_b, w_branch_b[0], m_w_branch_b[0], v_w_branch_b[0], "adamw_w_branch_b")
    g_o, d_o, nm_o, nv_o = _adamw_big(r_o, w_out[0], m_w_out[0], v_w_out[0], "adamw_w_out")

    dmod64 = lax.dynamic_slice(packed, (0, P_MOD + me * n_ada), (8 * N_DEV, n_ada))
    g_ada, d_ada, nm_ada, nv_ada = _adamw_w_ada(c64, dmod64, w_ada[0], m_w_ada[0], v_w_ada[0])

    def flat_relb(t):
        return jnp.pad(t.T, ((0, 0), (0, 128 - N_BUCKETS))).reshape(1, A_HEADS * 128)

    def unflat_relb(t):
        return t.reshape(A_HEADS, 128)[:, :N_BUCKETS].T

    fg2 = lambda t: t.reshape(1, D)
    smalls = _adamw_small(
        packed, b_ada, norm_g, hgrn_onorm_g, flat_relb(rel_bias), hgrn_lb, fg2(final_g),
        [m_b_ada, m_norm_g, m_hgrn_onorm_g, flat_relb(m_rel_bias), m_hgrn_lb, fg2(m_final_g)],
        [v_b_ada, v_norm_g, v_hgrn_onorm_g, flat_relb(v_rel_bias), v_hgrn_lb, fg2(v_final_g)])
    loss = smalls[0][0, 0]

    def small(kind):
        s = smalls[1 + 6 * kind:7 + 6 * kind]
        return s[0], s[1], s[2], unflat_relb(s[3]), s[4], s[5].reshape(D)

    def leaves(ada, sm, w_in_, wa_, wb_, wo_):
        b_, n_, o_, r_, l_, f_ = sm
        return (ada[None], b_, n_, w_in_[None], o_, wa_[None], wb_[None], wo_[None], r_, l_, f_)

    return (loss, gx[None],
            *leaves(g_ada, small(0), g_in, g_a, g_b, g_o),
            *leaves(d_ada, small(1), d_in, d_a, d_b, d_o),
            *leaves(nm_ada, small(2), nm_in, nm_a, nm_b, nm_o),
            *leaves(nv_ada, small(3), nv_in, nv_a, nv_b, nv_o))
```

```python
import functools
import math

import numpy as np
import jax
import jax.numpy as jnp
from jax import lax
from jax.experimental import pallas as pl
from jax.experimental.pallas import tpu as pltpu

F32 = jnp.float32
BF16 = jnp.bfloat16
MXU_DTYPE = jnp.bfloat16
XCHG_DTYPE = jnp.bfloat16

N_DEV = 8
D = 1024
A_HEADS = 8
A_HD = 64
A_W = A_HEADS * A_HD
A_BLK = 128
PATTERNS = ((128, 1), (512, 4), (2048, 16))
N_BUCKETS = 32
MAX_DISTANCE = 2048
NEG = -1e30
G_HEADS = 8
G_DK = 128
G_W = G_HEADS * G_DK
IN_W = 8 * D
EPS = 1e-6
ADAM_LR = 0.001
ADAM_B1 = 0.9
ADAM_B2 = 0.999
ADAM_EPS = 1e-08
ADAM_WD = 0.01
ADAM_STEP = 10

G_CHUNK = 128
G_SUB = 16
G_HPS = 2
G_RB = 16
VMEM_LIMIT = 56 * 1024 * 1024

NN = (((1,), (0,)), ((), ()))
NT = (((1,), (1,)), ((), ()))
TN = (((0,), (0,)), ((), ()))
MESH = pl.DeviceIdType.MESH


def _mm(a, b, dims=NN):
    return lax.dot_general(a.astype(MXU_DTYPE), b.astype(MXU_DTYPE), dims,
                           preferred_element_type=F32)


def _mm_exact(a, b):
    return lax.dot_general(a, b, NN, precision=lax.Precision.HIGHEST,
                           preferred_element_type=F32)


def _sigmoid(x):
    return 1.0 / (1.0 + jnp.exp(-x))


def _params(sem=None):
    return pltpu.CompilerParams(dimension_semantics=sem, vmem_limit_bytes=VMEM_LIMIT)


def _all_gather(xs, name):
    n = len(xs)

    def body(*refs):
        ins, outs = refs[:n], refs[n:2 * n]
        send_sems, recv_sems, local_sems = refs[2 * n:]
        x, y, c = lax.axis_index("x"), lax.axis_index("y"), lax.axis_index("c")
        me, sibling = (x, y, c), (x, y, 1 - c)
        chips = [(1 - x, y), (x, 1 - y), (1 - x, 1 - y)]

        def slot(ref, dev):
            return ref.at[4 * dev[0] + 2 * dev[1] + dev[2]]

        def copy(a, k, block, to, src=None):
            return pltpu.make_async_remote_copy(
                src_ref=slot(outs[a], block) if src is None else src,
                dst_ref=slot(outs[a], block),
                send_sem=send_sems.at[a, k], recv_sem=recv_sems.at[a, k],
                device_id=to, device_id_type=MESH)

        mine, first, passed = [], [], []
        for a in range(n):
            cp = pltpu.make_async_copy(ins[a], slot(outs[a], me), local_sems.at[a])
            cp.start()
            mine.append(cp)
            first.append(copy(a, 0, me, sibling, src=ins[a]))
            for j, chip in enumerate(chips):
                first.append(copy(a, 1 + j, me, (*chip, c), src=ins[a]))
        for cp in first:
            cp.start()
        for j, chip in enumerate(chips):
            for a in range(n):
                copy(a, 1 + j, (*chip, c), me).wait_recv()
                cp = copy(a, 4 + j, (*chip, c), sibling)
                cp.start()
                passed.append(cp)
        for a in range(n):
            copy(a, 0, sibling, me).wait_recv()
            for j, chip in enumerate(chips):
                copy(a, 4 + j, (*chip, 1 - c), me).wait_recv()
        for cp in first + passed:
            cp.wait_send()
        for cp in mine:
            cp.wait()

    any_spec = pl.BlockSpec(memory_space=pl.ANY)
    return pl.pallas_call(
        body, name=name,
        out_shape=[jax.ShapeDtypeStruct((N_DEV,) + v.shape, v.dtype) for v in xs],
        in_specs=[any_spec] * n, out_specs=[any_spec] * n,
        scratch_shapes=[pltpu.SemaphoreType.DMA((n, 7)), pltpu.SemaphoreType.DMA((n, 7)),
                        pltpu.SemaphoreType.DMA((n,))],
    )(*xs)


def _all_to_all_copies(ins, outs, send_sems, recv_sems, local_sems):
    n = len(ins)
    x, y, c = lax.axis_index("x"), lax.axis_index("y"), lax.axis_index("c")
    me = 4 * x + 2 * y + c
    peers = []
    for m in range(1, N_DEV):
        peers.append((1 - x if m & 4 else x, 1 - y if m & 2 else y, 1 - c if m & 1 else c))

    def copy(a, k, landing):
        peer = peers[k]
        pid = 4 * peer[0] + 2 * peer[1] + peer[2]
        return pltpu.make_async_remote_copy(
            src_ref=ins[a].at[pid], dst_ref=outs[a].at[pid if landing else me],
            send_sem=send_sems.at[a, k], recv_sem=recv_sems.at[a, k],
            device_id=peer, device_id_type=MESH)

    def local(a):
        return pltpu.make_async_copy(ins[a].at[me], outs[a].at[me], local_sems.at[a])

    def start():
        for a in range(n):
            local(a).start()
        for k in range(N_DEV - 1):
            for a in range(n):
                copy(a, k, False).start()

    def wait():
        for k in range(N_DEV - 1):
            for a in range(n):
                copy(a, k, True).wait_recv()
        for k in range(N_DEV - 1):
            for a in range(n):
                copy(a, k, False).wait_send()
        for a in range(n):
            local(a).wait()

    return start, wait


def _mod_fwd(c64, w_ada, b_loc):
    def body(c_ref, w_ref, b_ref, o_ref):
        cv = c_ref[...]
        sc = cv * _sigmoid(cv)
        o_ref[...] = _mm(sc, w_ref[...]) + b_ref[...]

    return pl.pallas_call(
        body, name="mod_fwd",
        out_shape=jax.ShapeDtypeStruct((c64.shape[0], w_ada.shape[1]), F32),
        compiler_params=_params(),
    )(c64, w_ada, b_loc)


def _inproj(x, mod3, norm_g, w_in_g, tm=1024):
    S = x.shape[0]

    def body(x_ref, mod_ref, g_ref, w_ref, proj_ref, h_ref, hs):
        @pl.when(pl.program_id(1) == 0)
        def _():
            xv = x_ref[...]
            r = lax.rsqrt(jnp.mean(xv * xv, axis=-1, keepdims=True) + EPS)
            h = (xv * r * g_ref[...]) * (1.0 + mod_ref[1:2, :]) + mod_ref[0:1, :]
            hs[...] = h.astype(MXU_DTYPE)
            h_ref[...] = h.astype(MXU_DTYPE)
        proj_ref[...] = _mm(hs[...], w_ref[0])

    return pl.pallas_call(
        body, name="inproj",
        grid=(S // tm, N_DEV),
        in_specs=[pl.BlockSpec((tm, D), lambda i, j: (i, 0)),
                  pl.BlockSpec((8, D), lambda i, j: (0, 0)),
                  pl.BlockSpec((1, D), lambda i, j: (0, 0)),
                  pl.BlockSpec((1, D, D), lambda i, j: (j, 0, 0))],
        out_specs=[pl.BlockSpec((tm, D), lambda i, j: (i, j)),
                   pl.BlockSpec((tm, D), lambda i, j: (i, 0))],
        out_shape=[jax.ShapeDtypeStruct((S, IN_W), F32), jax.ShapeDtypeStruct((S, D), MXU_DTYPE)],
        scratch_shapes=[pltpu.VMEM((tm, D), MXU_DTYPE)],
        compiler_params=_params(("parallel", "arbitrary")),
    )(x, mod3, norm_g, w_in_g)


def _bucket_tables():
    qi = np.arange(A_BLK)[:, None]
    kj = np.arange(2 * A_BLK)[None, :]
    delta = qi + A_BLK - kj
    out = []
    for window, dil in PATTERNS:
        span = window // dil
        band = (delta >= 0) & (delta <= span)
        dist = np.clip(delta, 0, None) * dil
        max_exact = N_BUCKETS // 2
        nf = dist.astype(np.float32)
        large = max_exact + (np.log(np.maximum(nf, np.float32(1.0)) / np.float32(max_exact))
                             / np.float32(math.log(MAX_DISTANCE / max_exact))
                             * np.float32(N_BUCKETS - max_exact)).astype(np.int32)
        large = np.minimum(large, N_BUCKETS - 1)
        bucket = np.where(dist < max_exact, dist, large)
        out.append(np.where(band, bucket, -1).astype(np.int32))
    return np.stack(out)


def _bias_tables(rel_bias, buckets):
    def body(rb_ref, bk_ref, o_ref):
        h = pl.program_id(1)
        bk = bk_ref[0]
        acc = jnp.full(bk.shape, NEG, F32)
        for b in range(N_BUCKETS):
            acc = jnp.where(bk == b, rb_ref[b, h], acc)
        o_ref[0, 0] = acc

    return pl.pallas_call(
        body, name="bias_tables",
        grid=(3, A_HEADS),
        in_specs=[pl.BlockSpec(memory_space=pltpu.SMEM),
                  pl.BlockSpec((1, A_BLK, 2 * A_BLK), lambda p, h: (p, 0, 0))],
        out_specs=pl.BlockSpec((1, 1, A_BLK, 2 * A_BLK), lambda p, h: (p, h, 0, 0)),
        out_shape=jax.ShapeDtypeStruct((3, A_HEADS, A_BLK, 2 * A_BLK), F32),
        compiler_params=_params(("arbitrary", "arbitrary")),
    )(rel_bias, buckets)


A_TILES = 8


def _attn_heads_per_step(d):
    return A_HEADS if d == 1 else 2


def _attn_in_specs(sb, nsb, hw):
    w = A_HD * hw
    per = A_W // w

    def cur(col):
        return pl.BlockSpec((sb, w), lambda hp, n: (jnp.minimum(n, nsb - 1), per * col + hp))

    def prev(col):
        return pl.BlockSpec((sb, w), lambda hp, n: (jnp.maximum(jnp.minimum(n, nsb - 1) - 1, 0), per * col + hp))

    return [cur(0), prev(1), cur(1), prev(2), cur(2)]


def _rows(r, d):
    return pl.ds(r, A_BLK) if d == 1 else pl.ds(r, A_BLK, stride=d)


def _for_residues(d, hw, fn):
    unroll = min(d, max(1, A_TILES // hw))
    if d == unroll:
        for r in range(d):
            fn(r)
    else:
        def group(g, c):
            for u in range(unroll):
                fn(g * unroll + u)
            return c
        lax.fori_loop(0, d // unroll, group, 0)


def _attn_scores(q, kp, kc, bias, hh, first):
    sl = slice(A_HD * hh, A_HD * (hh + 1))
    k = jnp.concatenate([kp[:, sl], kc[:, sl]], axis=0)
    s = _mm(q[:, sl], k, NT) * (A_HD ** -0.5) + bias + first
    return s, k


def _attn_fwd(proj, bias_p, d, name):
    S = proj.shape[0]
    sb = A_BLK * d
    nsb = S // sb
    hw = _attn_heads_per_step(d)

    def body(q_ref, kp_ref, kc_ref, vp_ref, vc_ref, b_ref, o_ref, l_ref):
        n = pl.program_id(1)
        kj = lax.broadcasted_iota(jnp.int32, (A_BLK, 2 * A_BLK), 1)
        first = jnp.where((n == 0) & (kj < A_BLK), NEG, 0.0).astype(F32)

        def residue(r):
            rows = _rows(r, d)
            for pp in range(hw // 2):
                lanes = pl.ds(2 * A_HD * pp, 2 * A_HD)
                q, kp, kc, vp, vc = (ref[rows, lanes] for ref in (q_ref, kp_ref, kc_ref, vp_ref, vc_ref))
                os, ls = [], []
                for hh in range(2):
                    sl = slice(A_HD * hh, A_HD * (hh + 1))
                    s, _ = _attn_scores(q, kp, kc, b_ref[2 * pp + hh], hh, first)
                    v = jnp.concatenate([vp[:, sl], vc[:, sl]], axis=0)
                    m = jnp.max(s, axis=-1, keepdims=True)
                    p = jnp.exp(s - m)
                    den = jnp.sum(p, axis=-1, keepdims=True)
                    os.append(_mm(p, v) / den)
                    ls.append(jnp.broadcast_to(m + jnp.log(den), (A_BLK, A_HD)))
                o_ref[rows, lanes] = jnp.concatenate(os, axis=1)
                l_ref[rows, lanes] = jnp.concatenate(ls, axis=1)

        _for_residues(d, hw, residue)

    out = pl.BlockSpec((sb, A_HD * hw), lambda hp, n: (n, hp))
    return pl.pallas_call(
        body, name=name,
        grid=(A_HEADS // hw, nsb),
        in_specs=_attn_in_specs(sb, nsb, hw) + [pl.BlockSpec((hw, A_BLK, 2 * A_BLK), lambda hp, n: (hp, 0, 0))],
        out_specs=[out, out],
        out_shape=[jax.ShapeDtypeStruct((S, A_W), F32)] * 2,
        compiler_params=_params(("parallel", "parallel")),
    )(proj, proj, proj, proj, proj, bias_p)


def _attn_combine(os, ls, proj, tm=512):
    S = proj.shape[0]

    def body(o1, o2, o3, l1, l2, l3, z_ref, ao_ref, lt_ref, oa_ref):
        a1, a2, a3 = l1[...], l2[...], l3[...]
        m = jnp.maximum(jnp.maximum(a1, a2), a3)
        e1, e2, e3 = jnp.exp(a1 - m), jnp.exp(a2 - m), jnp.exp(a3 - m)
        den = e1 + e2 + e3
        ao = (e1 * o1[...] + e2 * o2[...] + e3 * o3[...]) / den
        z = z_ref[...]
        ao_ref[...] = ao
        lt_ref[...] = m + jnp.log(den)
        oa_ref[...] = (ao * (z * _sigmoid(z))).astype(MXU_DTYPE)

    spec = pl.BlockSpec((tm, A_W), lambda i: (i, 0))
    return pl.pallas_call(
        body, name="attn_combine",
        grid=(S // tm,),
        in_specs=[spec] * 6 + [pl.BlockSpec((tm, A_W), lambda i: (i, 3))],
        out_specs=[spec] * 3,
        out_shape=[jax.ShapeDtypeStruct((S, A_W), F32), jax.ShapeDtypeStruct((S, A_W), F32),
                   jax.ShapeDtypeStruct((S, A_W), MXU_DTYPE)],
        compiler_params=_params(("parallel",)),
    )(*os, *ls, proj)


def _attn_pre_bwd(doa, ao, proj, tm=512):
    S = proj.shape[0]

    def body(doa_ref, ao_ref, z_ref, do_ref, dz_ref, dl_ref):
        z = z_ref[...]
        sg = _sigmoid(z)
        g = doa_ref[...]
        ao_v = ao_ref[...]
        do = g * (z * sg)
        do_ref[...] = do
        dz_ref[...] = (g * ao_v * (sg * (1.0 + z * (1.0 - sg)))).astype(MXU_DTYPE)
        prod = do * ao_v
        for h in range(A_HEADS):
            sl = slice(A_HD * h, A_HD * (h + 1))
            dl_ref[:, sl] = jnp.broadcast_to(jnp.sum(prod[:, sl], axis=-1, keepdims=True), (tm, A_HD))

    spec = pl.BlockSpec((tm, A_W), lambda i: (i, 0))
    return pl.pallas_call(
        body, name="attn_pre_bwd",
        grid=(S // tm,),
        in_specs=[spec, spec, pl.BlockSpec((tm, A_W), lambda i: (i, 3))],
        out_specs=[spec] * 3,
        out_shape=[jax.ShapeDtypeStruct((S, A_W), F32), jax.ShapeDtypeStruct((S, A_W), MXU_DTYPE),
                   jax.ShapeDtypeStruct((S, A_W), F32)],
        compiler_params=_params(("parallel",)),
    )(doa, ao, proj)


def _attn_bwd(proj, do, lt, delta, bias_p, d, name):
    S = proj.shape[0]
    sb = A_BLK * d
    nsb = S // sb
    hw = _attn_heads_per_step(d)

    def body(q_ref, kp_ref, kc_ref, vp_ref, vc_ref, do_ref, lt_ref, dl_ref, b_ref,
             dq_ref, dk_ref, dv_ref, db_ref, ck, cv):
        n = pl.program_id(1)

        @pl.when(n == 0)
        def _():
            db_ref[...] = jnp.zeros_like(db_ref)
            ck[...] = jnp.zeros_like(ck)
            cv[...] = jnp.zeros_like(cv)

        @pl.when(n < nsb)
        def _():
            kj = lax.broadcasted_iota(jnp.int32, (A_BLK, 2 * A_BLK), 1)
            first = jnp.where((n == 0) & (kj < A_BLK), NEG, 0.0).astype(F32)

            def residue(r):
                rows = _rows(r, d)
                for pp in range(hw // 2):
                    lanes = pl.ds(2 * A_HD * pp, 2 * A_HD)
                    q, kp, kc, vp, vc, do_r, lt_r, dl_r = (
                        ref[rows, lanes] for ref in (q_ref, kp_ref, kc_ref, vp_ref, vc_ref, do_ref, lt_ref, dl_ref))
                    dqs, dks, dvs = [], [], []
                    for hh in range(2):
                        sl = slice(A_HD * hh, A_HD * (hh + 1))
                        s, k = _attn_scores(q, kp, kc, b_ref[2 * pp + hh], hh, first)
                        v = jnp.concatenate([vp[:, sl], vc[:, sl]], axis=0)
                        p = jnp.exp(s - lt_r[:, A_HD * hh:A_HD * hh + 1])
                        do_h = do_r[:, sl]
                        ds = p * (_mm(do_h, v, NT) - dl_r[:, A_HD * hh:A_HD * hh + 1])
                        db_ref[2 * pp + hh] += ds
                        dqs.append(_mm(ds, k) * (A_HD ** -0.5))
                        dks.append(_mm(ds, q[:, sl], TN) * (A_HD ** -0.5))
                        dvs.append(_mm(p, do_h, TN))
                    dk = jnp.concatenate(dks, axis=1)
                    dv = jnp.concatenate(dvs, axis=1)
                    dq_ref[rows, lanes] = jnp.concatenate(dqs, axis=1)
                    dk_ref[rows, lanes] = ck[rows, lanes] + dk[:A_BLK]
                    dv_ref[rows, lanes] = cv[rows, lanes] + dv[:A_BLK]
                    ck[rows, lanes] = dk[A_BLK:]
                    cv[rows, lanes] = dv[A_BLK:]

            _for_residues(d, hw, residue)

        @pl.when(n == nsb)
        def _():
            dk_ref[...] = ck[...]
            dv_ref[...] = cv[...]

    w = A_HD * hw
    row = pl.BlockSpec((sb, w), lambda hp, n: (jnp.minimum(n, nsb - 1), hp))
    lag = pl.BlockSpec((sb, w), lambda hp, n: (jnp.maximum(n - 1, 0), hp))
    tab = pl.BlockSpec((hw, A_BLK, 2 * A_BLK), lambda hp, n: (hp, 0, 0))
    return pl.pallas_call(
        body, name=name,
        grid=(A_HEADS // hw, nsb + 1),
        in_specs=_attn_in_specs(sb, nsb, hw) + [row, row, row, tab],
        out_specs=[row, lag, lag, tab],
        out_shape=[jax.ShapeDtypeStruct((S, A_W), F32)] * 3
                  + [jax.ShapeDtypeStruct((A_HEADS, A_BLK, 2 * A_BLK), F32)],
        scratch_shapes=[pltpu.VMEM((sb, w), F32), pltpu.VMEM((sb, w), F32)],
        compiler_params=_params(("parallel", "arbitrary")),
    )(proj, proj, proj, proj, proj, do, lt, delta, bias_p)


def _attn_assemble(dqs, dks, dvs, dz, tm=512):
    S = dz.shape[0]

    def body(q1, q2, q3, k1, k2, k3, v1, v2, v3, z_ref, p0_ref, p1_ref):
        p0_ref[:, :A_W] = (q1[...] + q2[...] + q3[...]).astype(MXU_DTYPE)
        p0_ref[:, A_W:] = (k1[...] + k2[...] + k3[...]).astype(MXU_DTYPE)
        p1_ref[:, :A_W] = (v1[...] + v2[...] + v3[...]).astype(MXU_DTYPE)
        p1_ref[:, A_W:] = z_ref[...]

    spec = pl.BlockSpec((tm, A_W), lambda i: (i, 0))
    wide = pl.BlockSpec((tm, 2 * A_W), lambda i: (i, 0))
    return pl.pallas_call(
        body, name="attn_assemble",
        grid=(S // tm,),
        in_specs=[spec] * 10,
        out_specs=[wide, wide],
        out_shape=[jax.ShapeDtypeStruct((S, 2 * A_W), MXU_DTYPE)] * 2,
        compiler_params=_params(("parallel",)),
    )(*dqs, *dks, *dvs, dz)


def _rel_bias_grad(dbs, buckets):
    def body(d1, d2, d3, bk_ref, o_ref):
        row = lax.broadcasted_iota(jnp.int32, (A_HEADS, 128), 0)
        lane = lax.broadcasted_iota(jnp.int32, (A_HEADS, 128), 1)
        acc = jnp.zeros((A_HEADS, 128), F32)
        for p, dref in enumerate((d1, d2, d3)):
            bk = bk_ref[p]
            for h in range(A_HEADS):
                ds = dref[h]
                for b in range(N_BUCKETS):
                    s = jnp.sum(jnp.where(bk == b, ds, 0.0), keepdims=True)
                    acc = acc + jnp.where((row == h) & (lane == b), s, 0.0)
        o_ref[...] = acc

    return pl.pallas_call(
        body, name="rel_bias_grad",
        out_shape=jax.ShapeDtypeStruct((A_HEADS, 128), F32),
        compiler_params=_params(),
    )(*dbs, buckets)


def _tri(c):
    t = np.tril(np.ones((c, c), np.float32))
    return jnp.asarray(t), jnp.asarray(t.T.copy())


def _fill_above(ref, x, pad):
    ref[0:G_SUB, :] = jnp.full((G_SUB, x.shape[1]), pad, F32)
    ref[G_SUB:, :] = x


def _fill_below(ref, x, pad):
    ref[0:x.shape[0], :] = x
    ref[x.shape[0]:, :] = jnp.full((G_SUB, x.shape[1]), pad, F32)


def _hgrn_gates(q_ref, f_ref, lbp_ref, tri_ref):
    qraw = q_ref[...]
    sq = _sigmoid(qraw)
    q = qraw * sq
    sg = _sigmoid(f_ref[...])
    lb = _sigmoid(lbp_ref[0:1, :] - lbp_ref[1:2, :])
    f = lb + (1.0 - lb) * sg
    k = 1.0 - f
    b = _mm_exact(tri_ref[...], jnp.log(f))
    return qraw, sq, q, sg, lb, f, k, b


def _hgrn_col(C, base, idx):
    return pl.BlockSpec((C, G_HPS * G_DK), lambda h, n: (idx(n), base * (G_HEADS // G_HPS) + h))


def _hgrn_levels(C):
    out, m = [], G_SUB
    while 2 * m <= C:
        out.append(m)
        m *= 2
    return out


def _hgrn_level(b, C, m):
    zeros = jnp.zeros((m, G_DK), F32)
    refq, refk = [], []
    for blk in range(C // m):
        if blk % 2 == 1:
            refq.append(jnp.broadcast_to(b[blk * m:blk * m + 1], (m, G_DK)))
            refk.append(zeros)
        else:
            refq.append(zeros)
            refk.append(jnp.broadcast_to(b[(blk + 1) * m:(blk + 1) * m + 1], (m, G_DK)))
    refq = jnp.concatenate(refq, axis=0)
    refk = jnp.concatenate(refk, axis=0)
    right = (lax.broadcasted_iota(jnp.int32, (C, 1), 0) // m) % 2 == 1
    eq = jnp.exp(jnp.where(right, b - refq, NEG))
    ek = jnp.exp(jnp.where(right, NEG, refk - b))
    ti = lax.broadcasted_iota(jnp.int32, (C, C), 0)
    si = lax.broadcasted_iota(jnp.int32, (C, C), 1)
    return eq, ek, (ti // (2 * m) == si // (2 * m)) & (ti - si >= G_SUB)


def _hgrn_fwd(proj, hgrn_lb, onorm_g, C=G_CHUNK):
    S = proj.shape[0]
    nc = S // C
    tri, _ = _tri(C)

    def body(q_ref, f_ref, i_ref, z_ref, lbp_ref, go_ref, tri_ref, o_ref, ob_ref, st_ref, St, kp, vp, fp):
        @pl.when(pl.program_id(1) == 0)
        def _():
            St[...] = jnp.zeros_like(St)

        for hh in range(G_HPS):
            ln = pl.ds(G_DK * hh, G_DK)
            head(q_ref.at[:, ln], f_ref.at[:, ln], i_ref.at[:, ln], z_ref.at[:, ln], lbp_ref.at[:, ln], go_ref,
                 tri_ref, o_ref.at[:, ln], ob_ref.at[:, ln], st_ref.at[0, hh], St.at[hh], kp.at[hh], vp.at[hh],
                 fp.at[hh])

    def head(q_ref, f_ref, i_ref, z_ref, lbp_ref, go_ref, tri_ref, o_ref, ob_ref, st_ref, St, kp, vp, fp):
        _, _, q, _, _, f, k, b = _hgrn_gates(q_ref, f_ref, lbp_ref, tri_ref)
        v = i_ref[...]
        bC = b[C - 1:C, :]
        S0 = St[...]
        o = _mm(q * jnp.exp(b), S0, NT)
        _fill_above(kp, k, 0.0)
        _fill_above(vp, v, 0.0)
        _fill_above(fp, f, 1.0)
        near = []
        for r0 in range(0, C, G_RB):
            qb = q[r0:r0 + G_RB]
            acc = e = None
            for l in range(G_SUB):
                rows = pl.ds(G_SUB - l + r0, G_RB)
                if l > 0:
                    fl = fp[pl.ds(G_SUB - l + 1 + r0, G_RB), :]
                    e = fl if e is None else e * fl
                kl = kp[rows, :]
                a = jnp.sum(qb * kl if e is None else qb * kl * e, axis=-1, keepdims=True)
                t = a * vp[rows, :]
                acc = t if acc is None else acc + t
            near.append(acc)
        o = o + jnp.concatenate(near, axis=0)
        a_off = jnp.zeros((C, C), F32)
        for m in _hgrn_levels(C):
            eq, ek, pm = _hgrn_level(b, C, m)
            a_off = a_off + jnp.where(pm, _mm(q * eq, k * ek, NT), 0.0)
        o = o + _mm(a_off, v)
        S1 = S0 * jnp.exp(bC) + _mm(v, k * jnp.exp(bC - b), TN)
        St[...] = S1
        st_ref[...] = S1
        o_ref[...] = o
        r = lax.rsqrt(jnp.mean(o * o, axis=-1, keepdims=True) + EPS)
        z = z_ref[...]
        ob_ref[...] = (o * r * go_ref[...] * (z * _sigmoid(z))).astype(MXU_DTYPE)

    ident = lambda n: n
    w = G_HPS * G_DK
    out = pl.BlockSpec((C, w), lambda h, n: (n, h))
    return pl.pallas_call(
        body, name="hgrn_fwd",
        grid=(G_HEADS // G_HPS, nc),
        in_specs=[_hgrn_col(C, 2, ident), _hgrn_col(C, 3, ident), _hgrn_col(C, 4, ident), _hgrn_col(C, 5, ident),
                  pl.BlockSpec((2, w), lambda h, n: (0, h)),
                  pl.BlockSpec((1, G_DK), lambda h, n: (0, 0)),
                  pl.BlockSpec((C, C), lambda h, n: (0, 0))],
        out_specs=[out, out, pl.BlockSpec((1, G_HPS, G_DK, G_DK), lambda h, n: (n, h, 0, 0))],
        out_shape=[jax.ShapeDtypeStruct((S, G_W), F32), jax.ShapeDtypeStruct((S, G_W), MXU_DTYPE),
                   jax.ShapeDtypeStruct((nc, G_HEADS, G_DK, G_DK), F32)],
        scratch_shapes=[pltpu.VMEM((G_HPS, G_DK, G_DK), F32)] + [pltpu.VMEM((G_HPS, C + G_SUB, G_DK), F32)] * 3,
        compiler_params=_params(("parallel", "arbitrary")),
    )(proj, proj, proj, proj, hgrn_lb, onorm_g, tri)


def _hgrn_bwd(proj, o_raw, dob, states, hgrn_lb, onorm_g, C=G_CHUNK):
    S = proj.shape[0]
    nc = S // C
    tri, triu = _tri(C)

    def body(q_ref, f_ref, i_ref, z_ref, o_ref, dob_ref, s0_ref, s1_ref, lbp_ref, go_ref, tri_ref, triu_ref,
             dq_ref, df_ref, di_ref, dz_ref, dlb_ref, dgo_ref, dSt, *shifted):
        @pl.when(pl.program_id(1) == 0)
        def _():
            dSt[...] = jnp.zeros_like(dSt)
            dlb_ref[...] = jnp.zeros_like(dlb_ref)
            dgo_ref[...] = jnp.zeros_like(dgo_ref)

        for hh in range(G_HPS):
            ln = pl.ds(G_DK * hh, G_DK)
            head(q_ref.at[:, ln], f_ref.at[:, ln], i_ref.at[:, ln], z_ref.at[:, ln], o_ref.at[:, ln],
                 dob_ref.at[:, ln], s0_ref.at[0, hh], s1_ref.at[0, hh], lbp_ref.at[:, ln], go_ref, tri_ref, triu_ref,
                 dq_ref.at[:, ln], df_ref.at[:, ln], di_ref.at[:, ln], dz_ref.at[:, ln], dlb_ref.at[:, ln],
                 dgo_ref.at[pl.ds(8 * hh, 8), :], dSt.at[hh], *[t.at[hh] for t in shifted])

    def head(q_ref, f_ref, i_ref, z_ref, o_ref, dob_ref, s0_ref, s1_ref, lbp_ref, go_ref, tri_ref, triu_ref,
             dq_ref, df_ref, di_ref, dz_ref, dlb_ref, dgo_ref, dSt, kp, vp, fp, qn, dn_, fn):
        cn = nc - 1 - pl.program_id(1)
        qraw, sq, q, sg, lb, f, k, b = _hgrn_gates(q_ref, f_ref, lbp_ref, tri_ref)
        v = i_ref[...]
        bC = b[C - 1:C, :]
        eb = jnp.exp(b)
        ecb = jnp.exp(bC - b)
        o = o_ref[...]
        z = z_ref[...]
        sz = _sigmoid(z)
        go = go_ref[...]
        g_ob = dob_ref[...]
        r = lax.rsqrt(jnp.mean(o * o, axis=-1, keepdims=True) + EPS)
        nh = o * r
        dnrm = g_ob * (z * sz)
        dz_ref[...] = (g_ob * (nh * go) * (sz * (1.0 + z * (1.0 - sz)))).astype(MXU_DTYPE)
        dgo_ref[0:1, :] += jnp.sum(dnrm * nh, axis=0, keepdims=True)
        dn = dnrm * go
        do = r * (dn - nh * jnp.mean(dn * nh, axis=-1, keepdims=True))

        S0 = jnp.where(cn == 0, 0.0, s0_ref[...])
        S1 = s1_ref[...]
        dS1 = dSt[...]
        dq = eb * _mm(do, S0)
        dk = ecb * _mm(v, dS1)
        dv = _mm(k * ecb, dS1, NT)
        bnd = jnp.sum(dS1 * S1, axis=0, keepdims=True)
        dSt[...] = dS1 * jnp.exp(bC) + _mm(do, q * eb, TN)

        _fill_above(kp, k, 0.0)
        _fill_above(vp, v, 0.0)
        _fill_above(fp, f, 1.0)
        _fill_below(qn, q, 0.0)
        _fill_below(dn_, do, 0.0)
        _fill_below(fn, f, 1.0)
        near_q, near_k, near_v = [], [], []
        for r0 in range(0, C, G_RB):
            do_b, k_b, v_b = do[r0:r0 + G_RB], k[r0:r0 + G_RB], v[r0:r0 + G_RB]
            aq = ak = av = e = e2 = None
            for l in range(G_SUB):
                down, up = pl.ds(G_SUB - l + r0, G_RB), pl.ds(l + r0, G_RB)
                if l > 0:
                    fl = fp[pl.ds(G_SUB - l + 1 + r0, G_RB), :]
                    e = fl if e is None else e * fl
                    fu = fn[up, :]
                    e2 = fu if e2 is None else e2 * fu
                kl = kp[down, :]
                da = jnp.sum(do_b * vp[down, :], axis=-1, keepdims=True)
                t = da * (kl if e is None else kl * e)
                aq = t if aq is None else aq + t
                qu = qn[up, :]
                qe = qu if e2 is None else qu * e2
                dou = dn_[up, :]
                da2 = jnp.sum(dou * v_b, axis=-1, keepdims=True)
                a2 = jnp.sum(qe * k_b, axis=-1, keepdims=True)
                t = da2 * qe
                ak = t if ak is None else ak + t
                t = a2 * dou
                av = t if av is None else av + t
            near_q.append(aq)
            near_k.append(ak)
            near_v.append(av)
        dq = dq + jnp.concatenate(near_q, axis=0)
        dk = dk + jnp.concatenate(near_k, axis=0)
        dv = dv + jnp.concatenate(near_v, axis=0)

        da_all = _mm(do, v, NT)
        a_off = jnp.zeros((C, C), F32)
        for m in _hgrn_levels(C):
            eq, ek, pm = _hgrn_level(b, C, m)
            qt, kt = q * eq, k * ek
            da_m = jnp.where(pm, da_all, 0.0)
            a_off = a_off + jnp.where(pm, _mm(qt, kt, NT), 0.0)
            dq = dq + _mm(da_m, kt) * eq
            dk = dk + _mm(da_m, qt, TN) * ek
        dv = dv + _mm(a_off, do, TN)

        row = lax.broadcasted_iota(jnp.int32, (C, 1), 0)
        db = q * dq - k * dk + jnp.where(row == C - 1, bnd, 0.0)
        dg = _mm_exact(triu_ref[...], db)
        df = dg / f - dk
        df_ref[...] = (df * (1.0 - lb) * (sg * (1.0 - sg))).astype(MXU_DTYPE)
        dlb_ref[0:1, :] += jnp.sum(df * (1.0 - sg), axis=0, keepdims=True)
        dq_ref[...] = (dq * (sq * (1.0 + qraw * (1.0 - sq)))).astype(MXU_DTYPE)
        di_ref[...] = dv.astype(MXU_DTYPE)

    rev = lambda n: nc - 1 - n
    w = G_HPS * G_DK
    blk = pl.BlockSpec((C, w), lambda h, n: (nc - 1 - n, h))
    return pl.pallas_call(
        body, name="hgrn_bwd",
        grid=(G_HEADS // G_HPS, nc),
        in_specs=[_hgrn_col(C, 2, rev), _hgrn_col(C, 3, rev), _hgrn_col(C, 4, rev), _hgrn_col(C, 5, rev), blk, blk,
                  pl.BlockSpec((1, G_HPS, G_DK, G_DK), lambda h, n: (jnp.maximum(nc - 2 - n, 0), h, 0, 0)),
                  pl.BlockSpec((1, G_HPS, G_DK, G_DK), lambda h, n: (nc - 1 - n, h, 0, 0)),
                  pl.BlockSpec((2, w), lambda h, n: (0, h)),
                  pl.BlockSpec((1, G_DK), lambda h, n: (0, 0)),
                  pl.BlockSpec((C, C), lambda h, n: (0, 0)),
                  pl.BlockSpec((C, C), lambda h, n: (0, 0))],
        out_specs=[blk, blk, blk, blk,
                   pl.BlockSpec((8, w), lambda h, n: (0, h)),
                   pl.BlockSpec((8 * G_HPS, G_DK), lambda h, n: (h, 0))],
        out_shape=[jax.ShapeDtypeStruct((S, G_W), MXU_DTYPE)] * 4
                  + [jax.ShapeDtypeStruct((8, G_W), F32), jax.ShapeDtypeStruct((8 * G_HEADS, G_DK), F32)],
        scratch_shapes=[pltpu.VMEM((G_HPS, G_DK, G_DK), F32)] + [pltpu.VMEM((G_HPS, C + G_SUB, G_DK), F32)] * 6,
        compiler_params=_params(("parallel", "arbitrary")),
    )(proj, proj, proj, proj, o_raw, dob, states, states, hgrn_lb, onorm_g, tri, triu)


def _tail(x, target, oa, ob, proj, mod3, final_g, wa, wb, wo, tm=256):
    S = x.shape[0]
    nt = S // tm

    def body(x_ref, t_ref, oa_ref, ob_ref, ga_ref, gb_ref, mod_ref, fg_ref, wa_ref, wb_ref, wo_ref,
             dx2_ref, doa_ref, dob_ref, dga_ref, dgb_ref, sums_ref, gwa_ref, gwb_ref, gwo_ref,
             acc_a, acc_b, acc_o):
        i = pl.program_id(0)

        @pl.when(i == 0)
        def _():
            sums_ref[...] = jnp.zeros_like(sums_ref)
            acc_a[...] = jnp.zeros_like(acc_a)
            acc_b[...] = jnp.zeros_like(acc_b)
            acc_o[...] = jnp.zeros_like(acc_o)

        oa_v, ob_v = oa_ref[...], ob_ref[...]
        pa = _mm(oa_v, wa_ref[...])
        pb = _mm(ob_v, wb_ref[...])
        sa, sb = _sigmoid(ga_ref[...]), _sigmoid(gb_ref[...])
        ym = sa * pa + sb * pb
        u = _mm(ym, wo_ref[...])
        gate = mod_ref[2:3, :]
        fg = fg_ref[...]
        x2 = x_ref[...] + gate * u
        r2 = lax.rsqrt(jnp.mean(x2 * x2, axis=-1, keepdims=True) + EPS)
        xn2 = x2 * r2
        e = xn2 * fg - t_ref[...]
        dy = e * (1.0 / D)
        dn = dy * fg
        dx2 = r2 * (dn - xn2 * jnp.mean(dn * xn2, axis=-1, keepdims=True))
        dx2_ref[...] = dx2
        sums_ref[0:1, :] += jnp.sum(dy * xn2, axis=0, keepdims=True)
        sums_ref[1:2, :] += jnp.sum(dx2 * u, axis=0, keepdims=True)
        sums_ref[2:3, :] += (0.5 / D) * jnp.sum(e * e, axis=0, keepdims=True)
        du = dx2 * gate
        dym = _mm(du, wo_ref[...], NT)
        acc_o[...] += _mm(ym, du, TN)
        dpa, dpb = dym * sa, dym * sb
        dga_ref[...] = (dym * pa * (sa * (1.0 - sa))).astype(MXU_DTYPE)
        dgb_ref[...] = (dym * pb * (sb * (1.0 - sb))).astype(MXU_DTYPE)
        doa_ref[...] = _mm(dpa, wa_ref[...], NT)
        dob_ref[...] = _mm(dpb, wb_ref[...], NT)
        acc_a[...] += _mm(oa_v, dpa, TN)
        acc_b[...] += _mm(ob_v, dpb, TN)

        @pl.when(i == nt - 1)
        def _():
            pltpu.sync_copy(acc_a, gwa_ref)
            pltpu.sync_copy(acc_b, gwb_ref)
            pltpu.sync_copy(acc_o, gwo_ref)

    row = lambda w: pl.BlockSpec((tm, w), lambda i: (i, 0))
    full = lambda a, b: pl.BlockSpec((a, b), lambda i: (0, 0))
    any_spec = pl.BlockSpec(memory_space=pl.ANY)
    return pl.pallas_call(
        body, name="tail",
        grid=(nt,),
        in_specs=[row(D), row(D), row(A_W), row(D),
                  pl.BlockSpec((tm, D), lambda i: (i, 6)), pl.BlockSpec((tm, D), lambda i: (i, 7)),
                  full(8, D), full(1, D), full(A_W, D), full(D, D), full(D, D)],
        out_specs=[row(D), row(A_W), row(D), row(D), row(D), full(8, D), any_spec, any_spec, any_spec],
        out_shape=[jax.ShapeDtypeStruct((S, D), F32), jax.ShapeDtypeStruct((S, A_W), F32),
                   jax.ShapeDtypeStruct((S, D), F32), jax.ShapeDtypeStruct((S, D), MXU_DTYPE),
                   jax.ShapeDtypeStruct((S, D), MXU_DTYPE), jax.ShapeDtypeStruct((8, D), F32),
                   jax.ShapeDtypeStruct((A_W, D), F32), jax.ShapeDtypeStruct((D, D), F32),
                   jax.ShapeDtypeStruct((D, D), F32)],
        scratch_shapes=[pltpu.VMEM((A_W, D), F32), pltpu.VMEM((D, D), F32), pltpu.VMEM((D, D), F32)],
        compiler_params=_params(("arbitrary",)),
    )(x, target, oa, ob, proj, proj, mod3, final_g, wa, wb, wo)


def _dh(pieces, w_in_g, x, dx2, mod3, norm_g, grads, tm=512):
    S = x.shape[0]
    ni = S // tm
    ng = len(grads)

    def body(*refs):
        p_refs = refs[:N_DEV]
        w_ref, x_ref, dx2_ref, mod_ref, g_ref = refs[N_DEV:N_DEV + 5]
        g_ins = refs[N_DEV + 5:N_DEV + 5 + ng]
        gx_ref, sums_ref = refs[N_DEV + 5 + ng:N_DEV + 7 + ng]
        g_outs = refs[N_DEV + 7 + ng:N_DEV + 7 + 2 * ng]
        acc, send_sems, recv_sems, local_sems = refs[N_DEV + 7 + 2 * ng:]
        i, j = pl.program_id(0), pl.program_id(1)
        start, wait = _all_to_all_copies(g_ins, g_outs, send_sems, recv_sems, local_sems)

        @pl.when((i == 0) & (j == 0))
        def _():
            start()
            sums_ref[...] = jnp.zeros_like(sums_ref)

        @pl.when(j == 0)
        def _():
            acc[...] = jnp.zeros_like(acc)

        for k in range(N_DEV):
            @pl.when(j == k)
            def _(k=k):
                acc[...] += _mm(p_refs[k][...], w_ref[0], NT)

        @pl.when(j == N_DEV - 1)
        def _():
            dh = acc[...]
            xv = x_ref[...]
            g = g_ref[...]
            sc1 = 1.0 + mod_ref[1:2, :]
            r = lax.rsqrt(jnp.mean(xv * xv, axis=-1, keepdims=True) + EPS)
            xn = xv * r
            sums_ref[0:1, :] += jnp.sum(dh, axis=0, keepdims=True)
            sums_ref[1:2, :] += jnp.sum(dh * (xn * g), axis=0, keepdims=True)
            sums_ref[2:3, :] += jnp.sum(dh * sc1 * xn, axis=0, keepdims=True)
            dxn = dh * sc1 * g
            gx_ref[...] = dx2_ref[...] + r * (dxn - xn * jnp.mean(dxn * xn, axis=-1, keepdims=True))

        @pl.when((i == ni - 1) & (j == N_DEV - 1))
        def _():
            wait()

    row = pl.BlockSpec((tm, D), lambda i, j: (i, 0))
    any_spec = pl.BlockSpec(memory_space=pl.ANY)
    return pl.pallas_call(
        body, name="dh_scatter",
        grid=(ni, N_DEV),
        in_specs=[row] * N_DEV
                 + [pl.BlockSpec((1, D, D), lambda i, j: (j, 0, 0)),
                    row, row,
                    pl.BlockSpec((8, D), lambda i, j: (0, 0)),
                    pl.BlockSpec((1, D), lambda i, j: (0, 0))]
                 + [any_spec] * ng,
        out_specs=[row, pl.BlockSpec((8, D), lambda i, j: (0, 0))] + [any_spec] * ng,
        out_shape=[jax.ShapeDtypeStruct((S, D), F32), jax.ShapeDtypeStruct((8, D), F32)]
                  + [jax.ShapeDtypeStruct(g.shape, g.dtype) for g in grads],
        scratch_shapes=[pltpu.VMEM((tm, D), F32),
                        pltpu.SemaphoreType.DMA((ng, N_DEV - 1)), pltpu.SemaphoreType.DMA((ng, N_DEV - 1)),
                        pltpu.SemaphoreType.DMA((ng,))],
        compiler_params=_params(("arbitrary", "arbitrary")),
    )(*pieces, w_in_g, x, dx2, mod3, norm_g, *grads)


def _gw_in(h, pieces, tm=512):
    S = h.shape[0]
    nt = S // tm

    def body(*refs):
        h_ref, p_refs, o_ref, acc = refs[0], refs[1:1 + N_DEV], refs[1 + N_DEV], refs[2 + N_DEV]
        j, i = pl.program_id(0), pl.program_id(1)

        @pl.when(i == 0)
        def _():
            acc[...] = jnp.zeros_like(acc)

        for k in range(N_DEV):
            @pl.when(j == k)
            def _(k=k):
                acc[...] += _mm(h_ref[...], p_refs[k][...], TN)

        @pl.when(i == nt - 1)
        def _():
            o_ref[0] = acc[...].astype(XCHG_DTYPE)

    def piece(k):
        return pl.BlockSpec((tm, D), lambda j, i: (jnp.where(j == k, i, 0), 0))

    return pl.pallas_call(
        body, name="gw_in",
        grid=(N_DEV, nt),
        in_specs=[pl.BlockSpec((tm, D), lambda j, i: (i, 0))] + [piece(k) for k in range(N_DEV)],
        out_specs=pl.BlockSpec((1, D, D), lambda j, i: (j, 0, 0)),
        out_shape=jax.ShapeDtypeStruct((N_DEV, D, D), XCHG_DTYPE),
        scratch_shapes=[pltpu.VMEM((D, D), F32)],
        compiler_params=_params(("parallel", "arbitrary")),
    )(h, *pieces)


def _adamw_math(w, g, m, v):
    m = ADAM_B1 * m + (1.0 - ADAM_B1) * g
    v = ADAM_B2 * v + (1.0 - ADAM_B2) * (g * g)
    m_hat = m / (1.0 - ADAM_B1 ** ADAM_STEP)
    v_hat = v / (1.0 - ADAM_B2 ** ADAM_STEP)
    delta = -ADAM_LR * (m_hat / (jnp.sqrt(v_hat) + ADAM_EPS) + ADAM_WD * w)
    return delta, m, v


def _adamw_big(recv, w, m, v, name, tr=128):
    M, N = w.shape
    tr = min(tr, M)

    def body(r_ref, w_ref, m_ref, v_ref, g_ref, d_ref, nm_ref, nv_ref):
        g = r_ref[0].astype(F32)
        for j in range(1, N_DEV):
            g = g + r_ref[j].astype(F32)
        g_ref[...] = g
        d_ref[...], nm_ref[...], nv_ref[...] = _adamw_math(w_ref[...], g, m_ref[...], v_ref[...])

    blk = pl.BlockSpec((tr, N), lambda i: (i, 0))
    return pl.pallas_call(
        body, name=name,
        grid=(M // tr,),
        in_specs=[pl.BlockSpec((N_DEV, tr, N), lambda i: (0, i, 0)), blk, blk, blk],
        out_specs=[blk] * 4,
        out_shape=[jax.ShapeDtypeStruct((M, N), F32)] * 4,
        compiler_params=_params(("parallel",)),
    )(recv, w, m, v)


def _adamw_w_ada(c64, dmod64, w, m, v):
    def body(c_ref, dm_ref, w_ref, m_ref, v_ref, g_ref, d_ref, nm_ref, nv_ref):
        cv = c_ref[...]
        g = _mm(cv * _sigmoid(cv), dm_ref[...], TN)
        g_ref[...] = g
        d_ref[...], nm_ref[...], nv_ref[...] = _adamw_math(w_ref[...], g, m_ref[...], v_ref[...])

    return pl.pallas_call(
        body, name="adamw_w_ada",
        out_shape=[jax.ShapeDtypeStruct(w.shape, F32)] * 4,
        compiler_params=_params(),
    )(c64, dmod64, w, m, v)


P_MOD, P_NORM, P_ONORM, P_RELB, P_LB, P_FINAL, P_LOSS, P_END = (0, 3 * D, 4 * D, 5 * D, 6 * D, 7 * D, 8 * D, 9 * D)


def _adamw_small(packed, b_ada, norm_g, onorm_g, relb, hgrn_lb, final_g, ms, vs):
    def body(pk_ref, b_ref, ng_ref, og_ref, rb_ref, lb_ref, fg_ref,
             mb, mn, mo, mr, ml, mf, vb, vn, vo, vr, vl, vf,
             loss_ref, gb, gn, go, gr, gl, gf, db, dn, do, dr, dl, df,
             nmb, nmn, nmo, nmr, nml, nmf, nvb, nvn, nvo, nvr, nvl, nvf):
        tot = pk_ref[0:1, :]
        for j in range(1, N_DEV):
            tot = tot + pk_ref[8 * j:8 * j + 1, :]
        loss_ref[...] = jnp.broadcast_to(jnp.sum(tot[:, P_LOSS:P_END], axis=-1, keepdims=True), (8, 128))

        def upd(g, w_ref, m_ref, v_ref, g_out, d_out, m_out, v_out):
            g_out[...] = g
            d_out[...], m_out[...], v_out[...] = _adamw_math(w_ref[...], g, m_ref[...], v_ref[...])

        upd(tot[:, P_MOD:P_NORM], b_ref, mb, vb, gb, db, nmb, nvb)
        upd(tot[:, P_NORM:P_ONORM], ng_ref, mn, vn, gn, dn, nmn, nvn)
        g_on = tot[:, P_ONORM:P_ONORM + G_DK]
        for h in range(1, G_HEADS):
            g_on = g_on + tot[:, P_ONORM + G_DK * h:P_ONORM + G_DK * (h + 1)]
        upd(g_on, og_ref, mo, vo, go, do, nmo, nvo)
        upd(tot[:, P_RELB:P_LB], rb_ref, mr, vr, gr, dr, nmr, nvr)
        a = lb_ref[...]
        lb = _sigmoid(a[0:1, :] - a[1:2, :])
        g0 = tot[:, P_LB:P_FINAL] * lb * (1.0 - lb)
        row = lax.broadcasted_iota(jnp.int32, (2, D), 0)
        upd(jnp.where(row == 0, g0, -g0), lb_ref, ml, vl, gl, dl, nml, nvl)
        upd(tot[:, P_FINAL:P_LOSS], fg_ref, mf, vf, gf, df, nmf, nvf)

    shapes = [b_ada.shape, norm_g.shape, onorm_g.shape, relb.shape, hgrn_lb.shape, final_g.shape]
    outs = [jax.ShapeDtypeStruct((8, 128), F32)] + [jax.ShapeDtypeStruct(s, F32) for s in shapes] * 4
    return pl.pallas_call(
        body, name="adamw_small",
        out_shape=outs,
        compiler_params=_params(),
    )(packed, b_ada, norm_g, onorm_g, relb, hgrn_lb, final_g, *ms, *vs)


def _local_step(x, target, mod3, norm_g, w_in_g, onorm_g, wa, wb, wo, rel_bias, hgrn_lb, final_g):
    buckets = jnp.asarray(_bucket_tables())
    bias = _bias_tables(rel_bias, buckets)
    proj, h = _inproj(x, mod3, norm_g, w_in_g)
    os, ls = [], []
    for p, (_, d) in enumerate(PATTERNS):
        o, l = _attn_fwd(proj, bias[p], d, "attn_fwd_d%d" % d)
        os.append(o)
        ls.append(l)
    ao, lt, oa = _attn_combine(os, ls, proj)
    o_raw, ob, states = _hgrn_fwd(proj, hgrn_lb, onorm_g)
    dx2, doa, dob, dga, dgb, tsums, gwa, gwb, gwo = _tail(x, target, oa, ob, proj, mod3, final_g, wa, wb, wo)
    do, dza, delta = _attn_pre_bwd(doa, ao, proj)
    dqs, dks, dvs, dbs = [], [], [], []
    for p, (_, d) in enumerate(PATTERNS):
        dq, dk, dv, db = _attn_bwd(proj, do, lt, delta, bias[p], d, "attn_bwd_d%d" % d)
        dqs.append(dq)
        dks.append(dk)
        dvs.append(dv)
        dbs.append(db)
    p0, p1 = _attn_assemble(dqs, dks, dvs, dza)
    g_relb = _rel_bias_grad(dbs, buckets)
    dqb, dfb, dib, dzb, dlb, dgo = _hgrn_bwd(proj, o_raw, dob, states, hgrn_lb, onorm_g)
    pieces = [p0, p1, dqb, dfb, dib, dzb, dga, dgb]
    grads = [_gw_in(h, pieces),
             gwa.astype(XCHG_DTYPE).reshape(A_W, N_DEV, D // N_DEV).transpose(1, 0, 2),
             gwb.astype(XCHG_DTYPE).reshape(N_DEV, D // N_DEV, D),
             gwo.astype(XCHG_DTYPE).reshape(N_DEV, D // N_DEV, D)]
    gx, hsums, *received = _dh(pieces, w_in_g, x, dx2, mod3, norm_g, grads)
    row = jnp.concatenate([
        hsums[0], hsums[1], tsums[1],
        hsums[2],
        dgo.reshape(G_HEADS, 8, G_DK)[:, 0].reshape(-1),
        g_relb.reshape(-1),
        dlb[0],
        tsums[0],
        tsums[2],
    ])
    return gx, received, row


def kernel(x, c, w_ada, b_ada, norm_g, w_in, hgrn_onorm_g, w_branch_a, w_branch_b, w_out, rel_bias, hgrn_lb, final_g, loss_target, m_w_ada, m_b_ada, m_norm_g, m_w_in, m_hgrn_onorm_g, m_w_branch_a, m_w_branch_b, m_w_out, m_rel_bias, m_hgrn_lb, m_final_g, v_w_ada, v_b_ada, v_norm_g, v_w_in, v_hgrn_onorm_g, v_w_branch_a, v_w_branch_b, v_w_out, v_rel_bias, v_hgrn_lb, v_final_g):
    me = 4 * lax.axis_index("x") + 2 * lax.axis_index("y") + lax.axis_index("c")
    n_ada = w_ada.shape[2]

    w_in_g, wa_g, wb_g, wo_g = _all_gather(
        [w_in[0].astype(MXU_DTYPE), w_branch_a[0].astype(MXU_DTYPE),
         w_branch_b[0].astype(MXU_DTYPE), w_out[0].astype(MXU_DTYPE)], "gather_weights")
    wa = wa_g.transpose(1, 0, 2).reshape(A_W, D)
    wb = wb_g.reshape(D, D)
    wo = wo_g.reshape(D, D)

    (c_all,) = _all_gather([jnp.broadcast_to(c, (8, D))], "gather_c")
    c64 = c_all.reshape(8 * N_DEV, D)
    b_loc = lax.dynamic_slice(b_ada, (0, me * n_ada), (1, n_ada))
    mod_part = _mod_fwd(c64, w_ada[0], b_loc)[::8]
    (mod_all,) = _all_gather([mod_part], "gather_mod")
    mod = lax.dynamic_slice(mod_all, (0, me, 0), (N_DEV, 1, n_ada)).reshape(3, D)
    mod3 = jnp.concatenate([mod, jnp.zeros((5, D), F32)], axis=0)

    onorm_t = hgrn_onorm_g
    gx, (r_in, r_a, r_b, r_o), row = _local_step(
        x[0], loss_target[0], mod3, norm_g, w_in_g, onorm_t, wa, wb, wo, rel_bias, hgrn_lb,
        final_g.reshape(1, D))
    packed8 = jnp.concatenate([row[None, :], jnp.zeros((7, P_END), F32)], axis=0)
    (packed,) = _all_gather([packed8], "gather_small")
    packed = packed.reshape(8 * N_DEV, P_END)

    g_in, d_in, nm_in, nv_in = _adamw_big(r_in, w_in[0], m_w_in[0], v_w_in[0], "adamw_w_in")
    g_a, d_a, nm_a, nv_a = _adamw_big(r_a, w_branch_a[0], m_w_branch_a[0], v_w_branch_a[0], "adamw_w_branch_a")
    g_b, d_b, nm_b, nv_b = _adamw_big(r_b, w_branch_b[0], m_w_branch_b[0], v_w_branch_b[0], "adamw_w_branch_b")
    g_o, d_o, nm_o, nv_o = _adamw_big(r_o, w_out[0], m_w_out[0], v_w_out[0], "adamw_w_out")

    dmod64 = lax.dynamic_slice(packed, (0, P_MOD + me * n_ada), (8 * N_DEV, n_ada))
    g_ada, d_ada, nm_ada, nv_ada = _adamw_w_ada(c64, dmod64, w_ada[0], m_w_ada[0], v_w_ada[0])

    def flat_relb(t):
        return jnp.pad(t.T, ((0, 0), (0, 128 - N_BUCKETS))).reshape(1, A_HEADS * 128)

    def unflat_relb(t):
        return t.reshape(A_HEADS, 128)[:, :N_BUCKETS].T

    fg2 = lambda t: t.reshape(1, D)
    smalls = _adamw_small(
        packed, b_ada, norm_g, hgrn_onorm_g, flat_relb(rel_bias), hgrn_lb, fg2(final_g),
        [m_b_ada, m_norm_g, m_hgrn_onorm_g, flat_relb(m_rel_bias), m_hgrn_lb, fg2(m_final_g)],
        [v_b_ada, v_norm_g, v_hgrn_onorm_g, flat_relb(v_rel_bias), v_hgrn_lb, fg2(v_final_g)])
    loss = smalls[0][0, 0]

    def small(kind):
        s = smalls[1 + 6 * kind:7 + 6 * kind]
        return s[0], s[1], s[2], unflat_relb(s[3]), s[4], s[5].reshape(D)

    def leaves(ada, sm, w_in_, wa_, wb_, wo_):
        b_, n_, o_, r_, l_, f_ = sm
        return (ada[None], b_, n_, w_in_[None], o_, wa_[None], wb_[None], wo_[None], r_, l_, f_)

    return (loss, gx[None],
            *leaves(g_ada, small(0), g_in, g_a, g_b, g_o),
            *leaves(d_ada, small(1), d_in, d_a, d_b, d_o),
            *leaves(nm_ada, small(2), nm_in, nm_a, nm_b, nm_o),
            *leaves(nv_ada, small(3), nv_in, nv_a, nv_b, nv_o))
```

```python
import functools
import math

import numpy as np
import jax
import jax.numpy as jnp
from jax import lax
from jax.experimental import pallas as pl
from jax.experimental.pallas import tpu as pltpu

F32 = jnp.float32
BF16 = jnp.bfloat16
MXU_DTYPE = jnp.bfloat16
XCHG_DTYPE = jnp.bfloat16

N_DEV = 8
D = 1024
A_HEADS = 8
A_HD = 64
A_W = A_HEADS * A_HD
A_BLK = 128
PATTERNS = ((128, 1), (512, 4), (2048, 16))
N_BUCKETS = 32
MAX_DISTANCE = 2048
NEG = -1e30
G_HEADS = 8
G_DK = 128
G_W = G_HEADS * G_DK
IN_W = 8 * D
EPS = 1e-6
ADAM_LR = 0.001
ADAM_B1 = 0.9
ADAM_B2 = 0.999
ADAM_EPS = 1e-08
ADAM_WD = 0.01
ADAM_STEP = 10

G_CHUNK = 128
G_SUB = 16
G_HPS_FWD = 4
G_HPS_BWD = 4
G_RB = 16
VMEM_LIMIT = 56 * 1024 * 1024

NN = (((1,), (0,)), ((), ()))
NT = (((1,), (1,)), ((), ()))
TN = (((0,), (0,)), ((), ()))
MESH = pl.DeviceIdType.MESH


def _mm(a, b, dims=NN):
    return lax.dot_general(a.astype(MXU_DTYPE), b.astype(MXU_DTYPE), dims,
                           preferred_element_type=F32)


def _mm_exact(t, x):
    hi = x.astype(BF16)
    r = x - hi.astype(F32)
    mid = r.astype(BF16)
    lo = (r - mid.astype(F32)).astype(BF16)
    tb = t.astype(BF16)
    return sum(lax.dot_general(tb, p, NN, preferred_element_type=F32) for p in (hi, mid, lo))


def _sigmoid(x):
    return 0.5 * jnp.tanh(0.5 * x) + 0.5


def _params(sem=None):
    return pltpu.CompilerParams(dimension_semantics=sem, vmem_limit_bytes=VMEM_LIMIT)


def _all_gather(xs, name):
    n = len(xs)

    def body(*refs):
        ins, outs = refs[:n], refs[n:2 * n]
        send_sems, recv_sems, local_sems = refs[2 * n:]
        x, y, c = lax.axis_index("x"), lax.axis_index("y"), lax.axis_index("c")
        me, sibling = (x, y, c), (x, y, 1 - c)
        chips = [(1 - x, y), (x, 1 - y), (1 - x, 1 - y)]

        def slot(ref, dev):
            return ref.at[4 * dev[0] + 2 * dev[1] + dev[2]]

        def copy(a, k, block, to, src=None):
            return pltpu.make_async_remote_copy(
                src_ref=slot(outs[a], block) if src is None else src,
                dst_ref=slot(outs[a], block),
                send_sem=send_sems.at[a, k], recv_sem=recv_sems.at[a, k],
                device_id=to, device_id_type=MESH)

        mine, first, passed = [], [], []
        for a in range(n):
            cp = pltpu.make_async_copy(ins[a], slot(outs[a], me), local_sems.at[a])
            cp.start()
            mine.append(cp)
            first.append(copy(a, 0, me, sibling, src=ins[a]))
            for j, chip in enumerate(chips):
                first.append(copy(a, 1 + j, me, (*chip, c), src=ins[a]))
        for cp in first:
            cp.start()
        for j, chip in enumerate(chips):
            for a in range(n):
                copy(a, 1 + j, (*chip, c), me).wait_recv()
                cp = copy(a, 4 + j, (*chip, c), sibling)
                cp.start()
                passed.append(cp)
        for a in range(n):
            copy(a, 0, sibling, me).wait_recv()
            for j, chip in enumerate(chips):
                copy(a, 4 + j, (*chip, 1 - c), me).wait_recv()
        for cp in first + passed:
            cp.wait_send()
        for cp in mine:
            cp.wait()

    any_spec = pl.BlockSpec(memory_space=pl.ANY)
    return pl.pallas_call(
        body, name=name,
        out_shape=[jax.ShapeDtypeStruct((N_DEV,) + v.shape, v.dtype) for v in xs],
        in_specs=[any_spec] * n, out_specs=[any_spec] * n,
        scratch_shapes=[pltpu.SemaphoreType.DMA((n, 7)), pltpu.SemaphoreType.DMA((n, 7)),
                        pltpu.SemaphoreType.DMA((n,))],
    )(*xs)


def _all_to_all_copies(ins, outs, send_sems, recv_sems, local_sems):
    n = len(ins)
    x, y, c = lax.axis_index("x"), lax.axis_index("y"), lax.axis_index("c")
    me = 4 * x + 2 * y + c
    peers = []
    for m in range(1, N_DEV):
        peers.append((1 - x if m & 4 else x, 1 - y if m & 2 else y, 1 - c if m & 1 else c))

    def copy(a, k, landing):
        peer = peers[k]
        pid = 4 * peer[0] + 2 * peer[1] + peer[2]
        return pltpu.make_async_remote_copy(
            src_ref=ins[a].at[pid], dst_ref=outs[a].at[pid if landing else me],
            send_sem=send_sems.at[a, k], recv_sem=recv_sems.at[a, k],
            device_id=peer, device_id_type=MESH)

    def local(a):
        return pltpu.make_async_copy(ins[a].at[me], outs[a].at[me], local_sems.at[a])

    def start():
        for a in range(n):
            local(a).start()
        for k in range(N_DEV - 1):
            for a in range(n):
                copy(a, k, False).start()

    def wait():
        for k in range(N_DEV - 1):
            for a in range(n):
                copy(a, k, True).wait_recv()
        for k in range(N_DEV - 1):
            for a in range(n):
                copy(a, k, False).wait_send()
        for a in range(n):
            local(a).wait()

    return start, wait


def _mod_fwd(c64, w_ada, b_loc):
    def body(c_ref, w_ref, b_ref, o_ref):
        cv = c_ref[...]
        sc = cv * _sigmoid(cv)
        o_ref[...] = _mm(sc, w_ref[...]) + b_ref[...]

    return pl.pallas_call(
        body, name="mod_fwd",
        out_shape=jax.ShapeDtypeStruct((c64.shape[0], w_ada.shape[1]), F32),
        compiler_params=_params(),
    )(c64, w_ada, b_loc)


def _inproj(x, mod3, norm_g, w_in_g, tm=1024):
    S = x.shape[0]

    def body(x_ref, mod_ref, g_ref, w_ref, proj_ref, h_ref, hs):
        @pl.when(pl.program_id(1) == 0)
        def _():
            xv = x_ref[...]
            r = lax.rsqrt(jnp.mean(xv * xv, axis=-1, keepdims=True) + EPS)
            h = (xv * r * g_ref[...]) * (1.0 + mod_ref[1:2, :]) + mod_ref[0:1, :]
            hs[...] = h.astype(MXU_DTYPE)
            h_ref[...] = h.astype(MXU_DTYPE)
        proj_ref[...] = _mm(hs[...], w_ref[0])

    return pl.pallas_call(
        body, name="inproj",
        grid=(S // tm, N_DEV),
        in_specs=[pl.BlockSpec((tm, D), lambda i, j: (i, 0)),
                  pl.BlockSpec((8, D), lambda i, j: (0, 0)),
                  pl.BlockSpec((1, D), lambda i, j: (0, 0)),
                  pl.BlockSpec((1, D, D), lambda i, j: (j, 0, 0))],
        out_specs=[pl.BlockSpec((tm, D), lambda i, j: (i, j)),
                   pl.BlockSpec((tm, D), lambda i, j: (i, 0))],
        out_shape=[jax.ShapeDtypeStruct((S, IN_W), F32), jax.ShapeDtypeStruct((S, D), MXU_DTYPE)],
        scratch_shapes=[pltpu.VMEM((tm, D), MXU_DTYPE)],
        compiler_params=_params(("parallel", "arbitrary")),
    )(x, mod3, norm_g, w_in_g)


def _bucket_tables():
    qi = np.arange(A_BLK)[:, None]
    kj = np.arange(2 * A_BLK)[None, :]
    delta = qi + A_BLK - kj
    out = []
    for window, dil in PATTERNS:
        span = window // dil
        band = (delta >= 0) & (delta <= span)
        dist = np.clip(delta, 0, None) * dil
        max_exact = N_BUCKETS // 2
        nf = dist.astype(np.float32)
        large = max_exact + (np.log(np.maximum(nf, np.float32(1.0)) / np.float32(max_exact))
                             / np.float32(math.log(MAX_DISTANCE / max_exact))
                             * np.float32(N_BUCKETS - max_exact)).astype(np.int32)
        large = np.minimum(large, N_BUCKETS - 1)
        bucket = np.where(dist < max_exact, dist, large)
        out.append(np.where(band, bucket, -1).astype(np.int32))
    return np.stack(out)


def _bias_tables(rel_bias, buckets):
    def body(rb_ref, bk_ref, o_ref):
        h = pl.program_id(1)
        bk = bk_ref[0]
        acc = jnp.full(bk.shape, NEG, F32)
        for b in range(N_BUCKETS):
            acc = jnp.where(bk == b, rb_ref[b, h], acc)
        o_ref[0, 0] = acc

    return pl.pallas_call(
        body, name="bias_tables",
        grid=(3, A_HEADS),
        in_specs=[pl.BlockSpec(memory_space=pltpu.SMEM),
                  pl.BlockSpec((1, A_BLK, 2 * A_BLK), lambda p, h: (p, 0, 0))],
        out_specs=pl.BlockSpec((1, 1, A_BLK, 2 * A_BLK), lambda p, h: (p, h, 0, 0)),
        out_shape=jax.ShapeDtypeStruct((3, A_HEADS, A_BLK, 2 * A_BLK), F32),
        compiler_params=_params(("arbitrary", "arbitrary")),
    )(rel_bias, buckets)


A_TILES = 8


def _attn_heads_per_step(d):
    return A_HEADS if d == 1 else 2


def _attn_in_specs(sb, nsb, hw):
    w = A_HD * hw
    per = A_W // w

    def cur(col):
        return pl.BlockSpec((sb, w), lambda hp, n: (jnp.minimum(n, nsb - 1), per * col + hp))

    def prev(col):
        return pl.BlockSpec((sb, w), lambda hp, n: (jnp.maximum(jnp.minimum(n, nsb - 1) - 1, 0), per * col + hp))

    return [cur(0), prev(1), cur(1), prev(2), cur(2)]


def _rows(r, d):
    return pl.ds(r, A_BLK) if d == 1 else pl.ds(r, A_BLK, stride=d)


def _for_residues(d, hw, fn):
    unroll = min(d, max(1, A_TILES // hw))
    if d == unroll:
        for r in range(d):
            fn(r)
    else:
        def group(g, c):
            for u in range(unroll):
                fn(g * unroll + u)
            return c
        lax.fori_loop(0, d // unroll, group, 0)


def _attn_scores(q, kp, kc, bias, hh, first):
    sl = slice(A_HD * hh, A_HD * (hh + 1))
    k = jnp.concatenate([kp[:, sl], kc[:, sl]], axis=0)
    s = _mm(q[:, sl], k, NT) * (A_HD ** -0.5) + bias + first
    return s, k


def _attn_fwd(proj, bias_p, d, name):
    S = proj.shape[0]
    sb = A_BLK * d
    nsb = S // sb
    hw = _attn_heads_per_step(d)

    def body(q_ref, kp_ref, kc_ref, vp_ref, vc_ref, b_ref, o_ref, l_ref):
        n = pl.program_id(1)
        kj = lax.broadcasted_iota(jnp.int32, (A_BLK, 2 * A_BLK), 1)
        first = jnp.where((n == 0) & (kj < A_BLK), NEG, 0.0).astype(F32)

        def residue(r):
            rows = _rows(r, d)
            for pp in range(hw // 2):
                lanes = pl.ds(2 * A_HD * pp, 2 * A_HD)
                q, kp, kc, vp, vc = (ref[rows, lanes] for ref in (q_ref, kp_ref, kc_ref, vp_ref, vc_ref))
                os, ls = [], []
                for hh in range(2):
                    sl = slice(A_HD * hh, A_HD * (hh + 1))
                    s, _ = _attn_scores(q, kp, kc, b_ref[2 * pp + hh], hh, first)
                    v = jnp.concatenate([vp[:, sl], vc[:, sl]], axis=0)
                    m = jnp.max(s, axis=-1, keepdims=True)
                    p = jnp.exp(s - m)
                    den = jnp.sum(p, axis=-1, keepdims=True)
                    os.append(_mm(p, v) / den)
                    ls.append(jnp.broadcast_to(m + jnp.log(den), (A_BLK, A_HD)))
                o_ref[rows, lanes] = jnp.concatenate(os, axis=1)
                l_ref[rows, lanes] = jnp.concatenate(ls, axis=1)

        _for_residues(d, hw, residue)

    out = pl.BlockSpec((sb, A_HD * hw), lambda hp, n: (n, hp))
    return pl.pallas_call(
        body, name=name,
        grid=(A_HEADS // hw, nsb),
        in_specs=_attn_in_specs(sb, nsb, hw) + [pl.BlockSpec((hw, A_BLK, 2 * A_BLK), lambda hp, n: (hp, 0, 0))],
        out_specs=[out, out],
        out_shape=[jax.ShapeDtypeStruct((S, A_W), F32)] * 2,
        compiler_params=_params(("parallel", "parallel")),
    )(proj, proj, proj, proj, proj, bias_p)


def _attn_combine(os, ls, proj, tm=512):
    S = proj.shape[0]

    def body(o1, o2, o3, l1, l2, l3, z_ref, ao_ref, lt_ref, oa_ref):
        a1, a2, a3 = l1[...], l2[...], l3[...]
        m = jnp.maximum(jnp.maximum(a1, a2), a3)
        e1, e2, e3 = jnp.exp(a1 - m), jnp.exp(a2 - m), jnp.exp(a3 - m)
        den = e1 + e2 + e3
        ao = (e1 * o1[...] + e2 * o2[...] + e3 * o3[...]) / den
        z = z_ref[...]
        ao_ref[...] = ao
        lt_ref[...] = m + jnp.log(den)
        oa_ref[...] = (ao * (z * _sigmoid(z))).astype(MXU_DTYPE)

    spec = pl.BlockSpec((tm, A_W), lambda i: (i, 0))
    return pl.pallas_call(
        body, name="attn_combine",
        grid=(S // tm,),
        in_specs=[spec] * 6 + [pl.BlockSpec((tm, A_W), lambda i: (i, 3))],
        out_specs=[spec] * 3,
        out_shape=[jax.ShapeDtypeStruct((S, A_W), F32), jax.ShapeDtypeStruct((S, A_W), F32),
                   jax.ShapeDtypeStruct((S, A_W), MXU_DTYPE)],
        compiler_params=_params(("parallel",)),
    )(*os, *ls, proj)


def _attn_pre_bwd(doa, ao, proj, tm=512):
    S = proj.shape[0]

    def body(doa_ref, ao_ref, z_ref, do_ref, dz_ref, dl_ref):
        z = z_ref[...]
        sg = _sigmoid(z)
        g = doa_ref[...]
        ao_v = ao_ref[...]
        do = g * (z * sg)
        do_ref[...] = do
        dz_ref[...] = (g * ao_v * (sg * (1.0 + z * (1.0 - sg)))).astype(MXU_DTYPE)
        prod = do * ao_v
        for h in range(A_HEADS):
            sl = slice(A_HD * h, A_HD * (h + 1))
            dl_ref[:, sl] = jnp.broadcast_to(jnp.sum(prod[:, sl], axis=-1, keepdims=True), (tm, A_HD))

    spec = pl.BlockSpec((tm, A_W), lambda i: (i, 0))
    return pl.pallas_call(
        body, name="attn_pre_bwd",
        grid=(S // tm,),
        in_specs=[spec, spec, pl.BlockSpec((tm, A_W), lambda i: (i, 3))],
        out_specs=[spec] * 3,
        out_shape=[jax.ShapeDtypeStruct((S, A_W), F32), jax.ShapeDtypeStruct((S, A_W), MXU_DTYPE),
                   jax.ShapeDtypeStruct((S, A_W), F32)],
        compiler_params=_params(("parallel",)),
    )(doa, ao, proj)


def _attn_bwd(proj, do, lt, delta, bias_p, d, name):
    S = proj.shape[0]
    sb = A_BLK * d
    nsb = S // sb
    hw = _attn_heads_per_step(d)

    def body(q_ref, kp_ref, kc_ref, vp_ref, vc_ref, do_ref, lt_ref, dl_ref, b_ref,
             dq_ref, dk_ref, dv_ref, db_ref, ck, cv):
        n = pl.program_id(1)

        @pl.when(n == 0)
        def _():
            db_ref[...] = jnp.zeros_like(db_ref)
            ck[...] = jnp.zeros_like(ck)
            cv[...] = jnp.zeros_like(cv)

        @pl.when(n < nsb)
        def _():
            kj = lax.broadcasted_iota(jnp.int32, (A_BLK, 2 * A_BLK), 1)
            first = jnp.where((n == 0) & (kj < A_BLK), NEG, 0.0).astype(F32)

            def residue(r):
                rows = _rows(r, d)
                for pp in range(hw // 2):
                    lanes = pl.ds(2 * A_HD * pp, 2 * A_HD)
                    q, kp, kc, vp, vc, do_r, lt_r, dl_r = (
                        ref[rows, lanes] for ref in (q_ref, kp_ref, kc_ref, vp_ref, vc_ref, do_ref, lt_ref, dl_ref))
                    dqs, dks, dvs = [], [], []
                    for hh in range(2):
                        sl = slice(A_HD * hh, A_HD * (hh + 1))
                        s, k = _attn_scores(q, kp, kc, b_ref[2 * pp + hh], hh, first)
                        v = jnp.concatenate([vp[:, sl], vc[:, sl]], axis=0)
                        p = jnp.exp(s - lt_r[:, A_HD * hh:A_HD * hh + 1])
                        do_h = do_r[:, sl]
                        ds = p * (_mm(do_h, v, NT) - dl_r[:, A_HD * hh:A_HD * hh + 1])
                        db_ref[2 * pp + hh] += ds
                        dqs.append(_mm(ds, k) * (A_HD ** -0.5))
                        dks.append(_mm(ds, q[:, sl], TN) * (A_HD ** -0.5))
                        dvs.append(_mm(p, do_h, TN))
                    dk = jnp.concatenate(dks, axis=1)
                    dv = jnp.concatenate(dvs, axis=1)
                    dq_ref[rows, lanes] = jnp.concatenate(dqs, axis=1)
                    dk_ref[rows, lanes] = ck[rows, lanes] + dk[:A_BLK]
                    dv_ref[rows, lanes] = cv[rows, lanes] + dv[:A_BLK]
                    ck[rows, lanes] = dk[A_BLK:]
                    cv[rows, lanes] = dv[A_BLK:]

            _for_residues(d, hw, residue)

        @pl.when(n == nsb)
        def _():
            dk_ref[...] = ck[...]
            dv_ref[...] = cv[...]

    w = A_HD * hw
    row = pl.BlockSpec((sb, w), lambda hp, n: (jnp.minimum(n, nsb - 1), hp))
    lag = pl.BlockSpec((sb, w), lambda hp, n: (jnp.maximum(n - 1, 0), hp))
    tab = pl.BlockSpec((hw, A_BLK, 2 * A_BLK), lambda hp, n: (hp, 0, 0))
    return pl.pallas_call(
        body, name=name,
        grid=(A_HEADS // hw, nsb + 1),
        in_specs=_attn_in_specs(sb, nsb, hw) + [row, row, row, tab],
        out_specs=[row, lag, lag, tab],
        out_shape=[jax.ShapeDtypeStruct((S, A_W), F32)] * 3
                  + [jax.ShapeDtypeStruct((A_HEADS, A_BLK, 2 * A_BLK), F32)],
        scratch_shapes=[pltpu.VMEM((sb, w), F32), pltpu.VMEM((sb, w), F32)],
        compiler_params=_params(("parallel", "arbitrary")),
    )(proj, proj, proj, proj, proj, do, lt, delta, bias_p)


def _attn_assemble(dqs, dks, dvs, dz, tm=512):
    S = dz.shape[0]

    def body(q1, q2, q3, k1, k2, k3, v1, v2, v3, z_ref, p0_ref, p1_ref):
        p0_ref[:, :A_W] = (q1[...] + q2[...] + q3[...]).astype(MXU_DTYPE)
        p0_ref[:, A_W:] = (k1[...] + k2[...] + k3[...]).astype(MXU_DTYPE)
        p1_ref[:, :A_W] = (v1[...] + v2[...] + v3[...]).astype(MXU_DTYPE)
        p1_ref[:, A_W:] = z_ref[...]

    spec = pl.BlockSpec((tm, A_W), lambda i: (i, 0))
    wide = pl.BlockSpec((tm, 2 * A_W), lambda i: (i, 0))
    return pl.pallas_call(
        body, name="attn_assemble",
        grid=(S // tm,),
        in_specs=[spec] * 10,
        out_specs=[wide, wide],
        out_shape=[jax.ShapeDtypeStruct((S, 2 * A_W), MXU_DTYPE)] * 2,
        compiler_params=_params(("parallel",)),
    )(*dqs, *dks, *dvs, dz)


def _rel_bias_grad(dbs, buckets):
    def body(d1, d2, d3, bk_ref, o_ref):
        row = lax.broadcasted_iota(jnp.int32, (A_HEADS, 128), 0)
        lane = lax.broadcasted_iota(jnp.int32, (A_HEADS, 128), 1)
        acc = jnp.zeros((A_HEADS, 128), F32)
        for p, dref in enumerate((d1, d2, d3)):
            bk = bk_ref[p]
            for h in range(A_HEADS):
                ds = dref[h]
                for b in range(N_BUCKETS):
                    s = jnp.sum(jnp.where(bk == b, ds, 0.0), keepdims=True)
                    acc = acc + jnp.where((row == h) & (lane == b), s, 0.0)
        o_ref[...] = acc

    return pl.pallas_call(
        body, name="rel_bias_grad",
        out_shape=jax.ShapeDtypeStruct((A_HEADS, 128), F32),
        compiler_params=_params(),
    )(*dbs, buckets)


def _tri(c):
    t = np.tril(np.ones((c, c), np.float32))
    return jnp.asarray(t), jnp.asarray(t.T.copy())


def _fill_above(ref, x, pad):
    ref[0:G_SUB, :] = jnp.full((G_SUB, x.shape[1]), pad, F32)
    ref[G_SUB:, :] = x


def _fill_below(ref, x, pad):
    ref[0:x.shape[0], :] = x
    ref[x.shape[0]:, :] = jnp.full((G_SUB, x.shape[1]), pad, F32)


def _hgrn_gates(q_ref, f_ref, lbp_ref, tri_ref):
    qraw = q_ref[...]
    sq = _sigmoid(qraw)
    q = qraw * sq
    sg = _sigmoid(f_ref[...])
    lb = _sigmoid(lbp_ref[0:1, :] - lbp_ref[1:2, :])
    f = lb + (1.0 - lb) * sg
    k = 1.0 - f
    b = _mm_exact(tri_ref[...], jnp.log(f))
    return qraw, sq, q, sg, lb, f, k, b


def _hgrn_col(C, base, idx, hps):
    return pl.BlockSpec((C, hps * G_DK), lambda h, n: (idx(n), base * (G_HEADS // hps) + h))


def _hgrn_levels(C):
    out, m = [], G_SUB
    while 2 * m <= C:
        out.append(m)
        m *= 2
    return out


def _hgrn_level_masks(C):
    ti = np.arange(C)[:, None]
    si = np.arange(C)[None, :]
    return jnp.asarray(np.stack([((ti // (2 * m) == si // (2 * m)) & (ti - si >= G_SUB)).astype(np.float32)
                                 for m in _hgrn_levels(C)]))


def _hgrn_level(b, q, k, C, m):
    zeros = jnp.zeros((m, G_DK), F32)
    eq, ek, qt, kt = [], [], [], []
    for blk in range(0, C // m, 2):
        lo, mid, hi = blk * m, (blk + 1) * m, (blk + 2) * m
        ref = b[mid:mid + 1]
        e_right = jnp.exp(b[mid:hi] - ref)
        e_left = jnp.exp(ref - b[lo:mid])
        eq += [zeros, e_right]
        ek += [e_left, zeros]
        qt += [zeros, q[mid:hi] * e_right]
        kt += [k[lo:mid] * e_left, zeros]
    cat = lambda parts: jnp.concatenate(parts, axis=0)
    return cat(qt), cat(kt), cat(eq), cat(ek)


def _hgrn_fwd(proj, hgrn_lb, onorm_g, C=G_CHUNK):
    S = proj.shape[0]
    nc = S // C
    tri, _ = _tri(C)
    masks = _hgrn_level_masks(C)
    hps = G_HPS_FWD

    def body(q_ref, f_ref, i_ref, z_ref, lbp_ref, go_ref, tri_ref, pm_ref, o_ref, ob_ref, st_ref, St, kp, vp, fp):
        @pl.when(pl.program_id(1) == 0)
        def _():
            St[...] = jnp.zeros_like(St)

        for hh in range(hps):
            ln = pl.ds(G_DK * hh, G_DK)
            head(q_ref.at[:, ln], f_ref.at[:, ln], i_ref.at[:, ln], z_ref.at[:, ln], lbp_ref.at[:, ln], go_ref,
                 tri_ref, pm_ref, o_ref.at[:, ln], ob_ref.at[:, ln], st_ref.at[0, hh], St.at[hh], kp.at[hh], vp.at[hh],
                 fp.at[hh])

    def head(q_ref, f_ref, i_ref, z_ref, lbp_ref, go_ref, tri_ref, pm_ref, o_ref, ob_ref, st_ref, St, kp, vp, fp):
        _, _, q, _, _, f, k, b = _hgrn_gates(q_ref, f_ref, lbp_ref, tri_ref)
        v = i_ref[...]
        bC = b[C - 1:C, :]
        S0 = St[...]
        o = _mm(q * jnp.exp(b), S0, NT)
        _fill_above(kp, k, 0.0)
        _fill_above(vp, v, 0.0)
        _fill_above(fp, f, 1.0)
        near = []
        for r0 in range(0, C, G_RB):
            qb = q[r0:r0 + G_RB]
            acc = e = None
            for l in range(G_SUB):
                rows = pl.ds(G_SUB - l + r0, G_RB)
                if l > 0:
                    fl = fp[pl.ds(G_SUB - l + 1 + r0, G_RB), :]
                    e = fl if e is None else e * fl
                kl = kp[rows, :]
                a = jnp.sum(qb * kl if e is None else qb * kl * e, axis=-1, keepdims=True)
                t = a * vp[rows, :]
                acc = t if acc is None else acc + t
            near.append(acc)
        o = o + jnp.concatenate(near, axis=0)
        a_off = jnp.zeros((C, C), F32)
        for lv, m in enumerate(_hgrn_levels(C)):
            qt, kt, _, _ = _hgrn_level(b, q, k, C, m)
            a_off = a_off + pm_ref[lv] * _mm(qt, kt, NT)
        o = o + _mm(a_off, v)
        S1 = S0 * jnp.exp(bC) + _mm(v, k * jnp.exp(bC - b), TN)
        St[...] = S1
        st_ref[...] = S1
        o_ref[...] = o
        r = lax.rsqrt(jnp.mean(o * o, axis=-1, keepdims=True) + EPS)
        z = z_ref[...]
        ob_ref[...] = (o * r * go_ref[...] * (z * _sigmoid(z))).astype(MXU_DTYPE)

    ident = lambda n: n
    w = hps * G_DK
    out = pl.BlockSpec((C, w), lambda h, n: (n, h))
    return pl.pallas_call(
        body, name="hgrn_fwd",
        grid=(G_HEADS // hps, nc),
        in_specs=[_hgrn_col(C, base, ident, hps) for base in (2, 3, 4, 5)] + [
                  pl.BlockSpec((2, w), lambda h, n: (0, h)),
                  pl.BlockSpec((1, G_DK), lambda h, n: (0, 0)),
                  pl.BlockSpec((C, C), lambda h, n: (0, 0)),
                  pl.BlockSpec(masks.shape, lambda h, n: (0, 0, 0))],
        out_specs=[out, out, pl.BlockSpec((1, hps, G_DK, G_DK), lambda h, n: (n, h, 0, 0))],
        out_shape=[jax.ShapeDtypeStruct((S, G_W), F32), jax.ShapeDtypeStruct((S, G_W), MXU_DTYPE),
                   jax.ShapeDtypeStruct((nc, G_HEADS, G_DK, G_DK), F32)],
        scratch_shapes=[pltpu.VMEM((hps, G_DK, G_DK), F32)] + [pltpu.VMEM((hps, C + G_SUB, G_DK), F32)] * 3,
        compiler_params=_params(("parallel", "arbitrary")),
    )(proj, proj, proj, proj, hgrn_lb, onorm_g, tri, masks)


def _hgrn_bwd(proj, o_raw, dob, states, hgrn_lb, onorm_g, C=G_CHUNK):
    S = proj.shape[0]
    nc = S // C
    tri, triu = _tri(C)
    masks = _hgrn_level_masks(C)
    hps = G_HPS_BWD

    def body(q_ref, f_ref, i_ref, z_ref, o_ref, dob_ref, s0_ref, s1_ref, lbp_ref, go_ref, tri_ref, triu_ref,
             pm_ref, dq_ref, df_ref, di_ref, dz_ref, dlb_ref, dgo_ref, dSt, *shifted):
        @pl.when(pl.program_id(1) == 0)
        def _():
            dSt[...] = jnp.zeros_like(dSt)
            dlb_ref[...] = jnp.zeros_like(dlb_ref)
            dgo_ref[...] = jnp.zeros_like(dgo_ref)

        for hh in range(hps):
            ln = pl.ds(G_DK * hh, G_DK)
            head(q_ref.at[:, ln], f_ref.at[:, ln], i_ref.at[:, ln], z_ref.at[:, ln], o_ref.at[:, ln],
                 dob_ref.at[:, ln], s0_ref.at[0, hh], s1_ref.at[0, hh], lbp_ref.at[:, ln], go_ref, tri_ref, triu_ref,
                 pm_ref, dq_ref.at[:, ln], df_ref.at[:, ln], di_ref.at[:, ln], dz_ref.at[:, ln], dlb_ref.at[:, ln],
                 dgo_ref.at[pl.ds(8 * hh, 8), :], dSt.at[hh], *[t.at[hh] for t in shifted])

    def head(q_ref, f_ref, i_ref, z_ref, o_ref, dob_ref, s0_ref, s1_ref, lbp_ref, go_ref, tri_ref, triu_ref,
             pm_ref, dq_ref, df_ref, di_ref, dz_ref, dlb_ref, dgo_ref, dSt, kp, vp, fp, qn, dn_, fn, xs, dac):
        cn = nc - 1 - pl.program_id(1)
        qraw, sq, q, sg, lb, f, k, b = _hgrn_gates(q_ref, f_ref, lbp_ref, tri_ref)
        v = i_ref[...]
        bC = b[C - 1:C, :]
        eb = jnp.exp(b)
        ecb = jnp.exp(bC - b)
        o = o_ref[...]
        z = z_ref[...]
        sz = _sigmoid(z)
        go = go_ref[...]
        g_ob = dob_ref[...]
        r = lax.rsqrt(jnp.mean(o * o, axis=-1, keepdims=True) + EPS)
        nh = o * r
        dnrm = g_ob * (z * sz)
        dz_ref[...] = (g_ob * (nh * go) * (sz * (1.0 + z * (1.0 - sz)))).astype(MXU_DTYPE)
        dgo_ref[0:1, :] += jnp.sum(dnrm * nh, axis=0, keepdims=True)
        dn = dnrm * go
        do = r * (dn - nh * jnp.mean(dn * nh, axis=-1, keepdims=True))

        S0 = jnp.where(cn == 0, 0.0, s0_ref[...])
        S1 = s1_ref[...]
        dS1 = dSt[...]
        dq = eb * _mm(do, S0)
        dk = ecb * _mm(v, dS1)
        dv = _mm(k * ecb, dS1, NT)
        bnd = jnp.sum(dS1 * S1, axis=0, keepdims=True)
        dSt[...] = dS1 * jnp.exp(bC) + _mm(do, q * eb, TN)

        _fill_above(kp, k, 0.0)
        _fill_above(vp, v, 0.0)
        _fill_above(fp, f, 1.0)
        _fill_below(qn, q, 0.0)
        _fill_below(dn_, do, 0.0)
        _fill_below(fn, f, 1.0)
        for r0 in range(0, C, G_RB):
            do_b = do[r0:r0 + G_RB]
            for l in range(G_SUB):
                xs[pl.ds(l * C + r0, G_RB), :] = (do_b * vp[pl.ds(G_SUB - l + r0, G_RB), :]).astype(MXU_DTYPE)
        dac[0:G_SUB * C, :] = _mm(xs[...], jnp.ones((G_DK, G_DK), MXU_DTYPE))
        dac[G_SUB * C:, :] = jnp.zeros((G_SUB, G_DK), F32)
        near_q, near_k, near_v = [], [], []
        for r0 in range(0, C, G_RB):
            k_b = k[r0:r0 + G_RB]
            aq = ak = av = e = e2 = None
            for l in range(G_SUB):
                down, up = pl.ds(G_SUB - l + r0, G_RB), pl.ds(l + r0, G_RB)
                if l > 0:
                    fl = fp[pl.ds(G_SUB - l + 1 + r0, G_RB), :]
                    e = fl if e is None else e * fl
                    fu = fn[up, :]
                    e2 = fu if e2 is None else e2 * fu
                kl = kp[down, :]
                t = dac[pl.ds(l * C + r0, G_RB), :] * (kl if e is None else kl * e)
                aq = t if aq is None else aq + t
                qu = qn[up, :]
                qe = qu if e2 is None else qu * e2
                dou = dn_[up, :]
                a2 = jnp.sum(qe * k_b, axis=-1, keepdims=True)
                t = dac[pl.ds(l * C + l + r0, G_RB), :] * qe
                ak = t if ak is None else ak + t
                t = a2 * dou
                av = t if av is None else av + t
            near_q.append(aq)
            near_k.append(ak)
            near_v.append(av)
        dq = dq + jnp.concatenate(near_q, axis=0)
        dk = dk + jnp.concatenate(near_k, axis=0)
        dv = dv + jnp.concatenate(near_v, axis=0)

        da_all = _mm(do, v, NT)
        a_off = jnp.zeros((C, C), F32)
        for lv, m in enumerate(_hgrn_levels(C)):
            qt, kt, eq, ek = _hgrn_level(b, q, k, C, m)
            da_m = pm_ref[lv] * da_all
            a_off = a_off + pm_ref[lv] * _mm(qt, kt, NT)
            dq = dq + _mm(da_m, kt) * eq
            dk = dk + _mm(da_m, qt, TN) * ek
        dv = dv + _mm(a_off, do, TN)

        row = lax.broadcasted_iota(jnp.int32, (C, 1), 0)
        db = q * dq - k * dk + jnp.where(row == C - 1, bnd, 0.0)
        dg = _mm_exact(triu_ref[...], db)
        df = dg / f - dk
        df_ref[...] = (df * (1.0 - lb) * (sg * (1.0 - sg))).astype(MXU_DTYPE)
        dlb_ref[0:1, :] += jnp.sum(df * (1.0 - sg), axis=0, keepdims=True)
        dq_ref[...] = (dq * (sq * (1.0 + qraw * (1.0 - sq)))).astype(MXU_DTYPE)
        di_ref[...] = dv.astype(MXU_DTYPE)

    rev = lambda n: nc - 1 - n
    w = hps * G_DK
    blk = pl.BlockSpec((C, w), lambda h, n: (nc - 1 - n, h))
    return pl.pallas_call(
        body, name="hgrn_bwd",
        grid=(G_HEADS // hps, nc),
        in_specs=[_hgrn_col(C, base, rev, hps) for base in (2, 3, 4, 5)] + [
                  blk, blk,
                  pl.BlockSpec((1, hps, G_DK, G_DK), lambda h, n: (jnp.maximum(nc - 2 - n, 0), h, 0, 0)),
                  pl.BlockSpec((1, hps, G_DK, G_DK), lambda h, n: (nc - 1 - n, h, 0, 0)),
                  pl.BlockSpec((2, w), lambda h, n: (0, h)),
                  pl.BlockSpec((1, G_DK), lambda h, n: (0, 0)),
                  pl.BlockSpec((C, C), lambda h, n: (0, 0)),
                  pl.BlockSpec((C, C), lambda h, n: (0, 0)),
                  pl.BlockSpec(masks.shape, lambda h, n: (0, 0, 0))],
        out_specs=[blk, blk, blk, blk,
                   pl.BlockSpec((8, w), lambda h, n: (0, h)),
                   pl.BlockSpec((8 * hps, G_DK), lambda h, n: (h, 0))],
        out_shape=[jax.ShapeDtypeStruct((S, G_W), MXU_DTYPE)] * 4
                  + [jax.ShapeDtypeStruct((8, G_W), F32), jax.ShapeDtypeStruct((8 * G_HEADS, G_DK), F32)],
        scratch_shapes=[pltpu.VMEM((hps, G_DK, G_DK), F32)] + [pltpu.VMEM((hps, C + G_SUB, G_DK), F32)] * 6
                       + [pltpu.VMEM((hps, G_SUB * C, G_DK), MXU_DTYPE),
                          pltpu.VMEM((hps, G_SUB * C + G_SUB, G_DK), F32)],
        compiler_params=_params(("parallel", "arbitrary")),
    )(proj, proj, proj, proj, o_raw, dob, states, states, hgrn_lb, onorm_g, tri, triu, masks)


def _tail(x, target, oa, ob, proj, mod3, final_g, wa, wb, wo, tm=256):
    S = x.shape[0]
    nt = S // tm

    def body(x_ref, t_ref, oa_ref, ob_ref, ga_ref, gb_ref, mod_ref, fg_ref, wa_ref, wb_ref, wo_ref,
             dx2_ref, doa_ref, dob_ref, dga_ref, dgb_ref, sums_ref, gwa_ref, gwb_ref, gwo_ref,
             acc_a, acc_b, acc_o):
        i = pl.program_id(0)

        @pl.when(i == 0)
        def _():
            sums_ref[...] = jnp.zeros_like(sums_ref)
            acc_a[...] = jnp.zeros_like(acc_a)
            acc_b[...] = jnp.zeros_like(acc_b)
            acc_o[...] = jnp.zeros_like(acc_o)

        oa_v, ob_v = oa_ref[...], ob_ref[...]
        pa = _mm(oa_v, wa_ref[...])
        pb = _mm(ob_v, wb_ref[...])
        sa, sb = _sigmoid(ga_ref[...]), _sigmoid(gb_ref[...])
        ym = sa * pa + sb * pb
        u = _mm(ym, wo_ref[...])
        gate = mod_ref[2:3, :]
        fg = fg_ref[...]
        x2 = x_ref[...] + gate * u
        r2 = lax.rsqrt(jnp.mean(x2 * x2, axis=-1, keepdims=True) + EPS)
        xn2 = x2 * r2
        e = xn2 * fg - t_ref[...]
        dy = e * (1.0 / D)
        dn = dy * fg
        dx2 = r2 * (dn - xn2 * jnp.mean(dn * xn2, axis=-1, keepdims=True))
        dx2_ref[...] = dx2
        sums_ref[0:1, :] += jnp.sum(dy * xn2, axis=0, keepdims=True)
        sums_ref[1:2, :] += jnp.sum(dx2 * u, axis=0, keepdims=True)
        sums_ref[2:3, :] += (0.5 / D) * jnp.sum(e * e, axis=0, keepdims=True)
        du = dx2 * gate
        dym = _mm(du, wo_ref[...], NT)
        acc_o[...] += _mm(ym, du, TN)
        dpa, dpb = dym * sa, dym * sb
        dga_ref[...] = (dym * pa * (sa * (1.0 - sa))).astype(MXU_DTYPE)
        dgb_ref[...] = (dym * pb * (sb * (1.0 - sb))).astype(MXU_DTYPE)
        doa_ref[...] = _mm(dpa, wa_ref[...], NT)
        dob_ref[...] = _mm(dpb, wb_ref[...], NT)
        acc_a[...] += _mm(oa_v, dpa, TN)
        acc_b[...] += _mm(ob_v, dpb, TN)

        @pl.when(i == nt - 1)
        def _():
            pltpu.sync_copy(acc_a, gwa_ref)
            pltpu.sync_copy(acc_b, gwb_ref)
            pltpu.sync_copy(acc_o, gwo_ref)

    row = lambda w: pl.BlockSpec((tm, w), lambda i: (i, 0))
    full = lambda a, b: pl.BlockSpec((a, b), lambda i: (0, 0))
    any_spec = pl.BlockSpec(memory_space=pl.ANY)
    return pl.pallas_call(
        body, name="tail",
        grid=(nt,),
        in_specs=[row(D), row(D), row(A_W), row(D),
                  pl.BlockSpec((tm, D), lambda i: (i, 6)), pl.BlockSpec((tm, D), lambda i: (i, 7)),
                  full(8, D), full(1, D), full(A_W, D), full(D, D), full(D, D)],
        out_specs=[row(D), row(A_W), row(D), row(D), row(D), full(8, D), any_spec, any_spec, any_spec],
        out_shape=[jax.ShapeDtypeStruct((S, D), F32), jax.ShapeDtypeStruct((S, A_W), F32),
                   jax.ShapeDtypeStruct((S, D), F32), jax.ShapeDtypeStruct((S, D), MXU_DTYPE),
                   jax.ShapeDtypeStruct((S, D), MXU_DTYPE), jax.ShapeDtypeStruct((8, D), F32),
                   jax.ShapeDtypeStruct((A_W, D), F32), jax.ShapeDtypeStruct((D, D), F32),
                   jax.ShapeDtypeStruct((D, D), F32)],
        scratch_shapes=[pltpu.VMEM((A_W, D), F32), pltpu.VMEM((D, D), F32), pltpu.VMEM((D, D), F32)],
        compiler_params=_params(("arbitrary",)),
    )(x, target, oa, ob, proj, proj, mod3, final_g, wa, wb, wo)


def _dh(pieces, w_in_g, x, dx2, mod3, norm_g, grads, tm=512):
    S = x.shape[0]
    ni = S // tm
    ng = len(grads)

    def body(*refs):
        p_refs = refs[:N_DEV]
        w_ref, x_ref, dx2_ref, mod_ref, g_ref = refs[N_DEV:N_DEV + 5]
        g_ins = refs[N_DEV + 5:N_DEV + 5 + ng]
        gx_ref, sums_ref = refs[N_DEV + 5 + ng:N_DEV + 7 + ng]
        g_outs = refs[N_DEV + 7 + ng:N_DEV + 7 + 2 * ng]
        acc, send_sems, recv_sems, local_sems = refs[N_DEV + 7 + 2 * ng:]
        i, j = pl.program_id(0), pl.program_id(1)
        start, wait = _all_to_all_copies(g_ins, g_outs, send_sems, recv_sems, local_sems)

        @pl.when((i == 0) & (j == 0))
        def _():
            start()
            sums_ref[...] = jnp.zeros_like(sums_ref)

        @pl.when(j == 0)
        def _():
            acc[...] = jnp.zeros_like(acc)

        for k in range(N_DEV):
            @pl.when(j == k)
            def _(k=k):
                acc[...] += _mm(p_refs[k][...], w_ref[0], NT)

        @pl.when(j == N_DEV - 1)
        def _():
            dh = acc[...]
            xv = x_ref[...]
            g = g_ref[...]
            sc1 = 1.0 + mod_ref[1:2, :]
            r = lax.rsqrt(jnp.mean(xv * xv, axis=-1, keepdims=True) + EPS)
            xn = xv * r
            sums_ref[0:1, :] += jnp.sum(dh, axis=0, keepdims=True)
            sums_ref[1:2, :] += jnp.sum(dh * (xn * g), axis=0, keepdims=True)
            sums_ref[2:3, :] += jnp.sum(dh * sc1 * xn, axis=0, keepdims=True)
            dxn = dh * sc1 * g
            gx_ref[...] = dx2_ref[...] + r * (dxn - xn * jnp.mean(dxn * xn, axis=-1, keepdims=True))

        @pl.when((i == ni - 1) & (j == N_DEV - 1))
        def _():
            wait()

    row = pl.BlockSpec((tm, D), lambda i, j: (i, 0))
    any_spec = pl.BlockSpec(memory_space=pl.ANY)
    return pl.pallas_call(
        body, name="dh_scatter",
        grid=(ni, N_DEV),
        in_specs=[row] * N_DEV
                 + [pl.BlockSpec((1, D, D), lambda i, j: (j, 0, 0)),
                    row, row,
                    pl.BlockSpec((8, D), lambda i, j: (0, 0)),
                    pl.BlockSpec((1, D), lambda i, j: (0, 0))]
                 + [any_spec] * ng,
        out_specs=[row, pl.BlockSpec((8, D), lambda i, j: (0, 0))] + [any_spec] * ng,
        out_shape=[jax.ShapeDtypeStruct((S, D), F32), jax.ShapeDtypeStruct((8, D), F32)]
                  + [jax.ShapeDtypeStruct(g.shape, g.dtype) for g in grads],
        scratch_shapes=[pltpu.VMEM((tm, D), F32),
                        pltpu.SemaphoreType.DMA((ng, N_DEV - 1)), pltpu.SemaphoreType.DMA((ng, N_DEV - 1)),
                        pltpu.SemaphoreType.DMA((ng,))],
        compiler_params=_params(("arbitrary", "arbitrary")),
    )(*pieces, w_in_g, x, dx2, mod3, norm_g, *grads)


def _gw_in(h, pieces, tm=512):
    S = h.shape[0]
    nt = S // tm

    def body(*refs):
        h_ref, p_refs, o_ref, acc = refs[0], refs[1:1 + N_DEV], refs[1 + N_DEV], refs[2 + N_DEV]
        j, i = pl.program_id(0), pl.program_id(1)

        @pl.when(i == 0)
        def _():
            acc[...] = jnp.zeros_like(acc)

        for k in range(N_DEV):
            @pl.when(j == k)
            def _(k=k):
                acc[...] += _mm(h_ref[...], p_refs[k][...], TN)

        @pl.when(i == nt - 1)
        def _():
            o_ref[0] = acc[...].astype(XCHG_DTYPE)

    def piece(k):
        return pl.BlockSpec((tm, D), lambda j, i: (jnp.where(j == k, i, 0), 0))

    return pl.pallas_call(
        body, name="gw_in",
        grid=(N_DEV, nt),
        in_specs=[pl.BlockSpec((tm, D), lambda j, i: (i, 0))] + [piece(k) for k in range(N_DEV)],
        out_specs=pl.BlockSpec((1, D, D), lambda j, i: (j, 0, 0)),
        out_shape=jax.ShapeDtypeStruct((N_DEV, D, D), XCHG_DTYPE),
        scratch_shapes=[pltpu.VMEM((D, D), F32)],
        compiler_params=_params(("parallel", "arbitrary")),
    )(h, *pieces)


def _adamw_math(w, g, m, v):
    m = ADAM_B1 * m + (1.0 - ADAM_B1) * g
    v = ADAM_B2 * v + (1.0 - ADAM_B2) * (g * g)
    m_hat = m / (1.0 - ADAM_B1 ** ADAM_STEP)
    v_hat = v / (1.0 - ADAM_B2 ** ADAM_STEP)
    delta = -ADAM_LR * (m_hat / (jnp.sqrt(v_hat) + ADAM_EPS) + ADAM_WD * w)
    return delta, m, v


def _adamw_big(recv, w, m, v, name, tr=128):
    M, N = w.shape
    tr = min(tr, M)

    def body(r_ref, w_ref, m_ref, v_ref, g_ref, d_ref, nm_ref, nv_ref):
        g = r_ref[0].astype(F32)
        for j in range(1, N_DEV):
            g = g + r_ref[j].astype(F32)
        g_ref[...] = g
        d_ref[...], nm_ref[...], nv_ref[...] = _adamw_math(w_ref[...], g, m_ref[...], v_ref[...])

    blk = pl.BlockSpec((tr, N), lambda i: (i, 0))
    return pl.pallas_call(
        body, name=name,
        grid=(M // tr,),
        in_specs=[pl.BlockSpec((N_DEV, tr, N), lambda i: (0, i, 0)), blk, blk, blk],
        out_specs=[blk] * 4,
        out_shape=[jax.ShapeDtypeStruct((M, N), F32)] * 4,
        compiler_params=_params(("parallel",)),
    )(recv, w, m, v)


def _adamw_w_ada(c64, dmod64, w, m, v):
    def body(c_ref, dm_ref, w_ref, m_ref, v_ref, g_ref, d_ref, nm_ref, nv_ref):
        cv = c_ref[...]
        g = _mm(cv * _sigmoid(cv), dm_ref[...], TN)
        g_ref[...] = g
        d_ref[...], nm_ref[...], nv_ref[...] = _adamw_math(w_ref[...], g, m_ref[...], v_ref[...])

    return pl.pallas_call(
        body, name="adamw_w_ada",
        out_shape=[jax.ShapeDtypeStruct(w.shape, F32)] * 4,
        compiler_params=_params(),
    )(c64, dmod64, w, m, v)


P_MOD, P_NORM, P_ONORM, P_RELB, P_LB, P_FINAL, P_LOSS, P_END = (0, 3 * D, 4 * D, 5 * D, 6 * D, 7 * D, 8 * D, 9 * D)


def _adamw_small(packed, b_ada, norm_g, onorm_g, relb, hgrn_lb, final_g, ms, vs):
    def body(pk_ref, b_ref, ng_ref, og_ref, rb_ref, lb_ref, fg_ref,
             mb, mn, mo, mr, ml, mf, vb, vn, vo, vr, vl, vf,
             loss_ref, gb, gn, go, gr, gl, gf, db, dn, do, dr, dl, df,
             nmb, nmn, nmo, nmr, nml, nmf, nvb, nvn, nvo, nvr, nvl, nvf):
        tot = pk_ref[0:1, :]
        for j in range(1, N_DEV):
            tot = tot + pk_ref[8 * j:8 * j + 1, :]
        loss_ref[...] = jnp.broadcast_to(jnp.sum(tot[:, P_LOSS:P_END], axis=-1, keepdims=True), (8, 128))

        def upd(g, w_ref, m_ref, v_ref, g_out, d_out, m_out, v_out):
            g_out[...] = g
            d_out[...], m_out[...], v_out[...] = _adamw_math(w_ref[...], g, m_ref[...], v_ref[...])

        upd(tot[:, P_MOD:P_NORM], b_ref, mb, vb, gb, db, nmb, nvb)
        upd(tot[:, P_NORM:P_ONORM], ng_ref, mn, vn, gn, dn, nmn, nvn)
        g_on = tot[:, P_ONORM:P_ONORM + G_DK]
        for h in range(1, G_HEADS):
            g_on = g_on + tot[:, P_ONORM + G_DK * h:P_ONORM + G_DK * (h + 1)]
        upd(g_on, og_ref, mo, vo, go, do, nmo, nvo)
        upd(tot[:, P_RELB:P_LB], rb_ref, mr, vr, gr, dr, nmr, nvr)
        a = lb_ref[...]
        lb = _sigmoid(a[0:1, :] - a[1:2, :])
        g0 = tot[:, P_LB:P_FINAL] * lb * (1.0 - lb)
        row = lax.broadcasted_iota(jnp.int32, (2, D), 0)
        upd(jnp.where(row == 0, g0, -g0), lb_ref, ml, vl, gl, dl, nml, nvl)
        upd(tot[:, P_FINAL:P_LOSS], fg_ref, mf, vf, gf, df, nmf, nvf)

    shapes = [b_ada.shape, norm_g.shape, onorm_g.shape, relb.shape, hgrn_lb.shape, final_g.shape]
    outs = [jax.ShapeDtypeStruct((8, 128), F32)] + [jax.ShapeDtypeStruct(s, F32) for s in shapes] * 4
    return pl.pallas_call(
        body, name="adamw_small",
        out_shape=outs,
        compiler_params=_params(),
    )(packed, b_ada, norm_g, onorm_g, relb, hgrn_lb, final_g, *ms, *vs)


def _local_step(x, target, mod3, norm_g, w_in_g, onorm_g, wa, wb, wo, rel_bias, hgrn_lb, final_g):
    buckets = jnp.asarray(_bucket_tables())
    bias = _bias_tables(rel_bias, buckets)
    proj, h = _inproj(x, mod3, norm_g, w_in_g)
    os, ls = [], []
    for p, (_, d) in enumerate(PATTERNS):
        o, l = _attn_fwd(proj, bias[p], d, "attn_fwd_d%d" % d)
        os.append(o)
        ls.append(l)
    ao, lt, oa = _attn_combine(os, ls, proj)
    o_raw, ob, states = _hgrn_fwd(proj, hgrn_lb, onorm_g)
    dx2, doa, dob, dga, dgb, tsums, gwa, gwb, gwo = _tail(x, target, oa, ob, proj, mod3, final_g, wa, wb, wo)
    do, dza, delta = _attn_pre_bwd(doa, ao, proj)
    dqs, dks, dvs, dbs = [], [], [], []
    for p, (_, d) in enumerate(PATTERNS):
        dq, dk, dv, db = _attn_bwd(proj, do, lt, delta, bias[p], d, "attn_bwd_d%d" % d)
        dqs.append(dq)
        dks.append(dk)
        dvs.append(dv)
        dbs.append(db)
    p0, p1 = _attn_assemble(dqs, dks, dvs, dza)
    g_relb = _rel_bias_grad(dbs, buckets)
    dqb, dfb, dib, dzb, dlb, dgo = _hgrn_bwd(proj, o_raw, dob, states, hgrn_lb, onorm_g)
    pieces = [p0, p1, dqb, dfb, dib, dzb, dga, dgb]
    grads = [_gw_in(h, pieces),
             gwa.astype(XCHG_DTYPE).reshape(A_W, N_DEV, D // N_DEV).transpose(1, 0, 2),
             gwb.astype(XCHG_DTYPE).reshape(N_DEV, D // N_DEV, D),
             gwo.astype(XCHG_DTYPE).reshape(N_DEV, D // N_DEV, D)]
    gx, hsums, *received = _dh(pieces, w_in_g, x, dx2, mod3, norm_g, grads)
    row = jnp.concatenate([
        hsums[0], hsums[1], tsums[1],
        hsums[2],
        dgo.reshape(G_HEADS, 8, G_DK)[:, 0].reshape(-1),
        g_relb.reshape(-1),
        dlb[0],
        tsums[0],
        tsums[2],
    ])
    return gx, received, row


def kernel(x, c, w_ada, b_ada, norm_g, w_in, hgrn_onorm_g, w_branch_a, w_branch_b, w_out, rel_bias, hgrn_lb, final_g, loss_target, m_w_ada, m_b_ada, m_norm_g, m_w_in, m_hgrn_onorm_g, m_w_branch_a, m_w_branch_b, m_w_out, m_rel_bias, m_hgrn_lb, m_final_g, v_w_ada, v_b_ada, v_norm_g, v_w_in, v_hgrn_onorm_g, v_w_branch_a, v_w_branch_b, v_w_out, v_rel_bias, v_hgrn_lb, v_final_g):
    me = 4 * lax.axis_index("x") + 2 * lax.axis_index("y") + lax.axis_index("c")
    n_ada = w_ada.shape[2]

    w_in_g, wa_g, wb_g, wo_g = _all_gather(
        [w_in[0].astype(MXU_DTYPE), w_branch_a[0].astype(MXU_DTYPE),
         w_branch_b[0].astype(MXU_DTYPE), w_out[0].astype(MXU_DTYPE)], "gather_weights")
    wa = wa_g.transpose(1, 0, 2).reshape(A_W, D)
    wb = wb_g.reshape(D, D)
    wo = wo_g.reshape(D, D)

    (c_all,) = _all_gather([jnp.broadcast_to(c, (8, D))], "gather_c")
    c64 = c_all.reshape(8 * N_DEV, D)
    b_loc = lax.dynamic_slice(b_ada, (0, me * n_ada), (1, n_ada))
    mod_part = _mod_fwd(c64, w_ada[0], b_loc)[::8]
    (mod_all,) = _all_gather([mod_part], "gather_mod")
    mod = lax.dynamic_slice(mod_all, (0, me, 0), (N_DEV, 1, n_ada)).reshape(3, D)
    mod3 = jnp.concatenate([mod, jnp.zeros((5, D), F32)], axis=0)

    onorm_t = hgrn_onorm_g
    gx, (r_in, r_a, r_b, r_o), row = _local_step(
        x[0], loss_target[0], mod3, norm_g, w_in_g, onorm_t, wa, wb, wo, rel_bias, hgrn_lb,
        final_g.reshape(1, D))
    packed8 = jnp.concatenate([row[None, :], jnp.zeros((7, P_END), F32)], axis=0)
    (packed,) = _all_gather([packed8], "gather_small")
    packed = packed.reshape(8 * N_DEV, P_END)

    g_in, d_in, nm_in, nv_in = _adamw_big(r_in, w_in[0], m_w_in[0], v_w_in[0], "adamw_w_in")
    g_a, d_a, nm_a, nv_a = _adamw_big(r_a, w_branch_a[0], m_w_branch_a[0], v_w_branch_a[0], "adamw_w_branch_a")
    g_b, d_b, nm_b, nv_b = _adamw_big(r_b, w_branch_b[0], m_w_branch_b[0], v_w_branch_b[0], "adamw_w_branch_b")
    g_o, d_o, nm_o, nv_o = _adamw_big(r_o, w_out[0], m_w_out[0], v_w_out[0], "adamw_w_out")

    dmod64 = lax.dynamic_slice(packed, (0, P_MOD + me * n_ada), (8 * N_DEV, n_ada))
    g_ada, d_ada, nm_ada, nv_ada = _adamw_w_ada(c64, dmod64, w_ada[0], m_w_ada[0], v_w_ada[0])

    def flat_relb(t):
        return jnp.pad(t.T, ((0, 0), (0, 128 - N_BUCKETS))).reshape(1, A_HEADS * 128)

    def unflat_relb(t):
        return t.reshape(A_HEADS, 128)[:, :N_BUCKETS].T

    fg2 = lambda t: t.reshape(1, D)
    smalls = _adamw_small(
        packed, b_ada, norm_g, hgrn_onorm_g, flat_relb(rel_bias), hgrn_lb, fg2(final_g),
        [m_b_ada, m_norm_g, m_hgrn_onorm_g, flat_relb(m_rel_bias), m_hgrn_lb, fg2(m_final_g)],
        [v_b_ada, v_norm_g, v_hgrn_onorm_g, flat_relb(v_rel_bias), v_hgrn_lb, fg2(v_final_g)])
    loss = smalls[0][0, 0]

    def small(kind):
        s = smalls[1 + 6 * kind:7 + 6 * kind]
        return s[0], s[1], s[2], unflat_relb(s[3]), s[4], s[5].reshape(D)

    def leaves(ada, sm, w_in_, wa_, wb_, wo_):
        b_, n_, o_, r_, l_, f_ = sm
        return (ada[None], b_, n_, w_in_[None], o_, wa_[None], wb_[None], wo_[None], r_, l_, f_)

    return (loss, gx[None],
            *leaves(g_ada, small(0), g_in, g_a, g_b, g_o),
            *leaves(d_ada, small(1), d_in, d_a, d_b, d_o),
            *leaves(nm_ada, small(2), nm_in, nm_a, nm_b, nm_o),
            *leaves(nv_ada, small(3), nv_in, nv_a, nv_b, nv_o))
```

```python
import functools
import math

import numpy as np
import jax
import jax.numpy as jnp
from jax import lax
from jax.experimental import pallas as pl
from jax.experimental.pallas import tpu as pltpu

F32 = jnp.float32
BF16 = jnp.bfloat16
MXU_DTYPE = jnp.bfloat16
XCHG_DTYPE = jnp.bfloat16

N_DEV = 8
D = 1024
A_HEADS = 8
A_HD = 64
A_W = A_HEADS * A_HD
A_BLK = 128
PATTERNS = ((128, 1), (512, 4), (2048, 16))
N_BUCKETS = 32
MAX_DISTANCE = 2048
NEG = -1e30
G_HEADS = 8
G_DK = 128
G_W = G_HEADS * G_DK
IN_W = 8 * D
EPS = 1e-6
ADAM_LR = 0.001
ADAM_B1 = 0.9
ADAM_B2 = 0.999
ADAM_EPS = 1e-08
ADAM_WD = 0.01
ADAM_STEP = 10

G_CHUNK = 128
G_SUB = 16
G_HPS_FWD = 4
G_HPS_BWD = 4
G_RB = 16
VMEM_LIMIT = 56 * 1024 * 1024

NN = (((1,), (0,)), ((), ()))
NT = (((1,), (1,)), ((), ()))
TN = (((0,), (0,)), ((), ()))
MESH = pl.DeviceIdType.MESH


def _mm(a, b, dims=NN):
    return lax.dot_general(a.astype(MXU_DTYPE), b.astype(MXU_DTYPE), dims,
                           preferred_element_type=F32)


def _mm_exact(t, x):
    hi = x.astype(BF16)
    r = x - hi.astype(F32)
    mid = r.astype(BF16)
    lo = (r - mid.astype(F32)).astype(BF16)
    tb = t.astype(BF16)
    return sum(lax.dot_general(tb, p, NN, preferred_element_type=F32) for p in (hi, mid, lo))


def _sigmoid(x):
    return 0.5 * jnp.tanh(0.5 * x) + 0.5


def _params(sem=None):
    return pltpu.CompilerParams(dimension_semantics=sem, vmem_limit_bytes=VMEM_LIMIT)


def _all_gather(xs, name):
    n = len(xs)

    def body(*refs):
        ins, outs = refs[:n], refs[n:2 * n]
        send_sems, recv_sems, local_sems = refs[2 * n:]
        x, y, c = lax.axis_index("x"), lax.axis_index("y"), lax.axis_index("c")
        me, sibling = (x, y, c), (x, y, 1 - c)
        chips = [(1 - x, y), (x, 1 - y), (1 - x, 1 - y)]

        def slot(ref, dev):
            return ref.at[4 * dev[0] + 2 * dev[1] + dev[2]]

        def copy(a, k, block, to, src=None):
            return pltpu.make_async_remote_copy(
                src_ref=slot(outs[a], block) if src is None else src,
                dst_ref=slot(outs[a], block),
                send_sem=send_sems.at[a, k], recv_sem=recv_sems.at[a, k],
                device_id=to, device_id_type=MESH)

        mine, first, passed = [], [], []
        for a in range(n):
            cp = pltpu.make_async_copy(ins[a], slot(outs[a], me), local_sems.at[a])
            cp.start()
            mine.append(cp)
            first.append(copy(a, 0, me, sibling, src=ins[a]))
            for j, chip in enumerate(chips):
                first.append(copy(a, 1 + j, me, (*chip, c), src=ins[a]))
        for cp in first:
            cp.start()
        for j, chip in enumerate(chips):
            for a in range(n):
                copy(a, 1 + j, (*chip, c), me).wait_recv()
                cp = copy(a, 4 + j, (*chip, c), sibling)
                cp.start()
                passed.append(cp)
        for a in range(n):
            copy(a, 0, sibling, me).wait_recv()
            for j, chip in enumerate(chips):
                copy(a, 4 + j, (*chip, 1 - c), me).wait_recv()
        for cp in first + passed:
            cp.wait_send()
        for cp in mine:
            cp.wait()

    any_spec = pl.BlockSpec(memory_space=pl.ANY)
    return pl.pallas_call(
        body, name=name,
        out_shape=[jax.ShapeDtypeStruct((N_DEV,) + v.shape, v.dtype) for v in xs],
        in_specs=[any_spec] * n, out_specs=[any_spec] * n,
        scratch_shapes=[pltpu.SemaphoreType.DMA((n, 7)), pltpu.SemaphoreType.DMA((n, 7)),
                        pltpu.SemaphoreType.DMA((n,))],
    )(*xs)


def _all_to_all_copies(ins, outs, send_sems, recv_sems, local_sems):
    n = len(ins)
    x, y, c = lax.axis_index("x"), lax.axis_index("y"), lax.axis_index("c")
    me = 4 * x + 2 * y + c
    peers = []
    for m in range(1, N_DEV):
        peers.append((1 - x if m & 4 else x, 1 - y if m & 2 else y, 1 - c if m & 1 else c))

    def copy(a, k, landing):
        peer = peers[k]
        pid = 4 * peer[0] + 2 * peer[1] + peer[2]
        return pltpu.make_async_remote_copy(
            src_ref=ins[a].at[pid], dst_ref=outs[a].at[pid if landing else me],
            send_sem=send_sems.at[a, k], recv_sem=recv_sems.at[a, k],
            device_id=peer, device_id_type=MESH)

    def local(a):
        return pltpu.make_async_copy(ins[a].at[me], outs[a].at[me], local_sems.at[a])

    def start():
        for a in range(n):
            local(a).start()
        for k in range(N_DEV - 1):
            for a in range(n):
                copy(a, k, False).start()

    def wait():
        for k in range(N_DEV - 1):
            for a in range(n):
                copy(a, k, True).wait_recv()
        for k in range(N_DEV - 1):
            for a in range(n):
                copy(a, k, False).wait_send()
        for a in range(n):
            local(a).wait()

    return start, wait


def _mod_fwd(c64, w_ada, b_loc):
    def body(c_ref, w_ref, b_ref, o_ref):
        cv = c_ref[...]
        sc = cv * _sigmoid(cv)
        o_ref[...] = _mm(sc, w_ref[...]) + b_ref[...]

    return pl.pallas_call(
        body, name="mod_fwd",
        out_shape=jax.ShapeDtypeStruct((c64.shape[0], w_ada.shape[1]), F32),
        compiler_params=_params(),
    )(c64, w_ada, b_loc)


def _inproj(x, mod3, norm_g, w_in_g, tm=1024):
    S = x.shape[0]

    def body(x_ref, mod_ref, g_ref, w_ref, proj_ref, h_ref, hs):
        @pl.when(pl.program_id(1) == 0)
        def _():
            xv = x_ref[...]
            r = lax.rsqrt(jnp.mean(xv * xv, axis=-1, keepdims=True) + EPS)
            h = (xv * r * g_ref[...]) * (1.0 + mod_ref[1:2, :]) + mod_ref[0:1, :]
            hs[...] = h.astype(MXU_DTYPE)
            h_ref[...] = h.astype(MXU_DTYPE)
        proj_ref[...] = _mm(hs[...], w_ref[0])

    return pl.pallas_call(
        body, name="inproj",
        grid=(S // tm, N_DEV),
        in_specs=[pl.BlockSpec((tm, D), lambda i, j: (i, 0)),
                  pl.BlockSpec((8, D), lambda i, j: (0, 0)),
                  pl.BlockSpec((1, D), lambda i, j: (0, 0)),
                  pl.BlockSpec((1, D, D), lambda i, j: (j, 0, 0))],
        out_specs=[pl.BlockSpec((tm, D), lambda i, j: (i, j)),
                   pl.BlockSpec((tm, D), lambda i, j: (i, 0))],
        out_shape=[jax.ShapeDtypeStruct((S, IN_W), F32), jax.ShapeDtypeStruct((S, D), MXU_DTYPE)],
        scratch_shapes=[pltpu.VMEM((tm, D), MXU_DTYPE)],
        compiler_params=_params(("parallel", "arbitrary")),
    )(x, mod3, norm_g, w_in_g)


def _bucket_tables():
    qi = np.arange(A_BLK)[:, None]
    kj = np.arange(2 * A_BLK)[None, :]
    delta = qi + A_BLK - kj
    out = []
    for window, dil in PATTERNS:
        span = window // dil
        band = (delta >= 0) & (delta <= span)
        dist = np.clip(delta, 0, None) * dil
        max_exact = N_BUCKETS // 2
        nf = dist.astype(np.float32)
        large = max_exact + (np.log(np.maximum(nf, np.float32(1.0)) / np.float32(max_exact))
                             / np.float32(math.log(MAX_DISTANCE / max_exact))
                             * np.float32(N_BUCKETS - max_exact)).astype(np.int32)
        large = np.minimum(large, N_BUCKETS - 1)
        bucket = np.where(dist < max_exact, dist, large)
        out.append(np.where(band, bucket, -1).astype(np.int32))
    return np.stack(out)


def _bias_tables(rel_bias, buckets):
    def body(rb_ref, bk_ref, o_ref):
        h = pl.program_id(1)
        bk = bk_ref[0]
        acc = jnp.full(bk.shape, NEG, F32)
        for b in range(N_BUCKETS):
            acc = jnp.where(bk == b, rb_ref[b, h], acc)
        o_ref[0, 0] = acc

    return pl.pallas_call(
        body, name="bias_tables",
        grid=(3, A_HEADS),
        in_specs=[pl.BlockSpec(memory_space=pltpu.SMEM),
                  pl.BlockSpec((1, A_BLK, 2 * A_BLK), lambda p, h: (p, 0, 0))],
        out_specs=pl.BlockSpec((1, 1, A_BLK, 2 * A_BLK), lambda p, h: (p, h, 0, 0)),
        out_shape=jax.ShapeDtypeStruct((3, A_HEADS, A_BLK, 2 * A_BLK), F32),
        compiler_params=_params(("arbitrary", "arbitrary")),
    )(rel_bias, buckets)


A_TILES = 16


def _attn_heads_per_step(d):
    return A_HEADS if d == 1 else 2


def _attn_in_specs(sb, nsb, hw):
    w = A_HD * hw
    per = A_W // w

    def cur(col):
        return pl.BlockSpec((sb, w), lambda hp, n: (jnp.minimum(n, nsb - 1), per * col + hp))

    def prev(col):
        return pl.BlockSpec((sb, w), lambda hp, n: (jnp.maximum(jnp.minimum(n, nsb - 1) - 1, 0), per * col + hp))

    return [cur(0), prev(1), cur(1), prev(2), cur(2)]


def _rows(r, d):
    return pl.ds(r, A_BLK) if d == 1 else pl.ds(r, A_BLK, stride=d)


def _for_residues(d, hw, fn):
    unroll = min(d, max(1, A_TILES // hw))
    if d == unroll:
        for r in range(d):
            fn(r)
    else:
        def group(g, c):
            for u in range(unroll):
                fn(g * unroll + u)
            return c
        lax.fori_loop(0, d // unroll, group, 0)


def _attn_stack(t):
    first_half = lax.broadcasted_iota(jnp.int32, (1, 2 * A_HD), 1) < A_HD
    return jnp.concatenate([jnp.where(first_half, t, 0.0), jnp.where(first_half, 0.0, t)], axis=0)


def _attn_unstack(t2):
    first_half = lax.broadcasted_iota(jnp.int32, (1, 2 * A_HD), 1) < A_HD
    return jnp.where(first_half, t2[:A_BLK], t2[A_BLK:])


def _attn_scores(q, k, b_ref, pp, first):
    bias = jnp.concatenate([b_ref[2 * pp] + first, b_ref[2 * pp + 1] + first], axis=0)
    return _mm(_attn_stack(q), k, NT) * (A_HD ** -0.5) + bias


def _attn_fwd(proj, bias_p, d, name):
    S = proj.shape[0]
    sb = A_BLK * d
    nsb = S // sb
    hw = _attn_heads_per_step(d)

    def body(q_ref, kp_ref, kc_ref, vp_ref, vc_ref, b_ref, o_ref, l_ref):
        n = pl.program_id(1)
        kj = lax.broadcasted_iota(jnp.int32, (A_BLK, 2 * A_BLK), 1)
        first = jnp.where((n == 0) & (kj < A_BLK), NEG, 0.0).astype(F32)

        def residue(r):
            rows = _rows(r, d)
            for pp in range(hw // 2):
                lanes = pl.ds(2 * A_HD * pp, 2 * A_HD)
                k = jnp.concatenate([kp_ref[rows, lanes], kc_ref[rows, lanes]], axis=0)
                v = jnp.concatenate([vp_ref[rows, lanes], vc_ref[rows, lanes]], axis=0)
                s = _attn_scores(q_ref[rows, lanes], k, b_ref, pp, first)
                m = jnp.max(s, axis=-1, keepdims=True)
                p = jnp.exp(s - m)
                den = jnp.sum(p, axis=-1, keepdims=True)
                o_ref[rows, lanes] = _attn_unstack(_mm(p, v) / den)
                l_ref[rows, lanes] = _attn_unstack(jnp.broadcast_to(m + jnp.log(den), (2 * A_BLK, 2 * A_HD)))

        _for_residues(d, hw, residue)

    out = pl.BlockSpec((sb, A_HD * hw), lambda hp, n: (n, hp))
    return pl.pallas_call(
        body, name=name,
        grid=(A_HEADS // hw, nsb),
        in_specs=_attn_in_specs(sb, nsb, hw) + [pl.BlockSpec((hw, A_BLK, 2 * A_BLK), lambda hp, n: (hp, 0, 0))],
        out_specs=[out, out],
        out_shape=[jax.ShapeDtypeStruct((S, A_W), F32)] * 2,
        compiler_params=_params(("parallel", "parallel")),
    )(proj, proj, proj, proj, proj, bias_p)


def _attn_combine(os, ls, proj, tm=512):
    S = proj.shape[0]

    def body(o1, o2, o3, l1, l2, l3, z_ref, ao_ref, lt_ref, oa_ref):
        a1, a2, a3 = l1[...], l2[...], l3[...]
        m = jnp.maximum(jnp.maximum(a1, a2), a3)
        e1, e2, e3 = jnp.exp(a1 - m), jnp.exp(a2 - m), jnp.exp(a3 - m)
        den = e1 + e2 + e3
        ao = (e1 * o1[...] + e2 * o2[...] + e3 * o3[...]) / den
        z = z_ref[...]
        ao_ref[...] = ao
        lt_ref[...] = m + jnp.log(den)
        oa_ref[...] = (ao * (z * _sigmoid(z))).astype(MXU_DTYPE)

    spec = pl.BlockSpec((tm, A_W), lambda i: (i, 0))
    return pl.pallas_call(
        body, name="attn_combine",
        grid=(S // tm,),
        in_specs=[spec] * 6 + [pl.BlockSpec((tm, A_W), lambda i: (i, 3))],
        out_specs=[spec] * 3,
        out_shape=[jax.ShapeDtypeStruct((S, A_W), F32), jax.ShapeDtypeStruct((S, A_W), F32),
                   jax.ShapeDtypeStruct((S, A_W), MXU_DTYPE)],
        compiler_params=_params(("parallel",)),
    )(*os, *ls, proj)


def _attn_pre_bwd(doa, ao, proj, tm=512):
    S = proj.shape[0]

    def body(doa_ref, ao_ref, z_ref, do_ref, dz_ref, dl_ref):
        z = z_ref[...]
        sg = _sigmoid(z)
        g = doa_ref[...]
        ao_v = ao_ref[...]
        do = g * (z * sg)
        do_ref[...] = do
        dz_ref[...] = (g * ao_v * (sg * (1.0 + z * (1.0 - sg)))).astype(MXU_DTYPE)
        prod = do * ao_v
        for h in range(A_HEADS):
            sl = slice(A_HD * h, A_HD * (h + 1))
            dl_ref[:, sl] = jnp.broadcast_to(jnp.sum(prod[:, sl], axis=-1, keepdims=True), (tm, A_HD))

    spec = pl.BlockSpec((tm, A_W), lambda i: (i, 0))
    return pl.pallas_call(
        body, name="attn_pre_bwd",
        grid=(S // tm,),
        in_specs=[spec, spec, pl.BlockSpec((tm, A_W), lambda i: (i, 3))],
        out_specs=[spec] * 3,
        out_shape=[jax.ShapeDtypeStruct((S, A_W), F32), jax.ShapeDtypeStruct((S, A_W), MXU_DTYPE),
                   jax.ShapeDtypeStruct((S, A_W), F32)],
        compiler_params=_params(("parallel",)),
    )(doa, ao, proj)


def _attn_bwd(proj, do, lt, delta, bias_p, d, name):
    S = proj.shape[0]
    sb = A_BLK * d
    nsb = S // sb
    hw = _attn_heads_per_step(d)

    def body(q_ref, kp_ref, kc_ref, vp_ref, vc_ref, do_ref, lt_ref, dl_ref, b_ref,
             dq_ref, dk_ref, dv_ref, db_ref, ck, cv):
        n = pl.program_id(1)

        @pl.when(n == 0)
        def _():
            db_ref[...] = jnp.zeros_like(db_ref)
            ck[...] = jnp.zeros_like(ck)
            cv[...] = jnp.zeros_like(cv)

        @pl.when(n < nsb)
        def _():
            kj = lax.broadcasted_iota(jnp.int32, (A_BLK, 2 * A_BLK), 1)
            first = jnp.where((n == 0) & (kj < A_BLK), NEG, 0.0).astype(F32)

            def residue(r):
                rows = _rows(r, d)
                for pp in range(hw // 2):
                    lanes = pl.ds(2 * A_HD * pp, 2 * A_HD)
                    lt_r, dl_r = lt_ref[rows, lanes], dl_ref[rows, lanes]
                    k = jnp.concatenate([kp_ref[rows, lanes], kc_ref[rows, lanes]], axis=0)
                    v = jnp.concatenate([vp_ref[rows, lanes], vc_ref[rows, lanes]], axis=0)
                    q2 = _attn_stack(q_ref[rows, lanes])
                    do2 = _attn_stack(do_ref[rows, lanes])
                    col = lambda t: jnp.concatenate([t[:, 0:1], t[:, A_HD:A_HD + 1]], axis=0)
                    s = _attn_scores(q_ref[rows, lanes], k, b_ref, pp, first)
                    p = jnp.exp(s - col(lt_r))
                    ds = p * (_mm(do2, v, NT) - col(dl_r))
                    db_ref[2 * pp] += ds[:A_BLK]
                    db_ref[2 * pp + 1] += ds[A_BLK:]
                    dq_ref[rows, lanes] = _attn_unstack(_mm(ds, k)) * (A_HD ** -0.5)
                    dk = _mm(ds, q2, TN) * (A_HD ** -0.5)
                    dv = _mm(p, do2, TN)
                    dk_ref[rows, lanes] = ck[rows, lanes] + dk[:A_BLK]
                    dv_ref[rows, lanes] = cv[rows, lanes] + dv[:A_BLK]
                    ck[rows, lanes] = dk[A_BLK:]
                    cv[rows, lanes] = dv[A_BLK:]

            _for_residues(d, hw, residue)

        @pl.when(n == nsb)
        def _():
            dk_ref[...] = ck[...]
            dv_ref[...] = cv[...]

    w = A_HD * hw
    row = pl.BlockSpec((sb, w), lambda hp, n: (jnp.minimum(n, nsb - 1), hp))
    lag = pl.BlockSpec((sb, w), lambda hp, n: (jnp.maximum(n - 1, 0), hp))
    tab = pl.BlockSpec((hw, A_BLK, 2 * A_BLK), lambda hp, n: (hp, 0, 0))
    return pl.pallas_call(
        body, name=name,
        grid=(A_HEADS // hw, nsb + 1),
        in_specs=_attn_in_specs(sb, nsb, hw) + [row, row, row, tab],
        out_specs=[row, lag, lag, tab],
        out_shape=[jax.ShapeDtypeStruct((S, A_W), F32)] * 3
                  + [jax.ShapeDtypeStruct((A_HEADS, A_BLK, 2 * A_BLK), F32)],
        scratch_shapes=[pltpu.VMEM((sb, w), F32), pltpu.VMEM((sb, w), F32)],
        compiler_params=_params(("parallel", "arbitrary")),
    )(proj, proj, proj, proj, proj, do, lt, delta, bias_p)


def _attn_assemble(dqs, dks, dvs, dz, tm=512):
    S = dz.shape[0]

    def body(q1, q2, q3, k1, k2, k3, v1, v2, v3, z_ref, p0_ref, p1_ref):
        p0_ref[:, :A_W] = (q1[...] + q2[...] + q3[...]).astype(MXU_DTYPE)
        p0_ref[:, A_W:] = (k1[...] + k2[...] + k3[...]).astype(MXU_DTYPE)
        p1_ref[:, :A_W] = (v1[...] + v2[...] + v3[...]).astype(MXU_DTYPE)
        p1_ref[:, A_W:] = z_ref[...]

    spec = pl.BlockSpec((tm, A_W), lambda i: (i, 0))
    wide = pl.BlockSpec((tm, 2 * A_W), lambda i: (i, 0))
    return pl.pallas_call(
        body, name="attn_assemble",
        grid=(S // tm,),
        in_specs=[spec] * 10,
        out_specs=[wide, wide],
        out_shape=[jax.ShapeDtypeStruct((S, 2 * A_W), MXU_DTYPE)] * 2,
        compiler_params=_params(("parallel",)),
    )(*dqs, *dks, *dvs, dz)


def _rel_bias_grad(dbs, buckets):
    def body(d1, d2, d3, bk_ref, o_ref):
        row = lax.broadcasted_iota(jnp.int32, (A_HEADS, 128), 0)
        lane = lax.broadcasted_iota(jnp.int32, (A_HEADS, 128), 1)
        acc = jnp.zeros((A_HEADS, 128), F32)
        for p, dref in enumerate((d1, d2, d3)):
            bk = bk_ref[p]
            for h in range(A_HEADS):
                ds = dref[h]
                for b in range(N_BUCKETS):
                    s = jnp.sum(jnp.where(bk == b, ds, 0.0), keepdims=True)
                    acc = acc + jnp.where((row == h) & (lane == b), s, 0.0)
        o_ref[...] = acc

    return pl.pallas_call(
        body, name="rel_bias_grad",
        out_shape=jax.ShapeDtypeStruct((A_HEADS, 128), F32),
        compiler_params=_params(),
    )(*dbs, buckets)


def _tri(c):
    t = np.tril(np.ones((c, c), np.float32))
    return jnp.asarray(t), jnp.asarray(t.T.copy())


def _fill_above(ref, x, pad):
    ref[0:G_SUB, :] = jnp.full((G_SUB, x.shape[1]), pad, F32)
    ref[G_SUB:, :] = x


def _fill_below(ref, x, pad):
    ref[0:x.shape[0], :] = x
    ref[x.shape[0]:, :] = jnp.full((G_SUB, x.shape[1]), pad, F32)


def _hgrn_gates(q_ref, f_ref, lbp_ref, tri_ref):
    qraw = q_ref[...]
    sq = _sigmoid(qraw)
    q = qraw * sq
    sg = _sigmoid(f_ref[...])
    lb = _sigmoid(lbp_ref[0:1, :] - lbp_ref[1:2, :])
    f = lb + (1.0 - lb) * sg
    k = 1.0 - f
    b = _mm_exact(tri_ref[...], jnp.log(f))
    return qraw, sq, q, sg, lb, f, k, b


def _hgrn_col(C, base, idx, hps):
    return pl.BlockSpec((C, hps * G_DK), lambda h, n: (idx(n), base * (G_HEADS // hps) + h))


def _hgrn_levels(C):
    out, m = [], G_SUB
    while 2 * m <= C:
        out.append(m)
        m *= 2
    return out


def _hgrn_level_masks(C):
    ti = np.arange(C)[:, None]
    si = np.arange(C)[None, :]
    return jnp.asarray(np.stack([((ti // (2 * m) == si // (2 * m)) & (ti - si >= G_SUB)).astype(np.float32)
                                 for m in _hgrn_levels(C)]))


def _hgrn_level(b, q, k, C, m):
    zeros = jnp.zeros((m, G_DK), F32)
    eq, ek, qt, kt = [], [], [], []
    for blk in range(0, C // m, 2):
        lo, mid, hi = blk * m, (blk + 1) * m, (blk + 2) * m
        ref = b[mid:mid + 1]
        e_right = jnp.exp(b[mid:hi] - ref)
        e_left = jnp.exp(ref - b[lo:mid])
        eq += [zeros, e_right]
        ek += [e_left, zeros]
        qt += [zeros, q[mid:hi] * e_right]
        kt += [k[lo:mid] * e_left, zeros]
    cat = lambda parts: jnp.concatenate(parts, axis=0)
    return cat(qt), cat(kt), cat(eq), cat(ek)


def _hgrn_fwd(proj, hgrn_lb, onorm_g, C=G_CHUNK):
    S = proj.shape[0]
    nc = S // C
    tri, _ = _tri(C)
    masks = _hgrn_level_masks(C)
    hps = G_HPS_FWD

    def body(q_ref, f_ref, i_ref, z_ref, lbp_ref, go_ref, tri_ref, pm_ref, o_ref, ob_ref, st_ref, St, kp, vp, fp):
        @pl.when(pl.program_id(1) == 0)
        def _():
            St[...] = jnp.zeros_like(St)

        for hh in range(hps):
            ln = pl.ds(G_DK * hh, G_DK)
            head(q_ref.at[:, ln], f_ref.at[:, ln], i_ref.at[:, ln], z_ref.at[:, ln], lbp_ref.at[:, ln], go_ref,
                 tri_ref, pm_ref, o_ref.at[:, ln], ob_ref.at[:, ln], st_ref.at[0, hh], St.at[hh], kp.at[hh], vp.at[hh],
                 fp.at[hh])

    def head(q_ref, f_ref, i_ref, z_ref, lbp_ref, go_ref, tri_ref, pm_ref, o_ref, ob_ref, st_ref, St, kp, vp, fp):
        _, _, q, _, _, f, k, b = _hgrn_gates(q_ref, f_ref, lbp_ref, tri_ref)
        v = i_ref[...]
        bC = b[C - 1:C, :]
        S0 = St[...]
        o = _mm(q * jnp.exp(b), S0, NT)
        _fill_above(kp, k, 0.0)
        _fill_above(vp, v, 0.0)
        _fill_above(fp, f, 1.0)
        near = []
        for r0 in range(0, C, G_RB):
            qb = q[r0:r0 + G_RB]
            acc = e = None
            for l in range(G_SUB):
                rows = pl.ds(G_SUB - l + r0, G_RB)
                if l > 0:
                    fl = fp[pl.ds(G_SUB - l + 1 + r0, G_RB), :]
                    e = fl if e is None else e * fl
                kl = kp[rows, :]
                a = jnp.sum(qb * kl if e is None else qb * kl * e, axis=-1, keepdims=True)
                t = a * vp[rows, :]
                acc = t if acc is None else acc + t
            near.append(acc)
        o = o + jnp.concatenate(near, axis=0)
        a_off = jnp.zeros((C, C), F32)
        for lv, m in enumerate(_hgrn_levels(C)):
            qt, kt, _, _ = _hgrn_level(b, q, k, C, m)
            a_off = a_off + pm_ref[lv] * _mm(qt, kt, NT)
        o = o + _mm(a_off, v)
        S1 = S0 * jnp.exp(bC) + _mm(v, k * jnp.exp(bC - b), TN)
        St[...] = S1
        st_ref[...] = S1
        o_ref[...] = o
        r = lax.rsqrt(jnp.mean(o * o, axis=-1, keepdims=True) + EPS)
        z = z_ref[...]
        ob_ref[...] = (o * r * go_ref[...] * (z * _sigmoid(z))).astype(MXU_DTYPE)

    ident = lambda n: n
    w = hps * G_DK
    out = pl.BlockSpec((C, w), lambda h, n: (n, h))
    return pl.pallas_call(
        body, name="hgrn_fwd",
        grid=(G_HEADS // hps, nc),
        in_specs=[_hgrn_col(C, base, ident, hps) for base in (2, 3, 4, 5)] + [
                  pl.BlockSpec((2, w), lambda h, n: (0, h)),
                  pl.BlockSpec((1, G_DK), lambda h, n: (0, 0)),
                  pl.BlockSpec((C, C), lambda h, n: (0, 0)),
                  pl.BlockSpec(masks.shape, lambda h, n: (0, 0, 0))],
        out_specs=[out, out, pl.BlockSpec((1, hps, G_DK, G_DK), lambda h, n: (n, h, 0, 0))],
        out_shape=[jax.ShapeDtypeStruct((S, G_W), F32), jax.ShapeDtypeStruct((S, G_W), MXU_DTYPE),
                   jax.ShapeDtypeStruct((nc, G_HEADS, G_DK, G_DK), F32)],
        scratch_shapes=[pltpu.VMEM((hps, G_DK, G_DK), F32)] + [pltpu.VMEM((hps, C + G_SUB, G_DK), F32)] * 3,
        compiler_params=_params(("parallel", "arbitrary")),
    )(proj, proj, proj, proj, hgrn_lb, onorm_g, tri, masks)


def _hgrn_bwd(proj, o_raw, dob, states, hgrn_lb, onorm_g, C=G_CHUNK):
    S = proj.shape[0]
    nc = S // C
    tri, triu = _tri(C)
    masks = _hgrn_level_masks(C)
    hps = G_HPS_BWD

    def body(q_ref, f_ref, i_ref, z_ref, o_ref, dob_ref, s0_ref, s1_ref, lbp_ref, go_ref, tri_ref, triu_ref,
             pm_ref, dq_ref, df_ref, di_ref, dz_ref, dlb_ref, dgo_ref, dSt, *shifted):
        @pl.when(pl.program_id(1) == 0)
        def _():
            dSt[...] = jnp.zeros_like(dSt)
            dlb_ref[...] = jnp.zeros_like(dlb_ref)
            dgo_ref[...] = jnp.zeros_like(dgo_ref)

        for hh in range(hps):
            ln = pl.ds(G_DK * hh, G_DK)
            head(q_ref.at[:, ln], f_ref.at[:, ln], i_ref.at[:, ln], z_ref.at[:, ln], o_ref.at[:, ln],
                 dob_ref.at[:, ln], s0_ref.at[0, hh], s1_ref.at[0, hh], lbp_ref.at[:, ln], go_ref, tri_ref, triu_ref,
                 pm_ref, dq_ref.at[:, ln], df_ref.at[:, ln], di_ref.at[:, ln], dz_ref.at[:, ln], dlb_ref.at[:, ln],
                 dgo_ref.at[pl.ds(8 * hh, 8), :], dSt.at[hh], *[t.at[hh] for t in shifted])

    def head(q_ref, f_ref, i_ref, z_ref, o_ref, dob_ref, s0_ref, s1_ref, lbp_ref, go_ref, tri_ref, triu_ref,
             pm_ref, dq_ref, df_ref, di_ref, dz_ref, dlb_ref, dgo_ref, dSt, kp, vp, fp, qn, dn_, fn, xs, dac):
        cn = nc - 1 - pl.program_id(1)
        qraw, sq, q, sg, lb, f, k, b = _hgrn_gates(q_ref, f_ref, lbp_ref, tri_ref)
        v = i_ref[...]
        bC = b[C - 1:C, :]
        eb = jnp.exp(b)
        ecb = jnp.exp(bC - b)
        o = o_ref[...]
        z = z_ref[...]
        sz = _sigmoid(z)
        go = go_ref[...]
        g_ob = dob_ref[...]
        r = lax.rsqrt(jnp.mean(o * o, axis=-1, keepdims=True) + EPS)
        nh = o * r
        dnrm = g_ob * (z * sz)
        dz_ref[...] = (g_ob * (nh * go) * (sz * (1.0 + z * (1.0 - sz)))).astype(MXU_DTYPE)
        dgo_ref[0:1, :] += jnp.sum(dnrm * nh, axis=0, keepdims=True)
        dn = dnrm * go
        do = r * (dn - nh * jnp.mean(dn * nh, axis=-1, keepdims=True))

        S0 = jnp.where(cn == 0, 0.0, s0_ref[...])
        S1 = s1_ref[...]
        dS1 = dSt[...]
        dq = eb * _mm(do, S0)
        dk = ecb * _mm(v, dS1)
        dv = _mm(k * ecb, dS1, NT)
        bnd = jnp.sum(dS1 * S1, axis=0, keepdims=True)
        dSt[...] = dS1 * jnp.exp(bC) + _mm(do, q * eb, TN)

        _fill_above(kp, k, 0.0)
        _fill_above(vp, v, 0.0)
        _fill_above(fp, f, 1.0)
        _fill_below(qn, q, 0.0)
        _fill_below(dn_, do, 0.0)
        _fill_below(fn, f, 1.0)
        for r0 in range(0, C, G_RB):
            do_b = do[r0:r0 + G_RB]
            for l in range(G_SUB):
                xs[pl.ds(l * C + r0, G_RB), :] = (do_b * vp[pl.ds(G_SUB - l + r0, G_RB), :]).astype(MXU_DTYPE)
        dac[0:G_SUB * C, :] = _mm(xs[...], jnp.ones((G_DK, G_DK), MXU_DTYPE))
        dac[G_SUB * C:, :] = jnp.zeros((G_SUB, G_DK), F32)
        near_q, near_k, near_v = [], [], []
        for r0 in range(0, C, G_RB):
            k_b = k[r0:r0 + G_RB]
            aq = ak = av = e = e2 = None
            for l in range(G_SUB):
                down, up = pl.ds(G_SUB - l + r0, G_RB), pl.ds(l + r0, G_RB)
                if l > 0:
                    fl = fp[pl.ds(G_SUB - l + 1 + r0, G_RB), :]
                    e = fl if e is None else e * fl
                    fu = fn[up, :]
                    e2 = fu if e2 is None else e2 * fu
                kl = kp[down, :]
                t = dac[pl.ds(l * C + r0, G_RB), :] * (kl if e is None else kl * e)
                aq = t if aq is None else aq + t
                qu = qn[up, :]
                qe = qu if e2 is None else qu * e2
                dou = dn_[up, :]
                a2 = jnp.sum(qe * k_b, axis=-1, keepdims=True)
                t = dac[pl.ds(l * C + l + r0, G_RB), :] * qe
                ak = t if ak is None else ak + t
                t = a2 * dou
                av = t if av is None else av + t
            near_q.append(aq)
            near_k.append(ak)
            near_v.append(av)
        dq = dq + jnp.concatenate(near_q, axis=0)
        dk = dk + jnp.concatenate(near_k, axis=0)
        dv = dv + jnp.concatenate(near_v, axis=0)

        da_all = _mm(do, v, NT)
        a_off = jnp.zeros((C, C), F32)
        for lv, m in enumerate(_hgrn_levels(C)):
            qt, kt, eq, ek = _hgrn_level(b, q, k, C, m)
            da_m = pm_ref[lv] * da_all
            a_off = a_off + pm_ref[lv] * _mm(qt, kt, NT)
            dq = dq + _mm(da_m, kt) * eq
            dk = dk + _mm(da_m, qt, TN) * ek
        dv = dv + _mm(a_off, do, TN)

        row = lax.broadcasted_iota(jnp.int32, (C, 1), 0)
        db = q * dq - k * dk + jnp.where(row == C - 1, bnd, 0.0)
        dg = _mm_exact(triu_ref[...], db)
        df = dg / f - dk
        df_ref[...] = (df * (1.0 - lb) * (sg * (1.0 - sg))).astype(MXU_DTYPE)
        dlb_ref[0:1, :] += jnp.sum(df * (1.0 - sg), axis=0, keepdims=True)
        dq_ref[...] = (dq * (sq * (1.0 + qraw * (1.0 - sq)))).astype(MXU_DTYPE)
        di_ref[...] = dv.astype(MXU_DTYPE)

    rev = lambda n: nc - 1 - n
    w = hps * G_DK
    blk = pl.BlockSpec((C, w), lambda h, n: (nc - 1 - n, h))
    return pl.pallas_call(
        body, name="hgrn_bwd",
        grid=(G_HEADS // hps, nc),
        in_specs=[_hgrn_col(C, base, rev, hps) for base in (2, 3, 4, 5)] + [
                  blk, blk,
                  pl.BlockSpec((1, hps, G_DK, G_DK), lambda h, n: (jnp.maximum(nc - 2 - n, 0), h, 0, 0)),
                  pl.BlockSpec((1, hps, G_DK, G_DK), lambda h, n: (nc - 1 - n, h, 0, 0)),
                  pl.BlockSpec((2, w), lambda h, n: (0, h)),
                  pl.BlockSpec((1, G_DK), lambda h, n: (0, 0)),
                  pl.BlockSpec((C, C), lambda h, n: (0, 0)),
                  pl.BlockSpec((C, C), lambda h, n: (0, 0)),
                  pl.BlockSpec(masks.shape, lambda h, n: (0, 0, 0))],
        out_specs=[blk, blk, blk, blk,
                   pl.BlockSpec((8, w), lambda h, n: (0, h)),
                   pl.BlockSpec((8 * hps, G_DK), lambda h, n: (h, 0))],
        out_shape=[jax.ShapeDtypeStruct((S, G_W), MXU_DTYPE)] * 4
                  + [jax.ShapeDtypeStruct((8, G_W), F32), jax.ShapeDtypeStruct((8 * G_HEADS, G_DK), F32)],
        scratch_shapes=[pltpu.VMEM((hps, G_DK, G_DK), F32)] + [pltpu.VMEM((hps, C + G_SUB, G_DK), F32)] * 6
                       + [pltpu.VMEM((hps, G_SUB * C, G_DK), MXU_DTYPE),
                          pltpu.VMEM((hps, G_SUB * C + G_SUB, G_DK), F32)],
        compiler_params=_params(("parallel", "arbitrary")),
    )(proj, proj, proj, proj, o_raw, dob, states, states, hgrn_lb, onorm_g, tri, triu, masks)


def _tail(x, target, oa, ob, proj, mod3, final_g, wa, wb, wo, tm=256):
    S = x.shape[0]
    nt = S // tm

    def body(x_ref, t_ref, oa_ref, ob_ref, ga_ref, gb_ref, mod_ref, fg_ref, wa_ref, wb_ref, wo_ref,
             dx2_ref, doa_ref, dob_ref, dga_ref, dgb_ref, sums_ref, gwa_ref, gwb_ref, gwo_ref,
             acc_a, acc_b, acc_o):
        i = pl.program_id(0)

        @pl.when(i == 0)
        def _():
            sums_ref[...] = jnp.zeros_like(sums_ref)
            acc_a[...] = jnp.zeros_like(acc_a)
            acc_b[...] = jnp.zeros_like(acc_b)
            acc_o[...] = jnp.zeros_like(acc_o)

        oa_v, ob_v = oa_ref[...], ob_ref[...]
        pa = _mm(oa_v, wa_ref[...])
        pb = _mm(ob_v, wb_ref[...])
        sa, sb = _sigmoid(ga_ref[...]), _sigmoid(gb_ref[...])
        ym = sa * pa + sb * pb
        u = _mm(ym, wo_ref[...])
        gate = mod_ref[2:3, :]
        fg = fg_ref[...]
        x2 = x_ref[...] + gate * u
        r2 = lax.rsqrt(jnp.mean(x2 * x2, axis=-1, keepdims=True) + EPS)
        xn2 = x2 * r2
        e = xn2 * fg - t_ref[...]
        dy = e * (1.0 / D)
        dn = dy * fg
        dx2 = r2 * (dn - xn2 * jnp.mean(dn * xn2, axis=-1, keepdims=True))
        dx2_ref[...] = dx2
        sums_ref[0:1, :] += jnp.sum(dy * xn2, axis=0, keepdims=True)
        sums_ref[1:2, :] += jnp.sum(dx2 * u, axis=0, keepdims=True)
        sums_ref[2:3, :] += (0.5 / D) * jnp.sum(e * e, axis=0, keepdims=True)
        du = dx2 * gate
        dym = _mm(du, wo_ref[...], NT)
        acc_o[...] += _mm(ym, du, TN)
        dpa, dpb = dym * sa, dym * sb
        dga_ref[...] = (dym * pa * (sa * (1.0 - sa))).astype(MXU_DTYPE)
        dgb_ref[...] = (dym * pb * (sb * (1.0 - sb))).astype(MXU_DTYPE)
        doa_ref[...] = _mm(dpa, wa_ref[...], NT)
        dob_ref[...] = _mm(dpb, wb_ref[...], NT)
        acc_a[...] += _mm(oa_v, dpa, TN)
        acc_b[...] += _mm(ob_v, dpb, TN)

        @pl.when(i == nt - 1)
        def _():
            pltpu.sync_copy(acc_a, gwa_ref)
            pltpu.sync_copy(acc_b, gwb_ref)
            pltpu.sync_copy(acc_o, gwo_ref)

    row = lambda w: pl.BlockSpec((tm, w), lambda i: (i, 0))
    full = lambda a, b: pl.BlockSpec((a, b), lambda i: (0, 0))
    any_spec = pl.BlockSpec(memory_space=pl.ANY)
    return pl.pallas_call(
        body, name="tail",
        grid=(nt,),
        in_specs=[row(D), row(D), row(A_W), row(D),
                  pl.BlockSpec((tm, D), lambda i: (i, 6)), pl.BlockSpec((tm, D), lambda i: (i, 7)),
                  full(8, D), full(1, D), full(A_W, D), full(D, D), full(D, D)],
        out_specs=[row(D), row(A_W), row(D), row(D), row(D), full(8, D), any_spec, any_spec, any_spec],
        out_shape=[jax.ShapeDtypeStruct((S, D), F32), jax.ShapeDtypeStruct((S, A_W), F32),
                   jax.ShapeDtypeStruct((S, D), F32), jax.ShapeDtypeStruct((S, D), MXU_DTYPE),
                   jax.ShapeDtypeStruct((S, D), MXU_DTYPE), jax.ShapeDtypeStruct((8, D), F32),
                   jax.ShapeDtypeStruct((A_W, D), F32), jax.ShapeDtypeStruct((D, D), F32),
                   jax.ShapeDtypeStruct((D, D), F32)],
        scratch_shapes=[pltpu.VMEM((A_W, D), F32), pltpu.VMEM((D, D), F32), pltpu.VMEM((D, D), F32)],
        compiler_params=_params(("arbitrary",)),
    )(x, target, oa, ob, proj, proj, mod3, final_g, wa, wb, wo)


def _dh(pieces, w_in_g, x, dx2, mod3, norm_g, grads, tm=512):
    S = x.shape[0]
    ni = S // tm
    ng = len(grads)

    def body(*refs):
        p_refs = refs[:N_DEV]
        w_ref, x_ref, dx2_ref, mod_ref, g_ref = refs[N_DEV:N_DEV + 5]
        g_ins = refs[N_DEV + 5:N_DEV + 5 + ng]
        gx_ref, sums_ref = refs[N_DEV + 5 + ng:N_DEV + 7 + ng]
        g_outs = refs[N_DEV + 7 + ng:N_DEV + 7 + 2 * ng]
        acc, send_sems, recv_sems, local_sems = refs[N_DEV + 7 + 2 * ng:]
        i, j = pl.program_id(0), pl.program_id(1)
        start, wait = _all_to_all_copies(g_ins, g_outs, send_sems, recv_sems, local_sems)

        @pl.when((i == 0) & (j == 0))
        def _():
            start()
            sums_ref[...] = jnp.zeros_like(sums_ref)

        @pl.when(j == 0)
        def _():
            acc[...] = jnp.zeros_like(acc)

        for k in range(N_DEV):
            @pl.when(j == k)
            def _(k=k):
                acc[...] += _mm(p_refs[k][...], w_ref[0], NT)

        @pl.when(j == N_DEV - 1)
        def _():
            dh = acc[...]
            xv = x_ref[...]
            g = g_ref[...]
            sc1 = 1.0 + mod_ref[1:2, :]
            r = lax.rsqrt(jnp.mean(xv * xv, axis=-1, keepdims=True) + EPS)
            xn = xv * r
            sums_ref[0:1, :] += jnp.sum(dh, axis=0, keepdims=True)
            sums_ref[1:2, :] += jnp.sum(dh * (xn * g), axis=0, keepdims=True)
            sums_ref[2:3, :] += jnp.sum(dh * sc1 * xn, axis=0, keepdims=True)
            dxn = dh * sc1 * g
            gx_ref[...] = dx2_ref[...] + r * (dxn - xn * jnp.mean(dxn * xn, axis=-1, keepdims=True))

        @pl.when((i == ni - 1) & (j == N_DEV - 1))
        def _():
            wait()

    row = pl.BlockSpec((tm, D), lambda i, j: (i, 0))
    any_spec = pl.BlockSpec(memory_space=pl.ANY)
    return pl.pallas_call(
        body, name="dh_scatter",
        grid=(ni, N_DEV),
        in_specs=[row] * N_DEV
                 + [pl.BlockSpec((1, D, D), lambda i, j: (j, 0, 0)),
                    row, row,
                    pl.BlockSpec((8, D), lambda i, j: (0, 0)),
                    pl.BlockSpec((1, D), lambda i, j: (0, 0))]
                 + [any_spec] * ng,
        out_specs=[row, pl.BlockSpec((8, D), lambda i, j: (0, 0))] + [any_spec] * ng,
        out_shape=[jax.ShapeDtypeStruct((S, D), F32), jax.ShapeDtypeStruct((8, D), F32)]
                  + [jax.ShapeDtypeStruct(g.shape, g.dtype) for g in grads],
        scratch_shapes=[pltpu.VMEM((tm, D), F32),
                        pltpu.SemaphoreType.DMA((ng, N_DEV - 1)), pltpu.SemaphoreType.DMA((ng, N_DEV - 1)),
                        pltpu.SemaphoreType.DMA((ng,))],
        compiler_params=_params(("arbitrary", "arbitrary")),
    )(*pieces, w_in_g, x, dx2, mod3, norm_g, *grads)


def _gw_in(h, pieces, tm=512):
    S = h.shape[0]
    nt = S // tm

    def body(*refs):
        h_ref, p_refs, o_ref, acc = refs[0], refs[1:1 + N_DEV], refs[1 + N_DEV], refs[2 + N_DEV]
        j, i = pl.program_id(0), pl.program_id(1)

        @pl.when(i == 0)
        def _():
            acc[...] = jnp.zeros_like(acc)

        for k in range(N_DEV):
            @pl.when(j == k)
            def _(k=k):
                acc[...] += _mm(h_ref[...], p_refs[k][...], TN)

        @pl.when(i == nt - 1)
        def _():
            o_ref[0] = acc[...].astype(XCHG_DTYPE)

    def piece(k):
        return pl.BlockSpec((tm, D), lambda j, i: (jnp.where(j == k, i, 0), 0))

    return pl.pallas_call(
        body, name="gw_in",
        grid=(N_DEV, nt),
        in_specs=[pl.BlockSpec((tm, D), lambda j, i: (i, 0))] + [piece(k) for k in range(N_DEV)],
        out_specs=pl.BlockSpec((1, D, D), lambda j, i: (j, 0, 0)),
        out_shape=jax.ShapeDtypeStruct((N_DEV, D, D), XCHG_DTYPE),
        scratch_shapes=[pltpu.VMEM((D, D), F32)],
        compiler_params=_params(("parallel", "arbitrary")),
    )(h, *pieces)


def _adamw_math(w, g, m, v):
    m = ADAM_B1 * m + (1.0 - ADAM_B1) * g
    v = ADAM_B2 * v + (1.0 - ADAM_B2) * (g * g)
    m_hat = m / (1.0 - ADAM_B1 ** ADAM_STEP)
    v_hat = v / (1.0 - ADAM_B2 ** ADAM_STEP)
    delta = -ADAM_LR * (m_hat / (jnp.sqrt(v_hat) + ADAM_EPS) + ADAM_WD * w)
    return delta, m, v


def _adamw_big(recv, w, m, v, name, tr=128):
    M, N = w.shape
    tr = min(tr, M)

    def body(r_ref, w_ref, m_ref, v_ref, g_ref, d_ref, nm_ref, nv_ref):
        g = r_ref[0].astype(F32)
        for j in range(1, N_DEV):
            g = g + r_ref[j].astype(F32)
        g_ref[...] = g
        d_ref[...], nm_ref[...], nv_ref[...] = _adamw_math(w_ref[...], g, m_ref[...], v_ref[...])

    blk = pl.BlockSpec((tr, N), lambda i: (i, 0))
    return pl.pallas_call(
        body, name=name,
        grid=(M // tr,),
        in_specs=[pl.BlockSpec((N_DEV, tr, N), lambda i: (0, i, 0)), blk, blk, blk],
        out_specs=[blk] * 4,
        out_shape=[jax.ShapeDtypeStruct((M, N), F32)] * 4,
        compiler_params=_params(("parallel",)),
    )(recv, w, m, v)


def _adamw_w_ada(c64, dmod64, w, m, v):
    def body(c_ref, dm_ref, w_ref, m_ref, v_ref, g_ref, d_ref, nm_ref, nv_ref):
        cv = c_ref[...]
        g = _mm(cv * _sigmoid(cv), dm_ref[...], TN)
        g_ref[...] = g
        d_ref[...], nm_ref[...], nv_ref[...] = _adamw_math(w_ref[...], g, m_ref[...], v_ref[...])

    return pl.pallas_call(
        body, name="adamw_w_ada",
        out_shape=[jax.ShapeDtypeStruct(w.shape, F32)] * 4,
        compiler_params=_params(),
    )(c64, dmod64, w, m, v)


P_MOD, P_NORM, P_ONORM, P_RELB, P_LB, P_FINAL, P_LOSS, P_END = (0, 3 * D, 4 * D, 5 * D, 6 * D, 7 * D, 8 * D, 9 * D)


def _adamw_small(packed, b_ada, norm_g, onorm_g, relb, hgrn_lb, final_g, ms, vs):
    def body(pk_ref, b_ref, ng_ref, og_ref, rb_ref, lb_ref, fg_ref,
             mb, mn, mo, mr, ml, mf, vb, vn, vo, vr, vl, vf,
             loss_ref, gb, gn, go, gr, gl, gf, db, dn, do, dr, dl, df,
             nmb, nmn, nmo, nmr, nml, nmf, nvb, nvn, nvo, nvr, nvl, nvf):
        tot = pk_ref[0:1, :]
        for j in range(1, N_DEV):
            tot = tot + pk_ref[8 * j:8 * j + 1, :]
        loss_ref[...] = jnp.broadcast_to(jnp.sum(tot[:, P_LOSS:P_END], axis=-1, keepdims=True), (8, 128))

        def upd(g, w_ref, m_ref, v_ref, g_out, d_out, m_out, v_out):
            g_out[...] = g
            d_out[...], m_out[...], v_out[...] = _adamw_math(w_ref[...], g, m_ref[...], v_ref[...])

        upd(tot[:, P_MOD:P_NORM], b_ref, mb, vb, gb, db, nmb, nvb)
        upd(tot[:, P_NORM:P_ONORM], ng_ref, mn, vn, gn, dn, nmn, nvn)
        g_on = tot[:, P_ONORM:P_ONORM + G_DK]
        for h in range(1, G_HEADS):
            g_on = g_on + tot[:, P_ONORM + G_DK * h:P_ONORM + G_DK * (h + 1)]
        upd(g_on, og_ref, mo, vo, go, do, nmo, nvo)
        upd(tot[:, P_RELB:P_LB], rb_ref, mr, vr, gr, dr, nmr, nvr)
        a = lb_ref[...]
        lb = _sigmoid(a[0:1, :] - a[1:2, :])
        g0 = tot[:, P_LB:P_FINAL] * lb * (1.0 - lb)
        row = lax.broadcasted_iota(jnp.int32, (2, D), 0)
        upd(jnp.where(row == 0, g0, -g0), lb_ref, ml, vl, gl, dl, nml, nvl)
        upd(tot[:, P_FINAL:P_LOSS], fg_ref, mf, vf, gf, df, nmf, nvf)

    shapes = [b_ada.shape, norm_g.shape, onorm_g.shape, relb.shape, hgrn_lb.shape, final_g.shape]
    outs = [jax.ShapeDtypeStruct((8, 128), F32)] + [jax.ShapeDtypeStruct(s, F32) for s in shapes] * 4
    return pl.pallas_call(
        body, name="adamw_small",
        out_shape=outs,
        compiler_params=_params(),
    )(packed, b_ada, norm_g, onorm_g, relb, hgrn_lb, final_g, *ms, *vs)


def _local_step(x, target, mod3, norm_g, w_in_g, onorm_g, wa, wb, wo, rel_bias, hgrn_lb, final_g):
    buckets = jnp.asarray(_bucket_tables())
    bias = _bias_tables(rel_bias, buckets)
    proj, h = _inproj(x, mod3, norm_g, w_in_g)
    os, ls = [], []
    for p, (_, d) in enumerate(PATTERNS):
        o, l = _attn_fwd(proj, bias[p], d, "attn_fwd_d%d" % d)
        os.append(o)
        ls.append(l)
    ao, lt, oa = _attn_combine(os, ls, proj)
    o_raw, ob, states = _hgrn_fwd(proj, hgrn_lb, onorm_g)
    dx2, doa, dob, dga, dgb, tsums, gwa, gwb, gwo = _tail(x, target, oa, ob, proj, mod3, final_g, wa, wb, wo)
    do, dza, delta = _attn_pre_bwd(doa, ao, proj)
    dqs, dks, dvs, dbs = [], [], [], []
    for p, (_, d) in enumerate(PATTERNS):
        dq, dk, dv, db = _attn_bwd(proj, do, lt, delta, bias[p], d, "attn_bwd_d%d" % d)
        dqs.append(dq)
        dks.append(dk)
        dvs.append(dv)
        dbs.append(db)
    p0, p1 = _attn_assemble(dqs, dks, dvs, dza)
    g_relb = _rel_bias_grad(dbs, buckets)
    dqb, dfb, dib, dzb, dlb, dgo = _hgrn_bwd(proj, o_raw, dob, states, hgrn_lb, onorm_g)
    pieces = [p0, p1, dqb, dfb, dib, dzb, dga, dgb]
    grads = [_gw_in(h, pieces),
             gwa.astype(XCHG_DTYPE).reshape(A_W, N_DEV, D // N_DEV).transpose(1, 0, 2),
             gwb.astype(XCHG_DTYPE).reshape(N_DEV, D // N_DEV, D),
             gwo.astype(XCHG_DTYPE).reshape(N_DEV, D // N_DEV, D)]
    gx, hsums, *received = _dh(pieces, w_in_g, x, dx2, mod3, norm_g, grads)
    row = jnp.concatenate([
        hsums[0], hsums[1], tsums[1],
        hsums[2],
        dgo.reshape(G_HEADS, 8, G_DK)[:, 0].reshape(-1),
        g_relb.reshape(-1),
        dlb[0],
        tsums[0],
        tsums[2],
    ])
    return gx, received, row


def kernel(x, c, w_ada, b_ada, norm_g, w_in, hgrn_onorm_g, w_branch_a, w_branch_b, w_out, rel_bias, hgrn_lb, final_g, loss_target, m_w_ada, m_b_ada, m_norm_g, m_w_in, m_hgrn_onorm_g, m_w_branch_a, m_w_branch_b, m_w_out, m_rel_bias, m_hgrn_lb, m_final_g, v_w_ada, v_b_ada, v_norm_g, v_w_in, v_hgrn_onorm_g, v_w_branch_a, v_w_branch_b, v_w_out, v_rel_bias, v_hgrn_lb, v_final_g):
    me = 4 * lax.axis_index("x") + 2 * lax.axis_index("y") + lax.axis_index("c")
    n_ada = w_ada.shape[2]

    w_in_g, wa_g, wb_g, wo_g = _all_gather(
        [w_in[0].astype(MXU_DTYPE), w_branch_a[0].astype(MXU_DTYPE),
         w_branch_b[0].astype(MXU_DTYPE), w_out[0].astype(MXU_DTYPE)], "gather_weights")
    wa = wa_g.transpose(1, 0, 2).reshape(A_W, D)
    wb = wb_g.reshape(D, D)
    wo = wo_g.reshape(D, D)

    (c_all,) = _all_gather([jnp.broadcast_to(c, (8, D))], "gather_c")
    c64 = c_all.reshape(8 * N_DEV, D)
    b_loc = lax.dynamic_slice(b_ada, (0, me * n_ada), (1, n_ada))
    mod_part = _mod_fwd(c64, w_ada[0], b_loc)[::8]
    (mod_all,) = _all_gather([mod_part], "gather_mod")
    mod = lax.dynamic_slice(mod_all, (0, me, 0), (N_DEV, 1, n_ada)).reshape(3, D)
    mod3 = jnp.concatenate([mod, jnp.zeros((5, D), F32)], axis=0)

    onorm_t = hgrn_onorm_g
    gx, (r_in, r_a, r_b, r_o), row = _local_step(
        x[0], loss_target[0], mod3, norm_g, w_in_g, onorm_t, wa, wb, wo, rel_bias, hgrn_lb,
        final_g.reshape(1, D))
    packed8 = jnp.concatenate([row[None, :], jnp.zeros((7, P_END), F32)], axis=0)
    (packed,) = _all_gather([packed8], "gather_small")
    packed = packed.reshape(8 * N_DEV, P_END)

    g_in, d_in, nm_in, nv_in = _adamw_big(r_in, w_in[0], m_w_in[0], v_w_in[0], "adamw_w_in")
    g_a, d_a, nm_a, nv_a = _adamw_big(r_a, w_branch_a[0], m_w_branch_a[0], v_w_branch_a[0], "adamw_w_branch_a")
    g_b, d_b, nm_b, nv_b = _adamw_big(r_b, w_branch_b[0], m_w_branch_b[0], v_w_branch_b[0], "adamw_w_branch_b")
    g_o, d_o, nm_o, nv_o = _adamw_big(r_o, w_out[0], m_w_out[0], v_w_out[0], "adamw_w_out")

    dmod64 = lax.dynamic_slice(packed, (0, P_MOD + me * n_ada), (8 * N_DEV, n_ada))
    g_ada, d_ada, nm_ada, nv_ada = _adamw_w_ada(c64, dmod64, w_ada[0], m_w_ada[0], v_w_ada[0])

    def flat_relb(t):
        return jnp.pad(t.T, ((0, 0), (0, 128 - N_BUCKETS))).reshape(1, A_HEADS * 128)

    def unflat_relb(t):
        return t.reshape(A_HEADS, 128)[:, :N_BUCKETS].T

    fg2 = lambda t: t.reshape(1, D)
    smalls = _adamw_small(
        packed, b_ada, norm_g, hgrn_onorm_g, flat_relb(rel_bias), hgrn_lb, fg2(final_g),
        [m_b_ada, m_norm_g, m_hgrn_onorm_g, flat_relb(m_rel_bias), m_hgrn_lb, fg2(m_final_g)],
        [v_b_ada, v_norm_g, v_hgrn_onorm_g, flat_relb(v_rel_bias), v_hgrn_lb, fg2(v_final_g)])
    loss = smalls[0][0, 0]

    def small(kind):
        s = smalls[1 + 6 * kind:7 + 6 * kind]
        return s[0], s[1], s[2], unflat_relb(s[3]), s[4], s[5].reshape(D)

    def leaves(ada, sm, w_in_, wa_, wb_, wo_):
        b_, n_, o_, r_, l_, f_ = sm
        return (ada[None], b_, n_, w_in_[None], o_, wa_[None], wb_[None], wo_[None], r_, l_, f_)

    return (loss, gx[None],
            *leaves(g_ada, small(0), g_in, g_a, g_b, g_o),
            *leaves(d_ada, small(1), d_in, d_a, d_b, d_o),
            *leaves(nm_ada, small(2), nm_in, nm_a, nm_b, nm_o),
            *leaves(nv_ada, small(3), nv_in, nv_a, nv_b, nv_o))
```

```python
import functools
import math

import numpy as np
import jax
import jax.numpy as jnp
from jax import lax
from jax.experimental import pallas as pl
from jax.experimental.pallas import tpu as pltpu

F32 = jnp.float32
BF16 = jnp.bfloat16
MXU_DTYPE = jnp.bfloat16
XCHG_DTYPE = jnp.bfloat16

N_DEV = 8
D = 1024
A_HEADS = 8
A_HD = 64
A_W = A_HEADS * A_HD
A_BLK = 128
PATTERNS = ((128, 1), (512, 4), (2048, 16))
N_BUCKETS = 32
MAX_DISTANCE = 2048
NEG = -1e30
G_HEADS = 8
G_DK = 128
G_W = G_HEADS * G_DK
IN_W = 8 * D
EPS = 1e-6
ADAM_LR = 0.001
ADAM_B1 = 0.9
ADAM_B2 = 0.999
ADAM_EPS = 1e-08
ADAM_WD = 0.01
ADAM_STEP = 10

G_CHUNK = 128
G_SUB = 16
G_HPS_FWD = 4
G_HPS_BWD = 4
G_RB = 16
VMEM_LIMIT = 56 * 1024 * 1024

NN = (((1,), (0,)), ((), ()))
NT = (((1,), (1,)), ((), ()))
TN = (((0,), (0,)), ((), ()))
MESH = pl.DeviceIdType.MESH


def _mm(a, b, dims=NN):
    return lax.dot_general(a.astype(MXU_DTYPE), b.astype(MXU_DTYPE), dims,
                           preferred_element_type=F32)


def _mm_exact(t, x):
    hi = x.astype(BF16)
    r = x - hi.astype(F32)
    mid = r.astype(BF16)
    lo = (r - mid.astype(F32)).astype(BF16)
    tb = t.astype(BF16)
    return sum(lax.dot_general(tb, p, NN, preferred_element_type=F32) for p in (hi, mid, lo))


def _sigmoid(x):
    return 0.5 * jnp.tanh(0.5 * x) + 0.5


def _params(sem=None):
    return pltpu.CompilerParams(dimension_semantics=sem, vmem_limit_bytes=VMEM_LIMIT)


def _all_gather(xs, name):
    n = len(xs)

    def body(*refs):
        ins, outs = refs[:n], refs[n:2 * n]
        send_sems, recv_sems, local_sems = refs[2 * n:]
        x, y, c = lax.axis_index("x"), lax.axis_index("y"), lax.axis_index("c")
        me, sibling = (x, y, c), (x, y, 1 - c)
        chips = [(1 - x, y), (x, 1 - y), (1 - x, 1 - y)]

        def slot(ref, dev):
            return ref.at[4 * dev[0] + 2 * dev[1] + dev[2]]

        def copy(a, k, block, to, src=None):
            return pltpu.make_async_remote_copy(
                src_ref=slot(outs[a], block) if src is None else src,
                dst_ref=slot(outs[a], block),
                send_sem=send_sems.at[a, k], recv_sem=recv_sems.at[a, k],
                device_id=to, device_id_type=MESH)

        mine, first, passed = [], [], []
        for a in range(n):
            cp = pltpu.make_async_copy(ins[a], slot(outs[a], me), local_sems.at[a])
            cp.start()
            mine.append(cp)
            first.append(copy(a, 0, me, sibling, src=ins[a]))
            for j, chip in enumerate(chips):
                first.append(copy(a, 1 + j, me, (*chip, c), src=ins[a]))
        for cp in first:
            cp.start()
        for j, chip in enumerate(chips):
            for a in range(n):
                copy(a, 1 + j, (*chip, c), me).wait_recv()
                cp = copy(a, 4 + j, (*chip, c), sibling)
                cp.start()
                passed.append(cp)
        for a in range(n):
            copy(a, 0, sibling, me).wait_recv()
            for j, chip in enumerate(chips):
                copy(a, 4 + j, (*chip, 1 - c), me).wait_recv()
        for cp in first + passed:
            cp.wait_send()
        for cp in mine:
            cp.wait()

    any_spec = pl.BlockSpec(memory_space=pl.ANY)
    return pl.pallas_call(
        body, name=name,
        out_shape=[jax.ShapeDtypeStruct((N_DEV,) + v.shape, v.dtype) for v in xs],
        in_specs=[any_spec] * n, out_specs=[any_spec] * n,
        scratch_shapes=[pltpu.SemaphoreType.DMA((n, 7)), pltpu.SemaphoreType.DMA((n, 7)),
                        pltpu.SemaphoreType.DMA((n,))],
    )(*xs)


def _all_to_all_copies(ins, outs, send_sems, recv_sems, local_sems):
    n = len(ins)
    x, y, c = lax.axis_index("x"), lax.axis_index("y"), lax.axis_index("c")
    me = 4 * x + 2 * y + c
    peers = []
    for m in range(1, N_DEV):
        peers.append((1 - x if m & 4 else x, 1 - y if m & 2 else y, 1 - c if m & 1 else c))

    def copy(a, k, landing):
        peer = peers[k]
        pid = 4 * peer[0] + 2 * peer[1] + peer[2]
        return pltpu.make_async_remote_copy(
            src_ref=ins[a].at[pid], dst_ref=outs[a].at[pid if landing else me],
            send_sem=send_sems.at[a, k], recv_sem=recv_sems.at[a, k],
            device_id=peer, device_id_type=MESH)

    def local(a):
        return pltpu.make_async_copy(ins[a].at[me], outs[a].at[me], local_sems.at[a])

    def start():
        for a in range(n):
            local(a).start()
        for k in range(N_DEV - 1):
            for a in range(n):
                copy(a, k, False).start()

    def wait():
        for k in range(N_DEV - 1):
            for a in range(n):
                copy(a, k, True).wait_recv()
        for k in range(N_DEV - 1):
            for a in range(n):
                copy(a, k, False).wait_send()
        for a in range(n):
            local(a).wait()

    return start, wait


def _mod_fwd(c64, w_ada, b_loc):
    def body(c_ref, w_ref, b_ref, o_ref):
        cv = c_ref[...]
        sc = cv * _sigmoid(cv)
        o_ref[...] = _mm(sc, w_ref[...]) + b_ref[...]

    return pl.pallas_call(
        body, name="mod_fwd",
        out_shape=jax.ShapeDtypeStruct((c64.shape[0], w_ada.shape[1]), F32),
        compiler_params=_params(),
    )(c64, w_ada, b_loc)


def _inproj(x, mod3, norm_g, w_in_g, tm=256):
    S = x.shape[0]

    def body(x_ref, mod_ref, g_ref, w_ref, proj_ref, ht_ref, w_all):
        @pl.when(pl.program_id(0) == 0)
        def _():
            pltpu.sync_copy(w_ref, w_all)

        xv = x_ref[...]
        r = lax.rsqrt(jnp.mean(xv * xv, axis=-1, keepdims=True) + EPS)
        h = ((xv * r * g_ref[...]) * (1.0 + mod_ref[1:2, :]) + mod_ref[0:1, :]).astype(MXU_DTYPE)
        ht_ref[...] = h.T
        for j in range(N_DEV):
            proj_ref[:, j * D:(j + 1) * D] = _mm(h, w_all[j])

    return pl.pallas_call(
        body, name="inproj",
        grid=(S // tm,),
        in_specs=[pl.BlockSpec((tm, D), lambda i: (i, 0)),
                  pl.BlockSpec((8, D), lambda i: (0, 0)),
                  pl.BlockSpec((1, D), lambda i: (0, 0)),
                  pl.BlockSpec(memory_space=pl.ANY)],
        out_specs=[pl.BlockSpec((tm, IN_W), lambda i: (i, 0)),
                   pl.BlockSpec((D, tm), lambda i: (0, i))],
        out_shape=[jax.ShapeDtypeStruct((S, IN_W), F32), jax.ShapeDtypeStruct((D, S), MXU_DTYPE)],
        scratch_shapes=[pltpu.VMEM(w_in_g.shape, w_in_g.dtype)],
        compiler_params=_params(("arbitrary",)),
    )(x, mod3, norm_g, w_in_g)


def _bucket_tables():
    qi = np.arange(A_BLK)[:, None]
    kj = np.arange(2 * A_BLK)[None, :]
    delta = qi + A_BLK - kj
    out = []
    for window, dil in PATTERNS:
        span = window // dil
        band = (delta >= 0) & (delta <= span)
        dist = np.clip(delta, 0, None) * dil
        max_exact = N_BUCKETS // 2
        nf = dist.astype(np.float32)
        large = max_exact + (np.log(np.maximum(nf, np.float32(1.0)) / np.float32(max_exact))
                             / np.float32(math.log(MAX_DISTANCE / max_exact))
                             * np.float32(N_BUCKETS - max_exact)).astype(np.int32)
        large = np.minimum(large, N_BUCKETS - 1)
        bucket = np.where(dist < max_exact, dist, large)
        out.append(np.where(band, bucket, -1).astype(np.int32))
    return np.stack(out)


def _bias_tables(rel_bias, buckets):
    def body(rb_ref, bk_ref, o_ref):
        h = pl.program_id(1)
        bk = bk_ref[0]
        acc = jnp.full(bk.shape, NEG, F32)
        for b in range(N_BUCKETS):
            acc = jnp.where(bk == b, rb_ref[b, h], acc)
        o_ref[0, 0] = acc

    return pl.pallas_call(
        body, name="bias_tables",
        grid=(3, A_HEADS),
        in_specs=[pl.BlockSpec(memory_space=pltpu.SMEM),
                  pl.BlockSpec((1, A_BLK, 2 * A_BLK), lambda p, h: (p, 0, 0))],
        out_specs=pl.BlockSpec((1, 1, A_BLK, 2 * A_BLK), lambda p, h: (p, h, 0, 0)),
        out_shape=jax.ShapeDtypeStruct((3, A_HEADS, A_BLK, 2 * A_BLK), F32),
        compiler_params=_params(("arbitrary", "arbitrary")),
    )(rel_bias, buckets)


A_TILES = 16


def _attn_heads_per_step(d):
    return A_HEADS if d == 1 else 2


def _attn_in_specs(sb, nsb, hw):
    w = A_HD * hw
    per = A_W // w

    def cur(col):
        return pl.BlockSpec((sb, w), lambda hp, n: (jnp.minimum(n, nsb - 1), per * col + hp))

    def prev(col):
        return pl.BlockSpec((sb, w), lambda hp, n: (jnp.maximum(jnp.minimum(n, nsb - 1) - 1, 0), per * col + hp))

    return [cur(0), prev(1), cur(1), prev(2), cur(2)]


def _rows(r, d):
    return pl.ds(r, A_BLK) if d == 1 else pl.ds(r, A_BLK, stride=d)


def _for_residues(d, hw, fn):
    unroll = min(d, max(1, A_TILES // hw))
    if d == unroll:
        for r in range(d):
            fn(r)
    else:
        def group(g, c):
            for u in range(unroll):
                fn(g * unroll + u)
            return c
        lax.fori_loop(0, d // unroll, group, 0)


def _attn_stack(t):
    first_half = lax.broadcasted_iota(jnp.int32, (1, 2 * A_HD), 1) < A_HD
    return jnp.concatenate([jnp.where(first_half, t, 0.0), jnp.where(first_half, 0.0, t)], axis=0)


def _attn_unstack(t2):
    first_half = lax.broadcasted_iota(jnp.int32, (1, 2 * A_HD), 1) < A_HD
    return jnp.where(first_half, t2[:A_BLK], t2[A_BLK:])


def _attn_scores(q, k, b_ref, pp, first):
    bias = jnp.concatenate([b_ref[2 * pp] + first, b_ref[2 * pp + 1] + first], axis=0)
    return _mm(_attn_stack(q), k, NT) * (A_HD ** -0.5) + bias


def _attn_fwd(proj, bias_p, d, name):
    S = proj.shape[0]
    sb = A_BLK * d
    nsb = S // sb
    hw = _attn_heads_per_step(d)

    def body(q_ref, kp_ref, kc_ref, vp_ref, vc_ref, b_ref, o_ref, l_ref):
        n = pl.program_id(1)
        kj = lax.broadcasted_iota(jnp.int32, (A_BLK, 2 * A_BLK), 1)
        first = jnp.where((n == 0) & (kj < A_BLK), NEG, 0.0).astype(F32)

        def residue(r):
            rows = _rows(r, d)
            for pp in range(hw // 2):
                lanes = pl.ds(2 * A_HD * pp, 2 * A_HD)
                k = jnp.concatenate([kp_ref[rows, lanes], kc_ref[rows, lanes]], axis=0)
                v = jnp.concatenate([vp_ref[rows, lanes], vc_ref[rows, lanes]], axis=0)
                s = _attn_scores(q_ref[rows, lanes], k, b_ref, pp, first)
                m = jnp.max(s, axis=-1, keepdims=True)
                p = jnp.exp(s - m)
                den = jnp.sum(p, axis=-1, keepdims=True)
                o_ref[rows, lanes] = _attn_unstack(_mm(p, v) / den)
                l_ref[rows, lanes] = _attn_unstack(jnp.broadcast_to(m + jnp.log(den), (2 * A_BLK, 2 * A_HD)))

        _for_residues(d, hw, residue)

    out = pl.BlockSpec((sb, A_HD * hw), lambda hp, n: (n, hp))
    return pl.pallas_call(
        body, name=name,
        grid=(A_HEADS // hw, nsb),
        in_specs=_attn_in_specs(sb, nsb, hw) + [pl.BlockSpec((hw, A_BLK, 2 * A_BLK), lambda hp, n: (hp, 0, 0))],
        out_specs=[out, out],
        out_shape=[jax.ShapeDtypeStruct((S, A_W), F32)] * 2,
        compiler_params=_params(("parallel", "parallel")),
    )(proj, proj, proj, proj, proj, bias_p)


def _attn_combine(os, ls, proj, tm=512):
    S = proj.shape[0]

    def body(o1, o2, o3, l1, l2, l3, z_ref, ao_ref, lt_ref, oa_ref):
        a1, a2, a3 = l1[...], l2[...], l3[...]
        m = jnp.maximum(jnp.maximum(a1, a2), a3)
        e1, e2, e3 = jnp.exp(a1 - m), jnp.exp(a2 - m), jnp.exp(a3 - m)
        den = e1 + e2 + e3
        ao = (e1 * o1[...] + e2 * o2[...] + e3 * o3[...]) / den
        z = z_ref[...]
        ao_ref[...] = ao
        lt_ref[...] = m + jnp.log(den)
        oa_ref[...] = (ao * (z * _sigmoid(z))).astype(MXU_DTYPE)

    spec = pl.BlockSpec((tm, A_W), lambda i: (i, 0))
    return pl.pallas_call(
        body, name="attn_combine",
        grid=(S // tm,),
        in_specs=[spec] * 6 + [pl.BlockSpec((tm, A_W), lambda i: (i, 3))],
        out_specs=[spec] * 3,
        out_shape=[jax.ShapeDtypeStruct((S, A_W), F32), jax.ShapeDtypeStruct((S, A_W), F32),
                   jax.ShapeDtypeStruct((S, A_W), MXU_DTYPE)],
        compiler_params=_params(("parallel",)),
    )(*os, *ls, proj)


def _attn_pre_bwd(doa, ao, proj, tm=512):
    S = proj.shape[0]

    def body(doa_ref, ao_ref, z_ref, do_ref, dz_ref, dl_ref):
        z = z_ref[...]
        sg = _sigmoid(z)
        g = doa_ref[...]
        ao_v = ao_ref[...]
        do = g * (z * sg)
        do_ref[...] = do
        dz_ref[...] = (g * ao_v * (sg * (1.0 + z * (1.0 - sg)))).astype(MXU_DTYPE)
        prod = do * ao_v
        for h in range(A_HEADS):
            sl = slice(A_HD * h, A_HD * (h + 1))
            dl_ref[:, sl] = jnp.broadcast_to(jnp.sum(prod[:, sl], axis=-1, keepdims=True), (tm, A_HD))

    spec = pl.BlockSpec((tm, A_W), lambda i: (i, 0))
    return pl.pallas_call(
        body, name="attn_pre_bwd",
        grid=(S // tm,),
        in_specs=[spec, spec, pl.BlockSpec((tm, A_W), lambda i: (i, 3))],
        out_specs=[spec] * 3,
        out_shape=[jax.ShapeDtypeStruct((S, A_W), F32), jax.ShapeDtypeStruct((S, A_W), MXU_DTYPE),
                   jax.ShapeDtypeStruct((S, A_W), F32)],
        compiler_params=_params(("parallel",)),
    )(doa, ao, proj)


def _attn_bwd(proj, do, lt, delta, bias_p, d, name):
    S = proj.shape[0]
    sb = A_BLK * d
    nsb = S // sb
    hw = _attn_heads_per_step(d)

    def body(q_ref, kp_ref, kc_ref, vp_ref, vc_ref, do_ref, lt_ref, dl_ref, b_ref,
             dq_ref, dk_ref, dv_ref, db_ref, ck, cv):
        n = pl.program_id(1)

        @pl.when(n == 0)
        def _():
            db_ref[...] = jnp.zeros_like(db_ref)
            ck[...] = jnp.zeros_like(ck)
            cv[...] = jnp.zeros_like(cv)

        @pl.when(n < nsb)
        def _():
            kj = lax.broadcasted_iota(jnp.int32, (A_BLK, 2 * A_BLK), 1)
            first = jnp.where((n == 0) & (kj < A_BLK), NEG, 0.0).astype(F32)

            def residue(r):
                rows = _rows(r, d)
                for pp in range(hw // 2):
                    lanes = pl.ds(2 * A_HD * pp, 2 * A_HD)
                    lt_r, dl_r = lt_ref[rows, lanes], dl_ref[rows, lanes]
                    k = jnp.concatenate([kp_ref[rows, lanes], kc_ref[rows, lanes]], axis=0)
                    v = jnp.concatenate([vp_ref[rows, lanes], vc_ref[rows, lanes]], axis=0)
                    q2 = _attn_stack(q_ref[rows, lanes])
                    do2 = _attn_stack(do_ref[rows, lanes])
                    col = lambda t: jnp.concatenate([t[:, 0:1], t[:, A_HD:A_HD + 1]], axis=0)
                    s = _attn_scores(q_ref[rows, lanes], k, b_ref, pp, first)
                    p = jnp.exp(s - col(lt_r))
                    ds = p * (_mm(do2, v, NT) - col(dl_r))
                    db_ref[2 * pp] += ds[:A_BLK]
                    db_ref[2 * pp + 1] += ds[A_BLK:]
                    dq_ref[rows, lanes] = _attn_unstack(_mm(ds, k)) * (A_HD ** -0.5)
                    dk = _mm(ds, q2, TN) * (A_HD ** -0.5)
                    dv = _mm(p, do2, TN)
                    dk_ref[rows, lanes] = ck[rows, lanes] + dk[:A_BLK]
                    dv_ref[rows, lanes] = cv[rows, lanes] + dv[:A_BLK]
                    ck[rows, lanes] = dk[A_BLK:]
                    cv[rows, lanes] = dv[A_BLK:]

            _for_residues(d, hw, residue)

        @pl.when(n == nsb)
        def _():
            dk_ref[...] = ck[...]
            dv_ref[...] = cv[...]

    w = A_HD * hw
    row = pl.BlockSpec((sb, w), lambda hp, n: (jnp.minimum(n, nsb - 1), hp))
    lag = pl.BlockSpec((sb, w), lambda hp, n: (jnp.maximum(n - 1, 0), hp))
    tab = pl.BlockSpec((hw, A_BLK, 2 * A_BLK), lambda hp, n: (hp, 0, 0))
    return pl.pallas_call(
        body, name=name,
        grid=(A_HEADS // hw, nsb + 1),
        in_specs=_attn_in_specs(sb, nsb, hw) + [row, row, row, tab],
        out_specs=[row, lag, lag, tab],
        out_shape=[jax.ShapeDtypeStruct((S, A_W), F32)] * 3
                  + [jax.ShapeDtypeStruct((A_HEADS, A_BLK, 2 * A_BLK), F32)],
        scratch_shapes=[pltpu.VMEM((sb, w), F32), pltpu.VMEM((sb, w), F32)],
        compiler_params=_params(("parallel", "arbitrary")),
    )(proj, proj, proj, proj, proj, do, lt, delta, bias_p)


def _attn_assemble(dqs, dks, dvs, dz, tm=512):
    S = dz.shape[0]

    def body(q1, q2, q3, k1, k2, k3, v1, v2, v3, z_ref, p0_ref, p1_ref):
        p0_ref[:, :A_W] = (q1[...] + q2[...] + q3[...]).astype(MXU_DTYPE)
        p0_ref[:, A_W:] = (k1[...] + k2[...] + k3[...]).astype(MXU_DTYPE)
        p1_ref[:, :A_W] = (v1[...] + v2[...] + v3[...]).astype(MXU_DTYPE)
        p1_ref[:, A_W:] = z_ref[...]

    spec = pl.BlockSpec((tm, A_W), lambda i: (i, 0))
    wide = pl.BlockSpec((tm, 2 * A_W), lambda i: (i, 0))
    return pl.pallas_call(
        body, name="attn_assemble",
        grid=(S // tm,),
        in_specs=[spec] * 10,
        out_specs=[wide, wide],
        out_shape=[jax.ShapeDtypeStruct((S, 2 * A_W), MXU_DTYPE)] * 2,
        compiler_params=_params(("parallel",)),
    )(*dqs, *dks, *dvs, dz)


def _rel_bias_grad(dbs, buckets):
    def body(d1, d2, d3, bk_ref, o_ref):
        row = lax.broadcasted_iota(jnp.int32, (A_HEADS, 128), 0)
        lane = lax.broadcasted_iota(jnp.int32, (A_HEADS, 128), 1)
        acc = jnp.zeros((A_HEADS, 128), F32)
        for p, dref in enumerate((d1, d2, d3)):
            bk = bk_ref[p]
            for h in range(A_HEADS):
                ds = dref[h]
                for b in range(N_BUCKETS):
                    s = jnp.sum(jnp.where(bk == b, ds, 0.0), keepdims=True)
                    acc = acc + jnp.where((row == h) & (lane == b), s, 0.0)
        o_ref[...] = acc

    return pl.pallas_call(
        body, name="rel_bias_grad",
        out_shape=jax.ShapeDtypeStruct((A_HEADS, 128), F32),
        compiler_params=_params(),
    )(*dbs, buckets)


def _tri(c):
    t = np.tril(np.ones((c, c), np.float32))
    return jnp.asarray(t), jnp.asarray(t.T.copy())


def _fill_above(ref, x, pad):
    ref[0:G_SUB, :] = jnp.full((G_SUB, x.shape[1]), pad, F32)
    ref[G_SUB:, :] = x


def _fill_below(ref, x, pad):
    ref[0:x.shape[0], :] = x
    ref[x.shape[0]:, :] = jnp.full((G_SUB, x.shape[1]), pad, F32)


def _hgrn_gates(q_ref, f_ref, lbp_ref, tri_ref):
    qraw = q_ref[...]
    sq = _sigmoid(qraw)
    q = qraw * sq
    sg = _sigmoid(f_ref[...])
    lb = _sigmoid(lbp_ref[0:1, :] - lbp_ref[1:2, :])
    f = lb + (1.0 - lb) * sg
    k = 1.0 - f
    b = _mm_exact(tri_ref[...], jnp.log(f))
    return qraw, sq, q, sg, lb, f, k, b


def _hgrn_col(C, base, idx, hps):
    return pl.BlockSpec((C, hps * G_DK), lambda h, n: (idx(n), base * (G_HEADS // hps) + h))


def _hgrn_levels(C):
    out, m = [], G_SUB
    while 2 * m <= C:
        out.append(m)
        m *= 2
    return out


def _hgrn_level_masks(C):
    ti = np.arange(C)[:, None]
    si = np.arange(C)[None, :]
    return jnp.asarray(np.stack([((ti // (2 * m) == si // (2 * m)) & (ti - si >= G_SUB)).astype(np.float32)
                                 for m in _hgrn_levels(C)]))


def _hgrn_level(b, q, k, C, m):
    zeros = jnp.zeros((m, G_DK), F32)
    eq, ek, qt, kt = [], [], [], []
    for blk in range(0, C // m, 2):
        lo, mid, hi = blk * m, (blk + 1) * m, (blk + 2) * m
        ref = b[mid:mid + 1]
        e_right = jnp.exp(b[mid:hi] - ref)
        e_left = jnp.exp(ref - b[lo:mid])
        eq += [zeros, e_right]
        ek += [e_left, zeros]
        qt += [zeros, q[mid:hi] * e_right]
        kt += [k[lo:mid] * e_left, zeros]
    cat = lambda parts: jnp.concatenate(parts, axis=0)
    return cat(qt), cat(kt), cat(eq), cat(ek)


def _hgrn_fwd(proj, hgrn_lb, onorm_g, C=G_CHUNK):
    S = proj.shape[0]
    nc = S // C
    tri, _ = _tri(C)
    masks = _hgrn_level_masks(C)
    hps = G_HPS_FWD

    def body(q_ref, f_ref, i_ref, z_ref, lbp_ref, go_ref, tri_ref, pm_ref, o_ref, ob_ref, st_ref, St, kp, vp, fp):
        @pl.when(pl.program_id(1) == 0)
        def _():
            St[...] = jnp.zeros_like(St)

        for hh in range(hps):
            ln = pl.ds(G_DK * hh, G_DK)
            head(q_ref.at[:, ln], f_ref.at[:, ln], i_ref.at[:, ln], z_ref.at[:, ln], lbp_ref.at[:, ln], go_ref,
                 tri_ref, pm_ref, o_ref.at[:, ln], ob_ref.at[:, ln], st_ref.at[0, hh], St.at[hh], kp.at[hh], vp.at[hh],
                 fp.at[hh])

    def head(q_ref, f_ref, i_ref, z_ref, lbp_ref, go_ref, tri_ref, pm_ref, o_ref, ob_ref, st_ref, St, kp, vp, fp):
        _, _, q, _, _, f, k, b = _hgrn_gates(q_ref, f_ref, lbp_ref, tri_ref)
        v = i_ref[...]
        bC = b[C - 1:C, :]
        S0 = St[...]
        o = _mm(q * jnp.exp(b), S0, NT)
        _fill_above(kp, k, 0.0)
        _fill_above(vp, v, 0.0)
        _fill_above(fp, f, 1.0)
        near = []
        for r0 in range(0, C, G_RB):
            qb = q[r0:r0 + G_RB]
            acc = e = None
            for l in range(G_SUB):
                rows = pl.ds(G_SUB - l + r0, G_RB)
                if l > 0:
                    fl = fp[pl.ds(G_SUB - l + 1 + r0, G_RB), :]
                    e = fl if e is None else e * fl
                kl = kp[rows, :]
                a = jnp.sum(qb * kl if e is None else qb * kl * e, axis=-1, keepdims=True)
                t = a * vp[rows, :]
                acc = t if acc is None else acc + t
            near.append(acc)
        o = o + jnp.concatenate(near, axis=0)
        a_off = jnp.zeros((C, C), F32)
        for lv, m in enumerate(_hgrn_levels(C)):
            qt, kt, _, _ = _hgrn_level(b, q, k, C, m)
            a_off = a_off + pm_ref[lv] * _mm(qt, kt, NT)
        o = o + _mm(a_off, v)
        S1 = S0 * jnp.exp(bC) + _mm(v, k * jnp.exp(bC - b), TN)
        St[...] = S1
        st_ref[...] = S1
        o_ref[...] = o
        r = lax.rsqrt(jnp.mean(o * o, axis=-1, keepdims=True) + EPS)
        z = z_ref[...]
        ob_ref[...] = (o * r * go_ref[...] * (z * _sigmoid(z))).astype(MXU_DTYPE)

    ident = lambda n: n
    w = hps * G_DK
    out = pl.BlockSpec((C, w), lambda h, n: (n, h))
    return pl.pallas_call(
        body, name="hgrn_fwd",
        grid=(G_HEADS // hps, nc),
        in_specs=[_hgrn_col(C, base, ident, hps) for base in (2, 3, 4, 5)] + [
                  pl.BlockSpec((2, w), lambda h, n: (0, h)),
                  pl.BlockSpec((1, G_DK), lambda h, n: (0, 0)),
                  pl.BlockSpec((C, C), lambda h, n: (0, 0)),
                  pl.BlockSpec(masks.shape, lambda h, n: (0, 0, 0))],
        out_specs=[out, out, pl.BlockSpec((1, hps, G_DK, G_DK), lambda h, n: (n, h, 0, 0))],
        out_shape=[jax.ShapeDtypeStruct((S, G_W), F32), jax.ShapeDtypeStruct((S, G_W), MXU_DTYPE),
                   jax.ShapeDtypeStruct((nc, G_HEADS, G_DK, G_DK), F32)],
        scratch_shapes=[pltpu.VMEM((hps, G_DK, G_DK), F32)] + [pltpu.VMEM((hps, C + G_SUB, G_DK), F32)] * 3,
        compiler_params=_params(("parallel", "arbitrary")),
    )(proj, proj, proj, proj, hgrn_lb, onorm_g, tri, masks)


def _hgrn_bwd(proj, o_raw, dob, states, hgrn_lb, onorm_g, C=G_CHUNK):
    S = proj.shape[0]
    nc = S // C
    tri, triu = _tri(C)
    masks = _hgrn_level_masks(C)
    hps = G_HPS_BWD

    def body(q_ref, f_ref, i_ref, z_ref, o_ref, dob_ref, s0_ref, s1_ref, lbp_ref, go_ref, tri_ref, triu_ref,
             pm_ref, dq_ref, df_ref, di_ref, dz_ref, dlb_ref, dgo_ref, dSt, *shifted):
        @pl.when(pl.program_id(1) == 0)
        def _():
            dSt[...] = jnp.zeros_like(dSt)
            dlb_ref[...] = jnp.zeros_like(dlb_ref)
            dgo_ref[...] = jnp.zeros_like(dgo_ref)

        for hh in range(hps):
            ln = pl.ds(G_DK * hh, G_DK)
            head(q_ref.at[:, ln], f_ref.at[:, ln], i_ref.at[:, ln], z_ref.at[:, ln], o_ref.at[:, ln],
                 dob_ref.at[:, ln], s0_ref.at[0, hh], s1_ref.at[0, hh], lbp_ref.at[:, ln], go_ref, tri_ref, triu_ref,
                 pm_ref, dq_ref.at[:, ln], df_ref.at[:, ln], di_ref.at[:, ln], dz_ref.at[:, ln], dlb_ref.at[:, ln],
                 dgo_ref.at[pl.ds(8 * hh, 8), :], dSt.at[hh], *[t.at[hh] for t in shifted])

    def head(q_ref, f_ref, i_ref, z_ref, o_ref, dob_ref, s0_ref, s1_ref, lbp_ref, go_ref, tri_ref, triu_ref,
             pm_ref, dq_ref, df_ref, di_ref, dz_ref, dlb_ref, dgo_ref, dSt, kp, vp, fp, qn, dn_, fn, xs, dac):
        cn = nc - 1 - pl.program_id(1)
        qraw, sq, q, sg, lb, f, k, b = _hgrn_gates(q_ref, f_ref, lbp_ref, tri_ref)
        v = i_ref[...]
        bC = b[C - 1:C, :]
        eb = jnp.exp(b)
        ecb = jnp.exp(bC - b)
        o = o_ref[...]
        z = z_ref[...]
        sz = _sigmoid(z)
        go = go_ref[...]
        g_ob = dob_ref[...]
        r = lax.rsqrt(jnp.mean(o * o, axis=-1, keepdims=True) + EPS)
        nh = o * r
        dnrm = g_ob * (z * sz)
        dz_ref[...] = (g_ob * (nh * go) * (sz * (1.0 + z * (1.0 - sz)))).astype(MXU_DTYPE)
        dgo_ref[0:1, :] += jnp.sum(dnrm * nh, axis=0, keepdims=True)
        dn = dnrm * go
        do = r * (dn - nh * jnp.mean(dn * nh, axis=-1, keepdims=True))

        S0 = jnp.where(cn == 0, 0.0, s0_ref[...])
        S1 = s1_ref[...]
        dS1 = dSt[...]
        dq = eb * _mm(do, S0)
        dk = ecb * _mm(v, dS1)
        dv = _mm(k * ecb, dS1, NT)
        bnd = jnp.sum(dS1 * S1, axis=0, keepdims=True)
        dSt[...] = dS1 * jnp.exp(bC) + _mm(do, q * eb, TN)

        _fill_above(kp, k, 0.0)
        _fill_above(vp, v, 0.0)
        _fill_above(fp, f, 1.0)
        _fill_below(qn, q, 0.0)
        _fill_below(dn_, do, 0.0)
        _fill_below(fn, f, 1.0)
        for r0 in range(0, C, G_RB):
            do_b = do[r0:r0 + G_RB]
            for l in range(G_SUB):
                xs[pl.ds(l * C + r0, G_RB), :] = (do_b * vp[pl.ds(G_SUB - l + r0, G_RB), :]).astype(MXU_DTYPE)
        dac[0:G_SUB * C, :] = _mm(xs[...], jnp.ones((G_DK, G_DK), MXU_DTYPE))
        dac[G_SUB * C:, :] = jnp.zeros((G_SUB, G_DK), F32)
        near_q, near_k, near_v = [], [], []
        for r0 in range(0, C, G_RB):
            k_b = k[r0:r0 + G_RB]
            aq = ak = av = e = e2 = None
            for l in range(G_SUB):
                down, up = pl.ds(G_SUB - l + r0, G_RB), pl.ds(l + r0, G_RB)
                if l > 0:
                    fl = fp[pl.ds(G_SUB - l + 1 + r0, G_RB), :]
                    e = fl if e is None else e * fl
                    fu = fn[up, :]
                    e2 = fu if e2 is None else e2 * fu
                kl = kp[down, :]
                t = dac[pl.ds(l * C + r0, G_RB), :] * (kl if e is None else kl * e)
                aq = t if aq is None else aq + t
                qu = qn[up, :]
                qe = qu if e2 is None else qu * e2
                dou = dn_[up, :]
                a2 = jnp.sum(qe * k_b, axis=-1, keepdims=True)
                t = dac[pl.ds(l * C + l + r0, G_RB), :] * qe
                ak = t if ak is None else ak + t
                t = a2 * dou
                av = t if av is None else av + t
            near_q.append(aq)
            near_k.append(ak)
            near_v.append(av)
        dq = dq + jnp.concatenate(near_q, axis=0)
        dk = dk + jnp.concatenate(near_k, axis=0)
        dv = dv + jnp.concatenate(near_v, axis=0)

        da_all = _mm(do, v, NT)
        a_off = jnp.zeros((C, C), F32)
        for lv, m in enumerate(_hgrn_levels(C)):
            qt, kt, eq, ek = _hgrn_level(b, q, k, C, m)
            da_m = pm_ref[lv] * da_all
            a_off = a_off + pm_ref[lv] * _mm(qt, kt, NT)
            dq = dq + _mm(da_m, kt) * eq
            dk = dk + _mm(da_m, qt, TN) * ek
        dv = dv + _mm(a_off, do, TN)

        row = lax.broadcasted_iota(jnp.int32, (C, 1), 0)
        db = q * dq - k * dk + jnp.where(row == C - 1, bnd, 0.0)
        dg = _mm_exact(triu_ref[...], db)
        df = dg / f - dk
        df_ref[...] = (df * (1.0 - lb) * (sg * (1.0 - sg))).astype(MXU_DTYPE)
        dlb_ref[0:1, :] += jnp.sum(df * (1.0 - sg), axis=0, keepdims=True)
        dq_ref[...] = (dq * (sq * (1.0 + qraw * (1.0 - sq)))).astype(MXU_DTYPE)
        di_ref[...] = dv.astype(MXU_DTYPE)

    rev = lambda n: nc - 1 - n
    w = hps * G_DK
    blk = pl.BlockSpec((C, w), lambda h, n: (nc - 1 - n, h))
    return pl.pallas_call(
        body, name="hgrn_bwd",
        grid=(G_HEADS // hps, nc),
        in_specs=[_hgrn_col(C, base, rev, hps) for base in (2, 3, 4, 5)] + [
                  blk, blk,
                  pl.BlockSpec((1, hps, G_DK, G_DK), lambda h, n: (jnp.maximum(nc - 2 - n, 0), h, 0, 0)),
                  pl.BlockSpec((1, hps, G_DK, G_DK), lambda h, n: (nc - 1 - n, h, 0, 0)),
                  pl.BlockSpec((2, w), lambda h, n: (0, h)),
                  pl.BlockSpec((1, G_DK), lambda h, n: (0, 0)),
                  pl.BlockSpec((C, C), lambda h, n: (0, 0)),
                  pl.BlockSpec((C, C), lambda h, n: (0, 0)),
                  pl.BlockSpec(masks.shape, lambda h, n: (0, 0, 0))],
        out_specs=[blk, blk, blk, blk,
                   pl.BlockSpec((8, w), lambda h, n: (0, h)),
                   pl.BlockSpec((8 * hps, G_DK), lambda h, n: (h, 0))],
        out_shape=[jax.ShapeDtypeStruct((S, G_W), MXU_DTYPE)] * 4
                  + [jax.ShapeDtypeStruct((8, G_W), F32), jax.ShapeDtypeStruct((8 * G_HEADS, G_DK), F32)],
        scratch_shapes=[pltpu.VMEM((hps, G_DK, G_DK), F32)] + [pltpu.VMEM((hps, C + G_SUB, G_DK), F32)] * 6
                       + [pltpu.VMEM((hps, G_SUB * C, G_DK), MXU_DTYPE),
                          pltpu.VMEM((hps, G_SUB * C + G_SUB, G_DK), F32)],
        compiler_params=_params(("parallel", "arbitrary")),
    )(proj, proj, proj, proj, o_raw, dob, states, states, hgrn_lb, onorm_g, tri, triu, masks)


def _tail(x, target, oa, ob, proj, mod3, final_g, wa, wb, wo, tm=256):
    S = x.shape[0]
    nt = S // tm

    def body(x_ref, t_ref, oa_ref, ob_ref, ga_ref, gb_ref, mod_ref, fg_ref, wa_ref, wb_ref, wo_ref,
             dx2_ref, doa_ref, dob_ref, dga_ref, dgb_ref, sums_ref, gwa_ref, gwb_ref, gwo_ref,
             acc_a, acc_b, acc_o):
        i = pl.program_id(0)

        @pl.when(i == 0)
        def _():
            sums_ref[...] = jnp.zeros_like(sums_ref)
            acc_a[...] = jnp.zeros_like(acc_a)
            acc_b[...] = jnp.zeros_like(acc_b)
            acc_o[...] = jnp.zeros_like(acc_o)

        oa_v, ob_v = oa_ref[...], ob_ref[...]
        pa = _mm(oa_v, wa_ref[...])
        pb = _mm(ob_v, wb_ref[...])
        sa, sb = _sigmoid(ga_ref[...]), _sigmoid(gb_ref[...])
        ym = sa * pa + sb * pb
        u = _mm(ym, wo_ref[...])
        gate = mod_ref[2:3, :]
        fg = fg_ref[...]
        x2 = x_ref[...] + gate * u
        r2 = lax.rsqrt(jnp.mean(x2 * x2, axis=-1, keepdims=True) + EPS)
        xn2 = x2 * r2
        e = xn2 * fg - t_ref[...]
        dy = e * (1.0 / D)
        dn = dy * fg
        dx2 = r2 * (dn - xn2 * jnp.mean(dn * xn2, axis=-1, keepdims=True))
        dx2_ref[...] = dx2
        sums_ref[0:1, :] += jnp.sum(dy * xn2, axis=0, keepdims=True)
        sums_ref[1:2, :] += jnp.sum(dx2 * u, axis=0, keepdims=True)
        sums_ref[2:3, :] += (0.5 / D) * jnp.sum(e * e, axis=0, keepdims=True)
        du = dx2 * gate
        dym = _mm(du, wo_ref[...], NT)
        acc_o[...] += _mm(ym, du, TN)
        dpa, dpb = dym * sa, dym * sb
        dga_ref[...] = (dym * pa * (sa * (1.0 - sa))).astype(MXU_DTYPE)
        dgb_ref[...] = (dym * pb * (sb * (1.0 - sb))).astype(MXU_DTYPE)
        doa_ref[...] = _mm(dpa, wa_ref[...], NT)
        dob_ref[...] = _mm(dpb, wb_ref[...], NT)
        acc_a[...] += _mm(oa_v, dpa, TN)
        acc_b[...] += _mm(ob_v, dpb, TN)

        @pl.when(i == nt - 1)
        def _():
            pltpu.sync_copy(acc_a, gwa_ref)
            pltpu.sync_copy(acc_b, gwb_ref)
            pltpu.sync_copy(acc_o, gwo_ref)

    row = lambda w: pl.BlockSpec((tm, w), lambda i: (i, 0))
    full = lambda a, b: pl.BlockSpec((a, b), lambda i: (0, 0))
    any_spec = pl.BlockSpec(memory_space=pl.ANY)
    return pl.pallas_call(
        body, name="tail",
        grid=(nt,),
        in_specs=[row(D), row(D), row(A_W), row(D),
                  pl.BlockSpec((tm, D), lambda i: (i, 6)), pl.BlockSpec((tm, D), lambda i: (i, 7)),
                  full(8, D), full(1, D), full(A_W, D), full(D, D), full(D, D)],
        out_specs=[row(D), row(A_W), row(D), row(D), row(D), full(8, D), any_spec, any_spec, any_spec],
        out_shape=[jax.ShapeDtypeStruct((S, D), F32), jax.ShapeDtypeStruct((S, A_W), F32),
                   jax.ShapeDtypeStruct((S, D), F32), jax.ShapeDtypeStruct((S, D), MXU_DTYPE),
                   jax.ShapeDtypeStruct((S, D), MXU_DTYPE), jax.ShapeDtypeStruct((8, D), F32),
                   jax.ShapeDtypeStruct((A_W, D), F32), jax.ShapeDtypeStruct((D, D), F32),
                   jax.ShapeDtypeStruct((D, D), F32)],
        scratch_shapes=[pltpu.VMEM((A_W, D), F32), pltpu.VMEM((D, D), F32), pltpu.VMEM((D, D), F32)],
        compiler_params=_params(("arbitrary",)),
    )(x, target, oa, ob, proj, proj, mod3, final_g, wa, wb, wo)


def _dh(pieces, w_in_g, x, dx2, mod3, norm_g, grads, tm=256):
    S = x.shape[0]
    ni = S // tm
    ng = len(grads)

    def body(*refs):
        p_refs = refs[:N_DEV]
        w_ref, x_ref, dx2_ref, mod_ref, g_ref = refs[N_DEV:N_DEV + 5]
        g_ins = refs[N_DEV + 5:N_DEV + 5 + ng]
        gx_ref, sums_ref = refs[N_DEV + 5 + ng:N_DEV + 7 + ng]
        g_outs = refs[N_DEV + 7 + ng:N_DEV + 7 + 2 * ng]
        w_all, send_sems, recv_sems, local_sems = refs[N_DEV + 7 + 2 * ng:]
        i = pl.program_id(0)
        start, wait = _all_to_all_copies(g_ins, g_outs, send_sems, recv_sems, local_sems)

        @pl.when(i == 0)
        def _():
            start()
            sums_ref[...] = jnp.zeros_like(sums_ref)
            pltpu.sync_copy(w_ref, w_all)

        dh = _mm(p_refs[0][...], w_all[0], NT)
        for k in range(1, N_DEV):
            dh = dh + _mm(p_refs[k][...], w_all[k], NT)
        xv = x_ref[...]
        g = g_ref[...]
        sc1 = 1.0 + mod_ref[1:2, :]
        r = lax.rsqrt(jnp.mean(xv * xv, axis=-1, keepdims=True) + EPS)
        xn = xv * r
        sums_ref[0:1, :] += jnp.sum(dh, axis=0, keepdims=True)
        sums_ref[1:2, :] += jnp.sum(dh * (xn * g), axis=0, keepdims=True)
        sums_ref[2:3, :] += jnp.sum(dh * sc1 * xn, axis=0, keepdims=True)
        dxn = dh * sc1 * g
        gx_ref[...] = dx2_ref[...] + r * (dxn - xn * jnp.mean(dxn * xn, axis=-1, keepdims=True))

        @pl.when(i == ni - 1)
        def _():
            wait()

    row = pl.BlockSpec((tm, D), lambda i: (i, 0))
    any_spec = pl.BlockSpec(memory_space=pl.ANY)
    return pl.pallas_call(
        body, name="dh_scatter",
        grid=(ni,),
        in_specs=[row] * N_DEV
                 + [any_spec, row, row,
                    pl.BlockSpec((8, D), lambda i: (0, 0)),
                    pl.BlockSpec((1, D), lambda i: (0, 0))]
                 + [any_spec] * ng,
        out_specs=[row, pl.BlockSpec((8, D), lambda i: (0, 0))] + [any_spec] * ng,
        out_shape=[jax.ShapeDtypeStruct((S, D), F32), jax.ShapeDtypeStruct((8, D), F32)]
                  + [jax.ShapeDtypeStruct(g.shape, g.dtype) for g in grads],
        scratch_shapes=[pltpu.VMEM(w_in_g.shape, w_in_g.dtype),
                        pltpu.SemaphoreType.DMA((ng, N_DEV - 1)), pltpu.SemaphoreType.DMA((ng, N_DEV - 1)),
                        pltpu.SemaphoreType.DMA((ng,))],
        compiler_params=_params(("arbitrary",)),
    )(*pieces, w_in_g, x, dx2, mod3, norm_g, *grads)


def _gw_in(ht, pieces, tm=512):
    S = ht.shape[1]
    nt = S // tm

    def body(*refs):
        h_ref, p_refs, o_ref, acc = refs[0], refs[1:1 + N_DEV], refs[1 + N_DEV], refs[2 + N_DEV]
        j, i = pl.program_id(0), pl.program_id(1)

        @pl.when(i == 0)
        def _():
            acc[...] = jnp.zeros_like(acc)

        for k in range(N_DEV):
            @pl.when(j == k)
            def _(k=k):
                acc[...] += _mm(h_ref[...], p_refs[k][...])

        @pl.when(i == nt - 1)
        def _():
            o_ref[0] = acc[...].astype(XCHG_DTYPE)

    def piece(k):
        return pl.BlockSpec((tm, D), lambda j, i: (jnp.where(j == k, i, 0), 0))

    return pl.pallas_call(
        body, name="gw_in",
        grid=(N_DEV, nt),
        in_specs=[pl.BlockSpec((D, tm), lambda j, i: (0, i))] + [piece(k) for k in range(N_DEV)],
        out_specs=pl.BlockSpec((1, D, D), lambda j, i: (j, 0, 0)),
        out_shape=jax.ShapeDtypeStruct((N_DEV, D, D), XCHG_DTYPE),
        scratch_shapes=[pltpu.VMEM((D, D), F32)],
        compiler_params=_params(("parallel", "arbitrary")),
    )(ht, *pieces)


def _adamw_math(w, g, m, v):
    m = ADAM_B1 * m + (1.0 - ADAM_B1) * g
    v = ADAM_B2 * v + (1.0 - ADAM_B2) * (g * g)
    m_hat = m / (1.0 - ADAM_B1 ** ADAM_STEP)
    v_hat = v / (1.0 - ADAM_B2 ** ADAM_STEP)
    delta = -ADAM_LR * (m_hat / (jnp.sqrt(v_hat) + ADAM_EPS) + ADAM_WD * w)
    return delta, m, v


def _adamw_big(recv, w, m, v, name, tr=128):
    M, N = w.shape
    tr = min(tr, M)

    def body(r_ref, w_ref, m_ref, v_ref, g_ref, d_ref, nm_ref, nv_ref):
        g = r_ref[0].astype(F32)
        for j in range(1, N_DEV):
            g = g + r_ref[j].astype(F32)
        g_ref[...] = g
        d_ref[...], nm_ref[...], nv_ref[...] = _adamw_math(w_ref[...], g, m_ref[...], v_ref[...])

    blk = pl.BlockSpec((tr, N), lambda i: (i, 0))
    return pl.pallas_call(
        body, name=name,
        grid=(M // tr,),
        in_specs=[pl.BlockSpec((N_DEV, tr, N), lambda i: (0, i, 0)), blk, blk, blk],
        out_specs=[blk] * 4,
        out_shape=[jax.ShapeDtypeStruct((M, N), F32)] * 4,
        compiler_params=_params(("parallel",)),
    )(recv, w, m, v)


def _adamw_w_ada(c64, dmod64, w, m, v):
    def body(c_ref, dm_ref, w_ref, m_ref, v_ref, g_ref, d_ref, nm_ref, nv_ref):
        cv = c_ref[...]
        g = _mm(cv * _sigmoid(cv), dm_ref[...], TN)
        g_ref[...] = g
        d_ref[...], nm_ref[...], nv_ref[...] = _adamw_math(w_ref[...], g, m_ref[...], v_ref[...])

    return pl.pallas_call(
        body, name="adamw_w_ada",
        out_shape=[jax.ShapeDtypeStruct(w.shape, F32)] * 4,
        compiler_params=_params(),
    )(c64, dmod64, w, m, v)


P_MOD, P_NORM, P_ONORM, P_RELB, P_LB, P_FINAL, P_LOSS, P_END = (0, 3 * D, 4 * D, 5 * D, 6 * D, 7 * D, 8 * D, 9 * D)


def _adamw_small(packed, b_ada, norm_g, onorm_g, relb, hgrn_lb, final_g, ms, vs):
    def body(pk_ref, b_ref, ng_ref, og_ref, rb_ref, lb_ref, fg_ref,
             mb, mn, mo, mr, ml, mf, vb, vn, vo, vr, vl, vf,
             loss_ref, gb, gn, go, gr, gl, gf, db, dn, do, dr, dl, df,
             nmb, nmn, nmo, nmr, nml, nmf, nvb, nvn, nvo, nvr, nvl, nvf):
        tot = pk_ref[0:1, :]
        for j in range(1, N_DEV):
            tot = tot + pk_ref[8 * j:8 * j + 1, :]
        loss_ref[...] = jnp.broadcast_to(jnp.sum(tot[:, P_LOSS:P_END], axis=-1, keepdims=True), (8, 128))

        def upd(g, w_ref, m_ref, v_ref, g_out, d_out, m_out, v_out):
            g_out[...] = g
            d_out[...], m_out[...], v_out[...] = _adamw_math(w_ref[...], g, m_ref[...], v_ref[...])

        upd(tot[:, P_MOD:P_NORM], b_ref, mb, vb, gb, db, nmb, nvb)
        upd(tot[:, P_NORM:P_ONORM], ng_ref, mn, vn, gn, dn, nmn, nvn)
        g_on = tot[:, P_ONORM:P_ONORM + G_DK]
        for h in range(1, G_HEADS):
            g_on = g_on + tot[:, P_ONORM + G_DK * h:P_ONORM + G_DK * (h + 1)]
        upd(g_on, og_ref, mo, vo, go, do, nmo, nvo)
        upd(tot[:, P_RELB:P_LB], rb_ref, mr, vr, gr, dr, nmr, nvr)
        a = lb_ref[...]
        lb = _sigmoid(a[0:1, :] - a[1:2, :])
        g0 = tot[:, P_LB:P_FINAL] * lb * (1.0 - lb)
        row = lax.broadcasted_iota(jnp.int32, (2, D), 0)
        upd(jnp.where(row == 0, g0, -g0), lb_ref, ml, vl, gl, dl, nml, nvl)
        upd(tot[:, P_FINAL:P_LOSS], fg_ref, mf, vf, gf, df, nmf, nvf)

    shapes = [b_ada.shape, norm_g.shape, onorm_g.shape, relb.shape, hgrn_lb.shape, final_g.shape]
    outs = [jax.ShapeDtypeStruct((8, 128), F32)] + [jax.ShapeDtypeStruct(s, F32) for s in shapes] * 4
    return pl.pallas_call(
        body, name="adamw_small",
        out_shape=outs,
        compiler_params=_params(),
    )(packed, b_ada, norm_g, onorm_g, relb, hgrn_lb, final_g, *ms, *vs)


def _local_step(x, target, mod3, norm_g, w_in_g, onorm_g, wa, wb, wo, rel_bias, hgrn_lb, final_g):
    buckets = jnp.asarray(_bucket_tables())
    bias = _bias_tables(rel_bias, buckets)
    proj, ht = _inproj(x, mod3, norm_g, w_in_g)
    os, ls = [], []
    for p, (_, d) in enumerate(PATTERNS):
        o, l = _attn_fwd(proj, bias[p], d, "attn_fwd_d%d" % d)
        os.append(o)
        ls.append(l)
    ao, lt, oa = _attn_combine(os, ls, proj)
    o_raw, ob, states = _hgrn_fwd(proj, hgrn_lb, onorm_g)
    dx2, doa, dob, dga, dgb, tsums, gwa, gwb, gwo = _tail(x, target, oa, ob, proj, mod3, final_g, wa, wb, wo)
    do, dza, delta = _attn_pre_bwd(doa, ao, proj)
    dqs, dks, dvs, dbs = [], [], [], []
    for p, (_, d) in enumerate(PATTERNS):
        dq, dk, dv, db = _attn_bwd(proj, do, lt, delta, bias[p], d, "attn_bwd_d%d" % d)
        dqs.append(dq)
        dks.append(dk)
        dvs.append(dv)
        dbs.append(db)
    p0, p1 = _attn_assemble(dqs, dks, dvs, dza)
    g_relb = _rel_bias_grad(dbs, buckets)
    dqb, dfb, dib, dzb, dlb, dgo = _hgrn_bwd(proj, o_raw, dob, states, hgrn_lb, onorm_g)
    pieces = [p0, p1, dqb, dfb, dib, dzb, dga, dgb]
    grads = [_gw_in(ht, pieces),
             gwa.astype(XCHG_DTYPE).reshape(A_W, N_DEV, D // N_DEV).transpose(1, 0, 2),
             gwb.astype(XCHG_DTYPE).reshape(N_DEV, D // N_DEV, D),
             gwo.astype(XCHG_DTYPE).reshape(N_DEV, D // N_DEV, D)]
    gx, hsums, *received = _dh(pieces, w_in_g, x, dx2, mod3, norm_g, grads)
    row = jnp.concatenate([
        hsums[0], hsums[1], tsums[1],
        hsums[2],
        dgo.reshape(G_HEADS, 8, G_DK)[:, 0].reshape(-1),
        g_relb.reshape(-1),
        dlb[0],
        tsums[0],
        tsums[2],
    ])
    return gx, received, row


def kernel(x, c, w_ada, b_ada, norm_g, w_in, hgrn_onorm_g, w_branch_a, w_branch_b, w_out, rel_bias, hgrn_lb, final_g, loss_target, m_w_ada, m_b_ada, m_norm_g, m_w_in, m_hgrn_onorm_g, m_w_branch_a, m_w_branch_b, m_w_out, m_rel_bias, m_hgrn_lb, m_final_g, v_w_ada, v_b_ada, v_norm_g, v_w_in, v_hgrn_onorm_g, v_w_branch_a, v_w_branch_b, v_w_out, v_rel_bias, v_hgrn_lb, v_final_g):
    me = 4 * lax.axis_index("x") + 2 * lax.axis_index("y") + lax.axis_index("c")
    n_ada = w_ada.shape[2]

    w_in_g, wa_g, wb_g, wo_g = _all_gather(
        [w_in[0].astype(MXU_DTYPE), w_branch_a[0].astype(MXU_DTYPE),
         w_branch_b[0].astype(MXU_DTYPE), w_out[0].astype(MXU_DTYPE)], "gather_weights")
    wa = wa_g.transpose(1, 0, 2).reshape(A_W, D)
    wb = wb_g.reshape(D, D)
    wo = wo_g.reshape(D, D)

    (c_all,) = _all_gather([jnp.broadcast_to(c, (8, D))], "gather_c")
    c64 = c_all.reshape(8 * N_DEV, D)
    b_loc = lax.dynamic_slice(b_ada, (0, me * n_ada), (1, n_ada))
    mod_part = _mod_fwd(c64, w_ada[0], b_loc)[::8]
    (mod_all,) = _all_gather([mod_part], "gather_mod")
    mod = lax.dynamic_slice(mod_all, (0, me, 0), (N_DEV, 1, n_ada)).reshape(3, D)
    mod3 = jnp.concatenate([mod, jnp.zeros((5, D), F32)], axis=0)

    onorm_t = hgrn_onorm_g
    gx, (r_in, r_a, r_b, r_o), row = _local_step(
        x[0], loss_target[0], mod3, norm_g, w_in_g, onorm_t, wa, wb, wo, rel_bias, hgrn_lb,
        final_g.reshape(1, D))
    packed8 = jnp.concatenate([row[None, :], jnp.zeros((7, P_END), F32)], axis=0)
    (packed,) = _all_gather([packed8], "gather_small")
    packed = packed.reshape(8 * N_DEV, P_END)

    g_in, d_in, nm_in, nv_in = _adamw_big(r_in, w_in[0], m_w_in[0], v_w_in[0], "adamw_w_in")
    g_a, d_a, nm_a, nv_a = _adamw_big(r_a, w_branch_a[0], m_w_branch_a[0], v_w_branch_a[0], "adamw_w_branch_a")
    g_b, d_b, nm_b, nv_b = _adamw_big(r_b, w_branch_b[0], m_w_branch_b[0], v_w_branch_b[0], "adamw_w_branch_b")
    g_o, d_o, nm_o, nv_o = _adamw_big(r_o, w_out[0], m_w_out[0], v_w_out[0], "adamw_w_out")

    dmod64 = lax.dynamic_slice(packed, (0, P_MOD + me * n_ada), (8 * N_DEV, n_ada))
    g_ada, d_ada, nm_ada, nv_ada = _adamw_w_ada(c64, dmod64, w_ada[0], m_w_ada[0], v_w_ada[0])

    def flat_relb(t):
        return jnp.pad(t.T, ((0, 0), (0, 128 - N_BUCKETS))).reshape(1, A_HEADS * 128)

    def unflat_relb(t):
        return t.reshape(A_HEADS, 128)[:, :N_BUCKETS].T

    fg2 = lambda t: t.reshape(1, D)
    smalls = _adamw_small(
        packed, b_ada, norm_g, hgrn_onorm_g, flat_relb(rel_bias), hgrn_lb, fg2(final_g),
        [m_b_ada, m_norm_g, m_hgrn_onorm_g, flat_relb(m_rel_bias), m_hgrn_lb, fg2(m_final_g)],
        [v_b_ada, v_norm_g, v_hgrn_onorm_g, flat_relb(v_rel_bias), v_hgrn_lb, fg2(v_final_g)])
    loss = smalls[0][0, 0]

    def small(kind):
        s = smalls[1 + 6 * kind:7 + 6 * kind]
        return s[0], s[1], s[2], unflat_relb(s[3]), s[4], s[5].reshape(D)

    def leaves(ada, sm, w_in_, wa_, wb_, wo_):
        b_, n_, o_, r_, l_, f_ = sm
        return (ada[None], b_, n_, w_in_[None], o_, wa_[None], wb_[None], wo_[None], r_, l_, f_)

    return (loss, gx[None],
            *leaves(g_ada, small(0), g_in, g_a, g_b, g_o),
            *leaves(d_ada, small(1), d_in, d_a, d_b, d_o),
            *leaves(nm_ada, small(2), nm_in, nm_a, nm_b, nm_o),
            *leaves(nv_ada, small(3), nv_in, nv_a, nv_b, nv_o))
```

```python
import functools
import math

import numpy as np
import jax
import jax.numpy as jnp
from jax import lax
from jax.experimental import pallas as pl
from jax.experimental.pallas import tpu as pltpu

F32 = jnp.float32
BF16 = jnp.bfloat16
MXU_DTYPE = jnp.bfloat16
XCHG_DTYPE = jnp.bfloat16

N_DEV = 8
D = 1024
A_HEADS = 8
A_HD = 64
A_W = A_HEADS * A_HD
A_BLK = 128
PATTERNS = ((128, 1), (512, 4), (2048, 16))
N_BUCKETS = 32
MAX_DISTANCE = 2048
NEG = -1e30
G_HEADS = 8
G_DK = 128
G_W = G_HEADS * G_DK
IN_W = 8 * D
EPS = 1e-6
ADAM_LR = 0.001
ADAM_B1 = 0.9
ADAM_B2 = 0.999
ADAM_EPS = 1e-08
ADAM_WD = 0.01
ADAM_STEP = 10

G_CHUNK = 128
G_SUB = 16
G_HPS_FWD = 4
G_HPS_BWD = 4
G_RB = 16
VMEM_LIMIT = 56 * 1024 * 1024

NN = (((1,), (0,)), ((), ()))
NT = (((1,), (1,)), ((), ()))
TN = (((0,), (0,)), ((), ()))
MESH = pl.DeviceIdType.MESH


def _mm(a, b, dims=NN):
    return lax.dot_general(a.astype(MXU_DTYPE), b.astype(MXU_DTYPE), dims,
                           preferred_element_type=F32)


def _mm_exact(t, x):
    hi = x.astype(BF16)
    r = x - hi.astype(F32)
    mid = r.astype(BF16)
    lo = (r - mid.astype(F32)).astype(BF16)
    tb = t.astype(BF16)
    return sum(lax.dot_general(tb, p, NN, preferred_element_type=F32) for p in (hi, mid, lo))


def _sigmoid(x):
    return 0.5 * jnp.tanh(0.5 * x) + 0.5


def _params(sem=None):
    return pltpu.CompilerParams(dimension_semantics=sem, vmem_limit_bytes=VMEM_LIMIT)


def _all_gather(xs, name):
    n = len(xs)

    def body(*refs):
        ins, outs = refs[:n], refs[n:2 * n]
        send_sems, recv_sems, local_sems = refs[2 * n:]
        x, y, c = lax.axis_index("x"), lax.axis_index("y"), lax.axis_index("c")
        me, sibling = (x, y, c), (x, y, 1 - c)
        chips = [(1 - x, y), (x, 1 - y), (1 - x, 1 - y)]

        def slot(ref, dev):
            return ref.at[4 * dev[0] + 2 * dev[1] + dev[2]]

        def copy(a, k, block, to, src=None):
            return pltpu.make_async_remote_copy(
                src_ref=slot(outs[a], block) if src is None else src,
                dst_ref=slot(outs[a], block),
                send_sem=send_sems.at[a, k], recv_sem=recv_sems.at[a, k],
                device_id=to, device_id_type=MESH)

        mine, first, passed = [], [], []
        for a in range(n):
            cp = pltpu.make_async_copy(ins[a], slot(outs[a], me), local_sems.at[a])
            cp.start()
            mine.append(cp)
            first.append(copy(a, 0, me, sibling, src=ins[a]))
            for j, chip in enumerate(chips):
                first.append(copy(a, 1 + j, me, (*chip, c), src=ins[a]))
        for cp in first:
            cp.start()
        for j, chip in enumerate(chips):
            for a in range(n):
                copy(a, 1 + j, (*chip, c), me).wait_recv()
                cp = copy(a, 4 + j, (*chip, c), sibling)
                cp.start()
                passed.append(cp)
        for a in range(n):
            copy(a, 0, sibling, me).wait_recv()
            for j, chip in enumerate(chips):
                copy(a, 4 + j, (*chip, 1 - c), me).wait_recv()
        for cp in first + passed:
            cp.wait_send()
        for cp in mine:
            cp.wait()

    any_spec = pl.BlockSpec(memory_space=pl.ANY)
    return pl.pallas_call(
        body, name=name,
        out_shape=[jax.ShapeDtypeStruct((N_DEV,) + v.shape, v.dtype) for v in xs],
        in_specs=[any_spec] * n, out_specs=[any_spec] * n,
        scratch_shapes=[pltpu.SemaphoreType.DMA((n, 7)), pltpu.SemaphoreType.DMA((n, 7)),
                        pltpu.SemaphoreType.DMA((n,))],
    )(*xs)


def _all_to_all_copies(ins, outs, send_sems, recv_sems, local_sems, gather=False):
    n = len(ins)
    x, y, c = lax.axis_index("x"), lax.axis_index("y"), lax.axis_index("c")
    me = 4 * x + 2 * y + c
    peers = []
    for m in range(1, N_DEV):
        peers.append((1 - x if m & 4 else x, 1 - y if m & 2 else y, 1 - c if m & 1 else c))

    def chunk(a, j):
        return ins[a] if gather else ins[a].at[j]

    def copy(a, k, landing):
        peer = peers[k]
        pid = 4 * peer[0] + 2 * peer[1] + peer[2]
        return pltpu.make_async_remote_copy(
            src_ref=chunk(a, pid), dst_ref=outs[a].at[pid if landing else me],
            send_sem=send_sems.at[a, k], recv_sem=recv_sems.at[a, k],
            device_id=peer, device_id_type=MESH)

    def local(a):
        return pltpu.make_async_copy(chunk(a, me), outs[a].at[me], local_sems.at[a])

    def start():
        for a in range(n):
            local(a).start()
        for k in range(N_DEV - 1):
            for a in range(n):
                copy(a, k, False).start()

    def wait():
        for k in range(N_DEV - 1):
            for a in range(n):
                copy(a, k, True).wait_recv()
        for k in range(N_DEV - 1):
            for a in range(n):
                copy(a, k, False).wait_send()
        for a in range(n):
            local(a).wait()

    return start, wait


def _mod_fwd(c64, w_ada, b_loc):
    def body(c_ref, w_ref, b_ref, o_ref):
        cv = c_ref[...]
        sc = cv * _sigmoid(cv)
        o_ref[...] = _mm(sc, w_ref[...]) + b_ref[...]

    return pl.pallas_call(
        body, name="mod_fwd",
        out_shape=jax.ShapeDtypeStruct((c64.shape[0], w_ada.shape[1]), F32),
        compiler_params=_params(),
    )(c64, w_ada, b_loc)


def _inproj(x, mod3, norm_g, w_in_g, blocks, tm=256):
    S = x.shape[0]
    ni = S // tm
    nb = len(blocks)

    def body(*refs):
        x_ref, mod_ref, g_ref, w_ref = refs[:4]
        b_ins = refs[4:4 + nb]
        proj_ref, ht_ref = refs[4 + nb:6 + nb]
        b_outs = refs[6 + nb:6 + 2 * nb]
        w_all, send_sems, recv_sems, local_sems = refs[6 + 2 * nb:]
        i = pl.program_id(0)
        start, wait = _all_to_all_copies(b_ins, b_outs, send_sems, recv_sems, local_sems, gather=True)

        @pl.when(i == 0)
        def _():
            start()
            pltpu.sync_copy(w_ref, w_all)

        xv = x_ref[...]
        r = lax.rsqrt(jnp.mean(xv * xv, axis=-1, keepdims=True) + EPS)
        h = ((xv * r * g_ref[...]) * (1.0 + mod_ref[1:2, :]) + mod_ref[0:1, :]).astype(MXU_DTYPE)
        ht_ref[...] = h.T
        for j in range(N_DEV):
            proj_ref[:, j * D:(j + 1) * D] = _mm(h, w_all[j])

        @pl.when(i == ni - 1)
        def _():
            wait()

    any_spec = pl.BlockSpec(memory_space=pl.ANY)
    return pl.pallas_call(
        body, name="inproj_gather",
        grid=(ni,),
        in_specs=[pl.BlockSpec((tm, D), lambda i: (i, 0)),
                  pl.BlockSpec((8, D), lambda i: (0, 0)),
                  pl.BlockSpec((1, D), lambda i: (0, 0)),
                  any_spec] + [any_spec] * nb,
        out_specs=[pl.BlockSpec((tm, IN_W), lambda i: (i, 0)),
                   pl.BlockSpec((D, tm), lambda i: (0, i))] + [any_spec] * nb,
        out_shape=[jax.ShapeDtypeStruct((S, IN_W), F32), jax.ShapeDtypeStruct((D, S), MXU_DTYPE)]
                  + [jax.ShapeDtypeStruct((N_DEV,) + b.shape, b.dtype) for b in blocks],
        scratch_shapes=[pltpu.VMEM(w_in_g.shape, w_in_g.dtype),
                        pltpu.SemaphoreType.DMA((nb, N_DEV - 1)), pltpu.SemaphoreType.DMA((nb, N_DEV - 1)),
                        pltpu.SemaphoreType.DMA((nb,))],
        compiler_params=_params(("arbitrary",)),
    )(x, mod3, norm_g, w_in_g, *blocks)


def _bucket_tables():
    qi = np.arange(A_BLK)[:, None]
    kj = np.arange(2 * A_BLK)[None, :]
    delta = qi + A_BLK - kj
    out = []
    for window, dil in PATTERNS:
        span = window // dil
        band = (delta >= 0) & (delta <= span)
        dist = np.clip(delta, 0, None) * dil
        max_exact = N_BUCKETS // 2
        nf = dist.astype(np.float32)
        large = max_exact + (np.log(np.maximum(nf, np.float32(1.0)) / np.float32(max_exact))
                             / np.float32(math.log(MAX_DISTANCE / max_exact))
                             * np.float32(N_BUCKETS - max_exact)).astype(np.int32)
        large = np.minimum(large, N_BUCKETS - 1)
        bucket = np.where(dist < max_exact, dist, large)
        out.append(np.where(band, bucket, -1).astype(np.int32))
    return np.stack(out)


def _bias_tables(rel_bias, buckets):
    def body(rb_ref, bk_ref, o_ref):
        h = pl.program_id(1)
        bk = bk_ref[0]
        acc = jnp.full(bk.shape, NEG, F32)
        for b in range(N_BUCKETS):
            acc = jnp.where(bk == b, rb_ref[b, h], acc)
        o_ref[0, 0] = acc

    return pl.pallas_call(
        body, name="bias_tables",
        grid=(3, A_HEADS),
        in_specs=[pl.BlockSpec(memory_space=pltpu.SMEM),
                  pl.BlockSpec((1, A_BLK, 2 * A_BLK), lambda p, h: (p, 0, 0))],
        out_specs=pl.BlockSpec((1, 1, A_BLK, 2 * A_BLK), lambda p, h: (p, h, 0, 0)),
        out_shape=jax.ShapeDtypeStruct((3, A_HEADS, A_BLK, 2 * A_BLK), F32),
        compiler_params=_params(("arbitrary", "arbitrary")),
    )(rel_bias, buckets)


A_TILES = 16


def _attn_heads_per_step(d):
    return A_HEADS if d == 1 else 2


def _attn_in_specs(sb, nsb, hw):
    w = A_HD * hw
    per = A_W // w

    def cur(col):
        return pl.BlockSpec((sb, w), lambda hp, n: (jnp.minimum(n, nsb - 1), per * col + hp))

    def prev(col):
        return pl.BlockSpec((sb, w), lambda hp, n: (jnp.maximum(jnp.minimum(n, nsb - 1) - 1, 0), per * col + hp))

    return [cur(0), prev(1), cur(1), prev(2), cur(2)]


def _rows(r, d):
    return pl.ds(r, A_BLK) if d == 1 else pl.ds(r, A_BLK, stride=d)


def _for_residues(d, hw, fn):
    unroll = min(d, max(1, A_TILES // hw))
    if d == unroll:
        for r in range(d):
            fn(r)
    else:
        def group(g, c):
            for u in range(unroll):
                fn(g * unroll + u)
            return c
        lax.fori_loop(0, d // unroll, group, 0)


def _attn_stack(t):
    first_half = lax.broadcasted_iota(jnp.int32, (1, 2 * A_HD), 1) < A_HD
    return jnp.concatenate([jnp.where(first_half, t, 0.0), jnp.where(first_half, 0.0, t)], axis=0)


def _attn_unstack(t2):
    first_half = lax.broadcasted_iota(jnp.int32, (1, 2 * A_HD), 1) < A_HD
    return jnp.where(first_half, t2[:A_BLK], t2[A_BLK:])


def _attn_scores(q, k, b_ref, pp, first):
    bias = jnp.concatenate([b_ref[2 * pp] + first, b_ref[2 * pp + 1] + first], axis=0)
    return _mm(_attn_stack(q), k, NT) * (A_HD ** -0.5) + bias


def _attn_fwd(proj, bias_p, d, name):
    S = proj.shape[0]
    sb = A_BLK * d
    nsb = S // sb
    hw = _attn_heads_per_step(d)

    def body(q_ref, kp_ref, kc_ref, vp_ref, vc_ref, b_ref, o_ref, l_ref):
        n = pl.program_id(1)
        kj = lax.broadcasted_iota(jnp.int32, (A_BLK, 2 * A_BLK), 1)
        first = jnp.where((n == 0) & (kj < A_BLK), NEG, 0.0).astype(F32)

        def residue(r):
            rows = _rows(r, d)
            for pp in range(hw // 2):
                lanes = pl.ds(2 * A_HD * pp, 2 * A_HD)
                k = jnp.concatenate([kp_ref[rows, lanes], kc_ref[rows, lanes]], axis=0)
                v = jnp.concatenate([vp_ref[rows, lanes], vc_ref[rows, lanes]], axis=0)
                s = _attn_scores(q_ref[rows, lanes], k, b_ref, pp, first)
                m = jnp.max(s, axis=-1, keepdims=True)
                p = jnp.exp(s - m)
                den = jnp.sum(p, axis=-1, keepdims=True)
                o_ref[rows, lanes] = _attn_unstack(_mm(p, v) / den)
                l_ref[rows, lanes] = _attn_unstack(jnp.broadcast_to(m + jnp.log(den), (2 * A_BLK, 2 * A_HD)))

        _for_residues(d, hw, residue)

    out = pl.BlockSpec((sb, A_HD * hw), lambda hp, n: (n, hp))
    return pl.pallas_call(
        body, name=name,
        grid=(A_HEADS // hw, nsb),
        in_specs=_attn_in_specs(sb, nsb, hw) + [pl.BlockSpec((hw, A_BLK, 2 * A_BLK), lambda hp, n: (hp, 0, 0))],
        out_specs=[out, out],
        out_shape=[jax.ShapeDtypeStruct((S, A_W), F32)] * 2,
        compiler_params=_params(("parallel", "parallel")),
    )(proj, proj, proj, proj, proj, bias_p)


def _attn_combine(os, ls, proj, tm=512):
    S = proj.shape[0]

    def body(o1, o2, o3, l1, l2, l3, z_ref, ao_ref, lt_ref, oa_ref):
        a1, a2, a3 = l1[...], l2[...], l3[...]
        m = jnp.maximum(jnp.maximum(a1, a2), a3)
        e1, e2, e3 = jnp.exp(a1 - m), jnp.exp(a2 - m), jnp.exp(a3 - m)
        den = e1 + e2 + e3
        ao = (e1 * o1[...] + e2 * o2[...] + e3 * o3[...]) / den
        z = z_ref[...]
        ao_ref[...] = ao
        lt_ref[...] = m + jnp.log(den)
        oa_ref[...] = (ao * (z * _sigmoid(z))).astype(MXU_DTYPE)

    spec = pl.BlockSpec((tm, A_W), lambda i: (i, 0))
    return pl.pallas_call(
        body, name="attn_combine",
        grid=(S // tm,),
        in_specs=[spec] * 6 + [pl.BlockSpec((tm, A_W), lambda i: (i, 3))],
        out_specs=[spec] * 3,
        out_shape=[jax.ShapeDtypeStruct((S, A_W), F32), jax.ShapeDtypeStruct((S, A_W), F32),
                   jax.ShapeDtypeStruct((S, A_W), MXU_DTYPE)],
        compiler_params=_params(("parallel",)),
    )(*os, *ls, proj)


def _attn_pre_bwd(doa, ao, proj, tm=512):
    S = proj.shape[0]

    def body(doa_ref, ao_ref, z_ref, do_ref, dz_ref, dl_ref):
        z = z_ref[...]
        sg = _sigmoid(z)
        g = doa_ref[...]
        ao_v = ao_ref[...]
        do = g * (z * sg)
        do_ref[...] = do
        dz_ref[...] = (g * ao_v * (sg * (1.0 + z * (1.0 - sg)))).astype(MXU_DTYPE)
        prod = do * ao_v
        for h in range(A_HEADS):
            sl = slice(A_HD * h, A_HD * (h + 1))
            dl_ref[:, sl] = jnp.broadcast_to(jnp.sum(prod[:, sl], axis=-1, keepdims=True), (tm, A_HD))

    spec = pl.BlockSpec((tm, A_W), lambda i: (i, 0))
    return pl.pallas_call(
        body, name="attn_pre_bwd",
        grid=(S // tm,),
        in_specs=[spec, spec, pl.BlockSpec((tm, A_W), lambda i: (i, 3))],
        out_specs=[spec] * 3,
        out_shape=[jax.ShapeDtypeStruct((S, A_W), F32), jax.ShapeDtypeStruct((S, A_W), MXU_DTYPE),
                   jax.ShapeDtypeStruct((S, A_W), F32)],
        compiler_params=_params(("parallel",)),
    )(doa, ao, proj)


def _attn_bwd(proj, do, lt, delta, bias_p, d, name):
    S = proj.shape[0]
    sb = A_BLK * d
    nsb = S // sb
    hw = _attn_heads_per_step(d)

    def body(q_ref, kp_ref, kc_ref, vp_ref, vc_ref, do_ref, lt_ref, dl_ref, b_ref,
             dq_ref, dk_ref, dv_ref, db_ref, ck, cv):
        n = pl.program_id(1)

        @pl.when(n == 0)
        def _():
            db_ref[...] = jnp.zeros_like(db_ref)
            ck[...] = jnp.zeros_like(ck)
            cv[...] = jnp.zeros_like(cv)

        @pl.when(n < nsb)
        def _():
            kj = lax.broadcasted_iota(jnp.int32, (A_BLK, 2 * A_BLK), 1)
            first = jnp.where((n == 0) & (kj < A_BLK), NEG, 0.0).astype(F32)

            def residue(r):
                rows = _rows(r, d)
                for pp in range(hw // 2):
                    lanes = pl.ds(2 * A_HD * pp, 2 * A_HD)
                    lt_r, dl_r = lt_ref[rows, lanes], dl_ref[rows, lanes]
                    k = jnp.concatenate([kp_ref[rows, lanes], kc_ref[rows, lanes]], axis=0)
                    v = jnp.concatenate([vp_ref[rows, lanes], vc_ref[rows, lanes]], axis=0)
                    q2 = _attn_stack(q_ref[rows, lanes])
                    do2 = _attn_stack(do_ref[rows, lanes])
                    col = lambda t: jnp.concatenate([t[:, 0:1], t[:, A_HD:A_HD + 1]], axis=0)
                    s = _attn_scores(q_ref[rows, lanes], k, b_ref, pp, first)
                    p = jnp.exp(s - col(lt_r))
                    ds = p * (_mm(do2, v, NT) - col(dl_r))
                    db_ref[2 * pp] += ds[:A_BLK]
                    db_ref[2 * pp + 1] += ds[A_BLK:]
                    dq_ref[rows, lanes] = _attn_unstack(_mm(ds, k)) * (A_HD ** -0.5)
                    dk = _mm(ds, q2, TN) * (A_HD ** -0.5)
                    dv = _mm(p, do2, TN)
                    dk_ref[rows, lanes] = ck[rows, lanes] + dk[:A_BLK]
                    dv_ref[rows, lanes] = cv[rows, lanes] + dv[:A_BLK]
                    ck[rows, lanes] = dk[A_BLK:]
                    cv[rows, lanes] = dv[A_BLK:]

            _for_residues(d, hw, residue)

        @pl.when(n == nsb)
        def _():
            dk_ref[...] = ck[...]
            dv_ref[...] = cv[...]

    w = A_HD * hw
    row = pl.BlockSpec((sb, w), lambda hp, n: (jnp.minimum(n, nsb - 1), hp))
    lag = pl.BlockSpec((sb, w), lambda hp, n: (jnp.maximum(n - 1, 0), hp))
    tab = pl.BlockSpec((hw, A_BLK, 2 * A_BLK), lambda hp, n: (hp, 0, 0))
    return pl.pallas_call(
        body, name=name,
        grid=(A_HEADS // hw, nsb + 1),
        in_specs=_attn_in_specs(sb, nsb, hw) + [row, row, row, tab],
        out_specs=[row, lag, lag, tab],
        out_shape=[jax.ShapeDtypeStruct((S, A_W), F32)] * 3
                  + [jax.ShapeDtypeStruct((A_HEADS, A_BLK, 2 * A_BLK), F32)],
        scratch_shapes=[pltpu.VMEM((sb, w), F32), pltpu.VMEM((sb, w), F32)],
        compiler_params=_params(("parallel", "arbitrary")),
    )(proj, proj, proj, proj, proj, do, lt, delta, bias_p)


def _attn_assemble(dqs, dks, dvs, dz, tm=512):
    S = dz.shape[0]

    def body(q1, q2, q3, k1, k2, k3, v1, v2, v3, z_ref, p0_ref, p1_ref):
        p0_ref[:, :A_W] = (q1[...] + q2[...] + q3[...]).astype(MXU_DTYPE)
        p0_ref[:, A_W:] = (k1[...] + k2[...] + k3[...]).astype(MXU_DTYPE)
        p1_ref[:, :A_W] = (v1[...] + v2[...] + v3[...]).astype(MXU_DTYPE)
        p1_ref[:, A_W:] = z_ref[...]

    spec = pl.BlockSpec((tm, A_W), lambda i: (i, 0))
    wide = pl.BlockSpec((tm, 2 * A_W), lambda i: (i, 0))
    return pl.pallas_call(
        body, name="attn_assemble",
        grid=(S // tm,),
        in_specs=[spec] * 10,
        out_specs=[wide, wide],
        out_shape=[jax.ShapeDtypeStruct((S, 2 * A_W), MXU_DTYPE)] * 2,
        compiler_params=_params(("parallel",)),
    )(*dqs, *dks, *dvs, dz)


def _rel_bias_grad(dbs, buckets):
    def body(d1, d2, d3, bk_ref, o_ref):
        row = lax.broadcasted_iota(jnp.int32, (A_HEADS, 128), 0)
        lane = lax.broadcasted_iota(jnp.int32, (A_HEADS, 128), 1)
        acc = jnp.zeros((A_HEADS, 128), F32)
        for p, dref in enumerate((d1, d2, d3)):
            bk = bk_ref[p]
            for h in range(A_HEADS):
                ds = dref[h]
                for b in range(N_BUCKETS):
                    s = jnp.sum(jnp.where(bk == b, ds, 0.0), keepdims=True)
                    acc = acc + jnp.where((row == h) & (lane == b), s, 0.0)
        o_ref[...] = acc

    return pl.pallas_call(
        body, name="rel_bias_grad",
        out_shape=jax.ShapeDtypeStruct((A_HEADS, 128), F32),
        compiler_params=_params(),
    )(*dbs, buckets)


def _tri(c):
    t = np.tril(np.ones((c, c), np.float32))
    return jnp.asarray(t), jnp.asarray(t.T.copy())


def _fill_above(ref, x, pad):
    ref[0:G_SUB, :] = jnp.full((G_SUB, x.shape[1]), pad, F32)
    ref[G_SUB:, :] = x


def _fill_below(ref, x, pad):
    ref[0:x.shape[0], :] = x
    ref[x.shape[0]:, :] = jnp.full((G_SUB, x.shape[1]), pad, F32)


def _hgrn_gates(q_ref, f_ref, lbp_ref, tri_ref):
    qraw = q_ref[...]
    sq = _sigmoid(qraw)
    q = qraw * sq
    sg = _sigmoid(f_ref[...])
    lb = _sigmoid(lbp_ref[0:1, :] - lbp_ref[1:2, :])
    f = lb + (1.0 - lb) * sg
    k = 1.0 - f
    b = _mm_exact(tri_ref[...], jnp.log(f))
    return qraw, sq, q, sg, lb, f, k, b


def _hgrn_col(C, base, idx, hps):
    return pl.BlockSpec((C, hps * G_DK), lambda h, n: (idx(n), base * (G_HEADS // hps) + h))


def _hgrn_levels(C):
    out, m = [], G_SUB
    while 2 * m <= C:
        out.append(m)
        m *= 2
    return out


def _hgrn_level_masks(C):
    ti = np.arange(C)[:, None]
    si = np.arange(C)[None, :]
    return jnp.asarray(np.stack([((ti // (2 * m) == si // (2 * m)) & (ti - si >= G_SUB)).astype(np.float32)
                                 for m in _hgrn_levels(C)]))


def _hgrn_level(b, q, k, C, m):
    zeros = jnp.zeros((m, G_DK), F32)
    eq, ek, qt, kt = [], [], [], []
    for blk in range(0, C // m, 2):
        lo, mid, hi = blk * m, (blk + 1) * m, (blk + 2) * m
        ref = b[mid:mid + 1]
        e_right = jnp.exp(b[mid:hi] - ref)
        e_left = jnp.exp(ref - b[lo:mid])
        eq += [zeros, e_right]
        ek += [e_left, zeros]
        qt += [zeros, q[mid:hi] * e_right]
        kt += [k[lo:mid] * e_left, zeros]
    cat = lambda parts: jnp.concatenate(parts, axis=0)
    return cat(qt), cat(kt), cat(eq), cat(ek)


def _hgrn_fwd(proj, hgrn_lb, onorm_g, C=G_CHUNK):
    S = proj.shape[0]
    nc = S // C
    tri, _ = _tri(C)
    masks = _hgrn_level_masks(C)
    hps = G_HPS_FWD

    def body(q_ref, f_ref, i_ref, z_ref, lbp_ref, go_ref, tri_ref, pm_ref, o_ref, ob_ref, st_ref, St, kp, vp, fp):
        @pl.when(pl.program_id(1) == 0)
        def _():
            St[...] = jnp.zeros_like(St)

        for hh in range(hps):
            ln = pl.ds(G_DK * hh, G_DK)
            head(q_ref.at[:, ln], f_ref.at[:, ln], i_ref.at[:, ln], z_ref.at[:, ln], lbp_ref.at[:, ln], go_ref,
                 tri_ref, pm_ref, o_ref.at[:, ln], ob_ref.at[:, ln], st_ref.at[0, hh], St.at[hh], kp.at[hh], vp.at[hh],
                 fp.at[hh])

    def head(q_ref, f_ref, i_ref, z_ref, lbp_ref, go_ref, tri_ref, pm_ref, o_ref, ob_ref, st_ref, St, kp, vp, fp):
        _, _, q, _, _, f, k, b = _hgrn_gates(q_ref, f_ref, lbp_ref, tri_ref)
        v = i_ref[...]
        bC = b[C - 1:C, :]
        S0 = St[...]
        o = _mm(q * jnp.exp(b), S0, NT)
        _fill_above(kp, k, 0.0)
        _fill_above(vp, v, 0.0)
        _fill_above(fp, f, 1.0)
        near = []
        for r0 in range(0, C, G_RB):
            qb = q[r0:r0 + G_RB]
            acc = e = None
            for l in range(G_SUB):
                rows = pl.ds(G_SUB - l + r0, G_RB)
                if l > 0:
                    fl = fp[pl.ds(G_SUB - l + 1 + r0, G_RB), :]
                    e = fl if e is None else e * fl
                kl = kp[rows, :]
                a = jnp.sum(qb * kl if e is None else qb * kl * e, axis=-1, keepdims=True)
                t = a * vp[rows, :]
                acc = t if acc is None else acc + t
            near.append(acc)
        o = o + jnp.concatenate(near, axis=0)
        a_off = jnp.zeros((C, C), F32)
        for lv, m in enumerate(_hgrn_levels(C)):
            qt, kt, _, _ = _hgrn_level(b, q, k, C, m)
            a_off = a_off + pm_ref[lv] * _mm(qt, kt, NT)
        o = o + _mm(a_off, v)
        S1 = S0 * jnp.exp(bC) + _mm(v, k * jnp.exp(bC - b), TN)
        St[...] = S1
        st_ref[...] = S1
        o_ref[...] = o
        r = lax.rsqrt(jnp.mean(o * o, axis=-1, keepdims=True) + EPS)
        z = z_ref[...]
        ob_ref[...] = (o * r * go_ref[...] * (z * _sigmoid(z))).astype(MXU_DTYPE)

    ident = lambda n: n
    w = hps * G_DK
    out = pl.BlockSpec((C, w), lambda h, n: (n, h))
    return pl.pallas_call(
        body, name="hgrn_fwd",
        grid=(G_HEADS // hps, nc),
        in_specs=[_hgrn_col(C, base, ident, hps) for base in (2, 3, 4, 5)] + [
                  pl.BlockSpec((2, w), lambda h, n: (0, h)),
                  pl.BlockSpec((1, G_DK), lambda h, n: (0, 0)),
                  pl.BlockSpec((C, C), lambda h, n: (0, 0)),
                  pl.BlockSpec(masks.shape, lambda h, n: (0, 0, 0))],
        out_specs=[out, out, pl.BlockSpec((1, hps, G_DK, G_DK), lambda h, n: (n, h, 0, 0))],
        out_shape=[jax.ShapeDtypeStruct((S, G_W), F32), jax.ShapeDtypeStruct((S, G_W), MXU_DTYPE),
                   jax.ShapeDtypeStruct((nc, G_HEADS, G_DK, G_DK), F32)],
        scratch_shapes=[pltpu.VMEM((hps, G_DK, G_DK), F32)] + [pltpu.VMEM((hps, C + G_SUB, G_DK), F32)] * 3,
        compiler_params=_params(("parallel", "arbitrary")),
    )(proj, proj, proj, proj, hgrn_lb, onorm_g, tri, masks)


def _hgrn_bwd(proj, o_raw, dob, states, hgrn_lb, onorm_g, C=G_CHUNK):
    S = proj.shape[0]
    nc = S // C
    tri, triu = _tri(C)
    masks = _hgrn_level_masks(C)
    hps = G_HPS_BWD

    def body(q_ref, f_ref, i_ref, z_ref, o_ref, dob_ref, s0_ref, s1_ref, lbp_ref, go_ref, tri_ref, triu_ref,
             pm_ref, dq_ref, df_ref, di_ref, dz_ref, dlb_ref, dgo_ref, dSt, *shifted):
        @pl.when(pl.program_id(1) == 0)
        def _():
            dSt[...] = jnp.zeros_like(dSt)
            dlb_ref[...] = jnp.zeros_like(dlb_ref)
            dgo_ref[...] = jnp.zeros_like(dgo_ref)

        for hh in range(hps):
            ln = pl.ds(G_DK * hh, G_DK)
            head(q_ref.at[:, ln], f_ref.at[:, ln], i_ref.at[:, ln], z_ref.at[:, ln], o_ref.at[:, ln],
                 dob_ref.at[:, ln], s0_ref.at[0, hh], s1_ref.at[0, hh], lbp_ref.at[:, ln], go_ref, tri_ref, triu_ref,
                 pm_ref, dq_ref.at[:, ln], df_ref.at[:, ln], di_ref.at[:, ln], dz_ref.at[:, ln], dlb_ref.at[:, ln],
                 dgo_ref.at[pl.ds(8 * hh, 8), :], dSt.at[hh], *[t.at[hh] for t in shifted])

    def head(q_ref, f_ref, i_ref, z_ref, o_ref, dob_ref, s0_ref, s1_ref, lbp_ref, go_ref, tri_ref, triu_ref,
             pm_ref, dq_ref, df_ref, di_ref, dz_ref, dlb_ref, dgo_ref, dSt, kp, vp, fp, qn, dn_, fn, xs, dac):
        cn = nc - 1 - pl.program_id(1)
        qraw, sq, q, sg, lb, f, k, b = _hgrn_gates(q_ref, f_ref, lbp_ref, tri_ref)
        v = i_ref[...]
        bC = b[C - 1:C, :]
        eb = jnp.exp(b)
        ecb = jnp.exp(bC - b)
        o = o_ref[...]
        z = z_ref[...]
        sz = _sigmoid(z)
        go = go_ref[...]
        g_ob = dob_ref[...]
        r = lax.rsqrt(jnp.mean(o * o, axis=-1, keepdims=True) + EPS)
        nh = o * r
        dnrm = g_ob * (z * sz)
        dz_ref[...] = (g_ob * (nh * go) * (sz * (1.0 + z * (1.0 - sz)))).astype(MXU_DTYPE)
        dgo_ref[0:1, :] += jnp.sum(dnrm * nh, axis=0, keepdims=True)
        dn = dnrm * go
        do = r * (dn - nh * jnp.mean(dn * nh, axis=-1, keepdims=True))

        S0 = jnp.where(cn == 0, 0.0, s0_ref[...])
        S1 = s1_ref[...]
        dS1 = dSt[...]
        dq = eb * _mm(do, S0)
        dk = ecb * _mm(v, dS1)
        dv = _mm(k * ecb, dS1, NT)
        bnd = jnp.sum(dS1 * S1, axis=0, keepdims=True)
        dSt[...] = dS1 * jnp.exp(bC) + _mm(do, q * eb, TN)

        _fill_above(kp, k, 0.0)
        _fill_above(vp, v, 0.0)
        _fill_above(fp, f, 1.0)
        _fill_below(qn, q, 0.0)
        _fill_below(dn_, do, 0.0)
        _fill_below(fn, f, 1.0)
        for r0 in range(0, C, G_RB):
            do_b = do[r0:r0 + G_RB]
            for l in range(G_SUB):
                xs[pl.ds(l * C + r0, G_RB), :] = (do_b * vp[pl.ds(G_SUB - l + r0, G_RB), :]).astype(MXU_DTYPE)
        dac[0:G_SUB * C, :] = _mm(xs[...], jnp.ones((G_DK, G_DK), MXU_DTYPE))
        dac[G_SUB * C:, :] = jnp.zeros((G_SUB, G_DK), F32)
        near_q, near_k, near_v = [], [], []
        for r0 in range(0, C, G_RB):
            k_b = k[r0:r0 + G_RB]
            aq = ak = av = e = e2 = None
            for l in range(G_SUB):
                down, up = pl.ds(G_SUB - l + r0, G_RB), pl.ds(l + r0, G_RB)
                if l > 0:
                    fl = fp[pl.ds(G_SUB - l + 1 + r0, G_RB), :]
                    e = fl if e is None else e * fl
                    fu = fn[up, :]
                    e2 = fu if e2 is None else e2 * fu
                kl = kp[down, :]
                t = dac[pl.ds(l * C + r0, G_RB), :] * (kl if e is None else kl * e)
                aq = t if aq is None else aq + t
                qu = qn[up, :]
                qe = qu if e2 is None else qu * e2
                dou = dn_[up, :]
                a2 = jnp.sum(qe * k_b, axis=-1, keepdims=True)
                t = dac[pl.ds(l * C + l + r0, G_RB), :] * qe
                ak = t if ak is None else ak + t
                t = a2 * dou
                av = t if av is None else av + t
            near_q.append(aq)
            near_k.append(ak)
            near_v.append(av)
        dq = dq + jnp.concatenate(near_q, axis=0)
        dk = dk + jnp.concatenate(near_k, axis=0)
        dv = dv + jnp.concatenate(near_v, axis=0)

        da_all = _mm(do, v, NT)
        a_off = jnp.zeros((C, C), F32)
        for lv, m in enumerate(_hgrn_levels(C)):
            qt, kt, eq, ek = _hgrn_level(b, q, k, C, m)
            da_m = pm_ref[lv] * da_all
            a_off = a_off + pm_ref[lv] * _mm(qt, kt, NT)
            dq = dq + _mm(da_m, kt) * eq
            dk = dk + _mm(da_m, qt, TN) * ek
        dv = dv + _mm(a_off, do, TN)

        row = lax.broadcasted_iota(jnp.int32, (C, 1), 0)
        db = q * dq - k * dk + jnp.where(row == C - 1, bnd, 0.0)
        dg = _mm_exact(triu_ref[...], db)
        df = dg / f - dk
        df_ref[...] = (df * (1.0 - lb) * (sg * (1.0 - sg))).astype(MXU_DTYPE)
        dlb_ref[0:1, :] += jnp.sum(df * (1.0 - sg), axis=0, keepdims=True)
        dq_ref[...] = (dq * (sq * (1.0 + qraw * (1.0 - sq)))).astype(MXU_DTYPE)
        di_ref[...] = dv.astype(MXU_DTYPE)

    rev = lambda n: nc - 1 - n
    w = hps * G_DK
    blk = pl.BlockSpec((C, w), lambda h, n: (nc - 1 - n, h))
    return pl.pallas_call(
        body, name="hgrn_bwd",
        grid=(G_HEADS // hps, nc),
        in_specs=[_hgrn_col(C, base, rev, hps) for base in (2, 3, 4, 5)] + [
                  blk, blk,
                  pl.BlockSpec((1, hps, G_DK, G_DK), lambda h, n: (jnp.maximum(nc - 2 - n, 0), h, 0, 0)),
                  pl.BlockSpec((1, hps, G_DK, G_DK), lambda h, n: (nc - 1 - n, h, 0, 0)),
                  pl.BlockSpec((2, w), lambda h, n: (0, h)),
                  pl.BlockSpec((1, G_DK), lambda h, n: (0, 0)),
                  pl.BlockSpec((C, C), lambda h, n: (0, 0)),
                  pl.BlockSpec((C, C), lambda h, n: (0, 0)),
                  pl.BlockSpec(masks.shape, lambda h, n: (0, 0, 0))],
        out_specs=[blk, blk, blk, blk,
                   pl.BlockSpec((8, w), lambda h, n: (0, h)),
                   pl.BlockSpec((8 * hps, G_DK), lambda h, n: (h, 0))],
        out_shape=[jax.ShapeDtypeStruct((S, G_W), MXU_DTYPE)] * 4
                  + [jax.ShapeDtypeStruct((8, G_W), F32), jax.ShapeDtypeStruct((8 * G_HEADS, G_DK), F32)],
        scratch_shapes=[pltpu.VMEM((hps, G_DK, G_DK), F32)] + [pltpu.VMEM((hps, C + G_SUB, G_DK), F32)] * 6
                       + [pltpu.VMEM((hps, G_SUB * C, G_DK), MXU_DTYPE),
                          pltpu.VMEM((hps, G_SUB * C + G_SUB, G_DK), F32)],
        compiler_params=_params(("parallel", "arbitrary")),
    )(proj, proj, proj, proj, o_raw, dob, states, states, hgrn_lb, onorm_g, tri, triu, masks)


def _tail(x, target, oa, ob, proj, mod3, final_g, wa, wb, wo, tm=256):
    S = x.shape[0]
    nt = S // tm

    def body(x_ref, t_ref, oa_ref, ob_ref, ga_ref, gb_ref, mod_ref, fg_ref, wa_ref, wb_ref, wo_ref,
             dx2_ref, doa_ref, dob_ref, dga_ref, dgb_ref, sums_ref, gwa_ref, gwb_ref, gwo_ref,
             acc_a, acc_b, acc_o):
        i = pl.program_id(0)

        @pl.when(i == 0)
        def _():
            sums_ref[...] = jnp.zeros_like(sums_ref)
            acc_a[...] = jnp.zeros_like(acc_a)
            acc_b[...] = jnp.zeros_like(acc_b)
            acc_o[...] = jnp.zeros_like(acc_o)

        oa_v, ob_v = oa_ref[...], ob_ref[...]
        pa = _mm(oa_v, wa_ref[...])
        pb = _mm(ob_v, wb_ref[...])
        sa, sb = _sigmoid(ga_ref[...]), _sigmoid(gb_ref[...])
        ym = sa * pa + sb * pb
        u = _mm(ym, wo_ref[...])
        gate = mod_ref[2:3, :]
        fg = fg_ref[...]
        x2 = x_ref[...] + gate * u
        r2 = lax.rsqrt(jnp.mean(x2 * x2, axis=-1, keepdims=True) + EPS)
        xn2 = x2 * r2
        e = xn2 * fg - t_ref[...]
        dy = e * (1.0 / D)
        dn = dy * fg
        dx2 = r2 * (dn - xn2 * jnp.mean(dn * xn2, axis=-1, keepdims=True))
        dx2_ref[...] = dx2
        sums_ref[0:1, :] += jnp.sum(dy * xn2, axis=0, keepdims=True)
        sums_ref[1:2, :] += jnp.sum(dx2 * u, axis=0, keepdims=True)
        sums_ref[2:3, :] += (0.5 / D) * jnp.sum(e * e, axis=0, keepdims=True)
        du = dx2 * gate
        dym = _mm(du, wo_ref[...], NT)
        acc_o[...] += _mm(ym, du, TN)
        dpa, dpb = dym * sa, dym * sb
        dga_ref[...] = (dym * pa * (sa * (1.0 - sa))).astype(MXU_DTYPE)
        dgb_ref[...] = (dym * pb * (sb * (1.0 - sb))).astype(MXU_DTYPE)
        doa_ref[...] = _mm(dpa, wa_ref[...], NT)
        dob_ref[...] = _mm(dpb, wb_ref[...], NT)
        acc_a[...] += _mm(oa_v, dpa, TN)
        acc_b[...] += _mm(ob_v, dpb, TN)

        @pl.when(i == nt - 1)
        def _():
            pltpu.sync_copy(acc_a, gwa_ref)
            pltpu.sync_copy(acc_b, gwb_ref)
            pltpu.sync_copy(acc_o, gwo_ref)

    row = lambda w: pl.BlockSpec((tm, w), lambda i: (i, 0))
    full = lambda a, b: pl.BlockSpec((a, b), lambda i: (0, 0))
    any_spec = pl.BlockSpec(memory_space=pl.ANY)
    return pl.pallas_call(
        body, name="tail",
        grid=(nt,),
        in_specs=[row(D), row(D), row(A_W), row(D),
                  pl.BlockSpec((tm, D), lambda i: (i, 6)), pl.BlockSpec((tm, D), lambda i: (i, 7)),
                  full(8, D), full(1, D), full(A_W, D), full(D, D), full(D, D)],
        out_specs=[row(D), row(A_W), row(D), row(D), row(D), full(8, D), any_spec, any_spec, any_spec],
        out_shape=[jax.ShapeDtypeStruct((S, D), F32), jax.ShapeDtypeStruct((S, A_W), F32),
                   jax.ShapeDtypeStruct((S, D), F32), jax.ShapeDtypeStruct((S, D), MXU_DTYPE),
                   jax.ShapeDtypeStruct((S, D), MXU_DTYPE), jax.ShapeDtypeStruct((8, D), F32),
                   jax.ShapeDtypeStruct((A_W, D), F32), jax.ShapeDtypeStruct((D, D), F32),
                   jax.ShapeDtypeStruct((D, D), F32)],
        scratch_shapes=[pltpu.VMEM((A_W, D), F32), pltpu.VMEM((D, D), F32), pltpu.VMEM((D, D), F32)],
        compiler_params=_params(("arbitrary",)),
    )(x, target, oa, ob, proj, proj, mod3, final_g, wa, wb, wo)


def _dh(pieces, w_in_g, x, dx2, mod3, norm_g, grads, tm=256):
    S = x.shape[0]
    ni = S // tm
    ng = len(grads)

    def body(*refs):
        p_refs = refs[:N_DEV]
        w_ref, x_ref, dx2_ref, mod_ref, g_ref = refs[N_DEV:N_DEV + 5]
        g_ins = refs[N_DEV + 5:N_DEV + 5 + ng]
        gx_ref, sums_ref = refs[N_DEV + 5 + ng:N_DEV + 7 + ng]
        g_outs = refs[N_DEV + 7 + ng:N_DEV + 7 + 2 * ng]
        w_all, send_sems, recv_sems, local_sems = refs[N_DEV + 7 + 2 * ng:]
        i = pl.program_id(0)
        start, wait = _all_to_all_copies(g_ins, g_outs, send_sems, recv_sems, local_sems)

        @pl.when(i == 0)
        def _():
            start()
            sums_ref[...] = jnp.zeros_like(sums_ref)
            pltpu.sync_copy(w_ref, w_all)

        dh = _mm(p_refs[0][...], w_all[0], NT)
        for k in range(1, N_DEV):
            dh = dh + _mm(p_refs[k][...], w_all[k], NT)
        xv = x_ref[...]
        g = g_ref[...]
        sc1 = 1.0 + mod_ref[1:2, :]
        r = lax.rsqrt(jnp.mean(xv * xv, axis=-1, keepdims=True) + EPS)
        xn = xv * r
        sums_ref[0:1, :] += jnp.sum(dh, axis=0, keepdims=True)
        sums_ref[1:2, :] += jnp.sum(dh * (xn * g), axis=0, keepdims=True)
        sums_ref[2:3, :] += jnp.sum(dh * sc1 * xn, axis=0, keepdims=True)
        dxn = dh * sc1 * g
        gx_ref[...] = dx2_ref[...] + r * (dxn - xn * jnp.mean(dxn * xn, axis=-1, keepdims=True))

        @pl.when(i == ni - 1)
        def _():
            wait()

    row = pl.BlockSpec((tm, D), lambda i: (i, 0))
    any_spec = pl.BlockSpec(memory_space=pl.ANY)
    return pl.pallas_call(
        body, name="dh_scatter",
        grid=(ni,),
        in_specs=[row] * N_DEV
                 + [any_spec, row, row,
                    pl.BlockSpec((8, D), lambda i: (0, 0)),
                    pl.BlockSpec((1, D), lambda i: (0, 0))]
                 + [any_spec] * ng,
        out_specs=[row, pl.BlockSpec((8, D), lambda i: (0, 0))] + [any_spec] * ng,
        out_shape=[jax.ShapeDtypeStruct((S, D), F32), jax.ShapeDtypeStruct((8, D), F32)]
                  + [jax.ShapeDtypeStruct(g.shape, g.dtype) for g in grads],
        scratch_shapes=[pltpu.VMEM(w_in_g.shape, w_in_g.dtype),
                        pltpu.SemaphoreType.DMA((ng, N_DEV - 1)), pltpu.SemaphoreType.DMA((ng, N_DEV - 1)),
                        pltpu.SemaphoreType.DMA((ng,))],
        compiler_params=_params(("arbitrary",)),
    )(*pieces, w_in_g, x, dx2, mod3, norm_g, *grads)


def _gw_in(ht, pieces, tm=512):
    S = ht.shape[1]
    nt = S // tm

    def body(*refs):
        h_ref, p_refs, o_ref, acc = refs[0], refs[1:1 + N_DEV], refs[1 + N_DEV], refs[2 + N_DEV]
        j, i = pl.program_id(0), pl.program_id(1)

        @pl.when(i == 0)
        def _():
            acc[...] = jnp.zeros_like(acc)

        for k in range(N_DEV):
            @pl.when(j == k)
            def _(k=k):
                acc[...] += _mm(h_ref[...], p_refs[k][...])

        @pl.when(i == nt - 1)
        def _():
            o_ref[0] = acc[...].astype(XCHG_DTYPE)

    def piece(k):
        return pl.BlockSpec((tm, D), lambda j, i: (jnp.where(j == k, i, 0), 0))

    return pl.pallas_call(
        body, name="gw_in",
        grid=(N_DEV, nt),
        in_specs=[pl.BlockSpec((D, tm), lambda j, i: (0, i))] + [piece(k) for k in range(N_DEV)],
        out_specs=pl.BlockSpec((1, D, D), lambda j, i: (j, 0, 0)),
        out_shape=jax.ShapeDtypeStruct((N_DEV, D, D), XCHG_DTYPE),
        scratch_shapes=[pltpu.VMEM((D, D), F32)],
        compiler_params=_params(("parallel", "arbitrary")),
    )(ht, *pieces)


def _adamw_math(w, g, m, v):
    m = ADAM_B1 * m + (1.0 - ADAM_B1) * g
    v = ADAM_B2 * v + (1.0 - ADAM_B2) * (g * g)
    m_hat = m / (1.0 - ADAM_B1 ** ADAM_STEP)
    v_hat = v / (1.0 - ADAM_B2 ** ADAM_STEP)
    delta = -ADAM_LR * (m_hat / (jnp.sqrt(v_hat) + ADAM_EPS) + ADAM_WD * w)
    return delta, m, v


def _adamw_big(recv, w, m, v, name, tr=128):
    M, N = w.shape
    tr = min(tr, M)

    def body(r_ref, w_ref, m_ref, v_ref, g_ref, d_ref, nm_ref, nv_ref):
        g = r_ref[0].astype(F32)
        for j in range(1, N_DEV):
            g = g + r_ref[j].astype(F32)
        g_ref[...] = g
        d_ref[...], nm_ref[...], nv_ref[...] = _adamw_math(w_ref[...], g, m_ref[...], v_ref[...])

    blk = pl.BlockSpec((tr, N), lambda i: (i, 0))
    return pl.pallas_call(
        body, name=name,
        grid=(M // tr,),
        in_specs=[pl.BlockSpec((N_DEV, tr, N), lambda i: (0, i, 0)), blk, blk, blk],
        out_specs=[blk] * 4,
        out_shape=[jax.ShapeDtypeStruct((M, N), F32)] * 4,
        compiler_params=_params(("parallel",)),
    )(recv, w, m, v)


def _adamw_w_ada(c64, dmod64, w, m, v):
    def body(c_ref, dm_ref, w_ref, m_ref, v_ref, g_ref, d_ref, nm_ref, nv_ref):
        cv = c_ref[...]
        g = _mm(cv * _sigmoid(cv), dm_ref[...], TN)
        g_ref[...] = g
        d_ref[...], nm_ref[...], nv_ref[...] = _adamw_math(w_ref[...], g, m_ref[...], v_ref[...])

    return pl.pallas_call(
        body, name="adamw_w_ada",
        out_shape=[jax.ShapeDtypeStruct(w.shape, F32)] * 4,
        compiler_params=_params(),
    )(c64, dmod64, w, m, v)


P_MOD, P_NORM, P_ONORM, P_RELB, P_LB, P_FINAL, P_LOSS, P_END = (0, 3 * D, 4 * D, 5 * D, 6 * D, 7 * D, 8 * D, 9 * D)


def _adamw_small(packed, b_ada, norm_g, onorm_g, relb, hgrn_lb, final_g, ms, vs):
    def body(pk_ref, b_ref, ng_ref, og_ref, rb_ref, lb_ref, fg_ref,
             mb, mn, mo, mr, ml, mf, vb, vn, vo, vr, vl, vf,
             loss_ref, gb, gn, go, gr, gl, gf, db, dn, do, dr, dl, df,
             nmb, nmn, nmo, nmr, nml, nmf, nvb, nvn, nvo, nvr, nvl, nvf):
        tot = pk_ref[0:1, :]
        for j in range(1, N_DEV):
            tot = tot + pk_ref[8 * j:8 * j + 1, :]
        loss_ref[...] = jnp.broadcast_to(jnp.sum(tot[:, P_LOSS:P_END], axis=-1, keepdims=True), (8, 128))

        def upd(g, w_ref, m_ref, v_ref, g_out, d_out, m_out, v_out):
            g_out[...] = g
            d_out[...], m_out[...], v_out[...] = _adamw_math(w_ref[...], g, m_ref[...], v_ref[...])

        upd(tot[:, P_MOD:P_NORM], b_ref, mb, vb, gb, db, nmb, nvb)
        upd(tot[:, P_NORM:P_ONORM], ng_ref, mn, vn, gn, dn, nmn, nvn)
        g_on = tot[:, P_ONORM:P_ONORM + G_DK]
        for h in range(1, G_HEADS):
            g_on = g_on + tot[:, P_ONORM + G_DK * h:P_ONORM + G_DK * (h + 1)]
        upd(g_on, og_ref, mo, vo, go, do, nmo, nvo)
        upd(tot[:, P_RELB:P_LB], rb_ref, mr, vr, gr, dr, nmr, nvr)
        a = lb_ref[...]
        lb = _sigmoid(a[0:1, :] - a[1:2, :])
        g0 = tot[:, P_LB:P_FINAL] * lb * (1.0 - lb)
        row = lax.broadcasted_iota(jnp.int32, (2, D), 0)
        upd(jnp.where(row == 0, g0, -g0), lb_ref, ml, vl, gl, dl, nml, nvl)
        upd(tot[:, P_FINAL:P_LOSS], fg_ref, mf, vf, gf, df, nmf, nvf)

    shapes = [b_ada.shape, norm_g.shape, onorm_g.shape, relb.shape, hgrn_lb.shape, final_g.shape]
    outs = [jax.ShapeDtypeStruct((8, 128), F32)] + [jax.ShapeDtypeStruct(s, F32) for s in shapes] * 4
    return pl.pallas_call(
        body, name="adamw_small",
        out_shape=outs,
        compiler_params=_params(),
    )(packed, b_ada, norm_g, onorm_g, relb, hgrn_lb, final_g, *ms, *vs)


def _local_step(x, target, mod3, norm_g, w_in_g, onorm_g, wa_blk, wb_blk, wo_blk, rel_bias, hgrn_lb, final_g):
    buckets = jnp.asarray(_bucket_tables())
    bias = _bias_tables(rel_bias, buckets)
    proj, ht, wa_g, wb_g, wo_g = _inproj(x, mod3, norm_g, w_in_g, [wa_blk, wb_blk, wo_blk])
    wa = wa_g.transpose(1, 0, 2).reshape(A_W, D)
    wb = wb_g.reshape(D, D)
    wo = wo_g.reshape(D, D)
    os, ls = [], []
    for p, (_, d) in enumerate(PATTERNS):
        o, l = _attn_fwd(proj, bias[p], d, "attn_fwd_d%d" % d)
        os.append(o)
        ls.append(l)
    ao, lt, oa = _attn_combine(os, ls, proj)
    o_raw, ob, states = _hgrn_fwd(proj, hgrn_lb, onorm_g)
    dx2, doa, dob, dga, dgb, tsums, gwa, gwb, gwo = _tail(x, target, oa, ob, proj, mod3, final_g, wa, wb, wo)
    do, dza, delta = _attn_pre_bwd(doa, ao, proj)
    dqs, dks, dvs, dbs = [], [], [], []
    for p, (_, d) in enumerate(PATTERNS):
        dq, dk, dv, db = _attn_bwd(proj, do, lt, delta, bias[p], d, "attn_bwd_d%d" % d)
        dqs.append(dq)
        dks.append(dk)
        dvs.append(dv)
        dbs.append(db)
    p0, p1 = _attn_assemble(dqs, dks, dvs, dza)
    g_relb = _rel_bias_grad(dbs, buckets)
    dqb, dfb, dib, dzb, dlb, dgo = _hgrn_bwd(proj, o_raw, dob, states, hgrn_lb, onorm_g)
    pieces = [p0, p1, dqb, dfb, dib, dzb, dga, dgb]
    grads = [_gw_in(ht, pieces),
             gwa.astype(XCHG_DTYPE).reshape(A_W, N_DEV, D // N_DEV).transpose(1, 0, 2),
             gwb.astype(XCHG_DTYPE).reshape(N_DEV, D // N_DEV, D),
             gwo.astype(XCHG_DTYPE).reshape(N_DEV, D // N_DEV, D)]
    gx, hsums, *received = _dh(pieces, w_in_g, x, dx2, mod3, norm_g, grads)
    row = jnp.concatenate([
        hsums[0], hsums[1], tsums[1],
        hsums[2],
        dgo.reshape(G_HEADS, 8, G_DK)[:, 0].reshape(-1),
        g_relb.reshape(-1),
        dlb[0],
        tsums[0],
        tsums[2],
    ])
    return gx, received, row


def kernel(x, c, w_ada, b_ada, norm_g, w_in, hgrn_onorm_g, w_branch_a, w_branch_b, w_out, rel_bias, hgrn_lb, final_g, loss_target, m_w_ada, m_b_ada, m_norm_g, m_w_in, m_hgrn_onorm_g, m_w_branch_a, m_w_branch_b, m_w_out, m_rel_bias, m_hgrn_lb, m_final_g, v_w_ada, v_b_ada, v_norm_g, v_w_in, v_hgrn_onorm_g, v_w_branch_a, v_w_branch_b, v_w_out, v_rel_bias, v_hgrn_lb, v_final_g):
    me = 4 * lax.axis_index("x") + 2 * lax.axis_index("y") + lax.axis_index("c")
    n_ada = w_ada.shape[2]

    (w_in_g,) = _all_gather([w_in[0].astype(MXU_DTYPE)], "gather_w_in")

    (c_all,) = _all_gather([jnp.broadcast_to(c, (8, D))], "gather_c")
    c64 = c_all.reshape(8 * N_DEV, D)
    b_loc = lax.dynamic_slice(b_ada, (0, me * n_ada), (1, n_ada))
    mod_part = _mod_fwd(c64, w_ada[0], b_loc)[::8]
    (mod_all,) = _all_gather([mod_part], "gather_mod")
    mod = lax.dynamic_slice(mod_all, (0, me, 0), (N_DEV, 1, n_ada)).reshape(3, D)
    mod3 = jnp.concatenate([mod, jnp.zeros((5, D), F32)], axis=0)

    onorm_t = hgrn_onorm_g
    gx, (r_in, r_a, r_b, r_o), row = _local_step(
        x[0], loss_target[0], mod3, norm_g, w_in_g, onorm_t, w_branch_a[0].astype(MXU_DTYPE),
        w_branch_b[0].astype(MXU_DTYPE), w_out[0].astype(MXU_DTYPE), rel_bias, hgrn_lb,
        final_g.reshape(1, D))
    packed8 = jnp.concatenate([row[None, :], jnp.zeros((7, P_END), F32)], axis=0)
    (packed,) = _all_gather([packed8], "gather_small")
    packed = packed.reshape(8 * N_DEV, P_END)

    g_in, d_in, nm_in, nv_in = _adamw_big(r_in, w_in[0], m_w_in[0], v_w_in[0], "adamw_w_in")
    g_a, d_a, nm_a, nv_a = _adamw_big(r_a, w_branch_a[0], m_w_branch_a[0], v_w_branch_a[0], "adamw_w_branch_a")
    g_b, d_b, nm_b, nv_b = _adamw_big(r_b, w_branch_b[0], m_w_branch_b[0], v_w_branch_b[0], "adamw_w_branch_b")
    g_o, d_o, nm_o, nv_o = _adamw_big(r_o, w_out[0], m_w_out[0], v_w_out[0], "adamw_w_out")

    dmod64 = lax.dynamic_slice(packed, (0, P_MOD + me * n_ada), (8 * N_DEV, n_ada))
    g_ada, d_ada, nm_ada, nv_ada = _adamw_w_ada(c64, dmod64, w_ada[0], m_w_ada[0], v_w_ada[0])

    def flat_relb(t):
        return jnp.pad(t.T, ((0, 0), (0, 128 - N_BUCKETS))).reshape(1, A_HEADS * 128)

    def unflat_relb(t):
        return t.reshape(A_HEADS, 128)[:, :N_BUCKETS].T

    fg2 = lambda t: t.reshape(1, D)
    smalls = _adamw_small(
        packed, b_ada, norm_g, hgrn_onorm_g, flat_relb(rel_bias), hgrn_lb, fg2(final_g),
        [m_b_ada, m_norm_g, m_hgrn_onorm_g, flat_relb(m_rel_bias), m_hgrn_lb, fg2(m_final_g)],
        [v_b_ada, v_norm_g, v_hgrn_onorm_g, flat_relb(v_rel_bias), v_hgrn_lb, fg2(v_final_g)])
    loss = smalls[0][0, 0]

    def small(kind):
        s = smalls[1 + 6 * kind:7 + 6 * kind]
        return s[0], s[1], s[2], unflat_relb(s[3]), s[4], s[5].reshape(D)

    def leaves(ada, sm, w_in_, wa_, wb_, wo_):
        b_, n_, o_, r_, l_, f_ = sm
        return (ada[None], b_, n_, w_in_[None], o_, wa_[None], wb_[None], wo_[None], r_, l_, f_)

    return (loss, gx[None],
            *leaves(g_ada, small(0), g_in, g_a, g_b, g_o),
            *leaves(d_ada, small(1), d_in, d_a, d_b, d_o),
            *leaves(nm_ada, small(2), nm_in, nm_a, nm_b, nm_o),
            *leaves(nv_ada, small(3), nv_in, nv_a, nv_b, nv_o))
```

```python
import functools
import math

import numpy as np
import jax
import jax.numpy as jnp
from jax import lax
from jax.experimental import pallas as pl
from jax.experimental.pallas import tpu as pltpu

F32 = jnp.float32
BF16 = jnp.bfloat16
MXU_DTYPE = jnp.bfloat16
XCHG_DTYPE = jnp.bfloat16

N_DEV = 8
D = 1024
A_HEADS = 8
A_HD = 64
A_W = A_HEADS * A_HD
A_BLK = 128
PATTERNS = ((128, 1), (512, 4), (2048, 16))
N_BUCKETS = 32
MAX_DISTANCE = 2048
NEG = -1e30
G_HEADS = 8
G_DK = 128
G_W = G_HEADS * G_DK
IN_W = 8 * D
EPS = 1e-6
ADAM_LR = 0.001
ADAM_B1 = 0.9
ADAM_B2 = 0.999
ADAM_EPS = 1e-08
ADAM_WD = 0.01
ADAM_STEP = 10

G_CHUNK = 128
G_SUB = 8
G_HPS_FWD = 4
G_HPS_BWD = 4
G_RB = 16
VMEM_LIMIT = 56 * 1024 * 1024

NN = (((1,), (0,)), ((), ()))
NT = (((1,), (1,)), ((), ()))
TN = (((0,), (0,)), ((), ()))
MESH = pl.DeviceIdType.MESH


def _mm(a, b, dims=NN):
    return lax.dot_general(a.astype(MXU_DTYPE), b.astype(MXU_DTYPE), dims,
                           preferred_element_type=F32)


def _mm_exact(t, x):
    hi = x.astype(BF16)
    r = x - hi.astype(F32)
    mid = r.astype(BF16)
    lo = (r - mid.astype(F32)).astype(BF16)
    tb = t.astype(BF16)
    return sum(lax.dot_general(tb, p, NN, preferred_element_type=F32) for p in (hi, mid, lo))


def _split(x):
    hi = x.astype(BF16)
    return hi, (x - hi.astype(F32)).astype(BF16)


def _mm_split(a, b, dims):
    dot = lambda p, q: lax.dot_general(p, q, dims, preferred_element_type=F32)
    return dot(a[0], b[0]) + dot(a[0], b[1]) + dot(a[1], b[0])


def _sigmoid(x):
    return 0.5 * jnp.tanh(0.5 * x) + 0.5


def _params(sem=None):
    return pltpu.CompilerParams(dimension_semantics=sem, vmem_limit_bytes=VMEM_LIMIT)


def _all_gather(xs, name):
    n = len(xs)

    def body(*refs):
        ins, outs = refs[:n], refs[n:2 * n]
        send_sems, recv_sems, local_sems = refs[2 * n:]
        x, y, c = lax.axis_index("x"), lax.axis_index("y"), lax.axis_index("c")
        me, sibling = (x, y, c), (x, y, 1 - c)
        chips = [(1 - x, y), (x, 1 - y), (1 - x, 1 - y)]

        def slot(ref, dev):
            return ref.at[4 * dev[0] + 2 * dev[1] + dev[2]]

        def copy(a, k, block, to, src=None):
            return pltpu.make_async_remote_copy(
                src_ref=slot(outs[a], block) if src is None else src,
                dst_ref=slot(outs[a], block),
                send_sem=send_sems.at[a, k], recv_sem=recv_sems.at[a, k],
                device_id=to, device_id_type=MESH)

        mine, first, passed = [], [], []
        for a in range(n):
            cp = pltpu.make_async_copy(ins[a], slot(outs[a], me), local_sems.at[a])
            cp.start()
            mine.append(cp)
            first.append(copy(a, 0, me, sibling, src=ins[a]))
            for j, chip in enumerate(chips):
                first.append(copy(a, 1 + j, me, (*chip, c), src=ins[a]))
        for cp in first:
            cp.start()
        for j, chip in enumerate(chips):
            for a in range(n):
                copy(a, 1 + j, (*chip, c), me).wait_recv()
                cp = copy(a, 4 + j, (*chip, c), sibling)
                cp.start()
                passed.append(cp)
        for a in range(n):
            copy(a, 0, sibling, me).wait_recv()
            for j, chip in enumerate(chips):
                copy(a, 4 + j, (*chip, 1 - c), me).wait_recv()
        for cp in first + passed:
            cp.wait_send()
        for cp in mine:
            cp.wait()

    any_spec = pl.BlockSpec(memory_space=pl.ANY)
    return pl.pallas_call(
        body, name=name,
        out_shape=[jax.ShapeDtypeStruct((N_DEV,) + v.shape, v.dtype) for v in xs],
        in_specs=[any_spec] * n, out_specs=[any_spec] * n,
        scratch_shapes=[pltpu.SemaphoreType.DMA((n, 7)), pltpu.SemaphoreType.DMA((n, 7)),
                        pltpu.SemaphoreType.DMA((n,))],
    )(*xs)


def _all_to_all_copies(ins, outs, send_sems, recv_sems, local_sems, gather=False):
    n = len(ins)
    x, y, c = lax.axis_index("x"), lax.axis_index("y"), lax.axis_index("c")
    me = 4 * x + 2 * y + c
    peers = []
    for m in range(1, N_DEV):
        peers.append((1 - x if m & 4 else x, 1 - y if m & 2 else y, 1 - c if m & 1 else c))

    def chunk(a, j):
        return ins[a] if gather else ins[a].at[j]

    def copy(a, k, landing):
        peer = peers[k]
        pid = 4 * peer[0] + 2 * peer[1] + peer[2]
        return pltpu.make_async_remote_copy(
            src_ref=chunk(a, pid), dst_ref=outs[a].at[pid if landing else me],
            send_sem=send_sems.at[a, k], recv_sem=recv_sems.at[a, k],
            device_id=peer, device_id_type=MESH)

    def local(a):
        return pltpu.make_async_copy(chunk(a, me), outs[a].at[me], local_sems.at[a])

    def start():
        for a in range(n):
            local(a).start()
        for k in range(N_DEV - 1):
            for a in range(n):
                copy(a, k, False).start()

    def wait():
        for k in range(N_DEV - 1):
            for a in range(n):
                copy(a, k, True).wait_recv()
        for k in range(N_DEV - 1):
            for a in range(n):
                copy(a, k, False).wait_send()
        for a in range(n):
            local(a).wait()

    return start, wait


def _mod_fwd(c64, w_ada, b_loc):
    def body(c_ref, w_ref, b_ref, o_ref):
        cv = c_ref[...]
        sc = cv * _sigmoid(cv)
        o_ref[...] = _mm(sc, w_ref[...]) + b_ref[...]

    return pl.pallas_call(
        body, name="mod_fwd",
        out_shape=jax.ShapeDtypeStruct((c64.shape[0], w_ada.shape[1]), F32),
        compiler_params=_params(),
    )(c64, w_ada, b_loc)


def _inproj(x, mod3, norm_g, w_in_g, blocks, tm=256):
    S = x.shape[0]
    ni = S // tm
    nb = len(blocks)

    def body(*refs):
        x_ref, mod_ref, g_ref, w_ref = refs[:4]
        b_ins = refs[4:4 + nb]
        proj_ref, ht_ref = refs[4 + nb:6 + nb]
        b_outs = refs[6 + nb:6 + 2 * nb]
        w_all, send_sems, recv_sems, local_sems = refs[6 + 2 * nb:]
        i = pl.program_id(0)
        start, wait = _all_to_all_copies(b_ins, b_outs, send_sems, recv_sems, local_sems, gather=True)

        @pl.when(i == 0)
        def _():
            start()
            pltpu.sync_copy(w_ref, w_all)

        xv = x_ref[...]
        r = lax.rsqrt(jnp.mean(xv * xv, axis=-1, keepdims=True) + EPS)
        h = ((xv * r * g_ref[...]) * (1.0 + mod_ref[1:2, :]) + mod_ref[0:1, :]).astype(MXU_DTYPE)
        ht_ref[...] = h.T
        for j in range(N_DEV):
            proj_ref[:, j * D:(j + 1) * D] = _mm(h, w_all[j])

        @pl.when(i == ni - 1)
        def _():
            wait()

    any_spec = pl.BlockSpec(memory_space=pl.ANY)
    return pl.pallas_call(
        body, name="inproj_gather",
        grid=(ni,),
        in_specs=[pl.BlockSpec((tm, D), lambda i: (i, 0)),
                  pl.BlockSpec((8, D), lambda i: (0, 0)),
                  pl.BlockSpec((1, D), lambda i: (0, 0)),
                  any_spec] + [any_spec] * nb,
        out_specs=[pl.BlockSpec((tm, IN_W), lambda i: (i, 0)),
                   pl.BlockSpec((D, tm), lambda i: (0, i))] + [any_spec] * nb,
        out_shape=[jax.ShapeDtypeStruct((S, IN_W), F32), jax.ShapeDtypeStruct((D, S), MXU_DTYPE)]
                  + [jax.ShapeDtypeStruct((N_DEV,) + b.shape, b.dtype) for b in blocks],
        scratch_shapes=[pltpu.VMEM(w_in_g.shape, w_in_g.dtype),
                        pltpu.SemaphoreType.DMA((nb, N_DEV - 1)), pltpu.SemaphoreType.DMA((nb, N_DEV - 1)),
                        pltpu.SemaphoreType.DMA((nb,))],
        compiler_params=_params(("arbitrary",)),
    )(x, mod3, norm_g, w_in_g, *blocks)


def _bucket_tables():
    qi = np.arange(A_BLK)[:, None]
    kj = np.arange(2 * A_BLK)[None, :]
    delta = qi + A_BLK - kj
    out = []
    for window, dil in PATTERNS:
        span = window // dil
        band = (delta >= 0) & (delta <= span)
        dist = np.clip(delta, 0, None) * dil
        max_exact = N_BUCKETS // 2
        nf = dist.astype(np.float32)
        large = max_exact + (np.log(np.maximum(nf, np.float32(1.0)) / np.float32(max_exact))
                             / np.float32(math.log(MAX_DISTANCE / max_exact))
                             * np.float32(N_BUCKETS - max_exact)).astype(np.int32)
        large = np.minimum(large, N_BUCKETS - 1)
        bucket = np.where(dist < max_exact, dist, large)
        out.append(np.where(band, bucket, -1).astype(np.int32))
    return np.stack(out)


def _bias_tables(rel_bias, buckets):
    def body(rb_ref, bk_ref, o_ref):
        h = pl.program_id(1)
        bk = bk_ref[0]
        acc = jnp.full(bk.shape, NEG, F32)
        for b in range(N_BUCKETS):
            acc = jnp.where(bk == b, rb_ref[b, h], acc)
        o_ref[0, 0] = acc

    return pl.pallas_call(
        body, name="bias_tables",
        grid=(3, A_HEADS),
        in_specs=[pl.BlockSpec(memory_space=pltpu.SMEM),
                  pl.BlockSpec((1, A_BLK, 2 * A_BLK), lambda p, h: (p, 0, 0))],
        out_specs=pl.BlockSpec((1, 1, A_BLK, 2 * A_BLK), lambda p, h: (p, h, 0, 0)),
        out_shape=jax.ShapeDtypeStruct((3, A_HEADS, A_BLK, 2 * A_BLK), F32),
        compiler_params=_params(("arbitrary", "arbitrary")),
    )(rel_bias, buckets)


A_TILES = 16


def _attn_heads_per_step(d):
    return A_HEADS if d == 1 else 2


def _attn_in_specs(sb, nsb, hw):
    w = A_HD * hw
    per = A_W // w

    def cur(col):
        return pl.BlockSpec((sb, w), lambda hp, n: (jnp.minimum(n, nsb - 1), per * col + hp))

    def prev(col):
        return pl.BlockSpec((sb, w), lambda hp, n: (jnp.maximum(jnp.minimum(n, nsb - 1) - 1, 0), per * col + hp))

    return [cur(0), prev(1), cur(1), prev(2), cur(2)]


def _rows(r, d):
    return pl.ds(r, A_BLK) if d == 1 else pl.ds(r, A_BLK, stride=d)


def _for_residues(d, hw, fn):
    unroll = min(d, max(1, A_TILES // hw))
    if d == unroll:
        for r in range(d):
            fn(r)
    else:
        def group(g, c):
            for u in range(unroll):
                fn(g * unroll + u)
            return c
        lax.fori_loop(0, d // unroll, group, 0)


def _attn_stack(t):
    first_half = lax.broadcasted_iota(jnp.int32, (1, 2 * A_HD), 1) < A_HD
    return jnp.concatenate([jnp.where(first_half, t, 0.0), jnp.where(first_half, 0.0, t)], axis=0)


def _attn_unstack(t2):
    first_half = lax.broadcasted_iota(jnp.int32, (1, 2 * A_HD), 1) < A_HD
    return jnp.where(first_half, t2[:A_BLK], t2[A_BLK:])


def _attn_scores(q, k, b_ref, pp, first):
    bias = jnp.concatenate([b_ref[2 * pp] + first, b_ref[2 * pp + 1] + first], axis=0)
    return _mm(_attn_stack(q), k, NT) * (A_HD ** -0.5) + bias


def _attn_fwd(proj, bias_p, d, name):
    S = proj.shape[0]
    sb = A_BLK * d
    nsb = S // sb
    hw = _attn_heads_per_step(d)

    def body(q_ref, kp_ref, kc_ref, vp_ref, vc_ref, b_ref, o_ref, l_ref):
        n = pl.program_id(1)
        kj = lax.broadcasted_iota(jnp.int32, (A_BLK, 2 * A_BLK), 1)
        first = jnp.where((n == 0) & (kj < A_BLK), NEG, 0.0).astype(F32)

        def residue(r):
            rows = _rows(r, d)
            for pp in range(hw // 2):
                lanes = pl.ds(2 * A_HD * pp, 2 * A_HD)
                k = jnp.concatenate([kp_ref[rows, lanes], kc_ref[rows, lanes]], axis=0)
                v = jnp.concatenate([vp_ref[rows, lanes], vc_ref[rows, lanes]], axis=0)
                s = _attn_scores(q_ref[rows, lanes], k, b_ref, pp, first)
                m = jnp.max(s, axis=-1, keepdims=True)
                p = jnp.exp(s - m)
                den = jnp.sum(p, axis=-1, keepdims=True)
                o_ref[rows, lanes] = _attn_unstack(_mm(p, v) / den)
                l_ref[rows, lanes] = _attn_unstack(jnp.broadcast_to(m + jnp.log(den), (2 * A_BLK, 2 * A_HD)))

        _for_residues(d, hw, residue)

    out = pl.BlockSpec((sb, A_HD * hw), lambda hp, n: (n, hp))
    return pl.pallas_call(
        body, name=name,
        grid=(A_HEADS // hw, nsb),
        in_specs=_attn_in_specs(sb, nsb, hw) + [pl.BlockSpec((hw, A_BLK, 2 * A_BLK), lambda hp, n: (hp, 0, 0))],
        out_specs=[out, out],
        out_shape=[jax.ShapeDtypeStruct((S, A_W), F32)] * 2,
        compiler_params=_params(("parallel", "parallel")),
    )(proj, proj, proj, proj, proj, bias_p)


def _attn_combine(os, ls, proj, tm=512):
    S = proj.shape[0]

    def body(o1, o2, o3, l1, l2, l3, z_ref, ao_ref, lt_ref, oa_ref):
        a1, a2, a3 = l1[...], l2[...], l3[...]
        m = jnp.maximum(jnp.maximum(a1, a2), a3)
        e1, e2, e3 = jnp.exp(a1 - m), jnp.exp(a2 - m), jnp.exp(a3 - m)
        den = e1 + e2 + e3
        ao = (e1 * o1[...] + e2 * o2[...] + e3 * o3[...]) / den
        z = z_ref[...]
        ao_ref[...] = ao
        lt_ref[...] = m + jnp.log(den)
        oa_ref[...] = (ao * (z * _sigmoid(z))).astype(MXU_DTYPE)

    spec = pl.BlockSpec((tm, A_W), lambda i: (i, 0))
    return pl.pallas_call(
        body, name="attn_combine",
        grid=(S // tm,),
        in_specs=[spec] * 6 + [pl.BlockSpec((tm, A_W), lambda i: (i, 3))],
        out_specs=[spec] * 3,
        out_shape=[jax.ShapeDtypeStruct((S, A_W), F32), jax.ShapeDtypeStruct((S, A_W), F32),
                   jax.ShapeDtypeStruct((S, A_W), MXU_DTYPE)],
        compiler_params=_params(("parallel",)),
    )(*os, *ls, proj)


def _attn_pre_bwd(doa, ao, proj, tm=512):
    S = proj.shape[0]

    def body(doa_ref, ao_ref, z_ref, do_ref, dz_ref, dl_ref):
        z = z_ref[...]
        sg = _sigmoid(z)
        g = doa_ref[...]
        ao_v = ao_ref[...]
        do = g * (z * sg)
        do_ref[...] = do
        dz_ref[...] = (g * ao_v * (sg * (1.0 + z * (1.0 - sg)))).astype(MXU_DTYPE)
        prod = do * ao_v
        for h in range(A_HEADS):
            sl = slice(A_HD * h, A_HD * (h + 1))
            dl_ref[:, sl] = jnp.broadcast_to(jnp.sum(prod[:, sl], axis=-1, keepdims=True), (tm, A_HD))

    spec = pl.BlockSpec((tm, A_W), lambda i: (i, 0))
    return pl.pallas_call(
        body, name="attn_pre_bwd",
        grid=(S // tm,),
        in_specs=[spec, spec, pl.BlockSpec((tm, A_W), lambda i: (i, 3))],
        out_specs=[spec] * 3,
        out_shape=[jax.ShapeDtypeStruct((S, A_W), F32), jax.ShapeDtypeStruct((S, A_W), MXU_DTYPE),
                   jax.ShapeDtypeStruct((S, A_W), F32)],
        compiler_params=_params(("parallel",)),
    )(doa, ao, proj)


def _attn_bwd(proj, do, lt, delta, bias_p, d, name):
    S = proj.shape[0]
    sb = A_BLK * d
    nsb = S // sb
    hw = _attn_heads_per_step(d)

    def body(q_ref, kp_ref, kc_ref, vp_ref, vc_ref, do_ref, lt_ref, dl_ref, b_ref,
             dq_ref, dk_ref, dv_ref, db_ref, ck, cv):
        n = pl.program_id(1)

        @pl.when(n == 0)
        def _():
            db_ref[...] = jnp.zeros_like(db_ref)
            ck[...] = jnp.zeros_like(ck)
            cv[...] = jnp.zeros_like(cv)

        @pl.when(n < nsb)
        def _():
            kj = lax.broadcasted_iota(jnp.int32, (A_BLK, 2 * A_BLK), 1)
            first = jnp.where((n == 0) & (kj < A_BLK), NEG, 0.0).astype(F32)

            def residue(r):
                rows = _rows(r, d)
                for pp in range(hw // 2):
                    lanes = pl.ds(2 * A_HD * pp, 2 * A_HD)
                    lt_r, dl_r = lt_ref[rows, lanes], dl_ref[rows, lanes]
                    k = jnp.concatenate([kp_ref[rows, lanes], kc_ref[rows, lanes]], axis=0)
                    v = jnp.concatenate([vp_ref[rows, lanes], vc_ref[rows, lanes]], axis=0)
                    q2 = _attn_stack(q_ref[rows, lanes])
                    do2 = _attn_stack(do_ref[rows, lanes])
                    col = lambda t: jnp.concatenate([t[:, 0:1], t[:, A_HD:A_HD + 1]], axis=0)
                    s = _attn_scores(q_ref[rows, lanes], k, b_ref, pp, first)
                    p = jnp.exp(s - col(lt_r))
                    ds = p * (_mm(do2, v, NT) - col(dl_r))
                    db_ref[2 * pp] += ds[:A_BLK]
                    db_ref[2 * pp + 1] += ds[A_BLK:]
                    dq_ref[rows, lanes] = _attn_unstack(_mm(ds, k)) * (A_HD ** -0.5)
                    dk = _mm(ds, q2, TN) * (A_HD ** -0.5)
                    dv = _mm(p, do2, TN)
                    dk_ref[rows, lanes] = ck[rows, lanes] + dk[:A_BLK]
                    dv_ref[rows, lanes] = cv[rows, lanes] + dv[:A_BLK]
                    ck[rows, lanes] = dk[A_BLK:]
                    cv[rows, lanes] = dv[A_BLK:]

            _for_residues(d, hw, residue)

        @pl.when(n == nsb)
        def _():
            dk_ref[...] = ck[...]
            dv_ref[...] = cv[...]

    w = A_HD * hw
    row = pl.BlockSpec((sb, w), lambda hp, n: (jnp.minimum(n, nsb - 1), hp))
    lag = pl.BlockSpec((sb, w), lambda hp, n: (jnp.maximum(n - 1, 0), hp))
    tab = pl.BlockSpec((hw, A_BLK, 2 * A_BLK), lambda hp, n: (hp, 0, 0))
    return pl.pallas_call(
        body, name=name,
        grid=(A_HEADS // hw, nsb + 1),
        in_specs=_attn_in_specs(sb, nsb, hw) + [row, row, row, tab],
        out_specs=[row, lag, lag, tab],
        out_shape=[jax.ShapeDtypeStruct((S, A_W), F32)] * 3
                  + [jax.ShapeDtypeStruct((A_HEADS, A_BLK, 2 * A_BLK), F32)],
        scratch_shapes=[pltpu.VMEM((sb, w), F32), pltpu.VMEM((sb, w), F32)],
        compiler_params=_params(("parallel", "arbitrary")),
    )(proj, proj, proj, proj, proj, do, lt, delta, bias_p)


def _attn_assemble(dqs, dks, dvs, dz, tm=512):
    S = dz.shape[0]

    def body(q1, q2, q3, k1, k2, k3, v1, v2, v3, z_ref, p0_ref, p1_ref):
        p0_ref[:, :A_W] = (q1[...] + q2[...] + q3[...]).astype(MXU_DTYPE)
        p0_ref[:, A_W:] = (k1[...] + k2[...] + k3[...]).astype(MXU_DTYPE)
        p1_ref[:, :A_W] = (v1[...] + v2[...] + v3[...]).astype(MXU_DTYPE)
        p1_ref[:, A_W:] = z_ref[...]

    spec = pl.BlockSpec((tm, A_W), lambda i: (i, 0))
    wide = pl.BlockSpec((tm, 2 * A_W), lambda i: (i, 0))
    return pl.pallas_call(
        body, name="attn_assemble",
        grid=(S // tm,),
        in_specs=[spec] * 10,
        out_specs=[wide, wide],
        out_shape=[jax.ShapeDtypeStruct((S, 2 * A_W), MXU_DTYPE)] * 2,
        compiler_params=_params(("parallel",)),
    )(*dqs, *dks, *dvs, dz)


def _rel_bias_grad(dbs, buckets):
    def body(d1, d2, d3, bk_ref, o_ref):
        row = lax.broadcasted_iota(jnp.int32, (A_HEADS, 128), 0)
        lane = lax.broadcasted_iota(jnp.int32, (A_HEADS, 128), 1)
        acc = jnp.zeros((A_HEADS, 128), F32)
        for p, dref in enumerate((d1, d2, d3)):
            bk = bk_ref[p]
            for h in range(A_HEADS):
                ds = dref[h]
                for b in range(N_BUCKETS):
                    s = jnp.sum(jnp.where(bk == b, ds, 0.0), keepdims=True)
                    acc = acc + jnp.where((row == h) & (lane == b), s, 0.0)
        o_ref[...] = acc

    return pl.pallas_call(
        body, name="rel_bias_grad",
        out_shape=jax.ShapeDtypeStruct((A_HEADS, 128), F32),
        compiler_params=_params(),
    )(*dbs, buckets)


def _tri(c):
    t = np.tril(np.ones((c, c), np.float32))
    return jnp.asarray(t), jnp.asarray(t.T.copy())


def _fill_above(ref, x, pad):
    ref[0:G_SUB, :] = jnp.full((G_SUB, x.shape[1]), pad, F32)
    ref[G_SUB:, :] = x


def _fill_below(ref, x, pad):
    ref[0:x.shape[0], :] = x
    ref[x.shape[0]:, :] = jnp.full((G_SUB, x.shape[1]), pad, F32)


def _hgrn_gates(q_ref, f_ref, lbp_ref, tri_ref):
    qraw = q_ref[...]
    sq = _sigmoid(qraw)
    q = qraw * sq
    sg = _sigmoid(f_ref[...])
    lb = _sigmoid(lbp_ref[0:1, :] - lbp_ref[1:2, :])
    f = lb + (1.0 - lb) * sg
    k = 1.0 - f
    b = _mm_exact(tri_ref[...], jnp.log(f))
    return qraw, sq, q, sg, lb, f, k, b


def _hgrn_col(C, base, idx, hps):
    return pl.BlockSpec((C, hps * G_DK), lambda h, n: (idx(n), base * (G_HEADS // hps) + h))


def _hgrn_levels(C):
    out, m = [], G_SUB
    while 2 * m <= C:
        out.append(m)
        m *= 2
    return out


def _hgrn_level_masks(C):
    ti = np.arange(C)[:, None]
    si = np.arange(C)[None, :]
    return jnp.asarray(np.stack([((ti // (2 * m) == si // (2 * m)) & (ti - si >= G_SUB)).astype(np.float32)
                                 for m in _hgrn_levels(C)]))


def _hgrn_level(b, q, k, C, m):
    zeros = jnp.zeros((m, G_DK), F32)
    eq, ek, qt, kt = [], [], [], []
    for blk in range(0, C // m, 2):
        lo, mid, hi = blk * m, (blk + 1) * m, (blk + 2) * m
        ref = b[mid:mid + 1]
        e_right = jnp.exp(b[mid:hi] - ref)
        e_left = jnp.exp(ref - b[lo:mid])
        eq += [zeros, e_right]
        ek += [e_left, zeros]
        qt += [zeros, q[mid:hi] * e_right]
        kt += [k[lo:mid] * e_left, zeros]
    cat = lambda parts: jnp.concatenate(parts, axis=0)
    return cat(qt), cat(kt), cat(eq), cat(ek)


def _hgrn_fwd(proj, hgrn_lb, onorm_g, C=G_CHUNK):
    S = proj.shape[0]
    nc = S // C
    tri, _ = _tri(C)
    masks = _hgrn_level_masks(C)
    hps = G_HPS_FWD

    def body(q_ref, f_ref, i_ref, z_ref, lbp_ref, go_ref, tri_ref, pm_ref, o_ref, ob_ref, st_ref, St, kp, vp, fp):
        @pl.when(pl.program_id(1) == 0)
        def _():
            St[...] = jnp.zeros_like(St)

        for hh in range(hps):
            ln = pl.ds(G_DK * hh, G_DK)
            head(q_ref.at[:, ln], f_ref.at[:, ln], i_ref.at[:, ln], z_ref.at[:, ln], lbp_ref.at[:, ln], go_ref,
                 tri_ref, pm_ref, o_ref.at[:, ln], ob_ref.at[:, ln], st_ref.at[0, hh], St.at[hh], kp.at[hh], vp.at[hh],
                 fp.at[hh])

    def head(q_ref, f_ref, i_ref, z_ref, lbp_ref, go_ref, tri_ref, pm_ref, o_ref, ob_ref, st_ref, St, kp, vp, fp):
        _, _, q, _, _, f, k, b = _hgrn_gates(q_ref, f_ref, lbp_ref, tri_ref)
        v = i_ref[...]
        bC = b[C - 1:C, :]
        S0 = St[...]
        o = _mm(q * jnp.exp(b), S0, NT)
        _fill_above(kp, k, 0.0)
        _fill_above(vp, v, 0.0)
        _fill_above(fp, f, 1.0)
        near = []
        for r0 in range(0, C, G_RB):
            qb = q[r0:r0 + G_RB]
            acc = e = None
            for l in range(G_SUB):
                rows = pl.ds(G_SUB - l + r0, G_RB)
                if l > 0:
                    fl = fp[pl.ds(G_SUB - l + 1 + r0, G_RB), :]
                    e = fl if e is None else e * fl
                kl = kp[rows, :]
                a = jnp.sum(qb * kl if e is None else qb * kl * e, axis=-1, keepdims=True)
                t = a * vp[rows, :]
                acc = t if acc is None else acc + t
            near.append(acc)
        o = o + jnp.concatenate(near, axis=0)
        a_off = jnp.zeros((C, C), F32)
        for lv, m in enumerate(_hgrn_levels(C)):
            qt, kt, _, _ = _hgrn_level(b, q, k, C, m)
            prod = _mm_split(_split(qt), _split(kt), NT) if m == G_SUB else _mm(qt, kt, NT)
            a_off = a_off + pm_ref[lv] * prod
        o = o + _mm(a_off, v)
        S1 = S0 * jnp.exp(bC) + _mm(v, k * jnp.exp(bC - b), TN)
        St[...] = S1
        st_ref[...] = S1
        o_ref[...] = o
        r = lax.rsqrt(jnp.mean(o * o, axis=-1, keepdims=True) + EPS)
        z = z_ref[...]
        ob_ref[...] = (o * r * go_ref[...] * (z * _sigmoid(z))).astype(MXU_DTYPE)

    ident = lambda n: n
    w = hps * G_DK
    out = pl.BlockSpec((C, w), lambda h, n: (n, h))
    return pl.pallas_call(
        body, name="hgrn_fwd",
        grid=(G_HEADS // hps, nc),
        in_specs=[_hgrn_col(C, base, ident, hps) for base in (2, 3, 4, 5)] + [
                  pl.BlockSpec((2, w), lambda h, n: (0, h)),
                  pl.BlockSpec((1, G_DK), lambda h, n: (0, 0)),
                  pl.BlockSpec((C, C), lambda h, n: (0, 0)),
                  pl.BlockSpec(masks.shape, lambda h, n: (0, 0, 0))],
        out_specs=[out, out, pl.BlockSpec((1, hps, G_DK, G_DK), lambda h, n: (n, h, 0, 0))],
        out_shape=[jax.ShapeDtypeStruct((S, G_W), F32), jax.ShapeDtypeStruct((S, G_W), MXU_DTYPE),
                   jax.ShapeDtypeStruct((nc, G_HEADS, G_DK, G_DK), F32)],
        scratch_shapes=[pltpu.VMEM((hps, G_DK, G_DK), F32)] + [pltpu.VMEM((hps, C + G_SUB, G_DK), F32)] * 3,
        compiler_params=_params(("parallel", "arbitrary")),
    )(proj, proj, proj, proj, hgrn_lb, onorm_g, tri, masks)


def _hgrn_bwd(proj, o_raw, dob, states, hgrn_lb, onorm_g, C=G_CHUNK):
    S = proj.shape[0]
    nc = S // C
    tri, triu = _tri(C)
    masks = _hgrn_level_masks(C)
    hps = G_HPS_BWD

    def body(q_ref, f_ref, i_ref, z_ref, o_ref, dob_ref, s0_ref, s1_ref, lbp_ref, go_ref, tri_ref, triu_ref,
             pm_ref, dq_ref, df_ref, di_ref, dz_ref, dlb_ref, dgo_ref, dSt, *shifted):
        @pl.when(pl.program_id(1) == 0)
        def _():
            dSt[...] = jnp.zeros_like(dSt)
            dlb_ref[...] = jnp.zeros_like(dlb_ref)
            dgo_ref[...] = jnp.zeros_like(dgo_ref)

        for hh in range(hps):
            ln = pl.ds(G_DK * hh, G_DK)
            head(q_ref.at[:, ln], f_ref.at[:, ln], i_ref.at[:, ln], z_ref.at[:, ln], o_ref.at[:, ln],
                 dob_ref.at[:, ln], s0_ref.at[0, hh], s1_ref.at[0, hh], lbp_ref.at[:, ln], go_ref, tri_ref, triu_ref,
                 pm_ref, dq_ref.at[:, ln], df_ref.at[:, ln], di_ref.at[:, ln], dz_ref.at[:, ln], dlb_ref.at[:, ln],
                 dgo_ref.at[pl.ds(8 * hh, 8), :], dSt.at[hh], *[t.at[hh] for t in shifted])

    def head(q_ref, f_ref, i_ref, z_ref, o_ref, dob_ref, s0_ref, s1_ref, lbp_ref, go_ref, tri_ref, triu_ref,
             pm_ref, dq_ref, df_ref, di_ref, dz_ref, dlb_ref, dgo_ref, dSt, kp, vp, fp, qn, dn_, fn, xs, dac):
        cn = nc - 1 - pl.program_id(1)
        qraw, sq, q, sg, lb, f, k, b = _hgrn_gates(q_ref, f_ref, lbp_ref, tri_ref)
        v = i_ref[...]
        bC = b[C - 1:C, :]
        eb = jnp.exp(b)
        ecb = jnp.exp(bC - b)
        o = o_ref[...]
        z = z_ref[...]
        sz = _sigmoid(z)
        go = go_ref[...]
        g_ob = dob_ref[...]
        r = lax.rsqrt(jnp.mean(o * o, axis=-1, keepdims=True) + EPS)
        nh = o * r
        dnrm = g_ob * (z * sz)
        dz_ref[...] = (g_ob * (nh * go) * (sz * (1.0 + z * (1.0 - sz)))).astype(MXU_DTYPE)
        dgo_ref[0:1, :] += jnp.sum(dnrm * nh, axis=0, keepdims=True)
        dn = dnrm * go
        do = r * (dn - nh * jnp.mean(dn * nh, axis=-1, keepdims=True))

        S0 = jnp.where(cn == 0, 0.0, s0_ref[...])
        S1 = s1_ref[...]
        dS1 = dSt[...]
        dq = eb * _mm(do, S0)
        dk = ecb * _mm(v, dS1)
        dv = _mm(k * ecb, dS1, NT)
        bnd = jnp.sum(dS1 * S1, axis=0, keepdims=True)
        dSt[...] = dS1 * jnp.exp(bC) + _mm(do, q * eb, TN)

        _fill_above(kp, k, 0.0)
        _fill_above(vp, v, 0.0)
        _fill_above(fp, f, 1.0)
        _fill_below(qn, q, 0.0)
        _fill_below(dn_, do, 0.0)
        _fill_below(fn, f, 1.0)
        for r0 in range(0, C, G_RB):
            do_b = do[r0:r0 + G_RB]
            for l in range(G_SUB):
                xs[pl.ds(l * C + r0, G_RB), :] = (do_b * vp[pl.ds(G_SUB - l + r0, G_RB), :]).astype(MXU_DTYPE)
        dac[0:G_SUB * C, :] = _mm(xs[...], jnp.ones((G_DK, G_DK), MXU_DTYPE))
        dac[G_SUB * C:, :] = jnp.zeros((G_SUB, G_DK), F32)
        near_q, near_k, near_v = [], [], []
        for r0 in range(0, C, G_RB):
            k_b = k[r0:r0 + G_RB]
            aq = ak = av = e = e2 = None
            for l in range(G_SUB):
                down, up = pl.ds(G_SUB - l + r0, G_RB), pl.ds(l + r0, G_RB)
                if l > 0:
                    fl = fp[pl.ds(G_SUB - l + 1 + r0, G_RB), :]
                    e = fl if e is None else e * fl
                    fu = fn[up, :]
                    e2 = fu if e2 is None else e2 * fu
                kl = kp[down, :]
                t = dac[pl.ds(l * C + r0, G_RB), :] * (kl if e is None else kl * e)
                aq = t if aq is None else aq + t
                qu = qn[up, :]
                qe = qu if e2 is None else qu * e2
                dou = dn_[up, :]
                a2 = jnp.sum(qe * k_b, axis=-1, keepdims=True)
                t = dac[pl.ds(l * C + l + r0, G_RB), :] * qe
                ak = t if ak is None else ak + t
                t = a2 * dou
                av = t if av is None else av + t
            near_q.append(aq)
            near_k.append(ak)
            near_v.append(av)
        dq = dq + jnp.concatenate(near_q, axis=0)
        dk = dk + jnp.concatenate(near_k, axis=0)
        dv = dv + jnp.concatenate(near_v, axis=0)

        da_all = _mm(do, v, NT)
        a_off = jnp.zeros((C, C), F32)
        for lv, m in enumerate(_hgrn_levels(C)):
            qt, kt, eq, ek = _hgrn_level(b, q, k, C, m)
            da_m = pm_ref[lv] * da_all
            if m == G_SUB:
                qs, ks, das = _split(qt), _split(kt), _split(da_m)
                a_off = a_off + pm_ref[lv] * _mm_split(qs, ks, NT)
                dq = dq + _mm_split(das, ks, NN) * eq
                dk = dk + _mm_split(das, qs, TN) * ek
            else:
                a_off = a_off + pm_ref[lv] * _mm(qt, kt, NT)
                dq = dq + _mm(da_m, kt) * eq
                dk = dk + _mm(da_m, qt, TN) * ek
        dv = dv + _mm(a_off, do, TN)

        row = lax.broadcasted_iota(jnp.int32, (C, 1), 0)
        db = q * dq - k * dk + jnp.where(row == C - 1, bnd, 0.0)
        dg = _mm_exact(triu_ref[...], db)
        df = dg / f - dk
        df_ref[...] = (df * (1.0 - lb) * (sg * (1.0 - sg))).astype(MXU_DTYPE)
        dlb_ref[0:1, :] += jnp.sum(df * (1.0 - sg), axis=0, keepdims=True)
        dq_ref[...] = (dq * (sq * (1.0 + qraw * (1.0 - sq)))).astype(MXU_DTYPE)
        di_ref[...] = dv.astype(MXU_DTYPE)

    rev = lambda n: nc - 1 - n
    w = hps * G_DK
    blk = pl.BlockSpec((C, w), lambda h, n: (nc - 1 - n, h))
    return pl.pallas_call(
        body, name="hgrn_bwd",
        grid=(G_HEADS // hps, nc),
        in_specs=[_hgrn_col(C, base, rev, hps) for base in (2, 3, 4, 5)] + [
                  blk, blk,
                  pl.BlockSpec((1, hps, G_DK, G_DK), lambda h, n: (jnp.maximum(nc - 2 - n, 0), h, 0, 0)),
                  pl.BlockSpec((1, hps, G_DK, G_DK), lambda h, n: (nc - 1 - n, h, 0, 0)),
                  pl.BlockSpec((2, w), lambda h, n: (0, h)),
                  pl.BlockSpec((1, G_DK), lambda h, n: (0, 0)),
                  pl.BlockSpec((C, C), lambda h, n: (0, 0)),
                  pl.BlockSpec((C, C), lambda h, n: (0, 0)),
                  pl.BlockSpec(masks.shape, lambda h, n: (0, 0, 0))],
        out_specs=[blk, blk, blk, blk,
                   pl.BlockSpec((8, w), lambda h, n: (0, h)),
                   pl.BlockSpec((8 * hps, G_DK), lambda h, n: (h, 0))],
        out_shape=[jax.ShapeDtypeStruct((S, G_W), MXU_DTYPE)] * 4
                  + [jax.ShapeDtypeStruct((8, G_W), F32), jax.ShapeDtypeStruct((8 * G_HEADS, G_DK), F32)],
        scratch_shapes=[pltpu.VMEM((hps, G_DK, G_DK), F32)] + [pltpu.VMEM((hps, C + G_SUB, G_DK), F32)] * 6
                       + [pltpu.VMEM((hps, G_SUB * C, G_DK), MXU_DTYPE),
                          pltpu.VMEM((hps, G_SUB * C + G_SUB, G_DK), F32)],
        compiler_params=_params(("parallel", "arbitrary")),
    )(proj, proj, proj, proj, o_raw, dob, states, states, hgrn_lb, onorm_g, tri, triu, masks)


def _tail(x, target, oa, ob, proj, mod3, final_g, wa, wb, wo, tm=256):
    S = x.shape[0]
    nt = S // tm

    def body(x_ref, t_ref, oa_ref, ob_ref, ga_ref, gb_ref, mod_ref, fg_ref, wa_ref, wb_ref, wo_ref,
             dx2_ref, doa_ref, dob_ref, dga_ref, dgb_ref, sums_ref, gwa_ref, gwb_ref, gwo_ref,
             acc_a, acc_b, acc_o):
        i = pl.program_id(0)

        @pl.when(i == 0)
        def _():
            sums_ref[...] = jnp.zeros_like(sums_ref)
            acc_a[...] = jnp.zeros_like(acc_a)
            acc_b[...] = jnp.zeros_like(acc_b)
            acc_o[...] = jnp.zeros_like(acc_o)

        oa_v, ob_v = oa_ref[...], ob_ref[...]
        pa = _mm(oa_v, wa_ref[...])
        pb = _mm(ob_v, wb_ref[...])
        sa, sb = _sigmoid(ga_ref[...]), _sigmoid(gb_ref[...])
        ym = sa * pa + sb * pb
        u = _mm(ym, wo_ref[...])
        gate = mod_ref[2:3, :]
        fg = fg_ref[...]
        x2 = x_ref[...] + gate * u
        r2 = lax.rsqrt(jnp.mean(x2 * x2, axis=-1, keepdims=True) + EPS)
        xn2 = x2 * r2
        e = xn2 * fg - t_ref[...]
        dy = e * (1.0 / D)
        dn = dy * fg
        dx2 = r2 * (dn - xn2 * jnp.mean(dn * xn2, axis=-1, keepdims=True))
        dx2_ref[...] = dx2
        sums_ref[0:1, :] += jnp.sum(dy * xn2, axis=0, keepdims=True)
        sums_ref[1:2, :] += jnp.sum(dx2 * u, axis=0, keepdims=True)
        sums_ref[2:3, :] += (0.5 / D) * jnp.sum(e * e, axis=0, keepdims=True)
        du = dx2 * gate
        dym = _mm(du, wo_ref[...], NT)
        acc_o[...] += _mm(ym, du, TN)
        dpa, dpb = dym * sa, dym * sb
        dga_ref[...] = (dym * pa * (sa * (1.0 - sa))).astype(MXU_DTYPE)
        dgb_ref[...] = (dym * pb * (sb * (1.0 - sb))).astype(MXU_DTYPE)
        doa_ref[...] = _mm(dpa, wa_ref[...], NT)
        dob_ref[...] = _mm(dpb, wb_ref[...], NT)
        acc_a[...] += _mm(oa_v, dpa, TN)
        acc_b[...] += _mm(ob_v, dpb, TN)

        @pl.when(i == nt - 1)
        def _():
            pltpu.sync_copy(acc_a, gwa_ref)
            pltpu.sync_copy(acc_b, gwb_ref)
            pltpu.sync_copy(acc_o, gwo_ref)

    row = lambda w: pl.BlockSpec((tm, w), lambda i: (i, 0))
    full = lambda a, b: pl.BlockSpec((a, b), lambda i: (0, 0))
    any_spec = pl.BlockSpec(memory_space=pl.ANY)
    return pl.pallas_call(
        body, name="tail",
        grid=(nt,),
        in_specs=[row(D), row(D), row(A_W), row(D),
                  pl.BlockSpec((tm, D), lambda i: (i, 6)), pl.BlockSpec((tm, D), lambda i: (i, 7)),
                  full(8, D), full(1, D), full(A_W, D), full(D, D), full(D, D)],
        out_specs=[row(D), row(A_W), row(D), row(D), row(D), full(8, D), any_spec, any_spec, any_spec],
        out_shape=[jax.ShapeDtypeStruct((S, D), F32), jax.ShapeDtypeStruct((S, A_W), F32),
                   jax.ShapeDtypeStruct((S, D), F32), jax.ShapeDtypeStruct((S, D), MXU_DTYPE),
                   jax.ShapeDtypeStruct((S, D), MXU_DTYPE), jax.ShapeDtypeStruct((8, D), F32),
                   jax.ShapeDtypeStruct((A_W, D), F32), jax.ShapeDtypeStruct((D, D), F32),
                   jax.ShapeDtypeStruct((D, D), F32)],
        scratch_shapes=[pltpu.VMEM((A_W, D), F32), pltpu.VMEM((D, D), F32), pltpu.VMEM((D, D), F32)],
        compiler_params=_params(("arbitrary",)),
    )(x, target, oa, ob, proj, proj, mod3, final_g, wa, wb, wo)


def _dh(pieces, w_in_g, x, dx2, mod3, norm_g, grads, tm=256):
    S = x.shape[0]
    ni = S // tm
    ng = len(grads)

    def body(*refs):
        p_refs = refs[:N_DEV]
        w_ref, x_ref, dx2_ref, mod_ref, g_ref = refs[N_DEV:N_DEV + 5]
        g_ins = refs[N_DEV + 5:N_DEV + 5 + ng]
        gx_ref, sums_ref = refs[N_DEV + 5 + ng:N_DEV + 7 + ng]
        g_outs = refs[N_DEV + 7 + ng:N_DEV + 7 + 2 * ng]
        w_all, send_sems, recv_sems, local_sems = refs[N_DEV + 7 + 2 * ng:]
        i = pl.program_id(0)
        start, wait = _all_to_all_copies(g_ins, g_outs, send_sems, recv_sems, local_sems)

        @pl.when(i == 0)
        def _():
            start()
            sums_ref[...] = jnp.zeros_like(sums_ref)
            pltpu.sync_copy(w_ref, w_all)

        dh = _mm(p_refs[0][...], w_all[0], NT)
        for k in range(1, N_DEV):
            dh = dh + _mm(p_refs[k][...], w_all[k], NT)
        xv = x_ref[...]
        g = g_ref[...]
        sc1 = 1.0 + mod_ref[1:2, :]
        r = lax.rsqrt(jnp.mean(xv * xv, axis=-1, keepdims=True) + EPS)
        xn = xv * r
        sums_ref[0:1, :] += jnp.sum(dh, axis=0, keepdims=True)
        sums_ref[1:2, :] += jnp.sum(dh * (xn * g), axis=0, keepdims=True)
        sums_ref[2:3, :] += jnp.sum(dh * sc1 * xn, axis=0, keepdims=True)
        dxn = dh * sc1 * g
        gx_ref[...] = dx2_ref[...] + r * (dxn - xn * jnp.mean(dxn * xn, axis=-1, keepdims=True))

        @pl.when(i == ni - 1)
        def _():
            wait()

    row = pl.BlockSpec((tm, D), lambda i: (i, 0))
    any_spec = pl.BlockSpec(memory_space=pl.ANY)
    return pl.pallas_call(
        body, name="dh_scatter",
        grid=(ni,),
        in_specs=[row] * N_DEV
                 + [any_spec, row, row,
                    pl.BlockSpec((8, D), lambda i: (0, 0)),
                    pl.BlockSpec((1, D), lambda i: (0, 0))]
                 + [any_spec] * ng,
        out_specs=[row, pl.BlockSpec((8, D), lambda i: (0, 0))] + [any_spec] * ng,
        out_shape=[jax.ShapeDtypeStruct((S, D), F32), jax.ShapeDtypeStruct((8, D), F32)]
                  + [jax.ShapeDtypeStruct(g.shape, g.dtype) for g in grads],
        scratch_shapes=[pltpu.VMEM(w_in_g.shape, w_in_g.dtype),
                        pltpu.SemaphoreType.DMA((ng, N_DEV - 1)), pltpu.SemaphoreType.DMA((ng, N_DEV - 1)),
                        pltpu.SemaphoreType.DMA((ng,))],
        compiler_params=_params(("arbitrary",)),
    )(*pieces, w_in_g, x, dx2, mod3, norm_g, *grads)


def _gw_in(ht, pieces, tm=512):
    S = ht.shape[1]
    nt = S // tm

    def body(*refs):
        h_ref, p_refs, o_ref, acc = refs[0], refs[1:1 + N_DEV], refs[1 + N_DEV], refs[2 + N_DEV]
        j, i = pl.program_id(0), pl.program_id(1)

        @pl.when(i == 0)
        def _():
            acc[...] = jnp.zeros_like(acc)

        for k in range(N_DEV):
            @pl.when(j == k)
            def _(k=k):
                acc[...] += _mm(h_ref[...], p_refs[k][...])

        @pl.when(i == nt - 1)
        def _():
            o_ref[0] = acc[...].astype(XCHG_DTYPE)

    def piece(k):
        return pl.BlockSpec((tm, D), lambda j, i: (jnp.where(j == k, i, 0), 0))

    return pl.pallas_call(
        body, name="gw_in",
        grid=(N_DEV, nt),
        in_specs=[pl.BlockSpec((D, tm), lambda j, i: (0, i))] + [piece(k) for k in range(N_DEV)],
        out_specs=pl.BlockSpec((1, D, D), lambda j, i: (j, 0, 0)),
        out_shape=jax.ShapeDtypeStruct((N_DEV, D, D), XCHG_DTYPE),
        scratch_shapes=[pltpu.VMEM((D, D), F32)],
        compiler_params=_params(("parallel", "arbitrary")),
    )(ht, *pieces)


def _adamw_math(w, g, m, v):
    m = ADAM_B1 * m + (1.0 - ADAM_B1) * g
    v = ADAM_B2 * v + (1.0 - ADAM_B2) * (g * g)
    m_hat = m / (1.0 - ADAM_B1 ** ADAM_STEP)
    v_hat = v / (1.0 - ADAM_B2 ** ADAM_STEP)
    delta = -ADAM_LR * (m_hat / (jnp.sqrt(v_hat) + ADAM_EPS) + ADAM_WD * w)
    return delta, m, v


def _adamw_big(recv, w, m, v, name, tr=128):
    M, N = w.shape
    tr = min(tr, M)

    def body(r_ref, w_ref, m_ref, v_ref, g_ref, d_ref, nm_ref, nv_ref):
        g = r_ref[0].astype(F32)
        for j in range(1, N_DEV):
            g = g + r_ref[j].astype(F32)
        g_ref[...] = g
        d_ref[...], nm_ref[...], nv_ref[...] = _adamw_math(w_ref[...], g, m_ref[...], v_ref[...])

    blk = pl.BlockSpec((tr, N), lambda i: (i, 0))
    return pl.pallas_call(
        body, name=name,
        grid=(M // tr,),
        in_specs=[pl.BlockSpec((N_DEV, tr, N), lambda i: (0, i, 0)), blk, blk, blk],
        out_specs=[blk] * 4,
        out_shape=[jax.ShapeDtypeStruct((M, N), F32)] * 4,
        compiler_params=_params(("parallel",)),
    )(recv, w, m, v)


def _adamw_w_ada(c64, dmod64, w, m, v):
    def body(c_ref, dm_ref, w_ref, m_ref, v_ref, g_ref, d_ref, nm_ref, nv_ref):
        cv = c_ref[...]
        g = _mm(cv * _sigmoid(cv), dm_ref[...], TN)
        g_ref[...] = g
        d_ref[...], nm_ref[...], nv_ref[...] = _adamw_math(w_ref[...], g, m_ref[...], v_ref[...])

    return pl.pallas_call(
        body, name="adamw_w_ada",
        out_shape=[jax.ShapeDtypeStruct(w.shape, F32)] * 4,
        compiler_params=_params(),
    )(c64, dmod64, w, m, v)


P_MOD, P_NORM, P_ONORM, P_RELB, P_LB, P_FINAL, P_LOSS, P_END = (0, 3 * D, 4 * D, 5 * D, 6 * D, 7 * D, 8 * D, 9 * D)


def _adamw_small(packed, b_ada, norm_g, onorm_g, relb, hgrn_lb, final_g, ms, vs):
    def body(pk_ref, b_ref, ng_ref, og_ref, rb_ref, lb_ref, fg_ref,
             mb, mn, mo, mr, ml, mf, vb, vn, vo, vr, vl, vf,
             loss_ref, gb, gn, go, gr, gl, gf, db, dn, do, dr, dl, df,
             nmb, nmn, nmo, nmr, nml, nmf, nvb, nvn, nvo, nvr, nvl, nvf):
        tot = pk_ref[0:1, :]
        for j in range(1, N_DEV):
            tot = tot + pk_ref[8 * j:8 * j + 1, :]
        loss_ref[...] = jnp.broadcast_to(jnp.sum(tot[:, P_LOSS:P_END], axis=-1, keepdims=True), (8, 128))

        def upd(g, w_ref, m_ref, v_ref, g_out, d_out, m_out, v_out):
            g_out[...] = g
            d_out[...], m_out[...], v_out[...] = _adamw_math(w_ref[...], g, m_ref[...], v_ref[...])

        upd(tot[:, P_MOD:P_NORM], b_ref, mb, vb, gb, db, nmb, nvb)
        upd(tot[:, P_NORM:P_ONORM], ng_ref, mn, vn, gn, dn, nmn, nvn)
        g_on = tot[:, P_ONORM:P_ONORM + G_DK]
        for h in range(1, G_HEADS):
            g_on = g_on + tot[:, P_ONORM + G_DK * h:P_ONORM + G_DK * (h + 1)]
        upd(g_on, og_ref, mo, vo, go, do, nmo, nvo)
        upd(tot[:, P_RELB:P_LB], rb_ref, mr, vr, gr, dr, nmr, nvr)
        a = lb_ref[...]
        lb = _sigmoid(a[0:1, :] - a[1:2, :])
        g0 = tot[:, P_LB:P_FINAL] * lb * (1.0 - lb)
        row = lax.broadcasted_iota(jnp.int32, (2, D), 0)
        upd(jnp.where(row == 0, g0, -g0), lb_ref, ml, vl, gl, dl, nml, nvl)
        upd(tot[:, P_FINAL:P_LOSS], fg_ref, mf, vf, gf, df, nmf, nvf)

    shapes = [b_ada.shape, norm_g.shape, onorm_g.shape, relb.shape, hgrn_lb.shape, final_g.shape]
    outs = [jax.ShapeDtypeStruct((8, 128), F32)] + [jax.ShapeDtypeStruct(s, F32) for s in shapes] * 4
    return pl.pallas_call(
        body, name="adamw_small",
        out_shape=outs,
        compiler_params=_params(),
    )(packed, b_ada, norm_g, onorm_g, relb, hgrn_lb, final_g, *ms, *vs)


def _local_step(x, target, mod3, norm_g, w_in_g, onorm_g, wa_blk, wb_blk, wo_blk, rel_bias, hgrn_lb, final_g):
    buckets = jnp.asarray(_bucket_tables())
    bias = _bias_tables(rel_bias, buckets)
    proj, ht, wa_g, wb_g, wo_g = _inproj(x, mod3, norm_g, w_in_g, [wa_blk, wb_blk, wo_blk])
    wa = wa_g.transpose(1, 0, 2).reshape(A_W, D)
    wb = wb_g.reshape(D, D)
    wo = wo_g.reshape(D, D)
    os, ls = [], []
    for p, (_, d) in enumerate(PATTERNS):
        o, l = _attn_fwd(proj, bias[p], d, "attn_fwd_d%d" % d)
        os.append(o)
        ls.append(l)
    ao, lt, oa = _attn_combine(os, ls, proj)
    o_raw, ob, states = _hgrn_fwd(proj, hgrn_lb, onorm_g)
    dx2, doa, dob, dga, dgb, tsums, gwa, gwb, gwo = _tail(x, target, oa, ob, proj, mod3, final_g, wa, wb, wo)
    do, dza, delta = _attn_pre_bwd(doa, ao, proj)
    dqs, dks, dvs, dbs = [], [], [], []
    for p, (_, d) in enumerate(PATTERNS):
        dq, dk, dv, db = _attn_bwd(proj, do, lt, delta, bias[p], d, "attn_bwd_d%d" % d)
        dqs.append(dq)
        dks.append(dk)
        dvs.append(dv)
        dbs.append(db)
    p0, p1 = _attn_assemble(dqs, dks, dvs, dza)
    g_relb = _rel_bias_grad(dbs, buckets)
    dqb, dfb, dib, dzb, dlb, dgo = _hgrn_bwd(proj, o_raw, dob, states, hgrn_lb, onorm_g)
    pieces = [p0, p1, dqb, dfb, dib, dzb, dga, dgb]
    grads = [_gw_in(ht, pieces),
             gwa.astype(XCHG_DTYPE).reshape(A_W, N_DEV, D // N_DEV).transpose(1, 0, 2),
             gwb.astype(XCHG_DTYPE).reshape(N_DEV, D // N_DEV, D),
             gwo.astype(XCHG_DTYPE).reshape(N_DEV, D // N_DEV, D)]
    gx, hsums, *received = _dh(pieces, w_in_g, x, dx2, mod3, norm_g, grads)
    row = jnp.concatenate([
        hsums[0], hsums[1], tsums[1],
        hsums[2],
        dgo.reshape(G_HEADS, 8, G_DK)[:, 0].reshape(-1),
        g_relb.reshape(-1),
        dlb[0],
        tsums[0],
        tsums[2],
    ])
    return gx, received, row


def kernel(x, c, w_ada, b_ada, norm_g, w_in, hgrn_onorm_g, w_branch_a, w_branch_b, w_out, rel_bias, hgrn_lb, final_g, loss_target, m_w_ada, m_b_ada, m_norm_g, m_w_in, m_hgrn_onorm_g, m_w_branch_a, m_w_branch_b, m_w_out, m_rel_bias, m_hgrn_lb, m_final_g, v_w_ada, v_b_ada, v_norm_g, v_w_in, v_hgrn_onorm_g, v_w_branch_a, v_w_branch_b, v_w_out, v_rel_bias, v_hgrn_lb, v_final_g):
    me = 4 * lax.axis_index("x") + 2 * lax.axis_index("y") + lax.axis_index("c")
    n_ada = w_ada.shape[2]

    w_in_g, c_all = _all_gather([w_in[0].astype(MXU_DTYPE), jnp.broadcast_to(c, (8, D))], "gather_w_in_c")

    c64 = c_all.reshape(8 * N_DEV, D)
    b_loc = lax.dynamic_slice(b_ada, (0, me * n_ada), (1, n_ada))
    mod_part = _mod_fwd(c64, w_ada[0], b_loc)[::8]
    (mod_all,) = _all_gather([mod_part], "gather_mod")
    mod = lax.dynamic_slice(mod_all, (0, me, 0), (N_DEV, 1, n_ada)).reshape(3, D)
    mod3 = jnp.concatenate([mod, jnp.zeros((5, D), F32)], axis=0)

    onorm_t = hgrn_onorm_g
    gx, (r_in, r_a, r_b, r_o), row = _local_step(
        x[0], loss_target[0], mod3, norm_g, w_in_g, onorm_t, w_branch_a[0].astype(MXU_DTYPE),
        w_branch_b[0].astype(MXU_DTYPE), w_out[0].astype(MXU_DTYPE), rel_bias, hgrn_lb,
        final_g.reshape(1, D))
    packed8 = jnp.concatenate([row[None, :], jnp.zeros((7, P_END), F32)], axis=0)
    (packed,) = _all_gather([packed8], "gather_small")
    packed = packed.reshape(8 * N_DEV, P_END)

    g_in, d_in, nm_in, nv_in = _adamw_big(r_in, w_in[0], m_w_in[0], v_w_in[0], "adamw_w_in")
    g_a, d_a, nm_a, nv_a = _adamw_big(r_a, w_branch_a[0], m_w_branch_a[0], v_w_branch_a[0], "adamw_w_branch_a")
    g_b, d_b, nm_b, nv_b = _adamw_big(r_b, w_branch_b[0], m_w_branch_b[0], v_w_branch_b[0], "adamw_w_branch_b")
    g_o, d_o, nm_o, nv_o = _adamw_big(r_o, w_out[0], m_w_out[0], v_w_out[0], "adamw_w_out")

    dmod64 = lax.dynamic_slice(packed, (0, P_MOD + me * n_ada), (8 * N_DEV, n_ada))
    g_ada, d_ada, nm_ada, nv_ada = _adamw_w_ada(c64, dmod64, w_ada[0], m_w_ada[0], v_w_ada[0])

    def flat_relb(t):
        return jnp.pad(t.T, ((0, 0), (0, 128 - N_BUCKETS))).reshape(1, A_HEADS * 128)

    def unflat_relb(t):
        return t.reshape(A_HEADS, 128)[:, :N_BUCKETS].T

    fg2 = lambda t: t.reshape(1, D)
    smalls = _adamw_small(
        packed, b_ada, norm_g, hgrn_onorm_g, flat_relb(rel_bias), hgrn_lb, fg2(final_g),
        [m_b_ada, m_norm_g, m_hgrn_onorm_g, flat_relb(m_rel_bias), m_hgrn_lb, fg2(m_final_g)],
        [v_b_ada, v_norm_g, v_hgrn_onorm_g, flat_relb(v_rel_bias), v_hgrn_lb, fg2(v_final_g)])
    loss = smalls[0][0, 0]

    def small(kind):
        s = smalls[1 + 6 * kind:7 + 6 * kind]
        return s[0], s[1], s[2], unflat_relb(s[3]), s[4], s[5].reshape(D)

    def leaves(ada, sm, w_in_, wa_, wb_, wo_):
        b_, n_, o_, r_, l_, f_ = sm
        return (ada[None], b_, n_, w_in_[None], o_, wa_[None], wb_[None], wo_[None], r_, l_, f_)

    return (loss, gx[None],
            *leaves(g_ada, small(0), g_in, g_a, g_b, g_o),
            *leaves(d_ada, small(1), d_in, d_a, d_b, d_o),
            *leaves(nm_ada, small(2), nm_in, nm_a, nm_b, nm_o),
            *leaves(nv_ada, small(3), nv_in, nv_a, nv_b, nv_o))
```

```python
import functools
import math

import numpy as np
import jax
import jax.numpy as jnp
from jax import lax
from jax.experimental import pallas as pl
from jax.experimental.pallas import tpu as pltpu

F32 = jnp.float32
BF16 = jnp.bfloat16
MXU_DTYPE = jnp.bfloat16
XCHG_DTYPE = jnp.bfloat16

N_DEV = 8
D = 1024
A_HEADS = 8
A_HD = 64
A_W = A_HEADS * A_HD
A_BLK = 128
PATTERNS = ((128, 1), (512, 4), (2048, 16))
N_BUCKETS = 32
MAX_DISTANCE = 2048
NEG = -1e30
G_HEADS = 8
G_DK = 128
G_W = G_HEADS * G_DK
IN_W = 8 * D
EPS = 1e-6
ADAM_LR = 0.001
ADAM_B1 = 0.9
ADAM_B2 = 0.999
ADAM_EPS = 1e-08
ADAM_WD = 0.01
ADAM_STEP = 10

G_CHUNK = 128
G_SUB = 8
G_HPS_FWD = 4
G_HPS_BWD = 4
G_RB = 16
VMEM_LIMIT = 56 * 1024 * 1024

NN = (((1,), (0,)), ((), ()))
NT = (((1,), (1,)), ((), ()))
TN = (((0,), (0,)), ((), ()))
MESH = pl.DeviceIdType.MESH


def _mm(a, b, dims=NN):
    return lax.dot_general(a.astype(MXU_DTYPE), b.astype(MXU_DTYPE), dims,
                           preferred_element_type=F32)


def _mm_exact(t, x):
    hi = x.astype(BF16)
    r = x - hi.astype(F32)
    mid = r.astype(BF16)
    lo = (r - mid.astype(F32)).astype(BF16)
    tb = t.astype(BF16)
    return sum(lax.dot_general(tb, p, NN, preferred_element_type=F32) for p in (hi, mid, lo))


def _split(x):
    hi = x.astype(BF16)
    return hi, (x - hi.astype(F32)).astype(BF16)


def _mm_split(a, b, dims):
    dot = lambda p, q: lax.dot_general(p, q, dims, preferred_element_type=F32)
    return dot(a[0], b[0]) + dot(a[0], b[1]) + dot(a[1], b[0])


def _sigmoid(x):
    return 0.5 * jnp.tanh(0.5 * x) + 0.5


def _params(sem=None):
    return pltpu.CompilerParams(dimension_semantics=sem, vmem_limit_bytes=VMEM_LIMIT)


def _all_gather(xs, name):
    n = len(xs)

    def body(*refs):
        ins, outs = refs[:n], refs[n:2 * n]
        send_sems, recv_sems, local_sems = refs[2 * n:]
        x, y, c = lax.axis_index("x"), lax.axis_index("y"), lax.axis_index("c")
        me, sibling = (x, y, c), (x, y, 1 - c)
        chips = [(1 - x, y), (x, 1 - y), (1 - x, 1 - y)]

        def slot(ref, dev):
            return ref.at[4 * dev[0] + 2 * dev[1] + dev[2]]

        def copy(a, k, block, to, src=None):
            return pltpu.make_async_remote_copy(
                src_ref=slot(outs[a], block) if src is None else src,
                dst_ref=slot(outs[a], block),
                send_sem=send_sems.at[a, k], recv_sem=recv_sems.at[a, k],
                device_id=to, device_id_type=MESH)

        mine, first, passed = [], [], []
        for a in range(n):
            cp = pltpu.make_async_copy(ins[a], slot(outs[a], me), local_sems.at[a])
            cp.start()
            mine.append(cp)
            first.append(copy(a, 0, me, sibling, src=ins[a]))
            for j, chip in enumerate(chips):
                first.append(copy(a, 1 + j, me, (*chip, c), src=ins[a]))
        for cp in first:
            cp.start()
        for j, chip in enumerate(chips):
            for a in range(n):
                copy(a, 1 + j, (*chip, c), me).wait_recv()
                cp = copy(a, 4 + j, (*chip, c), sibling)
                cp.start()
                passed.append(cp)
        for a in range(n):
            copy(a, 0, sibling, me).wait_recv()
            for j, chip in enumerate(chips):
                copy(a, 4 + j, (*chip, 1 - c), me).wait_recv()
        for cp in first + passed:
            cp.wait_send()
        for cp in mine:
            cp.wait()

    any_spec = pl.BlockSpec(memory_space=pl.ANY)
    return pl.pallas_call(
        body, name=name,
        out_shape=[jax.ShapeDtypeStruct((N_DEV,) + v.shape, v.dtype) for v in xs],
        in_specs=[any_spec] * n, out_specs=[any_spec] * n,
        scratch_shapes=[pltpu.SemaphoreType.DMA((n, 7)), pltpu.SemaphoreType.DMA((n, 7)),
                        pltpu.SemaphoreType.DMA((n,))],
    )(*xs)


def _all_to_all_copies(ins, outs, send_sems, recv_sems, local_sems, gather=False):
    n = len(ins)
    x, y, c = lax.axis_index("x"), lax.axis_index("y"), lax.axis_index("c")
    me = 4 * x + 2 * y + c
    peers = []
    for m in range(1, N_DEV):
        peers.append((1 - x if m & 4 else x, 1 - y if m & 2 else y, 1 - c if m & 1 else c))

    def chunk(a, j):
        return ins[a] if gather else ins[a].at[j]

    def copy(a, k, landing):
        peer = peers[k]
        pid = 4 * peer[0] + 2 * peer[1] + peer[2]
        return pltpu.make_async_remote_copy(
            src_ref=chunk(a, pid), dst_ref=outs[a].at[pid if landing else me],
            send_sem=send_sems.at[a, k], recv_sem=recv_sems.at[a, k],
            device_id=peer, device_id_type=MESH)

    def local(a):
        return pltpu.make_async_copy(chunk(a, me), outs[a].at[me], local_sems.at[a])

    def start():
        for a in range(n):
            local(a).start()
        for k in range(N_DEV - 1):
            for a in range(n):
                copy(a, k, False).start()

    def wait():
        for k in range(N_DEV - 1):
            for a in range(n):
                copy(a, k, True).wait_recv()
        for k in range(N_DEV - 1):
            for a in range(n):
                copy(a, k, False).wait_send()
        for a in range(n):
            local(a).wait()

    return start, wait


def _mod_fwd(c64, w_ada, b_loc):
    def body(c_ref, w_ref, b_ref, o_ref):
        cv = c_ref[...]
        sc = cv * _sigmoid(cv)
        o_ref[...] = _mm(sc, w_ref[...]) + b_ref[...]

    return pl.pallas_call(
        body, name="mod_fwd",
        out_shape=jax.ShapeDtypeStruct((c64.shape[0], w_ada.shape[1]), F32),
        compiler_params=_params(),
    )(c64, w_ada, b_loc)


def _inproj(x, mod3, norm_g, w_in_g, blocks, tm=256):
    S = x.shape[0]
    ni = S // tm
    nb = len(blocks)

    def body(*refs):
        x_ref, mod_ref, g_ref, w_ref = refs[:4]
        b_ins = refs[4:4 + nb]
        proj_ref, ht_ref = refs[4 + nb:6 + nb]
        b_outs = refs[6 + nb:6 + 2 * nb]
        w_all, send_sems, recv_sems, local_sems = refs[6 + 2 * nb:]
        i = pl.program_id(0)
        start, wait = _all_to_all_copies(b_ins, b_outs, send_sems, recv_sems, local_sems, gather=True)

        @pl.when(i == 0)
        def _():
            start()
            pltpu.sync_copy(w_ref, w_all)

        xv = x_ref[...]
        r = lax.rsqrt(jnp.mean(xv * xv, axis=-1, keepdims=True) + EPS)
        h = ((xv * r * g_ref[...]) * (1.0 + mod_ref[1:2, :]) + mod_ref[0:1, :]).astype(MXU_DTYPE)
        ht_ref[...] = h.T
        for j in range(N_DEV):
            proj_ref[:, j * D:(j + 1) * D] = _mm(h, w_all[j])

        @pl.when(i == ni - 1)
        def _():
            wait()

    any_spec = pl.BlockSpec(memory_space=pl.ANY)
    return pl.pallas_call(
        body, name="inproj_gather",
        grid=(ni,),
        in_specs=[pl.BlockSpec((tm, D), lambda i: (i, 0)),
                  pl.BlockSpec((8, D), lambda i: (0, 0)),
                  pl.BlockSpec((1, D), lambda i: (0, 0)),
                  any_spec] + [any_spec] * nb,
        out_specs=[pl.BlockSpec((tm, IN_W), lambda i: (i, 0)),
                   pl.BlockSpec((D, tm), lambda i: (0, i))] + [any_spec] * nb,
        out_shape=[jax.ShapeDtypeStruct((S, IN_W), F32), jax.ShapeDtypeStruct((D, S), MXU_DTYPE)]
                  + [jax.ShapeDtypeStruct((N_DEV,) + b.shape, b.dtype) for b in blocks],
        scratch_shapes=[pltpu.VMEM(w_in_g.shape, w_in_g.dtype),
                        pltpu.SemaphoreType.DMA((nb, N_DEV - 1)), pltpu.SemaphoreType.DMA((nb, N_DEV - 1)),
                        pltpu.SemaphoreType.DMA((nb,))],
        compiler_params=_params(("arbitrary",)),
    )(x, mod3, norm_g, w_in_g, *blocks)


def _bucket_tables():
    qi = np.arange(A_BLK)[:, None]
    kj = np.arange(2 * A_BLK)[None, :]
    delta = qi + A_BLK - kj
    out = []
    for window, dil in PATTERNS:
        span = window // dil
        band = (delta >= 0) & (delta <= span)
        dist = np.clip(delta, 0, None) * dil
        max_exact = N_BUCKETS // 2
        nf = dist.astype(np.float32)
        large = max_exact + (np.log(np.maximum(nf, np.float32(1.0)) / np.float32(max_exact))
                             / np.float32(math.log(MAX_DISTANCE / max_exact))
                             * np.float32(N_BUCKETS - max_exact)).astype(np.int32)
        large = np.minimum(large, N_BUCKETS - 1)
        bucket = np.where(dist < max_exact, dist, large)
        out.append(np.where(band, bucket, -1).astype(np.int32))
    return np.stack(out)


def _bias_tables(rel_bias, buckets):
    def body(rb_ref, bk_ref, o_ref):
        h = pl.program_id(1)
        bk = bk_ref[0]
        acc = jnp.full(bk.shape, NEG, F32)
        for b in range(N_BUCKETS):
            acc = jnp.where(bk == b, rb_ref[b, h], acc)
        o_ref[0, 0] = acc

    return pl.pallas_call(
        body, name="bias_tables",
        grid=(3, A_HEADS),
        in_specs=[pl.BlockSpec(memory_space=pltpu.SMEM),
                  pl.BlockSpec((1, A_BLK, 2 * A_BLK), lambda p, h: (p, 0, 0))],
        out_specs=pl.BlockSpec((1, 1, A_BLK, 2 * A_BLK), lambda p, h: (p, h, 0, 0)),
        out_shape=jax.ShapeDtypeStruct((3, A_HEADS, A_BLK, 2 * A_BLK), F32),
        compiler_params=_params(("arbitrary", "arbitrary")),
    )(rel_bias, buckets)


A_TILES = 16


def _attn_heads_per_step(d):
    return A_HEADS if d == 1 else 2


def _attn_in_specs(sb, nsb, hw):
    w = A_HD * hw
    per = A_W // w

    def cur(col):
        return pl.BlockSpec((sb, w), lambda hp, n: (jnp.minimum(n, nsb - 1), per * col + hp))

    def prev(col):
        return pl.BlockSpec((sb, w), lambda hp, n: (jnp.maximum(jnp.minimum(n, nsb - 1) - 1, 0), per * col + hp))

    return [cur(0), prev(1), cur(1), prev(2), cur(2)]


def _rows(r, d):
    return pl.ds(r, A_BLK) if d == 1 else pl.ds(r, A_BLK, stride=d)


def _for_residues(d, hw, fn):
    unroll = min(d, max(1, A_TILES // hw))
    if d == unroll:
        _round_robin([g for r in range(d) for g in fn(r)])
    else:
        def group(g, c):
            _round_robin([t for u in range(unroll) for t in fn(g * unroll + u)])
            return c
        lax.fori_loop(0, d // unroll, group, 0)


def _attn_stack(t):
    first_half = lax.broadcasted_iota(jnp.int32, (1, 2 * A_HD), 1) < A_HD
    return jnp.concatenate([jnp.where(first_half, t, 0.0), jnp.where(first_half, 0.0, t)], axis=0)


def _attn_unstack(t2):
    first_half = lax.broadcasted_iota(jnp.int32, (1, 2 * A_HD), 1) < A_HD
    return jnp.where(first_half, t2[:A_BLK], t2[A_BLK:])


def _attn_scores(q, k, b_ref, pp, first):
    bias = jnp.concatenate([b_ref[2 * pp] + first, b_ref[2 * pp + 1] + first], axis=0)
    return _mm(_attn_stack(q), k, NT) * (A_HD ** -0.5) + bias


def _attn_fwd(proj, bias_p, d, name):
    S = proj.shape[0]
    sb = A_BLK * d
    nsb = S // sb
    hw = _attn_heads_per_step(d)

    def body(q_ref, kp_ref, kc_ref, vp_ref, vc_ref, b_ref, o_ref, l_ref):
        n = pl.program_id(1)
        kj = lax.broadcasted_iota(jnp.int32, (A_BLK, 2 * A_BLK), 1)
        first = jnp.where((n == 0) & (kj < A_BLK), NEG, 0.0).astype(F32)

        def residue(r):
            rows = _rows(r, d)

            def pair(pp):
                lanes = pl.ds(2 * A_HD * pp, 2 * A_HD)
                k = jnp.concatenate([kp_ref[rows, lanes], kc_ref[rows, lanes]], axis=0)
                v = jnp.concatenate([vp_ref[rows, lanes], vc_ref[rows, lanes]], axis=0)
                s = _attn_scores(q_ref[rows, lanes], k, b_ref, pp, first)
                yield
                m = jnp.max(s, axis=-1, keepdims=True)
                p = jnp.exp(s - m)
                den = jnp.sum(p, axis=-1, keepdims=True)
                pv = _mm(p, v)
                yield
                o_ref[rows, lanes] = _attn_unstack(pv / den)
                l_ref[rows, lanes] = _attn_unstack(jnp.broadcast_to(m + jnp.log(den), (2 * A_BLK, 2 * A_HD)))

            return [pair(pp) for pp in range(hw // 2)]

        _for_residues(d, hw, residue)

    out = pl.BlockSpec((sb, A_HD * hw), lambda hp, n: (n, hp))
    return pl.pallas_call(
        body, name=name,
        grid=(A_HEADS // hw, nsb),
        in_specs=_attn_in_specs(sb, nsb, hw) + [pl.BlockSpec((hw, A_BLK, 2 * A_BLK), lambda hp, n: (hp, 0, 0))],
        out_specs=[out, out],
        out_shape=[jax.ShapeDtypeStruct((S, A_W), F32)] * 2,
        compiler_params=_params(("parallel", "parallel")),
    )(proj, proj, proj, proj, proj, bias_p)


def _attn_combine(os, ls, proj, tm=512):
    S = proj.shape[0]

    def body(o1, o2, o3, l1, l2, l3, z_ref, ao_ref, lt_ref, oa_ref):
        a1, a2, a3 = l1[...], l2[...], l3[...]
        m = jnp.maximum(jnp.maximum(a1, a2), a3)
        e1, e2, e3 = jnp.exp(a1 - m), jnp.exp(a2 - m), jnp.exp(a3 - m)
        den = e1 + e2 + e3
        ao = (e1 * o1[...] + e2 * o2[...] + e3 * o3[...]) / den
        z = z_ref[...]
        ao_ref[...] = ao
        lt_ref[...] = m + jnp.log(den)
        oa_ref[...] = (ao * (z * _sigmoid(z))).astype(MXU_DTYPE)

    spec = pl.BlockSpec((tm, A_W), lambda i: (i, 0))
    return pl.pallas_call(
        body, name="attn_combine",
        grid=(S // tm,),
        in_specs=[spec] * 6 + [pl.BlockSpec((tm, A_W), lambda i: (i, 3))],
        out_specs=[spec] * 3,
        out_shape=[jax.ShapeDtypeStruct((S, A_W), F32), jax.ShapeDtypeStruct((S, A_W), F32),
                   jax.ShapeDtypeStruct((S, A_W), MXU_DTYPE)],
        compiler_params=_params(("parallel",)),
    )(*os, *ls, proj)


def _attn_pre_bwd(doa, ao, proj, tm=512):
    S = proj.shape[0]

    def body(doa_ref, ao_ref, z_ref, do_ref, dz_ref, dl_ref):
        z = z_ref[...]
        sg = _sigmoid(z)
        g = doa_ref[...]
        ao_v = ao_ref[...]
        do = g * (z * sg)
        do_ref[...] = do
        dz_ref[...] = (g * ao_v * (sg * (1.0 + z * (1.0 - sg)))).astype(MXU_DTYPE)
        prod = do * ao_v
        for h in range(A_HEADS):
            sl = slice(A_HD * h, A_HD * (h + 1))
            dl_ref[:, sl] = jnp.broadcast_to(jnp.sum(prod[:, sl], axis=-1, keepdims=True), (tm, A_HD))

    spec = pl.BlockSpec((tm, A_W), lambda i: (i, 0))
    return pl.pallas_call(
        body, name="attn_pre_bwd",
        grid=(S // tm,),
        in_specs=[spec, spec, pl.BlockSpec((tm, A_W), lambda i: (i, 3))],
        out_specs=[spec] * 3,
        out_shape=[jax.ShapeDtypeStruct((S, A_W), F32), jax.ShapeDtypeStruct((S, A_W), MXU_DTYPE),
                   jax.ShapeDtypeStruct((S, A_W), F32)],
        compiler_params=_params(("parallel",)),
    )(doa, ao, proj)


def _attn_bwd(proj, do, lt, delta, bias_p, d, name):
    S = proj.shape[0]
    sb = A_BLK * d
    nsb = S // sb
    hw = _attn_heads_per_step(d)

    def body(q_ref, kp_ref, kc_ref, vp_ref, vc_ref, do_ref, lt_ref, dl_ref, b_ref,
             dq_ref, dk_ref, dv_ref, db_ref, ck, cv):
        n = pl.program_id(1)

        @pl.when(n == 0)
        def _():
            db_ref[...] = jnp.zeros_like(db_ref)
            ck[...] = jnp.zeros_like(ck)
            cv[...] = jnp.zeros_like(cv)

        @pl.when(n < nsb)
        def _():
            kj = lax.broadcasted_iota(jnp.int32, (A_BLK, 2 * A_BLK), 1)
            first = jnp.where((n == 0) & (kj < A_BLK), NEG, 0.0).astype(F32)

            def residue(r):
                rows = _rows(r, d)

                def pair(pp):
                    lanes = pl.ds(2 * A_HD * pp, 2 * A_HD)
                    lt_r, dl_r = lt_ref[rows, lanes], dl_ref[rows, lanes]
                    k = jnp.concatenate([kp_ref[rows, lanes], kc_ref[rows, lanes]], axis=0)
                    v = jnp.concatenate([vp_ref[rows, lanes], vc_ref[rows, lanes]], axis=0)
                    q2 = _attn_stack(q_ref[rows, lanes])
                    do2 = _attn_stack(do_ref[rows, lanes])
                    col = lambda t: jnp.concatenate([t[:, 0:1], t[:, A_HD:A_HD + 1]], axis=0)
                    s = _attn_scores(q_ref[rows, lanes], k, b_ref, pp, first)
                    dp = _mm(do2, v, NT)
                    yield
                    p = jnp.exp(s - col(lt_r))
                    ds = p * (dp - col(dl_r))
                    db_ref[2 * pp] += ds[:A_BLK]
                    db_ref[2 * pp + 1] += ds[A_BLK:]
                    dq = _mm(ds, k)
                    dk = _mm(ds, q2, TN) * (A_HD ** -0.5)
                    dv = _mm(p, do2, TN)
                    yield
                    dq_ref[rows, lanes] = _attn_unstack(dq) * (A_HD ** -0.5)
                    dk_ref[rows, lanes] = ck[rows, lanes] + dk[:A_BLK]
                    dv_ref[rows, lanes] = cv[rows, lanes] + dv[:A_BLK]
                    ck[rows, lanes] = dk[A_BLK:]
                    cv[rows, lanes] = dv[A_BLK:]

                return [pair(pp) for pp in range(hw // 2)]

            _for_residues(d, hw, residue)

        @pl.when(n == nsb)
        def _():
            dk_ref[...] = ck[...]
            dv_ref[...] = cv[...]

    w = A_HD * hw
    row = pl.BlockSpec((sb, w), lambda hp, n: (jnp.minimum(n, nsb - 1), hp))
    lag = pl.BlockSpec((sb, w), lambda hp, n: (jnp.maximum(n - 1, 0), hp))
    tab = pl.BlockSpec((hw, A_BLK, 2 * A_BLK), lambda hp, n: (hp, 0, 0))
    return pl.pallas_call(
        body, name=name,
        grid=(A_HEADS // hw, nsb + 1),
        in_specs=_attn_in_specs(sb, nsb, hw) + [row, row, row, tab],
        out_specs=[row, lag, lag, tab],
        out_shape=[jax.ShapeDtypeStruct((S, A_W), F32)] * 3
                  + [jax.ShapeDtypeStruct((A_HEADS, A_BLK, 2 * A_BLK), F32)],
        scratch_shapes=[pltpu.VMEM((sb, w), F32), pltpu.VMEM((sb, w), F32)],
        compiler_params=_params(("parallel", "arbitrary")),
    )(proj, proj, proj, proj, proj, do, lt, delta, bias_p)


def _attn_assemble(dqs, dks, dvs, dz, tm=512):
    S = dz.shape[0]

    def body(q1, q2, q3, k1, k2, k3, v1, v2, v3, z_ref, p0_ref, p1_ref):
        p0_ref[:, :A_W] = (q1[...] + q2[...] + q3[...]).astype(MXU_DTYPE)
        p0_ref[:, A_W:] = (k1[...] + k2[...] + k3[...]).astype(MXU_DTYPE)
        p1_ref[:, :A_W] = (v1[...] + v2[...] + v3[...]).astype(MXU_DTYPE)
        p1_ref[:, A_W:] = z_ref[...]

    spec = pl.BlockSpec((tm, A_W), lambda i: (i, 0))
    wide = pl.BlockSpec((tm, 2 * A_W), lambda i: (i, 0))
    return pl.pallas_call(
        body, name="attn_assemble",
        grid=(S // tm,),
        in_specs=[spec] * 10,
        out_specs=[wide, wide],
        out_shape=[jax.ShapeDtypeStruct((S, 2 * A_W), MXU_DTYPE)] * 2,
        compiler_params=_params(("parallel",)),
    )(*dqs, *dks, *dvs, dz)


def _rel_bias_grad(dbs, buckets):
    def body(d1, d2, d3, bk_ref, o_ref):
        row = lax.broadcasted_iota(jnp.int32, (A_HEADS, 128), 0)
        lane = lax.broadcasted_iota(jnp.int32, (A_HEADS, 128), 1)
        acc = jnp.zeros((A_HEADS, 128), F32)
        for p, dref in enumerate((d1, d2, d3)):
            bk = bk_ref[p]
            for h in range(A_HEADS):
                ds = dref[h]
                for b in range(N_BUCKETS):
                    s = jnp.sum(jnp.where(bk == b, ds, 0.0), keepdims=True)
                    acc = acc + jnp.where((row == h) & (lane == b), s, 0.0)
        o_ref[...] = acc

    return pl.pallas_call(
        body, name="rel_bias_grad",
        out_shape=jax.ShapeDtypeStruct((A_HEADS, 128), F32),
        compiler_params=_params(),
    )(*dbs, buckets)


def _tri(c):
    t = np.tril(np.ones((c, c), np.float32))
    return jnp.asarray(t), jnp.asarray(t.T.copy())


def _fill_above(ref, x, pad):
    ref[0:G_SUB, :] = jnp.full((G_SUB, x.shape[1]), pad, F32)
    ref[G_SUB:, :] = x


def _fill_below(ref, x, pad):
    ref[0:x.shape[0], :] = x
    ref[x.shape[0]:, :] = jnp.full((G_SUB, x.shape[1]), pad, F32)


def _hgrn_gates(q_ref, f_ref, lbp_ref, tri_ref):
    qraw = q_ref[...]
    sq = _sigmoid(qraw)
    q = qraw * sq
    sg = _sigmoid(f_ref[...])
    lb = _sigmoid(lbp_ref[0:1, :] - lbp_ref[1:2, :])
    f = lb + (1.0 - lb) * sg
    k = 1.0 - f
    b = _mm_exact(tri_ref[...], jnp.log(f))
    return qraw, sq, q, sg, lb, f, k, b


def _hgrn_col(C, base, idx, hps):
    return pl.BlockSpec((C, hps * G_DK), lambda h, n: (idx(n), base * (G_HEADS // hps) + h))


def _round_robin(stages):
    live = list(stages)
    while live:
        nxt = []
        for g in live:
            try:
                next(g)
                nxt.append(g)
            except StopIteration:
                pass
        live = nxt


def _hgrn_levels(C):
    out, m = [], G_SUB
    while 2 * m <= C:
        out.append(m)
        m *= 2
    return out


def _hgrn_level_masks(C):
    ti = np.arange(C)[:, None]
    si = np.arange(C)[None, :]
    return jnp.asarray(np.stack([((ti // (2 * m) == si // (2 * m)) & (ti - si >= G_SUB)).astype(np.float32)
                                 for m in _hgrn_levels(C)]))


def _hgrn_level(b, q, k, C, m):
    zeros = jnp.zeros((m, G_DK), F32)
    eq, ek, qt, kt = [], [], [], []
    for blk in range(0, C // m, 2):
        lo, mid, hi = blk * m, (blk + 1) * m, (blk + 2) * m
        ref = b[mid:mid + 1]
        e_right = jnp.exp(b[mid:hi] - ref)
        e_left = jnp.exp(ref - b[lo:mid])
        eq += [zeros, e_right]
        ek += [e_left, zeros]
        qt += [zeros, q[mid:hi] * e_right]
        kt += [k[lo:mid] * e_left, zeros]
    cat = lambda parts: jnp.concatenate(parts, axis=0)
    return cat(qt), cat(kt), cat(eq), cat(ek)


def _hgrn_fwd(proj, hgrn_lb, onorm_g, C=G_CHUNK):
    S = proj.shape[0]
    nc = S // C
    tri, _ = _tri(C)
    masks = _hgrn_level_masks(C)
    hps = G_HPS_FWD

    def body(q_ref, f_ref, i_ref, z_ref, lbp_ref, go_ref, tri_ref, pm_ref, o_ref, ob_ref, st_ref, St, kp, vp, fp):
        @pl.when(pl.program_id(1) == 0)
        def _():
            St[...] = jnp.zeros_like(St)

        heads = []
        for hh in range(hps):
            ln = pl.ds(G_DK * hh, G_DK)
            heads.append(head(
                q_ref.at[:, ln], f_ref.at[:, ln], i_ref.at[:, ln], z_ref.at[:, ln], lbp_ref.at[:, ln], go_ref,
                tri_ref, pm_ref, o_ref.at[:, ln], ob_ref.at[:, ln], st_ref.at[0, hh], St.at[hh], kp.at[hh], vp.at[hh],
                fp.at[hh]))
        _round_robin(heads)

    def head(q_ref, f_ref, i_ref, z_ref, lbp_ref, go_ref, tri_ref, pm_ref, o_ref, ob_ref, st_ref, St, kp, vp, fp):
        _, _, q, _, _, f, k, b = _hgrn_gates(q_ref, f_ref, lbp_ref, tri_ref)
        v = i_ref[...]
        bC = b[C - 1:C, :]
        S0 = St[...]
        o = _mm(q * jnp.exp(b), S0, NT)
        yield
        _fill_above(kp, k, 0.0)
        _fill_above(vp, v, 0.0)
        _fill_above(fp, f, 1.0)
        near = []
        for r0 in range(0, C, G_RB):
            qb = q[r0:r0 + G_RB]
            acc = e = None
            for l in range(G_SUB):
                rows = pl.ds(G_SUB - l + r0, G_RB)
                if l > 0:
                    fl = fp[pl.ds(G_SUB - l + 1 + r0, G_RB), :]
                    e = fl if e is None else e * fl
                kl = kp[rows, :]
                a = jnp.sum(qb * kl if e is None else qb * kl * e, axis=-1, keepdims=True)
                t = a * vp[rows, :]
                acc = t if acc is None else acc + t
            near.append(acc)
        o = o + jnp.concatenate(near, axis=0)
        yield
        a_off = jnp.zeros((C, C), F32)
        for lv, m in enumerate(_hgrn_levels(C)):
            qt, kt, _, _ = _hgrn_level(b, q, k, C, m)
            prod = _mm_split(_split(qt), _split(kt), NT) if m == G_SUB else _mm(qt, kt, NT)
            a_off = a_off + pm_ref[lv] * prod
        yield
        o = o + _mm(a_off, v)
        S1 = S0 * jnp.exp(bC) + _mm(v, k * jnp.exp(bC - b), TN)
        St[...] = S1
        st_ref[...] = S1
        o_ref[...] = o
        r = lax.rsqrt(jnp.mean(o * o, axis=-1, keepdims=True) + EPS)
        z = z_ref[...]
        ob_ref[...] = (o * r * go_ref[...] * (z * _sigmoid(z))).astype(MXU_DTYPE)

    ident = lambda n: n
    w = hps * G_DK
    out = pl.BlockSpec((C, w), lambda h, n: (n, h))
    return pl.pallas_call(
        body, name="hgrn_fwd",
        grid=(G_HEADS // hps, nc),
        in_specs=[_hgrn_col(C, base, ident, hps) for base in (2, 3, 4, 5)] + [
                  pl.BlockSpec((2, w), lambda h, n: (0, h)),
                  pl.BlockSpec((1, G_DK), lambda h, n: (0, 0)),
                  pl.BlockSpec((C, C), lambda h, n: (0, 0)),
                  pl.BlockSpec(masks.shape, lambda h, n: (0, 0, 0))],
        out_specs=[out, out, pl.BlockSpec((1, hps, G_DK, G_DK), lambda h, n: (n, h, 0, 0))],
        out_shape=[jax.ShapeDtypeStruct((S, G_W), F32), jax.ShapeDtypeStruct((S, G_W), MXU_DTYPE),
                   jax.ShapeDtypeStruct((nc, G_HEADS, G_DK, G_DK), F32)],
        scratch_shapes=[pltpu.VMEM((hps, G_DK, G_DK), F32)] + [pltpu.VMEM((hps, C + G_SUB, G_DK), F32)] * 3,
        compiler_params=_params(("parallel", "arbitrary")),
    )(proj, proj, proj, proj, hgrn_lb, onorm_g, tri, masks)


def _hgrn_bwd(proj, o_raw, dob, states, hgrn_lb, onorm_g, C=G_CHUNK):
    S = proj.shape[0]
    nc = S // C
    tri, triu = _tri(C)
    masks = _hgrn_level_masks(C)
    hps = G_HPS_BWD

    def body(q_ref, f_ref, i_ref, z_ref, o_ref, dob_ref, s0_ref, s1_ref, lbp_ref, go_ref, tri_ref, triu_ref,
             pm_ref, dq_ref, df_ref, di_ref, dz_ref, dlb_ref, dgo_ref, dSt, *shifted):
        @pl.when(pl.program_id(1) == 0)
        def _():
            dSt[...] = jnp.zeros_like(dSt)
            dlb_ref[...] = jnp.zeros_like(dlb_ref)
            dgo_ref[...] = jnp.zeros_like(dgo_ref)

        heads = []
        for hh in range(hps):
            ln = pl.ds(G_DK * hh, G_DK)
            heads.append(head(
                q_ref.at[:, ln], f_ref.at[:, ln], i_ref.at[:, ln], z_ref.at[:, ln], o_ref.at[:, ln],
                dob_ref.at[:, ln], s0_ref.at[0, hh], s1_ref.at[0, hh], lbp_ref.at[:, ln], go_ref, tri_ref, triu_ref,
                pm_ref, dq_ref.at[:, ln], df_ref.at[:, ln], di_ref.at[:, ln], dz_ref.at[:, ln], dlb_ref.at[:, ln],
                dgo_ref.at[pl.ds(8 * hh, 8), :], dSt.at[hh], *[t.at[hh] for t in shifted]))
        _round_robin(heads)

    def head(q_ref, f_ref, i_ref, z_ref, o_ref, dob_ref, s0_ref, s1_ref, lbp_ref, go_ref, tri_ref, triu_ref,
             pm_ref, dq_ref, df_ref, di_ref, dz_ref, dlb_ref, dgo_ref, dSt, kp, vp, fp, qn, dn_, fn, xs, dac):
        cn = nc - 1 - pl.program_id(1)
        qraw, sq, q, sg, lb, f, k, b = _hgrn_gates(q_ref, f_ref, lbp_ref, tri_ref)
        v = i_ref[...]
        bC = b[C - 1:C, :]
        eb = jnp.exp(b)
        ecb = jnp.exp(bC - b)
        o = o_ref[...]
        z = z_ref[...]
        sz = _sigmoid(z)
        go = go_ref[...]
        g_ob = dob_ref[...]
        r = lax.rsqrt(jnp.mean(o * o, axis=-1, keepdims=True) + EPS)
        nh = o * r
        dnrm = g_ob * (z * sz)
        dz_ref[...] = (g_ob * (nh * go) * (sz * (1.0 + z * (1.0 - sz)))).astype(MXU_DTYPE)
        dgo_ref[0:1, :] += jnp.sum(dnrm * nh, axis=0, keepdims=True)
        dn = dnrm * go
        do = r * (dn - nh * jnp.mean(dn * nh, axis=-1, keepdims=True))

        yield
        S0 = jnp.where(cn == 0, 0.0, s0_ref[...])
        S1 = s1_ref[...]
        dS1 = dSt[...]
        dq = eb * _mm(do, S0)
        dk = ecb * _mm(v, dS1)
        dv = _mm(k * ecb, dS1, NT)
        bnd = jnp.sum(dS1 * S1, axis=0, keepdims=True)
        dSt[...] = dS1 * jnp.exp(bC) + _mm(do, q * eb, TN)

        _fill_above(kp, k, 0.0)
        _fill_above(vp, v, 0.0)
        _fill_above(fp, f, 1.0)
        _fill_below(qn, q, 0.0)
        _fill_below(dn_, do, 0.0)
        _fill_below(fn, f, 1.0)
        yield
        for r0 in range(0, C, G_RB):
            do_b = do[r0:r0 + G_RB]
            for l in range(G_SUB):
                xs[pl.ds(l * C + r0, G_RB), :] = (do_b * vp[pl.ds(G_SUB - l + r0, G_RB), :]).astype(MXU_DTYPE)
        dac[0:G_SUB * C, :] = _mm(xs[...], jnp.ones((G_DK, G_DK), MXU_DTYPE))
        dac[G_SUB * C:, :] = jnp.zeros((G_SUB, G_DK), F32)
        yield
        near_q, near_k, near_v = [], [], []
        for r0 in range(0, C, G_RB):
            k_b = k[r0:r0 + G_RB]
            aq = ak = av = e = e2 = None
            for l in range(G_SUB):
                down, up = pl.ds(G_SUB - l + r0, G_RB), pl.ds(l + r0, G_RB)
                if l > 0:
                    fl = fp[pl.ds(G_SUB - l + 1 + r0, G_RB), :]
                    e = fl if e is None else e * fl
                    fu = fn[up, :]
                    e2 = fu if e2 is None else e2 * fu
                kl = kp[down, :]
                t = dac[pl.ds(l * C + r0, G_RB), :] * (kl if e is None else kl * e)
                aq = t if aq is None else aq + t
                qu = qn[up, :]
                qe = qu if e2 is None else qu * e2
                dou = dn_[up, :]
                a2 = jnp.sum(qe * k_b, axis=-1, keepdims=True)
                t = dac[pl.ds(l * C + l + r0, G_RB), :] * qe
                ak = t if ak is None else ak + t
                t = a2 * dou
                av = t if av is None else av + t
            near_q.append(aq)
            near_k.append(ak)
            near_v.append(av)
        dq = dq + jnp.concatenate(near_q, axis=0)
        dk = dk + jnp.concatenate(near_k, axis=0)
        dv = dv + jnp.concatenate(near_v, axis=0)

        yield
        da_all = _mm(do, v, NT)
        a_off = jnp.zeros((C, C), F32)
        for lv, m in enumerate(_hgrn_levels(C)):
            qt, kt, eq, ek = _hgrn_level(b, q, k, C, m)
            da_m = pm_ref[lv] * da_all
            if m == G_SUB:
                qs, ks, das = _split(qt), _split(kt), _split(da_m)
                a_off = a_off + pm_ref[lv] * _mm_split(qs, ks, NT)
                dq = dq + _mm_split(das, ks, NN) * eq
                dk = dk + _mm_split(das, qs, TN) * ek
            else:
                a_off = a_off + pm_ref[lv] * _mm(qt, kt, NT)
                dq = dq + _mm(da_m, kt) * eq
                dk = dk + _mm(da_m, qt, TN) * ek
        dv = dv + _mm(a_off, do, TN)

        yield
        row = lax.broadcasted_iota(jnp.int32, (C, 1), 0)
        db = q * dq - k * dk + jnp.where(row == C - 1, bnd, 0.0)
        dg = _mm_exact(triu_ref[...], db)
        df = dg / f - dk
        df_ref[...] = (df * (1.0 - lb) * (sg * (1.0 - sg))).astype(MXU_DTYPE)
        dlb_ref[0:1, :] += jnp.sum(df * (1.0 - sg), axis=0, keepdims=True)
        dq_ref[...] = (dq * (sq * (1.0 + qraw * (1.0 - sq)))).astype(MXU_DTYPE)
        di_ref[...] = dv.astype(MXU_DTYPE)

    rev = lambda n: nc - 1 - n
    w = hps * G_DK
    blk = pl.BlockSpec((C, w), lambda h, n: (nc - 1 - n, h))
    return pl.pallas_call(
        body, name="hgrn_bwd",
        grid=(G_HEADS // hps, nc),
        in_specs=[_hgrn_col(C, base, rev, hps) for base in (2, 3, 4, 5)] + [
                  blk, blk,
                  pl.BlockSpec((1, hps, G_DK, G_DK), lambda h, n: (jnp.maximum(nc - 2 - n, 0), h, 0, 0)),
                  pl.BlockSpec((1, hps, G_DK, G_DK), lambda h, n: (nc - 1 - n, h, 0, 0)),
                  pl.BlockSpec((2, w), lambda h, n: (0, h)),
                  pl.BlockSpec((1, G_DK), lambda h, n: (0, 0)),
                  pl.BlockSpec((C, C), lambda h, n: (0, 0)),
                  pl.BlockSpec((C, C), lambda h, n: (0, 0)),
                  pl.BlockSpec(masks.shape, lambda h, n: (0, 0, 0))],
        out_specs=[blk, blk, blk, blk,
                   pl.BlockSpec((8, w), lambda h, n: (0, h)),
                   pl.BlockSpec((8 * hps, G_DK), lambda h, n: (h, 0))],
        out_shape=[jax.ShapeDtypeStruct((S, G_W), MXU_DTYPE)] * 4
                  + [jax.ShapeDtypeStruct((8, G_W), F32), jax.ShapeDtypeStruct((8 * G_HEADS, G_DK), F32)],
        scratch_shapes=[pltpu.VMEM((hps, G_DK, G_DK), F32)] + [pltpu.VMEM((hps, C + G_SUB, G_DK), F32)] * 6
                       + [pltpu.VMEM((hps, G_SUB * C, G_DK), MXU_DTYPE),
                          pltpu.VMEM((hps, G_SUB * C + G_SUB, G_DK), F32)],
        compiler_params=_params(("parallel", "arbitrary")),
    )(proj, proj, proj, proj, o_raw, dob, states, states, hgrn_lb, onorm_g, tri, triu, masks)


def _tail(x, target, oa, ob, proj, mod3, final_g, wa, wb, wo, tm=256):
    S = x.shape[0]
    nt = S // tm

    def body(x_ref, t_ref, oa_ref, ob_ref, ga_ref, gb_ref, mod_ref, fg_ref, wa_ref, wb_ref, wo_ref,
             dx2_ref, doa_ref, dob_ref, dga_ref, dgb_ref, sums_ref, gwa_ref, gwb_ref, gwo_ref,
             acc_a, acc_b, acc_o):
        i = pl.program_id(0)

        @pl.when(i == 0)
        def _():
            sums_ref[...] = jnp.zeros_like(sums_ref)
            acc_a[...] = jnp.zeros_like(acc_a)
            acc_b[...] = jnp.zeros_like(acc_b)
            acc_o[...] = jnp.zeros_like(acc_o)

        oa_v, ob_v = oa_ref[...], ob_ref[...]
        pa = _mm(oa_v, wa_ref[...])
        pb = _mm(ob_v, wb_ref[...])
        sa, sb = _sigmoid(ga_ref[...]), _sigmoid(gb_ref[...])
        ym = sa * pa + sb * pb
        u = _mm(ym, wo_ref[...])
        gate = mod_ref[2:3, :]
        fg = fg_ref[...]
        x2 = x_ref[...] + gate * u
        r2 = lax.rsqrt(jnp.mean(x2 * x2, axis=-1, keepdims=True) + EPS)
        xn2 = x2 * r2
        e = xn2 * fg - t_ref[...]
        dy = e * (1.0 / D)
        dn = dy * fg
        dx2 = r2 * (dn - xn2 * jnp.mean(dn * xn2, axis=-1, keepdims=True))
        dx2_ref[...] = dx2
        sums_ref[0:1, :] += jnp.sum(dy * xn2, axis=0, keepdims=True)
        sums_ref[1:2, :] += jnp.sum(dx2 * u, axis=0, keepdims=True)
        sums_ref[2:3, :] += (0.5 / D) * jnp.sum(e * e, axis=0, keepdims=True)
        du = dx2 * gate
        dym = _mm(du, wo_ref[...], NT)
        acc_o[...] += _mm(ym, du, TN)
        dpa, dpb = dym * sa, dym * sb
        dga_ref[...] = (dym * pa * (sa * (1.0 - sa))).astype(MXU_DTYPE)
        dgb_ref[...] = (dym * pb * (sb * (1.0 - sb))).astype(MXU_DTYPE)
        doa_ref[...] = _mm(dpa, wa_ref[...], NT)
        dob_ref[...] = _mm(dpb, wb_ref[...], NT)
        acc_a[...] += _mm(oa_v, dpa, TN)
        acc_b[...] += _mm(ob_v, dpb, TN)

        @pl.when(i == nt - 1)
        def _():
            pltpu.sync_copy(acc_a, gwa_ref)
            pltpu.sync_copy(acc_b, gwb_ref)
            pltpu.sync_copy(acc_o, gwo_ref)

    row = lambda w: pl.BlockSpec((tm, w), lambda i: (i, 0))
    full = lambda a, b: pl.BlockSpec((a, b), lambda i: (0, 0))
    any_spec = pl.BlockSpec(memory_space=pl.ANY)
    return pl.pallas_call(
        body, name="tail",
        grid=(nt,),
        in_specs=[row(D), row(D), row(A_W), row(D),
                  pl.BlockSpec((tm, D), lambda i: (i, 6)), pl.BlockSpec((tm, D), lambda i: (i, 7)),
                  full(8, D), full(1, D), full(A_W, D), full(D, D), full(D, D)],
        out_specs=[row(D), row(A_W), row(D), row(D), row(D), full(8, D), any_spec, any_spec, any_spec],
        out_shape=[jax.ShapeDtypeStruct((S, D), F32), jax.ShapeDtypeStruct((S, A_W), F32),
                   jax.ShapeDtypeStruct((S, D), F32), jax.ShapeDtypeStruct((S, D), MXU_DTYPE),
                   jax.ShapeDtypeStruct((S, D), MXU_DTYPE), jax.ShapeDtypeStruct((8, D), F32),
                   jax.ShapeDtypeStruct((A_W, D), F32), jax.ShapeDtypeStruct((D, D), F32),
                   jax.ShapeDtypeStruct((D, D), F32)],
        scratch_shapes=[pltpu.VMEM((A_W, D), F32), pltpu.VMEM((D, D), F32), pltpu.VMEM((D, D), F32)],
        compiler_params=_params(("arbitrary",)),
    )(x, target, oa, ob, proj, proj, mod3, final_g, wa, wb, wo)


def _dh(pieces, w_in_g, x, dx2, mod3, norm_g, grads, tm=256):
    S = x.shape[0]
    ni = S // tm
    ng = len(grads)

    def body(*refs):
        p_refs = refs[:N_DEV]
        w_ref, x_ref, dx2_ref, mod_ref, g_ref = refs[N_DEV:N_DEV + 5]
        g_ins = refs[N_DEV + 5:N_DEV + 5 + ng]
        gx_ref, sums_ref = refs[N_DEV + 5 + ng:N_DEV + 7 + ng]
        g_outs = refs[N_DEV + 7 + ng:N_DEV + 7 + 2 * ng]
        w_all, send_sems, recv_sems, local_sems = refs[N_DEV + 7 + 2 * ng:]
        i = pl.program_id(0)
        start, wait = _all_to_all_copies(g_ins, g_outs, send_sems, recv_sems, local_sems)

        @pl.when(i == 0)
        def _():
            start()
            sums_ref[...] = jnp.zeros_like(sums_ref)
            pltpu.sync_copy(w_ref, w_all)

        dh = _mm(p_refs[0][...], w_all[0], NT)
        for k in range(1, N_DEV):
            dh = dh + _mm(p_refs[k][...], w_all[k], NT)
        xv = x_ref[...]
        g = g_ref[...]
        sc1 = 1.0 + mod_ref[1:2, :]
        r = lax.rsqrt(jnp.mean(xv * xv, axis=-1, keepdims=True) + EPS)
        xn = xv * r
        sums_ref[0:1, :] += jnp.sum(dh, axis=0, keepdims=True)
        sums_ref[1:2, :] += jnp.sum(dh * (xn * g), axis=0, keepdims=True)
        sums_ref[2:3, :] += jnp.sum(dh * sc1 * xn, axis=0, keepdims=True)
        dxn = dh * sc1 * g
        gx_ref[...] = dx2_ref[...] + r * (dxn - xn * jnp.mean(dxn * xn, axis=-1, keepdims=True))

        @pl.when(i == ni - 1)
        def _():
            wait()

    row = pl.BlockSpec((tm, D), lambda i: (i, 0))
    any_spec = pl.BlockSpec(memory_space=pl.ANY)
    return pl.pallas_call(
        body, name="dh_scatter",
        grid=(ni,),
        in_specs=[row] * N_DEV
                 + [any_spec, row, row,
                    pl.BlockSpec((8, D), lambda i: (0, 0)),
                    pl.BlockSpec((1, D), lambda i: (0, 0))]
                 + [any_spec] * ng,
        out_specs=[row, pl.BlockSpec((8, D), lambda i: (0, 0))] + [any_spec] * ng,
        out_shape=[jax.ShapeDtypeStruct((S, D), F32), jax.ShapeDtypeStruct((8, D), F32)]
                  + [jax.ShapeDtypeStruct(g.shape, g.dtype) for g in grads],
        scratch_shapes=[pltpu.VMEM(w_in_g.shape, w_in_g.dtype),
                        pltpu.SemaphoreType.DMA((ng, N_DEV - 1)), pltpu.SemaphoreType.DMA((ng, N_DEV - 1)),
                        pltpu.SemaphoreType.DMA((ng,))],
        compiler_params=_params(("arbitrary",)),
    )(*pieces, w_in_g, x, dx2, mod3, norm_g, *grads)


def _gw_in(ht, pieces, tm=512):
    S = ht.shape[1]
    nt = S // tm

    def body(*refs):
        h_ref, p_refs, o_ref, acc = refs[0], refs[1:1 + N_DEV], refs[1 + N_DEV], refs[2 + N_DEV]
        j, i = pl.program_id(0), pl.program_id(1)

        @pl.when(i == 0)
        def _():
            acc[...] = jnp.zeros_like(acc)

        for k in range(N_DEV):
            @pl.when(j == k)
            def _(k=k):
                acc[...] += _mm(h_ref[...], p_refs[k][...])

        @pl.when(i == nt - 1)
        def _():
            o_ref[0] = acc[...].astype(XCHG_DTYPE)

    def piece(k):
        return pl.BlockSpec((tm, D), lambda j, i: (jnp.where(j == k, i, 0), 0))

    return pl.pallas_call(
        body, name="gw_in",
        grid=(N_DEV, nt),
        in_specs=[pl.BlockSpec((D, tm), lambda j, i: (0, i))] + [piece(k) for k in range(N_DEV)],
        out_specs=pl.BlockSpec((1, D, D), lambda j, i: (j, 0, 0)),
        out_shape=jax.ShapeDtypeStruct((N_DEV, D, D), XCHG_DTYPE),
        scratch_shapes=[pltpu.VMEM((D, D), F32)],
        compiler_params=_params(("parallel", "arbitrary")),
    )(ht, *pieces)


def _adamw_math(w, g, m, v):
    m = ADAM_B1 * m + (1.0 - ADAM_B1) * g
    v = ADAM_B2 * v + (1.0 - ADAM_B2) * (g * g)
    m_hat = m / (1.0 - ADAM_B1 ** ADAM_STEP)
    v_hat = v / (1.0 - ADAM_B2 ** ADAM_STEP)
    delta = -ADAM_LR * (m_hat / (jnp.sqrt(v_hat) + ADAM_EPS) + ADAM_WD * w)
    return delta, m, v


def _adamw_big(recv, w, m, v, name, tr=128):
    M, N = w.shape
    tr = min(tr, M)

    def body(r_ref, w_ref, m_ref, v_ref, g_ref, d_ref, nm_ref, nv_ref):
        g = r_ref[0].astype(F32)
        for j in range(1, N_DEV):
            g = g + r_ref[j].astype(F32)
        g_ref[...] = g
        d_ref[...], nm_ref[...], nv_ref[...] = _adamw_math(w_ref[...], g, m_ref[...], v_ref[...])

    blk = pl.BlockSpec((tr, N), lambda i: (i, 0))
    return pl.pallas_call(
        body, name=name,
        grid=(M // tr,),
        in_specs=[pl.BlockSpec((N_DEV, tr, N), lambda i: (0, i, 0)), blk, blk, blk],
        out_specs=[blk] * 4,
        out_shape=[jax.ShapeDtypeStruct((M, N), F32)] * 4,
        compiler_params=_params(("parallel",)),
    )(recv, w, m, v)


def _adamw_w_ada(c64, dmod64, w, m, v):
    def body(c_ref, dm_ref, w_ref, m_ref, v_ref, g_ref, d_ref, nm_ref, nv_ref):
        cv = c_ref[...]
        g = _mm(cv * _sigmoid(cv), dm_ref[...], TN)
        g_ref[...] = g
        d_ref[...], nm_ref[...], nv_ref[...] = _adamw_math(w_ref[...], g, m_ref[...], v_ref[...])

    return pl.pallas_call(
        body, name="adamw_w_ada",
        out_shape=[jax.ShapeDtypeStruct(w.shape, F32)] * 4,
        compiler_params=_params(),
    )(c64, dmod64, w, m, v)


P_MOD, P_NORM, P_ONORM, P_RELB, P_LB, P_FINAL, P_LOSS, P_END = (0, 3 * D, 4 * D, 5 * D, 6 * D, 7 * D, 8 * D, 9 * D)


def _adamw_small(packed, b_ada, norm_g, onorm_g, relb, hgrn_lb, final_g, ms, vs):
    def body(pk_ref, b_ref, ng_ref, og_ref, rb_ref, lb_ref, fg_ref,
             mb, mn, mo, mr, ml, mf, vb, vn, vo, vr, vl, vf,
             loss_ref, gb, gn, go, gr, gl, gf, db, dn, do, dr, dl, df,
             nmb, nmn, nmo, nmr, nml, nmf, nvb, nvn, nvo, nvr, nvl, nvf):
        tot = pk_ref[0:1, :]
        for j in range(1, N_DEV):
            tot = tot + pk_ref[8 * j:8 * j + 1, :]
        loss_ref[...] = jnp.broadcast_to(jnp.sum(tot[:, P_LOSS:P_END], axis=-1, keepdims=True), (8, 128))

        def upd(g, w_ref, m_ref, v_ref, g_out, d_out, m_out, v_out):
            g_out[...] = g
            d_out[...], m_out[...], v_out[...] = _adamw_math(w_ref[...], g, m_ref[...], v_ref[...])

        upd(tot[:, P_MOD:P_NORM], b_ref, mb, vb, gb, db, nmb, nvb)
        upd(tot[:, P_NORM:P_ONORM], ng_ref, mn, vn, gn, dn, nmn, nvn)
        g_on = tot[:, P_ONORM:P_ONORM + G_DK]
        for h in range(1, G_HEADS):
            g_on = g_on + tot[:, P_ONORM + G_DK * h:P_ONORM + G_DK * (h + 1)]
        upd(g_on, og_ref, mo, vo, go, do, nmo, nvo)
        upd(tot[:, P_RELB:P_LB], rb_ref, mr, vr, gr, dr, nmr, nvr)
        a = lb_ref[...]
        lb = _sigmoid(a[0:1, :] - a[1:2, :])
        g0 = tot[:, P_LB:P_FINAL] * lb * (1.0 - lb)
        row = lax.broadcasted_iota(jnp.int32, (2, D), 0)
        upd(jnp.where(row == 0, g0, -g0), lb_ref, ml, vl, gl, dl, nml, nvl)
        upd(tot[:, P_FINAL:P_LOSS], fg_ref, mf, vf, gf, df, nmf, nvf)

    shapes = [b_ada.shape, norm_g.shape, onorm_g.shape, relb.shape, hgrn_lb.shape, final_g.shape]
    outs = [jax.ShapeDtypeStruct((8, 128), F32)] + [jax.ShapeDtypeStruct(s, F32) for s in shapes] * 4
    return pl.pallas_call(
        body, name="adamw_small",
        out_shape=outs,
        compiler_params=_params(),
    )(packed, b_ada, norm_g, onorm_g, relb, hgrn_lb, final_g, *ms, *vs)


def _local_step(x, target, mod3, norm_g, w_in_g, onorm_g, wa_blk, wb_blk, wo_blk, rel_bias, hgrn_lb, final_g):
    buckets = jnp.asarray(_bucket_tables())
    bias = _bias_tables(rel_bias, buckets)
    proj, ht, wa_g, wb_g, wo_g = _inproj(x, mod3, norm_g, w_in_g, [wa_blk, wb_blk, wo_blk])
    wa = wa_g.transpose(1, 0, 2).reshape(A_W, D)
    wb = wb_g.reshape(D, D)
    wo = wo_g.reshape(D, D)
    os, ls = [], []
    for p, (_, d) in enumerate(PATTERNS):
        o, l = _attn_fwd(proj, bias[p], d, "attn_fwd_d%d" % d)
        os.append(o)
        ls.append(l)
    ao, lt, oa = _attn_combine(os, ls, proj)
    o_raw, ob, states = _hgrn_fwd(proj, hgrn_lb, onorm_g)
    dx2, doa, dob, dga, dgb, tsums, gwa, gwb, gwo = _tail(x, target, oa, ob, proj, mod3, final_g, wa, wb, wo)
    do, dza, delta = _attn_pre_bwd(doa, ao, proj)
    dqs, dks, dvs, dbs = [], [], [], []
    for p, (_, d) in enumerate(PATTERNS):
        dq, dk, dv, db = _attn_bwd(proj, do, lt, delta, bias[p], d, "attn_bwd_d%d" % d)
        dqs.append(dq)
        dks.append(dk)
        dvs.append(dv)
        dbs.append(db)
    p0, p1 = _attn_assemble(dqs, dks, dvs, dza)
    g_relb = _rel_bias_grad(dbs, buckets)
    dqb, dfb, dib, dzb, dlb, dgo = _hgrn_bwd(proj, o_raw, dob, states, hgrn_lb, onorm_g)
    pieces = [p0, p1, dqb, dfb, dib, dzb, dga, dgb]
    grads = [_gw_in(ht, pieces),
             gwa.astype(XCHG_DTYPE).reshape(A_W, N_DEV, D // N_DEV).transpose(1, 0, 2),
             gwb.astype(XCHG_DTYPE).reshape(N_DEV, D // N_DEV, D),
             gwo.astype(XCHG_DTYPE).reshape(N_DEV, D // N_DEV, D)]
    gx, hsums, *received = _dh(pieces, w_in_g, x, dx2, mod3, norm_g, grads)
    row = jnp.concatenate([
        hsums[0], hsums[1], tsums[1],
        hsums[2],
        dgo.reshape(G_HEADS, 8, G_DK)[:, 0].reshape(-1),
        g_relb.reshape(-1),
        dlb[0],
        tsums[0],
        tsums[2],
    ])
    return gx, received, row


def kernel(x, c, w_ada, b_ada, norm_g, w_in, hgrn_onorm_g, w_branch_a, w_branch_b, w_out, rel_bias, hgrn_lb, final_g, loss_target, m_w_ada, m_b_ada, m_norm_g, m_w_in, m_hgrn_onorm_g, m_w_branch_a, m_w_branch_b, m_w_out, m_rel_bias, m_hgrn_lb, m_final_g, v_w_ada, v_b_ada, v_norm_g, v_w_in, v_hgrn_onorm_g, v_w_branch_a, v_w_branch_b, v_w_out, v_rel_bias, v_hgrn_lb, v_final_g):
    me = 4 * lax.axis_index("x") + 2 * lax.axis_index("y") + lax.axis_index("c")
    n_ada = w_ada.shape[2]

    w_in_g, c_all = _all_gather([w_in[0].astype(MXU_DTYPE), jnp.broadcast_to(c, (8, D))], "gather_w_in_c")

    c64 = c_all.reshape(8 * N_DEV, D)
    b_loc = lax.dynamic_slice(b_ada, (0, me * n_ada), (1, n_ada))
    mod_part = _mod_fwd(c64, w_ada[0], b_loc)[::8]
    (mod_all,) = _all_gather([mod_part], "gather_mod")
    mod = lax.dynamic_slice(mod_all, (0, me, 0), (N_DEV, 1, n_ada)).reshape(3, D)
    mod3 = jnp.concatenate([mod, jnp.zeros((5, D), F32)], axis=0)

    onorm_t = hgrn_onorm_g
    gx, (r_in, r_a, r_b, r_o), row = _local_step(
        x[0], loss_target[0], mod3, norm_g, w_in_g, onorm_t, w_branch_a[0].astype(MXU_DTYPE),
        w_branch_b[0].astype(MXU_DTYPE), w_out[0].astype(MXU_DTYPE), rel_bias, hgrn_lb,
        final_g.reshape(1, D))
    packed8 = jnp.concatenate([row[None, :], jnp.zeros((7, P_END), F32)], axis=0)
    (packed,) = _all_gather([packed8], "gather_small")
    packed = packed.reshape(8 * N_DEV, P_END)

    g_in, d_in, nm_in, nv_in = _adamw_big(r_in, w_in[0], m_w_in[0], v_w_in[0], "adamw_w_in")
    g_a, d_a, nm_a, nv_a = _adamw_big(r_a, w_branch_a[0], m_w_branch_a[0], v_w_branch_a[0], "adamw_w_branch_a")
    g_b, d_b, nm_b, nv_b = _adamw_big(r_b, w_branch_b[0], m_w_branch_b[0], v_w_branch_b[0], "adamw_w_branch_b")
    g_o, d_o, nm_o, nv_o = _adamw_big(r_o, w_out[0], m_w_out[0], v_w_out[0], "adamw_w_out")

    dmod64 = lax.dynamic_slice(packed, (0, P_MOD + me * n_ada), (8 * N_DEV, n_ada))
    g_ada, d_ada, nm_ada, nv_ada = _adamw_w_ada(c64, dmod64, w_ada[0], m_w_ada[0], v_w_ada[0])

    def flat_relb(t):
        return jnp.pad(t.T, ((0, 0), (0, 128 - N_BUCKETS))).reshape(1, A_HEADS * 128)

    def unflat_relb(t):
        return t.reshape(A_HEADS, 128)[:, :N_BUCKETS].T

    fg2 = lambda t: t.reshape(1, D)
    smalls = _adamw_small(
        packed, b_ada, norm_g, hgrn_onorm_g, flat_relb(rel_bias), hgrn_lb, fg2(final_g),
        [m_b_ada, m_norm_g, m_hgrn_onorm_g, flat_relb(m_rel_bias), m_hgrn_lb, fg2(m_final_g)],
        [v_b_ada, v_norm_g, v_hgrn_onorm_g, flat_relb(v_rel_bias), v_hgrn_lb, fg2(v_final_g)])
    loss = smalls[0][0, 0]

    def small(kind):
        s = smalls[1 + 6 * kind:7 + 6 * kind]
        return s[0], s[1], s[2], unflat_relb(s[3]), s[4], s[5].reshape(D)

    def leaves(ada, sm, w_in_, wa_, wb_, wo_):
        b_, n_, o_, r_, l_, f_ = sm
        return (ada[None], b_, n_, w_in_[None], o_, wa_[None], wb_[None], wo_[None], r_, l_, f_)

    return (loss, gx[None],
            *leaves(g_ada, small(0), g_in, g_a, g_b, g_o),
            *leaves(d_ada, small(1), d_in, d_a, d_b, d_o),
            *leaves(nm_ada, small(2), nm_in, nm_a, nm_b, nm_o),
            *leaves(nv_ada, small(3), nv_in, nv_a, nv_b, nv_o))
```

```python
import functools
import math

import numpy as np
import jax
import jax.numpy as jnp
from jax import lax
from jax.experimental import pallas as pl
from jax.experimental.pallas import tpu as pltpu

F32 = jnp.float32
BF16 = jnp.bfloat16
MXU_DTYPE = jnp.bfloat16
XCHG_DTYPE = jnp.bfloat16

N_DEV = 8
D = 1024
A_HEADS = 8
A_HD = 64
A_W = A_HEADS * A_HD
A_BLK = 128
PATTERNS = ((128, 1), (512, 4), (2048, 16))
N_BUCKETS = 32
MAX_DISTANCE = 2048
NEG = -1e30
G_HEADS = 8
G_DK = 128
G_W = G_HEADS * G_DK
IN_W = 8 * D
EPS = 1e-6
ADAM_LR = 0.001
ADAM_B1 = 0.9
ADAM_B2 = 0.999
ADAM_EPS = 1e-08
ADAM_WD = 0.01
ADAM_STEP = 10

G_CHUNK = 128
G_SUB = 8
G_HPS_FWD = 4
G_HPS_BWD = 4
G_RB = 16
VMEM_LIMIT = 56 * 1024 * 1024

NN = (((1,), (0,)), ((), ()))
NT = (((1,), (1,)), ((), ()))
TN = (((0,), (0,)), ((), ()))
MESH = pl.DeviceIdType.MESH


def _mm(a, b, dims=NN):
    return lax.dot_general(a.astype(MXU_DTYPE), b.astype(MXU_DTYPE), dims,
                           preferred_element_type=F32)


def _mm_exact(t, x):
    hi = x.astype(BF16)
    r = x - hi.astype(F32)
    mid = r.astype(BF16)
    lo = (r - mid.astype(F32)).astype(BF16)
    tb = t.astype(BF16)
    return sum(lax.dot_general(tb, p, NN, preferred_element_type=F32) for p in (hi, mid, lo))


def _split(x):
    hi = x.astype(BF16)
    return hi, (x - hi.astype(F32)).astype(BF16)


def _mm_split(a, b, dims):
    dot = lambda p, q: lax.dot_general(p, q, dims, preferred_element_type=F32)
    return dot(a[0], b[0]) + dot(a[0], b[1]) + dot(a[1], b[0])


def _sigmoid(x):
    return 0.5 * jnp.tanh(0.5 * x) + 0.5


def _params(sem=None):
    return pltpu.CompilerParams(dimension_semantics=sem, vmem_limit_bytes=VMEM_LIMIT)


def _all_gather(xs, name):
    n = len(xs)

    def body(*refs):
        ins, outs = refs[:n], refs[n:2 * n]
        send_sems, recv_sems, local_sems = refs[2 * n:]
        x, y, c = lax.axis_index("x"), lax.axis_index("y"), lax.axis_index("c")
        me, sibling = (x, y, c), (x, y, 1 - c)
        chips = [(1 - x, y), (x, 1 - y), (1 - x, 1 - y)]

        def slot(ref, dev):
            return ref.at[4 * dev[0] + 2 * dev[1] + dev[2]]

        def copy(a, k, block, to, src=None):
            return pltpu.make_async_remote_copy(
                src_ref=slot(outs[a], block) if src is None else src,
                dst_ref=slot(outs[a], block),
                send_sem=send_sems.at[a, k], recv_sem=recv_sems.at[a, k],
                device_id=to, device_id_type=MESH)

        mine, first, passed = [], [], []
        for a in range(n):
            cp = pltpu.make_async_copy(ins[a], slot(outs[a], me), local_sems.at[a])
            cp.start()
            mine.append(cp)
            first.append(copy(a, 0, me, sibling, src=ins[a]))
            for j, chip in enumerate(chips):
                first.append(copy(a, 1 + j, me, (*chip, c), src=ins[a]))
        for cp in first:
            cp.start()
        for j, chip in enumerate(chips):
            for a in range(n):
                copy(a, 1 + j, (*chip, c), me).wait_recv()
                cp = copy(a, 4 + j, (*chip, c), sibling)
                cp.start()
                passed.append(cp)
        for a in range(n):
            copy(a, 0, sibling, me).wait_recv()
            for j, chip in enumerate(chips):
                copy(a, 4 + j, (*chip, 1 - c), me).wait_recv()
        for cp in first + passed:
            cp.wait_send()
        for cp in mine:
            cp.wait()

    any_spec = pl.BlockSpec(memory_space=pl.ANY)
    return pl.pallas_call(
        body, name=name,
        out_shape=[jax.ShapeDtypeStruct((N_DEV,) + v.shape, v.dtype) for v in xs],
        in_specs=[any_spec] * n, out_specs=[any_spec] * n,
        scratch_shapes=[pltpu.SemaphoreType.DMA((n, 7)), pltpu.SemaphoreType.DMA((n, 7)),
                        pltpu.SemaphoreType.DMA((n,))],
    )(*xs)


def _all_to_all_copies(ins, outs, send_sems, recv_sems, local_sems, gather=False):
    n = len(ins)
    x, y, c = lax.axis_index("x"), lax.axis_index("y"), lax.axis_index("c")
    me = 4 * x + 2 * y + c
    peers = []
    for m in range(1, N_DEV):
        peers.append((1 - x if m & 4 else x, 1 - y if m & 2 else y, 1 - c if m & 1 else c))

    def chunk(a, j):
        return ins[a] if gather else ins[a].at[j]

    def copy(a, k, landing):
        peer = peers[k]
        pid = 4 * peer[0] + 2 * peer[1] + peer[2]
        return pltpu.make_async_remote_copy(
            src_ref=chunk(a, pid), dst_ref=outs[a].at[pid if landing else me],
            send_sem=send_sems.at[a, k], recv_sem=recv_sems.at[a, k],
            device_id=peer, device_id_type=MESH)

    def local(a):
        return pltpu.make_async_copy(chunk(a, me), outs[a].at[me], local_sems.at[a])

    def start():
        for a in range(n):
            local(a).start()
        for k in range(N_DEV - 1):
            for a in range(n):
                copy(a, k, False).start()

    def wait():
        for k in range(N_DEV - 1):
            for a in range(n):
                copy(a, k, True).wait_recv()
        for k in range(N_DEV - 1):
            for a in range(n):
                copy(a, k, False).wait_send()
        for a in range(n):
            local(a).wait()

    return start, wait


def _mod_fwd(c64, w_ada, b_loc):
    def body(c_ref, w_ref, b_ref, o_ref):
        cv = c_ref[...]
        sc = cv * _sigmoid(cv)
        o_ref[...] = _mm(sc, w_ref[...]) + b_ref[...]

    return pl.pallas_call(
        body, name="mod_fwd",
        out_shape=jax.ShapeDtypeStruct((c64.shape[0], w_ada.shape[1]), F32),
        compiler_params=_params(),
    )(c64, w_ada, b_loc)


def _inproj(x, mod3, norm_g, w_in_g, blocks, tm=256):
    S = x.shape[0]
    ni = S // tm
    nb = len(blocks)

    def body(*refs):
        x_ref, mod_ref, g_ref, w_ref = refs[:4]
        b_ins = refs[4:4 + nb]
        proj_ref, ht_ref = refs[4 + nb:6 + nb]
        b_outs = refs[6 + nb:6 + 2 * nb]
        w_all, send_sems, recv_sems, local_sems = refs[6 + 2 * nb:]
        i = pl.program_id(0)
        start, wait = _all_to_all_copies(b_ins, b_outs, send_sems, recv_sems, local_sems, gather=True)

        @pl.when(i == 0)
        def _():
            start()
            pltpu.sync_copy(w_ref, w_all)

        xv = x_ref[...]
        r = lax.rsqrt(jnp.mean(xv * xv, axis=-1, keepdims=True) + EPS)
        h = ((xv * r * g_ref[...]) * (1.0 + mod_ref[1:2, :]) + mod_ref[0:1, :]).astype(MXU_DTYPE)
        ht_ref[...] = h.T
        for j in range(N_DEV):
            proj_ref[:, j * D:(j + 1) * D] = _mm(h, w_all[j])

        @pl.when(i == ni - 1)
        def _():
            wait()

    any_spec = pl.BlockSpec(memory_space=pl.ANY)
    return pl.pallas_call(
        body, name="inproj_gather",
        grid=(ni,),
        in_specs=[pl.BlockSpec((tm, D), lambda i: (i, 0)),
                  pl.BlockSpec((8, D), lambda i: (0, 0)),
                  pl.BlockSpec((1, D), lambda i: (0, 0)),
                  any_spec] + [any_spec] * nb,
        out_specs=[pl.BlockSpec((tm, IN_W), lambda i: (i, 0)),
                   pl.BlockSpec((D, tm), lambda i: (0, i))] + [any_spec] * nb,
        out_shape=[jax.ShapeDtypeStruct((S, IN_W), F32), jax.ShapeDtypeStruct((D, S), MXU_DTYPE)]
                  + [jax.ShapeDtypeStruct((N_DEV,) + b.shape, b.dtype) for b in blocks],
        scratch_shapes=[pltpu.VMEM(w_in_g.shape, w_in_g.dtype),
                        pltpu.SemaphoreType.DMA((nb, N_DEV - 1)), pltpu.SemaphoreType.DMA((nb, N_DEV - 1)),
                        pltpu.SemaphoreType.DMA((nb,))],
        compiler_params=_params(("arbitrary",)),
    )(x, mod3, norm_g, w_in_g, *blocks)


def _bucket_tables():
    qi = np.arange(A_BLK)[:, None]
    kj = np.arange(2 * A_BLK)[None, :]
    delta = qi + A_BLK - kj
    out = []
    for window, dil in PATTERNS:
        span = window // dil
        band = (delta >= 0) & (delta <= span)
        dist = np.clip(delta, 0, None) * dil
        max_exact = N_BUCKETS // 2
        nf = dist.astype(np.float32)
        large = max_exact + (np.log(np.maximum(nf, np.float32(1.0)) / np.float32(max_exact))
                             / np.float32(math.log(MAX_DISTANCE / max_exact))
                             * np.float32(N_BUCKETS - max_exact)).astype(np.int32)
        large = np.minimum(large, N_BUCKETS - 1)
        bucket = np.where(dist < max_exact, dist, large)
        out.append(np.where(band, bucket, -1).astype(np.int32))
    return np.stack(out)


def _bias_tables(rel_bias, buckets):
    def body(rb_ref, bk_ref, o_ref):
        h = pl.program_id(1)
        bk = bk_ref[0]
        acc = jnp.full(bk.shape, NEG, F32)
        for b in range(N_BUCKETS):
            acc = jnp.where(bk == b, rb_ref[b, h], acc)
        o_ref[0, 0] = acc

    return pl.pallas_call(
        body, name="bias_tables",
        grid=(3, A_HEADS),
        in_specs=[pl.BlockSpec(memory_space=pltpu.SMEM),
                  pl.BlockSpec((1, A_BLK, 2 * A_BLK), lambda p, h: (p, 0, 0))],
        out_specs=pl.BlockSpec((1, 1, A_BLK, 2 * A_BLK), lambda p, h: (p, h, 0, 0)),
        out_shape=jax.ShapeDtypeStruct((3, A_HEADS, A_BLK, 2 * A_BLK), F32),
        compiler_params=_params(("arbitrary", "arbitrary")),
    )(rel_bias, buckets)


A_TILES = 16


def _attn_heads_per_step(d):
    return A_HEADS if d == 1 else 2


def _attn_in_specs(sb, nsb, hw):
    w = A_HD * hw
    per = A_W // w

    def cur(col):
        return pl.BlockSpec((sb, w), lambda hp, n: (jnp.minimum(n, nsb - 1), per * col + hp))

    def prev(col):
        return pl.BlockSpec((sb, w), lambda hp, n: (jnp.maximum(jnp.minimum(n, nsb - 1) - 1, 0), per * col + hp))

    return [cur(0), prev(1), cur(1), prev(2), cur(2)]


def _rows(r, d):
    return pl.ds(r, A_BLK) if d == 1 else pl.ds(r, A_BLK, stride=d)


def _for_residues(d, hw, fn):
    unroll = min(d, max(1, A_TILES // hw))
    if d == unroll:
        _round_robin([g for r in range(d) for g in fn(r)])
    else:
        def group(g, c):
            _round_robin([t for u in range(unroll) for t in fn(g * unroll + u)])
            return c
        lax.fori_loop(0, d // unroll, group, 0)


def _attn_stack(t):
    first_half = lax.broadcasted_iota(jnp.int32, (1, 2 * A_HD), 1) < A_HD
    return jnp.concatenate([jnp.where(first_half, t, 0.0), jnp.where(first_half, 0.0, t)], axis=0)


def _attn_unstack(t2):
    first_half = lax.broadcasted_iota(jnp.int32, (1, 2 * A_HD), 1) < A_HD
    return jnp.where(first_half, t2[:A_BLK], t2[A_BLK:])


def _attn_scores(q, k, b_ref, pp, first):
    bias = jnp.concatenate([b_ref[2 * pp] + first, b_ref[2 * pp + 1] + first], axis=0)
    return _mm(_attn_stack(q), k, NT) * (A_HD ** -0.5) + bias


def _attn_fwd(proj, bias_p, d, name):
    S = proj.shape[0]
    sb = A_BLK * d
    nsb = S // sb
    hw = _attn_heads_per_step(d)

    def body(q_ref, kp_ref, kc_ref, vp_ref, vc_ref, b_ref, o_ref, l_ref):
        n = pl.program_id(1)
        kj = lax.broadcasted_iota(jnp.int32, (A_BLK, 2 * A_BLK), 1)
        first = jnp.where((n == 0) & (kj < A_BLK), NEG, 0.0).astype(F32)

        def residue(r):
            rows = _rows(r, d)

            def pair(pp):
                lanes = pl.ds(2 * A_HD * pp, 2 * A_HD)
                k = jnp.concatenate([kp_ref[rows, lanes], kc_ref[rows, lanes]], axis=0)
                v = jnp.concatenate([vp_ref[rows, lanes], vc_ref[rows, lanes]], axis=0)
                s = _attn_scores(q_ref[rows, lanes], k, b_ref, pp, first)
                yield
                m = jnp.max(s, axis=-1, keepdims=True)
                p = jnp.exp(s - m)
                den = jnp.sum(p, axis=-1, keepdims=True)
                pv = _mm(p, v)
                yield
                o_ref[rows, lanes] = _attn_unstack(pv / den)
                l_ref[rows, lanes] = _attn_unstack(jnp.broadcast_to(m + jnp.log(den), (2 * A_BLK, 2 * A_HD)))

            return [pair(pp) for pp in range(hw // 2)]

        _for_residues(d, hw, residue)

    out = pl.BlockSpec((sb, A_HD * hw), lambda hp, n: (n, hp))
    return pl.pallas_call(
        body, name=name,
        grid=(A_HEADS // hw, nsb),
        in_specs=_attn_in_specs(sb, nsb, hw) + [pl.BlockSpec((hw, A_BLK, 2 * A_BLK), lambda hp, n: (hp, 0, 0))],
        out_specs=[out, out],
        out_shape=[jax.ShapeDtypeStruct((S, A_W), F32)] * 2,
        compiler_params=_params(("parallel", "parallel")),
    )(proj, proj, proj, proj, proj, bias_p)


def _attn_combine(os, ls, proj, tm=512):
    S = proj.shape[0]

    def body(o1, o2, o3, l1, l2, l3, z_ref, ao_ref, lt_ref, oa_ref):
        a1, a2, a3 = l1[...], l2[...], l3[...]
        m = jnp.maximum(jnp.maximum(a1, a2), a3)
        e1, e2, e3 = jnp.exp(a1 - m), jnp.exp(a2 - m), jnp.exp(a3 - m)
        den = e1 + e2 + e3
        ao = (e1 * o1[...] + e2 * o2[...] + e3 * o3[...]) / den
        z = z_ref[...]
        ao_ref[...] = ao
        lt_ref[...] = m + jnp.log(den)
        oa_ref[...] = (ao * (z * _sigmoid(z))).astype(MXU_DTYPE)

    spec = pl.BlockSpec((tm, A_W), lambda i: (i, 0))
    return pl.pallas_call(
        body, name="attn_combine",
        grid=(S // tm,),
        in_specs=[spec] * 6 + [pl.BlockSpec((tm, A_W), lambda i: (i, 3))],
        out_specs=[spec] * 3,
        out_shape=[jax.ShapeDtypeStruct((S, A_W), F32), jax.ShapeDtypeStruct((S, A_W), F32),
                   jax.ShapeDtypeStruct((S, A_W), MXU_DTYPE)],
        compiler_params=_params(("parallel",)),
    )(*os, *ls, proj)


def _attn_pre_bwd(doa, ao, proj, tm=512):
    S = proj.shape[0]

    def body(doa_ref, ao_ref, z_ref, do_ref, dz_ref, dl_ref):
        z = z_ref[...]
        sg = _sigmoid(z)
        g = doa_ref[...]
        ao_v = ao_ref[...]
        do = g * (z * sg)
        do_ref[...] = do
        dz_ref[...] = (g * ao_v * (sg * (1.0 + z * (1.0 - sg)))).astype(MXU_DTYPE)
        prod = do * ao_v
        for h in range(A_HEADS):
            sl = slice(A_HD * h, A_HD * (h + 1))
            dl_ref[:, sl] = jnp.broadcast_to(jnp.sum(prod[:, sl], axis=-1, keepdims=True), (tm, A_HD))

    spec = pl.BlockSpec((tm, A_W), lambda i: (i, 0))
    return pl.pallas_call(
        body, name="attn_pre_bwd",
        grid=(S // tm,),
        in_specs=[spec, spec, pl.BlockSpec((tm, A_W), lambda i: (i, 3))],
        out_specs=[spec] * 3,
        out_shape=[jax.ShapeDtypeStruct((S, A_W), F32), jax.ShapeDtypeStruct((S, A_W), MXU_DTYPE),
                   jax.ShapeDtypeStruct((S, A_W), F32)],
        compiler_params=_params(("parallel",)),
    )(doa, ao, proj)


def _attn_bwd(proj, do, lt, delta, bias_p, d, name):
    S = proj.shape[0]
    sb = A_BLK * d
    nsb = S // sb
    hw = _attn_heads_per_step(d)

    def body(q_ref, kp_ref, kc_ref, vp_ref, vc_ref, do_ref, lt_ref, dl_ref, b_ref,
             dq_ref, dk_ref, dv_ref, db_ref, ck, cv):
        n = pl.program_id(1)

        @pl.when(n == 0)
        def _():
            db_ref[...] = jnp.zeros_like(db_ref)
            ck[...] = jnp.zeros_like(ck)
            cv[...] = jnp.zeros_like(cv)

        @pl.when(n < nsb)
        def _():
            kj = lax.broadcasted_iota(jnp.int32, (A_BLK, 2 * A_BLK), 1)
            first = jnp.where((n == 0) & (kj < A_BLK), NEG, 0.0).astype(F32)

            def residue(r):
                rows = _rows(r, d)

                def pair(pp):
                    lanes = pl.ds(2 * A_HD * pp, 2 * A_HD)
                    lt_r, dl_r = lt_ref[rows, lanes], dl_ref[rows, lanes]
                    k = jnp.concatenate([kp_ref[rows, lanes], kc_ref[rows, lanes]], axis=0)
                    v = jnp.concatenate([vp_ref[rows, lanes], vc_ref[rows, lanes]], axis=0)
                    q2 = _attn_stack(q_ref[rows, lanes])
                    do2 = _attn_stack(do_ref[rows, lanes])
                    col = lambda t: jnp.concatenate([t[:, 0:1], t[:, A_HD:A_HD + 1]], axis=0)
                    s = _attn_scores(q_ref[rows, lanes], k, b_ref, pp, first)
                    dp = _mm(do2, v, NT)
                    yield
                    p = jnp.exp(s - col(lt_r))
                    ds = p * (dp - col(dl_r))
                    db_ref[2 * pp] += ds[:A_BLK]
                    db_ref[2 * pp + 1] += ds[A_BLK:]
                    dq = _mm(ds, k)
                    dk = _mm(ds, q2, TN) * (A_HD ** -0.5)
                    dv = _mm(p, do2, TN)
                    yield
                    dq_ref[rows, lanes] = _attn_unstack(dq) * (A_HD ** -0.5)
                    dk_ref[rows, lanes] = ck[rows, lanes] + dk[:A_BLK]
                    dv_ref[rows, lanes] = cv[rows, lanes] + dv[:A_BLK]
                    ck[rows, lanes] = dk[A_BLK:]
                    cv[rows, lanes] = dv[A_BLK:]

                return [pair(pp) for pp in range(hw // 2)]

            _for_residues(d, hw, residue)

        @pl.when(n == nsb)
        def _():
            dk_ref[...] = ck[...]
            dv_ref[...] = cv[...]

    w = A_HD * hw
    row = pl.BlockSpec((sb, w), lambda hp, n: (jnp.minimum(n, nsb - 1), hp))
    lag = pl.BlockSpec((sb, w), lambda hp, n: (jnp.maximum(n - 1, 0), hp))
    tab = pl.BlockSpec((hw, A_BLK, 2 * A_BLK), lambda hp, n: (hp, 0, 0))
    return pl.pallas_call(
        body, name=name,
        grid=(A_HEADS // hw, nsb + 1),
        in_specs=_attn_in_specs(sb, nsb, hw) + [row, row, row, tab],
        out_specs=[row, lag, lag, tab],
        out_shape=[jax.ShapeDtypeStruct((S, A_W), F32)] * 3
                  + [jax.ShapeDtypeStruct((A_HEADS, A_BLK, 2 * A_BLK), F32)],
        scratch_shapes=[pltpu.VMEM((sb, w), F32), pltpu.VMEM((sb, w), F32)],
        compiler_params=_params(("parallel", "arbitrary")),
    )(proj, proj, proj, proj, proj, do, lt, delta, bias_p)


def _attn_assemble(dqs, dks, dvs, dz, tm=512):
    S = dz.shape[0]

    def body(q1, q2, q3, k1, k2, k3, v1, v2, v3, z_ref, p0_ref, p1_ref):
        p0_ref[:, :A_W] = (q1[...] + q2[...] + q3[...]).astype(MXU_DTYPE)
        p0_ref[:, A_W:] = (k1[...] + k2[...] + k3[...]).astype(MXU_DTYPE)
        p1_ref[:, :A_W] = (v1[...] + v2[...] + v3[...]).astype(MXU_DTYPE)
        p1_ref[:, A_W:] = z_ref[...]

    spec = pl.BlockSpec((tm, A_W), lambda i: (i, 0))
    wide = pl.BlockSpec((tm, 2 * A_W), lambda i: (i, 0))
    return pl.pallas_call(
        body, name="attn_assemble",
        grid=(S // tm,),
        in_specs=[spec] * 10,
        out_specs=[wide, wide],
        out_shape=[jax.ShapeDtypeStruct((S, 2 * A_W), MXU_DTYPE)] * 2,
        compiler_params=_params(("parallel",)),
    )(*dqs, *dks, *dvs, dz)


def _rel_bias_grad(dbs, buckets):
    def body(d1, d2, d3, bk_ref, o_ref):
        row = lax.broadcasted_iota(jnp.int32, (A_HEADS, 128), 0)
        lane = lax.broadcasted_iota(jnp.int32, (A_HEADS, 128), 1)
        acc = jnp.zeros((A_HEADS, 128), F32)
        for p, dref in enumerate((d1, d2, d3)):
            bk = bk_ref[p]
            for h in range(A_HEADS):
                ds = dref[h]
                for b in range(N_BUCKETS):
                    s = jnp.sum(jnp.where(bk == b, ds, 0.0), keepdims=True)
                    acc = acc + jnp.where((row == h) & (lane == b), s, 0.0)
        o_ref[...] = acc

    return pl.pallas_call(
        body, name="rel_bias_grad",
        out_shape=jax.ShapeDtypeStruct((A_HEADS, 128), F32),
        compiler_params=_params(),
    )(*dbs, buckets)


def _tri(c):
    t = np.tril(np.ones((c, c), np.float32))
    return jnp.asarray(t), jnp.asarray(t.T.copy())


def _fill_above(ref, x, pad):
    ref[0:G_SUB, :] = jnp.full((G_SUB, x.shape[1]), pad, F32)
    ref[G_SUB:, :] = x


def _fill_below(ref, x, pad):
    ref[0:x.shape[0], :] = x
    ref[x.shape[0]:, :] = jnp.full((G_SUB, x.shape[1]), pad, F32)


def _hgrn_gates(q_ref, f_ref, lbp_ref, tri_ref):
    qraw = q_ref[...]
    sq = _sigmoid(qraw)
    q = qraw * sq
    sg = _sigmoid(f_ref[...])
    lb = _sigmoid(lbp_ref[0:1, :] - lbp_ref[1:2, :])
    f = lb + (1.0 - lb) * sg
    k = 1.0 - f
    b = _mm_exact(tri_ref[...], jnp.log(f))
    return qraw, sq, q, sg, lb, f, k, b


def _hgrn_col(C, base, idx, hps):
    return pl.BlockSpec((C, hps * G_DK), lambda h, n: (idx(n), base * (G_HEADS // hps) + h))


def _round_robin(stages):
    live = list(stages)
    while live:
        nxt = []
        for g in live:
            try:
                next(g)
                nxt.append(g)
            except StopIteration:
                pass
        live = nxt


def _hgrn_levels(C):
    out, m = [], G_SUB
    while 2 * m <= C:
        out.append(m)
        m *= 2
    return out


def _hgrn_level_masks(C):
    ti = np.arange(C)[:, None]
    si = np.arange(C)[None, :]
    return jnp.asarray(np.stack([((ti // (2 * m) == si // (2 * m)) & (ti - si >= G_SUB)).astype(np.float32)
                                 for m in _hgrn_levels(C)]))


def _hgrn_level(b, q, k, C, m):
    zeros = jnp.zeros((m, G_DK), F32)
    eq, ek, qt, kt = [], [], [], []
    for blk in range(0, C // m, 2):
        lo, mid, hi = blk * m, (blk + 1) * m, (blk + 2) * m
        ref = b[mid:mid + 1]
        e_right = jnp.exp(b[mid:hi] - ref)
        e_left = jnp.exp(ref - b[lo:mid])
        eq += [zeros, e_right]
        ek += [e_left, zeros]
        qt += [zeros, q[mid:hi] * e_right]
        kt += [k[lo:mid] * e_left, zeros]
    cat = lambda parts: jnp.concatenate(parts, axis=0)
    return cat(qt), cat(kt), cat(eq), cat(ek)


def _hgrn_fwd(proj, hgrn_lb, onorm_g, C=G_CHUNK):
    S = proj.shape[0]
    nc = S // C
    tri, _ = _tri(C)
    masks = _hgrn_level_masks(C)
    hps = G_HPS_FWD

    def body(q_ref, f_ref, i_ref, z_ref, lbp_ref, go_ref, tri_ref, pm_ref, o_ref, ob_ref, st_ref, St, kp, vp, fp):
        @pl.when(pl.program_id(1) == 0)
        def _():
            St[...] = jnp.zeros_like(St)

        heads = []
        for hh in range(hps):
            ln = pl.ds(G_DK * hh, G_DK)
            heads.append(head(
                q_ref.at[:, ln], f_ref.at[:, ln], i_ref.at[:, ln], z_ref.at[:, ln], lbp_ref.at[:, ln], go_ref,
                tri_ref, pm_ref, o_ref.at[:, ln], ob_ref.at[:, ln], st_ref.at[0, hh], St.at[hh], kp.at[hh], vp.at[hh],
                fp.at[hh]))
        _round_robin(heads)

    def head(q_ref, f_ref, i_ref, z_ref, lbp_ref, go_ref, tri_ref, pm_ref, o_ref, ob_ref, st_ref, St, kp, vp, fp):
        _, _, q, _, _, f, k, b = _hgrn_gates(q_ref, f_ref, lbp_ref, tri_ref)
        v = i_ref[...]
        bC = b[C - 1:C, :]
        S0 = St[...]
        o = _mm(q * jnp.exp(b), S0, NT)
        yield
        _fill_above(kp, k, 0.0)
        _fill_above(vp, v, 0.0)
        _fill_above(fp, f, 1.0)
        near = []
        for r0 in range(0, C, G_RB):
            qb = q[r0:r0 + G_RB]
            acc = e = None
            for l in range(G_SUB):
                rows = pl.ds(G_SUB - l + r0, G_RB)
                if l > 0:
                    fl = fp[pl.ds(G_SUB - l + 1 + r0, G_RB), :]
                    e = fl if e is None else e * fl
                kl = kp[rows, :]
                a = jnp.sum(qb * kl if e is None else qb * kl * e, axis=-1, keepdims=True)
                t = a * vp[rows, :]
                acc = t if acc is None else acc + t
            near.append(acc)
        o = o + jnp.concatenate(near, axis=0)
        yield
        a_off = jnp.zeros((C, C), F32)
        for lv, m in enumerate(_hgrn_levels(C)):
            qt, kt, _, _ = _hgrn_level(b, q, k, C, m)
            prod = _mm_split(_split(qt), _split(kt), NT) if m == G_SUB else _mm(qt, kt, NT)
            a_off = a_off + pm_ref[lv] * prod
        yield
        o = o + _mm(a_off, v)
        S1 = S0 * jnp.exp(bC) + _mm(v, k * jnp.exp(bC - b), TN)
        St[...] = S1
        st_ref[...] = S1
        o_ref[...] = o
        r = lax.rsqrt(jnp.mean(o * o, axis=-1, keepdims=True) + EPS)
        z = z_ref[...]
        ob_ref[...] = (o * r * go_ref[...] * (z * _sigmoid(z))).astype(MXU_DTYPE)

    ident = lambda n: n
    w = hps * G_DK
    out = pl.BlockSpec((C, w), lambda h, n: (n, h))
    return pl.pallas_call(
        body, name="hgrn_fwd",
        grid=(G_HEADS // hps, nc),
        in_specs=[_hgrn_col(C, base, ident, hps) for base in (2, 3, 4, 5)] + [
                  pl.BlockSpec((2, w), lambda h, n: (0, h)),
                  pl.BlockSpec((1, G_DK), lambda h, n: (0, 0)),
                  pl.BlockSpec((C, C), lambda h, n: (0, 0)),
                  pl.BlockSpec(masks.shape, lambda h, n: (0, 0, 0))],
        out_specs=[out, out, pl.BlockSpec((1, hps, G_DK, G_DK), lambda h, n: (n, h, 0, 0))],
        out_shape=[jax.ShapeDtypeStruct((S, G_W), F32), jax.ShapeDtypeStruct((S, G_W), MXU_DTYPE),
                   jax.ShapeDtypeStruct((nc, G_HEADS, G_DK, G_DK), F32)],
        scratch_shapes=[pltpu.VMEM((hps, G_DK, G_DK), F32)] + [pltpu.VMEM((hps, C + G_SUB, G_DK), F32)] * 3,
        compiler_params=_params(("parallel", "arbitrary")),
    )(proj, proj, proj, proj, hgrn_lb, onorm_g, tri, masks)


def _hgrn_bwd(proj, o_raw, dob, states, hgrn_lb, onorm_g, C=G_CHUNK):
    S = proj.shape[0]
    nc = S // C
    tri, triu = _tri(C)
    masks = _hgrn_level_masks(C)
    hps = G_HPS_BWD

    def body(q_ref, f_ref, i_ref, z_ref, o_ref, dob_ref, s0_ref, s1_ref, lbp_ref, go_ref, tri_ref, triu_ref,
             pm_ref, dq_ref, df_ref, di_ref, dz_ref, dlb_ref, dgo_ref, dSt, *shifted):
        @pl.when(pl.program_id(1) == 0)
        def _():
            dSt[...] = jnp.zeros_like(dSt)
            dlb_ref[...] = jnp.zeros_like(dlb_ref)
            dgo_ref[...] = jnp.zeros_like(dgo_ref)

        heads = []
        for hh in range(hps):
            ln = pl.ds(G_DK * hh, G_DK)
            heads.append(head(
                q_ref.at[:, ln], f_ref.at[:, ln], i_ref.at[:, ln], z_ref.at[:, ln], o_ref.at[:, ln],
                dob_ref.at[:, ln], s0_ref.at[0, hh], s1_ref.at[0, hh], lbp_ref.at[:, ln], go_ref, tri_ref, triu_ref,
                pm_ref, dq_ref.at[:, ln], df_ref.at[:, ln], di_ref.at[:, ln], dz_ref.at[:, ln], dlb_ref.at[:, ln],
                dgo_ref.at[pl.ds(8 * hh, 8), :], dSt.at[hh], *[t.at[hh] for t in shifted]))
        _round_robin(heads)

    def head(q_ref, f_ref, i_ref, z_ref, o_ref, dob_ref, s0_ref, s1_ref, lbp_ref, go_ref, tri_ref, triu_ref,
             pm_ref, dq_ref, df_ref, di_ref, dz_ref, dlb_ref, dgo_ref, dSt, kp, vp, fp, qn, dn_, fn, xs, dac):
        cn = nc - 1 - pl.program_id(1)
        qraw, sq, q, sg, lb, f, k, b = _hgrn_gates(q_ref, f_ref, lbp_ref, tri_ref)
        v = i_ref[...]
        bC = b[C - 1:C, :]
        eb = jnp.exp(b)
        ecb = jnp.exp(bC - b)
        o = o_ref[...]
        z = z_ref[...]
        sz = _sigmoid(z)
        go = go_ref[...]
        g_ob = dob_ref[...]
        r = lax.rsqrt(jnp.mean(o * o, axis=-1, keepdims=True) + EPS)
        nh = o * r
        dnrm = g_ob * (z * sz)
        dz_ref[...] = (g_ob * (nh * go) * (sz * (1.0 + z * (1.0 - sz)))).astype(MXU_DTYPE)
        dgo_ref[0:1, :] += jnp.sum(dnrm * nh, axis=0, keepdims=True)
        dn = dnrm * go
        do = r * (dn - nh * jnp.mean(dn * nh, axis=-1, keepdims=True))

        yield
        S0 = jnp.where(cn == 0, 0.0, s0_ref[...])
        S1 = s1_ref[...]
        dS1 = dSt[...]
        dq = eb * _mm(do, S0)
        dk = ecb * _mm(v, dS1)
        dv = _mm(k * ecb, dS1, NT)
        bnd = jnp.sum(dS1 * S1, axis=0, keepdims=True)
        dSt[...] = dS1 * jnp.exp(bC) + _mm(do, q * eb, TN)

        _fill_above(kp, k, 0.0)
        _fill_above(vp, v, 0.0)
        _fill_above(fp, f, 1.0)
        _fill_below(qn, q, 0.0)
        _fill_below(dn_, do, 0.0)
        _fill_below(fn, f, 1.0)
        yield
        for r0 in range(0, C, G_RB):
            do_b = do[r0:r0 + G_RB]
            for l in range(G_SUB):
                xs[pl.ds(l * C + r0, G_RB), :] = (do_b * vp[pl.ds(G_SUB - l + r0, G_RB), :]).astype(MXU_DTYPE)
        dac[0:G_SUB * C, :] = _mm(xs[...], jnp.ones((G_DK, G_DK), MXU_DTYPE))
        dac[G_SUB * C:, :] = jnp.zeros((G_SUB, G_DK), F32)
        yield
        near_q, near_k, near_v = [], [], []
        for r0 in range(0, C, G_RB):
            k_b = k[r0:r0 + G_RB]
            aq = ak = av = e = e2 = None
            for l in range(G_SUB):
                down, up = pl.ds(G_SUB - l + r0, G_RB), pl.ds(l + r0, G_RB)
                if l > 0:
                    fl = fp[pl.ds(G_SUB - l + 1 + r0, G_RB), :]
                    e = fl if e is None else e * fl
                    fu = fn[up, :]
                    e2 = fu if e2 is None else e2 * fu
                kl = kp[down, :]
                t = dac[pl.ds(l * C + r0, G_RB), :] * (kl if e is None else kl * e)
                aq = t if aq is None else aq + t
                qu = qn[up, :]
                qe = qu if e2 is None else qu * e2
                dou = dn_[up, :]
                a2 = jnp.sum(qe * k_b, axis=-1, keepdims=True)
                t = dac[pl.ds(l * C + l + r0, G_RB), :] * qe
                ak = t if ak is None else ak + t
                t = a2 * dou
                av = t if av is None else av + t
            near_q.append(aq)
            near_k.append(ak)
            near_v.append(av)
        dq = dq + jnp.concatenate(near_q, axis=0)
        dk = dk + jnp.concatenate(near_k, axis=0)
        dv = dv + jnp.concatenate(near_v, axis=0)

        yield
        da_all = _mm(do, v, NT)
        a_off = jnp.zeros((C, C), F32)
        for lv, m in enumerate(_hgrn_levels(C)):
            qt, kt, eq, ek = _hgrn_level(b, q, k, C, m)
            da_m = pm_ref[lv] * da_all
            if m == G_SUB:
                qs, ks, das = _split(qt), _split(kt), _split(da_m)
                a_off = a_off + pm_ref[lv] * _mm_split(qs, ks, NT)
                dq = dq + _mm_split(das, ks, NN) * eq
                dk = dk + _mm_split(das, qs, TN) * ek
            else:
                a_off = a_off + pm_ref[lv] * _mm(qt, kt, NT)
                dq = dq + _mm(da_m, kt) * eq
                dk = dk + _mm(da_m, qt, TN) * ek
        dv = dv + _mm(a_off, do, TN)

        yield
        row = lax.broadcasted_iota(jnp.int32, (C, 1), 0)
        db = q * dq - k * dk + jnp.where(row == C - 1, bnd, 0.0)
        dg = _mm_exact(triu_ref[...], db)
        df = dg / f - dk
        df_ref[...] = (df * (1.0 - lb) * (sg * (1.0 - sg))).astype(MXU_DTYPE)
        dlb_ref[0:1, :] += jnp.sum(df * (1.0 - sg), axis=0, keepdims=True)
        dq_ref[...] = (dq * (sq * (1.0 + qraw * (1.0 - sq)))).astype(MXU_DTYPE)
        di_ref[...] = dv.astype(MXU_DTYPE)

    rev = lambda n: nc - 1 - n
    w = hps * G_DK
    blk = pl.BlockSpec((C, w), lambda h, n: (nc - 1 - n, h))
    return pl.pallas_call(
        body, name="hgrn_bwd",
        grid=(G_HEADS // hps, nc),
        in_specs=[_hgrn_col(C, base, rev, hps) for base in (2, 3, 4, 5)] + [
                  blk, blk,
                  pl.BlockSpec((1, hps, G_DK, G_DK), lambda h, n: (jnp.maximum(nc - 2 - n, 0), h, 0, 0)),
                  pl.BlockSpec((1, hps, G_DK, G_DK), lambda h, n: (nc - 1 - n, h, 0, 0)),
                  pl.BlockSpec((2, w), lambda h, n: (0, h)),
                  pl.BlockSpec((1, G_DK), lambda h, n: (0, 0)),
                  pl.BlockSpec((C, C), lambda h, n: (0, 0)),
                  pl.BlockSpec((C, C), lambda h, n: (0, 0)),
                  pl.BlockSpec(masks.shape, lambda h, n: (0, 0, 0))],
        out_specs=[blk, blk, blk, blk,
                   pl.BlockSpec((8, w), lambda h, n: (0, h)),
                   pl.BlockSpec((8 * hps, G_DK), lambda h, n: (h, 0))],
        out_shape=[jax.ShapeDtypeStruct((S, G_W), MXU_DTYPE)] * 4
                  + [jax.ShapeDtypeStruct((8, G_W), F32), jax.ShapeDtypeStruct((8 * G_HEADS, G_DK), F32)],
        scratch_shapes=[pltpu.VMEM((hps, G_DK, G_DK), F32)] + [pltpu.VMEM((hps, C + G_SUB, G_DK), F32)] * 6
                       + [pltpu.VMEM((hps, G_SUB * C, G_DK), MXU_DTYPE),
                          pltpu.VMEM((hps, G_SUB * C + G_SUB, G_DK), F32)],
        compiler_params=_params(("parallel", "arbitrary")),
    )(proj, proj, proj, proj, o_raw, dob, states, states, hgrn_lb, onorm_g, tri, triu, masks)


def _tail(x, target, oa, ob, proj, mod3, final_g, wa, wb, wo, tm=256):
    S = x.shape[0]
    nt = S // tm

    def body(x_ref, t_ref, oa_ref, ob_ref, ga_ref, gb_ref, mod_ref, fg_ref, wa_ref, wb_ref, wo_ref,
             dx2_ref, doa_ref, dob_ref, dga_ref, dgb_ref, sums_ref, gwa_ref, gwb_ref, gwo_ref,
             acc_a, acc_b, acc_o):
        i = pl.program_id(0)

        @pl.when(i == 0)
        def _():
            sums_ref[...] = jnp.zeros_like(sums_ref)
            acc_a[...] = jnp.zeros_like(acc_a)
            acc_b[...] = jnp.zeros_like(acc_b)
            acc_o[...] = jnp.zeros_like(acc_o)

        oa_v, ob_v = oa_ref[...], ob_ref[...]
        pa = _mm(oa_v, wa_ref[...])
        pb = _mm(ob_v, wb_ref[...])
        sa, sb = _sigmoid(ga_ref[...]), _sigmoid(gb_ref[...])
        ym = sa * pa + sb * pb
        u = _mm(ym, wo_ref[...])
        gate = mod_ref[2:3, :]
        fg = fg_ref[...]
        x2 = x_ref[...] + gate * u
        r2 = lax.rsqrt(jnp.mean(x2 * x2, axis=-1, keepdims=True) + EPS)
        xn2 = x2 * r2
        e = xn2 * fg - t_ref[...]
        dy = e * (1.0 / D)
        dn = dy * fg
        dx2 = r2 * (dn - xn2 * jnp.mean(dn * xn2, axis=-1, keepdims=True))
        dx2_ref[...] = dx2
        sums_ref[0:1, :] += jnp.sum(dy * xn2, axis=0, keepdims=True)
        sums_ref[1:2, :] += jnp.sum(dx2 * u, axis=0, keepdims=True)
        sums_ref[2:3, :] += (0.5 / D) * jnp.sum(e * e, axis=0, keepdims=True)
        du = dx2 * gate
        dym = _mm(du, wo_ref[...], NT)
        acc_o[...] += _mm(ym, du, TN)
        dpa, dpb = dym * sa, dym * sb
        dga_ref[...] = (dym * pa * (sa * (1.0 - sa))).astype(MXU_DTYPE)
        dgb_ref[...] = (dym * pb * (sb * (1.0 - sb))).astype(MXU_DTYPE)
        doa_ref[...] = _mm(dpa, wa_ref[...], NT)
        dob_ref[...] = _mm(dpb, wb_ref[...], NT)
        acc_a[...] += _mm(oa_v, dpa, TN)
        acc_b[...] += _mm(ob_v, dpb, TN)

        @pl.when(i == nt - 1)
        def _():
            pltpu.sync_copy(acc_a, gwa_ref)
            pltpu.sync_copy(acc_b, gwb_ref)
            pltpu.sync_copy(acc_o, gwo_ref)

    row = lambda w: pl.BlockSpec((tm, w), lambda i: (i, 0))
    full = lambda a, b: pl.BlockSpec((a, b), lambda i: (0, 0))
    any_spec = pl.BlockSpec(memory_space=pl.ANY)
    return pl.pallas_call(
        body, name="tail",
        grid=(nt,),
        in_specs=[row(D), row(D), row(A_W), row(D),
                  pl.BlockSpec((tm, D), lambda i: (i, 6)), pl.BlockSpec((tm, D), lambda i: (i, 7)),
                  full(8, D), full(1, D), full(A_W, D), full(D, D), full(D, D)],
        out_specs=[row(D), row(A_W), row(D), row(D), row(D), full(8, D), any_spec, any_spec, any_spec],
        out_shape=[jax.ShapeDtypeStruct((S, D), F32), jax.ShapeDtypeStruct((S, A_W), F32),
                   jax.ShapeDtypeStruct((S, D), F32), jax.ShapeDtypeStruct((S, D), MXU_DTYPE),
                   jax.ShapeDtypeStruct((S, D), MXU_DTYPE), jax.ShapeDtypeStruct((8, D), F32),
                   jax.ShapeDtypeStruct((A_W, D), F32), jax.ShapeDtypeStruct((D, D), F32),
                   jax.ShapeDtypeStruct((D, D), F32)],
        scratch_shapes=[pltpu.VMEM((A_W, D), F32), pltpu.VMEM((D, D), F32), pltpu.VMEM((D, D), F32)],
        compiler_params=_params(("arbitrary",)),
    )(x, target, oa, ob, proj, proj, mod3, final_g, wa, wb, wo)


def _dh(pieces, w_in_g, x, dx2, mod3, norm_g, grads, tm=256):
    S = x.shape[0]
    ni = S // tm
    ng = len(grads)

    def body(*refs):
        p_refs = refs[:N_DEV]
        w_ref, x_ref, dx2_ref, mod_ref, g_ref = refs[N_DEV:N_DEV + 5]
        g_ins = refs[N_DEV + 5:N_DEV + 5 + ng]
        gx_ref, sums_ref = refs[N_DEV + 5 + ng:N_DEV + 7 + ng]
        g_outs = refs[N_DEV + 7 + ng:N_DEV + 7 + 2 * ng]
        w_all, send_sems, recv_sems, local_sems = refs[N_DEV + 7 + 2 * ng:]
        i = pl.program_id(0)
        start, wait = _all_to_all_copies(g_ins, g_outs, send_sems, recv_sems, local_sems)

        @pl.when(i == 0)
        def _():
            start()
            sums_ref[...] = jnp.zeros_like(sums_ref)
            pltpu.sync_copy(w_ref, w_all)

        dh = _mm(p_refs[0][...], w_all[0], NT)
        for k in range(1, N_DEV):
            dh = dh + _mm(p_refs[k][...], w_all[k], NT)
        xv = x_ref[...]
        g = g_ref[...]
        sc1 = 1.0 + mod_ref[1:2, :]
        r = lax.rsqrt(jnp.mean(xv * xv, axis=-1, keepdims=True) + EPS)
        xn = xv * r
        sums_ref[0:1, :] += jnp.sum(dh, axis=0, keepdims=True)
        sums_ref[1:2, :] += jnp.sum(dh * (xn * g), axis=0, keepdims=True)
        sums_ref[2:3, :] += jnp.sum(dh * sc1 * xn, axis=0, keepdims=True)
        dxn = dh * sc1 * g
        gx_ref[...] = dx2_ref[...] + r * (dxn - xn * jnp.mean(dxn * xn, axis=-1, keepdims=True))

        @pl.when(i == ni - 1)
        def _():
            wait()

    row = pl.BlockSpec((tm, D), lambda i: (i, 0))
    any_spec = pl.BlockSpec(memory_space=pl.ANY)
    return pl.pallas_call(
        body, name="dh_scatter",
        grid=(ni,),
        in_specs=[row] * N_DEV
                 + [any_spec, row, row,
                    pl.BlockSpec((8, D), lambda i: (0, 0)),
                    pl.BlockSpec((1, D), lambda i: (0, 0))]
                 + [any_spec] * ng,
        out_specs=[row, pl.BlockSpec((8, D), lambda i: (0, 0))] + [any_spec] * ng,
        out_shape=[jax.ShapeDtypeStruct((S, D), F32), jax.ShapeDtypeStruct((8, D), F32)]
                  + [jax.ShapeDtypeStruct(g.shape, g.dtype) for g in grads],
        scratch_shapes=[pltpu.VMEM(w_in_g.shape, w_in_g.dtype),
                        pltpu.SemaphoreType.DMA((ng, N_DEV - 1)), pltpu.SemaphoreType.DMA((ng, N_DEV - 1)),
                        pltpu.SemaphoreType.DMA((ng,))],
        compiler_params=_params(("arbitrary",)),
    )(*pieces, w_in_g, x, dx2, mod3, norm_g, *grads)


def _gw_in(ht, pieces, grads, tm=512):
    S = ht.shape[1]
    nt = S // tm
    ng = len(grads)

    def body(*refs):
        h_ref, p_refs = refs[0], refs[1:1 + N_DEV]
        g_ins = refs[1 + N_DEV:1 + N_DEV + ng]
        o_ref = refs[1 + N_DEV + ng]
        g_outs = refs[2 + N_DEV + ng:2 + N_DEV + 2 * ng]
        acc, send_sems, recv_sems, local_sems = refs[2 + N_DEV + 2 * ng:]
        j, i = pl.program_id(0), pl.program_id(1)
        start, wait = _all_to_all_copies(g_ins, g_outs, send_sems, recv_sems, local_sems)

        @pl.when((j == 0) & (i == 0))
        def _():
            start()

        @pl.when(i == 0)
        def _():
            acc[...] = jnp.zeros_like(acc)

        for k in range(N_DEV):
            @pl.when(j == k)
            def _(k=k):
                acc[...] += _mm(h_ref[...], p_refs[k][...])

        @pl.when(i == nt - 1)
        def _():
            o_ref[0] = acc[...].astype(XCHG_DTYPE)

        @pl.when((j == N_DEV - 1) & (i == nt - 1))
        def _():
            wait()

    def piece(k):
        return pl.BlockSpec((tm, D), lambda j, i: (jnp.where(j == k, i, 0), 0))

    any_spec = pl.BlockSpec(memory_space=pl.ANY)
    return pl.pallas_call(
        body, name="gw_in_scatter",
        grid=(N_DEV, nt),
        in_specs=[pl.BlockSpec((D, tm), lambda j, i: (0, i))] + [piece(k) for k in range(N_DEV)] + [any_spec] * ng,
        out_specs=[pl.BlockSpec((1, D, D), lambda j, i: (j, 0, 0))] + [any_spec] * ng,
        out_shape=[jax.ShapeDtypeStruct((N_DEV, D, D), XCHG_DTYPE)]
                  + [jax.ShapeDtypeStruct(g.shape, g.dtype) for g in grads],
        scratch_shapes=[pltpu.VMEM((D, D), F32),
                        pltpu.SemaphoreType.DMA((ng, N_DEV - 1)), pltpu.SemaphoreType.DMA((ng, N_DEV - 1)),
                        pltpu.SemaphoreType.DMA((ng,))],
        compiler_params=_params(("arbitrary", "arbitrary")),
    )(ht, *pieces, *grads)


def _adamw_math(w, g, m, v):
    m = ADAM_B1 * m + (1.0 - ADAM_B1) * g
    v = ADAM_B2 * v + (1.0 - ADAM_B2) * (g * g)
    m_hat = m / (1.0 - ADAM_B1 ** ADAM_STEP)
    v_hat = v / (1.0 - ADAM_B2 ** ADAM_STEP)
    delta = -ADAM_LR * (m_hat / (jnp.sqrt(v_hat) + ADAM_EPS) + ADAM_WD * w)
    return delta, m, v


def _adamw_big(recv, w, m, v, name, tr=128):
    M, N = w.shape
    tr = min(tr, M)

    def body(r_ref, w_ref, m_ref, v_ref, g_ref, d_ref, nm_ref, nv_ref):
        g = r_ref[0].astype(F32)
        for j in range(1, N_DEV):
            g = g + r_ref[j].astype(F32)
        g_ref[...] = g
        d_ref[...], nm_ref[...], nv_ref[...] = _adamw_math(w_ref[...], g, m_ref[...], v_ref[...])

    blk = pl.BlockSpec((tr, N), lambda i: (i, 0))
    return pl.pallas_call(
        body, name=name,
        grid=(M // tr,),
        in_specs=[pl.BlockSpec((N_DEV, tr, N), lambda i: (0, i, 0)), blk, blk, blk],
        out_specs=[blk] * 4,
        out_shape=[jax.ShapeDtypeStruct((M, N), F32)] * 4,
        compiler_params=_params(("parallel",)),
    )(recv, w, m, v)


def _adamw_w_ada(c64, dmod64, w, m, v):
    def body(c_ref, dm_ref, w_ref, m_ref, v_ref, g_ref, d_ref, nm_ref, nv_ref):
        cv = c_ref[...]
        g = _mm(cv * _sigmoid(cv), dm_ref[...], TN)
        g_ref[...] = g
        d_ref[...], nm_ref[...], nv_ref[...] = _adamw_math(w_ref[...], g, m_ref[...], v_ref[...])

    return pl.pallas_call(
        body, name="adamw_w_ada",
        out_shape=[jax.ShapeDtypeStruct(w.shape, F32)] * 4,
        compiler_params=_params(),
    )(c64, dmod64, w, m, v)


P_MOD, P_NORM, P_ONORM, P_RELB, P_LB, P_FINAL, P_LOSS, P_END = (0, 3 * D, 4 * D, 5 * D, 6 * D, 7 * D, 8 * D, 9 * D)


def _adamw_small(packed, b_ada, norm_g, onorm_g, relb, hgrn_lb, final_g, ms, vs):
    def body(pk_ref, b_ref, ng_ref, og_ref, rb_ref, lb_ref, fg_ref,
             mb, mn, mo, mr, ml, mf, vb, vn, vo, vr, vl, vf,
             loss_ref, gb, gn, go, gr, gl, gf, db, dn, do, dr, dl, df,
             nmb, nmn, nmo, nmr, nml, nmf, nvb, nvn, nvo, nvr, nvl, nvf):
        tot = pk_ref[0:1, :]
        for j in range(1, N_DEV):
            tot = tot + pk_ref[8 * j:8 * j + 1, :]
        loss_ref[...] = jnp.broadcast_to(jnp.sum(tot[:, P_LOSS:P_END], axis=-1, keepdims=True), (8, 128))

        def upd(g, w_ref, m_ref, v_ref, g_out, d_out, m_out, v_out):
            g_out[...] = g
            d_out[...], m_out[...], v_out[...] = _adamw_math(w_ref[...], g, m_ref[...], v_ref[...])

        upd(tot[:, P_MOD:P_NORM], b_ref, mb, vb, gb, db, nmb, nvb)
        upd(tot[:, P_NORM:P_ONORM], ng_ref, mn, vn, gn, dn, nmn, nvn)
        g_on = tot[:, P_ONORM:P_ONORM + G_DK]
        for h in range(1, G_HEADS):
            g_on = g_on + tot[:, P_ONORM + G_DK * h:P_ONORM + G_DK * (h + 1)]
        upd(g_on, og_ref, mo, vo, go, do, nmo, nvo)
        upd(tot[:, P_RELB:P_LB], rb_ref, mr, vr, gr, dr, nmr, nvr)
        a = lb_ref[...]
        lb = _sigmoid(a[0:1, :] - a[1:2, :])
        g0 = tot[:, P_LB:P_FINAL] * lb * (1.0 - lb)
        row = lax.broadcasted_iota(jnp.int32, (2, D), 0)
        upd(jnp.where(row == 0, g0, -g0), lb_ref, ml, vl, gl, dl, nml, nvl)
        upd(tot[:, P_FINAL:P_LOSS], fg_ref, mf, vf, gf, df, nmf, nvf)

    shapes = [b_ada.shape, norm_g.shape, onorm_g.shape, relb.shape, hgrn_lb.shape, final_g.shape]
    outs = [jax.ShapeDtypeStruct((8, 128), F32)] + [jax.ShapeDtypeStruct(s, F32) for s in shapes] * 4
    return pl.pallas_call(
        body, name="adamw_small",
        out_shape=outs,
        compiler_params=_params(),
    )(packed, b_ada, norm_g, onorm_g, relb, hgrn_lb, final_g, *ms, *vs)


def _local_step(x, target, mod3, norm_g, w_in_g, onorm_g, wa_blk, wb_blk, wo_blk, rel_bias, hgrn_lb, final_g):
    buckets = jnp.asarray(_bucket_tables())
    bias = _bias_tables(rel_bias, buckets)
    proj, ht, wa_g, wb_g, wo_g = _inproj(x, mod3, norm_g, w_in_g, [wa_blk, wb_blk, wo_blk])
    wa = wa_g.transpose(1, 0, 2).reshape(A_W, D)
    wb = wb_g.reshape(D, D)
    wo = wo_g.reshape(D, D)
    os, ls = [], []
    for p, (_, d) in enumerate(PATTERNS):
        o, l = _attn_fwd(proj, bias[p], d, "attn_fwd_d%d" % d)
        os.append(o)
        ls.append(l)
    ao, lt, oa = _attn_combine(os, ls, proj)
    o_raw, ob, states = _hgrn_fwd(proj, hgrn_lb, onorm_g)
    dx2, doa, dob, dga, dgb, tsums, gwa, gwb, gwo = _tail(x, target, oa, ob, proj, mod3, final_g, wa, wb, wo)
    do, dza, delta = _attn_pre_bwd(doa, ao, proj)
    dqs, dks, dvs, dbs = [], [], [], []
    for p, (_, d) in enumerate(PATTERNS):
        dq, dk, dv, db = _attn_bwd(proj, do, lt, delta, bias[p], d, "attn_bwd_d%d" % d)
        dqs.append(dq)
        dks.append(dk)
        dvs.append(dv)
        dbs.append(db)
    p0, p1 = _attn_assemble(dqs, dks, dvs, dza)
    g_relb = _rel_bias_grad(dbs, buckets)
    dqb, dfb, dib, dzb, dlb, dgo = _hgrn_bwd(proj, o_raw, dob, states, hgrn_lb, onorm_g)
    pieces = [p0, p1, dqb, dfb, dib, dzb, dga, dgb]
    small = [gwa.astype(XCHG_DTYPE).reshape(A_W, N_DEV, D // N_DEV).transpose(1, 0, 2),
             gwb.astype(XCHG_DTYPE).reshape(N_DEV, D // N_DEV, D),
             gwo.astype(XCHG_DTYPE).reshape(N_DEV, D // N_DEV, D)]
    gw_in, *received_small = _gw_in(ht, pieces, small)
    gx, hsums, received_in = _dh(pieces, w_in_g, x, dx2, mod3, norm_g, [gw_in])
    received = [received_in] + received_small
    row = jnp.concatenate([
        hsums[0], hsums[1], tsums[1],
        hsums[2],
        dgo.reshape(G_HEADS, 8, G_DK)[:, 0].reshape(-1),
        g_relb.reshape(-1),
        dlb[0],
        tsums[0],
        tsums[2],
    ])
    return gx, received, row


def kernel(x, c, w_ada, b_ada, norm_g, w_in, hgrn_onorm_g, w_branch_a, w_branch_b, w_out, rel_bias, hgrn_lb, final_g, loss_target, m_w_ada, m_b_ada, m_norm_g, m_w_in, m_hgrn_onorm_g, m_w_branch_a, m_w_branch_b, m_w_out, m_rel_bias, m_hgrn_lb, m_final_g, v_w_ada, v_b_ada, v_norm_g, v_w_in, v_hgrn_onorm_g, v_w_branch_a, v_w_branch_b, v_w_out, v_rel_bias, v_hgrn_lb, v_final_g):
    me = 4 * lax.axis_index("x") + 2 * lax.axis_index("y") + lax.axis_index("c")
    n_ada = w_ada.shape[2]

    w_in_g, c_all = _all_gather([w_in[0].astype(MXU_DTYPE), jnp.broadcast_to(c, (8, D))], "gather_w_in_c")

    c64 = c_all.reshape(8 * N_DEV, D)
    b_loc = lax.dynamic_slice(b_ada, (0, me * n_ada), (1, n_ada))
    mod_part = _mod_fwd(c64, w_ada[0], b_loc)[::8]
    (mod_all,) = _all_gather([mod_part], "gather_mod")
    mod = lax.dynamic_slice(mod_all, (0, me, 0), (N_DEV, 1, n_ada)).reshape(3, D)
    mod3 = jnp.concatenate([mod, jnp.zeros((5, D), F32)], axis=0)

    onorm_t = hgrn_onorm_g
    gx, (r_in, r_a, r_b, r_o), row = _local_step(
        x[0], loss_target[0], mod3, norm_g, w_in_g, onorm_t, w_branch_a[0].astype(MXU_DTYPE),
        w_branch_b[0].astype(MXU_DTYPE), w_out[0].astype(MXU_DTYPE), rel_bias, hgrn_lb,
        final_g.reshape(1, D))
    packed8 = jnp.concatenate([row[None, :], jnp.zeros((7, P_END), F32)], axis=0)
    (packed,) = _all_gather([packed8], "gather_small")
    packed = packed.reshape(8 * N_DEV, P_END)

    g_in, d_in, nm_in, nv_in = _adamw_big(r_in, w_in[0], m_w_in[0], v_w_in[0], "adamw_w_in")
    g_a, d_a, nm_a, nv_a = _adamw_big(r_a, w_branch_a[0], m_w_branch_a[0], v_w_branch_a[0], "adamw_w_branch_a")
    g_b, d_b, nm_b, nv_b = _adamw_big(r_b, w_branch_b[0], m_w_branch_b[0], v_w_branch_b[0], "adamw_w_branch_b")
    g_o, d_o, nm_o, nv_o = _adamw_big(r_o, w_out[0], m_w_out[0], v_w_out[0], "adamw_w_out")

    dmod64 = lax.dynamic_slice(packed, (0, P_MOD + me * n_ada), (8 * N_DEV, n_ada))
    g_ada, d_ada, nm_ada, nv_ada = _adamw_w_ada(c64, dmod64, w_ada[0], m_w_ada[0], v_w_ada[0])

    def flat_relb(t):
        return jnp.pad(t.T, ((0, 0), (0, 128 - N_BUCKETS))).reshape(1, A_HEADS * 128)

    def unflat_relb(t):
        return t.reshape(A_HEADS, 128)[:, :N_BUCKETS].T

    fg2 = lambda t: t.reshape(1, D)
    smalls = _adamw_small(
        packed, b_ada, norm_g, hgrn_onorm_g, flat_relb(rel_bias), hgrn_lb, fg2(final_g),
        [m_b_ada, m_norm_g, m_hgrn_onorm_g, flat_relb(m_rel_bias), m_hgrn_lb, fg2(m_final_g)],
        [v_b_ada, v_norm_g, v_hgrn_onorm_g, flat_relb(v_rel_bias), v_hgrn_lb, fg2(v_final_g)])
    loss = smalls[0][0, 0]

    def small(kind):
        s = smalls[1 + 6 * kind:7 + 6 * kind]
        return s[0], s[1], s[2], unflat_relb(s[3]), s[4], s[5].reshape(D)

    def leaves(ada, sm, w_in_, wa_, wb_, wo_):
        b_, n_, o_, r_, l_, f_ = sm
        return (ada[None], b_, n_, w_in_[None], o_, wa_[None], wb_[None], wo_[None], r_, l_, f_)

    return (loss, gx[None],
            *leaves(g_ada, small(0), g_in, g_a, g_b, g_o),
            *leaves(d_ada, small(1), d_in, d_a, d_b, d_o),
            *leaves(nm_ada, small(2), nm_in, nm_a, nm_b, nm_o),
            *leaves(nv_ada, small(3), nv_in, nv_a, nv_b, nv_o))
```

```python
import functools
import math

import numpy as np
import jax
import jax.numpy as jnp
from jax import lax
from jax.experimental import pallas as pl
from jax.experimental.pallas import tpu as pltpu

F32 = jnp.float32
BF16 = jnp.bfloat16
MXU_DTYPE = jnp.bfloat16
XCHG_DTYPE = jnp.bfloat16

N_DEV = 8
D = 1024
A_HEADS = 8
A_HD = 64
A_W = A_HEADS * A_HD
A_BLK = 128
PATTERNS = ((128, 1), (512, 4), (2048, 16))
N_BUCKETS = 32
MAX_DISTANCE = 2048
NEG = -1e30
G_HEADS = 8
G_DK = 128
G_W = G_HEADS * G_DK
IN_W = 8 * D
EPS = 1e-6
ADAM_LR = 0.001
ADAM_B1 = 0.9
ADAM_B2 = 0.999
ADAM_EPS = 1e-08
ADAM_WD = 0.01
ADAM_STEP = 10

G_CHUNK = 128
G_SUB = 8
G_HPS_FWD = 8
G_HPS_BWD = 4
G_RB = 16
VMEM_LIMIT = 56 * 1024 * 1024

NN = (((1,), (0,)), ((), ()))
NT = (((1,), (1,)), ((), ()))
TN = (((0,), (0,)), ((), ()))
MESH = pl.DeviceIdType.MESH


def _mm(a, b, dims=NN):
    return lax.dot_general(a.astype(MXU_DTYPE), b.astype(MXU_DTYPE), dims,
                           preferred_element_type=F32)


def _mm_exact(t, x):
    hi = x.astype(BF16)
    r = x - hi.astype(F32)
    mid = r.astype(BF16)
    lo = (r - mid.astype(F32)).astype(BF16)
    tb = t.astype(BF16)
    return sum(lax.dot_general(tb, p, NN, preferred_element_type=F32) for p in (hi, mid, lo))


def _split(x):
    hi = x.astype(BF16)
    return hi, (x - hi.astype(F32)).astype(BF16)


def _mm_split(a, b, dims):
    dot = lambda p, q: lax.dot_general(p, q, dims, preferred_element_type=F32)
    return dot(a[0], b[0]) + dot(a[0], b[1]) + dot(a[1], b[0])


def _sigmoid(x):
    return 0.5 * jnp.tanh(0.5 * x) + 0.5


def _params(sem=None):
    return pltpu.CompilerParams(dimension_semantics=sem, vmem_limit_bytes=VMEM_LIMIT)


def _all_gather(xs, name):
    n = len(xs)

    def body(*refs):
        ins, outs = refs[:n], refs[n:2 * n]
        send_sems, recv_sems, local_sems = refs[2 * n:]
        x, y, c = lax.axis_index("x"), lax.axis_index("y"), lax.axis_index("c")
        me, sibling = (x, y, c), (x, y, 1 - c)
        chips = [(1 - x, y), (x, 1 - y), (1 - x, 1 - y)]

        def slot(ref, dev):
            return ref.at[4 * dev[0] + 2 * dev[1] + dev[2]]

        def copy(a, k, block, to, src=None):
            return pltpu.make_async_remote_copy(
                src_ref=slot(outs[a], block) if src is None else src,
                dst_ref=slot(outs[a], block),
                send_sem=send_sems.at[a, k], recv_sem=recv_sems.at[a, k],
                device_id=to, device_id_type=MESH)

        mine, first, passed = [], [], []
        for a in range(n):
            cp = pltpu.make_async_copy(ins[a], slot(outs[a], me), local_sems.at[a])
            cp.start()
            mine.append(cp)
            first.append(copy(a, 0, me, sibling, src=ins[a]))
            for j, chip in enumerate(chips):
                first.append(copy(a, 1 + j, me, (*chip, c), src=ins[a]))
        for cp in first:
            cp.start()
        for j, chip in enumerate(chips):
            for a in range(n):
                copy(a, 1 + j, (*chip, c), me).wait_recv()
                cp = copy(a, 4 + j, (*chip, c), sibling)
                cp.start()
                passed.append(cp)
        for a in range(n):
            copy(a, 0, sibling, me).wait_recv()
            for j, chip in enumerate(chips):
                copy(a, 4 + j, (*chip, 1 - c), me).wait_recv()
        for cp in first + passed:
            cp.wait_send()
        for cp in mine:
            cp.wait()

    any_spec = pl.BlockSpec(memory_space=pl.ANY)
    return pl.pallas_call(
        body, name=name,
        out_shape=[jax.ShapeDtypeStruct((N_DEV,) + v.shape, v.dtype) for v in xs],
        in_specs=[any_spec] * n, out_specs=[any_spec] * n,
        scratch_shapes=[pltpu.SemaphoreType.DMA((n, 7)), pltpu.SemaphoreType.DMA((n, 7)),
                        pltpu.SemaphoreType.DMA((n,))],
    )(*xs)


def _all_to_all_copies(ins, outs, send_sems, recv_sems, local_sems, gather=False):
    n = len(ins)
    x, y, c = lax.axis_index("x"), lax.axis_index("y"), lax.axis_index("c")
    me = 4 * x + 2 * y + c
    peers = []
    for m in range(1, N_DEV):
        peers.append((1 - x if m & 4 else x, 1 - y if m & 2 else y, 1 - c if m & 1 else c))

    def chunk(a, j):
        return ins[a] if gather else ins[a].at[j]

    def copy(a, k, landing):
        peer = peers[k]
        pid = 4 * peer[0] + 2 * peer[1] + peer[2]
        return pltpu.make_async_remote_copy(
            src_ref=chunk(a, pid), dst_ref=outs[a].at[pid if landing else me],
            send_sem=send_sems.at[a, k], recv_sem=recv_sems.at[a, k],
            device_id=peer, device_id_type=MESH)

    def local(a):
        return pltpu.make_async_copy(chunk(a, me), outs[a].at[me], local_sems.at[a])

    def start():
        for a in range(n):
            local(a).start()
        for k in range(N_DEV - 1):
            for a in range(n):
                copy(a, k, False).start()

    def wait():
        for k in range(N_DEV - 1):
            for a in range(n):
                copy(a, k, True).wait_recv()
        for k in range(N_DEV - 1):
            for a in range(n):
                copy(a, k, False).wait_send()
        for a in range(n):
            local(a).wait()

    return start, wait


def _mod_fwd(c64, w_ada, b_loc):
    def body(c_ref, w_ref, b_ref, o_ref):
        cv = c_ref[...]
        sc = cv * _sigmoid(cv)
        o_ref[...] = _mm(sc, w_ref[...]) + b_ref[...]

    return pl.pallas_call(
        body, name="mod_fwd",
        out_shape=jax.ShapeDtypeStruct((c64.shape[0], w_ada.shape[1]), F32),
        compiler_params=_params(),
    )(c64, w_ada, b_loc)


def _inproj(x, mod3, norm_g, w_in_g, blocks, tm=256):
    S = x.shape[0]
    ni = S // tm
    nb = len(blocks)

    def body(*refs):
        x_ref, mod_ref, g_ref, w_ref = refs[:4]
        b_ins = refs[4:4 + nb]
        proj_ref, ht_ref = refs[4 + nb:6 + nb]
        b_outs = refs[6 + nb:6 + 2 * nb]
        w_all, send_sems, recv_sems, local_sems = refs[6 + 2 * nb:]
        i = pl.program_id(0)
        start, wait = _all_to_all_copies(b_ins, b_outs, send_sems, recv_sems, local_sems, gather=True)

        @pl.when(i == 0)
        def _():
            start()
            pltpu.sync_copy(w_ref, w_all)

        xv = x_ref[...]
        r = lax.rsqrt(jnp.mean(xv * xv, axis=-1, keepdims=True) + EPS)
        h = ((xv * r * g_ref[...]) * (1.0 + mod_ref[1:2, :]) + mod_ref[0:1, :]).astype(MXU_DTYPE)
        ht_ref[...] = h.T
        for j in range(N_DEV):
            proj_ref[:, j * D:(j + 1) * D] = _mm(h, w_all[j])

        @pl.when(i == ni - 1)
        def _():
            wait()

    any_spec = pl.BlockSpec(memory_space=pl.ANY)
    return pl.pallas_call(
        body, name="inproj_gather",
        grid=(ni,),
        in_specs=[pl.BlockSpec((tm, D), lambda i: (i, 0)),
                  pl.BlockSpec((8, D), lambda i: (0, 0)),
                  pl.BlockSpec((1, D), lambda i: (0, 0)),
                  any_spec] + [any_spec] * nb,
        out_specs=[pl.BlockSpec((tm, IN_W), lambda i: (i, 0)),
                   pl.BlockSpec((D, tm), lambda i: (0, i))] + [any_spec] * nb,
        out_shape=[jax.ShapeDtypeStruct((S, IN_W), F32), jax.ShapeDtypeStruct((D, S), MXU_DTYPE)]
                  + [jax.ShapeDtypeStruct((N_DEV,) + b.shape, b.dtype) for b in blocks],
        scratch_shapes=[pltpu.VMEM(w_in_g.shape, w_in_g.dtype),
                        pltpu.SemaphoreType.DMA((nb, N_DEV - 1)), pltpu.SemaphoreType.DMA((nb, N_DEV - 1)),
                        pltpu.SemaphoreType.DMA((nb,))],
        compiler_params=_params(("arbitrary",)),
    )(x, mod3, norm_g, w_in_g, *blocks)


def _bucket_tables():
    qi = np.arange(A_BLK)[:, None]
    kj = np.arange(2 * A_BLK)[None, :]
    delta = qi + A_BLK - kj
    out = []
    for window, dil in PATTERNS:
        span = window // dil
        band = (delta >= 0) & (delta <= span)
        dist = np.clip(delta, 0, None) * dil
        max_exact = N_BUCKETS // 2
        nf = dist.astype(np.float32)
        large = max_exact + (np.log(np.maximum(nf, np.float32(1.0)) / np.float32(max_exact))
                             / np.float32(math.log(MAX_DISTANCE / max_exact))
                             * np.float32(N_BUCKETS - max_exact)).astype(np.int32)
        large = np.minimum(large, N_BUCKETS - 1)
        bucket = np.where(dist < max_exact, dist, large)
        out.append(np.where(band, bucket, -1).astype(np.int32))
    return np.stack(out)


def _bias_tables(rel_bias, buckets):
    def body(rb_ref, bk_ref, o_ref):
        h = pl.program_id(1)
        bk = bk_ref[0]
        acc = jnp.full(bk.shape, NEG, F32)
        for b in range(N_BUCKETS):
            acc = jnp.where(bk == b, rb_ref[b, h], acc)
        o_ref[0, 0] = acc

    return pl.pallas_call(
        body, name="bias_tables",
        grid=(3, A_HEADS),
        in_specs=[pl.BlockSpec(memory_space=pltpu.SMEM),
                  pl.BlockSpec((1, A_BLK, 2 * A_BLK), lambda p, h: (p, 0, 0))],
        out_specs=pl.BlockSpec((1, 1, A_BLK, 2 * A_BLK), lambda p, h: (p, h, 0, 0)),
        out_shape=jax.ShapeDtypeStruct((3, A_HEADS, A_BLK, 2 * A_BLK), F32),
        compiler_params=_params(("arbitrary", "arbitrary")),
    )(rel_bias, buckets)


A_TILES = 16


def _attn_heads_per_step(d):
    return A_HEADS if d == 1 else 2


def _attn_in_specs(sb, nsb, hw):
    w = A_HD * hw
    per = A_W // w

    def cur(col):
        return pl.BlockSpec((sb, w), lambda hp, n: (jnp.minimum(n, nsb - 1), per * col + hp))

    def prev(col):
        return pl.BlockSpec((sb, w), lambda hp, n: (jnp.maximum(jnp.minimum(n, nsb - 1) - 1, 0), per * col + hp))

    return [cur(0), prev(1), cur(1), prev(2), cur(2)]


def _rows(r, d):
    return pl.ds(r, A_BLK) if d == 1 else pl.ds(r, A_BLK, stride=d)


def _for_residues(d, hw, fn):
    unroll = min(d, max(1, A_TILES // hw))
    if d == unroll:
        _round_robin([g for r in range(d) for g in fn(r)])
    else:
        def group(g, c):
            _round_robin([t for u in range(unroll) for t in fn(g * unroll + u)])
            return c
        lax.fori_loop(0, d // unroll, group, 0)


def _attn_stack(t):
    first_half = lax.broadcasted_iota(jnp.int32, (1, 2 * A_HD), 1) < A_HD
    return jnp.concatenate([jnp.where(first_half, t, 0.0), jnp.where(first_half, 0.0, t)], axis=0)


def _attn_unstack(t2):
    first_half = lax.broadcasted_iota(jnp.int32, (1, 2 * A_HD), 1) < A_HD
    return jnp.where(first_half, t2[:A_BLK], t2[A_BLK:])


def _attn_scores(q, k, b_ref, pp, first):
    bias = jnp.concatenate([b_ref[2 * pp] + first, b_ref[2 * pp + 1] + first], axis=0)
    return _mm(_attn_stack(q), k, NT) * (A_HD ** -0.5) + bias


def _attn_fwd(proj, bias_p, d, name):
    S = proj.shape[0]
    sb = A_BLK * d
    nsb = S // sb
    hw = _attn_heads_per_step(d)

    def body(q_ref, kp_ref, kc_ref, vp_ref, vc_ref, b_ref, o_ref, l_ref):
        n = pl.program_id(1)
        kj = lax.broadcasted_iota(jnp.int32, (A_BLK, 2 * A_BLK), 1)
        first = jnp.where((n == 0) & (kj < A_BLK), NEG, 0.0).astype(F32)

        def residue(r):
            rows = _rows(r, d)

            def pair(pp):
                lanes = pl.ds(2 * A_HD * pp, 2 * A_HD)
                k = jnp.concatenate([kp_ref[rows, lanes], kc_ref[rows, lanes]], axis=0)
                v = jnp.concatenate([vp_ref[rows, lanes], vc_ref[rows, lanes]], axis=0)
                s = _attn_scores(q_ref[rows, lanes], k, b_ref, pp, first)
                yield
                m = jnp.max(s, axis=-1, keepdims=True)
                p = jnp.exp(s - m)
                den = jnp.sum(p, axis=-1, keepdims=True)
                pv = _mm(p, v)
                yield
                o_ref[rows, lanes] = _attn_unstack(pv / den)
                l_ref[rows, lanes] = _attn_unstack(jnp.broadcast_to(m + jnp.log(den), (2 * A_BLK, 2 * A_HD)))

            return [pair(pp) for pp in range(hw // 2)]

        _for_residues(d, hw, residue)

    out = pl.BlockSpec((sb, A_HD * hw), lambda hp, n: (n, hp))
    return pl.pallas_call(
        body, name=name,
        grid=(A_HEADS // hw, nsb),
        in_specs=_attn_in_specs(sb, nsb, hw) + [pl.BlockSpec((hw, A_BLK, 2 * A_BLK), lambda hp, n: (hp, 0, 0))],
        out_specs=[out, out],
        out_shape=[jax.ShapeDtypeStruct((S, A_W), F32)] * 2,
        compiler_params=_params(("parallel", "parallel")),
    )(proj, proj, proj, proj, proj, bias_p)


def _attn_bwd(proj, do, lt, delta, bias_p, d, name):
    S = proj.shape[0]
    sb = A_BLK * d
    nsb = S // sb
    hw = _attn_heads_per_step(d)

    def body(q_ref, kp_ref, kc_ref, vp_ref, vc_ref, do_ref, lt_ref, dl_ref, b_ref,
             dq_ref, dk_ref, dv_ref, db_ref, ck, cv):
        n = pl.program_id(1)

        @pl.when(n == 0)
        def _():
            db_ref[...] = jnp.zeros_like(db_ref)
            ck[...] = jnp.zeros_like(ck)
            cv[...] = jnp.zeros_like(cv)

        @pl.when(n < nsb)
        def _():
            kj = lax.broadcasted_iota(jnp.int32, (A_BLK, 2 * A_BLK), 1)
            first = jnp.where((n == 0) & (kj < A_BLK), NEG, 0.0).astype(F32)

            def residue(r):
                rows = _rows(r, d)

                def pair(pp):
                    lanes = pl.ds(2 * A_HD * pp, 2 * A_HD)
                    lt_r, dl_r = lt_ref[rows, lanes], dl_ref[rows, lanes]
                    k = jnp.concatenate([kp_ref[rows, lanes], kc_ref[rows, lanes]], axis=0)
                    v = jnp.concatenate([vp_ref[rows, lanes], vc_ref[rows, lanes]], axis=0)
                    q2 = _attn_stack(q_ref[rows, lanes])
                    do2 = _attn_stack(do_ref[rows, lanes])
                    col = lambda t: jnp.concatenate([t[:, 0:1], t[:, A_HD:A_HD + 1]], axis=0)
                    s = _attn_scores(q_ref[rows, lanes], k, b_ref, pp, first)
                    dp = _mm(do2, v, NT)
                    yield
                    p = jnp.exp(s - col(lt_r))
                    ds = p * (dp - col(dl_r))
                    db_ref[2 * pp] += ds[:A_BLK]
                    db_ref[2 * pp + 1] += ds[A_BLK:]
                    dq = _mm(ds, k)
                    dk = _mm(ds, q2, TN) * (A_HD ** -0.5)
                    dv = _mm(p, do2, TN)
                    yield
                    dq_ref[rows, lanes] = _attn_unstack(dq) * (A_HD ** -0.5)
                    dk_ref[rows, lanes] = ck[rows, lanes] + dk[:A_BLK]
                    dv_ref[rows, lanes] = cv[rows, lanes] + dv[:A_BLK]
                    ck[rows, lanes] = dk[A_BLK:]
                    cv[rows, lanes] = dv[A_BLK:]

                return [pair(pp) for pp in range(hw // 2)]

            _for_residues(d, hw, residue)

        @pl.when(n == nsb)
        def _():
            dk_ref[...] = ck[...]
            dv_ref[...] = cv[...]

    w = A_HD * hw
    row = pl.BlockSpec((sb, w), lambda hp, n: (jnp.minimum(n, nsb - 1), hp))
    lag = pl.BlockSpec((sb, w), lambda hp, n: (jnp.maximum(n - 1, 0), hp))
    tab = pl.BlockSpec((hw, A_BLK, 2 * A_BLK), lambda hp, n: (hp, 0, 0))
    return pl.pallas_call(
        body, name=name,
        grid=(A_HEADS // hw, nsb + 1),
        in_specs=_attn_in_specs(sb, nsb, hw) + [row, row, row, tab],
        out_specs=[row, lag, lag, tab],
        out_shape=[jax.ShapeDtypeStruct((S, A_W), F32)] * 3
                  + [jax.ShapeDtypeStruct((A_HEADS, A_BLK, 2 * A_BLK), F32)],
        scratch_shapes=[pltpu.VMEM((sb, w), F32), pltpu.VMEM((sb, w), F32)],
        compiler_params=_params(("parallel", "arbitrary")),
    )(proj, proj, proj, proj, proj, do, lt, delta, bias_p)


def _attn_assemble(dqs, dks, dvs, dz, tm=512):
    S = dz.shape[0]

    def body(q1, q2, q3, k1, k2, k3, v1, v2, v3, z_ref, p0_ref, p1_ref):
        p0_ref[:, :A_W] = (q1[...] + q2[...] + q3[...]).astype(MXU_DTYPE)
        p0_ref[:, A_W:] = (k1[...] + k2[...] + k3[...]).astype(MXU_DTYPE)
        p1_ref[:, :A_W] = (v1[...] + v2[...] + v3[...]).astype(MXU_DTYPE)
        p1_ref[:, A_W:] = z_ref[...]

    spec = pl.BlockSpec((tm, A_W), lambda i: (i, 0))
    wide = pl.BlockSpec((tm, 2 * A_W), lambda i: (i, 0))
    return pl.pallas_call(
        body, name="attn_assemble",
        grid=(S // tm,),
        in_specs=[spec] * 10,
        out_specs=[wide, wide],
        out_shape=[jax.ShapeDtypeStruct((S, 2 * A_W), MXU_DTYPE)] * 2,
        compiler_params=_params(("parallel",)),
    )(*dqs, *dks, *dvs, dz)


def _rel_bias_grad(dbs, buckets):
    def body(d1, d2, d3, bk_ref, o_ref):
        row = lax.broadcasted_iota(jnp.int32, (A_HEADS, 128), 0)
        lane = lax.broadcasted_iota(jnp.int32, (A_HEADS, 128), 1)
        acc = jnp.zeros((A_HEADS, 128), F32)
        for p, dref in enumerate((d1, d2, d3)):
            bk = bk_ref[p]
            for h in range(A_HEADS):
                ds = dref[h]
                for b in range(N_BUCKETS):
                    s = jnp.sum(jnp.where(bk == b, ds, 0.0), keepdims=True)
                    acc = acc + jnp.where((row == h) & (lane == b), s, 0.0)
        o_ref[...] = acc

    return pl.pallas_call(
        body, name="rel_bias_grad",
        out_shape=jax.ShapeDtypeStruct((A_HEADS, 128), F32),
        compiler_params=_params(),
    )(*dbs, buckets)


def _tri(c):
    t = np.tril(np.ones((c, c), np.float32))
    return jnp.asarray(t), jnp.asarray(t.T.copy())


def _fill_above(ref, x, pad):
    ref[0:G_SUB, :] = jnp.full((G_SUB, x.shape[1]), pad, F32)
    ref[G_SUB:, :] = x


def _fill_below(ref, x, pad):
    ref[0:x.shape[0], :] = x
    ref[x.shape[0]:, :] = jnp.full((G_SUB, x.shape[1]), pad, F32)


def _hgrn_gates(q_ref, f_ref, lbp_ref, tri_ref):
    qraw = q_ref[...]
    sq = _sigmoid(qraw)
    q = qraw * sq
    sg = _sigmoid(f_ref[...])
    lb = _sigmoid(lbp_ref[0:1, :] - lbp_ref[1:2, :])
    f = lb + (1.0 - lb) * sg
    k = 1.0 - f
    b = _mm_exact(tri_ref[...], jnp.log(f))
    return qraw, sq, q, sg, lb, f, k, b


def _hgrn_col(C, base, idx, hps):
    return pl.BlockSpec((C, hps * G_DK), lambda h, n: (idx(n), base * (G_HEADS // hps) + h))


def _round_robin(stages):
    live = list(stages)
    while live:
        nxt = []
        for g in live:
            try:
                next(g)
                nxt.append(g)
            except StopIteration:
                pass
        live = nxt


def _hgrn_levels(C):
    out, m = [], G_SUB
    while 2 * m <= C:
        out.append(m)
        m *= 2
    return out


def _hgrn_level_masks(C):
    ti = np.arange(C)[:, None]
    si = np.arange(C)[None, :]
    return jnp.asarray(np.stack([((ti // (2 * m) == si // (2 * m)) & (ti - si >= G_SUB)).astype(np.float32)
                                 for m in _hgrn_levels(C)]))


def _hgrn_level(b, q, k, C, m):
    zeros = jnp.zeros((m, G_DK), F32)
    eq, ek, qt, kt = [], [], [], []
    for blk in range(0, C // m, 2):
        lo, mid, hi = blk * m, (blk + 1) * m, (blk + 2) * m
        ref = b[mid:mid + 1]
        e_right = jnp.exp(b[mid:hi] - ref)
        e_left = jnp.exp(ref - b[lo:mid])
        eq += [zeros, e_right]
        ek += [e_left, zeros]
        qt += [zeros, q[mid:hi] * e_right]
        kt += [k[lo:mid] * e_left, zeros]
    cat = lambda parts: jnp.concatenate(parts, axis=0)
    return cat(qt), cat(kt), cat(eq), cat(ek)


def _hgrn_fwd(proj, hgrn_lb, onorm_g, C=G_CHUNK):
    S = proj.shape[0]
    nc = S // C
    tri, _ = _tri(C)
    masks = _hgrn_level_masks(C)
    hps = G_HPS_FWD

    def body(q_ref, f_ref, i_ref, z_ref, lbp_ref, go_ref, tri_ref, pm_ref, o_ref, ob_ref, st_ref, St, kp, vp, fp):
        @pl.when(pl.program_id(1) == 0)
        def _():
            St[...] = jnp.zeros_like(St)

        heads = []
        for hh in range(hps):
            ln = pl.ds(G_DK * hh, G_DK)
            heads.append(head(
                q_ref.at[:, ln], f_ref.at[:, ln], i_ref.at[:, ln], z_ref.at[:, ln], lbp_ref.at[:, ln], go_ref,
                tri_ref, pm_ref, o_ref.at[:, ln], ob_ref.at[:, ln], st_ref.at[0, hh], St.at[hh], kp.at[hh], vp.at[hh],
                fp.at[hh]))
        _round_robin(heads)

    def head(q_ref, f_ref, i_ref, z_ref, lbp_ref, go_ref, tri_ref, pm_ref, o_ref, ob_ref, st_ref, St, kp, vp, fp):
        _, _, q, _, _, f, k, b = _hgrn_gates(q_ref, f_ref, lbp_ref, tri_ref)
        v = i_ref[...]
        bC = b[C - 1:C, :]
        S0 = St[...]
        o = _mm(q * jnp.exp(b), S0, NT)
        yield
        _fill_above(kp, k, 0.0)
        _fill_above(vp, v, 0.0)
        _fill_above(fp, f, 1.0)
        near = []
        for r0 in range(0, C, G_RB):
            qb = q[r0:r0 + G_RB]
            acc = e = None
            for l in range(G_SUB):
                rows = pl.ds(G_SUB - l + r0, G_RB)
                if l > 0:
                    fl = fp[pl.ds(G_SUB - l + 1 + r0, G_RB), :]
                    e = fl if e is None else e * fl
                kl = kp[rows, :]
                a = jnp.sum(qb * kl if e is None else qb * kl * e, axis=-1, keepdims=True)
                t = a * vp[rows, :]
                acc = t if acc is None else acc + t
            near.append(acc)
        o = o + jnp.concatenate(near, axis=0)
        yield
        a_off = jnp.zeros((C, C), F32)
        for lv, m in enumerate(_hgrn_levels(C)):
            qt, kt, _, _ = _hgrn_level(b, q, k, C, m)
            prod = _mm_split(_split(qt), _split(kt), NT) if m == G_SUB else _mm(qt, kt, NT)
            a_off = a_off + pm_ref[lv] * prod
        yield
        o = o + _mm(a_off, v)
        S1 = S0 * jnp.exp(bC) + _mm(v, k * jnp.exp(bC - b), TN)
        St[...] = S1
        st_ref[...] = S1
        o_ref[...] = o
        r = lax.rsqrt(jnp.mean(o * o, axis=-1, keepdims=True) + EPS)
        z = z_ref[...]
        ob_ref[...] = (o * r * go_ref[...] * (z * _sigmoid(z))).astype(MXU_DTYPE)

    ident = lambda n: n
    w = hps * G_DK
    out = pl.BlockSpec((C, w), lambda h, n: (n, h))
    return pl.pallas_call(
        body, name="hgrn_fwd",
        grid=(G_HEADS // hps, nc),
        in_specs=[_hgrn_col(C, base, ident, hps) for base in (2, 3, 4, 5)] + [
                  pl.BlockSpec((2, w), lambda h, n: (0, h)),
                  pl.BlockSpec((1, G_DK), lambda h, n: (0, 0)),
                  pl.BlockSpec((C, C), lambda h, n: (0, 0)),
                  pl.BlockSpec(masks.shape, lambda h, n: (0, 0, 0))],
        out_specs=[out, out, pl.BlockSpec((1, hps, G_DK, G_DK), lambda h, n: (n, h, 0, 0))],
        out_shape=[jax.ShapeDtypeStruct((S, G_W), F32), jax.ShapeDtypeStruct((S, G_W), MXU_DTYPE),
                   jax.ShapeDtypeStruct((nc, G_HEADS, G_DK, G_DK), F32)],
        scratch_shapes=[pltpu.VMEM((hps, G_DK, G_DK), F32)] + [pltpu.VMEM((hps, C + G_SUB, G_DK), F32)] * 3,
        compiler_params=_params(("parallel", "arbitrary")),
    )(proj, proj, proj, proj, hgrn_lb, onorm_g, tri, masks)


def _hgrn_bwd(proj, o_raw, dob, states, hgrn_lb, onorm_g, C=G_CHUNK):
    S = proj.shape[0]
    nc = S // C
    tri, triu = _tri(C)
    masks = _hgrn_level_masks(C)
    hps = G_HPS_BWD

    def body(q_ref, f_ref, i_ref, z_ref, o_ref, dob_ref, s0_ref, s1_ref, lbp_ref, go_ref, tri_ref, triu_ref,
             pm_ref, dq_ref, df_ref, di_ref, dz_ref, dlb_ref, dgo_ref, dSt, *shifted):
        @pl.when(pl.program_id(1) == 0)
        def _():
            dSt[...] = jnp.zeros_like(dSt)
            dlb_ref[...] = jnp.zeros_like(dlb_ref)
            dgo_ref[...] = jnp.zeros_like(dgo_ref)

        heads = []
        for hh in range(hps):
            ln = pl.ds(G_DK * hh, G_DK)
            heads.append(head(
                q_ref.at[:, ln], f_ref.at[:, ln], i_ref.at[:, ln], z_ref.at[:, ln], o_ref.at[:, ln],
                dob_ref.at[:, ln], s0_ref.at[0, hh], s1_ref.at[0, hh], lbp_ref.at[:, ln], go_ref, tri_ref, triu_ref,
                pm_ref, dq_ref.at[:, ln], df_ref.at[:, ln], di_ref.at[:, ln], dz_ref.at[:, ln], dlb_ref.at[:, ln],
                dgo_ref.at[pl.ds(8 * hh, 8), :], dSt.at[hh], *[t.at[hh] for t in shifted]))
        _round_robin(heads)

    def head(q_ref, f_ref, i_ref, z_ref, o_ref, dob_ref, s0_ref, s1_ref, lbp_ref, go_ref, tri_ref, triu_ref,
             pm_ref, dq_ref, df_ref, di_ref, dz_ref, dlb_ref, dgo_ref, dSt, kp, vp, fp, qn, dn_, fn, xs, dac):
        cn = nc - 1 - pl.program_id(1)
        qraw, sq, q, sg, lb, f, k, b = _hgrn_gates(q_ref, f_ref, lbp_ref, tri_ref)
        v = i_ref[...]
        bC = b[C - 1:C, :]
        eb = jnp.exp(b)
        ecb = jnp.exp(bC - b)
        o = o_ref[...]
        z = z_ref[...]
        sz = _sigmoid(z)
        go = go_ref[...]
        g_ob = dob_ref[...]
        r = lax.rsqrt(jnp.mean(o * o, axis=-1, keepdims=True) + EPS)
        nh = o * r
        dnrm = g_ob * (z * sz)
        dz_ref[...] = (g_ob * (nh * go) * (sz * (1.0 + z * (1.0 - sz)))).astype(MXU_DTYPE)
        dgo_ref[0:1, :] += jnp.sum(dnrm * nh, axis=0, keepdims=True)
        dn = dnrm * go
        do = r * (dn - nh * jnp.mean(dn * nh, axis=-1, keepdims=True))

        yield
        S0 = jnp.where(cn == 0, 0.0, s0_ref[...])
        S1 = s1_ref[...]
        dS1 = dSt[...]
        dq = eb * _mm(do, S0)
        dk = ecb * _mm(v, dS1)
        dv = _mm(k * ecb, dS1, NT)
        bnd = jnp.sum(dS1 * S1, axis=0, keepdims=True)
        dSt[...] = dS1 * jnp.exp(bC) + _mm(do, q * eb, TN)

        _fill_above(kp, k, 0.0)
        _fill_above(vp, v, 0.0)
        _fill_above(fp, f, 1.0)
        _fill_below(qn, q, 0.0)
        _fill_below(dn_, do, 0.0)
        _fill_below(fn, f, 1.0)
        yield
        for r0 in range(0, C, G_RB):
            do_b = do[r0:r0 + G_RB]
            for l in range(G_SUB):
                xs[pl.ds(l * C + r0, G_RB), :] = (do_b * vp[pl.ds(G_SUB - l + r0, G_RB), :]).astype(MXU_DTYPE)
        dac[0:G_SUB * C, :] = _mm(xs[...], jnp.ones((G_DK, G_DK), MXU_DTYPE))
        dac[G_SUB * C:, :] = jnp.zeros((G_SUB, G_DK), F32)
        yield
        near_q, near_k, near_v = [], [], []
        for r0 in range(0, C, G_RB):
            k_b = k[r0:r0 + G_RB]
            aq = ak = av = e = e2 = None
            for l in range(G_SUB):
                down, up = pl.ds(G_SUB - l + r0, G_RB), pl.ds(l + r0, G_RB)
                if l > 0:
                    fl = fp[pl.ds(G_SUB - l + 1 + r0, G_RB), :]
                    e = fl if e is None else e * fl
                    fu = fn[up, :]
                    e2 = fu if e2 is None else e2 * fu
                kl = kp[down, :]
                t = dac[pl.ds(l * C + r0, G_RB), :] * (kl if e is None else kl * e)
                aq = t if aq is None else aq + t
                qu = qn[up, :]
                qe = qu if e2 is None else qu * e2
                dou = dn_[up, :]
                a2 = jnp.sum(qe * k_b, axis=-1, keepdims=True)
                t = dac[pl.ds(l * C + l + r0, G_RB), :] * qe
                ak = t if ak is None else ak + t
                t = a2 * dou
                av = t if av is None else av + t
            near_q.append(aq)
            near_k.append(ak)
            near_v.append(av)
        dq = dq + jnp.concatenate(near_q, axis=0)
        dk = dk + jnp.concatenate(near_k, axis=0)
        dv = dv + jnp.concatenate(near_v, axis=0)

        yield
        da_all = _mm(do, v, NT)
        a_off = jnp.zeros((C, C), F32)
        for lv, m in enumerate(_hgrn_levels(C)):
            qt, kt, eq, ek = _hgrn_level(b, q, k, C, m)
            da_m = pm_ref[lv] * da_all
            if m == G_SUB:
                qs, ks, das = _split(qt), _split(kt), _split(da_m)
                a_off = a_off + pm_ref[lv] * _mm_split(qs, ks, NT)
                dq = dq + _mm_split(das, ks, NN) * eq
                dk = dk + _mm_split(das, qs, TN) * ek
            else:
                a_off = a_off + pm_ref[lv] * _mm(qt, kt, NT)
                dq = dq + _mm(da_m, kt) * eq
                dk = dk + _mm(da_m, qt, TN) * ek
        dv = dv + _mm(a_off, do, TN)

        yield
        row = lax.broadcasted_iota(jnp.int32, (C, 1), 0)
        db = q * dq - k * dk + jnp.where(row == C - 1, bnd, 0.0)
        dg = _mm_exact(triu_ref[...], db)
        df = dg / f - dk
        df_ref[...] = (df * (1.0 - lb) * (sg * (1.0 - sg))).astype(MXU_DTYPE)
        dlb_ref[0:1, :] += jnp.sum(df * (1.0 - sg), axis=0, keepdims=True)
        dq_ref[...] = (dq * (sq * (1.0 + qraw * (1.0 - sq)))).astype(MXU_DTYPE)
        di_ref[...] = dv.astype(MXU_DTYPE)

    rev = lambda n: nc - 1 - n
    w = hps * G_DK
    blk = pl.BlockSpec((C, w), lambda h, n: (nc - 1 - n, h))
    return pl.pallas_call(
        body, name="hgrn_bwd",
        grid=(G_HEADS // hps, nc),
        in_specs=[_hgrn_col(C, base, rev, hps) for base in (2, 3, 4, 5)] + [
                  blk, blk,
                  pl.BlockSpec((1, hps, G_DK, G_DK), lambda h, n: (jnp.maximum(nc - 2 - n, 0), h, 0, 0)),
                  pl.BlockSpec((1, hps, G_DK, G_DK), lambda h, n: (nc - 1 - n, h, 0, 0)),
                  pl.BlockSpec((2, w), lambda h, n: (0, h)),
                  pl.BlockSpec((1, G_DK), lambda h, n: (0, 0)),
                  pl.BlockSpec((C, C), lambda h, n: (0, 0)),
                  pl.BlockSpec((C, C), lambda h, n: (0, 0)),
                  pl.BlockSpec(masks.shape, lambda h, n: (0, 0, 0))],
        out_specs=[blk, blk, blk, blk,
                   pl.BlockSpec((8, w), lambda h, n: (0, h)),
                   pl.BlockSpec((8 * hps, G_DK), lambda h, n: (h, 0))],
        out_shape=[jax.ShapeDtypeStruct((S, G_W), MXU_DTYPE)] * 4
                  + [jax.ShapeDtypeStruct((8, G_W), F32), jax.ShapeDtypeStruct((8 * G_HEADS, G_DK), F32)],
        scratch_shapes=[pltpu.VMEM((hps, G_DK, G_DK), F32)] + [pltpu.VMEM((hps, C + G_SUB, G_DK), F32)] * 6
                       + [pltpu.VMEM((hps, G_SUB * C, G_DK), MXU_DTYPE),
                          pltpu.VMEM((hps, G_SUB * C + G_SUB, G_DK), F32)],
        compiler_params=_params(("parallel", "arbitrary")),
    )(proj, proj, proj, proj, o_raw, dob, states, states, hgrn_lb, onorm_g, tri, triu, masks)


def _tail(x, target, os, ls, ob, proj, mod3, final_g, wa, wb, wo, tm=256):
    S = x.shape[0]
    nt = S // tm

    def body(x_ref, t_ref, o1, o2, o3, l1, l2, l3, za_ref, ob_ref, ga_ref, gb_ref, mod_ref, fg_ref,
             wa_ref, wb_ref, wo_ref,
             lt_ref, dx2_ref, do_ref, dl_ref, dza_ref, dob_ref, dga_ref, dgb_ref, sums_ref,
             gwa_ref, gwb_ref, gwo_ref, acc_a, acc_b, acc_o):
        i = pl.program_id(0)

        @pl.when(i == 0)
        def _():
            sums_ref[...] = jnp.zeros_like(sums_ref)
            acc_a[...] = jnp.zeros_like(acc_a)
            acc_b[...] = jnp.zeros_like(acc_b)
            acc_o[...] = jnp.zeros_like(acc_o)

        a1, a2, a3 = l1[...], l2[...], l3[...]
        lm = jnp.maximum(jnp.maximum(a1, a2), a3)
        e1, e2, e3 = jnp.exp(a1 - lm), jnp.exp(a2 - lm), jnp.exp(a3 - lm)
        lden = e1 + e2 + e3
        ao = (e1 * o1[...] + e2 * o2[...] + e3 * o3[...]) / lden
        lt_ref[...] = lm + jnp.log(lden)
        za = za_ref[...]
        sza = _sigmoid(za)
        oa_v, ob_v = (ao * (za * sza)).astype(MXU_DTYPE), ob_ref[...]
        pa = _mm(oa_v, wa_ref[...])
        pb = _mm(ob_v, wb_ref[...])
        sa, sb = _sigmoid(ga_ref[...]), _sigmoid(gb_ref[...])
        ym = sa * pa + sb * pb
        u = _mm(ym, wo_ref[...])
        gate = mod_ref[2:3, :]
        fg = fg_ref[...]
        x2 = x_ref[...] + gate * u
        r2 = lax.rsqrt(jnp.mean(x2 * x2, axis=-1, keepdims=True) + EPS)
        xn2 = x2 * r2
        e = xn2 * fg - t_ref[...]
        dy = e * (1.0 / D)
        dn = dy * fg
        dx2 = r2 * (dn - xn2 * jnp.mean(dn * xn2, axis=-1, keepdims=True))
        dx2_ref[...] = dx2
        sums_ref[0:1, :] += jnp.sum(dy * xn2, axis=0, keepdims=True)
        sums_ref[1:2, :] += jnp.sum(dx2 * u, axis=0, keepdims=True)
        sums_ref[2:3, :] += (0.5 / D) * jnp.sum(e * e, axis=0, keepdims=True)
        du = dx2 * gate
        dym = _mm(du, wo_ref[...], NT)
        acc_o[...] += _mm(ym, du, TN)
        dpa, dpb = dym * sa, dym * sb
        dga_ref[...] = (dym * pa * (sa * (1.0 - sa))).astype(MXU_DTYPE)
        dgb_ref[...] = (dym * pb * (sb * (1.0 - sb))).astype(MXU_DTYPE)
        doa = _mm(dpa, wa_ref[...], NT)
        dza_ref[...] = (doa * ao * (sza * (1.0 + za * (1.0 - sza)))).astype(MXU_DTYPE)
        do = doa * (za * sza)
        do_ref[...] = do
        prod = do * ao
        for h in range(A_HEADS):
            sl = slice(A_HD * h, A_HD * (h + 1))
            dl_ref[:, sl] = jnp.broadcast_to(jnp.sum(prod[:, sl], axis=-1, keepdims=True), (tm, A_HD))
        dob_ref[...] = _mm(dpb, wb_ref[...], NT)
        acc_a[...] += _mm(oa_v, dpa, TN)
        acc_b[...] += _mm(ob_v, dpb, TN)

        @pl.when(i == nt - 1)
        def _():
            pltpu.sync_copy(acc_a, gwa_ref)
            pltpu.sync_copy(acc_b, gwb_ref)
            pltpu.sync_copy(acc_o, gwo_ref)

    row = lambda w: pl.BlockSpec((tm, w), lambda i: (i, 0))
    full = lambda a, b: pl.BlockSpec((a, b), lambda i: (0, 0))
    any_spec = pl.BlockSpec(memory_space=pl.ANY)
    return pl.pallas_call(
        body, name="tail",
        grid=(nt,),
        in_specs=[row(D), row(D)] + [row(A_W)] * 6 + [pl.BlockSpec((tm, A_W), lambda i: (i, 3)), row(D),
                  pl.BlockSpec((tm, D), lambda i: (i, 6)), pl.BlockSpec((tm, D), lambda i: (i, 7)),
                  full(8, D), full(1, D), full(A_W, D), full(D, D), full(D, D)],
        out_specs=[row(A_W), row(D), row(A_W), row(A_W), row(A_W), row(D), row(D), row(D), full(8, D),
                   any_spec, any_spec, any_spec],
        out_shape=[jax.ShapeDtypeStruct((S, A_W), F32),
                   jax.ShapeDtypeStruct((S, D), F32), jax.ShapeDtypeStruct((S, A_W), F32),
                   jax.ShapeDtypeStruct((S, A_W), F32), jax.ShapeDtypeStruct((S, A_W), MXU_DTYPE),
                   jax.ShapeDtypeStruct((S, D), F32), jax.ShapeDtypeStruct((S, D), MXU_DTYPE),
                   jax.ShapeDtypeStruct((S, D), MXU_DTYPE), jax.ShapeDtypeStruct((8, D), F32),
                   jax.ShapeDtypeStruct((A_W, D), F32), jax.ShapeDtypeStruct((D, D), F32),
                   jax.ShapeDtypeStruct((D, D), F32)],
        scratch_shapes=[pltpu.VMEM((A_W, D), F32), pltpu.VMEM((D, D), F32), pltpu.VMEM((D, D), F32)],
        compiler_params=_params(("arbitrary",)),
    )(x, target, *os, *ls, proj, ob, proj, proj, mod3, final_g, wa, wb, wo)


def _dh(pieces, w_in_g, x, dx2, mod3, norm_g, grads, tm=256):
    S = x.shape[0]
    ni = S // tm
    ng = len(grads)

    def body(*refs):
        p_refs = refs[:N_DEV]
        w_ref, x_ref, dx2_ref, mod_ref, g_ref = refs[N_DEV:N_DEV + 5]
        g_ins = refs[N_DEV + 5:N_DEV + 5 + ng]
        gx_ref, sums_ref = refs[N_DEV + 5 + ng:N_DEV + 7 + ng]
        g_outs = refs[N_DEV + 7 + ng:N_DEV + 7 + 2 * ng]
        w_all, send_sems, recv_sems, local_sems = refs[N_DEV + 7 + 2 * ng:]
        i = pl.program_id(0)
        start, wait = _all_to_all_copies(g_ins, g_outs, send_sems, recv_sems, local_sems)

        @pl.when(i == 0)
        def _():
            start()
            sums_ref[...] = jnp.zeros_like(sums_ref)
            pltpu.sync_copy(w_ref, w_all)

        dh = _mm(p_refs[0][...], w_all[0], NT)
        for k in range(1, N_DEV):
            dh = dh + _mm(p_refs[k][...], w_all[k], NT)
        xv = x_ref[...]
        g = g_ref[...]
        sc1 = 1.0 + mod_ref[1:2, :]
        r = lax.rsqrt(jnp.mean(xv * xv, axis=-1, keepdims=True) + EPS)
        xn = xv * r
        sums_ref[0:1, :] += jnp.sum(dh, axis=0, keepdims=True)
        sums_ref[1:2, :] += jnp.sum(dh * (xn * g), axis=0, keepdims=True)
        sums_ref[2:3, :] += jnp.sum(dh * sc1 * xn, axis=0, keepdims=True)
        dxn = dh * sc1 * g
        gx_ref[...] = dx2_ref[...] + r * (dxn - xn * jnp.mean(dxn * xn, axis=-1, keepdims=True))

        @pl.when(i == ni - 1)
        def _():
            wait()

    row = pl.BlockSpec((tm, D), lambda i: (i, 0))
    any_spec = pl.BlockSpec(memory_space=pl.ANY)
    return pl.pallas_call(
        body, name="dh_scatter",
        grid=(ni,),
        in_specs=[row] * N_DEV
                 + [any_spec, row, row,
                    pl.BlockSpec((8, D), lambda i: (0, 0)),
                    pl.BlockSpec((1, D), lambda i: (0, 0))]
                 + [any_spec] * ng,
        out_specs=[row, pl.BlockSpec((8, D), lambda i: (0, 0))] + [any_spec] * ng,
        out_shape=[jax.ShapeDtypeStruct((S, D), F32), jax.ShapeDtypeStruct((8, D), F32)]
                  + [jax.ShapeDtypeStruct(g.shape, g.dtype) for g in grads],
        scratch_shapes=[pltpu.VMEM(w_in_g.shape, w_in_g.dtype),
                        pltpu.SemaphoreType.DMA((ng, N_DEV - 1)), pltpu.SemaphoreType.DMA((ng, N_DEV - 1)),
                        pltpu.SemaphoreType.DMA((ng,))],
        compiler_params=_params(("arbitrary",)),
    )(*pieces, w_in_g, x, dx2, mod3, norm_g, *grads)


def _gw_in(ht, pieces, grads, tm=512):
    S = ht.shape[1]
    nt = S // tm
    ng = len(grads)

    def body(*refs):
        h_ref, p_refs = refs[0], refs[1:1 + N_DEV]
        g_ins = refs[1 + N_DEV:1 + N_DEV + ng]
        o_ref = refs[1 + N_DEV + ng]
        g_outs = refs[2 + N_DEV + ng:2 + N_DEV + 2 * ng]
        acc, send_sems, recv_sems, local_sems = refs[2 + N_DEV + 2 * ng:]
        j, i = pl.program_id(0), pl.program_id(1)
        start, wait = _all_to_all_copies(g_ins, g_outs, send_sems, recv_sems, local_sems)

        @pl.when((j == 0) & (i == 0))
        def _():
            start()

        @pl.when(i == 0)
        def _():
            acc[...] = jnp.zeros_like(acc)

        for k in range(N_DEV):
            @pl.when(j == k)
            def _(k=k):
                acc[...] += _mm(h_ref[...], p_refs[k][...])

        @pl.when(i == nt - 1)
        def _():
            o_ref[0] = acc[...].astype(XCHG_DTYPE)

        @pl.when((j == N_DEV - 1) & (i == nt - 1))
        def _():
            wait()

    def piece(k):
        return pl.BlockSpec((tm, D), lambda j, i: (jnp.where(j == k, i, 0), 0))

    any_spec = pl.BlockSpec(memory_space=pl.ANY)
    return pl.pallas_call(
        body, name="gw_in_scatter",
        grid=(N_DEV, nt),
        in_specs=[pl.BlockSpec((D, tm), lambda j, i: (0, i))] + [piece(k) for k in range(N_DEV)] + [any_spec] * ng,
        out_specs=[pl.BlockSpec((1, D, D), lambda j, i: (j, 0, 0))] + [any_spec] * ng,
        out_shape=[jax.ShapeDtypeStruct((N_DEV, D, D), XCHG_DTYPE)]
                  + [jax.ShapeDtypeStruct(g.shape, g.dtype) for g in grads],
        scratch_shapes=[pltpu.VMEM((D, D), F32),
                        pltpu.SemaphoreType.DMA((ng, N_DEV - 1)), pltpu.SemaphoreType.DMA((ng, N_DEV - 1)),
                        pltpu.SemaphoreType.DMA((ng,))],
        compiler_params=_params(("arbitrary", "arbitrary")),
    )(ht, *pieces, *grads)


def _adamw_math(w, g, m, v):
    m = ADAM_B1 * m + (1.0 - ADAM_B1) * g
    v = ADAM_B2 * v + (1.0 - ADAM_B2) * (g * g)
    m_hat = m / (1.0 - ADAM_B1 ** ADAM_STEP)
    v_hat = v / (1.0 - ADAM_B2 ** ADAM_STEP)
    delta = -ADAM_LR * (m_hat / (jnp.sqrt(v_hat) + ADAM_EPS) + ADAM_WD * w)
    return delta, m, v


def _adamw_big(recv, w, m, v, name, tr=128):
    M, N = w.shape
    tr = min(tr, M)

    def body(r_ref, w_ref, m_ref, v_ref, g_ref, d_ref, nm_ref, nv_ref):
        g = r_ref[0].astype(F32)
        for j in range(1, N_DEV):
            g = g + r_ref[j].astype(F32)
        g_ref[...] = g
        d_ref[...], nm_ref[...], nv_ref[...] = _adamw_math(w_ref[...], g, m_ref[...], v_ref[...])

    blk = pl.BlockSpec((tr, N), lambda i: (i, 0))
    return pl.pallas_call(
        body, name=name,
        grid=(M // tr,),
        in_specs=[pl.BlockSpec((N_DEV, tr, N), lambda i: (0, i, 0)), blk, blk, blk],
        out_specs=[blk] * 4,
        out_shape=[jax.ShapeDtypeStruct((M, N), F32)] * 4,
        compiler_params=_params(("parallel",)),
    )(recv, w, m, v)


def _adamw_w_ada(c64, dmod64, w, m, v):
    def body(c_ref, dm_ref, w_ref, m_ref, v_ref, g_ref, d_ref, nm_ref, nv_ref):
        cv = c_ref[...]
        g = _mm(cv * _sigmoid(cv), dm_ref[...], TN)
        g_ref[...] = g
        d_ref[...], nm_ref[...], nv_ref[...] = _adamw_math(w_ref[...], g, m_ref[...], v_ref[...])

    return pl.pallas_call(
        body, name="adamw_w_ada",
        out_shape=[jax.ShapeDtypeStruct(w.shape, F32)] * 4,
        compiler_params=_params(),
    )(c64, dmod64, w, m, v)


P_MOD, P_NORM, P_ONORM, P_RELB, P_LB, P_FINAL, P_LOSS, P_END = (0, 3 * D, 4 * D, 5 * D, 6 * D, 7 * D, 8 * D, 9 * D)


def _adamw_small(packed, b_ada, norm_g, onorm_g, relb, hgrn_lb, final_g, ms, vs):
    def body(pk_ref, b_ref, ng_ref, og_ref, rb_ref, lb_ref, fg_ref,
             mb, mn, mo, mr, ml, mf, vb, vn, vo, vr, vl, vf,
             loss_ref, gb, gn, go, gr, gl, gf, db, dn, do, dr, dl, df,
             nmb, nmn, nmo, nmr, nml, nmf, nvb, nvn, nvo, nvr, nvl, nvf):
        tot = pk_ref[0:1, :]
        for j in range(1, N_DEV):
            tot = tot + pk_ref[8 * j:8 * j + 1, :]
        loss_ref[...] = jnp.broadcast_to(jnp.sum(tot[:, P_LOSS:P_END], axis=-1, keepdims=True), (8, 128))

        def upd(g, w_ref, m_ref, v_ref, g_out, d_out, m_out, v_out):
            g_out[...] = g
            d_out[...], m_out[...], v_out[...] = _adamw_math(w_ref[...], g, m_ref[...], v_ref[...])

        upd(tot[:, P_MOD:P_NORM], b_ref, mb, vb, gb, db, nmb, nvb)
        upd(tot[:, P_NORM:P_ONORM], ng_ref, mn, vn, gn, dn, nmn, nvn)
        g_on = tot[:, P_ONORM:P_ONORM + G_DK]
        for h in range(1, G_HEADS):
            g_on = g_on + tot[:, P_ONORM + G_DK * h:P_ONORM + G_DK * (h + 1)]
        upd(g_on, og_ref, mo, vo, go, do, nmo, nvo)
        upd(tot[:, P_RELB:P_LB], rb_ref, mr, vr, gr, dr, nmr, nvr)
        a = lb_ref[...]
        lb = _sigmoid(a[0:1, :] - a[1:2, :])
        g0 = tot[:, P_LB:P_FINAL] * lb * (1.0 - lb)
        row = lax.broadcasted_iota(jnp.int32, (2, D), 0)
        upd(jnp.where(row == 0, g0, -g0), lb_ref, ml, vl, gl, dl, nml, nvl)
        upd(tot[:, P_FINAL:P_LOSS], fg_ref, mf, vf, gf, df, nmf, nvf)

    shapes = [b_ada.shape, norm_g.shape, onorm_g.shape, relb.shape, hgrn_lb.shape, final_g.shape]
    outs = [jax.ShapeDtypeStruct((8, 128), F32)] + [jax.ShapeDtypeStruct(s, F32) for s in shapes] * 4
    return pl.pallas_call(
        body, name="adamw_small",
        out_shape=outs,
        compiler_params=_params(),
    )(packed, b_ada, norm_g, onorm_g, relb, hgrn_lb, final_g, *ms, *vs)


def _local_step(x, target, mod3, norm_g, w_in_g, onorm_g, wa_blk, wb_blk, wo_blk, rel_bias, hgrn_lb, final_g):
    buckets = jnp.asarray(_bucket_tables())
    bias = _bias_tables(rel_bias, buckets)
    proj, ht, wa_g, wb_g, wo_g = _inproj(x, mod3, norm_g, w_in_g, [wa_blk, wb_blk, wo_blk])
    wa = wa_g.transpose(1, 0, 2).reshape(A_W, D)
    wb = wb_g.reshape(D, D)
    wo = wo_g.reshape(D, D)
    os, ls = [], []
    for p, (_, d) in enumerate(PATTERNS):
        o, l = _attn_fwd(proj, bias[p], d, "attn_fwd_d%d" % d)
        os.append(o)
        ls.append(l)
    o_raw, ob, states = _hgrn_fwd(proj, hgrn_lb, onorm_g)
    lt, dx2, do, delta, dza, dob, dga, dgb, tsums, gwa, gwb, gwo = _tail(
        x, target, os, ls, ob, proj, mod3, final_g, wa, wb, wo)
    dqs, dks, dvs, dbs = [], [], [], []
    for p, (_, d) in enumerate(PATTERNS):
        dq, dk, dv, db = _attn_bwd(proj, do, lt, delta, bias[p], d, "attn_bwd_d%d" % d)
        dqs.append(dq)
        dks.append(dk)
        dvs.append(dv)
        dbs.append(db)
    p0, p1 = _attn_assemble(dqs, dks, dvs, dza)
    g_relb = _rel_bias_grad(dbs, buckets)
    dqb, dfb, dib, dzb, dlb, dgo = _hgrn_bwd(proj, o_raw, dob, states, hgrn_lb, onorm_g)
    pieces = [p0, p1, dqb, dfb, dib, dzb, dga, dgb]
    small = [gwa.astype(XCHG_DTYPE).reshape(A_W, N_DEV, D // N_DEV).transpose(1, 0, 2),
             gwb.astype(XCHG_DTYPE).reshape(N_DEV, D // N_DEV, D),
             gwo.astype(XCHG_DTYPE).reshape(N_DEV, D // N_DEV, D)]
    gw_in, *received_small = _gw_in(ht, pieces, small)
    gx, hsums, received_in = _dh(pieces, w_in_g, x, dx2, mod3, norm_g, [gw_in])
    received = [received_in] + received_small
    row = jnp.concatenate([
        hsums[0], hsums[1], tsums[1],
        hsums[2],
        dgo.reshape(G_HEADS, 8, G_DK)[:, 0].reshape(-1),
        g_relb.reshape(-1),
        dlb[0],
        tsums[0],
        tsums[2],
    ])
    return gx, received, row


def kernel(x, c, w_ada, b_ada, norm_g, w_in, hgrn_onorm_g, w_branch_a, w_branch_b, w_out, rel_bias, hgrn_lb, final_g, loss_target, m_w_ada, m_b_ada, m_norm_g, m_w_in, m_hgrn_onorm_g, m_w_branch_a, m_w_branch_b, m_w_out, m_rel_bias, m_hgrn_lb, m_final_g, v_w_ada, v_b_ada, v_norm_g, v_w_in, v_hgrn_onorm_g, v_w_branch_a, v_w_branch_b, v_w_out, v_rel_bias, v_hgrn_lb, v_final_g):
    me = 4 * lax.axis_index("x") + 2 * lax.axis_index("y") + lax.axis_index("c")
    n_ada = w_ada.shape[2]

    w_in_g, c_all = _all_gather([w_in[0].astype(MXU_DTYPE), jnp.broadcast_to(c, (8, D))], "gather_w_in_c")

    c64 = c_all.reshape(8 * N_DEV, D)
    b_loc = lax.dynamic_slice(b_ada, (0, me * n_ada), (1, n_ada))
    mod_part = _mod_fwd(c64, w_ada[0], b_loc)[::8]
    (mod_all,) = _all_gather([mod_part], "gather_mod")
    mod = lax.dynamic_slice(mod_all, (0, me, 0), (N_DEV, 1, n_ada)).reshape(3, D)
    mod3 = jnp.concatenate([mod, jnp.zeros((5, D), F32)], axis=0)

    onorm_t = hgrn_onorm_g
    gx, (r_in, r_a, r_b, r_o), row = _local_step(
        x[0], loss_target[0], mod3, norm_g, w_in_g, onorm_t, w_branch_a[0].astype(MXU_DTYPE),
        w_branch_b[0].astype(MXU_DTYPE), w_out[0].astype(MXU_DTYPE), rel_bias, hgrn_lb,
        final_g.reshape(1, D))
    packed8 = jnp.concatenate([row[None, :], jnp.zeros((7, P_END), F32)], axis=0)
    (packed,) = _all_gather([packed8], "gather_small")
    packed = packed.reshape(8 * N_DEV, P_END)

    g_in, d_in, nm_in, nv_in = _adamw_big(r_in, w_in[0], m_w_in[0], v_w_in[0], "adamw_w_in")
    g_a, d_a, nm_a, nv_a = _adamw_big(r_a, w_branch_a[0], m_w_branch_a[0], v_w_branch_a[0], "adamw_w_branch_a")
    g_b, d_b, nm_b, nv_b = _adamw_big(r_b, w_branch_b[0], m_w_branch_b[0], v_w_branch_b[0], "adamw_w_branch_b")
    g_o, d_o, nm_o, nv_o = _adamw_big(r_o, w_out[0], m_w_out[0], v_w_out[0], "adamw_w_out")

    dmod64 = lax.dynamic_slice(packed, (0, P_MOD + me * n_ada), (8 * N_DEV, n_ada))
    g_ada, d_ada, nm_ada, nv_ada = _adamw_w_ada(c64, dmod64, w_ada[0], m_w_ada[0], v_w_ada[0])

    def flat_relb(t):
        return jnp.pad(t.T, ((0, 0), (0, 128 - N_BUCKETS))).reshape(1, A_HEADS * 128)

    def unflat_relb(t):
        return t.reshape(A_HEADS, 128)[:, :N_BUCKETS].T

    fg2 = lambda t: t.reshape(1, D)
    smalls = _adamw_small(
        packed, b_ada, norm_g, hgrn_onorm_g, flat_relb(rel_bias), hgrn_lb, fg2(final_g),
        [m_b_ada, m_norm_g, m_hgrn_onorm_g, flat_relb(m_rel_bias), m_hgrn_lb, fg2(m_final_g)],
        [v_b_ada, v_norm_g, v_hgrn_onorm_g, flat_relb(v_rel_bias), v_hgrn_lb, fg2(v_final_g)])
    loss = smalls[0][0, 0]

    def small(kind):
        s = smalls[1 + 6 * kind:7 + 6 * kind]
        return s[0], s[1], s[2], unflat_relb(s[3]), s[4], s[5].reshape(D)

    def leaves(ada, sm, w_in_, wa_, wb_, wo_):
        b_, n_, o_, r_, l_, f_ = sm
        return (ada[None], b_, n_, w_in_[None], o_, wa_[None], wb_[None], wo_[None], r_, l_, f_)

    return (loss, gx[None],
            *leaves(g_ada, small(0), g_in, g_a, g_b, g_o),
            *leaves(d_ada, small(1), d_in, d_a, d_b, d_o),
            *leaves(nm_ada, small(2), nm_in, nm_a, nm_b, nm_o),
            *leaves(nv_ada, small(3), nv_in, nv_a, nv_b, nv_o))
```

```python
import functools
import math

import numpy as np
import jax
import jax.numpy as jnp
from jax import lax
from jax.experimental import pallas as pl
from jax.experimental.pallas import tpu as pltpu

F32 = jnp.float32
BF16 = jnp.bfloat16
MXU_DTYPE = jnp.bfloat16
XCHG_DTYPE = jnp.bfloat16

N_DEV = 8
D = 1024
A_HEADS = 8
A_HD = 64
A_W = A_HEADS * A_HD
A_BLK = 128
PATTERNS = ((128, 1), (512, 4), (2048, 16))
N_BUCKETS = 32
MAX_DISTANCE = 2048
NEG = -1e30
G_HEADS = 8
G_DK = 128
G_W = G_HEADS * G_DK
IN_W = 8 * D
EPS = 1e-6
ADAM_LR = 0.001
ADAM_B1 = 0.9
ADAM_B2 = 0.999
ADAM_EPS = 1e-08
ADAM_WD = 0.01
ADAM_STEP = 10

G_CHUNK = 128
G_SUB = 8
G_HPS_FWD = 8
G_HPS_BWD = 4
G_RB = 16
VMEM_LIMIT = 56 * 1024 * 1024

NN = (((1,), (0,)), ((), ()))
NT = (((1,), (1,)), ((), ()))
TN = (((0,), (0,)), ((), ()))
MESH = pl.DeviceIdType.MESH


def _mm(a, b, dims=NN):
    return lax.dot_general(a.astype(MXU_DTYPE), b.astype(MXU_DTYPE), dims,
                           preferred_element_type=F32)


def _mm_exact(t, x):
    hi = x.astype(BF16)
    r = x - hi.astype(F32)
    mid = r.astype(BF16)
    lo = (r - mid.astype(F32)).astype(BF16)
    tb = t.astype(BF16)
    return sum(lax.dot_general(tb, p, NN, preferred_element_type=F32) for p in (hi, mid, lo))


def _split(x):
    hi = x.astype(BF16)
    return hi, (x - hi.astype(F32)).astype(BF16)


def _mm_split(a, b, dims):
    dot = lambda p, q: lax.dot_general(p, q, dims, preferred_element_type=F32)
    return dot(a[0], b[0]) + dot(a[0], b[1]) + dot(a[1], b[0])


def _sigmoid(x):
    return 0.5 * jnp.tanh(0.5 * x) + 0.5


def _params(sem=None):
    return pltpu.CompilerParams(dimension_semantics=sem, vmem_limit_bytes=VMEM_LIMIT)


def _all_gather(xs, name):
    n = len(xs)

    def body(*refs):
        ins, outs = refs[:n], refs[n:2 * n]
        send_sems, recv_sems, local_sems = refs[2 * n:]
        x, y, c = lax.axis_index("x"), lax.axis_index("y"), lax.axis_index("c")
        me, sibling = (x, y, c), (x, y, 1 - c)
        chips = [(1 - x, y), (x, 1 - y), (1 - x, 1 - y)]

        def slot(ref, dev):
            return ref.at[4 * dev[0] + 2 * dev[1] + dev[2]]

        def copy(a, k, block, to, src=None):
            return pltpu.make_async_remote_copy(
                src_ref=slot(outs[a], block) if src is None else src,
                dst_ref=slot(outs[a], block),
                send_sem=send_sems.at[a, k], recv_sem=recv_sems.at[a, k],
                device_id=to, device_id_type=MESH)

        mine, first, passed = [], [], []
        for a in range(n):
            cp = pltpu.make_async_copy(ins[a], slot(outs[a], me), local_sems.at[a])
            cp.start()
            mine.append(cp)
            first.append(copy(a, 0, me, sibling, src=ins[a]))
            for j, chip in enumerate(chips):
                first.append(copy(a, 1 + j, me, (*chip, c), src=ins[a]))
        for cp in first:
            cp.start()
        for j, chip in enumerate(chips):
            for a in range(n):
                copy(a, 1 + j, (*chip, c), me).wait_recv()
                cp = copy(a, 4 + j, (*chip, c), sibling)
                cp.start()
                passed.append(cp)
        for a in range(n):
            copy(a, 0, sibling, me).wait_recv()
            for j, chip in enumerate(chips):
                copy(a, 4 + j, (*chip, 1 - c), me).wait_recv()
        for cp in first + passed:
            cp.wait_send()
        for cp in mine:
            cp.wait()

    any_spec = pl.BlockSpec(memory_space=pl.ANY)
    return pl.pallas_call(
        body, name=name,
        out_shape=[jax.ShapeDtypeStruct((N_DEV,) + v.shape, v.dtype) for v in xs],
        in_specs=[any_spec] * n, out_specs=[any_spec] * n,
        scratch_shapes=[pltpu.SemaphoreType.DMA((n, 7)), pltpu.SemaphoreType.DMA((n, 7)),
                        pltpu.SemaphoreType.DMA((n,))],
    )(*xs)


def _all_to_all_copies(ins, outs, send_sems, recv_sems, local_sems, gather=False):
    n = len(ins)
    x, y, c = lax.axis_index("x"), lax.axis_index("y"), lax.axis_index("c")
    me = 4 * x + 2 * y + c
    peers = []
    for m in range(1, N_DEV):
        peers.append((1 - x if m & 4 else x, 1 - y if m & 2 else y, 1 - c if m & 1 else c))

    def chunk(a, j):
        return ins[a] if gather else ins[a].at[j]

    def copy(a, k, landing):
        peer = peers[k]
        pid = 4 * peer[0] + 2 * peer[1] + peer[2]
        return pltpu.make_async_remote_copy(
            src_ref=chunk(a, pid), dst_ref=outs[a].at[pid if landing else me],
            send_sem=send_sems.at[a, k], recv_sem=recv_sems.at[a, k],
            device_id=peer, device_id_type=MESH)

    def local(a):
        return pltpu.make_async_copy(chunk(a, me), outs[a].at[me], local_sems.at[a])

    def start():
        for a in range(n):
            local(a).start()
        for k in range(N_DEV - 1):
            for a in range(n):
                copy(a, k, False).start()

    def wait():
        for k in range(N_DEV - 1):
            for a in range(n):
                copy(a, k, True).wait_recv()
        for k in range(N_DEV - 1):
            for a in range(n):
                copy(a, k, False).wait_send()
        for a in range(n):
            local(a).wait()

    return start, wait


def _mod_fwd(c64, w_ada, b_loc):
    def body(c_ref, w_ref, b_ref, o_ref):
        cv = c_ref[...]
        sc = cv * _sigmoid(cv)
        o_ref[...] = _mm(sc, w_ref[...]) + b_ref[...]

    return pl.pallas_call(
        body, name="mod_fwd",
        out_shape=jax.ShapeDtypeStruct((c64.shape[0], w_ada.shape[1]), F32),
        compiler_params=_params(),
    )(c64, w_ada, b_loc)


def _inproj(x, mod3, norm_g, w_in_g, blocks, tm=256):
    S = x.shape[0]
    ni = S // tm
    nb = len(blocks)

    def body(*refs):
        x_ref, mod_ref, g_ref, w_ref = refs[:4]
        b_ins = refs[4:4 + nb]
        proj_ref, ht_ref = refs[4 + nb:6 + nb]
        b_outs = refs[6 + nb:6 + 2 * nb]
        w_all, send_sems, recv_sems, local_sems = refs[6 + 2 * nb:]
        i = pl.program_id(0)
        start, wait = _all_to_all_copies(b_ins, b_outs, send_sems, recv_sems, local_sems, gather=True)

        @pl.when(i == 0)
        def _():
            start()
            pltpu.sync_copy(w_ref, w_all)

        xv = x_ref[...]
        r = lax.rsqrt(jnp.mean(xv * xv, axis=-1, keepdims=True) + EPS)
        h = ((xv * r * g_ref[...]) * (1.0 + mod_ref[1:2, :]) + mod_ref[0:1, :]).astype(MXU_DTYPE)
        ht_ref[...] = h.T
        for j in range(N_DEV):
            proj_ref[:, j * D:(j + 1) * D] = _mm(h, w_all[j])

        @pl.when(i == ni - 1)
        def _():
            wait()

    any_spec = pl.BlockSpec(memory_space=pl.ANY)
    return pl.pallas_call(
        body, name="inproj_gather",
        grid=(ni,),
        in_specs=[pl.BlockSpec((tm, D), lambda i: (i, 0)),
                  pl.BlockSpec((8, D), lambda i: (0, 0)),
                  pl.BlockSpec((1, D), lambda i: (0, 0)),
                  any_spec] + [any_spec] * nb,
        out_specs=[pl.BlockSpec((tm, IN_W), lambda i: (i, 0)),
                   pl.BlockSpec((D, tm), lambda i: (0, i))] + [any_spec] * nb,
        out_shape=[jax.ShapeDtypeStruct((S, IN_W), F32), jax.ShapeDtypeStruct((D, S), MXU_DTYPE)]
                  + [jax.ShapeDtypeStruct((N_DEV,) + b.shape, b.dtype) for b in blocks],
        scratch_shapes=[pltpu.VMEM(w_in_g.shape, w_in_g.dtype),
                        pltpu.SemaphoreType.DMA((nb, N_DEV - 1)), pltpu.SemaphoreType.DMA((nb, N_DEV - 1)),
                        pltpu.SemaphoreType.DMA((nb,))],
        compiler_params=_params(("arbitrary",)),
    )(x, mod3, norm_g, w_in_g, *blocks)


def _bucket_tables():
    qi = np.arange(A_BLK)[:, None]
    kj = np.arange(2 * A_BLK)[None, :]
    delta = qi + A_BLK - kj
    out = []
    for window, dil in PATTERNS:
        span = window // dil
        band = (delta >= 0) & (delta <= span)
        dist = np.clip(delta, 0, None) * dil
        max_exact = N_BUCKETS // 2
        nf = dist.astype(np.float32)
        large = max_exact + (np.log(np.maximum(nf, np.float32(1.0)) / np.float32(max_exact))
                             / np.float32(math.log(MAX_DISTANCE / max_exact))
                             * np.float32(N_BUCKETS - max_exact)).astype(np.int32)
        large = np.minimum(large, N_BUCKETS - 1)
        bucket = np.where(dist < max_exact, dist, large)
        out.append(np.where(band, bucket, -1).astype(np.int32))
    return np.stack(out)


def _bias_tables(rel_bias, buckets):
    def body(rb_ref, bk_ref, o_ref):
        h = pl.program_id(1)
        bk = bk_ref[0]
        acc = jnp.full(bk.shape, NEG, F32)
        for b in range(N_BUCKETS):
            acc = jnp.where(bk == b, rb_ref[b, h], acc)
        o_ref[0, 0] = acc

    return pl.pallas_call(
        body, name="bias_tables",
        grid=(3, A_HEADS),
        in_specs=[pl.BlockSpec(memory_space=pltpu.SMEM),
                  pl.BlockSpec((1, A_BLK, 2 * A_BLK), lambda p, h: (p, 0, 0))],
        out_specs=pl.BlockSpec((1, 1, A_BLK, 2 * A_BLK), lambda p, h: (p, h, 0, 0)),
        out_shape=jax.ShapeDtypeStruct((3, A_HEADS, A_BLK, 2 * A_BLK), F32),
        compiler_params=_params(("arbitrary", "arbitrary")),
    )(rel_bias, buckets)


A_TILES = 16


def _attn_heads_per_step(d):
    return A_HEADS if d == 1 else 2


def _attn_in_specs(sb, nsb, hw):
    w = A_HD * hw
    per = A_W // w

    def cur(col):
        return pl.BlockSpec((sb, w), lambda hp, n: (jnp.minimum(n, nsb - 1), per * col + hp))

    def prev(col):
        return pl.BlockSpec((sb, w), lambda hp, n: (jnp.maximum(jnp.minimum(n, nsb - 1) - 1, 0), per * col + hp))

    return [cur(0), prev(1), cur(1), prev(2), cur(2)]


def _rows(r, d):
    return pl.ds(r, A_BLK) if d == 1 else pl.ds(r, A_BLK, stride=d)


def _for_residues(d, hw, fn):
    unroll = min(d, max(1, A_TILES // hw))
    if d == unroll:
        _round_robin([g for r in range(d) for g in fn(r)])
    else:
        def group(g, c):
            _round_robin([t for u in range(unroll) for t in fn(g * unroll + u)])
            return c
        lax.fori_loop(0, d // unroll, group, 0)


def _attn_stack(t):
    first_half = lax.broadcasted_iota(jnp.int32, (1, 2 * A_HD), 1) < A_HD
    return jnp.concatenate([jnp.where(first_half, t, 0.0), jnp.where(first_half, 0.0, t)], axis=0)


def _attn_unstack(t2):
    first_half = lax.broadcasted_iota(jnp.int32, (1, 2 * A_HD), 1) < A_HD
    return jnp.where(first_half, t2[:A_BLK], t2[A_BLK:])


def _attn_scores(q, k, b_ref, pp, first):
    bias = jnp.concatenate([b_ref[2 * pp] + first, b_ref[2 * pp + 1] + first], axis=0)
    return _mm(_attn_stack(q), k, NT) * (A_HD ** -0.5) + bias


def _attn_fwd(proj, bias_p, d, name):
    S = proj.shape[0]
    sb = A_BLK * d
    nsb = S // sb
    hw = _attn_heads_per_step(d)

    def body(q_ref, kp_ref, kc_ref, vp_ref, vc_ref, b_ref, o_ref, l_ref):
        n = pl.program_id(1)
        kj = lax.broadcasted_iota(jnp.int32, (A_BLK, 2 * A_BLK), 1)
        first = jnp.where((n == 0) & (kj < A_BLK), NEG, 0.0).astype(F32)

        def residue(r):
            rows = _rows(r, d)

            def pair(pp):
                lanes = pl.ds(2 * A_HD * pp, 2 * A_HD)
                k = jnp.concatenate([kp_ref[rows, lanes], kc_ref[rows, lanes]], axis=0)
                v = jnp.concatenate([vp_ref[rows, lanes], vc_ref[rows, lanes]], axis=0)
                s = _attn_scores(q_ref[rows, lanes], k, b_ref, pp, first)
                yield
                m = jnp.max(s, axis=-1, keepdims=True)
                p = jnp.exp(s - m)
                den = jnp.sum(p, axis=-1, keepdims=True)
                pv = _mm(p, v)
                yield
                o_ref[rows, lanes] = _attn_unstack(pv / den)
                l_ref[rows, lanes] = _attn_unstack(jnp.broadcast_to(m + jnp.log(den), (2 * A_BLK, 2 * A_HD)))

            return [pair(pp) for pp in range(hw // 2)]

        _for_residues(d, hw, residue)

    out = pl.BlockSpec((sb, A_HD * hw), lambda hp, n: (n, hp))
    return pl.pallas_call(
        body, name=name,
        grid=(A_HEADS // hw, nsb),
        in_specs=_attn_in_specs(sb, nsb, hw) + [pl.BlockSpec((hw, A_BLK, 2 * A_BLK), lambda hp, n: (hp, 0, 0))],
        out_specs=[out, out],
        out_shape=[jax.ShapeDtypeStruct((S, A_W), F32)] * 2,
        compiler_params=_params(("parallel", "parallel")),
    )(proj, proj, proj, proj, proj, bias_p)


def _attn_bwd(proj, do, lt, delta, bias_p, d, name, prev=(), out_dtype=F32):
    S = proj.shape[0]
    sb = A_BLK * d
    nsb = S // sb
    hw = _attn_heads_per_step(d)

    def body(*refs):
        q_ref, kp_ref, kc_ref, vp_ref, vc_ref, do_ref, lt_ref, dl_ref, b_ref = refs[:9]
        pq_ref, pk_ref, pv_ref = refs[9:9 + len(prev)] if prev else (None, None, None)
        dq_ref, dk_ref, dv_ref, db_ref, ck, cv = refs[9 + len(prev):]
        n = pl.program_id(1)
        plus = lambda t, p_ref, idx: (t if p_ref is None else t + p_ref[idx]).astype(out_dtype)

        @pl.when(n == 0)
        def _():
            db_ref[...] = jnp.zeros_like(db_ref)
            ck[...] = jnp.zeros_like(ck)
            cv[...] = jnp.zeros_like(cv)

        @pl.when(n < nsb)
        def _():
            kj = lax.broadcasted_iota(jnp.int32, (A_BLK, 2 * A_BLK), 1)
            first = jnp.where((n == 0) & (kj < A_BLK), NEG, 0.0).astype(F32)

            def residue(r):
                rows = _rows(r, d)

                def pair(pp):
                    lanes = pl.ds(2 * A_HD * pp, 2 * A_HD)
                    lt_r, dl_r = lt_ref[rows, lanes], dl_ref[rows, lanes]
                    k = jnp.concatenate([kp_ref[rows, lanes], kc_ref[rows, lanes]], axis=0)
                    v = jnp.concatenate([vp_ref[rows, lanes], vc_ref[rows, lanes]], axis=0)
                    q2 = _attn_stack(q_ref[rows, lanes])
                    do2 = _attn_stack(do_ref[rows, lanes])
                    col = lambda t: jnp.concatenate([t[:, 0:1], t[:, A_HD:A_HD + 1]], axis=0)
                    s = _attn_scores(q_ref[rows, lanes], k, b_ref, pp, first)
                    dp = _mm(do2, v, NT)
                    yield
                    p = jnp.exp(s - col(lt_r))
                    ds = p * (dp - col(dl_r))
                    db_ref[2 * pp] += ds[:A_BLK]
                    db_ref[2 * pp + 1] += ds[A_BLK:]
                    dq = _mm(ds, k)
                    dk = _mm(ds, q2, TN) * (A_HD ** -0.5)
                    dv = _mm(p, do2, TN)
                    yield
                    dq_ref[rows, lanes] = plus(_attn_unstack(dq) * (A_HD ** -0.5), pq_ref, (rows, lanes))
                    dk_ref[rows, lanes] = plus(ck[rows, lanes] + dk[:A_BLK], pk_ref, (rows, lanes))
                    dv_ref[rows, lanes] = plus(cv[rows, lanes] + dv[:A_BLK], pv_ref, (rows, lanes))
                    ck[rows, lanes] = dk[A_BLK:]
                    cv[rows, lanes] = dv[A_BLK:]

                return [pair(pp) for pp in range(hw // 2)]

            _for_residues(d, hw, residue)

        @pl.when(n == nsb)
        def _():
            dk_ref[...] = plus(ck[...], pk_ref, ...)
            dv_ref[...] = plus(cv[...], pv_ref, ...)

    w = A_HD * hw
    row = pl.BlockSpec((sb, w), lambda hp, n: (jnp.minimum(n, nsb - 1), hp))
    lag = pl.BlockSpec((sb, w), lambda hp, n: (jnp.maximum(n - 1, 0), hp))
    tab = pl.BlockSpec((hw, A_BLK, 2 * A_BLK), lambda hp, n: (hp, 0, 0))
    return pl.pallas_call(
        body, name=name,
        grid=(A_HEADS // hw, nsb + 1),
        in_specs=_attn_in_specs(sb, nsb, hw) + [row, row, row, tab] + ([row, lag, lag] if prev else []),
        out_specs=[row, lag, lag, tab],
        out_shape=[jax.ShapeDtypeStruct((S, A_W), out_dtype)] * 3
                  + [jax.ShapeDtypeStruct((A_HEADS, A_BLK, 2 * A_BLK), F32)],
        scratch_shapes=[pltpu.VMEM((sb, w), F32), pltpu.VMEM((sb, w), F32)],
        compiler_params=_params(("parallel", "arbitrary")),
    )(proj, proj, proj, proj, proj, do, lt, delta, bias_p, *prev)


def _rel_bias_grad(dbs, buckets):
    def body(d1, d2, d3, bk_ref, o_ref):
        row = lax.broadcasted_iota(jnp.int32, (A_HEADS, 128), 0)
        lane = lax.broadcasted_iota(jnp.int32, (A_HEADS, 128), 1)
        acc = jnp.zeros((A_HEADS, 128), F32)
        for p, dref in enumerate((d1, d2, d3)):
            bk = bk_ref[p]
            for h in range(A_HEADS):
                ds = dref[h]
                for b in range(N_BUCKETS):
                    s = jnp.sum(jnp.where(bk == b, ds, 0.0), keepdims=True)
                    acc = acc + jnp.where((row == h) & (lane == b), s, 0.0)
        o_ref[...] = acc

    return pl.pallas_call(
        body, name="rel_bias_grad",
        out_shape=jax.ShapeDtypeStruct((A_HEADS, 128), F32),
        compiler_params=_params(),
    )(*dbs, buckets)


def _tri(c):
    t = np.tril(np.ones((c, c), np.float32))
    return jnp.asarray(t), jnp.asarray(t.T.copy())


def _fill_above(ref, x, pad):
    ref[0:G_SUB, :] = jnp.full((G_SUB, x.shape[1]), pad, F32)
    ref[G_SUB:, :] = x


def _fill_below(ref, x, pad):
    ref[0:x.shape[0], :] = x
    ref[x.shape[0]:, :] = jnp.full((G_SUB, x.shape[1]), pad, F32)


def _hgrn_gates(q_ref, f_ref, lbp_ref, tri_ref):
    qraw = q_ref[...]
    sq = _sigmoid(qraw)
    q = qraw * sq
    sg = _sigmoid(f_ref[...])
    lb = _sigmoid(lbp_ref[0:1, :] - lbp_ref[1:2, :])
    f = lb + (1.0 - lb) * sg
    k = 1.0 - f
    b = _mm_exact(tri_ref[...], jnp.log(f))
    return qraw, sq, q, sg, lb, f, k, b


def _hgrn_col(C, base, idx, hps):
    return pl.BlockSpec((C, hps * G_DK), lambda h, n: (idx(n), base * (G_HEADS // hps) + h))


def _round_robin(stages):
    live = list(stages)
    while live:
        nxt = []
        for g in live:
            try:
                next(g)
                nxt.append(g)
            except StopIteration:
                pass
        live = nxt


def _hgrn_levels(C):
    out, m = [], G_SUB
    while 2 * m <= C:
        out.append(m)
        m *= 2
    return out


def _hgrn_level_masks(C):
    ti = np.arange(C)[:, None]
    si = np.arange(C)[None, :]
    return jnp.asarray(np.stack([((ti // (2 * m) == si // (2 * m)) & (ti - si >= G_SUB)).astype(np.float32)
                                 for m in _hgrn_levels(C)]))


def _hgrn_level(b, q, k, C, m):
    zeros = jnp.zeros((m, G_DK), F32)
    eq, ek, qt, kt = [], [], [], []
    for blk in range(0, C // m, 2):
        lo, mid, hi = blk * m, (blk + 1) * m, (blk + 2) * m
        ref = b[mid:mid + 1]
        e_right = jnp.exp(b[mid:hi] - ref)
        e_left = jnp.exp(ref - b[lo:mid])
        eq += [zeros, e_right]
        ek += [e_left, zeros]
        qt += [zeros, q[mid:hi] * e_right]
        kt += [k[lo:mid] * e_left, zeros]
    cat = lambda parts: jnp.concatenate(parts, axis=0)
    return cat(qt), cat(kt), cat(eq), cat(ek)


def _hgrn_fwd(proj, hgrn_lb, onorm_g, C=G_CHUNK):
    S = proj.shape[0]
    nc = S // C
    tri, _ = _tri(C)
    masks = _hgrn_level_masks(C)
    hps = G_HPS_FWD

    def body(q_ref, f_ref, i_ref, z_ref, lbp_ref, go_ref, tri_ref, pm_ref, o_ref, ob_ref, st_ref, St, kp, vp, fp):
        @pl.when(pl.program_id(1) == 0)
        def _():
            St[...] = jnp.zeros_like(St)

        heads = []
        for hh in range(hps):
            ln = pl.ds(G_DK * hh, G_DK)
            heads.append(head(
                q_ref.at[:, ln], f_ref.at[:, ln], i_ref.at[:, ln], z_ref.at[:, ln], lbp_ref.at[:, ln], go_ref,
                tri_ref, pm_ref, o_ref.at[:, ln], ob_ref.at[:, ln], st_ref.at[0, hh], St.at[hh], kp.at[hh], vp.at[hh],
                fp.at[hh]))
        _round_robin(heads)

    def head(q_ref, f_ref, i_ref, z_ref, lbp_ref, go_ref, tri_ref, pm_ref, o_ref, ob_ref, st_ref, St, kp, vp, fp):
        _, _, q, _, _, f, k, b = _hgrn_gates(q_ref, f_ref, lbp_ref, tri_ref)
        v = i_ref[...]
        bC = b[C - 1:C, :]
        S0 = St[...]
        o = _mm(q * jnp.exp(b), S0, NT)
        yield
        _fill_above(kp, k, 0.0)
        _fill_above(vp, v, 0.0)
        _fill_above(fp, f, 1.0)
        near = []
        for r0 in range(0, C, G_RB):
            qb = q[r0:r0 + G_RB]
            acc = e = None
            for l in range(G_SUB):
                rows = pl.ds(G_SUB - l + r0, G_RB)
                if l > 0:
                    fl = fp[pl.ds(G_SUB - l + 1 + r0, G_RB), :]
                    e = fl if e is None else e * fl
                kl = kp[rows, :]
                a = jnp.sum(qb * kl if e is None else qb * kl * e, axis=-1, keepdims=True)
                t = a * vp[rows, :]
                acc = t if acc is None else acc + t
            near.append(acc)
        o = o + jnp.concatenate(near, axis=0)
        yield
        a_off = jnp.zeros((C, C), F32)
        for lv, m in enumerate(_hgrn_levels(C)):
            qt, kt, _, _ = _hgrn_level(b, q, k, C, m)
            prod = _mm_split(_split(qt), _split(kt), NT) if m == G_SUB else _mm(qt, kt, NT)
            a_off = a_off + pm_ref[lv] * prod
        yield
        o = o + _mm(a_off, v)
        S1 = S0 * jnp.exp(bC) + _mm(v, k * jnp.exp(bC - b), TN)
        St[...] = S1
        st_ref[...] = S1
        o_ref[...] = o
        r = lax.rsqrt(jnp.mean(o * o, axis=-1, keepdims=True) + EPS)
        z = z_ref[...]
        ob_ref[...] = (o * r * go_ref[...] * (z * _sigmoid(z))).astype(MXU_DTYPE)

    ident = lambda n: n
    w = hps * G_DK
    out = pl.BlockSpec((C, w), lambda h, n: (n, h))
    return pl.pallas_call(
        body, name="hgrn_fwd",
        grid=(G_HEADS // hps, nc),
        in_specs=[_hgrn_col(C, base, ident, hps) for base in (2, 3, 4, 5)] + [
                  pl.BlockSpec((2, w), lambda h, n: (0, h)),
                  pl.BlockSpec((1, G_DK), lambda h, n: (0, 0)),
                  pl.BlockSpec((C, C), lambda h, n: (0, 0)),
                  pl.BlockSpec(masks.shape, lambda h, n: (0, 0, 0))],
        out_specs=[out, out, pl.BlockSpec((1, hps, G_DK, G_DK), lambda h, n: (n, h, 0, 0))],
        out_shape=[jax.ShapeDtypeStruct((S, G_W), F32), jax.ShapeDtypeStruct((S, G_W), MXU_DTYPE),
                   jax.ShapeDtypeStruct((nc, G_HEADS, G_DK, G_DK), F32)],
        scratch_shapes=[pltpu.VMEM((hps, G_DK, G_DK), F32)] + [pltpu.VMEM((hps, C + G_SUB, G_DK), F32)] * 3,
        compiler_params=_params(("parallel", "arbitrary")),
    )(proj, proj, proj, proj, hgrn_lb, onorm_g, tri, masks)


def _hgrn_bwd(proj, o_raw, dob, states, hgrn_lb, onorm_g, C=G_CHUNK):
    S = proj.shape[0]
    nc = S // C
    tri, triu = _tri(C)
    masks = _hgrn_level_masks(C)
    hps = G_HPS_BWD

    def body(q_ref, f_ref, i_ref, z_ref, o_ref, dob_ref, s0_ref, s1_ref, lbp_ref, go_ref, tri_ref, triu_ref,
             pm_ref, dq_ref, df_ref, di_ref, dz_ref, dlb_ref, dgo_ref, dSt, *shifted):
        @pl.when(pl.program_id(1) == 0)
        def _():
            dSt[...] = jnp.zeros_like(dSt)
            dlb_ref[...] = jnp.zeros_like(dlb_ref)
            dgo_ref[...] = jnp.zeros_like(dgo_ref)

        heads = []
        for hh in range(hps):
            ln = pl.ds(G_DK * hh, G_DK)
            heads.append(head(
                q_ref.at[:, ln], f_ref.at[:, ln], i_ref.at[:, ln], z_ref.at[:, ln], o_ref.at[:, ln],
                dob_ref.at[:, ln], s0_ref.at[0, hh], s1_ref.at[0, hh], lbp_ref.at[:, ln], go_ref, tri_ref, triu_ref,
                pm_ref, dq_ref.at[:, ln], df_ref.at[:, ln], di_ref.at[:, ln], dz_ref.at[:, ln], dlb_ref.at[:, ln],
                dgo_ref.at[pl.ds(8 * hh, 8), :], dSt.at[hh], *[t.at[hh] for t in shifted]))
        _round_robin(heads)

    def head(q_ref, f_ref, i_ref, z_ref, o_ref, dob_ref, s0_ref, s1_ref, lbp_ref, go_ref, tri_ref, triu_ref,
             pm_ref, dq_ref, df_ref, di_ref, dz_ref, dlb_ref, dgo_ref, dSt, kp, vp, fp, qn, dn_, fn, xs, dac):
        cn = nc - 1 - pl.program_id(1)
        qraw, sq, q, sg, lb, f, k, b = _hgrn_gates(q_ref, f_ref, lbp_ref, tri_ref)
        v = i_ref[...]
        bC = b[C - 1:C, :]
        eb = jnp.exp(b)
        ecb = jnp.exp(bC - b)
        o = o_ref[...]
        z = z_ref[...]
        sz = _sigmoid(z)
        go = go_ref[...]
        g_ob = dob_ref[...]
        r = lax.rsqrt(jnp.mean(o * o, axis=-1, keepdims=True) + EPS)
        nh = o * r
        dnrm = g_ob * (z * sz)
        dz_ref[...] = (g_ob * (nh * go) * (sz * (1.0 + z * (1.0 - sz)))).astype(MXU_DTYPE)
        dgo_ref[0:1, :] += jnp.sum(dnrm * nh, axis=0, keepdims=True)
        dn = dnrm * go
        do = r * (dn - nh * jnp.mean(dn * nh, axis=-1, keepdims=True))

        yield
        S0 = jnp.where(cn == 0, 0.0, s0_ref[...])
        S1 = s1_ref[...]
        dS1 = dSt[...]
        dq = eb * _mm(do, S0)
        dk = ecb * _mm(v, dS1)
        dv = _mm(k * ecb, dS1, NT)
        bnd = jnp.sum(dS1 * S1, axis=0, keepdims=True)
        dSt[...] = dS1 * jnp.exp(bC) + _mm(do, q * eb, TN)

        _fill_above(kp, k, 0.0)
        _fill_above(vp, v, 0.0)
        _fill_above(fp, f, 1.0)
        _fill_below(qn, q, 0.0)
        _fill_below(dn_, do, 0.0)
        _fill_below(fn, f, 1.0)
        yield
        for r0 in range(0, C, G_RB):
            do_b = do[r0:r0 + G_RB]
            for l in range(G_SUB):
                xs[pl.ds(l * C + r0, G_RB), :] = (do_b * vp[pl.ds(G_SUB - l + r0, G_RB), :]).astype(MXU_DTYPE)
        dac[0:G_SUB * C, :] = _mm(xs[...], jnp.ones((G_DK, G_DK), MXU_DTYPE))
        dac[G_SUB * C:, :] = jnp.zeros((G_SUB, G_DK), F32)
        yield
        near_q, near_k, near_v = [], [], []
        for r0 in range(0, C, G_RB):
            k_b = k[r0:r0 + G_RB]
            aq = ak = av = e = e2 = None
            for l in range(G_SUB):
                down, up = pl.ds(G_SUB - l + r0, G_RB), pl.ds(l + r0, G_RB)
                if l > 0:
                    fl = fp[pl.ds(G_SUB - l + 1 + r0, G_RB), :]
                    e = fl if e is None else e * fl
                    fu = fn[up, :]
                    e2 = fu if e2 is None else e2 * fu
                kl = kp[down, :]
                t = dac[pl.ds(l * C + r0, G_RB), :] * (kl if e is None else kl * e)
                aq = t if aq is None else aq + t
                qu = qn[up, :]
                qe = qu if e2 is None else qu * e2
                dou = dn_[up, :]
                a2 = jnp.sum(qe * k_b, axis=-1, keepdims=True)
                t = dac[pl.ds(l * C + l + r0, G_RB), :] * qe
                ak = t if ak is None else ak + t
                t = a2 * dou
                av = t if av is None else av + t
            near_q.append(aq)
            near_k.append(ak)
            near_v.append(av)
        dq = dq + jnp.concatenate(near_q, axis=0)
        dk = dk + jnp.concatenate(near_k, axis=0)
        dv = dv + jnp.concatenate(near_v, axis=0)

        yield
        da_all = _mm(do, v, NT)
        a_off = jnp.zeros((C, C), F32)
        for lv, m in enumerate(_hgrn_levels(C)):
            qt, kt, eq, ek = _hgrn_level(b, q, k, C, m)
            da_m = pm_ref[lv] * da_all
            if m == G_SUB:
                qs, ks, das = _split(qt), _split(kt), _split(da_m)
                a_off = a_off + pm_ref[lv] * _mm_split(qs, ks, NT)
                dq = dq + _mm_split(das, ks, NN) * eq
                dk = dk + _mm_split(das, qs, TN) * ek
            else:
                a_off = a_off + pm_ref[lv] * _mm(qt, kt, NT)
                dq = dq + _mm(da_m, kt) * eq
                dk = dk + _mm(da_m, qt, TN) * ek
        dv = dv + _mm(a_off, do, TN)

        yield
        row = lax.broadcasted_iota(jnp.int32, (C, 1), 0)
        db = q * dq - k * dk + jnp.where(row == C - 1, bnd, 0.0)
        dg = _mm_exact(triu_ref[...], db)
        df = dg / f - dk
        df_ref[...] = (df * (1.0 - lb) * (sg * (1.0 - sg))).astype(MXU_DTYPE)
        dlb_ref[0:1, :] += jnp.sum(df * (1.0 - sg), axis=0, keepdims=True)
        dq_ref[...] = (dq * (sq * (1.0 + qraw * (1.0 - sq)))).astype(MXU_DTYPE)
        di_ref[...] = dv.astype(MXU_DTYPE)

    rev = lambda n: nc - 1 - n
    w = hps * G_DK
    blk = pl.BlockSpec((C, w), lambda h, n: (nc - 1 - n, h))
    return pl.pallas_call(
        body, name="hgrn_bwd",
        grid=(G_HEADS // hps, nc),
        in_specs=[_hgrn_col(C, base, rev, hps) for base in (2, 3, 4, 5)] + [
                  blk, blk,
                  pl.BlockSpec((1, hps, G_DK, G_DK), lambda h, n: (jnp.maximum(nc - 2 - n, 0), h, 0, 0)),
                  pl.BlockSpec((1, hps, G_DK, G_DK), lambda h, n: (nc - 1 - n, h, 0, 0)),
                  pl.BlockSpec((2, w), lambda h, n: (0, h)),
                  pl.BlockSpec((1, G_DK), lambda h, n: (0, 0)),
                  pl.BlockSpec((C, C), lambda h, n: (0, 0)),
                  pl.BlockSpec((C, C), lambda h, n: (0, 0)),
                  pl.BlockSpec(masks.shape, lambda h, n: (0, 0, 0))],
        out_specs=[blk, blk, blk, blk,
                   pl.BlockSpec((8, w), lambda h, n: (0, h)),
                   pl.BlockSpec((8 * hps, G_DK), lambda h, n: (h, 0))],
        out_shape=[jax.ShapeDtypeStruct((S, G_W), MXU_DTYPE)] * 4
                  + [jax.ShapeDtypeStruct((8, G_W), F32), jax.ShapeDtypeStruct((8 * G_HEADS, G_DK), F32)],
        scratch_shapes=[pltpu.VMEM((hps, G_DK, G_DK), F32)] + [pltpu.VMEM((hps, C + G_SUB, G_DK), F32)] * 6
                       + [pltpu.VMEM((hps, G_SUB * C, G_DK), MXU_DTYPE),
                          pltpu.VMEM((hps, G_SUB * C + G_SUB, G_DK), F32)],
        compiler_params=_params(("parallel", "arbitrary")),
    )(proj, proj, proj, proj, o_raw, dob, states, states, hgrn_lb, onorm_g, tri, triu, masks)


def _tail(x, target, os, ls, ob, proj, mod3, final_g, wa, wb, wo, tm=256):
    S = x.shape[0]
    nt = S // tm

    def body(x_ref, t_ref, o1, o2, o3, l1, l2, l3, za_ref, ob_ref, ga_ref, gb_ref, mod_ref, fg_ref,
             wa_ref, wb_ref, wo_ref,
             lt_ref, dx2_ref, do_ref, dl_ref, dza_ref, dob_ref, dga_ref, dgb_ref, sums_ref,
             gwa_ref, gwb_ref, gwo_ref, acc_a, acc_b, acc_o):
        i = pl.program_id(0)

        @pl.when(i == 0)
        def _():
            sums_ref[...] = jnp.zeros_like(sums_ref)
            acc_a[...] = jnp.zeros_like(acc_a)
            acc_b[...] = jnp.zeros_like(acc_b)
            acc_o[...] = jnp.zeros_like(acc_o)

        a1, a2, a3 = l1[...], l2[...], l3[...]
        lm = jnp.maximum(jnp.maximum(a1, a2), a3)
        e1, e2, e3 = jnp.exp(a1 - lm), jnp.exp(a2 - lm), jnp.exp(a3 - lm)
        lden = e1 + e2 + e3
        ao = (e1 * o1[...] + e2 * o2[...] + e3 * o3[...]) / lden
        lt_ref[...] = lm + jnp.log(lden)
        za = za_ref[...]
        sza = _sigmoid(za)
        oa_v, ob_v = (ao * (za * sza)).astype(MXU_DTYPE), ob_ref[...]
        pa = _mm(oa_v, wa_ref[...])
        pb = _mm(ob_v, wb_ref[...])
        sa, sb = _sigmoid(ga_ref[...]), _sigmoid(gb_ref[...])
        ym = sa * pa + sb * pb
        u = _mm(ym, wo_ref[...])
        gate = mod_ref[2:3, :]
        fg = fg_ref[...]
        x2 = x_ref[...] + gate * u
        r2 = lax.rsqrt(jnp.mean(x2 * x2, axis=-1, keepdims=True) + EPS)
        xn2 = x2 * r2
        e = xn2 * fg - t_ref[...]
        dy = e * (1.0 / D)
        dn = dy * fg
        dx2 = r2 * (dn - xn2 * jnp.mean(dn * xn2, axis=-1, keepdims=True))
        dx2_ref[...] = dx2
        sums_ref[0:1, :] += jnp.sum(dy * xn2, axis=0, keepdims=True)
        sums_ref[1:2, :] += jnp.sum(dx2 * u, axis=0, keepdims=True)
        sums_ref[2:3, :] += (0.5 / D) * jnp.sum(e * e, axis=0, keepdims=True)
        du = dx2 * gate
        dym = _mm(du, wo_ref[...], NT)
        acc_o[...] += _mm(ym, du, TN)
        dpa, dpb = dym * sa, dym * sb
        dga_ref[...] = (dym * pa * (sa * (1.0 - sa))).astype(MXU_DTYPE)
        dgb_ref[...] = (dym * pb * (sb * (1.0 - sb))).astype(MXU_DTYPE)
        doa = _mm(dpa, wa_ref[...], NT)
        dza_ref[...] = (doa * ao * (sza * (1.0 + za * (1.0 - sza)))).astype(MXU_DTYPE)
        do = doa * (za * sza)
        do_ref[...] = do
        prod = do * ao
        for h in range(A_HEADS):
            sl = slice(A_HD * h, A_HD * (h + 1))
            dl_ref[:, sl] = jnp.broadcast_to(jnp.sum(prod[:, sl], axis=-1, keepdims=True), (tm, A_HD))
        dob_ref[...] = _mm(dpb, wb_ref[...], NT)
        acc_a[...] += _mm(oa_v, dpa, TN)
        acc_b[...] += _mm(ob_v, dpb, TN)

        @pl.when(i == nt - 1)
        def _():
            pltpu.sync_copy(acc_a, gwa_ref)
            pltpu.sync_copy(acc_b, gwb_ref)
            pltpu.sync_copy(acc_o, gwo_ref)

    row = lambda w: pl.BlockSpec((tm, w), lambda i: (i, 0))
    full = lambda a, b: pl.BlockSpec((a, b), lambda i: (0, 0))
    any_spec = pl.BlockSpec(memory_space=pl.ANY)
    return pl.pallas_call(
        body, name="tail",
        grid=(nt,),
        in_specs=[row(D), row(D)] + [row(A_W)] * 6 + [pl.BlockSpec((tm, A_W), lambda i: (i, 3)), row(D),
                  pl.BlockSpec((tm, D), lambda i: (i, 6)), pl.BlockSpec((tm, D), lambda i: (i, 7)),
                  full(8, D), full(1, D), full(A_W, D), full(D, D), full(D, D)],
        out_specs=[row(A_W), row(D), row(A_W), row(A_W), row(A_W), row(D), row(D), row(D), full(8, D),
                   any_spec, any_spec, any_spec],
        out_shape=[jax.ShapeDtypeStruct((S, A_W), F32),
                   jax.ShapeDtypeStruct((S, D), F32), jax.ShapeDtypeStruct((S, A_W), F32),
                   jax.ShapeDtypeStruct((S, A_W), F32), jax.ShapeDtypeStruct((S, A_W), MXU_DTYPE),
                   jax.ShapeDtypeStruct((S, D), F32), jax.ShapeDtypeStruct((S, D), MXU_DTYPE),
                   jax.ShapeDtypeStruct((S, D), MXU_DTYPE), jax.ShapeDtypeStruct((8, D), F32),
                   jax.ShapeDtypeStruct((A_W, D), F32), jax.ShapeDtypeStruct((D, D), F32),
                   jax.ShapeDtypeStruct((D, D), F32)],
        scratch_shapes=[pltpu.VMEM((A_W, D), F32), pltpu.VMEM((D, D), F32), pltpu.VMEM((D, D), F32)],
        compiler_params=_params(("arbitrary",)),
    )(x, target, *os, *ls, proj, ob, proj, proj, mod3, final_g, wa, wb, wo)


def _piece_parts(pieces):
    parts, where = [], []
    for k, piece in enumerate(pieces):
        off = 0
        for part in piece:
            parts.append(part)
            where.append((k, off, part.shape[1]))
            off += part.shape[1]
        assert off == D
    return parts, where


def _dh(pieces, w_in_g, x, dx2, mod3, norm_g, grads, tm=256):
    S = x.shape[0]
    ni = S // tm
    ng = len(grads)
    parts, where = _piece_parts(pieces)
    npart = len(parts)

    def body(*refs):
        p_refs = refs[:npart]
        w_ref, x_ref, dx2_ref, mod_ref, g_ref = refs[npart:npart + 5]
        g_ins = refs[npart + 5:npart + 5 + ng]
        gx_ref, sums_ref = refs[npart + 5 + ng:npart + 7 + ng]
        g_outs = refs[npart + 7 + ng:npart + 7 + 2 * ng]
        w_all, send_sems, recv_sems, local_sems = refs[npart + 7 + 2 * ng:]
        i = pl.program_id(0)
        start, wait = _all_to_all_copies(g_ins, g_outs, send_sems, recv_sems, local_sems)

        @pl.when(i == 0)
        def _():
            start()
            sums_ref[...] = jnp.zeros_like(sums_ref)
            pltpu.sync_copy(w_ref, w_all)

        dh = None
        for p_ref, (k, off, width) in zip(p_refs, where):
            term = _mm(p_ref[...], w_all[k, :, off:off + width], NT)
            dh = term if dh is None else dh + term
        xv = x_ref[...]
        g = g_ref[...]
        sc1 = 1.0 + mod_ref[1:2, :]
        r = lax.rsqrt(jnp.mean(xv * xv, axis=-1, keepdims=True) + EPS)
        xn = xv * r
        sums_ref[0:1, :] += jnp.sum(dh, axis=0, keepdims=True)
        sums_ref[1:2, :] += jnp.sum(dh * (xn * g), axis=0, keepdims=True)
        sums_ref[2:3, :] += jnp.sum(dh * sc1 * xn, axis=0, keepdims=True)
        dxn = dh * sc1 * g
        gx_ref[...] = dx2_ref[...] + r * (dxn - xn * jnp.mean(dxn * xn, axis=-1, keepdims=True))

        @pl.when(i == ni - 1)
        def _():
            wait()

    row = pl.BlockSpec((tm, D), lambda i: (i, 0))
    any_spec = pl.BlockSpec(memory_space=pl.ANY)
    return pl.pallas_call(
        body, name="dh_scatter",
        grid=(ni,),
        in_specs=[pl.BlockSpec((tm, width), lambda i: (i, 0)) for _, _, width in where]
                 + [any_spec, row, row,
                    pl.BlockSpec((8, D), lambda i: (0, 0)),
                    pl.BlockSpec((1, D), lambda i: (0, 0))]
                 + [any_spec] * ng,
        out_specs=[row, pl.BlockSpec((8, D), lambda i: (0, 0))] + [any_spec] * ng,
        out_shape=[jax.ShapeDtypeStruct((S, D), F32), jax.ShapeDtypeStruct((8, D), F32)]
                  + [jax.ShapeDtypeStruct(g.shape, g.dtype) for g in grads],
        scratch_shapes=[pltpu.VMEM(w_in_g.shape, w_in_g.dtype),
                        pltpu.SemaphoreType.DMA((ng, N_DEV - 1)), pltpu.SemaphoreType.DMA((ng, N_DEV - 1)),
                        pltpu.SemaphoreType.DMA((ng,))],
        compiler_params=_params(("arbitrary",)),
    )(*parts, w_in_g, x, dx2, mod3, norm_g, *grads)


def _gw_in(ht, pieces, grads, tm=1024):
    S = ht.shape[1]
    nt = S // tm
    ng = len(grads)
    parts, where = _piece_parts(pieces)
    npart = len(parts)

    def body(*refs):
        h_ref, p_refs = refs[0], refs[1:1 + npart]
        g_ins = refs[1 + npart:1 + npart + ng]
        o_ref = refs[1 + npart + ng]
        g_outs = refs[2 + npart + ng:2 + npart + 2 * ng]
        acc, send_sems, recv_sems, local_sems = refs[2 + npart + 2 * ng:]
        j, i = pl.program_id(0), pl.program_id(1)
        start, wait = _all_to_all_copies(g_ins, g_outs, send_sems, recv_sems, local_sems)

        @pl.when((j == 0) & (i == 0))
        def _():
            start()

        @pl.when(i == 0)
        def _():
            acc[...] = jnp.zeros_like(acc)

        for k in range(N_DEV):
            @pl.when(j == k)
            def _(k=k):
                for p_ref, (kk, off, width) in zip(p_refs, where):
                    if kk == k:
                        acc[:, off:off + width] += _mm(h_ref[...], p_ref[...])

        @pl.when(i == nt - 1)
        def _():
            o_ref[0] = acc[...].astype(XCHG_DTYPE)

        @pl.when((j == N_DEV - 1) & (i == nt - 1))
        def _():
            wait()

    def part_spec(k, width):
        return pl.BlockSpec((tm, width), lambda j, i: (jnp.where(j == k, i, 0), 0))

    any_spec = pl.BlockSpec(memory_space=pl.ANY)
    return pl.pallas_call(
        body, name="gw_in_scatter",
        grid=(N_DEV, nt),
        in_specs=[pl.BlockSpec((D, tm), lambda j, i: (0, i))] + [part_spec(k, width) for k, _, width in where] + [any_spec] * ng,
        out_specs=[pl.BlockSpec((1, D, D), lambda j, i: (j, 0, 0))] + [any_spec] * ng,
        out_shape=[jax.ShapeDtypeStruct((N_DEV, D, D), XCHG_DTYPE)]
                  + [jax.ShapeDtypeStruct(g.shape, g.dtype) for g in grads],
        scratch_shapes=[pltpu.VMEM((D, D), F32),
                        pltpu.SemaphoreType.DMA((ng, N_DEV - 1)), pltpu.SemaphoreType.DMA((ng, N_DEV - 1)),
                        pltpu.SemaphoreType.DMA((ng,))],
        compiler_params=_params(("arbitrary", "arbitrary")),
    )(ht, *parts, *grads)


def _adamw_math(w, g, m, v):
    m = ADAM_B1 * m + (1.0 - ADAM_B1) * g
    v = ADAM_B2 * v + (1.0 - ADAM_B2) * (g * g)
    m_hat = m / (1.0 - ADAM_B1 ** ADAM_STEP)
    v_hat = v / (1.0 - ADAM_B2 ** ADAM_STEP)
    delta = -ADAM_LR * (m_hat / (jnp.sqrt(v_hat) + ADAM_EPS) + ADAM_WD * w)
    return delta, m, v


def _adamw_big(recv, w, m, v, name, tr=128):
    M, N = w.shape
    tr = min(tr, M)

    def body(r_ref, w_ref, m_ref, v_ref, g_ref, d_ref, nm_ref, nv_ref):
        g = r_ref[0].astype(F32)
        for j in range(1, N_DEV):
            g = g + r_ref[j].astype(F32)
        g_ref[...] = g
        d_ref[...], nm_ref[...], nv_ref[...] = _adamw_math(w_ref[...], g, m_ref[...], v_ref[...])

    blk = pl.BlockSpec((tr, N), lambda i: (i, 0))
    return pl.pallas_call(
        body, name=name,
        grid=(M // tr,),
        in_specs=[pl.BlockSpec((N_DEV, tr, N), lambda i: (0, i, 0)), blk, blk, blk],
        out_specs=[blk] * 4,
        out_shape=[jax.ShapeDtypeStruct((M, N), F32)] * 4,
        compiler_params=_params(("parallel",)),
    )(recv, w, m, v)


def _adamw_w_ada(c64, dmod64, w, m, v):
    def body(c_ref, dm_ref, w_ref, m_ref, v_ref, g_ref, d_ref, nm_ref, nv_ref):
        cv = c_ref[...]
        g = _mm(cv * _sigmoid(cv), dm_ref[...], TN)
        g_ref[...] = g
        d_ref[...], nm_ref[...], nv_ref[...] = _adamw_math(w_ref[...], g, m_ref[...], v_ref[...])

    return pl.pallas_call(
        body, name="adamw_w_ada",
        out_shape=[jax.ShapeDtypeStruct(w.shape, F32)] * 4,
        compiler_params=_params(),
    )(c64, dmod64, w, m, v)


P_MOD, P_NORM, P_ONORM, P_RELB, P_LB, P_FINAL, P_LOSS, P_END = (0, 3 * D, 4 * D, 5 * D, 6 * D, 7 * D, 8 * D, 9 * D)


def _adamw_small(packed, b_ada, norm_g, onorm_g, relb, hgrn_lb, final_g, ms, vs):
    def body(pk_ref, b_ref, ng_ref, og_ref, rb_ref, lb_ref, fg_ref,
             mb, mn, mo, mr, ml, mf, vb, vn, vo, vr, vl, vf,
             loss_ref, gb, gn, go, gr, gl, gf, db, dn, do, dr, dl, df,
             nmb, nmn, nmo, nmr, nml, nmf, nvb, nvn, nvo, nvr, nvl, nvf):
        tot = pk_ref[0:1, :]
        for j in range(1, N_DEV):
            tot = tot + pk_ref[8 * j:8 * j + 1, :]
        loss_ref[...] = jnp.broadcast_to(jnp.sum(tot[:, P_LOSS:P_END], axis=-1, keepdims=True), (8, 128))

        def upd(g, w_ref, m_ref, v_ref, g_out, d_out, m_out, v_out):
            g_out[...] = g
            d_out[...], m_out[...], v_out[...] = _adamw_math(w_ref[...], g, m_ref[...], v_ref[...])

        upd(tot[:, P_MOD:P_NORM], b_ref, mb, vb, gb, db, nmb, nvb)
        upd(tot[:, P_NORM:P_ONORM], ng_ref, mn, vn, gn, dn, nmn, nvn)
        g_on = tot[:, P_ONORM:P_ONORM + G_DK]
        for h in range(1, G_HEADS):
            g_on = g_on + tot[:, P_ONORM + G_DK * h:P_ONORM + G_DK * (h + 1)]
        upd(g_on, og_ref, mo, vo, go, do, nmo, nvo)
        upd(tot[:, P_RELB:P_LB], rb_ref, mr, vr, gr, dr, nmr, nvr)
        a = lb_ref[...]
        lb = _sigmoid(a[0:1, :] - a[1:2, :])
        g0 = tot[:, P_LB:P_FINAL] * lb * (1.0 - lb)
        row = lax.broadcasted_iota(jnp.int32, (2, D), 0)
        upd(jnp.where(row == 0, g0, -g0), lb_ref, ml, vl, gl, dl, nml, nvl)
        upd(tot[:, P_FINAL:P_LOSS], fg_ref, mf, vf, gf, df, nmf, nvf)

    shapes = [b_ada.shape, norm_g.shape, onorm_g.shape, relb.shape, hgrn_lb.shape, final_g.shape]
    outs = [jax.ShapeDtypeStruct((8, 128), F32)] + [jax.ShapeDtypeStruct(s, F32) for s in shapes] * 4
    return pl.pallas_call(
        body, name="adamw_small",
        out_shape=outs,
        compiler_params=_params(),
    )(packed, b_ada, norm_g, onorm_g, relb, hgrn_lb, final_g, *ms, *vs)


def _local_step(x, target, mod3, norm_g, w_in_g, onorm_g, wa_blk, wb_blk, wo_blk, rel_bias, hgrn_lb, final_g):
    buckets = jnp.asarray(_bucket_tables())
    bias = _bias_tables(rel_bias, buckets)
    proj, ht, wa_g, wb_g, wo_g = _inproj(x, mod3, norm_g, w_in_g, [wa_blk, wb_blk, wo_blk])
    wa = wa_g.transpose(1, 0, 2).reshape(A_W, D)
    wb = wb_g.reshape(D, D)
    wo = wo_g.reshape(D, D)
    os, ls = [], []
    for p, (_, d) in enumerate(PATTERNS):
        o, l = _attn_fwd(proj, bias[p], d, "attn_fwd_d%d" % d)
        os.append(o)
        ls.append(l)
    o_raw, ob, states = _hgrn_fwd(proj, hgrn_lb, onorm_g)
    lt, dx2, do, delta, dza, dob, dga, dgb, tsums, gwa, gwb, gwo = _tail(
        x, target, os, ls, ob, proj, mod3, final_g, wa, wb, wo)
    dbs, acc = [None] * len(PATTERNS), ()
    for p in reversed(range(len(PATTERNS))):
        d = PATTERNS[p][1]
        *acc, dbs[p] = _attn_bwd(proj, do, lt, delta, bias[p], d, "attn_bwd_d%d" % d, prev=tuple(acc),
                                 out_dtype=MXU_DTYPE if p == 0 else F32)
    dqa, dka, dva = acc
    g_relb = _rel_bias_grad(dbs, buckets)
    dqb, dfb, dib, dzb, dlb, dgo = _hgrn_bwd(proj, o_raw, dob, states, hgrn_lb, onorm_g)
    pieces = [[dqa, dka], [dva, dza], [dqb], [dfb], [dib], [dzb], [dga], [dgb]]
    small = [gwa.astype(XCHG_DTYPE).reshape(A_W, N_DEV, D // N_DEV).transpose(1, 0, 2),
             gwb.astype(XCHG_DTYPE).reshape(N_DEV, D // N_DEV, D),
             gwo.astype(XCHG_DTYPE).reshape(N_DEV, D // N_DEV, D)]
    gw_in, *received_small = _gw_in(ht, pieces, small)
    gx, hsums, received_in = _dh(pieces, w_in_g, x, dx2, mod3, norm_g, [gw_in])
    received = [received_in] + received_small
    row = jnp.concatenate([
        hsums[0], hsums[1], tsums[1],
        hsums[2],
        dgo.reshape(G_HEADS, 8, G_DK)[:, 0].reshape(-1),
        g_relb.reshape(-1),
        dlb[0],
        tsums[0],
        tsums[2],
    ])
    return gx, received, row


def kernel(x, c, w_ada, b_ada, norm_g, w_in, hgrn_onorm_g, w_branch_a, w_branch_b, w_out, rel_bias, hgrn_lb, final_g, loss_target, m_w_ada, m_b_ada, m_norm_g, m_w_in, m_hgrn_onorm_g, m_w_branch_a, m_w_branch_b, m_w_out, m_rel_bias, m_hgrn_lb, m_final_g, v_w_ada, v_b_ada, v_norm_g, v_w_in, v_hgrn_onorm_g, v_w_branch_a, v_w_branch_b, v_w_out, v_rel_bias, v_hgrn_lb, v_final_g):
    me = 4 * lax.axis_index("x") + 2 * lax.axis_index("y") + lax.axis_index("c")
    n_ada = w_ada.shape[2]

    w_in_g, c_all = _all_gather([w_in[0].astype(MXU_DTYPE), jnp.broadcast_to(c, (8, D))], "gather_w_in_c")

    c64 = c_all.reshape(8 * N_DEV, D)
    b_loc = lax.dynamic_slice(b_ada, (0, me * n_ada), (1, n_ada))
    mod_part = _mod_fwd(c64, w_ada[0], b_loc)[::8]
    (mod_all,) = _all_gather([mod_part], "gather_mod")
    mod = lax.dynamic_slice(mod_all, (0, me, 0), (N_DEV, 1, n_ada)).reshape(3, D)
    mod3 = jnp.concatenate([mod, jnp.zeros((5, D), F32)], axis=0)

    onorm_t = hgrn_onorm_g
    gx, (r_in, r_a, r_b, r_o), row = _local_step(
        x[0], loss_target[0], mod3, norm_g, w_in_g, onorm_t, w_branch_a[0].astype(MXU_DTYPE),
        w_branch_b[0].astype(MXU_DTYPE), w_out[0].astype(MXU_DTYPE), rel_bias, hgrn_lb,
        final_g.reshape(1, D))
    packed8 = jnp.concatenate([row[None, :], jnp.zeros((7, P_END), F32)], axis=0)
    (packed,) = _all_gather([packed8], "gather_small")
    packed = packed.reshape(8 * N_DEV, P_END)

    g_in, d_in, nm_in, nv_in = _adamw_big(r_in, w_in[0], m_w_in[0], v_w_in[0], "adamw_w_in")
    g_a, d_a, nm_a, nv_a = _adamw_big(r_a, w_branch_a[0], m_w_branch_a[0], v_w_branch_a[0], "adamw_w_branch_a")
    g_b, d_b, nm_b, nv_b = _adamw_big(r_b, w_branch_b[0], m_w_branch_b[0], v_w_branch_b[0], "adamw_w_branch_b")
    g_o, d_o, nm_o, nv_o = _adamw_big(r_o, w_out[0], m_w_out[0], v_w_out[0], "adamw_w_out")

    dmod64 = lax.dynamic_slice(packed, (0, P_MOD + me * n_ada), (8 * N_DEV, n_ada))
    g_ada, d_ada, nm_ada, nv_ada = _adamw_w_ada(c64, dmod64, w_ada[0], m_w_ada[0], v_w_ada[0])

    def flat_relb(t):
        return jnp.pad(t.T, ((0, 0), (0, 128 - N_BUCKETS))).reshape(1, A_HEADS * 128)

    def unflat_relb(t):
        return t.reshape(A_HEADS, 128)[:, :N_BUCKETS].T

    fg2 = lambda t: t.reshape(1, D)
    smalls = _adamw_small(
        packed, b_ada, norm_g, hgrn_onorm_g, flat_relb(rel_bias), hgrn_lb, fg2(final_g),
        [m_b_ada, m_norm_g, m_hgrn_onorm_g, flat_relb(m_rel_bias), m_hgrn_lb, fg2(m_final_g)],
        [v_b_ada, v_norm_g, v_hgrn_onorm_g, flat_relb(v_rel_bias), v_hgrn_lb, fg2(v_final_g)])
    loss = smalls[0][0, 0]

    def small(kind):
        s = smalls[1 + 6 * kind:7 + 6 * kind]
        return s[0], s[1], s[2], unflat_relb(s[3]), s[4], s[5].reshape(D)

    def leaves(ada, sm, w_in_, wa_, wb_, wo_):
        b_, n_, o_, r_, l_, f_ = sm
        return (ada[None], b_, n_, w_in_[None], o_, wa_[None], wb_[None], wo_[None], r_, l_, f_)

    return (loss, gx[None],
            *leaves(g_ada, small(0), g_in, g_a, g_b, g_o),
            *leaves(d_ada, small(1), d_in, d_a, d_b, d_o),
            *leaves(nm_ada, small(2), nm_in, nm_a, nm_b, nm_o),
            *leaves(nv_ada, small(3), nv_in, nv_a, nv_b, nv_o))
```

```python
import functools
import math

import numpy as np
import jax
import jax.numpy as jnp
from jax import lax
from jax.experimental import pallas as pl
from jax.experimental.pallas import tpu as pltpu

F32 = jnp.float32
BF16 = jnp.bfloat16
MXU_DTYPE = jnp.bfloat16
XCHG_DTYPE = jnp.bfloat16

N_DEV = 8
D = 1024
A_HEADS = 8
A_HD = 64
A_W = A_HEADS * A_HD
A_BLK = 128
PATTERNS = ((128, 1), (512, 4), (2048, 16))
N_BUCKETS = 32
MAX_DISTANCE = 2048
NEG = -1e30
G_HEADS = 8
G_DK = 128
G_W = G_HEADS * G_DK
IN_W = 8 * D
EPS = 1e-6
ADAM_LR = 0.001
ADAM_B1 = 0.9
ADAM_B2 = 0.999
ADAM_EPS = 1e-08
ADAM_WD = 0.01
ADAM_STEP = 10

G_CHUNK = 128
G_SUB = 8
G_HPS_FWD = 8
G_HPS_BWD = 4
G_RB = 16
VMEM_LIMIT = 56 * 1024 * 1024

NN = (((1,), (0,)), ((), ()))
NT = (((1,), (1,)), ((), ()))
TN = (((0,), (0,)), ((), ()))
MESH = pl.DeviceIdType.MESH


def _mm(a, b, dims=NN):
    return lax.dot_general(a.astype(MXU_DTYPE), b.astype(MXU_DTYPE), dims,
                           preferred_element_type=F32)


def _mm_exact(t, x):
    hi = x.astype(BF16)
    r = x - hi.astype(F32)
    mid = r.astype(BF16)
    lo = (r - mid.astype(F32)).astype(BF16)
    tb = t.astype(BF16)
    return sum(lax.dot_general(tb, p, NN, preferred_element_type=F32) for p in (hi, mid, lo))


def _split(x):
    hi = x.astype(BF16)
    return hi, (x - hi.astype(F32)).astype(BF16)


def _mm_split(a, b, dims):
    dot = lambda p, q: lax.dot_general(p, q, dims, preferred_element_type=F32)
    return dot(a[0], b[0]) + dot(a[0], b[1]) + dot(a[1], b[0])


def _sigmoid(x):
    return 0.5 * jnp.tanh(0.5 * x) + 0.5


def _params(sem=None):
    return pltpu.CompilerParams(dimension_semantics=sem, vmem_limit_bytes=VMEM_LIMIT)


def _all_gather(xs, name):
    n = len(xs)

    def body(*refs):
        ins, outs = refs[:n], refs[n:2 * n]
        send_sems, recv_sems, local_sems = refs[2 * n:]
        x, y, c = lax.axis_index("x"), lax.axis_index("y"), lax.axis_index("c")
        me, sibling = (x, y, c), (x, y, 1 - c)
        chips = [(1 - x, y), (x, 1 - y), (1 - x, 1 - y)]

        def slot(ref, dev):
            return ref.at[4 * dev[0] + 2 * dev[1] + dev[2]]

        def copy(a, k, block, to, src=None):
            return pltpu.make_async_remote_copy(
                src_ref=slot(outs[a], block) if src is None else src,
                dst_ref=slot(outs[a], block),
                send_sem=send_sems.at[a, k], recv_sem=recv_sems.at[a, k],
                device_id=to, device_id_type=MESH)

        mine, first, passed = [], [], []
        for a in range(n):
            cp = pltpu.make_async_copy(ins[a], slot(outs[a], me), local_sems.at[a])
            cp.start()
            mine.append(cp)
            first.append(copy(a, 0, me, sibling, src=ins[a]))
            for j, chip in enumerate(chips):
                first.append(copy(a, 1 + j, me, (*chip, c), src=ins[a]))
        for cp in first:
            cp.start()
        for j, chip in enumerate(chips):
            for a in range(n):
                copy(a, 1 + j, (*chip, c), me).wait_recv()
                cp = copy(a, 4 + j, (*chip, c), sibling)
                cp.start()
                passed.append(cp)
        for a in range(n):
            copy(a, 0, sibling, me).wait_recv()
            for j, chip in enumerate(chips):
                copy(a, 4 + j, (*chip, 1 - c), me).wait_recv()
        for cp in first + passed:
            cp.wait_send()
        for cp in mine:
            cp.wait()

    any_spec = pl.BlockSpec(memory_space=pl.ANY)
    return pl.pallas_call(
        body, name=name,
        out_shape=[jax.ShapeDtypeStruct((N_DEV,) + v.shape, v.dtype) for v in xs],
        in_specs=[any_spec] * n, out_specs=[any_spec] * n,
        scratch_shapes=[pltpu.SemaphoreType.DMA((n, 7)), pltpu.SemaphoreType.DMA((n, 7)),
                        pltpu.SemaphoreType.DMA((n,))],
    )(*xs)


def _all_to_all_copies(ins, outs, send_sems, recv_sems, local_sems, gather=False):
    n = len(ins)
    x, y, c = lax.axis_index("x"), lax.axis_index("y"), lax.axis_index("c")
    me = 4 * x + 2 * y + c
    peers = []
    for m in range(1, N_DEV):
        peers.append((1 - x if m & 4 else x, 1 - y if m & 2 else y, 1 - c if m & 1 else c))

    def chunk(a, j):
        return ins[a] if gather else ins[a].at[j]

    def copy(a, k, landing):
        peer = peers[k]
        pid = 4 * peer[0] + 2 * peer[1] + peer[2]
        return pltpu.make_async_remote_copy(
            src_ref=chunk(a, pid), dst_ref=outs[a].at[pid if landing else me],
            send_sem=send_sems.at[a, k], recv_sem=recv_sems.at[a, k],
            device_id=peer, device_id_type=MESH)

    def local(a):
        return pltpu.make_async_copy(chunk(a, me), outs[a].at[me], local_sems.at[a])

    def start():
        for a in range(n):
            local(a).start()
        for k in range(N_DEV - 1):
            for a in range(n):
                copy(a, k, False).start()

    def wait():
        for k in range(N_DEV - 1):
            for a in range(n):
                copy(a, k, True).wait_recv()
        for k in range(N_DEV - 1):
            for a in range(n):
                copy(a, k, False).wait_send()
        for a in range(n):
            local(a).wait()

    return start, wait


def _mod_fwd(c64, w_ada, b_loc):
    def body(c_ref, w_ref, b_ref, o_ref):
        cv = c_ref[...]
        sc = cv * _sigmoid(cv)
        o_ref[...] = _mm(sc, w_ref[...]) + b_ref[...]

    return pl.pallas_call(
        body, name="mod_fwd",
        out_shape=jax.ShapeDtypeStruct((c64.shape[0], w_ada.shape[1]), F32),
        compiler_params=_params(),
    )(c64, w_ada, b_loc)


def _inproj(x, mod3, norm_g, w_in_g, blocks, tm=256):
    S = x.shape[0]
    ni = S // tm
    nb = len(blocks)

    def body(*refs):
        x_ref, mod_ref, g_ref, w_ref = refs[:4]
        b_ins = refs[4:4 + nb]
        proj_ref, ht_ref, qkv_ref = refs[4 + nb:7 + nb]
        b_outs = refs[7 + nb:7 + 2 * nb]
        w_all, send_sems, recv_sems, local_sems = refs[7 + 2 * nb:]
        i = pl.program_id(0)
        start, wait = _all_to_all_copies(b_ins, b_outs, send_sems, recv_sems, local_sems, gather=True)

        @pl.when(i == 0)
        def _():
            start()
            pltpu.sync_copy(w_ref, w_all)

        xv = x_ref[...]
        r = lax.rsqrt(jnp.mean(xv * xv, axis=-1, keepdims=True) + EPS)
        h = ((xv * r * g_ref[...]) * (1.0 + mod_ref[1:2, :]) + mod_ref[0:1, :]).astype(MXU_DTYPE)
        ht_ref[...] = h.T
        for j in range(N_DEV):
            pj = _mm(h, w_all[j])
            proj_ref[:, j * D:(j + 1) * D] = pj
            for c in range(3):
                if c // 2 == j:
                    for p in range(A_HEADS // 2):
                        lo = (c % 2) * A_W + 2 * A_HD * p
                        qkv_ref[c, p] = pj[:, lo:lo + 2 * A_HD]

        @pl.when(i == ni - 1)
        def _():
            wait()

    any_spec = pl.BlockSpec(memory_space=pl.ANY)
    return pl.pallas_call(
        body, name="inproj_gather",
        grid=(ni,),
        in_specs=[pl.BlockSpec((tm, D), lambda i: (i, 0)),
                  pl.BlockSpec((8, D), lambda i: (0, 0)),
                  pl.BlockSpec((1, D), lambda i: (0, 0)),
                  any_spec] + [any_spec] * nb,
        out_specs=[pl.BlockSpec((tm, IN_W), lambda i: (i, 0)),
                   pl.BlockSpec((D, tm), lambda i: (0, i)),
                   pl.BlockSpec((3, A_HEADS // 2, tm, 2 * A_HD), lambda i: (0, 0, i, 0))] + [any_spec] * nb,
        out_shape=[jax.ShapeDtypeStruct((S, IN_W), F32), jax.ShapeDtypeStruct((D, S), MXU_DTYPE),
                   jax.ShapeDtypeStruct((3, A_HEADS // 2, S, 2 * A_HD), F32)]
                  + [jax.ShapeDtypeStruct((N_DEV,) + b.shape, b.dtype) for b in blocks],
        scratch_shapes=[pltpu.VMEM(w_in_g.shape, w_in_g.dtype),
                        pltpu.SemaphoreType.DMA((nb, N_DEV - 1)), pltpu.SemaphoreType.DMA((nb, N_DEV - 1)),
                        pltpu.SemaphoreType.DMA((nb,))],
        compiler_params=_params(("arbitrary",)),
    )(x, mod3, norm_g, w_in_g, *blocks)


def _bucket_tables():
    qi = np.arange(A_BLK)[:, None]
    kj = np.arange(2 * A_BLK)[None, :]
    delta = qi + A_BLK - kj
    out = []
    for window, dil in PATTERNS:
        span = window // dil
        band = (delta >= 0) & (delta <= span)
        dist = np.clip(delta, 0, None) * dil
        max_exact = N_BUCKETS // 2
        nf = dist.astype(np.float32)
        large = max_exact + (np.log(np.maximum(nf, np.float32(1.0)) / np.float32(max_exact))
                             / np.float32(math.log(MAX_DISTANCE / max_exact))
                             * np.float32(N_BUCKETS - max_exact)).astype(np.int32)
        large = np.minimum(large, N_BUCKETS - 1)
        bucket = np.where(dist < max_exact, dist, large)
        out.append(np.where(band, bucket, -1).astype(np.int32))
    return np.stack(out)


def _bias_tables(rel_bias, buckets):
    def body(rb_ref, bk_ref, o_ref):
        h = pl.program_id(1)
        bk = bk_ref[0]
        acc = jnp.full(bk.shape, NEG, F32)
        for b in range(N_BUCKETS):
            acc = jnp.where(bk == b, rb_ref[b, h], acc)
        o_ref[0, 0] = acc

    return pl.pallas_call(
        body, name="bias_tables",
        grid=(3, A_HEADS),
        in_specs=[pl.BlockSpec(memory_space=pltpu.SMEM),
                  pl.BlockSpec((1, A_BLK, 2 * A_BLK), lambda p, h: (p, 0, 0))],
        out_specs=pl.BlockSpec((1, 1, A_BLK, 2 * A_BLK), lambda p, h: (p, h, 0, 0)),
        out_shape=jax.ShapeDtypeStruct((3, A_HEADS, A_BLK, 2 * A_BLK), F32),
        compiler_params=_params(("arbitrary", "arbitrary")),
    )(rel_bias, buckets)


A_TILES = 16


def _attn_heads_per_step(d):
    return A_HEADS if d == 1 else 2


def _attn_in_specs(sb, nsb, hw):
    blk = (1, hw // 2, sb, 2 * A_HD)

    def cur(c):
        return pl.BlockSpec(blk, lambda hp, n: (c, hp, jnp.minimum(n, nsb - 1), 0))

    def prev(c):
        return pl.BlockSpec(blk, lambda hp, n: (c, hp, jnp.maximum(jnp.minimum(n, nsb - 1) - 1, 0), 0))

    return [cur(0), prev(1), cur(1), prev(2), cur(2)]


def _rows(r, d):
    return pl.ds(r, A_BLK) if d == 1 else pl.ds(r, A_BLK, stride=d)


def _for_residues(d, hw, fn):
    unroll = min(d, max(1, A_TILES // hw))
    if d == unroll:
        _round_robin([g for r in range(d) for g in fn(r)])
    else:
        def group(g, c):
            _round_robin([t for u in range(unroll) for t in fn(g * unroll + u)])
            return c
        lax.fori_loop(0, d // unroll, group, 0)


def _attn_stack(t):
    first_half = lax.broadcasted_iota(jnp.int32, (1, 2 * A_HD), 1) < A_HD
    return jnp.concatenate([jnp.where(first_half, t, 0.0), jnp.where(first_half, 0.0, t)], axis=0)


def _attn_unstack(t2):
    first_half = lax.broadcasted_iota(jnp.int32, (1, 2 * A_HD), 1) < A_HD
    return jnp.where(first_half, t2[:A_BLK], t2[A_BLK:])


def _attn_scores(q, k, b_ref, pp, first):
    bias = jnp.concatenate([b_ref[2 * pp] + first, b_ref[2 * pp + 1] + first], axis=0)
    return _mm(_attn_stack(q), k, NT) * (A_HD ** -0.5) + bias


def _attn_fwd(qkv, bias_p, d, name):
    S = qkv.shape[2]
    sb = A_BLK * d
    nsb = S // sb
    hw = _attn_heads_per_step(d)

    def body(q_ref, kp_ref, kc_ref, vp_ref, vc_ref, b_ref, o_ref, l_ref):
        n = pl.program_id(1)
        kj = lax.broadcasted_iota(jnp.int32, (A_BLK, 2 * A_BLK), 1)
        first = jnp.where((n == 0) & (kj < A_BLK), NEG, 0.0).astype(F32)

        def residue(r):
            rows = _rows(r, d)

            def pair(pp):
                lanes = pl.ds(2 * A_HD * pp, 2 * A_HD)
                k = jnp.concatenate([kp_ref.at[0, pp][rows, :], kc_ref.at[0, pp][rows, :]], axis=0)
                v = jnp.concatenate([vp_ref.at[0, pp][rows, :], vc_ref.at[0, pp][rows, :]], axis=0)
                s = _attn_scores(q_ref.at[0, pp][rows, :], k, b_ref, pp, first)
                yield
                m = jnp.max(s, axis=-1, keepdims=True)
                p = jnp.exp(s - m)
                den = jnp.sum(p, axis=-1, keepdims=True)
                pv = _mm(p, v)
                yield
                o_ref[rows, lanes] = _attn_unstack(pv / den)
                l_ref[rows, lanes] = _attn_unstack(jnp.broadcast_to(m + jnp.log(den), (2 * A_BLK, 2 * A_HD)))

            return [pair(pp) for pp in range(hw // 2)]

        _for_residues(d, hw, residue)

    out = pl.BlockSpec((sb, A_HD * hw), lambda hp, n: (n, hp))
    return pl.pallas_call(
        body, name=name,
        grid=(A_HEADS // hw, nsb),
        in_specs=_attn_in_specs(sb, nsb, hw) + [pl.BlockSpec((hw, A_BLK, 2 * A_BLK), lambda hp, n: (hp, 0, 0))],
        out_specs=[out, out],
        out_shape=[jax.ShapeDtypeStruct((S, A_W), F32)] * 2,
        compiler_params=_params(("parallel", "parallel")),
    )(qkv, qkv, qkv, qkv, qkv, bias_p)


def _attn_bwd(qkv, do, lt, delta, bias_p, d, name, prev=(), out_dtype=F32):
    S = qkv.shape[2]
    sb = A_BLK * d
    nsb = S // sb
    hw = _attn_heads_per_step(d)

    def body(*refs):
        q_ref, kp_ref, kc_ref, vp_ref, vc_ref, do_ref, lt_ref, dl_ref, b_ref = refs[:9]
        pq_ref, pk_ref, pv_ref = refs[9:9 + len(prev)] if prev else (None, None, None)
        dq_ref, dk_ref, dv_ref, db_ref, ck, cv = refs[9 + len(prev):]
        n = pl.program_id(1)
        plus = lambda t, p_ref, idx: (t if p_ref is None else t + p_ref[idx]).astype(out_dtype)

        @pl.when(n == 0)
        def _():
            db_ref[...] = jnp.zeros_like(db_ref)
            ck[...] = jnp.zeros_like(ck)
            cv[...] = jnp.zeros_like(cv)

        @pl.when(n < nsb)
        def _():
            kj = lax.broadcasted_iota(jnp.int32, (A_BLK, 2 * A_BLK), 1)
            first = jnp.where((n == 0) & (kj < A_BLK), NEG, 0.0).astype(F32)

            def residue(r):
                rows = _rows(r, d)

                def pair(pp):
                    lanes = pl.ds(2 * A_HD * pp, 2 * A_HD)
                    lt_r, dl_r = lt_ref[rows, lanes], dl_ref[rows, lanes]
                    k = jnp.concatenate([kp_ref.at[0, pp][rows, :], kc_ref.at[0, pp][rows, :]], axis=0)
                    v = jnp.concatenate([vp_ref.at[0, pp][rows, :], vc_ref.at[0, pp][rows, :]], axis=0)
                    q2 = _attn_stack(q_ref.at[0, pp][rows, :])
                    do2 = _attn_stack(do_ref[rows, lanes])
                    col = lambda t: jnp.concatenate([t[:, 0:1], t[:, A_HD:A_HD + 1]], axis=0)
                    s = _attn_scores(q_ref.at[0, pp][rows, :], k, b_ref, pp, first)
                    dp = _mm(do2, v, NT)
                    yield
                    p = jnp.exp(s - col(lt_r))
                    ds = p * (dp - col(dl_r))
                    db_ref[2 * pp] += ds[:A_BLK]
                    db_ref[2 * pp + 1] += ds[A_BLK:]
                    dq = _mm(ds, k)
                    dk = _mm(ds, q2, TN) * (A_HD ** -0.5)
                    dv = _mm(p, do2, TN)
                    yield
                    dq_ref[rows, lanes] = plus(_attn_unstack(dq) * (A_HD ** -0.5), pq_ref, (rows, lanes))
                    dk_ref[rows, lanes] = plus(ck[rows, lanes] + dk[:A_BLK], pk_ref, (rows, lanes))
                    dv_ref[rows, lanes] = plus(cv[rows, lanes] + dv[:A_BLK], pv_ref, (rows, lanes))
                    ck[rows, lanes] = dk[A_BLK:]
                    cv[rows, lanes] = dv[A_BLK:]

                return [pair(pp) for pp in range(hw // 2)]

            _for_residues(d, hw, residue)

        @pl.when(n == nsb)
        def _():
            dk_ref[...] = plus(ck[...], pk_ref, ...)
            dv_ref[...] = plus(cv[...], pv_ref, ...)

    w = A_HD * hw
    row = pl.BlockSpec((sb, w), lambda hp, n: (jnp.minimum(n, nsb - 1), hp))
    lag = pl.BlockSpec((sb, w), lambda hp, n: (jnp.maximum(n - 1, 0), hp))
    tab = pl.BlockSpec((hw, A_BLK, 2 * A_BLK), lambda hp, n: (hp, 0, 0))
    return pl.pallas_call(
        body, name=name,
        grid=(A_HEADS // hw, nsb + 1),
        in_specs=_attn_in_specs(sb, nsb, hw) + [row, row, row, tab] + ([row, lag, lag] if prev else []),
        out_specs=[row, lag, lag, tab],
        out_shape=[jax.ShapeDtypeStruct((S, A_W), out_dtype)] * 3
                  + [jax.ShapeDtypeStruct((A_HEADS, A_BLK, 2 * A_BLK), F32)],
        scratch_shapes=[pltpu.VMEM((sb, w), F32), pltpu.VMEM((sb, w), F32)],
        compiler_params=_params(("parallel", "arbitrary")),
    )(qkv, qkv, qkv, qkv, qkv, do, lt, delta, bias_p, *prev)


def _rel_bias_grad(dbs, buckets):
    def body(d1, d2, d3, bk_ref, o_ref):
        row = lax.broadcasted_iota(jnp.int32, (A_HEADS, 128), 0)
        lane = lax.broadcasted_iota(jnp.int32, (A_HEADS, 128), 1)
        acc = jnp.zeros((A_HEADS, 128), F32)
        for p, dref in enumerate((d1, d2, d3)):
            bk = bk_ref[p]
            for h in range(A_HEADS):
                ds = dref[h]
                for b in range(N_BUCKETS):
                    s = jnp.sum(jnp.where(bk == b, ds, 0.0), keepdims=True)
                    acc = acc + jnp.where((row == h) & (lane == b), s, 0.0)
        o_ref[...] = acc

    return pl.pallas_call(
        body, name="rel_bias_grad",
        out_shape=jax.ShapeDtypeStruct((A_HEADS, 128), F32),
        compiler_params=_params(),
    )(*dbs, buckets)


def _tri(c):
    t = np.tril(np.ones((c, c), np.float32))
    return jnp.asarray(t), jnp.asarray(t.T.copy())


def _fill_above(ref, x, pad):
    ref[0:G_SUB, :] = jnp.full((G_SUB, x.shape[1]), pad, F32)
    ref[G_SUB:, :] = x


def _fill_below(ref, x, pad):
    ref[0:x.shape[0], :] = x
    ref[x.shape[0]:, :] = jnp.full((G_SUB, x.shape[1]), pad, F32)


def _hgrn_gates(q_ref, f_ref, lbp_ref, tri_ref):
    qraw = q_ref[...]
    sq = _sigmoid(qraw)
    q = qraw * sq
    sg = _sigmoid(f_ref[...])
    lb = _sigmoid(lbp_ref[0:1, :] - lbp_ref[1:2, :])
    f = lb + (1.0 - lb) * sg
    k = 1.0 - f
    b = _mm_exact(tri_ref[...], jnp.log(f))
    return qraw, sq, q, sg, lb, f, k, b


def _hgrn_col(C, base, idx, hps):
    return pl.BlockSpec((C, hps * G_DK), lambda h, n: (idx(n), base * (G_HEADS // hps) + h))


def _round_robin(stages):
    live = list(stages)
    while live:
        nxt = []
        for g in live:
            try:
                next(g)
                nxt.append(g)
            except StopIteration:
                pass
        live = nxt


def _hgrn_levels(C):
    out, m = [], G_SUB
    while 2 * m <= C:
        out.append(m)
        m *= 2
    return out


def _hgrn_level_masks(C):
    ti = np.arange(C)[:, None]
    si = np.arange(C)[None, :]
    return jnp.asarray(np.stack([((ti // (2 * m) == si // (2 * m)) & (ti - si >= G_SUB)).astype(np.float32)
                                 for m in _hgrn_levels(C)]))


def _hgrn_level(b, q, k, C, m):
    zeros = jnp.zeros((m, G_DK), F32)
    eq, ek, qt, kt = [], [], [], []
    for blk in range(0, C // m, 2):
        lo, mid, hi = blk * m, (blk + 1) * m, (blk + 2) * m
        ref = b[mid:mid + 1]
        e_right = jnp.exp(b[mid:hi] - ref)
        e_left = jnp.exp(ref - b[lo:mid])
        eq += [zeros, e_right]
        ek += [e_left, zeros]
        qt += [zeros, q[mid:hi] * e_right]
        kt += [k[lo:mid] * e_left, zeros]
    cat = lambda parts: jnp.concatenate(parts, axis=0)
    return cat(qt), cat(kt), cat(eq), cat(ek)


def _hgrn_fwd(proj, hgrn_lb, onorm_g, C=G_CHUNK):
    S = proj.shape[0]
    nc = S // C
    tri, _ = _tri(C)
    masks = _hgrn_level_masks(C)
    hps = G_HPS_FWD

    def body(q_ref, f_ref, i_ref, z_ref, lbp_ref, go_ref, tri_ref, pm_ref, o_ref, ob_ref, st_ref, St, kp, vp, fp):
        @pl.when(pl.program_id(1) == 0)
        def _():
            St[...] = jnp.zeros_like(St)

        heads = []
        for hh in range(hps):
            ln = pl.ds(G_DK * hh, G_DK)
            heads.append(head(
                q_ref.at[:, ln], f_ref.at[:, ln], i_ref.at[:, ln], z_ref.at[:, ln], lbp_ref.at[:, ln], go_ref,
                tri_ref, pm_ref, o_ref.at[:, ln], ob_ref.at[:, ln], st_ref.at[0, hh], St.at[hh], kp.at[hh], vp.at[hh],
                fp.at[hh]))
        _round_robin(heads)

    def head(q_ref, f_ref, i_ref, z_ref, lbp_ref, go_ref, tri_ref, pm_ref, o_ref, ob_ref, st_ref, St, kp, vp, fp):
        _, _, q, _, _, f, k, b = _hgrn_gates(q_ref, f_ref, lbp_ref, tri_ref)
        v = i_ref[...]
        bC = b[C - 1:C, :]
        S0 = St[...]
        o = _mm(q * jnp.exp(b), S0, NT)
        yield
        _fill_above(kp, k, 0.0)
        _fill_above(vp, v, 0.0)
        _fill_above(fp, f, 1.0)
        near = []
        for r0 in range(0, C, G_RB):
            qb = q[r0:r0 + G_RB]
            acc = e = None
            for l in range(G_SUB):
                rows = pl.ds(G_SUB - l + r0, G_RB)
                if l > 0:
                    fl = fp[pl.ds(G_SUB - l + 1 + r0, G_RB), :]
                    e = fl if e is None else e * fl
                kl = kp[rows, :]
                a = jnp.sum(qb * kl if e is None else qb * kl * e, axis=-1, keepdims=True)
                t = a * vp[rows, :]
                acc = t if acc is None else acc + t
            near.append(acc)
        o = o + jnp.concatenate(near, axis=0)
        yield
        a_off = jnp.zeros((C, C), F32)
        for lv, m in enumerate(_hgrn_levels(C)):
            qt, kt, _, _ = _hgrn_level(b, q, k, C, m)
            prod = _mm_split(_split(qt), _split(kt), NT) if m == G_SUB else _mm(qt, kt, NT)
            a_off = a_off + pm_ref[lv] * prod
        yield
        o = o + _mm(a_off, v)
        S1 = S0 * jnp.exp(bC) + _mm(v, k * jnp.exp(bC - b), TN)
        St[...] = S1
        st_ref[...] = S1
        o_ref[...] = o
        r = lax.rsqrt(jnp.mean(o * o, axis=-1, keepdims=True) + EPS)
        z = z_ref[...]
        ob_ref[...] = (o * r * go_ref[...] * (z * _sigmoid(z))).astype(MXU_DTYPE)

    ident = lambda n: n
    w = hps * G_DK
    out = pl.BlockSpec((C, w), lambda h, n: (n, h))
    return pl.pallas_call(
        body, name="hgrn_fwd",
        grid=(G_HEADS // hps, nc),
        in_specs=[_hgrn_col(C, base, ident, hps) for base in (2, 3, 4, 5)] + [
                  pl.BlockSpec((2, w), lambda h, n: (0, h)),
                  pl.BlockSpec((1, G_DK), lambda h, n: (0, 0)),
                  pl.BlockSpec((C, C), lambda h, n: (0, 0)),
                  pl.BlockSpec(masks.shape, lambda h, n: (0, 0, 0))],
        out_specs=[out, out, pl.BlockSpec((1, hps, G_DK, G_DK), lambda h, n: (n, h, 0, 0))],
        out_shape=[jax.ShapeDtypeStruct((S, G_W), F32), jax.ShapeDtypeStruct((S, G_W), MXU_DTYPE),
                   jax.ShapeDtypeStruct((nc, G_HEADS, G_DK, G_DK), F32)],
        scratch_shapes=[pltpu.VMEM((hps, G_DK, G_DK), F32)] + [pltpu.VMEM((hps, C + G_SUB, G_DK), F32)] * 3,
        compiler_params=_params(("parallel", "arbitrary")),
    )(proj, proj, proj, proj, hgrn_lb, onorm_g, tri, masks)


def _hgrn_bwd(proj, o_raw, dob, states, hgrn_lb, onorm_g, C=G_CHUNK):
    S = proj.shape[0]
    nc = S // C
    tri, triu = _tri(C)
    masks = _hgrn_level_masks(C)
    hps = G_HPS_BWD

    def body(q_ref, f_ref, i_ref, z_ref, o_ref, dob_ref, s0_ref, s1_ref, lbp_ref, go_ref, tri_ref, triu_ref,
             pm_ref, dq_ref, df_ref, di_ref, dz_ref, dlb_ref, dgo_ref, dSt, *shifted):
        @pl.when(pl.program_id(1) == 0)
        def _():
            dSt[...] = jnp.zeros_like(dSt)
            dlb_ref[...] = jnp.zeros_like(dlb_ref)
            dgo_ref[...] = jnp.zeros_like(dgo_ref)

        heads = []
        for hh in range(hps):
            ln = pl.ds(G_DK * hh, G_DK)
            heads.append(head(
                q_ref.at[:, ln], f_ref.at[:, ln], i_ref.at[:, ln], z_ref.at[:, ln], o_ref.at[:, ln],
                dob_ref.at[:, ln], s0_ref.at[0, hh], s1_ref.at[0, hh], lbp_ref.at[:, ln], go_ref, tri_ref, triu_ref,
                pm_ref, dq_ref.at[:, ln], df_ref.at[:, ln], di_ref.at[:, ln], dz_ref.at[:, ln], dlb_ref.at[:, ln],
                dgo_ref.at[pl.ds(8 * hh, 8), :], dSt.at[hh], *[t.at[hh] for t in shifted]))
        _round_robin(heads)

    def head(q_ref, f_ref, i_ref, z_ref, o_ref, dob_ref, s0_ref, s1_ref, lbp_ref, go_ref, tri_ref, triu_ref,
             pm_ref, dq_ref, df_ref, di_ref, dz_ref, dlb_ref, dgo_ref, dSt, kp, vp, fp, qn, dn_, fn, xs, dac):
        cn = nc - 1 - pl.program_id(1)
        qraw, sq, q, sg, lb, f, k, b = _hgrn_gates(q_ref, f_ref, lbp_ref, tri_ref)
        v = i_ref[...]
        bC = b[C - 1:C, :]
        eb = jnp.exp(b)
        ecb = jnp.exp(bC - b)
        o = o_ref[...]
        z = z_ref[...]
        sz = _sigmoid(z)
        go = go_ref[...]
        g_ob = dob_ref[...]
        r = lax.rsqrt(jnp.mean(o * o, axis=-1, keepdims=True) + EPS)
        nh = o * r
        dnrm = g_ob * (z * sz)
        dz_ref[...] = (g_ob * (nh * go) * (sz * (1.0 + z * (1.0 - sz)))).astype(MXU_DTYPE)
        dgo_ref[0:1, :] += jnp.sum(dnrm * nh, axis=0, keepdims=True)
        dn = dnrm * go
        do = r * (dn - nh * jnp.mean(dn * nh, axis=-1, keepdims=True))

        yield
        S0 = jnp.where(cn == 0, 0.0, s0_ref[...])
        S1 = s1_ref[...]
        dS1 = dSt[...]
        dq = eb * _mm(do, S0)
        dk = ecb * _mm(v, dS1)
        dv = _mm(k * ecb, dS1, NT)
        bnd = jnp.sum(dS1 * S1, axis=0, keepdims=True)
        dSt[...] = dS1 * jnp.exp(bC) + _mm(do, q * eb, TN)

        _fill_above(kp, k, 0.0)
        _fill_above(vp, v, 0.0)
        _fill_above(fp, f, 1.0)
        _fill_below(qn, q, 0.0)
        _fill_below(dn_, do, 0.0)
        _fill_below(fn, f, 1.0)
        yield
        for r0 in range(0, C, G_RB):
            do_b = do[r0:r0 + G_RB]
            for l in range(G_SUB):
                xs[pl.ds(l * C + r0, G_RB), :] = (do_b * vp[pl.ds(G_SUB - l + r0, G_RB), :]).astype(MXU_DTYPE)
        dac[0:G_SUB * C, :] = _mm(xs[...], jnp.ones((G_DK, G_DK), MXU_DTYPE))
        dac[G_SUB * C:, :] = jnp.zeros((G_SUB, G_DK), F32)
        yield
        near_q, near_k, near_v = [], [], []
        for r0 in range(0, C, G_RB):
            k_b = k[r0:r0 + G_RB]
            aq = ak = av = e = e2 = None
            for l in range(G_SUB):
                down, up = pl.ds(G_SUB - l + r0, G_RB), pl.ds(l + r0, G_RB)
                if l > 0:
                    fl = fp[pl.ds(G_SUB - l + 1 + r0, G_RB), :]
                    e = fl if e is None else e * fl
                    fu = fn[up, :]
                    e2 = fu if e2 is None else e2 * fu
                kl = kp[down, :]
                t = dac[pl.ds(l * C + r0, G_RB), :] * (kl if e is None else kl * e)
                aq = t if aq is None else aq + t
                qu = qn[up, :]
                qe = qu if e2 is None else qu * e2
                dou = dn_[up, :]
                a2 = jnp.sum(qe * k_b, axis=-1, keepdims=True)
                t = dac[pl.ds(l * C + l + r0, G_RB), :] * qe
                ak = t if ak is None else ak + t
                t = a2 * dou
                av = t if av is None else av + t
            near_q.append(aq)
            near_k.append(ak)
            near_v.append(av)
        dq = dq + jnp.concatenate(near_q, axis=0)
        dk = dk + jnp.concatenate(near_k, axis=0)
        dv = dv + jnp.concatenate(near_v, axis=0)

        yield
        da_all = _mm(do, v, NT)
        a_off = jnp.zeros((C, C), F32)
        for lv, m in enumerate(_hgrn_levels(C)):
            qt, kt, eq, ek = _hgrn_level(b, q, k, C, m)
            da_m = pm_ref[lv] * da_all
            if m == G_SUB:
                qs, ks, das = _split(qt), _split(kt), _split(da_m)
                a_off = a_off + pm_ref[lv] * _mm_split(qs, ks, NT)
                dq = dq + _mm_split(das, ks, NN) * eq
                dk = dk + _mm_split(das, qs, TN) * ek
            else:
                a_off = a_off + pm_ref[lv] * _mm(qt, kt, NT)
                dq = dq + _mm(da_m, kt) * eq
                dk = dk + _mm(da_m, qt, TN) * ek
        dv = dv + _mm(a_off, do, TN)

        yield
        row = lax.broadcasted_iota(jnp.int32, (C, 1), 0)
        db = q * dq - k * dk + jnp.where(row == C - 1, bnd, 0.0)
        dg = _mm_exact(triu_ref[...], db)
        df = dg / f - dk
        df_ref[...] = (df * (1.0 - lb) * (sg * (1.0 - sg))).astype(MXU_DTYPE)
        dlb_ref[0:1, :] += jnp.sum(df * (1.0 - sg), axis=0, keepdims=True)
        dq_ref[...] = (dq * (sq * (1.0 + qraw * (1.0 - sq)))).astype(MXU_DTYPE)
        di_ref[...] = dv.astype(MXU_DTYPE)

    rev = lambda n: nc - 1 - n
    w = hps * G_DK
    blk = pl.BlockSpec((C, w), lambda h, n: (nc - 1 - n, h))
    return pl.pallas_call(
        body, name="hgrn_bwd",
        grid=(G_HEADS // hps, nc),
        in_specs=[_hgrn_col(C, base, rev, hps) for base in (2, 3, 4, 5)] + [
                  blk, blk,
                  pl.BlockSpec((1, hps, G_DK, G_DK), lambda h, n: (jnp.maximum(nc - 2 - n, 0), h, 0, 0)),
                  pl.BlockSpec((1, hps, G_DK, G_DK), lambda h, n: (nc - 1 - n, h, 0, 0)),
                  pl.BlockSpec((2, w), lambda h, n: (0, h)),
                  pl.BlockSpec((1, G_DK), lambda h, n: (0, 0)),
                  pl.BlockSpec((C, C), lambda h, n: (0, 0)),
                  pl.BlockSpec((C, C), lambda h, n: (0, 0)),
                  pl.BlockSpec(masks.shape, lambda h, n: (0, 0, 0))],
        out_specs=[blk, blk, blk, blk,
                   pl.BlockSpec((8, w), lambda h, n: (0, h)),
                   pl.BlockSpec((8 * hps, G_DK), lambda h, n: (h, 0))],
        out_shape=[jax.ShapeDtypeStruct((S, G_W), MXU_DTYPE)] * 4
                  + [jax.ShapeDtypeStruct((8, G_W), F32), jax.ShapeDtypeStruct((8 * G_HEADS, G_DK), F32)],
        scratch_shapes=[pltpu.VMEM((hps, G_DK, G_DK), F32)] + [pltpu.VMEM((hps, C + G_SUB, G_DK), F32)] * 6
                       + [pltpu.VMEM((hps, G_SUB * C, G_DK), MXU_DTYPE),
                          pltpu.VMEM((hps, G_SUB * C + G_SUB, G_DK), F32)],
        compiler_params=_params(("parallel", "arbitrary")),
    )(proj, proj, proj, proj, o_raw, dob, states, states, hgrn_lb, onorm_g, tri, triu, masks)


def _tail(x, target, os, ls, ob, proj, mod3, final_g, wa, wb, wo, tm=256):
    S = x.shape[0]
    nt = S // tm

    def body(x_ref, t_ref, o1, o2, o3, l1, l2, l3, za_ref, ob_ref, ga_ref, gb_ref, mod_ref, fg_ref,
             wa_ref, wb_ref, wo_ref,
             lt_ref, dx2_ref, do_ref, dl_ref, dza_ref, dob_ref, dga_ref, dgb_ref, sums_ref,
             gwa_ref, gwb_ref, gwo_ref, acc_a, acc_b, acc_o):
        i = pl.program_id(0)

        @pl.when(i == 0)
        def _():
            sums_ref[...] = jnp.zeros_like(sums_ref)
            acc_a[...] = jnp.zeros_like(acc_a)
            acc_b[...] = jnp.zeros_like(acc_b)
            acc_o[...] = jnp.zeros_like(acc_o)

        a1, a2, a3 = l1[...], l2[...], l3[...]
        lm = jnp.maximum(jnp.maximum(a1, a2), a3)
        e1, e2, e3 = jnp.exp(a1 - lm), jnp.exp(a2 - lm), jnp.exp(a3 - lm)
        lden = e1 + e2 + e3
        ao = (e1 * o1[...] + e2 * o2[...] + e3 * o3[...]) / lden
        lt_ref[...] = lm + jnp.log(lden)
        za = za_ref[...]
        sza = _sigmoid(za)
        oa_v, ob_v = (ao * (za * sza)).astype(MXU_DTYPE), ob_ref[...]
        pa = _mm(oa_v, wa_ref[...])
        pb = _mm(ob_v, wb_ref[...])
        sa, sb = _sigmoid(ga_ref[...]), _sigmoid(gb_ref[...])
        ym = sa * pa + sb * pb
        u = _mm(ym, wo_ref[...])
        gate = mod_ref[2:3, :]
        fg = fg_ref[...]
        x2 = x_ref[...] + gate * u
        r2 = lax.rsqrt(jnp.mean(x2 * x2, axis=-1, keepdims=True) + EPS)
        xn2 = x2 * r2
        e = xn2 * fg - t_ref[...]
        dy = e * (1.0 / D)
        dn = dy * fg
        dx2 = r2 * (dn - xn2 * jnp.mean(dn * xn2, axis=-1, keepdims=True))
        dx2_ref[...] = dx2
        sums_ref[0:1, :] += jnp.sum(dy * xn2, axis=0, keepdims=True)
        sums_ref[1:2, :] += jnp.sum(dx2 * u, axis=0, keepdims=True)
        sums_ref[2:3, :] += (0.5 / D) * jnp.sum(e * e, axis=0, keepdims=True)
        du = dx2 * gate
        dym = _mm(du, wo_ref[...], NT)
        acc_o[...] += _mm(ym, du, TN)
        dpa, dpb = dym * sa, dym * sb
        dga_ref[...] = (dym * pa * (sa * (1.0 - sa))).astype(MXU_DTYPE)
        dgb_ref[...] = (dym * pb * (sb * (1.0 - sb))).astype(MXU_DTYPE)
        doa = _mm(dpa, wa_ref[...], NT)
        dza_ref[...] = (doa * ao * (sza * (1.0 + za * (1.0 - sza)))).astype(MXU_DTYPE)
        do = doa * (za * sza)
        do_ref[...] = do
        prod = do * ao
        for h in range(A_HEADS):
            sl = slice(A_HD * h, A_HD * (h + 1))
            dl_ref[:, sl] = jnp.broadcast_to(jnp.sum(prod[:, sl], axis=-1, keepdims=True), (tm, A_HD))
        dob_ref[...] = _mm(dpb, wb_ref[...], NT)
        acc_a[...] += _mm(oa_v, dpa, TN)
        acc_b[...] += _mm(ob_v, dpb, TN)

        @pl.when(i == nt - 1)
        def _():
            pltpu.sync_copy(acc_a, gwa_ref)
            pltpu.sync_copy(acc_b, gwb_ref)
            pltpu.sync_copy(acc_o, gwo_ref)

    row = lambda w: pl.BlockSpec((tm, w), lambda i: (i, 0))
    full = lambda a, b: pl.BlockSpec((a, b), lambda i: (0, 0))
    any_spec = pl.BlockSpec(memory_space=pl.ANY)
    return pl.pallas_call(
        body, name="tail",
        grid=(nt,),
        in_specs=[row(D), row(D)] + [row(A_W)] * 6 + [pl.BlockSpec((tm, A_W), lambda i: (i, 3)), row(D),
                  pl.BlockSpec((tm, D), lambda i: (i, 6)), pl.BlockSpec((tm, D), lambda i: (i, 7)),
                  full(8, D), full(1, D), full(A_W, D), full(D, D), full(D, D)],
        out_specs=[row(A_W), row(D), row(A_W), row(A_W), row(A_W), row(D), row(D), row(D), full(8, D),
                   any_spec, any_spec, any_spec],
        out_shape=[jax.ShapeDtypeStruct((S, A_W), F32),
                   jax.ShapeDtypeStruct((S, D), F32), jax.ShapeDtypeStruct((S, A_W), F32),
                   jax.ShapeDtypeStruct((S, A_W), F32), jax.ShapeDtypeStruct((S, A_W), MXU_DTYPE),
                   jax.ShapeDtypeStruct((S, D), F32), jax.ShapeDtypeStruct((S, D), MXU_DTYPE),
                   jax.ShapeDtypeStruct((S, D), MXU_DTYPE), jax.ShapeDtypeStruct((8, D), F32),
                   jax.ShapeDtypeStruct((A_W, D), F32), jax.ShapeDtypeStruct((D, D), F32),
                   jax.ShapeDtypeStruct((D, D), F32)],
        scratch_shapes=[pltpu.VMEM((A_W, D), F32), pltpu.VMEM((D, D), F32), pltpu.VMEM((D, D), F32)],
        compiler_params=_params(("arbitrary",)),
    )(x, target, *os, *ls, proj, ob, proj, proj, mod3, final_g, wa, wb, wo)


def _piece_parts(pieces):
    parts, where = [], []
    for k, piece in enumerate(pieces):
        off = 0
        for part in piece:
            parts.append(part)
            where.append((k, off, part.shape[1]))
            off += part.shape[1]
        assert off == D
    return parts, where


def _dh(pieces, w_in_g, x, dx2, mod3, norm_g, grads, tm=256):
    S = x.shape[0]
    ni = S // tm
    ng = len(grads)
    parts, where = _piece_parts(pieces)
    npart = len(parts)

    def body(*refs):
        p_refs = refs[:npart]
        w_ref, x_ref, dx2_ref, mod_ref, g_ref = refs[npart:npart + 5]
        g_ins = refs[npart + 5:npart + 5 + ng]
        gx_ref, sums_ref = refs[npart + 5 + ng:npart + 7 + ng]
        g_outs = refs[npart + 7 + ng:npart + 7 + 2 * ng]
        w_all, send_sems, recv_sems, local_sems = refs[npart + 7 + 2 * ng:]
        i = pl.program_id(0)
        start, wait = _all_to_all_copies(g_ins, g_outs, send_sems, recv_sems, local_sems)

        @pl.when(i == 0)
        def _():
            start()
            sums_ref[...] = jnp.zeros_like(sums_ref)
            pltpu.sync_copy(w_ref, w_all)

        dh = None
        for p_ref, (k, off, width) in zip(p_refs, where):
            term = _mm(p_ref[...], w_all[k, :, off:off + width], NT)
            dh = term if dh is None else dh + term
        xv = x_ref[...]
        g = g_ref[...]
        sc1 = 1.0 + mod_ref[1:2, :]
        r = lax.rsqrt(jnp.mean(xv * xv, axis=-1, keepdims=True) + EPS)
        xn = xv * r
        sums_ref[0:1, :] += jnp.sum(dh, axis=0, keepdims=True)
        sums_ref[1:2, :] += jnp.sum(dh * (xn * g), axis=0, keepdims=True)
        sums_ref[2:3, :] += jnp.sum(dh * sc1 * xn, axis=0, keepdims=True)
        dxn = dh * sc1 * g
        gx_ref[...] = dx2_ref[...] + r * (dxn - xn * jnp.mean(dxn * xn, axis=-1, keepdims=True))

        @pl.when(i == ni - 1)
        def _():
            wait()

    row = pl.BlockSpec((tm, D), lambda i: (i, 0))
    any_spec = pl.BlockSpec(memory_space=pl.ANY)
    return pl.pallas_call(
        body, name="dh_scatter",
        grid=(ni,),
        in_specs=[pl.BlockSpec((tm, width), lambda i: (i, 0)) for _, _, width in where]
                 + [any_spec, row, row,
                    pl.BlockSpec((8, D), lambda i: (0, 0)),
                    pl.BlockSpec((1, D), lambda i: (0, 0))]
                 + [any_spec] * ng,
        out_specs=[row, pl.BlockSpec((8, D), lambda i: (0, 0))] + [any_spec] * ng,
        out_shape=[jax.ShapeDtypeStruct((S, D), F32), jax.ShapeDtypeStruct((8, D), F32)]
                  + [jax.ShapeDtypeStruct(g.shape, g.dtype) for g in grads],
        scratch_shapes=[pltpu.VMEM(w_in_g.shape, w_in_g.dtype),
                        pltpu.SemaphoreType.DMA((ng, N_DEV - 1)), pltpu.SemaphoreType.DMA((ng, N_DEV - 1)),
                        pltpu.SemaphoreType.DMA((ng,))],
        compiler_params=_params(("arbitrary",)),
    )(*parts, w_in_g, x, dx2, mod3, norm_g, *grads)


def _gw_in(ht, pieces, grads, tm=1024):
    S = ht.shape[1]
    nt = S // tm
    ng = len(grads)
    parts, where = _piece_parts(pieces)
    npart = len(parts)

    def body(*refs):
        h_ref, p_refs = refs[0], refs[1:1 + npart]
        g_ins = refs[1 + npart:1 + npart + ng]
        o_ref = refs[1 + npart + ng]
        g_outs = refs[2 + npart + ng:2 + npart + 2 * ng]
        acc, send_sems, recv_sems, local_sems = refs[2 + npart + 2 * ng:]
        j, i = pl.program_id(0), pl.program_id(1)
        start, wait = _all_to_all_copies(g_ins, g_outs, send_sems, recv_sems, local_sems)

        @pl.when((j == 0) & (i == 0))
        def _():
            start()

        @pl.when(i == 0)
        def _():
            acc[...] = jnp.zeros_like(acc)

        for k in range(N_DEV):
            @pl.when(j == k)
            def _(k=k):
                for p_ref, (kk, off, width) in zip(p_refs, where):
                    if kk == k:
                        acc[:, off:off + width] += _mm(h_ref[...], p_ref[...])

        @pl.when(i == nt - 1)
        def _():
            o_ref[0] = acc[...].astype(XCHG_DTYPE)

        @pl.when((j == N_DEV - 1) & (i == nt - 1))
        def _():
            wait()

    def part_spec(k, width):
        return pl.BlockSpec((tm, width), lambda j, i: (jnp.where(j == k, i, 0), 0))

    any_spec = pl.BlockSpec(memory_space=pl.ANY)
    return pl.pallas_call(
        body, name="gw_in_scatter",
        grid=(N_DEV, nt),
        in_specs=[pl.BlockSpec((D, tm), lambda j, i: (0, i))] + [part_spec(k, width) for k, _, width in where] + [any_spec] * ng,
        out_specs=[pl.BlockSpec((1, D, D), lambda j, i: (j, 0, 0))] + [any_spec] * ng,
        out_shape=[jax.ShapeDtypeStruct((N_DEV, D, D), XCHG_DTYPE)]
                  + [jax.ShapeDtypeStruct(g.shape, g.dtype) for g in grads],
        scratch_shapes=[pltpu.VMEM((D, D), F32),
                        pltpu.SemaphoreType.DMA((ng, N_DEV - 1)), pltpu.SemaphoreType.DMA((ng, N_DEV - 1)),
                        pltpu.SemaphoreType.DMA((ng,))],
        compiler_params=_params(("arbitrary", "arbitrary")),
    )(ht, *parts, *grads)


def _adamw_math(w, g, m, v):
    m = ADAM_B1 * m + (1.0 - ADAM_B1) * g
    v = ADAM_B2 * v + (1.0 - ADAM_B2) * (g * g)
    m_hat = m / (1.0 - ADAM_B1 ** ADAM_STEP)
    v_hat = v / (1.0 - ADAM_B2 ** ADAM_STEP)
    delta = -ADAM_LR * (m_hat / (jnp.sqrt(v_hat) + ADAM_EPS) + ADAM_WD * w)
    return delta, m, v


def _adamw_big(recv, w, m, v, name, tr=128):
    M, N = w.shape
    tr = min(tr, M)

    def body(r_ref, w_ref, m_ref, v_ref, g_ref, d_ref, nm_ref, nv_ref):
        g = r_ref[0].astype(F32)
        for j in range(1, N_DEV):
            g = g + r_ref[j].astype(F32)
        g_ref[...] = g
        d_ref[...], nm_ref[...], nv_ref[...] = _adamw_math(w_ref[...], g, m_ref[...], v_ref[...])

    blk = pl.BlockSpec((tr, N), lambda i: (i, 0))
    return pl.pallas_call(
        body, name=name,
        grid=(M // tr,),
        in_specs=[pl.BlockSpec((N_DEV, tr, N), lambda i: (0, i, 0)), blk, blk, blk],
        out_specs=[blk] * 4,
        out_shape=[jax.ShapeDtypeStruct((M, N), F32)] * 4,
        compiler_params=_params(("parallel",)),
    )(recv, w, m, v)


def _adamw_w_ada(c64, dmod64, w, m, v):
    def body(c_ref, dm_ref, w_ref, m_ref, v_ref, g_ref, d_ref, nm_ref, nv_ref):
        cv = c_ref[...]
        g = _mm(cv * _sigmoid(cv), dm_ref[...], TN)
        g_ref[...] = g
        d_ref[...], nm_ref[...], nv_ref[...] = _adamw_math(w_ref[...], g, m_ref[...], v_ref[...])

    return pl.pallas_call(
        body, name="adamw_w_ada",
        out_shape=[jax.ShapeDtypeStruct(w.shape, F32)] * 4,
        compiler_params=_params(),
    )(c64, dmod64, w, m, v)


P_MOD, P_NORM, P_ONORM, P_RELB, P_LB, P_FINAL, P_LOSS, P_END = (0, 3 * D, 4 * D, 5 * D, 6 * D, 7 * D, 8 * D, 9 * D)


def _adamw_small(packed, b_ada, norm_g, onorm_g, relb, hgrn_lb, final_g, ms, vs):
    def body(pk_ref, b_ref, ng_ref, og_ref, rb_ref, lb_ref, fg_ref,
             mb, mn, mo, mr, ml, mf, vb, vn, vo, vr, vl, vf,
             loss_ref, gb, gn, go, gr, gl, gf, db, dn, do, dr, dl, df,
             nmb, nmn, nmo, nmr, nml, nmf, nvb, nvn, nvo, nvr, nvl, nvf):
        tot = pk_ref[0:1, :]
        for j in range(1, N_DEV):
            tot = tot + pk_ref[8 * j:8 * j + 1, :]
        loss_ref[...] = jnp.broadcast_to(jnp.sum(tot[:, P_LOSS:P_END], axis=-1, keepdims=True), (8, 128))

        def upd(g, w_ref, m_ref, v_ref, g_out, d_out, m_out, v_out):
            g_out[...] = g
            d_out[...], m_out[...], v_out[...] = _adamw_math(w_ref[...], g, m_ref[...], v_ref[...])

        upd(tot[:, P_MOD:P_NORM], b_ref, mb, vb, gb, db, nmb, nvb)
        upd(tot[:, P_NORM:P_ONORM], ng_ref, mn, vn, gn, dn, nmn, nvn)
        g_on = tot[:, P_ONORM:P_ONORM + G_DK]
        for h in range(1, G_HEADS):
            g_on = g_on + tot[:, P_ONORM + G_DK * h:P_ONORM + G_DK * (h + 1)]
        upd(g_on, og_ref, mo, vo, go, do, nmo, nvo)
        upd(tot[:, P_RELB:P_LB], rb_ref, mr, vr, gr, dr, nmr, nvr)
        a = lb_ref[...]
        lb = _sigmoid(a[0:1, :] - a[1:2, :])
        g0 = tot[:, P_LB:P_FINAL] * lb * (1.0 - lb)
        row = lax.broadcasted_iota(jnp.int32, (2, D), 0)
        upd(jnp.where(row == 0, g0, -g0), lb_ref, ml, vl, gl, dl, nml, nvl)
        upd(tot[:, P_FINAL:P_LOSS], fg_ref, mf, vf, gf, df, nmf, nvf)

    shapes = [b_ada.shape, norm_g.shape, onorm_g.shape, relb.shape, hgrn_lb.shape, final_g.shape]
    outs = [jax.ShapeDtypeStruct((8, 128), F32)] + [jax.ShapeDtypeStruct(s, F32) for s in shapes] * 4
    return pl.pallas_call(
        body, name="adamw_small",
        out_shape=outs,
        compiler_params=_params(),
    )(packed, b_ada, norm_g, onorm_g, relb, hgrn_lb, final_g, *ms, *vs)


def _local_step(x, target, mod3, norm_g, w_in_g, onorm_g, wa_blk, wb_blk, wo_blk, rel_bias, hgrn_lb, final_g):
    buckets = jnp.asarray(_bucket_tables())
    bias = _bias_tables(rel_bias, buckets)
    proj, ht, qkv, wa_g, wb_g, wo_g = _inproj(x, mod3, norm_g, w_in_g, [wa_blk, wb_blk, wo_blk])
    wa = wa_g.transpose(1, 0, 2).reshape(A_W, D)
    wb = wb_g.reshape(D, D)
    wo = wo_g.reshape(D, D)
    os, ls = [], []
    for p, (_, d) in enumerate(PATTERNS):
        o, l = _attn_fwd(qkv, bias[p], d, "attn_fwd_d%d" % d)
        os.append(o)
        ls.append(l)
    o_raw, ob, states = _hgrn_fwd(proj, hgrn_lb, onorm_g)
    lt, dx2, do, delta, dza, dob, dga, dgb, tsums, gwa, gwb, gwo = _tail(
        x, target, os, ls, ob, proj, mod3, final_g, wa, wb, wo)
    dbs, acc = [None] * len(PATTERNS), ()
    for p in reversed(range(len(PATTERNS))):
        d = PATTERNS[p][1]
        *acc, dbs[p] = _attn_bwd(qkv, do, lt, delta, bias[p], d, "attn_bwd_d%d" % d, prev=tuple(acc),
                                 out_dtype=MXU_DTYPE if p == 0 else F32)
    dqa, dka, dva = acc
    g_relb = _rel_bias_grad(dbs, buckets)
    dqb, dfb, dib, dzb, dlb, dgo = _hgrn_bwd(proj, o_raw, dob, states, hgrn_lb, onorm_g)
    pieces = [[dqa, dka], [dva, dza], [dqb], [dfb], [dib], [dzb], [dga], [dgb]]
    small = [gwa.astype(XCHG_DTYPE).reshape(A_W, N_DEV, D // N_DEV).transpose(1, 0, 2),
             gwb.astype(XCHG_DTYPE).reshape(N_DEV, D // N_DEV, D),
             gwo.astype(XCHG_DTYPE).reshape(N_DEV, D // N_DEV, D)]
    gw_in, *received_small = _gw_in(ht, pieces, small)
    gx, hsums, received_in = _dh(pieces, w_in_g, x, dx2, mod3, norm_g, [gw_in])
    received = [received_in] + received_small
    row = jnp.concatenate([
        hsums[0], hsums[1], tsums[1],
        hsums[2],
        dgo.reshape(G_HEADS, 8, G_DK)[:, 0].reshape(-1),
        g_relb.reshape(-1),
        dlb[0],
        tsums[0],
        tsums[2],
    ])
    return gx, received, row


def kernel(x, c, w_ada, b_ada, norm_g, w_in, hgrn_onorm_g, w_branch_a, w_branch_b, w_out, rel_bias, hgrn_lb, final_g, loss_target, m_w_ada, m_b_ada, m_norm_g, m_w_in, m_hgrn_onorm_g, m_w_branch_a, m_w_branch_b, m_w_out, m_rel_bias, m_hgrn_lb, m_final_g, v_w_ada, v_b_ada, v_norm_g, v_w_in, v_hgrn_onorm_g, v_w_branch_a, v_w_branch_b, v_w_out, v_rel_bias, v_hgrn_lb, v_final_g):
    me = 4 * lax.axis_index("x") + 2 * lax.axis_index("y") + lax.axis_index("c")
    n_ada = w_ada.shape[2]

    w_in_g, c_all = _all_gather([w_in[0].astype(MXU_DTYPE), jnp.broadcast_to(c, (8, D))], "gather_w_in_c")

    c64 = c_all.reshape(8 * N_DEV, D)
    b_loc = lax.dynamic_slice(b_ada, (0, me * n_ada), (1, n_ada))
    mod_part = _mod_fwd(c64, w_ada[0], b_loc)[::8]
    (mod_all,) = _all_gather([mod_part], "gather_mod")
    mod = lax.dynamic_slice(mod_all, (0, me, 0), (N_DEV, 1, n_ada)).reshape(3, D)
    mod3 = jnp.concatenate([mod, jnp.zeros((5, D), F32)], axis=0)

    onorm_t = hgrn_onorm_g
    gx, (r_in, r_a, r_b, r_o), row = _local_step(
        x[0], loss_target[0], mod3, norm_g, w_in_g, onorm_t, w_branch_a[0].astype(MXU_DTYPE),
        w_branch_b[0].astype(MXU_DTYPE), w_out[0].astype(MXU_DTYPE), rel_bias, hgrn_lb,
        final_g.reshape(1, D))
    packed8 = jnp.concatenate([row[None, :], jnp.zeros((7, P_END), F32)], axis=0)
    (packed,) = _all_gather([packed8], "gather_small")
    packed = packed.reshape(8 * N_DEV, P_END)

    g_in, d_in, nm_in, nv_in = _adamw_big(r_in, w_in[0], m_w_in[0], v_w_in[0], "adamw_w_in")
    g_a, d_a, nm_a, nv_a = _adamw_big(r_a, w_branch_a[0], m_w_branch_a[0], v_w_branch_a[0], "adamw_w_branch_a")
    g_b, d_b, nm_b, nv_b = _adamw_big(r_b, w_branch_b[0], m_w_branch_b[0], v_w_branch_b[0], "adamw_w_branch_b")
    g_o, d_o, nm_o, nv_o = _adamw_big(r_o, w_out[0], m_w_out[0], v_w_out[0], "adamw_w_out")

    dmod64 = lax.dynamic_slice(packed, (0, P_MOD + me * n_ada), (8 * N_DEV, n_ada))
    g_ada, d_ada, nm_ada, nv_ada = _adamw_w_ada(c64, dmod64, w_ada[0], m_w_ada[0], v_w_ada[0])

    def flat_relb(t):
        return jnp.pad(t.T, ((0, 0), (0, 128 - N_BUCKETS))).reshape(1, A_HEADS * 128)

    def unflat_relb(t):
        return t.reshape(A_HEADS, 128)[:, :N_BUCKETS].T

    fg2 = lambda t: t.reshape(1, D)
    smalls = _adamw_small(
        packed, b_ada, norm_g, hgrn_onorm_g, flat_relb(rel_bias), hgrn_lb, fg2(final_g),
        [m_b_ada, m_norm_g, m_hgrn_onorm_g, flat_relb(m_rel_bias), m_hgrn_lb, fg2(m_final_g)],
        [v_b_ada, v_norm_g, v_hgrn_onorm_g, flat_relb(v_rel_bias), v_hgrn_lb, fg2(v_final_g)])
    loss = smalls[0][0, 0]

    def small(kind):
        s = smalls[1 + 6 * kind:7 + 6 * kind]
        return s[0], s[1], s[2], unflat_relb(s[3]), s[4], s[5].reshape(D)

    def leaves(ada, sm, w_in_, wa_, wb_, wo_):
        b_, n_, o_, r_, l_, f_ = sm
        return (ada[None], b_, n_, w_in_[None], o_, wa_[None], wb_[None], wo_[None], r_, l_, f_)

    return (loss, gx[None],
            *leaves(g_ada, small(0), g_in, g_a, g_b, g_o),
            *leaves(d_ada, small(1), d_in, d_a, d_b, d_o),
            *leaves(nm_ada, small(2), nm_in, nm_a, nm_b, nm_o),
            *leaves(nv_ada, small(3), nv_in, nv_a, nv_b, nv_o))
```

```python
import functools
import math

import numpy as np
import jax
import jax.numpy as jnp
from jax import lax
from jax.experimental import pallas as pl
from jax.experimental.pallas import tpu as pltpu

F32 = jnp.float32
BF16 = jnp.bfloat16
MXU_DTYPE = jnp.bfloat16
XCHG_DTYPE = jnp.bfloat16

N_DEV = 8
D = 1024
A_HEADS = 8
A_HD = 64
A_W = A_HEADS * A_HD
A_BLK = 128
PATTERNS = ((128, 1), (512, 4), (2048, 16))
N_BUCKETS = 32
MAX_DISTANCE = 2048
NEG = -1e30
G_HEADS = 8
G_DK = 128
G_W = G_HEADS * G_DK
IN_W = 8 * D
EPS = 1e-6
ADAM_LR = 0.001
ADAM_B1 = 0.9
ADAM_B2 = 0.999
ADAM_EPS = 1e-08
ADAM_WD = 0.01
ADAM_STEP = 10

G_CHUNK = 128
G_SUB = 8
G_HPS_FWD = 8
G_HPS_BWD = 8
G_RB = 16
VMEM_LIMIT = 56 * 1024 * 1024

NN = (((1,), (0,)), ((), ()))
NT = (((1,), (1,)), ((), ()))
TN = (((0,), (0,)), ((), ()))
MESH = pl.DeviceIdType.MESH


def _mm(a, b, dims=NN):
    return lax.dot_general(a.astype(MXU_DTYPE), b.astype(MXU_DTYPE), dims,
                           preferred_element_type=F32)


def _mm_exact(t, x):
    hi = x.astype(BF16)
    r = x - hi.astype(F32)
    mid = r.astype(BF16)
    lo = (r - mid.astype(F32)).astype(BF16)
    tb = t.astype(BF16)
    return sum(lax.dot_general(tb, p, NN, preferred_element_type=F32) for p in (hi, mid, lo))


def _split(x):
    hi = x.astype(BF16)
    return hi, (x - hi.astype(F32)).astype(BF16)


def _mm_split(a, b, dims):
    dot = lambda p, q: lax.dot_general(p, q, dims, preferred_element_type=F32)
    return dot(a[0], b[0]) + dot(a[0], b[1]) + dot(a[1], b[0])


def _sigmoid(x):
    return 0.5 * jnp.tanh(0.5 * x) + 0.5


def _params(sem=None):
    return pltpu.CompilerParams(dimension_semantics=sem, vmem_limit_bytes=VMEM_LIMIT)


def _all_gather(xs, name):
    n = len(xs)

    def body(*refs):
        ins, outs = refs[:n], refs[n:2 * n]
        send_sems, recv_sems, local_sems = refs[2 * n:]
        x, y, c = lax.axis_index("x"), lax.axis_index("y"), lax.axis_index("c")
        me, sibling = (x, y, c), (x, y, 1 - c)
        chips = [(1 - x, y), (x, 1 - y), (1 - x, 1 - y)]

        def slot(ref, dev):
            return ref.at[4 * dev[0] + 2 * dev[1] + dev[2]]

        def copy(a, k, block, to, src=None):
            return pltpu.make_async_remote_copy(
                src_ref=slot(outs[a], block) if src is None else src,
                dst_ref=slot(outs[a], block),
                send_sem=send_sems.at[a, k], recv_sem=recv_sems.at[a, k],
                device_id=to, device_id_type=MESH)

        mine, first, passed = [], [], []
        for a in range(n):
            cp = pltpu.make_async_copy(ins[a], slot(outs[a], me), local_sems.at[a])
            cp.start()
            mine.append(cp)
            first.append(copy(a, 0, me, sibling, src=ins[a]))
            for j, chip in enumerate(chips):
                first.append(copy(a, 1 + j, me, (*chip, c), src=ins[a]))
        for cp in first:
            cp.start()
        for j, chip in enumerate(chips):
            for a in range(n):
                copy(a, 1 + j, (*chip, c), me).wait_recv()
                cp = copy(a, 4 + j, (*chip, c), sibling)
                cp.start()
                passed.append(cp)
        for a in range(n):
            copy(a, 0, sibling, me).wait_recv()
            for j, chip in enumerate(chips):
                copy(a, 4 + j, (*chip, 1 - c), me).wait_recv()
        for cp in first + passed:
            cp.wait_send()
        for cp in mine:
            cp.wait()

    any_spec = pl.BlockSpec(memory_space=pl.ANY)
    return pl.pallas_call(
        body, name=name,
        out_shape=[jax.ShapeDtypeStruct((N_DEV,) + v.shape, v.dtype) for v in xs],
        in_specs=[any_spec] * n, out_specs=[any_spec] * n,
        scratch_shapes=[pltpu.SemaphoreType.DMA((n, 7)), pltpu.SemaphoreType.DMA((n, 7)),
                        pltpu.SemaphoreType.DMA((n,))],
    )(*xs)


def _all_to_all_copies(ins, outs, send_sems, recv_sems, local_sems, gather=False):
    n = len(ins)
    x, y, c = lax.axis_index("x"), lax.axis_index("y"), lax.axis_index("c")
    me = 4 * x + 2 * y + c
    peers = []
    for m in range(1, N_DEV):
        peers.append((1 - x if m & 4 else x, 1 - y if m & 2 else y, 1 - c if m & 1 else c))

    def chunk(a, j):
        return ins[a] if gather else ins[a].at[j]

    def copy(a, k, landing):
        peer = peers[k]
        pid = 4 * peer[0] + 2 * peer[1] + peer[2]
        return pltpu.make_async_remote_copy(
            src_ref=chunk(a, pid), dst_ref=outs[a].at[pid if landing else me],
            send_sem=send_sems.at[a, k], recv_sem=recv_sems.at[a, k],
            device_id=peer, device_id_type=MESH)

    def local(a):
        return pltpu.make_async_copy(chunk(a, me), outs[a].at[me], local_sems.at[a])

    def start():
        for a in range(n):
            local(a).start()
        for k in range(N_DEV - 1):
            for a in range(n):
                copy(a, k, False).start()

    def wait():
        for k in range(N_DEV - 1):
            for a in range(n):
                copy(a, k, True).wait_recv()
        for k in range(N_DEV - 1):
            for a in range(n):
                copy(a, k, False).wait_send()
        for a in range(n):
            local(a).wait()

    return start, wait


def _mod_fwd(c64, w_ada, b_loc):
    def body(c_ref, w_ref, b_ref, o_ref):
        cv = c_ref[...]
        sc = cv * _sigmoid(cv)
        o_ref[...] = _mm(sc, w_ref[...]) + b_ref[...]

    return pl.pallas_call(
        body, name="mod_fwd",
        out_shape=jax.ShapeDtypeStruct((c64.shape[0], w_ada.shape[1]), F32),
        compiler_params=_params(),
    )(c64, w_ada, b_loc)


def _inproj(x, mod3, norm_g, w_in_g, blocks, tm=256):
    S = x.shape[0]
    ni = S // tm
    nb = len(blocks)

    def body(*refs):
        x_ref, mod_ref, g_ref, w_ref = refs[:4]
        b_ins = refs[4:4 + nb]
        proj_ref, ht_ref, qkv_ref = refs[4 + nb:7 + nb]
        b_outs = refs[7 + nb:7 + 2 * nb]
        w_all, send_sems, recv_sems, local_sems = refs[7 + 2 * nb:]
        i = pl.program_id(0)
        start, wait = _all_to_all_copies(b_ins, b_outs, send_sems, recv_sems, local_sems, gather=True)

        @pl.when(i == 0)
        def _():
            start()
            pltpu.sync_copy(w_ref, w_all)

        xv = x_ref[...]
        r = lax.rsqrt(jnp.mean(xv * xv, axis=-1, keepdims=True) + EPS)
        h = ((xv * r * g_ref[...]) * (1.0 + mod_ref[1:2, :]) + mod_ref[0:1, :]).astype(MXU_DTYPE)
        ht_ref[...] = h.T
        for j in range(N_DEV):
            pj = _mm(h, w_all[j])
            proj_ref[:, j * D:(j + 1) * D] = pj
            for c in range(3):
                if c // 2 == j:
                    for p in range(A_HEADS // 2):
                        lo = (c % 2) * A_W + 2 * A_HD * p
                        qkv_ref[c, p] = pj[:, lo:lo + 2 * A_HD]

        @pl.when(i == ni - 1)
        def _():
            wait()

    any_spec = pl.BlockSpec(memory_space=pl.ANY)
    return pl.pallas_call(
        body, name="inproj_gather",
        grid=(ni,),
        in_specs=[pl.BlockSpec((tm, D), lambda i: (i, 0)),
                  pl.BlockSpec((8, D), lambda i: (0, 0)),
                  pl.BlockSpec((1, D), lambda i: (0, 0)),
                  any_spec] + [any_spec] * nb,
        out_specs=[pl.BlockSpec((tm, IN_W), lambda i: (i, 0)),
                   pl.BlockSpec((D, tm), lambda i: (0, i)),
                   pl.BlockSpec((3, A_HEADS // 2, tm, 2 * A_HD), lambda i: (0, 0, i, 0))] + [any_spec] * nb,
        out_shape=[jax.ShapeDtypeStruct((S, IN_W), F32), jax.ShapeDtypeStruct((D, S), MXU_DTYPE),
                   jax.ShapeDtypeStruct((3, A_HEADS // 2, S, 2 * A_HD), F32)]
                  + [jax.ShapeDtypeStruct((N_DEV,) + b.shape, b.dtype) for b in blocks],
        scratch_shapes=[pltpu.VMEM(w_in_g.shape, w_in_g.dtype),
                        pltpu.SemaphoreType.DMA((nb, N_DEV - 1)), pltpu.SemaphoreType.DMA((nb, N_DEV - 1)),
                        pltpu.SemaphoreType.DMA((nb,))],
        compiler_params=_params(("arbitrary",)),
    )(x, mod3, norm_g, w_in_g, *blocks)


def _bucket_tables():
    qi = np.arange(A_BLK)[:, None]
    kj = np.arange(2 * A_BLK)[None, :]
    delta = qi + A_BLK - kj
    out = []
    for window, dil in PATTERNS:
        span = window // dil
        band = (delta >= 0) & (delta <= span)
        dist = np.clip(delta, 0, None) * dil
        max_exact = N_BUCKETS // 2
        nf = dist.astype(np.float32)
        large = max_exact + (np.log(np.maximum(nf, np.float32(1.0)) / np.float32(max_exact))
                             / np.float32(math.log(MAX_DISTANCE / max_exact))
                             * np.float32(N_BUCKETS - max_exact)).astype(np.int32)
        large = np.minimum(large, N_BUCKETS - 1)
        bucket = np.where(dist < max_exact, dist, large)
        out.append(np.where(band, bucket, -1).astype(np.int32))
    return np.stack(out)


def _bias_tables(rel_bias, buckets):
    def body(rb_ref, bk_ref, o_ref):
        h = pl.program_id(1)
        bk = bk_ref[0]
        acc = jnp.full(bk.shape, NEG, F32)
        for b in range(N_BUCKETS):
            acc = jnp.where(bk == b, rb_ref[b, h], acc)
        o_ref[0, 0] = acc

    return pl.pallas_call(
        body, name="bias_tables",
        grid=(3, A_HEADS),
        in_specs=[pl.BlockSpec(memory_space=pltpu.SMEM),
                  pl.BlockSpec((1, A_BLK, 2 * A_BLK), lambda p, h: (p, 0, 0))],
        out_specs=pl.BlockSpec((1, 1, A_BLK, 2 * A_BLK), lambda p, h: (p, h, 0, 0)),
        out_shape=jax.ShapeDtypeStruct((3, A_HEADS, A_BLK, 2 * A_BLK), F32),
        compiler_params=_params(("arbitrary", "arbitrary")),
    )(rel_bias, buckets)


A_TILES = 16


def _attn_heads_per_step(d):
    return A_HEADS if d == 1 else 2


def _attn_in_specs(sb, nsb, hw):
    blk = (1, hw // 2, sb, 2 * A_HD)

    def cur(c):
        return pl.BlockSpec(blk, lambda hp, n: (c, hp, jnp.minimum(n, nsb - 1), 0))

    def prev(c):
        return pl.BlockSpec(blk, lambda hp, n: (c, hp, jnp.maximum(jnp.minimum(n, nsb - 1) - 1, 0), 0))

    return [cur(0), prev(1), cur(1), prev(2), cur(2)]


def _rows(r, d):
    return pl.ds(r, A_BLK) if d == 1 else pl.ds(r, A_BLK, stride=d)


def _for_residues(d, hw, fn):
    unroll = min(d, max(1, A_TILES // hw))
    if d == unroll:
        _round_robin([g for r in range(d) for g in fn(r)])
    else:
        def group(g, c):
            _round_robin([t for u in range(unroll) for t in fn(g * unroll + u)])
            return c
        lax.fori_loop(0, d // unroll, group, 0)


def _attn_stack(t):
    first_half = lax.broadcasted_iota(jnp.int32, (1, 2 * A_HD), 1) < A_HD
    return jnp.concatenate([jnp.where(first_half, t, 0.0), jnp.where(first_half, 0.0, t)], axis=0)


def _attn_unstack(t2):
    first_half = lax.broadcasted_iota(jnp.int32, (1, 2 * A_HD), 1) < A_HD
    return jnp.where(first_half, t2[:A_BLK], t2[A_BLK:])


def _attn_scores(q, k, b_ref, pp, first):
    bias = jnp.concatenate([b_ref[2 * pp] + first, b_ref[2 * pp + 1] + first], axis=0)
    return _mm(_attn_stack(q), k, NT) * (A_HD ** -0.5) + bias


def _attn_fwd(qkv, bias_p, d, name):
    S = qkv.shape[2]
    sb = A_BLK * d
    nsb = S // sb
    hw = _attn_heads_per_step(d)

    def body(q_ref, kp_ref, kc_ref, vp_ref, vc_ref, b_ref, o_ref, l_ref):
        n = pl.program_id(1)
        kj = lax.broadcasted_iota(jnp.int32, (A_BLK, 2 * A_BLK), 1)
        first = jnp.where((n == 0) & (kj < A_BLK), NEG, 0.0).astype(F32)

        def residue(r):
            rows = _rows(r, d)

            def pair(pp):
                lanes = pl.ds(2 * A_HD * pp, 2 * A_HD)
                k = jnp.concatenate([kp_ref.at[0, pp][rows, :], kc_ref.at[0, pp][rows, :]], axis=0)
                v = jnp.concatenate([vp_ref.at[0, pp][rows, :], vc_ref.at[0, pp][rows, :]], axis=0)
                s = _attn_scores(q_ref.at[0, pp][rows, :], k, b_ref, pp, first)
                yield
                m = jnp.max(s, axis=-1, keepdims=True)
                p = jnp.exp(s - m)
                den = jnp.sum(p, axis=-1, keepdims=True)
                pv = _mm(p, v)
                yield
                o_ref[rows, lanes] = _attn_unstack(pv / den)
                l_ref[rows, lanes] = _attn_unstack(jnp.broadcast_to(m + jnp.log(den), (2 * A_BLK, 2 * A_HD)))

            return [pair(pp) for pp in range(hw // 2)]

        _for_residues(d, hw, residue)

    out = pl.BlockSpec((sb, A_HD * hw), lambda hp, n: (n, hp))
    return pl.pallas_call(
        body, name=name,
        grid=(A_HEADS // hw, nsb),
        in_specs=_attn_in_specs(sb, nsb, hw) + [pl.BlockSpec((hw, A_BLK, 2 * A_BLK), lambda hp, n: (hp, 0, 0))],
        out_specs=[out, out],
        out_shape=[jax.ShapeDtypeStruct((S, A_W), F32)] * 2,
        compiler_params=_params(("parallel", "parallel")),
    )(qkv, qkv, qkv, qkv, qkv, bias_p)


def _attn_bwd(qkv, do, lt, delta, bias_p, d, name, prev=(), out_dtype=F32):
    S = qkv.shape[2]
    sb = A_BLK * d
    nsb = S // sb
    hw = _attn_heads_per_step(d)

    def body(*refs):
        q_ref, kp_ref, kc_ref, vp_ref, vc_ref, do_ref, lt_ref, dl_ref, b_ref = refs[:9]
        pq_ref, pk_ref, pv_ref = refs[9:9 + len(prev)] if prev else (None, None, None)
        dq_ref, dk_ref, dv_ref, db_ref, ck, cv = refs[9 + len(prev):]
        n = pl.program_id(1)
        plus = lambda t, p_ref, idx: (t if p_ref is None else t + p_ref[idx]).astype(out_dtype)

        @pl.when(n == 0)
        def _():
            db_ref[...] = jnp.zeros_like(db_ref)
            ck[...] = jnp.zeros_like(ck)
            cv[...] = jnp.zeros_like(cv)

        @pl.when(n < nsb)
        def _():
            kj = lax.broadcasted_iota(jnp.int32, (A_BLK, 2 * A_BLK), 1)
            first = jnp.where((n == 0) & (kj < A_BLK), NEG, 0.0).astype(F32)

            def residue(r):
                rows = _rows(r, d)

                def pair(pp):
                    lanes = pl.ds(2 * A_HD * pp, 2 * A_HD)
                    lt_r, dl_r = lt_ref[rows, lanes], dl_ref[rows, lanes]
                    k = jnp.concatenate([kp_ref.at[0, pp][rows, :], kc_ref.at[0, pp][rows, :]], axis=0)
                    v = jnp.concatenate([vp_ref.at[0, pp][rows, :], vc_ref.at[0, pp][rows, :]], axis=0)
                    q2 = _attn_stack(q_ref.at[0, pp][rows, :])
                    do2 = _attn_stack(do_ref[rows, lanes])
                    col = lambda t: jnp.concatenate([t[:, 0:1], t[:, A_HD:A_HD + 1]], axis=0)
                    s = _attn_scores(q_ref.at[0, pp][rows, :], k, b_ref, pp, first)
                    dp = _mm(do2, v, NT)
                    yield
                    p = jnp.exp(s - col(lt_r))
                    ds = p * (dp - col(dl_r))
                    db_ref[2 * pp] += ds[:A_BLK]
                    db_ref[2 * pp + 1] += ds[A_BLK:]
                    dq = _mm(ds, k)
                    dk = _mm(ds, q2, TN) * (A_HD ** -0.5)
                    dv = _mm(p, do2, TN)
                    yield
                    dq_ref[rows, lanes] = plus(_attn_unstack(dq) * (A_HD ** -0.5), pq_ref, (rows, lanes))
                    dk_ref[rows, lanes] = plus(ck[rows, lanes] + dk[:A_BLK], pk_ref, (rows, lanes))
                    dv_ref[rows, lanes] = plus(cv[rows, lanes] + dv[:A_BLK], pv_ref, (rows, lanes))
                    ck[rows, lanes] = dk[A_BLK:]
                    cv[rows, lanes] = dv[A_BLK:]

                return [pair(pp) for pp in range(hw // 2)]

            _for_residues(d, hw, residue)

        @pl.when(n == nsb)
        def _():
            dk_ref[...] = plus(ck[...], pk_ref, ...)
            dv_ref[...] = plus(cv[...], pv_ref, ...)

    w = A_HD * hw
    row = pl.BlockSpec((sb, w), lambda hp, n: (jnp.minimum(n, nsb - 1), hp))
    lag = pl.BlockSpec((sb, w), lambda hp, n: (jnp.maximum(n - 1, 0), hp))
    tab = pl.BlockSpec((hw, A_BLK, 2 * A_BLK), lambda hp, n: (hp, 0, 0))
    return pl.pallas_call(
        body, name=name,
        grid=(A_HEADS // hw, nsb + 1),
        in_specs=_attn_in_specs(sb, nsb, hw) + [row, row, row, tab] + ([row, lag, lag] if prev else []),
        out_specs=[row, lag, lag, tab],
        out_shape=[jax.ShapeDtypeStruct((S, A_W), out_dtype)] * 3
                  + [jax.ShapeDtypeStruct((A_HEADS, A_BLK, 2 * A_BLK), F32)],
        scratch_shapes=[pltpu.VMEM((sb, w), F32), pltpu.VMEM((sb, w), F32)],
        compiler_params=_params(("parallel", "arbitrary")),
    )(qkv, qkv, qkv, qkv, qkv, do, lt, delta, bias_p, *prev)


def _rel_bias_grad(dbs, buckets):
    def body(d1, d2, d3, bk_ref, o_ref):
        row = lax.broadcasted_iota(jnp.int32, (A_HEADS, 128), 0)
        lane = lax.broadcasted_iota(jnp.int32, (A_HEADS, 128), 1)
        acc = jnp.zeros((A_HEADS, 128), F32)
        for p, dref in enumerate((d1, d2, d3)):
            bk = bk_ref[p]
            for h in range(A_HEADS):
                ds = dref[h]
                for b in range(N_BUCKETS):
                    s = jnp.sum(jnp.where(bk == b, ds, 0.0), keepdims=True)
                    acc = acc + jnp.where((row == h) & (lane == b), s, 0.0)
        o_ref[...] = acc

    return pl.pallas_call(
        body, name="rel_bias_grad",
        out_shape=jax.ShapeDtypeStruct((A_HEADS, 128), F32),
        compiler_params=_params(),
    )(*dbs, buckets)


def _tri(c):
    t = np.tril(np.ones((c, c), np.float32))
    return jnp.asarray(t), jnp.asarray(t.T.copy())


def _fill_above(ref, x, pad):
    ref[0:G_SUB, :] = jnp.full((G_SUB, x.shape[1]), pad, F32)
    ref[G_SUB:, :] = x


def _fill_below(ref, x, pad):
    ref[0:x.shape[0], :] = x
    ref[x.shape[0]:, :] = jnp.full((G_SUB, x.shape[1]), pad, F32)


def _hgrn_gates(q_ref, f_ref, lbp_ref, tri_ref):
    qraw = q_ref[...]
    sq = _sigmoid(qraw)
    q = qraw * sq
    sg = _sigmoid(f_ref[...])
    lb = _sigmoid(lbp_ref[0:1, :] - lbp_ref[1:2, :])
    f = lb + (1.0 - lb) * sg
    k = 1.0 - f
    b = _mm_exact(tri_ref[...], jnp.log(f))
    return qraw, sq, q, sg, lb, f, k, b


def _hgrn_col(C, base, idx, hps):
    return pl.BlockSpec((C, hps * G_DK), lambda h, n: (idx(n), base * (G_HEADS // hps) + h))


def _round_robin(stages):
    live = list(stages)
    while live:
        nxt = []
        for g in live:
            try:
                next(g)
                nxt.append(g)
            except StopIteration:
                pass
        live = nxt


def _hgrn_levels(C):
    out, m = [], G_SUB
    while 2 * m <= C:
        out.append(m)
        m *= 2
    return out


def _hgrn_level_masks(C):
    ti = np.arange(C)[:, None]
    si = np.arange(C)[None, :]
    return jnp.asarray(np.stack([((ti // (2 * m) == si // (2 * m)) & (ti - si >= G_SUB)).astype(np.float32)
                                 for m in _hgrn_levels(C)]))


def _hgrn_level(b, q, k, C, m):
    zeros = jnp.zeros((m, G_DK), F32)
    eq, ek, qt, kt = [], [], [], []
    for blk in range(0, C // m, 2):
        lo, mid, hi = blk * m, (blk + 1) * m, (blk + 2) * m
        ref = b[mid:mid + 1]
        e_right = jnp.exp(b[mid:hi] - ref)
        e_left = jnp.exp(ref - b[lo:mid])
        eq += [zeros, e_right]
        ek += [e_left, zeros]
        qt += [zeros, q[mid:hi] * e_right]
        kt += [k[lo:mid] * e_left, zeros]
    cat = lambda parts: jnp.concatenate(parts, axis=0)
    return cat(qt), cat(kt), cat(eq), cat(ek)


def _hgrn_fwd(proj, hgrn_lb, onorm_g, C=G_CHUNK):
    S = proj.shape[0]
    nc = S // C
    tri, _ = _tri(C)
    masks = _hgrn_level_masks(C)
    hps = G_HPS_FWD

    def body(q_ref, f_ref, i_ref, z_ref, lbp_ref, go_ref, tri_ref, pm_ref, o_ref, ob_ref, st_ref, St, kp, vp, fp):
        @pl.when(pl.program_id(1) == 0)
        def _():
            St[...] = jnp.zeros_like(St)

        heads = []
        for hh in range(hps):
            ln = pl.ds(G_DK * hh, G_DK)
            heads.append(head(
                q_ref.at[:, ln], f_ref.at[:, ln], i_ref.at[:, ln], z_ref.at[:, ln], lbp_ref.at[:, ln], go_ref,
                tri_ref, pm_ref, o_ref.at[:, ln], ob_ref.at[:, ln], st_ref.at[0, hh], St.at[hh], kp.at[hh], vp.at[hh],
                fp.at[hh]))
        _round_robin(heads)

    def head(q_ref, f_ref, i_ref, z_ref, lbp_ref, go_ref, tri_ref, pm_ref, o_ref, ob_ref, st_ref, St, kp, vp, fp):
        _, _, q, _, _, f, k, b = _hgrn_gates(q_ref, f_ref, lbp_ref, tri_ref)
        v = i_ref[...]
        bC = b[C - 1:C, :]
        S0 = St[...]
        o = _mm(q * jnp.exp(b), S0, NT)
        yield
        _fill_above(kp, k, 0.0)
        _fill_above(vp, v, 0.0)
        _fill_above(fp, f, 1.0)
        near = []
        for r0 in range(0, C, G_RB):
            qb = q[r0:r0 + G_RB]
            acc = e = None
            for l in range(G_SUB):
                rows = pl.ds(G_SUB - l + r0, G_RB)
                if l > 0:
                    fl = fp[pl.ds(G_SUB - l + 1 + r0, G_RB), :]
                    e = fl if e is None else e * fl
                kl = kp[rows, :]
                a = jnp.sum(qb * kl if e is None else qb * kl * e, axis=-1, keepdims=True)
                t = a * vp[rows, :]
                acc = t if acc is None else acc + t
            near.append(acc)
        o = o + jnp.concatenate(near, axis=0)
        yield
        a_off = jnp.zeros((C, C), F32)
        for lv, m in enumerate(_hgrn_levels(C)):
            qt, kt, _, _ = _hgrn_level(b, q, k, C, m)
            prod = _mm_split(_split(qt), _split(kt), NT) if m == G_SUB else _mm(qt, kt, NT)
            a_off = a_off + pm_ref[lv] * prod
        yield
        o = o + _mm(a_off, v)
        S1 = S0 * jnp.exp(bC) + _mm(v, k * jnp.exp(bC - b), TN)
        St[...] = S1
        st_ref[...] = S1
        o_ref[...] = o
        r = lax.rsqrt(jnp.mean(o * o, axis=-1, keepdims=True) + EPS)
        z = z_ref[...]
        ob_ref[...] = (o * r * go_ref[...] * (z * _sigmoid(z))).astype(MXU_DTYPE)

    ident = lambda n: n
    w = hps * G_DK
    out = pl.BlockSpec((C, w), lambda h, n: (n, h))
    return pl.pallas_call(
        body, name="hgrn_fwd",
        grid=(G_HEADS // hps, nc),
        in_specs=[_hgrn_col(C, base, ident, hps) for base in (2, 3, 4, 5)] + [
                  pl.BlockSpec((2, w), lambda h, n: (0, h)),
                  pl.BlockSpec((1, G_DK), lambda h, n: (0, 0)),
                  pl.BlockSpec((C, C), lambda h, n: (0, 0)),
                  pl.BlockSpec(masks.shape, lambda h, n: (0, 0, 0))],
        out_specs=[out, out, pl.BlockSpec((1, hps, G_DK, G_DK), lambda h, n: (n, h, 0, 0))],
        out_shape=[jax.ShapeDtypeStruct((S, G_W), F32), jax.ShapeDtypeStruct((S, G_W), MXU_DTYPE),
                   jax.ShapeDtypeStruct((nc, G_HEADS, G_DK, G_DK), F32)],
        scratch_shapes=[pltpu.VMEM((hps, G_DK, G_DK), F32)] + [pltpu.VMEM((hps, C + G_SUB, G_DK), F32)] * 3,
        compiler_params=_params(("parallel", "arbitrary")),
    )(proj, proj, proj, proj, hgrn_lb, onorm_g, tri, masks)


def _hgrn_bwd(proj, o_raw, dob, states, hgrn_lb, onorm_g, C=G_CHUNK):
    S = proj.shape[0]
    nc = S // C
    tri, triu = _tri(C)
    masks = _hgrn_level_masks(C)
    hps = G_HPS_BWD

    def body(q_ref, f_ref, i_ref, z_ref, o_ref, dob_ref, s0_ref, s1_ref, lbp_ref, go_ref, tri_ref, triu_ref,
             pm_ref, dq_ref, df_ref, di_ref, dz_ref, dlb_ref, dgo_ref, dSt, *shifted):
        @pl.when(pl.program_id(1) == 0)
        def _():
            dSt[...] = jnp.zeros_like(dSt)
            dlb_ref[...] = jnp.zeros_like(dlb_ref)
            dgo_ref[...] = jnp.zeros_like(dgo_ref)

        heads = []
        for hh in range(hps):
            ln = pl.ds(G_DK * hh, G_DK)
            heads.append(head(
                q_ref.at[:, ln], f_ref.at[:, ln], i_ref.at[:, ln], z_ref.at[:, ln], o_ref.at[:, ln],
                dob_ref.at[:, ln], s0_ref.at[0, hh], s1_ref.at[0, hh], lbp_ref.at[:, ln], go_ref, tri_ref, triu_ref,
                pm_ref, dq_ref.at[:, ln], df_ref.at[:, ln], di_ref.at[:, ln], dz_ref.at[:, ln], dlb_ref.at[:, ln],
                dgo_ref.at[pl.ds(8 * hh, 8), :], dSt.at[hh], *[t.at[hh] for t in shifted]))
        _round_robin(heads)

    def head(q_ref, f_ref, i_ref, z_ref, o_ref, dob_ref, s0_ref, s1_ref, lbp_ref, go_ref, tri_ref, triu_ref,
             pm_ref, dq_ref, df_ref, di_ref, dz_ref, dlb_ref, dgo_ref, dSt, kp, vp, fp, qn, dn_, fn, xs, dac):
        cn = nc - 1 - pl.program_id(1)
        qraw, sq, q, sg, lb, f, k, b = _hgrn_gates(q_ref, f_ref, lbp_ref, tri_ref)
        v = i_ref[...]
        bC = b[C - 1:C, :]
        eb = jnp.exp(b)
        ecb = jnp.exp(bC - b)
        o = o_ref[...]
        z = z_ref[...]
        sz = _sigmoid(z)
        go = go_ref[...]
        g_ob = dob_ref[...]
        r = lax.rsqrt(jnp.mean(o * o, axis=-1, keepdims=True) + EPS)
        nh = o * r
        dnrm = g_ob * (z * sz)
        dz_ref[...] = (g_ob * (nh * go) * (sz * (1.0 + z * (1.0 - sz)))).astype(MXU_DTYPE)
        dgo_ref[0:1, :] += jnp.sum(dnrm * nh, axis=0, keepdims=True)
        dn = dnrm * go
        do = r * (dn - nh * jnp.mean(dn * nh, axis=-1, keepdims=True))

        yield
        S0 = jnp.where(cn == 0, 0.0, s0_ref[...])
        S1 = s1_ref[...]
        dS1 = dSt[...]
        dq = eb * _mm(do, S0)
        dk = ecb * _mm(v, dS1)
        dv = _mm(k * ecb, dS1, NT)
        bnd = jnp.sum(dS1 * S1, axis=0, keepdims=True)
        dSt[...] = dS1 * jnp.exp(bC) + _mm(do, q * eb, TN)

        _fill_above(kp, k, 0.0)
        _fill_above(vp, v, 0.0)
        _fill_above(fp, f, 1.0)
        _fill_below(qn, q, 0.0)
        _fill_below(dn_, do, 0.0)
        _fill_below(fn, f, 1.0)
        yield
        for r0 in range(0, C, G_RB):
            do_b = do[r0:r0 + G_RB]
            for l in range(G_SUB):
                xs[pl.ds(l * C + r0, G_RB), :] = (do_b * vp[pl.ds(G_SUB - l + r0, G_RB), :]).astype(MXU_DTYPE)
        dac[0:G_SUB * C, :] = _mm(xs[...], jnp.ones((G_DK, G_DK), MXU_DTYPE))
        dac[G_SUB * C:, :] = jnp.zeros((G_SUB, G_DK), F32)
        yield
        near_q, near_k, near_v = [], [], []
        for r0 in range(0, C, G_RB):
            k_b = k[r0:r0 + G_RB]
            aq = ak = av = e = e2 = None
            for l in range(G_SUB):
                down, up = pl.ds(G_SUB - l + r0, G_RB), pl.ds(l + r0, G_RB)
                if l > 0:
                    fl = fp[pl.ds(G_SUB - l + 1 + r0, G_RB), :]
                    e = fl if e is None else e * fl
                    fu = fn[up, :]
                    e2 = fu if e2 is None else e2 * fu
                kl = kp[down, :]
                t = dac[pl.ds(l * C + r0, G_RB), :] * (kl if e is None else kl * e)
                aq = t if aq is None else aq + t
                qu = qn[up, :]
                qe = qu if e2 is None else qu * e2
                dou = dn_[up, :]
                a2 = jnp.sum(qe * k_b, axis=-1, keepdims=True)
                t = dac[pl.ds(l * C + l + r0, G_RB), :] * qe
                ak = t if ak is None else ak + t
                t = a2 * dou
                av = t if av is None else av + t
            near_q.append(aq)
            near_k.append(ak)
            near_v.append(av)
        dq = dq + jnp.concatenate(near_q, axis=0)
        dk = dk + jnp.concatenate(near_k, axis=0)
        dv = dv + jnp.concatenate(near_v, axis=0)

        yield
        da_all = _mm(do, v, NT)
        a_off = jnp.zeros((C, C), F32)
        for lv, m in enumerate(_hgrn_levels(C)):
            qt, kt, eq, ek = _hgrn_level(b, q, k, C, m)
            da_m = pm_ref[lv] * da_all
            if m == G_SUB:
                qs, ks, das = _split(qt), _split(kt), _split(da_m)
                a_off = a_off + pm_ref[lv] * _mm_split(qs, ks, NT)
                dq = dq + _mm_split(das, ks, NN) * eq
                dk = dk + _mm_split(das, qs, TN) * ek
            else:
                a_off = a_off + pm_ref[lv] * _mm(qt, kt, NT)
                dq = dq + _mm(da_m, kt) * eq
                dk = dk + _mm(da_m, qt, TN) * ek
        dv = dv + _mm(a_off, do, TN)

        yield
        row = lax.broadcasted_iota(jnp.int32, (C, 1), 0)
        db = q * dq - k * dk + jnp.where(row == C - 1, bnd, 0.0)
        dg = _mm_exact(triu_ref[...], db)
        df = dg / f - dk
        df_ref[...] = (df * (1.0 - lb) * (sg * (1.0 - sg))).astype(MXU_DTYPE)
        dlb_ref[0:1, :] += jnp.sum(df * (1.0 - sg), axis=0, keepdims=True)
        dq_ref[...] = (dq * (sq * (1.0 + qraw * (1.0 - sq)))).astype(MXU_DTYPE)
        di_ref[...] = dv.astype(MXU_DTYPE)

    rev = lambda n: nc - 1 - n
    w = hps * G_DK
    blk = pl.BlockSpec((C, w), lambda h, n: (nc - 1 - n, h))
    return pl.pallas_call(
        body, name="hgrn_bwd",
        grid=(G_HEADS // hps, nc),
        in_specs=[_hgrn_col(C, base, rev, hps) for base in (2, 3, 4, 5)] + [
                  blk, blk,
                  pl.BlockSpec((1, hps, G_DK, G_DK), lambda h, n: (jnp.maximum(nc - 2 - n, 0), h, 0, 0)),
                  pl.BlockSpec((1, hps, G_DK, G_DK), lambda h, n: (nc - 1 - n, h, 0, 0)),
                  pl.BlockSpec((2, w), lambda h, n: (0, h)),
                  pl.BlockSpec((1, G_DK), lambda h, n: (0, 0)),
                  pl.BlockSpec((C, C), lambda h, n: (0, 0)),
                  pl.BlockSpec((C, C), lambda h, n: (0, 0)),
                  pl.BlockSpec(masks.shape, lambda h, n: (0, 0, 0))],
        out_specs=[blk, blk, blk, blk,
                   pl.BlockSpec((8, w), lambda h, n: (0, h)),
                   pl.BlockSpec((8 * hps, G_DK), lambda h, n: (h, 0))],
        out_shape=[jax.ShapeDtypeStruct((S, G_W), MXU_DTYPE)] * 4
                  + [jax.ShapeDtypeStruct((8, G_W), F32), jax.ShapeDtypeStruct((8 * G_HEADS, G_DK), F32)],
        scratch_shapes=[pltpu.VMEM((hps, G_DK, G_DK), F32)] + [pltpu.VMEM((hps, C + G_SUB, G_DK), F32)] * 6
                       + [pltpu.VMEM((hps, G_SUB * C, G_DK), MXU_DTYPE),
                          pltpu.VMEM((hps, G_SUB * C + G_SUB, G_DK), F32)],
        compiler_params=_params(("parallel", "arbitrary")),
    )(proj, proj, proj, proj, o_raw, dob, states, states, hgrn_lb, onorm_g, tri, triu, masks)


def _tail(x, target, os, ls, ob, proj, mod3, final_g, wa, wb, wo, tm=256):
    S = x.shape[0]
    nt = S // tm

    def body(x_ref, t_ref, o1, o2, o3, l1, l2, l3, za_ref, ob_ref, ga_ref, gb_ref, mod_ref, fg_ref,
             wa_ref, wb_ref, wo_ref,
             lt_ref, dx2_ref, do_ref, dl_ref, dza_ref, dob_ref, dga_ref, dgb_ref, sums_ref,
             gwa_ref, gwb_ref, gwo_ref, acc_a, acc_b, acc_o):
        i = pl.program_id(0)

        @pl.when(i == 0)
        def _():
            sums_ref[...] = jnp.zeros_like(sums_ref)
            acc_a[...] = jnp.zeros_like(acc_a)
            acc_b[...] = jnp.zeros_like(acc_b)
            acc_o[...] = jnp.zeros_like(acc_o)

        a1, a2, a3 = l1[...], l2[...], l3[...]
        lm = jnp.maximum(jnp.maximum(a1, a2), a3)
        e1, e2, e3 = jnp.exp(a1 - lm), jnp.exp(a2 - lm), jnp.exp(a3 - lm)
        lden = e1 + e2 + e3
        ao = (e1 * o1[...] + e2 * o2[...] + e3 * o3[...]) / lden
        lt_ref[...] = lm + jnp.log(lden)
        za = za_ref[...]
        sza = _sigmoid(za)
        oa_v, ob_v = (ao * (za * sza)).astype(MXU_DTYPE), ob_ref[...]
        pa = _mm(oa_v, wa_ref[...])
        pb = _mm(ob_v, wb_ref[...])
        sa, sb = _sigmoid(ga_ref[...]), _sigmoid(gb_ref[...])
        ym = sa * pa + sb * pb
        u = _mm(ym, wo_ref[...])
        gate = mod_ref[2:3, :]
        fg = fg_ref[...]
        x2 = x_ref[...] + gate * u
        r2 = lax.rsqrt(jnp.mean(x2 * x2, axis=-1, keepdims=True) + EPS)
        xn2 = x2 * r2
        e = xn2 * fg - t_ref[...]
        dy = e * (1.0 / D)
        dn = dy * fg
        dx2 = r2 * (dn - xn2 * jnp.mean(dn * xn2, axis=-1, keepdims=True))
        dx2_ref[...] = dx2
        sums_ref[0:1, :] += jnp.sum(dy * xn2, axis=0, keepdims=True)
        sums_ref[1:2, :] += jnp.sum(dx2 * u, axis=0, keepdims=True)
        sums_ref[2:3, :] += (0.5 / D) * jnp.sum(e * e, axis=0, keepdims=True)
        du = dx2 * gate
        dym = _mm(du, wo_ref[...], NT)
        acc_o[...] += _mm(ym, du, TN)
        dpa, dpb = dym * sa, dym * sb
        dga_ref[...] = (dym * pa * (sa * (1.0 - sa))).astype(MXU_DTYPE)
        dgb_ref[...] = (dym * pb * (sb * (1.0 - sb))).astype(MXU_DTYPE)
        doa = _mm(dpa, wa_ref[...], NT)
        dza_ref[...] = (doa * ao * (sza * (1.0 + za * (1.0 - sza)))).astype(MXU_DTYPE)
        do = doa * (za * sza)
        do_ref[...] = do
        prod = do * ao
        for h in range(A_HEADS):
            sl = slice(A_HD * h, A_HD * (h + 1))
            dl_ref[:, sl] = jnp.broadcast_to(jnp.sum(prod[:, sl], axis=-1, keepdims=True), (tm, A_HD))
        dob_ref[...] = _mm(dpb, wb_ref[...], NT)
        acc_a[...] += _mm(oa_v, dpa, TN)
        acc_b[...] += _mm(ob_v, dpb, TN)

        @pl.when(i == nt - 1)
        def _():
            pltpu.sync_copy(acc_a, gwa_ref)
            pltpu.sync_copy(acc_b, gwb_ref)
            pltpu.sync_copy(acc_o, gwo_ref)

    row = lambda w: pl.BlockSpec((tm, w), lambda i: (i, 0))
    full = lambda a, b: pl.BlockSpec((a, b), lambda i: (0, 0))
    any_spec = pl.BlockSpec(memory_space=pl.ANY)
    return pl.pallas_call(
        body, name="tail",
        grid=(nt,),
        in_specs=[row(D), row(D)] + [row(A_W)] * 6 + [pl.BlockSpec((tm, A_W), lambda i: (i, 3)), row(D),
                  pl.BlockSpec((tm, D), lambda i: (i, 6)), pl.BlockSpec((tm, D), lambda i: (i, 7)),
                  full(8, D), full(1, D), full(A_W, D), full(D, D), full(D, D)],
        out_specs=[row(A_W), row(D), row(A_W), row(A_W), row(A_W), row(D), row(D), row(D), full(8, D),
                   any_spec, any_spec, any_spec],
        out_shape=[jax.ShapeDtypeStruct((S, A_W), F32),
                   jax.ShapeDtypeStruct((S, D), F32), jax.ShapeDtypeStruct((S, A_W), F32),
                   jax.ShapeDtypeStruct((S, A_W), F32), jax.ShapeDtypeStruct((S, A_W), MXU_DTYPE),
                   jax.ShapeDtypeStruct((S, D), F32), jax.ShapeDtypeStruct((S, D), MXU_DTYPE),
                   jax.ShapeDtypeStruct((S, D), MXU_DTYPE), jax.ShapeDtypeStruct((8, D), F32),
                   jax.ShapeDtypeStruct((A_W, D), F32), jax.ShapeDtypeStruct((D, D), F32),
                   jax.ShapeDtypeStruct((D, D), F32)],
        scratch_shapes=[pltpu.VMEM((A_W, D), F32), pltpu.VMEM((D, D), F32), pltpu.VMEM((D, D), F32)],
        compiler_params=_params(("arbitrary",)),
    )(x, target, *os, *ls, proj, ob, proj, proj, mod3, final_g, wa, wb, wo)


def _piece_parts(pieces):
    parts, where = [], []
    for k, piece in enumerate(pieces):
        off = 0
        for part in piece:
            parts.append(part)
            where.append((k, off, part.shape[1]))
            off += part.shape[1]
        assert off == D
    return parts, where


def _dh(pieces, w_in_g, x, dx2, mod3, norm_g, grads, tm=256):
    S = x.shape[0]
    ni = S // tm
    ng = len(grads)
    parts, where = _piece_parts(pieces)
    npart = len(parts)

    def body(*refs):
        p_refs = refs[:npart]
        w_ref, x_ref, dx2_ref, mod_ref, g_ref = refs[npart:npart + 5]
        g_ins = refs[npart + 5:npart + 5 + ng]
        gx_ref, sums_ref = refs[npart + 5 + ng:npart + 7 + ng]
        g_outs = refs[npart + 7 + ng:npart + 7 + 2 * ng]
        w_all, send_sems, recv_sems, local_sems = refs[npart + 7 + 2 * ng:]
        i = pl.program_id(0)
        start, wait = _all_to_all_copies(g_ins, g_outs, send_sems, recv_sems, local_sems)

        @pl.when(i == 0)
        def _():
            start()
            sums_ref[...] = jnp.zeros_like(sums_ref)
            pltpu.sync_copy(w_ref, w_all)

        dh = None
        for p_ref, (k, off, width) in zip(p_refs, where):
            term = _mm(p_ref[...], w_all[k, :, off:off + width], NT)
            dh = term if dh is None else dh + term
        xv = x_ref[...]
        g = g_ref[...]
        sc1 = 1.0 + mod_ref[1:2, :]
        r = lax.rsqrt(jnp.mean(xv * xv, axis=-1, keepdims=True) + EPS)
        xn = xv * r
        sums_ref[0:1, :] += jnp.sum(dh, axis=0, keepdims=True)
        sums_ref[1:2, :] += jnp.sum(dh * (xn * g), axis=0, keepdims=True)
        sums_ref[2:3, :] += jnp.sum(dh * sc1 * xn, axis=0, keepdims=True)
        dxn = dh * sc1 * g
        gx_ref[...] = dx2_ref[...] + r * (dxn - xn * jnp.mean(dxn * xn, axis=-1, keepdims=True))

        @pl.when(i == ni - 1)
        def _():
            wait()

    row = pl.BlockSpec((tm, D), lambda i: (i, 0))
    any_spec = pl.BlockSpec(memory_space=pl.ANY)
    return pl.pallas_call(
        body, name="dh_scatter",
        grid=(ni,),
        in_specs=[pl.BlockSpec((tm, width), lambda i: (i, 0)) for _, _, width in where]
                 + [any_spec, row, row,
                    pl.BlockSpec((8, D), lambda i: (0, 0)),
                    pl.BlockSpec((1, D), lambda i: (0, 0))]
                 + [any_spec] * ng,
        out_specs=[row, pl.BlockSpec((8, D), lambda i: (0, 0))] + [any_spec] * ng,
        out_shape=[jax.ShapeDtypeStruct((S, D), F32), jax.ShapeDtypeStruct((8, D), F32)]
                  + [jax.ShapeDtypeStruct(g.shape, g.dtype) for g in grads],
        scratch_shapes=[pltpu.VMEM(w_in_g.shape, w_in_g.dtype),
                        pltpu.SemaphoreType.DMA((ng, N_DEV - 1)), pltpu.SemaphoreType.DMA((ng, N_DEV - 1)),
                        pltpu.SemaphoreType.DMA((ng,))],
        compiler_params=_params(("arbitrary",)),
    )(*parts, w_in_g, x, dx2, mod3, norm_g, *grads)


def _gw_in(ht, pieces, grads, tm=1024):
    S = ht.shape[1]
    nt = S // tm
    ng = len(grads)
    parts, where = _piece_parts(pieces)
    npart = len(parts)

    def body(*refs):
        h_ref, p_refs = refs[0], refs[1:1 + npart]
        g_ins = refs[1 + npart:1 + npart + ng]
        o_ref = refs[1 + npart + ng]
        g_outs = refs[2 + npart + ng:2 + npart + 2 * ng]
        acc, send_sems, recv_sems, local_sems = refs[2 + npart + 2 * ng:]
        j, i = pl.program_id(0), pl.program_id(1)
        start, wait = _all_to_all_copies(g_ins, g_outs, send_sems, recv_sems, local_sems)

        @pl.when((j == 0) & (i == 0))
        def _():
            start()

        @pl.when(i == 0)
        def _():
            acc[...] = jnp.zeros_like(acc)

        for k in range(N_DEV):
            @pl.when(j == k)
            def _(k=k):
                for p_ref, (kk, off, width) in zip(p_refs, where):
                    if kk == k:
                        acc[:, off:off + width] += _mm(h_ref[...], p_ref[...])

        @pl.when(i == nt - 1)
        def _():
            o_ref[0] = acc[...].astype(XCHG_DTYPE)

        @pl.when((j == N_DEV - 1) & (i == nt - 1))
        def _():
            wait()

    def part_spec(k, width):
        return pl.BlockSpec((tm, width), lambda j, i: (jnp.where(j == k, i, 0), 0))

    any_spec = pl.BlockSpec(memory_space=pl.ANY)
    return pl.pallas_call(
        body, name="gw_in_scatter",
        grid=(N_DEV, nt),
        in_specs=[pl.BlockSpec((D, tm), lambda j, i: (0, i))] + [part_spec(k, width) for k, _, width in where] + [any_spec] * ng,
        out_specs=[pl.BlockSpec((1, D, D), lambda j, i: (j, 0, 0))] + [any_spec] * ng,
        out_shape=[jax.ShapeDtypeStruct((N_DEV, D, D), XCHG_DTYPE)]
                  + [jax.ShapeDtypeStruct(g.shape, g.dtype) for g in grads],
        scratch_shapes=[pltpu.VMEM((D, D), F32),
                        pltpu.SemaphoreType.DMA((ng, N_DEV - 1)), pltpu.SemaphoreType.DMA((ng, N_DEV - 1)),
                        pltpu.SemaphoreType.DMA((ng,))],
        compiler_params=_params(("arbitrary", "arbitrary")),
    )(ht, *parts, *grads)


def _adamw_math(w, g, m, v):
    m = ADAM_B1 * m + (1.0 - ADAM_B1) * g
    v = ADAM_B2 * v + (1.0 - ADAM_B2) * (g * g)
    m_hat = m / (1.0 - ADAM_B1 ** ADAM_STEP)
    v_hat = v / (1.0 - ADAM_B2 ** ADAM_STEP)
    delta = -ADAM_LR * (m_hat / (jnp.sqrt(v_hat) + ADAM_EPS) + ADAM_WD * w)
    return delta, m, v


def _adamw_big(recv, w, m, v, name, tr=128):
    M, N = w.shape
    tr = min(tr, M)

    def body(r_ref, w_ref, m_ref, v_ref, g_ref, d_ref, nm_ref, nv_ref):
        g = r_ref[0].astype(F32)
        for j in range(1, N_DEV):
            g = g + r_ref[j].astype(F32)
        g_ref[...] = g
        d_ref[...], nm_ref[...], nv_ref[...] = _adamw_math(w_ref[...], g, m_ref[...], v_ref[...])

    blk = pl.BlockSpec((tr, N), lambda i: (i, 0))
    return pl.pallas_call(
        body, name=name,
        grid=(M // tr,),
        in_specs=[pl.BlockSpec((N_DEV, tr, N), lambda i: (0, i, 0)), blk, blk, blk],
        out_specs=[blk] * 4,
        out_shape=[jax.ShapeDtypeStruct((M, N), F32)] * 4,
        compiler_params=_params(("parallel",)),
    )(recv, w, m, v)


def _adamw_w_ada(c64, dmod64, w, m, v):
    def body(c_ref, dm_ref, w_ref, m_ref, v_ref, g_ref, d_ref, nm_ref, nv_ref):
        cv = c_ref[...]
        g = _mm(cv * _sigmoid(cv), dm_ref[...], TN)
        g_ref[...] = g
        d_ref[...], nm_ref[...], nv_ref[...] = _adamw_math(w_ref[...], g, m_ref[...], v_ref[...])

    return pl.pallas_call(
        body, name="adamw_w_ada",
        out_shape=[jax.ShapeDtypeStruct(w.shape, F32)] * 4,
        compiler_params=_params(),
    )(c64, dmod64, w, m, v)


P_MOD, P_NORM, P_ONORM, P_RELB, P_LB, P_FINAL, P_LOSS, P_END = (0, 3 * D, 4 * D, 5 * D, 6 * D, 7 * D, 8 * D, 9 * D)


def _adamw_small(packed, b_ada, norm_g, onorm_g, relb, hgrn_lb, final_g, ms, vs):
    def body(pk_ref, b_ref, ng_ref, og_ref, rb_ref, lb_ref, fg_ref,
             mb, mn, mo, mr, ml, mf, vb, vn, vo, vr, vl, vf,
             loss_ref, gb, gn, go, gr, gl, gf, db, dn, do, dr, dl, df,
             nmb, nmn, nmo, nmr, nml, nmf, nvb, nvn, nvo, nvr, nvl, nvf):
        tot = pk_ref[0:1, :]
        for j in range(1, N_DEV):
            tot = tot + pk_ref[8 * j:8 * j + 1, :]
        loss_ref[...] = jnp.broadcast_to(jnp.sum(tot[:, P_LOSS:P_END], axis=-1, keepdims=True), (8, 128))

        def upd(g, w_ref, m_ref, v_ref, g_out, d_out, m_out, v_out):
            g_out[...] = g
            d_out[...], m_out[...], v_out[...] = _adamw_math(w_ref[...], g, m_ref[...], v_ref[...])

        upd(tot[:, P_MOD:P_NORM], b_ref, mb, vb, gb, db, nmb, nvb)
        upd(tot[:, P_NORM:P_ONORM], ng_ref, mn, vn, gn, dn, nmn, nvn)
        g_on = tot[:, P_ONORM:P_ONORM + G_DK]
        for h in range(1, G_HEADS):
            g_on = g_on + tot[:, P_ONORM + G_DK * h:P_ONORM + G_DK * (h + 1)]
        upd(g_on, og_ref, mo, vo, go, do, nmo, nvo)
        upd(tot[:, P_RELB:P_LB], rb_ref, mr, vr, gr, dr, nmr, nvr)
        a = lb_ref[...]
        lb = _sigmoid(a[0:1, :] - a[1:2, :])
        g0 = tot[:, P_LB:P_FINAL] * lb * (1.0 - lb)
        row = lax.broadcasted_iota(jnp.int32, (2, D), 0)
        upd(jnp.where(row == 0, g0, -g0), lb_ref, ml, vl, gl, dl, nml, nvl)
        upd(tot[:, P_FINAL:P_LOSS], fg_ref, mf, vf, gf, df, nmf, nvf)

    shapes = [b_ada.shape, norm_g.shape, onorm_g.shape, relb.shape, hgrn_lb.shape, final_g.shape]
    outs = [jax.ShapeDtypeStruct((8, 128), F32)] + [jax.ShapeDtypeStruct(s, F32) for s in shapes] * 4
    return pl.pallas_call(
        body, name="adamw_small",
        out_shape=outs,
        compiler_params=_params(),
    )(packed, b_ada, norm_g, onorm_g, relb, hgrn_lb, final_g, *ms, *vs)


def _local_step(x, target, mod3, norm_g, w_in_g, onorm_g, wa_blk, wb_blk, wo_blk, rel_bias, hgrn_lb, final_g):
    buckets = jnp.asarray(_bucket_tables())
    bias = _bias_tables(rel_bias, buckets)
    proj, ht, qkv, wa_g, wb_g, wo_g = _inproj(x, mod3, norm_g, w_in_g, [wa_blk, wb_blk, wo_blk])
    wa = wa_g.transpose(1, 0, 2).reshape(A_W, D)
    wb = wb_g.reshape(D, D)
    wo = wo_g.reshape(D, D)
    os, ls = [], []
    for p, (_, d) in enumerate(PATTERNS):
        o, l = _attn_fwd(qkv, bias[p], d, "attn_fwd_d%d" % d)
        os.append(o)
        ls.append(l)
    o_raw, ob, states = _hgrn_fwd(proj, hgrn_lb, onorm_g)
    lt, dx2, do, delta, dza, dob, dga, dgb, tsums, gwa, gwb, gwo = _tail(
        x, target, os, ls, ob, proj, mod3, final_g, wa, wb, wo)
    dbs, acc = [None] * len(PATTERNS), ()
    for p in reversed(range(len(PATTERNS))):
        d = PATTERNS[p][1]
        *acc, dbs[p] = _attn_bwd(qkv, do, lt, delta, bias[p], d, "attn_bwd_d%d" % d, prev=tuple(acc),
                                 out_dtype=MXU_DTYPE if p == 0 else F32)
    dqa, dka, dva = acc
    g_relb = _rel_bias_grad(dbs, buckets)
    dqb, dfb, dib, dzb, dlb, dgo = _hgrn_bwd(proj, o_raw, dob, states, hgrn_lb, onorm_g)
    pieces = [[dqa, dka], [dva, dza], [dqb], [dfb], [dib], [dzb], [dga], [dgb]]
    small = [gwa.astype(XCHG_DTYPE).reshape(A_W, N_DEV, D // N_DEV).transpose(1, 0, 2),
             gwb.astype(XCHG_DTYPE).reshape(N_DEV, D // N_DEV, D),
             gwo.astype(XCHG_DTYPE).reshape(N_DEV, D // N_DEV, D)]
    gw_in, *received_small = _gw_in(ht, pieces, small)
    gx, hsums, received_in = _dh(pieces, w_in_g, x, dx2, mod3, norm_g, [gw_in])
    received = [received_in] + received_small
    row = jnp.concatenate([
        hsums[0], hsums[1], tsums[1],
        hsums[2],
        dgo.reshape(G_HEADS, 8, G_DK)[:, 0].reshape(-1),
        g_relb.reshape(-1),
        dlb[0],
        tsums[0],
        tsums[2],
    ])
    return gx, received, row


def kernel(x, c, w_ada, b_ada, norm_g, w_in, hgrn_onorm_g, w_branch_a, w_branch_b, w_out, rel_bias, hgrn_lb, final_g, loss_target, m_w_ada, m_b_ada, m_norm_g, m_w_in, m_hgrn_onorm_g, m_w_branch_a, m_w_branch_b, m_w_out, m_rel_bias, m_hgrn_lb, m_final_g, v_w_ada, v_b_ada, v_norm_g, v_w_in, v_hgrn_onorm_g, v_w_branch_a, v_w_branch_b, v_w_out, v_rel_bias, v_hgrn_lb, v_final_g):
    me = 4 * lax.axis_index("x") + 2 * lax.axis_index("y") + lax.axis_index("c")
    n_ada = w_ada.shape[2]

    w_in_g, c_all = _all_gather([w_in[0].astype(MXU_DTYPE), jnp.broadcast_to(c, (8, D))], "gather_w_in_c")

    c64 = c_all.reshape(8 * N_DEV, D)
    b_loc = lax.dynamic_slice(b_ada, (0, me * n_ada), (1, n_ada))
    mod_part = _mod_fwd(c64, w_ada[0], b_loc)[::8]
    (mod_all,) = _all_gather([mod_part], "gather_mod")
    mod = lax.dynamic_slice(mod_all, (0, me, 0), (N_DEV, 1, n_ada)).reshape(3, D)
    mod3 = jnp.concatenate([mod, jnp.zeros((5, D), F32)], axis=0)

    onorm_t = hgrn_onorm_g
    gx, (r_in, r_a, r_b, r_o), row = _local_step(
        x[0], loss_target[0], mod3, norm_g, w_in_g, onorm_t, w_branch_a[0].astype(MXU_DTYPE),
        w_branch_b[0].astype(MXU_DTYPE), w_out[0].astype(MXU_DTYPE), rel_bias, hgrn_lb,
        final_g.reshape(1, D))
    packed8 = jnp.concatenate([row[None, :], jnp.zeros((7, P_END), F32)], axis=0)
    (packed,) = _all_gather([packed8], "gather_small")
    packed = packed.reshape(8 * N_DEV, P_END)

    g_in, d_in, nm_in, nv_in = _adamw_big(r_in, w_in[0], m_w_in[0], v_w_in[0], "adamw_w_in")
    g_a, d_a, nm_a, nv_a = _adamw_big(r_a, w_branch_a[0], m_w_branch_a[0], v_w_branch_a[0], "adamw_w_branch_a")
    g_b, d_b, nm_b, nv_b = _adamw_big(r_b, w_branch_b[0], m_w_branch_b[0], v_w_branch_b[0], "adamw_w_branch_b")
    g_o, d_o, nm_o, nv_o = _adamw_big(r_o, w_out[0], m_w_out[0], v_w_out[0], "adamw_w_out")

    dmod64 = lax.dynamic_slice(packed, (0, P_MOD + me * n_ada), (8 * N_DEV, n_ada))
    g_ada, d_ada, nm_ada, nv_ada = _adamw_w_ada(c64, dmod64, w_ada[0], m_w_ada[0], v_w_ada[0])

    def flat_relb(t):
        return jnp.pad(t.T, ((0, 0), (0, 128 - N_BUCKETS))).reshape(1, A_HEADS * 128)

    def unflat_relb(t):
        return t.reshape(A_HEADS, 128)[:, :N_BUCKETS].T

    fg2 = lambda t: t.reshape(1, D)
    smalls = _adamw_small(
        packed, b_ada, norm_g, hgrn_onorm_g, flat_relb(rel_bias), hgrn_lb, fg2(final_g),
        [m_b_ada, m_norm_g, m_hgrn_onorm_g, flat_relb(m_rel_bias), m_hgrn_lb, fg2(m_final_g)],
        [v_b_ada, v_norm_g, v_hgrn_onorm_g, flat_relb(v_rel_bias), v_hgrn_lb, fg2(v_final_g)])
    loss = smalls[0][0, 0]

    def small(kind):
        s = smalls[1 + 6 * kind:7 + 6 * kind]
        return s[0], s[1], s[2], unflat_relb(s[3]), s[4], s[5].reshape(D)

    def leaves(ada, sm, w_in_, wa_, wb_, wo_):
        b_, n_, o_, r_, l_, f_ = sm
        return (ada[None], b_, n_, w_in_[None], o_, wa_[None], wb_[None], wo_[None], r_, l_, f_)

    return (loss, gx[None],
            *leaves(g_ada, small(0), g_in, g_a, g_b, g_o),
            *leaves(d_ada, small(1), d_in, d_a, d_b, d_o),
            *leaves(nm_ada, small(2), nm_in, nm_a, nm_b, nm_o),
            *leaves(nv_ada, small(3), nv_in, nv_a, nv_b, nv_o))
```

```python
import functools
import math

import numpy as np
import jax
import jax.numpy as jnp
from jax import lax
from jax.experimental import pallas as pl
from jax.experimental.pallas import tpu as pltpu

F32 = jnp.float32
BF16 = jnp.bfloat16
MXU_DTYPE = jnp.bfloat16
XCHG_DTYPE = jnp.bfloat16

N_DEV = 8
D = 1024
A_HEADS = 8
A_HD = 64
A_W = A_HEADS * A_HD
A_BLK = 128
PATTERNS = ((128, 1), (512, 4), (2048, 16))
N_BUCKETS = 32
MAX_DISTANCE = 2048
NEG = -1e30
G_HEADS = 8
G_DK = 128
G_W = G_HEADS * G_DK
IN_W = 8 * D
EPS = 1e-6
ADAM_LR = 0.001
ADAM_B1 = 0.9
ADAM_B2 = 0.999
ADAM_EPS = 1e-08
ADAM_WD = 0.01
ADAM_STEP = 10

G_CHUNK = 128
G_SUB = 8
G_HPS_FWD = 8
G_HPS_BWD = 8
G_RB = 16
VMEM_LIMIT = 56 * 1024 * 1024

NN = (((1,), (0,)), ((), ()))
NT = (((1,), (1,)), ((), ()))
TN = (((0,), (0,)), ((), ()))
MESH = pl.DeviceIdType.MESH


def _mm(a, b, dims=NN):
    return lax.dot_general(a.astype(MXU_DTYPE), b.astype(MXU_DTYPE), dims,
                           preferred_element_type=F32)


def _mm_exact(t, x):
    hi = x.astype(BF16)
    r = x - hi.astype(F32)
    mid = r.astype(BF16)
    lo = (r - mid.astype(F32)).astype(BF16)
    tb = t.astype(BF16)
    return sum(lax.dot_general(tb, p, NN, preferred_element_type=F32) for p in (hi, mid, lo))


def _split(x):
    hi = x.astype(BF16)
    return hi, (x - hi.astype(F32)).astype(BF16)


def _mm_split(a, b, dims):
    dot = lambda p, q: lax.dot_general(p, q, dims, preferred_element_type=F32)
    return dot(a[0], b[0]) + dot(a[0], b[1]) + dot(a[1], b[0])


def _sigmoid(x):
    return 0.5 * jnp.tanh(0.5 * x) + 0.5


def _params(sem=None):
    return pltpu.CompilerParams(dimension_semantics=sem, vmem_limit_bytes=VMEM_LIMIT)


def _all_gather(xs, name):
    n = len(xs)

    def body(*refs):
        ins, outs = refs[:n], refs[n:2 * n]
        send_sems, recv_sems, local_sems = refs[2 * n:]
        x, y, c = lax.axis_index("x"), lax.axis_index("y"), lax.axis_index("c")
        me, sibling = (x, y, c), (x, y, 1 - c)
        chips = [(1 - x, y), (x, 1 - y), (1 - x, 1 - y)]

        def slot(ref, dev):
            return ref.at[4 * dev[0] + 2 * dev[1] + dev[2]]

        def copy(a, k, block, to, src=None):
            return pltpu.make_async_remote_copy(
                src_ref=slot(outs[a], block) if src is None else src,
                dst_ref=slot(outs[a], block),
                send_sem=send_sems.at[a, k], recv_sem=recv_sems.at[a, k],
                device_id=to, device_id_type=MESH)

        mine, first, passed = [], [], []
        for a in range(n):
            cp = pltpu.make_async_copy(ins[a], slot(outs[a], me), local_sems.at[a])
            cp.start()
            mine.append(cp)
            first.append(copy(a, 0, me, sibling, src=ins[a]))
            for j, chip in enumerate(chips):
                first.append(copy(a, 1 + j, me, (*chip, c), src=ins[a]))
        for cp in first:
            cp.start()
        for j, chip in enumerate(chips):
            for a in range(n):
                copy(a, 1 + j, (*chip, c), me).wait_recv()
                cp = copy(a, 4 + j, (*chip, c), sibling)
                cp.start()
                passed.append(cp)
        for a in range(n):
            copy(a, 0, sibling, me).wait_recv()
            for j, chip in enumerate(chips):
                copy(a, 4 + j, (*chip, 1 - c), me).wait_recv()
        for cp in first + passed:
            cp.wait_send()
        for cp in mine:
            cp.wait()

    any_spec = pl.BlockSpec(memory_space=pl.ANY)
    return pl.pallas_call(
        body, name=name,
        out_shape=[jax.ShapeDtypeStruct((N_DEV,) + v.shape, v.dtype) for v in xs],
        in_specs=[any_spec] * n, out_specs=[any_spec] * n,
        scratch_shapes=[pltpu.SemaphoreType.DMA((n, 7)), pltpu.SemaphoreType.DMA((n, 7)),
                        pltpu.SemaphoreType.DMA((n,))],
    )(*xs)


def _all_to_all_copies(ins, outs, send_sems, recv_sems, local_sems, gather=False):
    n = len(ins)
    x, y, c = lax.axis_index("x"), lax.axis_index("y"), lax.axis_index("c")
    me = 4 * x + 2 * y + c
    peers = []
    for m in range(1, N_DEV):
        peers.append((1 - x if m & 4 else x, 1 - y if m & 2 else y, 1 - c if m & 1 else c))

    def chunk(a, j):
        return ins[a] if gather else ins[a].at[j]

    def copy(a, k, landing):
        peer = peers[k]
        pid = 4 * peer[0] + 2 * peer[1] + peer[2]
        return pltpu.make_async_remote_copy(
            src_ref=chunk(a, pid), dst_ref=outs[a].at[pid if landing else me],
            send_sem=send_sems.at[a, k], recv_sem=recv_sems.at[a, k],
            device_id=peer, device_id_type=MESH)

    def local(a):
        return pltpu.make_async_copy(chunk(a, me), outs[a].at[me], local_sems.at[a])

    def start():
        for a in range(n):
            local(a).start()
        for k in range(N_DEV - 1):
            for a in range(n):
                copy(a, k, False).start()

    def wait():
        for k in range(N_DEV - 1):
            for a in range(n):
                copy(a, k, True).wait_recv()
        for k in range(N_DEV - 1):
            for a in range(n):
                copy(a, k, False).wait_send()
        for a in range(n):
            local(a).wait()

    return start, wait


def _mod_fwd(c64, w_ada, b_loc):
    def body(c_ref, w_ref, b_ref, o_ref):
        cv = c_ref[...]
        sc = cv * _sigmoid(cv)
        o_ref[...] = _mm(sc, w_ref[...]) + b_ref[...]

    return pl.pallas_call(
        body, name="mod_fwd",
        out_shape=jax.ShapeDtypeStruct((c64.shape[0], w_ada.shape[1]), F32),
        compiler_params=_params(),
    )(c64, w_ada, b_loc)


def _inproj(x, mod3, norm_g, w_in_g, blocks, tm=256):
    S = x.shape[0]
    ni = S // tm
    nb = len(blocks)

    def body(*refs):
        x_ref, mod_ref, g_ref, w_ref = refs[:4]
        b_ins = refs[4:4 + nb]
        proj_ref, ht_ref, qkv_ref = refs[4 + nb:7 + nb]
        b_outs = refs[7 + nb:7 + 2 * nb]
        w_all, send_sems, recv_sems, local_sems = refs[7 + 2 * nb:]
        i = pl.program_id(0)
        start, wait = _all_to_all_copies(b_ins, b_outs, send_sems, recv_sems, local_sems, gather=True)

        @pl.when(i == 0)
        def _():
            start()
            pltpu.sync_copy(w_ref, w_all)

        xv = x_ref[...]
        r = lax.rsqrt(jnp.mean(xv * xv, axis=-1, keepdims=True) + EPS)
        h = ((xv * r * g_ref[...]) * (1.0 + mod_ref[1:2, :]) + mod_ref[0:1, :]).astype(MXU_DTYPE)
        ht_ref[...] = h.T
        for j in range(N_DEV):
            pj = _mm(h, w_all[j])
            proj_ref[:, j * D:(j + 1) * D] = pj
            for c in range(3):
                if c // 2 == j:
                    for p in range(A_HEADS // 2):
                        lo = (c % 2) * A_W + 2 * A_HD * p
                        qkv_ref[c, p] = pj[:, lo:lo + 2 * A_HD]

        @pl.when(i == ni - 1)
        def _():
            wait()

    any_spec = pl.BlockSpec(memory_space=pl.ANY)
    return pl.pallas_call(
        body, name="inproj_gather",
        grid=(ni,),
        in_specs=[pl.BlockSpec((tm, D), lambda i: (i, 0)),
                  pl.BlockSpec((8, D), lambda i: (0, 0)),
                  pl.BlockSpec((1, D), lambda i: (0, 0)),
                  any_spec] + [any_spec] * nb,
        out_specs=[pl.BlockSpec((tm, IN_W), lambda i: (i, 0)),
                   pl.BlockSpec((D, tm), lambda i: (0, i)),
                   pl.BlockSpec((3, A_HEADS // 2, tm, 2 * A_HD), lambda i: (0, 0, i, 0))] + [any_spec] * nb,
        out_shape=[jax.ShapeDtypeStruct((S, IN_W), F32), jax.ShapeDtypeStruct((D, S), MXU_DTYPE),
                   jax.ShapeDtypeStruct((3, A_HEADS // 2, S, 2 * A_HD), F32)]
                  + [jax.ShapeDtypeStruct((N_DEV,) + b.shape, b.dtype) for b in blocks],
        scratch_shapes=[pltpu.VMEM(w_in_g.shape, w_in_g.dtype),
                        pltpu.SemaphoreType.DMA((nb, N_DEV - 1)), pltpu.SemaphoreType.DMA((nb, N_DEV - 1)),
                        pltpu.SemaphoreType.DMA((nb,))],
        compiler_params=_params(("arbitrary",)),
    )(x, mod3, norm_g, w_in_g, *blocks)


def _bucket_tables():
    qi = np.arange(A_BLK)[:, None]
    kj = np.arange(2 * A_BLK)[None, :]
    delta = qi + A_BLK - kj
    out = []
    for window, dil in PATTERNS:
        span = window // dil
        band = (delta >= 0) & (delta <= span)
        dist = np.clip(delta, 0, None) * dil
        max_exact = N_BUCKETS // 2
        nf = dist.astype(np.float32)
        large = max_exact + (np.log(np.maximum(nf, np.float32(1.0)) / np.float32(max_exact))
                             / np.float32(math.log(MAX_DISTANCE / max_exact))
                             * np.float32(N_BUCKETS - max_exact)).astype(np.int32)
        large = np.minimum(large, N_BUCKETS - 1)
        bucket = np.where(dist < max_exact, dist, large)
        out.append(np.where(band, bucket, -1).astype(np.int32))
    return np.stack(out)


def _bias_tables(rel_bias, buckets):
    def body(rb_ref, bk_ref, o_ref):
        h = pl.program_id(1)
        bk = bk_ref[0]
        acc = jnp.full(bk.shape, NEG, F32)
        for b in range(N_BUCKETS):
            acc = jnp.where(bk == b, rb_ref[b, h], acc)
        o_ref[0, 0] = acc

    return pl.pallas_call(
        body, name="bias_tables",
        grid=(3, A_HEADS),
        in_specs=[pl.BlockSpec(memory_space=pltpu.SMEM),
                  pl.BlockSpec((1, A_BLK, 2 * A_BLK), lambda p, h: (p, 0, 0))],
        out_specs=pl.BlockSpec((1, 1, A_BLK, 2 * A_BLK), lambda p, h: (p, h, 0, 0)),
        out_shape=jax.ShapeDtypeStruct((3, A_HEADS, A_BLK, 2 * A_BLK), F32),
        compiler_params=_params(("arbitrary", "arbitrary")),
    )(rel_bias, buckets)


A_TILES = 16


def _attn_heads_per_step(d):
    return A_HEADS if d == 1 else 2


def _attn_in_specs(sb, nsb, hw):
    blk = (1, hw // 2, sb, 2 * A_HD)

    def cur(c):
        return pl.BlockSpec(blk, lambda hp, n: (c, hp, jnp.minimum(n, nsb - 1), 0))

    def prev(c):
        return pl.BlockSpec(blk, lambda hp, n: (c, hp, jnp.maximum(jnp.minimum(n, nsb - 1) - 1, 0), 0))

    return [cur(0), prev(1), cur(1), prev(2), cur(2)]


def _rows(r, d):
    return pl.ds(r, A_BLK) if d == 1 else pl.ds(r, A_BLK, stride=d)


def _for_residues(d, hw, fn):
    unroll = min(d, max(1, A_TILES // hw))
    if d == unroll:
        _round_robin([g for r in range(d) for g in fn(r)])
    else:
        def group(g, c):
            _round_robin([t for u in range(unroll) for t in fn(g * unroll + u)])
            return c
        lax.fori_loop(0, d // unroll, group, 0)


def _attn_stack(t):
    first_half = lax.broadcasted_iota(jnp.int32, (1, 2 * A_HD), 1) < A_HD
    return jnp.concatenate([jnp.where(first_half, t, 0.0), jnp.where(first_half, 0.0, t)], axis=0)


def _attn_unstack(t2):
    first_half = lax.broadcasted_iota(jnp.int32, (1, 2 * A_HD), 1) < A_HD
    return jnp.where(first_half, t2[:A_BLK], t2[A_BLK:])


def _attn_scores(q, k, b_ref, pp, first):
    bias = jnp.concatenate([b_ref[2 * pp] + first, b_ref[2 * pp + 1] + first], axis=0)
    return _mm(_attn_stack(q), k, NT) * (A_HD ** -0.5) + bias


def _attn_fwd(qkv, bias_p, d, name):
    S = qkv.shape[2]
    hw = _attn_heads_per_step(d)
    sub = A_BLK * d
    nsub = max(1, A_TILES // (hw * d))
    sb = sub * nsub
    nsb = S // sb

    def body(q_ref, kp_ref, kc_ref, vp_ref, vc_ref, b_ref, o_ref, l_ref):
        n = pl.program_id(1)
        kj = lax.broadcasted_iota(jnp.int32, (A_BLK, 2 * A_BLK), 1)
        first = jnp.where((n == 0) & (kj < A_BLK), NEG, 0.0).astype(F32)
        no_first = jnp.zeros((A_BLK, 2 * A_BLK), F32)

        def residue(r, u=0):
            rows = _rows(u * sub + r, d)
            behind = _rows(((nsub if u == 0 else u) - 1) * sub + r, d)

            def pair(pp):
                lanes = pl.ds(2 * A_HD * pp, 2 * A_HD)
                kc, vc = kc_ref.at[0, pp], vc_ref.at[0, pp]
                kb, vb = (kp_ref.at[0, pp], vp_ref.at[0, pp]) if u == 0 else (kc, vc)
                k = jnp.concatenate([kb[behind, :], kc[rows, :]], axis=0)
                v = jnp.concatenate([vb[behind, :], vc[rows, :]], axis=0)
                s = _attn_scores(q_ref.at[0, pp][rows, :], k, b_ref, pp, first if u == 0 else no_first)
                yield
                m = jnp.max(s, axis=-1, keepdims=True)
                p = jnp.exp(s - m)
                den = jnp.sum(p, axis=-1, keepdims=True)
                pv = _mm(p, v)
                yield
                o_ref[rows, lanes] = _attn_unstack(pv / den)
                l_ref[rows, lanes] = _attn_unstack(jnp.broadcast_to(m + jnp.log(den), (2 * A_BLK, 2 * A_HD)))

            return [pair(pp) for pp in range(hw // 2)]

        if nsub == 1:
            _for_residues(d, hw, residue)
        else:
            _round_robin([g for u in range(nsub) for r in range(d) for g in residue(r, u)])

    out = pl.BlockSpec((sb, A_HD * hw), lambda hp, n: (n, hp))
    return pl.pallas_call(
        body, name=name,
        grid=(A_HEADS // hw, nsb),
        in_specs=_attn_in_specs(sb, nsb, hw) + [pl.BlockSpec((hw, A_BLK, 2 * A_BLK), lambda hp, n: (hp, 0, 0))],
        out_specs=[out, out],
        out_shape=[jax.ShapeDtypeStruct((S, A_W), F32)] * 2,
        compiler_params=_params(("parallel", "parallel")),
    )(qkv, qkv, qkv, qkv, qkv, bias_p)


def _attn_bwd(qkv, do, lt, delta, bias_p, d, name, prev=(), out_dtype=F32):
    S = qkv.shape[2]
    hw = _attn_heads_per_step(d)
    sub = A_BLK * d
    nsub = max(1, A_TILES // (hw * d))
    sb = sub * nsub
    nsb = S // sb
    done = (nsub - 1) * sub

    def body(*refs):
        q_ref, kp_ref, kc_ref, vp_ref, vc_ref, do_ref, lt_ref, dl_ref, b_ref = refs[:9]
        pq_ref, pk_ref, pv_ref = refs[9:9 + len(prev)] if prev else (None, None, None)
        dq_ref, dk_ref, dv_ref, db_ref, ck, cv = refs[9 + len(prev):]
        n = pl.program_id(1)
        plus = lambda t, p_ref, idx: (t if p_ref is None else t + p_ref[idx]).astype(out_dtype)

        @pl.when(n == 0)
        def _():
            db_ref[...] = jnp.zeros_like(db_ref)
            ck[...] = jnp.zeros_like(ck)
            cv[...] = jnp.zeros_like(cv)

        @pl.when(n < nsb)
        def _():
            kj = lax.broadcasted_iota(jnp.int32, (A_BLK, 2 * A_BLK), 1)
            first = jnp.where((n == 0) & (kj < A_BLK), NEG, 0.0).astype(F32)
            no_first = jnp.zeros((A_BLK, 2 * A_BLK), F32)
            if done:
                dk_ref[0:done, :] = plus(ck[0:done, :], pk_ref, (slice(0, done), slice(None)))
                dv_ref[0:done, :] = plus(cv[0:done, :], pv_ref, (slice(0, done), slice(None)))

            def residue(r, u=0):
                rows = _rows(u * sub + r, d)
                behind = _rows(((nsub if u == 0 else u) - 1) * sub + r, d)

                def pair(pp):
                    lanes = pl.ds(2 * A_HD * pp, 2 * A_HD)
                    lt_r, dl_r = lt_ref[rows, lanes], dl_ref[rows, lanes]
                    kc, vc = kc_ref.at[0, pp], vc_ref.at[0, pp]
                    kb, vb = (kp_ref.at[0, pp], vp_ref.at[0, pp]) if u == 0 else (kc, vc)
                    k = jnp.concatenate([kb[behind, :], kc[rows, :]], axis=0)
                    v = jnp.concatenate([vb[behind, :], vc[rows, :]], axis=0)
                    q2 = _attn_stack(q_ref.at[0, pp][rows, :])
                    do2 = _attn_stack(do_ref[rows, lanes])
                    col = lambda t: jnp.concatenate([t[:, 0:1], t[:, A_HD:A_HD + 1]], axis=0)
                    s = _attn_scores(q_ref.at[0, pp][rows, :], k, b_ref, pp, first if u == 0 else no_first)
                    dp = _mm(do2, v, NT)
                    yield
                    p = jnp.exp(s - col(lt_r))
                    ds = p * (dp - col(dl_r))
                    db_ref[2 * pp] += ds[:A_BLK]
                    db_ref[2 * pp + 1] += ds[A_BLK:]
                    dq = _mm(ds, k)
                    dk = _mm(ds, q2, TN) * (A_HD ** -0.5)
                    dv = _mm(p, do2, TN)
                    yield
                    dq_ref[rows, lanes] = plus(_attn_unstack(dq) * (A_HD ** -0.5), pq_ref, (rows, lanes))
                    if u == 0:
                        dk_ref[behind, lanes] = plus(ck[behind, lanes] + dk[:A_BLK], pk_ref, (behind, lanes))
                        dv_ref[behind, lanes] = plus(cv[behind, lanes] + dv[:A_BLK], pv_ref, (behind, lanes))
                    else:
                        ck[behind, lanes] += dk[:A_BLK]
                        cv[behind, lanes] += dv[:A_BLK]
                    ck[rows, lanes] = dk[A_BLK:]
                    cv[rows, lanes] = dv[A_BLK:]

                return [pair(pp) for pp in range(hw // 2)]

            if nsub == 1:
                _for_residues(d, hw, residue)
            else:
                _round_robin([g for u in range(nsub) for r in range(d) for g in residue(r, u)])

        @pl.when(n == nsb)
        def _():
            dk_ref[...] = plus(ck[...], pk_ref, ...)
            dv_ref[...] = plus(cv[...], pv_ref, ...)

    w = A_HD * hw
    row = pl.BlockSpec((sb, w), lambda hp, n: (jnp.minimum(n, nsb - 1), hp))
    lag = pl.BlockSpec((sb, w), lambda hp, n: (jnp.maximum(n - 1, 0), hp))
    tab = pl.BlockSpec((hw, A_BLK, 2 * A_BLK), lambda hp, n: (hp, 0, 0))
    return pl.pallas_call(
        body, name=name,
        grid=(A_HEADS // hw, nsb + 1),
        in_specs=_attn_in_specs(sb, nsb, hw) + [row, row, row, tab] + ([row, lag, lag] if prev else []),
        out_specs=[row, lag, lag, tab],
        out_shape=[jax.ShapeDtypeStruct((S, A_W), out_dtype)] * 3
                  + [jax.ShapeDtypeStruct((A_HEADS, A_BLK, 2 * A_BLK), F32)],
        scratch_shapes=[pltpu.VMEM((sb, w), F32), pltpu.VMEM((sb, w), F32)],
        compiler_params=_params(("parallel", "arbitrary")),
    )(qkv, qkv, qkv, qkv, qkv, do, lt, delta, bias_p, *prev)


def _rel_bias_grad(dbs, buckets):
    def body(d1, d2, d3, bk_ref, o_ref):
        row = lax.broadcasted_iota(jnp.int32, (A_HEADS, 128), 0)
        lane = lax.broadcasted_iota(jnp.int32, (A_HEADS, 128), 1)
        acc = jnp.zeros((A_HEADS, 128), F32)
        for p, dref in enumerate((d1, d2, d3)):
            bk = bk_ref[p]
            for h in range(A_HEADS):
                ds = dref[h]
                for b in range(N_BUCKETS):
                    s = jnp.sum(jnp.where(bk == b, ds, 0.0), keepdims=True)
                    acc = acc + jnp.where((row == h) & (lane == b), s, 0.0)
        o_ref[...] = acc

    return pl.pallas_call(
        body, name="rel_bias_grad",
        out_shape=jax.ShapeDtypeStruct((A_HEADS, 128), F32),
        compiler_params=_params(),
    )(*dbs, buckets)


def _tri(c):
    t = np.tril(np.ones((c, c), np.float32))
    return jnp.asarray(t), jnp.asarray(t.T.copy())


def _fill_above(ref, x, pad):
    ref[0:G_SUB, :] = jnp.full((G_SUB, x.shape[1]), pad, F32)
    ref[G_SUB:, :] = x


def _fill_below(ref, x, pad):
    ref[0:x.shape[0], :] = x
    ref[x.shape[0]:, :] = jnp.full((G_SUB, x.shape[1]), pad, F32)


def _hgrn_gates(q_ref, f_ref, lbp_ref, tri_ref):
    qraw = q_ref[...]
    sq = _sigmoid(qraw)
    q = qraw * sq
    sg = _sigmoid(f_ref[...])
    lb = _sigmoid(lbp_ref[0:1, :] - lbp_ref[1:2, :])
    f = lb + (1.0 - lb) * sg
    k = 1.0 - f
    b = _mm_exact(tri_ref[...], jnp.log(f))
    return qraw, sq, q, sg, lb, f, k, b


def _hgrn_col(C, base, idx, hps):
    return pl.BlockSpec((C, hps * G_DK), lambda h, n: (idx(n), base * (G_HEADS // hps) + h))


def _round_robin(stages):
    live = list(stages)
    while live:
        nxt = []
        for g in live:
            try:
                next(g)
                nxt.append(g)
            except StopIteration:
                pass
        live = nxt


def _hgrn_levels(C):
    out, m = [], G_SUB
    while 2 * m <= C:
        out.append(m)
        m *= 2
    return out


def _hgrn_level_masks(C):
    ti = np.arange(C)[:, None]
    si = np.arange(C)[None, :]
    return jnp.asarray(np.stack([((ti // (2 * m) == si // (2 * m)) & (ti - si >= G_SUB)).astype(np.float32)
                                 for m in _hgrn_levels(C)]))


def _hgrn_level(b, q, k, C, m):
    zeros = jnp.zeros((m, G_DK), F32)
    eq, ek, qt, kt = [], [], [], []
    for blk in range(0, C // m, 2):
        lo, mid, hi = blk * m, (blk + 1) * m, (blk + 2) * m
        ref = b[mid:mid + 1]
        e_right = jnp.exp(b[mid:hi] - ref)
        e_left = jnp.exp(ref - b[lo:mid])
        eq += [zeros, e_right]
        ek += [e_left, zeros]
        qt += [zeros, q[mid:hi] * e_right]
        kt += [k[lo:mid] * e_left, zeros]
    cat = lambda parts: jnp.concatenate(parts, axis=0)
    return cat(qt), cat(kt), cat(eq), cat(ek)


def _hgrn_fwd(proj, hgrn_lb, onorm_g, C=G_CHUNK):
    S = proj.shape[0]
    nc = S // C
    tri, _ = _tri(C)
    masks = _hgrn_level_masks(C)
    hps = G_HPS_FWD

    def body(q_ref, f_ref, i_ref, z_ref, lbp_ref, go_ref, tri_ref, pm_ref, o_ref, ob_ref, st_ref, St, kp, vp, fp):
        @pl.when(pl.program_id(1) == 0)
        def _():
            St[...] = jnp.zeros_like(St)

        heads = []
        for hh in range(hps):
            ln = pl.ds(G_DK * hh, G_DK)
            heads.append(head(
                q_ref.at[:, ln], f_ref.at[:, ln], i_ref.at[:, ln], z_ref.at[:, ln], lbp_ref.at[:, ln], go_ref,
                tri_ref, pm_ref, o_ref.at[:, ln], ob_ref.at[:, ln], st_ref.at[0, hh], St.at[hh], kp.at[hh], vp.at[hh],
                fp.at[hh]))
        _round_robin(heads)

    def head(q_ref, f_ref, i_ref, z_ref, lbp_ref, go_ref, tri_ref, pm_ref, o_ref, ob_ref, st_ref, St, kp, vp, fp):
        _, _, q, _, _, f, k, b = _hgrn_gates(q_ref, f_ref, lbp_ref, tri_ref)
        v = i_ref[...]
        bC = b[C - 1:C, :]
        S0 = St[...]
        o = _mm(q * jnp.exp(b), S0, NT)
        yield
        _fill_above(kp, k, 0.0)
        _fill_above(vp, v, 0.0)
        _fill_above(fp, f, 1.0)
        near = []
        for r0 in range(0, C, G_RB):
            qb = q[r0:r0 + G_RB]
            acc = e = None
            for l in range(G_SUB):
                rows = pl.ds(G_SUB - l + r0, G_RB)
                if l > 0:
                    fl = fp[pl.ds(G_SUB - l + 1 + r0, G_RB), :]
                    e = fl if e is None else e * fl
                kl = kp[rows, :]
                a = jnp.sum(qb * kl if e is None else qb * kl * e, axis=-1, keepdims=True)
                t = a * vp[rows, :]
                acc = t if acc is None else acc + t
            near.append(acc)
        o = o + jnp.concatenate(near, axis=0)
        yield
        a_off = jnp.zeros((C, C), F32)
        for lv, m in enumerate(_hgrn_levels(C)):
            qt, kt, _, _ = _hgrn_level(b, q, k, C, m)
            prod = _mm_split(_split(qt), _split(kt), NT) if m == G_SUB else _mm(qt, kt, NT)
            a_off = a_off + pm_ref[lv] * prod
        yield
        o = o + _mm(a_off, v)
        S1 = S0 * jnp.exp(bC) + _mm(v, k * jnp.exp(bC - b), TN)
        St[...] = S1
        st_ref[...] = S1
        o_ref[...] = o
        r = lax.rsqrt(jnp.mean(o * o, axis=-1, keepdims=True) + EPS)
        z = z_ref[...]
        ob_ref[...] = (o * r * go_ref[...] * (z * _sigmoid(z))).astype(MXU_DTYPE)

    ident = lambda n: n
    w = hps * G_DK
    out = pl.BlockSpec((C, w), lambda h, n: (n, h))
    return pl.pallas_call(
        body, name="hgrn_fwd",
        grid=(G_HEADS // hps, nc),
        in_specs=[_hgrn_col(C, base, ident, hps) for base in (2, 3, 4, 5)] + [
                  pl.BlockSpec((2, w), lambda h, n: (0, h)),
                  pl.BlockSpec((1, G_DK), lambda h, n: (0, 0)),
                  pl.BlockSpec((C, C), lambda h, n: (0, 0)),
                  pl.BlockSpec(masks.shape, lambda h, n: (0, 0, 0))],
        out_specs=[out, out, pl.BlockSpec((1, hps, G_DK, G_DK), lambda h, n: (n, h, 0, 0))],
        out_shape=[jax.ShapeDtypeStruct((S, G_W), F32), jax.ShapeDtypeStruct((S, G_W), MXU_DTYPE),
                   jax.ShapeDtypeStruct((nc, G_HEADS, G_DK, G_DK), F32)],
        scratch_shapes=[pltpu.VMEM((hps, G_DK, G_DK), F32)] + [pltpu.VMEM((hps, C + G_SUB, G_DK), F32)] * 3,
        compiler_params=_params(("parallel", "arbitrary")),
    )(proj, proj, proj, proj, hgrn_lb, onorm_g, tri, masks)


def _hgrn_bwd(proj, o_raw, dob, states, hgrn_lb, onorm_g, C=G_CHUNK):
    S = proj.shape[0]
    nc = S // C
    tri, triu = _tri(C)
    masks = _hgrn_level_masks(C)
    hps = G_HPS_BWD

    def body(q_ref, f_ref, i_ref, z_ref, o_ref, dob_ref, s0_ref, s1_ref, lbp_ref, go_ref, tri_ref, triu_ref,
             pm_ref, dq_ref, df_ref, di_ref, dz_ref, dlb_ref, dgo_ref, dSt, *shifted):
        @pl.when(pl.program_id(1) == 0)
        def _():
            dSt[...] = jnp.zeros_like(dSt)
            dlb_ref[...] = jnp.zeros_like(dlb_ref)
            dgo_ref[...] = jnp.zeros_like(dgo_ref)

        heads = []
        for hh in range(hps):
            ln = pl.ds(G_DK * hh, G_DK)
            heads.append(head(
                q_ref.at[:, ln], f_ref.at[:, ln], i_ref.at[:, ln], z_ref.at[:, ln], o_ref.at[:, ln],
                dob_ref.at[:, ln], s0_ref.at[0, hh], s1_ref.at[0, hh], lbp_ref.at[:, ln], go_ref, tri_ref, triu_ref,
                pm_ref, dq_ref.at[:, ln], df_ref.at[:, ln], di_ref.at[:, ln], dz_ref.at[:, ln], dlb_ref.at[:, ln],
                dgo_ref.at[pl.ds(8 * hh, 8), :], dSt.at[hh], *[t.at[hh] for t in shifted]))
        _round_robin(heads)

    def head(q_ref, f_ref, i_ref, z_ref, o_ref, dob_ref, s0_ref, s1_ref, lbp_ref, go_ref, tri_ref, triu_ref,
             pm_ref, dq_ref, df_ref, di_ref, dz_ref, dlb_ref, dgo_ref, dSt, kp, vp, fp, qn, dn_, fn, xs, dac):
        cn = nc - 1 - pl.program_id(1)
        qraw, sq, q, sg, lb, f, k, b = _hgrn_gates(q_ref, f_ref, lbp_ref, tri_ref)
        v = i_ref[...]
        bC = b[C - 1:C, :]
        eb = jnp.exp(b)
        ecb = jnp.exp(bC - b)
        o = o_ref[...]
        z = z_ref[...]
        sz = _sigmoid(z)
        go = go_ref[...]
        g_ob = dob_ref[...]
        r = lax.rsqrt(jnp.mean(o * o, axis=-1, keepdims=True) + EPS)
        nh = o * r
        dnrm = g_ob * (z * sz)
        dz_ref[...] = (g_ob * (nh * go) * (sz * (1.0 + z * (1.0 - sz)))).astype(MXU_DTYPE)
        dgo_ref[0:1, :] += jnp.sum(dnrm * nh, axis=0, keepdims=True)
        dn = dnrm * go
        do = r * (dn - nh * jnp.mean(dn * nh, axis=-1, keepdims=True))

        yield
        S0 = jnp.where(cn == 0, 0.0, s0_ref[...])
        S1 = s1_ref[...]
        dS1 = dSt[...]
        dq = eb * _mm(do, S0)
        dk = ecb * _mm(v, dS1)
        dv = _mm(k * ecb, dS1, NT)
        bnd = jnp.sum(dS1 * S1, axis=0, keepdims=True)
        dSt[...] = dS1 * jnp.exp(bC) + _mm(do, q * eb, TN)

        _fill_above(kp, k, 0.0)
        _fill_above(vp, v, 0.0)
        _fill_above(fp, f, 1.0)
        _fill_below(qn, q, 0.0)
        _fill_below(dn_, do, 0.0)
        _fill_below(fn, f, 1.0)
        yield
        for r0 in range(0, C, G_RB):
            do_b = do[r0:r0 + G_RB]
            for l in range(G_SUB):
                xs[pl.ds(l * C + r0, G_RB), :] = (do_b * vp[pl.ds(G_SUB - l + r0, G_RB), :]).astype(MXU_DTYPE)
        dac[0:G_SUB * C, :] = _mm(xs[...], jnp.ones((G_DK, G_DK), MXU_DTYPE))
        dac[G_SUB * C:, :] = jnp.zeros((G_SUB, G_DK), F32)
        yield
        near_q, near_k, near_v = [], [], []
        for r0 in range(0, C, G_RB):
            k_b = k[r0:r0 + G_RB]
            aq = ak = av = e = e2 = None
            for l in range(G_SUB):
                down, up = pl.ds(G_SUB - l + r0, G_RB), pl.ds(l + r0, G_RB)
                if l > 0:
                    fl = fp[pl.ds(G_SUB - l + 1 + r0, G_RB), :]
                    e = fl if e is None else e * fl
                    fu = fn[up, :]
                    e2 = fu if e2 is None else e2 * fu
                kl = kp[down, :]
                t = dac[pl.ds(l * C + r0, G_RB), :] * (kl if e is None else kl * e)
                aq = t if aq is None else aq + t
                qu = qn[up, :]
                qe = qu if e2 is None else qu * e2
                dou = dn_[up, :]
                a2 = jnp.sum(qe * k_b, axis=-1, keepdims=True)
                t = dac[pl.ds(l * C + l + r0, G_RB), :] * qe
                ak = t if ak is None else ak + t
                t = a2 * dou
                av = t if av is None else av + t
            near_q.append(aq)
            near_k.append(ak)
            near_v.append(av)
        dq = dq + jnp.concatenate(near_q, axis=0)
        dk = dk + jnp.concatenate(near_k, axis=0)
        dv = dv + jnp.concatenate(near_v, axis=0)

        yield
        da_all = _mm(do, v, NT)
        a_off = jnp.zeros((C, C), F32)
        for lv, m in enumerate(_hgrn_levels(C)):
            qt, kt, eq, ek = _hgrn_level(b, q, k, C, m)
            da_m = pm_ref[lv] * da_all
            if m == G_SUB:
                qs, ks, das = _split(qt), _split(kt), _split(da_m)
                a_off = a_off + pm_ref[lv] * _mm_split(qs, ks, NT)
                dq = dq + _mm_split(das, ks, NN) * eq
                dk = dk + _mm_split(das, qs, TN) * ek
            else:
                a_off = a_off + pm_ref[lv] * _mm(qt, kt, NT)
                dq = dq + _mm(da_m, kt) * eq
                dk = dk + _mm(da_m, qt, TN) * ek
        dv = dv + _mm(a_off, do, TN)

        yield
        row = lax.broadcasted_iota(jnp.int32, (C, 1), 0)
        db = q * dq - k * dk + jnp.where(row == C - 1, bnd, 0.0)
        dg = _mm_exact(triu_ref[...], db)
        df = dg / f - dk
        df_ref[...] = (df * (1.0 - lb) * (sg * (1.0 - sg))).astype(MXU_DTYPE)
        dlb_ref[0:1, :] += jnp.sum(df * (1.0 - sg), axis=0, keepdims=True)
        dq_ref[...] = (dq * (sq * (1.0 + qraw * (1.0 - sq)))).astype(MXU_DTYPE)
        di_ref[...] = dv.astype(MXU_DTYPE)

    rev = lambda n: nc - 1 - n
    w = hps * G_DK
    blk = pl.BlockSpec((C, w), lambda h, n: (nc - 1 - n, h))
    return pl.pallas_call(
        body, name="hgrn_bwd",
        grid=(G_HEADS // hps, nc),
        in_specs=[_hgrn_col(C, base, rev, hps) for base in (2, 3, 4, 5)] + [
                  blk, blk,
                  pl.BlockSpec((1, hps, G_DK, G_DK), lambda h, n: (jnp.maximum(nc - 2 - n, 0), h, 0, 0)),
                  pl.BlockSpec((1, hps, G_DK, G_DK), lambda h, n: (nc - 1 - n, h, 0, 0)),
                  pl.BlockSpec((2, w), lambda h, n: (0, h)),
                  pl.BlockSpec((1, G_DK), lambda h, n: (0, 0)),
                  pl.BlockSpec((C, C), lambda h, n: (0, 0)),
                  pl.BlockSpec((C, C), lambda h, n: (0, 0)),
                  pl.BlockSpec(masks.shape, lambda h, n: (0, 0, 0))],
        out_specs=[blk, blk, blk, blk,
                   pl.BlockSpec((8, w), lambda h, n: (0, h)),
                   pl.BlockSpec((8 * hps, G_DK), lambda h, n: (h, 0))],
        out_shape=[jax.ShapeDtypeStruct((S, G_W), MXU_DTYPE)] * 4
                  + [jax.ShapeDtypeStruct((8, G_W), F32), jax.ShapeDtypeStruct((8 * G_HEADS, G_DK), F32)],
        scratch_shapes=[pltpu.VMEM((hps, G_DK, G_DK), F32)] + [pltpu.VMEM((hps, C + G_SUB, G_DK), F32)] * 6
                       + [pltpu.VMEM((hps, G_SUB * C, G_DK), MXU_DTYPE),
                          pltpu.VMEM((hps, G_SUB * C + G_SUB, G_DK), F32)],
        compiler_params=_params(("parallel", "arbitrary")),
    )(proj, proj, proj, proj, o_raw, dob, states, states, hgrn_lb, onorm_g, tri, triu, masks)


def _tail(x, target, os, ls, ob, proj, mod3, final_g, wa, wb, wo, tm=256):
    S = x.shape[0]
    nt = S // tm

    def body(x_ref, t_ref, o1, o2, o3, l1, l2, l3, za_ref, ob_ref, ga_ref, gb_ref, mod_ref, fg_ref,
             wa_ref, wb_ref, wo_ref,
             lt_ref, dx2_ref, do_ref, dl_ref, dza_ref, dob_ref, dga_ref, dgb_ref, sums_ref,
             gwa_ref, gwb_ref, gwo_ref, acc_a, acc_b, acc_o):
        i = pl.program_id(0)

        @pl.when(i == 0)
        def _():
            sums_ref[...] = jnp.zeros_like(sums_ref)
            acc_a[...] = jnp.zeros_like(acc_a)
            acc_b[...] = jnp.zeros_like(acc_b)
            acc_o[...] = jnp.zeros_like(acc_o)

        a1, a2, a3 = l1[...], l2[...], l3[...]
        lm = jnp.maximum(jnp.maximum(a1, a2), a3)
        e1, e2, e3 = jnp.exp(a1 - lm), jnp.exp(a2 - lm), jnp.exp(a3 - lm)
        lden = e1 + e2 + e3
        ao = (e1 * o1[...] + e2 * o2[...] + e3 * o3[...]) / lden
        lt_ref[...] = lm + jnp.log(lden)
        za = za_ref[...]
        sza = _sigmoid(za)
        oa_v, ob_v = (ao * (za * sza)).astype(MXU_DTYPE), ob_ref[...]
        pa = _mm(oa_v, wa_ref[...])
        pb = _mm(ob_v, wb_ref[...])
        sa, sb = _sigmoid(ga_ref[...]), _sigmoid(gb_ref[...])
        ym = sa * pa + sb * pb
        u = _mm(ym, wo_ref[...])
        gate = mod_ref[2:3, :]
        fg = fg_ref[...]
        x2 = x_ref[...] + gate * u
        r2 = lax.rsqrt(jnp.mean(x2 * x2, axis=-1, keepdims=True) + EPS)
        xn2 = x2 * r2
        e = xn2 * fg - t_ref[...]
        dy = e * (1.0 / D)
        dn = dy * fg
        dx2 = r2 * (dn - xn2 * jnp.mean(dn * xn2, axis=-1, keepdims=True))
        dx2_ref[...] = dx2
        sums_ref[0:1, :] += jnp.sum(dy * xn2, axis=0, keepdims=True)
        sums_ref[1:2, :] += jnp.sum(dx2 * u, axis=0, keepdims=True)
        sums_ref[2:3, :] += (0.5 / D) * jnp.sum(e * e, axis=0, keepdims=True)
        du = dx2 * gate
        dym = _mm(du, wo_ref[...], NT)
        acc_o[...] += _mm(ym, du, TN)
        dpa, dpb = dym * sa, dym * sb
        dga_ref[...] = (dym * pa * (sa * (1.0 - sa))).astype(MXU_DTYPE)
        dgb_ref[...] = (dym * pb * (sb * (1.0 - sb))).astype(MXU_DTYPE)
        doa = _mm(dpa, wa_ref[...], NT)
        dza_ref[...] = (doa * ao * (sza * (1.0 + za * (1.0 - sza)))).astype(MXU_DTYPE)
        do = doa * (za * sza)
        do_ref[...] = do
        prod = do * ao
        for h in range(A_HEADS):
            sl = slice(A_HD * h, A_HD * (h + 1))
            dl_ref[:, sl] = jnp.broadcast_to(jnp.sum(prod[:, sl], axis=-1, keepdims=True), (tm, A_HD))
        dob_ref[...] = _mm(dpb, wb_ref[...], NT)
        acc_a[...] += _mm(oa_v, dpa, TN)
        acc_b[...] += _mm(ob_v, dpb, TN)

        @pl.when(i == nt - 1)
        def _():
            pltpu.sync_copy(acc_a, gwa_ref)
            pltpu.sync_copy(acc_b, gwb_ref)
            pltpu.sync_copy(acc_o, gwo_ref)

    row = lambda w: pl.BlockSpec((tm, w), lambda i: (i, 0))
    full = lambda a, b: pl.BlockSpec((a, b), lambda i: (0, 0))
    any_spec = pl.BlockSpec(memory_space=pl.ANY)
    return pl.pallas_call(
        body, name="tail",
        grid=(nt,),
        in_specs=[row(D), row(D)] + [row(A_W)] * 6 + [pl.BlockSpec((tm, A_W), lambda i: (i, 3)), row(D),
                  pl.BlockSpec((tm, D), lambda i: (i, 6)), pl.BlockSpec((tm, D), lambda i: (i, 7)),
                  full(8, D), full(1, D), full(A_W, D), full(D, D), full(D, D)],
        out_specs=[row(A_W), row(D), row(A_W), row(A_W), row(A_W), row(D), row(D), row(D), full(8, D),
                   any_spec, any_spec, any_spec],
        out_shape=[jax.ShapeDtypeStruct((S, A_W), F32),
                   jax.ShapeDtypeStruct((S, D), F32), jax.ShapeDtypeStruct((S, A_W), F32),
                   jax.ShapeDtypeStruct((S, A_W), F32), jax.ShapeDtypeStruct((S, A_W), MXU_DTYPE),
                   jax.ShapeDtypeStruct((S, D), F32), jax.ShapeDtypeStruct((S, D), MXU_DTYPE),
                   jax.ShapeDtypeStruct((S, D), MXU_DTYPE), jax.ShapeDtypeStruct((8, D), F32),
                   jax.ShapeDtypeStruct((A_W, D), F32), jax.ShapeDtypeStruct((D, D), F32),
                   jax.ShapeDtypeStruct((D, D), F32)],
        scratch_shapes=[pltpu.VMEM((A_W, D), F32), pltpu.VMEM((D, D), F32), pltpu.VMEM((D, D), F32)],
        compiler_params=_params(("arbitrary",)),
    )(x, target, *os, *ls, proj, ob, proj, proj, mod3, final_g, wa, wb, wo)


def _piece_parts(pieces):
    parts, where = [], []
    for k, piece in enumerate(pieces):
        off = 0
        for part in piece:
            parts.append(part)
            where.append((k, off, part.shape[1]))
            off += part.shape[1]
        assert off == D
    return parts, where


def _dh(pieces, w_in_g, x, dx2, mod3, norm_g, grads, tm=256):
    S = x.shape[0]
    ni = S // tm
    ng = len(grads)
    parts, where = _piece_parts(pieces)
    npart = len(parts)

    def body(*refs):
        p_refs = refs[:npart]
        w_ref, x_ref, dx2_ref, mod_ref, g_ref = refs[npart:npart + 5]
        g_ins = refs[npart + 5:npart + 5 + ng]
        gx_ref, sums_ref = refs[npart + 5 + ng:npart + 7 + ng]
        g_outs = refs[npart + 7 + ng:npart + 7 + 2 * ng]
        w_all, send_sems, recv_sems, local_sems = refs[npart + 7 + 2 * ng:]
        i = pl.program_id(0)
        start, wait = _all_to_all_copies(g_ins, g_outs, send_sems, recv_sems, local_sems)

        @pl.when(i == 0)
        def _():
            start()
            sums_ref[...] = jnp.zeros_like(sums_ref)
            pltpu.sync_copy(w_ref, w_all)

        dh = None
        for p_ref, (k, off, width) in zip(p_refs, where):
            term = _mm(p_ref[...], w_all[k, :, off:off + width], NT)
            dh = term if dh is None else dh + term
        xv = x_ref[...]
        g = g_ref[...]
        sc1 = 1.0 + mod_ref[1:2, :]
        r = lax.rsqrt(jnp.mean(xv * xv, axis=-1, keepdims=True) + EPS)
        xn = xv * r
        sums_ref[0:1, :] += jnp.sum(dh, axis=0, keepdims=True)
        sums_ref[1:2, :] += jnp.sum(dh * (xn * g), axis=0, keepdims=True)
        sums_ref[2:3, :] += jnp.sum(dh * sc1 * xn, axis=0, keepdims=True)
        dxn = dh * sc1 * g
        gx_ref[...] = dx2_ref[...] + r * (dxn - xn * jnp.mean(dxn * xn, axis=-1, keepdims=True))

        @pl.when(i == ni - 1)
        def _():
            wait()

    row = pl.BlockSpec((tm, D), lambda i: (i, 0))
    any_spec = pl.BlockSpec(memory_space=pl.ANY)
    return pl.pallas_call(
        body, name="dh_scatter",
        grid=(ni,),
        in_specs=[pl.BlockSpec((tm, width), lambda i: (i, 0)) for _, _, width in where]
                 + [any_spec, row, row,
                    pl.BlockSpec((8, D), lambda i: (0, 0)),
                    pl.BlockSpec((1, D), lambda i: (0, 0))]
                 + [any_spec] * ng,
        out_specs=[row, pl.BlockSpec((8, D), lambda i: (0, 0))] + [any_spec] * ng,
        out_shape=[jax.ShapeDtypeStruct((S, D), F32), jax.ShapeDtypeStruct((8, D), F32)]
                  + [jax.ShapeDtypeStruct(g.shape, g.dtype) for g in grads],
        scratch_shapes=[pltpu.VMEM(w_in_g.shape, w_in_g.dtype),
                        pltpu.SemaphoreType.DMA((ng, N_DEV - 1)), pltpu.SemaphoreType.DMA((ng, N_DEV - 1)),
                        pltpu.SemaphoreType.DMA((ng,))],
        compiler_params=_params(("arbitrary",)),
    )(*parts, w_in_g, x, dx2, mod3, norm_g, *grads)


def _gw_in(ht, pieces, grads, tm=1024):
    S = ht.shape[1]
    nt = S // tm
    ng = len(grads)
    parts, where = _piece_parts(pieces)
    npart = len(parts)

    def body(*refs):
        h_ref, p_refs = refs[0], refs[1:1 + npart]
        g_ins = refs[1 + npart:1 + npart + ng]
        o_ref = refs[1 + npart + ng]
        g_outs = refs[2 + npart + ng:2 + npart + 2 * ng]
        acc, send_sems, recv_sems, local_sems = refs[2 + npart + 2 * ng:]
        j, i = pl.program_id(0), pl.program_id(1)
        start, wait = _all_to_all_copies(g_ins, g_outs, send_sems, recv_sems, local_sems)

        @pl.when((j == 0) & (i == 0))
        def _():
            start()

        @pl.when(i == 0)
        def _():
            acc[...] = jnp.zeros_like(acc)

        for k in range(N_DEV):
            @pl.when(j == k)
            def _(k=k):
                for p_ref, (kk, off, width) in zip(p_refs, where):
                    if kk == k:
                        acc[:, off:off + width] += _mm(h_ref[...], p_ref[...])

        @pl.when(i == nt - 1)
        def _():
            o_ref[0] = acc[...].astype(XCHG_DTYPE)

        @pl.when((j == N_DEV - 1) & (i == nt - 1))
        def _():
            wait()

    def part_spec(k, width):
        return pl.BlockSpec((tm, width), lambda j, i: (jnp.where(j == k, i, 0), 0))

    any_spec = pl.BlockSpec(memory_space=pl.ANY)
    return pl.pallas_call(
        body, name="gw_in_scatter",
        grid=(N_DEV, nt),
        in_specs=[pl.BlockSpec((D, tm), lambda j, i: (0, i))] + [part_spec(k, width) for k, _, width in where] + [any_spec] * ng,
        out_specs=[pl.BlockSpec((1, D, D), lambda j, i: (j, 0, 0))] + [any_spec] * ng,
        out_shape=[jax.ShapeDtypeStruct((N_DEV, D, D), XCHG_DTYPE)]
                  + [jax.ShapeDtypeStruct(g.shape, g.dtype) for g in grads],
        scratch_shapes=[pltpu.VMEM((D, D), F32),
                        pltpu.SemaphoreType.DMA((ng, N_DEV - 1)), pltpu.SemaphoreType.DMA((ng, N_DEV - 1)),
                        pltpu.SemaphoreType.DMA((ng,))],
        compiler_params=_params(("arbitrary", "arbitrary")),
    )(ht, *parts, *grads)


def _adamw_math(w, g, m, v):
    m = ADAM_B1 * m + (1.0 - ADAM_B1) * g
    v = ADAM_B2 * v + (1.0 - ADAM_B2) * (g * g)
    m_hat = m / (1.0 - ADAM_B1 ** ADAM_STEP)
    v_hat = v / (1.0 - ADAM_B2 ** ADAM_STEP)
    delta = -ADAM_LR * (m_hat / (jnp.sqrt(v_hat) + ADAM_EPS) + ADAM_WD * w)
    return delta, m, v


def _adamw_big(recv, w, m, v, name, tr=128):
    M, N = w.shape
    tr = min(tr, M)

    def body(r_ref, w_ref, m_ref, v_ref, g_ref, d_ref, nm_ref, nv_ref):
        g = r_ref[0].astype(F32)
        for j in range(1, N_DEV):
            g = g + r_ref[j].astype(F32)
        g_ref[...] = g
        d_ref[...], nm_ref[...], nv_ref[...] = _adamw_math(w_ref[...], g, m_ref[...], v_ref[...])

    blk = pl.BlockSpec((tr, N), lambda i: (i, 0))
    return pl.pallas_call(
        body, name=name,
        grid=(M // tr,),
        in_specs=[pl.BlockSpec((N_DEV, tr, N), lambda i: (0, i, 0)), blk, blk, blk],
        out_specs=[blk] * 4,
        out_shape=[jax.ShapeDtypeStruct((M, N), F32)] * 4,
        compiler_params=_params(("parallel",)),
    )(recv, w, m, v)


def _adamw_w_ada(c64, dmod64, w, m, v):
    def body(c_ref, dm_ref, w_ref, m_ref, v_ref, g_ref, d_ref, nm_ref, nv_ref):
        cv = c_ref[...]
        g = _mm(cv * _sigmoid(cv), dm_ref[...], TN)
        g_ref[...] = g
        d_ref[...], nm_ref[...], nv_ref[...] = _adamw_math(w_ref[...], g, m_ref[...], v_ref[...])

    return pl.pallas_call(
        body, name="adamw_w_ada",
        out_shape=[jax.ShapeDtypeStruct(w.shape, F32)] * 4,
        compiler_params=_params(),
    )(c64, dmod64, w, m, v)


P_MOD, P_NORM, P_ONORM, P_RELB, P_LB, P_FINAL, P_LOSS, P_END = (0, 3 * D, 4 * D, 5 * D, 6 * D, 7 * D, 8 * D, 9 * D)


def _adamw_small(packed, b_ada, norm_g, onorm_g, relb, hgrn_lb, final_g, ms, vs):
    def body(pk_ref, b_ref, ng_ref, og_ref, rb_ref, lb_ref, fg_ref,
             mb, mn, mo, mr, ml, mf, vb, vn, vo, vr, vl, vf,
             loss_ref, gb, gn, go, gr, gl, gf, db, dn, do, dr, dl, df,
             nmb, nmn, nmo, nmr, nml, nmf, nvb, nvn, nvo, nvr, nvl, nvf):
        tot = pk_ref[0:1, :]
        for j in range(1, N_DEV):
            tot = tot + pk_ref[8 * j:8 * j + 1, :]
        loss_ref[...] = jnp.broadcast_to(jnp.sum(tot[:, P_LOSS:P_END], axis=-1, keepdims=True), (8, 128))

        def upd(g, w_ref, m_ref, v_ref, g_out, d_out, m_out, v_out):
            g_out[...] = g
            d_out[...], m_out[...], v_out[...] = _adamw_math(w_ref[...], g, m_ref[...], v_ref[...])

        upd(tot[:, P_MOD:P_NORM], b_ref, mb, vb, gb, db, nmb, nvb)
        upd(tot[:, P_NORM:P_ONORM], ng_ref, mn, vn, gn, dn, nmn, nvn)
        g_on = tot[:, P_ONORM:P_ONORM + G_DK]
        for h in range(1, G_HEADS):
            g_on = g_on + tot[:, P_ONORM + G_DK * h:P_ONORM + G_DK * (h + 1)]
        upd(g_on, og_ref, mo, vo, go, do, nmo, nvo)
        upd(tot[:, P_RELB:P_LB], rb_ref, mr, vr, gr, dr, nmr, nvr)
        a = lb_ref[...]
        lb = _sigmoid(a[0:1, :] - a[1:2, :])
        g0 = tot[:, P_LB:P_FINAL] * lb * (1.0 - lb)
        row = lax.broadcasted_iota(jnp.int32, (2, D), 0)
        upd(jnp.where(row == 0, g0, -g0), lb_ref, ml, vl, gl, dl, nml, nvl)
        upd(tot[:, P_FINAL:P_LOSS], fg_ref, mf, vf, gf, df, nmf, nvf)

    shapes = [b_ada.shape, norm_g.shape, onorm_g.shape, relb.shape, hgrn_lb.shape, final_g.shape]
    outs = [jax.ShapeDtypeStruct((8, 128), F32)] + [jax.ShapeDtypeStruct(s, F32) for s in shapes] * 4
    return pl.pallas_call(
        body, name="adamw_small",
        out_shape=outs,
        compiler_params=_params(),
    )(packed, b_ada, norm_g, onorm_g, relb, hgrn_lb, final_g, *ms, *vs)


def _local_step(x, target, mod3, norm_g, w_in_g, onorm_g, wa_blk, wb_blk, wo_blk, rel_bias, hgrn_lb, final_g):
    buckets = jnp.asarray(_bucket_tables())
    bias = _bias_tables(rel_bias, buckets)
    proj, ht, qkv, wa_g, wb_g, wo_g = _inproj(x, mod3, norm_g, w_in_g, [wa_blk, wb_blk, wo_blk])
    wa = wa_g.transpose(1, 0, 2).reshape(A_W, D)
    wb = wb_g.reshape(D, D)
    wo = wo_g.reshape(D, D)
    os, ls = [], []
    for p, (_, d) in enumerate(PATTERNS):
        o, l = _attn_fwd(qkv, bias[p], d, "attn_fwd_d%d" % d)
        os.append(o)
        ls.append(l)
    o_raw, ob, states = _hgrn_fwd(proj, hgrn_lb, onorm_g)
    lt, dx2, do, delta, dza, dob, dga, dgb, tsums, gwa, gwb, gwo = _tail(
        x, target, os, ls, ob, proj, mod3, final_g, wa, wb, wo)
    dbs, acc = [None] * len(PATTERNS), ()
    for p in reversed(range(len(PATTERNS))):
        d = PATTERNS[p][1]
        *acc, dbs[p] = _attn_bwd(qkv, do, lt, delta, bias[p], d, "attn_bwd_d%d" % d, prev=tuple(acc),
                                 out_dtype=MXU_DTYPE if p == 0 else F32)
    dqa, dka, dva = acc
    g_relb = _rel_bias_grad(dbs, buckets)
    dqb, dfb, dib, dzb, dlb, dgo = _hgrn_bwd(proj, o_raw, dob, states, hgrn_lb, onorm_g)
    pieces = [[dqa, dka], [dva, dza], [dqb], [dfb], [dib], [dzb], [dga], [dgb]]
    small = [gwa.astype(XCHG_DTYPE).reshape(A_W, N_DEV, D // N_DEV).transpose(1, 0, 2),
             gwb.astype(XCHG_DTYPE).reshape(N_DEV, D // N_DEV, D),
             gwo.astype(XCHG_DTYPE).reshape(N_DEV, D // N_DEV, D)]
    gw_in, *received_small = _gw_in(ht, pieces, small)
    gx, hsums, received_in = _dh(pieces, w_in_g, x, dx2, mod3, norm_g, [gw_in])
    received = [received_in] + received_small
    row = jnp.concatenate([
        hsums[0], hsums[1], tsums[1],
        hsums[2],
        dgo.reshape(G_HEADS, 8, G_DK)[:, 0].reshape(-1),
        g_relb.reshape(-1),
        dlb[0],
        tsums[0],
        tsums[2],
    ])
    return gx, received, row


def kernel(x, c, w_ada, b_ada, norm_g, w_in, hgrn_onorm_g, w_branch_a, w_branch_b, w_out, rel_bias, hgrn_lb, final_g, loss_target, m_w_ada, m_b_ada, m_norm_g, m_w_in, m_hgrn_onorm_g, m_w_branch_a, m_w_branch_b, m_w_out, m_rel_bias, m_hgrn_lb, m_final_g, v_w_ada, v_b_ada, v_norm_g, v_w_in, v_hgrn_onorm_g, v_w_branch_a, v_w_branch_b, v_w_out, v_rel_bias, v_hgrn_lb, v_final_g):
    me = 4 * lax.axis_index("x") + 2 * lax.axis_index("y") + lax.axis_index("c")
    n_ada = w_ada.shape[2]

    w_in_g, c_all = _all_gather([w_in[0].astype(MXU_DTYPE), jnp.broadcast_to(c, (8, D))], "gather_w_in_c")

    c64 = c_all.reshape(8 * N_DEV, D)
    b_loc = lax.dynamic_slice(b_ada, (0, me * n_ada), (1, n_ada))
    mod_part = _mod_fwd(c64, w_ada[0], b_loc)[::8]
    (mod_all,) = _all_gather([mod_part], "gather_mod")
    mod = lax.dynamic_slice(mod_all, (0, me, 0), (N_DEV, 1, n_ada)).reshape(3, D)
    mod3 = jnp.concatenate([mod, jnp.zeros((5, D), F32)], axis=0)

    onorm_t = hgrn_onorm_g
    gx, (r_in, r_a, r_b, r_o), row = _local_step(
        x[0], loss_target[0], mod3, norm_g, w_in_g, onorm_t, w_branch_a[0].astype(MXU_DTYPE),
        w_branch_b[0].astype(MXU_DTYPE), w_out[0].astype(MXU_DTYPE), rel_bias, hgrn_lb,
        final_g.reshape(1, D))
    packed8 = jnp.concatenate([row[None, :], jnp.zeros((7, P_END), F32)], axis=0)
    (packed,) = _all_gather([packed8], "gather_small")
    packed = packed.reshape(8 * N_DEV, P_END)

    g_in, d_in, nm_in, nv_in = _adamw_big(r_in, w_in[0], m_w_in[0], v_w_in[0], "adamw_w_in")
    g_a, d_a, nm_a, nv_a = _adamw_big(r_a, w_branch_a[0], m_w_branch_a[0], v_w_branch_a[0], "adamw_w_branch_a")
    g_b, d_b, nm_b, nv_b = _adamw_big(r_b, w_branch_b[0], m_w_branch_b[0], v_w_branch_b[0], "adamw_w_branch_b")
    g_o, d_o, nm_o, nv_o = _adamw_big(r_o, w_out[0], m_w_out[0], v_w_out[0], "adamw_w_out")

    dmod64 = lax.dynamic_slice(packed, (0, P_MOD + me * n_ada), (8 * N_DEV, n_ada))
    g_ada, d_ada, nm_ada, nv_ada = _adamw_w_ada(c64, dmod64, w_ada[0], m_w_ada[0], v_w_ada[0])

    def flat_relb(t):
        return jnp.pad(t.T, ((0, 0), (0, 128 - N_BUCKETS))).reshape(1, A_HEADS * 128)

    def unflat_relb(t):
        return t.reshape(A_HEADS, 128)[:, :N_BUCKETS].T

    fg2 = lambda t: t.reshape(1, D)
    smalls = _adamw_small(
        packed, b_ada, norm_g, hgrn_onorm_g, flat_relb(rel_bias), hgrn_lb, fg2(final_g),
        [m_b_ada, m_norm_g, m_hgrn_onorm_g, flat_relb(m_rel_bias), m_hgrn_lb, fg2(m_final_g)],
        [v_b_ada, v_norm_g, v_hgrn_onorm_g, flat_relb(v_rel_bias), v_hgrn_lb, fg2(v_final_g)])
    loss = smalls[0][0, 0]

    def small(kind):
        s = smalls[1 + 6 * kind:7 + 6 * kind]
        return s[0], s[1], s[2], unflat_relb(s[3]), s[4], s[5].reshape(D)

    def leaves(ada, sm, w_in_, wa_, wb_, wo_):
        b_, n_, o_, r_, l_, f_ = sm
        return (ada[None], b_, n_, w_in_[None], o_, wa_[None], wb_[None], wo_[None], r_, l_, f_)

    return (loss, gx[None],
            *leaves(g_ada, small(0), g_in, g_a, g_b, g_o),
            *leaves(d_ada, small(1), d_in, d_a, d_b, d_o),
            *leaves(nm_ada, small(2), nm_in, nm_a, nm_b, nm_o),
            *leaves(nv_ada, small(3), nv_in, nv_a, nv_b, nv_o))
```

```python
import functools
import math

import numpy as np
import jax
import jax.numpy as jnp
from jax import lax
from jax.experimental import pallas as pl
from jax.experimental.pallas import tpu as pltpu

F32 = jnp.float32
BF16 = jnp.bfloat16
MXU_DTYPE = jnp.bfloat16
XCHG_DTYPE = jnp.bfloat16

N_DEV = 8
D = 1024
A_HEADS = 8
A_HD = 64
A_W = A_HEADS * A_HD
A_BLK = 128
PATTERNS = ((128, 1), (512, 4), (2048, 16))
N_BUCKETS = 32
MAX_DISTANCE = 2048
NEG = -1e30
G_HEADS = 8
G_DK = 128
G_W = G_HEADS * G_DK
IN_W = 8 * D
EPS = 1e-6
ADAM_LR = 0.001
ADAM_B1 = 0.9
ADAM_B2 = 0.999
ADAM_EPS = 1e-08
ADAM_WD = 0.01
ADAM_STEP = 10

G_CHUNK = 128
G_SUB = 8
G_HPS_FWD = 8
G_HPS_BWD = 8
G_RB = 16
VMEM_LIMIT = 56 * 1024 * 1024

NN = (((1,), (0,)), ((), ()))
NT = (((1,), (1,)), ((), ()))
TN = (((0,), (0,)), ((), ()))
MESH = pl.DeviceIdType.MESH


def _mm(a, b, dims=NN):
    return lax.dot_general(a.astype(MXU_DTYPE), b.astype(MXU_DTYPE), dims,
                           preferred_element_type=F32)


def _mm_exact(t, x):
    hi = x.astype(BF16)
    r = x - hi.astype(F32)
    mid = r.astype(BF16)
    lo = (r - mid.astype(F32)).astype(BF16)
    tb = t.astype(BF16)
    return sum(lax.dot_general(tb, p, NN, preferred_element_type=F32) for p in (hi, mid, lo))


def _split(x):
    hi = x.astype(BF16)
    return hi, (x - hi.astype(F32)).astype(BF16)


def _mm_split(a, b, dims):
    dot = lambda p, q: lax.dot_general(p, q, dims, preferred_element_type=F32)
    return dot(a[0], b[0]) + dot(a[0], b[1]) + dot(a[1], b[0])


def _sigmoid(x):
    return 0.5 * jnp.tanh(0.5 * x) + 0.5


def _params(sem=None):
    return pltpu.CompilerParams(dimension_semantics=sem, vmem_limit_bytes=VMEM_LIMIT)


def _all_gather(xs, name):
    n = len(xs)

    def body(*refs):
        ins, outs = refs[:n], refs[n:2 * n]
        send_sems, recv_sems, local_sems = refs[2 * n:]
        x, y, c = lax.axis_index("x"), lax.axis_index("y"), lax.axis_index("c")
        me, sibling = (x, y, c), (x, y, 1 - c)
        chips = [(1 - x, y), (x, 1 - y), (1 - x, 1 - y)]

        def slot(ref, dev):
            return ref.at[4 * dev[0] + 2 * dev[1] + dev[2]]

        def copy(a, k, block, to, src=None):
            return pltpu.make_async_remote_copy(
                src_ref=slot(outs[a], block) if src is None else src,
                dst_ref=slot(outs[a], block),
                send_sem=send_sems.at[a, k], recv_sem=recv_sems.at[a, k],
                device_id=to, device_id_type=MESH)

        mine, first, passed = [], [], []
        for a in range(n):
            cp = pltpu.make_async_copy(ins[a], slot(outs[a], me), local_sems.at[a])
            cp.start()
            mine.append(cp)
            first.append(copy(a, 0, me, sibling, src=ins[a]))
            for j, chip in enumerate(chips):
                first.append(copy(a, 1 + j, me, (*chip, c), src=ins[a]))
        for cp in first:
            cp.start()
        for j, chip in enumerate(chips):
            for a in range(n):
                copy(a, 1 + j, (*chip, c), me).wait_recv()
                cp = copy(a, 4 + j, (*chip, c), sibling)
                cp.start()
                passed.append(cp)
        for a in range(n):
            copy(a, 0, sibling, me).wait_recv()
            for j, chip in enumerate(chips):
                copy(a, 4 + j, (*chip, 1 - c), me).wait_recv()
        for cp in first + passed:
            cp.wait_send()
        for cp in mine:
            cp.wait()

    any_spec = pl.BlockSpec(memory_space=pl.ANY)
    return pl.pallas_call(
        body, name=name,
        out_shape=[jax.ShapeDtypeStruct((N_DEV,) + v.shape, v.dtype) for v in xs],
        in_specs=[any_spec] * n, out_specs=[any_spec] * n,
        scratch_shapes=[pltpu.SemaphoreType.DMA((n, 7)), pltpu.SemaphoreType.DMA((n, 7)),
                        pltpu.SemaphoreType.DMA((n,))],
    )(*xs)


def _all_to_all_copies(ins, outs, send_sems, recv_sems, local_sems, gather=False):
    n = len(ins)
    x, y, c = lax.axis_index("x"), lax.axis_index("y"), lax.axis_index("c")
    me = 4 * x + 2 * y + c
    peers = []
    for m in range(1, N_DEV):
        peers.append((1 - x if m & 4 else x, 1 - y if m & 2 else y, 1 - c if m & 1 else c))

    def chunk(a, j):
        return ins[a] if gather else ins[a].at[j]

    def copy(a, k, landing):
        peer = peers[k]
        pid = 4 * peer[0] + 2 * peer[1] + peer[2]
        return pltpu.make_async_remote_copy(
            src_ref=chunk(a, pid), dst_ref=outs[a].at[pid if landing else me],
            send_sem=send_sems.at[a, k], recv_sem=recv_sems.at[a, k],
            device_id=peer, device_id_type=MESH)

    def local(a):
        return pltpu.make_async_copy(chunk(a, me), outs[a].at[me], local_sems.at[a])

    def start():
        for a in range(n):
            local(a).start()
        for k in range(N_DEV - 1):
            for a in range(n):
                copy(a, k, False).start()

    def wait():
        for k in range(N_DEV - 1):
            for a in range(n):
                copy(a, k, True).wait_recv()
        for k in range(N_DEV - 1):
            for a in range(n):
                copy(a, k, False).wait_send()
        for a in range(n):
            local(a).wait()

    return start, wait


def _mod_fwd(c64, w_ada, b_loc):
    def body(c_ref, w_ref, b_ref, o_ref):
        cv = c_ref[...]
        sc = cv * _sigmoid(cv)
        o_ref[...] = _mm(sc, w_ref[...]) + b_ref[...]

    return pl.pallas_call(
        body, name="mod_fwd",
        out_shape=jax.ShapeDtypeStruct((c64.shape[0], w_ada.shape[1]), F32),
        compiler_params=_params(),
    )(c64, w_ada, b_loc)


def _inproj(x, mod3, norm_g, w_in_g, blocks, tm=256):
    S = x.shape[0]
    ni = S // tm
    nb = len(blocks)

    def body(*refs):
        x_ref, mod_ref, g_ref, w_ref = refs[:4]
        b_ins = refs[4:4 + nb]
        proj_ref, ht_ref, qkv_ref = refs[4 + nb:7 + nb]
        b_outs = refs[7 + nb:7 + 2 * nb]
        w_all, send_sems, recv_sems, local_sems = refs[7 + 2 * nb:]
        i = pl.program_id(0)
        start, wait = _all_to_all_copies(b_ins, b_outs, send_sems, recv_sems, local_sems, gather=True)

        @pl.when(i == 0)
        def _():
            start()
            pltpu.sync_copy(w_ref, w_all)

        xv = x_ref[...]
        r = lax.rsqrt(jnp.mean(xv * xv, axis=-1, keepdims=True) + EPS)
        h = ((xv * r * g_ref[...]) * (1.0 + mod_ref[1:2, :]) + mod_ref[0:1, :]).astype(MXU_DTYPE)
        ht_ref[...] = h.T
        for j in range(N_DEV):
            pj = _mm(h, w_all[j])
            proj_ref[:, j * D:(j + 1) * D] = pj
            for c in range(3):
                if c // 2 == j:
                    for p in range(A_HEADS // 2):
                        lo = (c % 2) * A_W + 2 * A_HD * p
                        qkv_ref[c, p] = pj[:, lo:lo + 2 * A_HD]

        @pl.when(i == ni - 1)
        def _():
            wait()

    any_spec = pl.BlockSpec(memory_space=pl.ANY)
    return pl.pallas_call(
        body, name="inproj_gather",
        grid=(ni,),
        in_specs=[pl.BlockSpec((tm, D), lambda i: (i, 0)),
                  pl.BlockSpec((8, D), lambda i: (0, 0)),
                  pl.BlockSpec((1, D), lambda i: (0, 0)),
                  any_spec] + [any_spec] * nb,
        out_specs=[pl.BlockSpec((tm, IN_W), lambda i: (i, 0)),
                   pl.BlockSpec((D, tm), lambda i: (0, i)),
                   pl.BlockSpec((3, A_HEADS // 2, tm, 2 * A_HD), lambda i: (0, 0, i, 0))] + [any_spec] * nb,
        out_shape=[jax.ShapeDtypeStruct((S, IN_W), F32), jax.ShapeDtypeStruct((D, S), MXU_DTYPE),
                   jax.ShapeDtypeStruct((3, A_HEADS // 2, S, 2 * A_HD), F32)]
                  + [jax.ShapeDtypeStruct((N_DEV,) + b.shape, b.dtype) for b in blocks],
        scratch_shapes=[pltpu.VMEM(w_in_g.shape, w_in_g.dtype),
                        pltpu.SemaphoreType.DMA((nb, N_DEV - 1)), pltpu.SemaphoreType.DMA((nb, N_DEV - 1)),
                        pltpu.SemaphoreType.DMA((nb,))],
        compiler_params=_params(("arbitrary",)),
    )(x, mod3, norm_g, w_in_g, *blocks)


def _bucket_tables():
    qi = np.arange(A_BLK)[:, None]
    kj = np.arange(2 * A_BLK)[None, :]
    delta = qi + A_BLK - kj
    out = []
    for window, dil in PATTERNS:
        span = window // dil
        band = (delta >= 0) & (delta <= span)
        dist = np.clip(delta, 0, None) * dil
        max_exact = N_BUCKETS // 2
        nf = dist.astype(np.float32)
        large = max_exact + (np.log(np.maximum(nf, np.float32(1.0)) / np.float32(max_exact))
                             / np.float32(math.log(MAX_DISTANCE / max_exact))
                             * np.float32(N_BUCKETS - max_exact)).astype(np.int32)
        large = np.minimum(large, N_BUCKETS - 1)
        bucket = np.where(dist < max_exact, dist, large)
        out.append(np.where(band, bucket, -1).astype(np.int32))
    return np.stack(out)


def _bias_tables(rel_bias, buckets):
    def body(rb_ref, bk_ref, o_ref):
        h = pl.program_id(1)
        bk = bk_ref[0]
        acc = jnp.full(bk.shape, NEG, F32)
        for b in range(N_BUCKETS):
            acc = jnp.where(bk == b, rb_ref[b, h], acc)
        o_ref[0, 0] = acc

    return pl.pallas_call(
        body, name="bias_tables",
        grid=(3, A_HEADS),
        in_specs=[pl.BlockSpec(memory_space=pltpu.SMEM),
                  pl.BlockSpec((1, A_BLK, 2 * A_BLK), lambda p, h: (p, 0, 0))],
        out_specs=pl.BlockSpec((1, 1, A_BLK, 2 * A_BLK), lambda p, h: (p, h, 0, 0)),
        out_shape=jax.ShapeDtypeStruct((3, A_HEADS, A_BLK, 2 * A_BLK), F32),
        compiler_params=_params(("arbitrary", "arbitrary")),
    )(rel_bias, buckets)


A_TILES = 32


def _attn_heads_per_step(d):
    return A_HEADS if d == 1 else 2


def _attn_in_specs(sb, nsb, hw):
    blk = (1, hw // 2, sb, 2 * A_HD)

    def cur(c):
        return pl.BlockSpec(blk, lambda hp, n: (c, hp, jnp.minimum(n, nsb - 1), 0))

    def prev(c):
        return pl.BlockSpec(blk, lambda hp, n: (c, hp, jnp.maximum(jnp.minimum(n, nsb - 1) - 1, 0), 0))

    return [cur(0), prev(1), cur(1), prev(2), cur(2)]


def _rows(r, d):
    return pl.ds(r, A_BLK) if d == 1 else pl.ds(r, A_BLK, stride=d)


def _for_residues(d, hw, fn):
    unroll = min(d, max(1, A_TILES // hw))
    if d == unroll:
        _round_robin([g for r in range(d) for g in fn(r)])
    else:
        def group(g, c):
            _round_robin([t for u in range(unroll) for t in fn(g * unroll + u)])
            return c
        lax.fori_loop(0, d // unroll, group, 0)


def _attn_stack(t):
    first_half = lax.broadcasted_iota(jnp.int32, (1, 2 * A_HD), 1) < A_HD
    return jnp.concatenate([jnp.where(first_half, t, 0.0), jnp.where(first_half, 0.0, t)], axis=0)


def _attn_unstack(t2):
    first_half = lax.broadcasted_iota(jnp.int32, (1, 2 * A_HD), 1) < A_HD
    return jnp.where(first_half, t2[:A_BLK], t2[A_BLK:])


def _attn_scores(q, k, b_ref, pp, first):
    bias = jnp.concatenate([b_ref[2 * pp] + first, b_ref[2 * pp + 1] + first], axis=0)
    return _mm(_attn_stack(q), k, NT) * (A_HD ** -0.5) + bias


def _attn_fwd(qkv, bias_p, d, name):
    S = qkv.shape[2]
    hw = _attn_heads_per_step(d)
    sub = A_BLK * d
    nsub = max(1, A_TILES // (hw * d))
    sb = sub * nsub
    nsb = S // sb

    def body(q_ref, kp_ref, kc_ref, vp_ref, vc_ref, b_ref, o_ref, l_ref):
        n = pl.program_id(1)
        kj = lax.broadcasted_iota(jnp.int32, (A_BLK, 2 * A_BLK), 1)
        first = jnp.where((n == 0) & (kj < A_BLK), NEG, 0.0).astype(F32)
        no_first = jnp.zeros((A_BLK, 2 * A_BLK), F32)

        def residue(r, u=0):
            rows = _rows(u * sub + r, d)
            behind = _rows(((nsub if u == 0 else u) - 1) * sub + r, d)

            def pair(pp):
                lanes = pl.ds(2 * A_HD * pp, 2 * A_HD)
                kc, vc = kc_ref.at[0, pp], vc_ref.at[0, pp]
                kb, vb = (kp_ref.at[0, pp], vp_ref.at[0, pp]) if u == 0 else (kc, vc)
                k = jnp.concatenate([kb[behind, :], kc[rows, :]], axis=0)
                v = jnp.concatenate([vb[behind, :], vc[rows, :]], axis=0)
                s = _attn_scores(q_ref.at[0, pp][rows, :], k, b_ref, pp, first if u == 0 else no_first)
                yield
                m = jnp.max(s, axis=-1, keepdims=True)
                p = jnp.exp(s - m)
                den = jnp.sum(p, axis=-1, keepdims=True)
                pv = _mm(p, v)
                yield
                o_ref[rows, lanes] = _attn_unstack(pv / den)
                l_ref[rows, lanes] = _attn_unstack(jnp.broadcast_to(m + jnp.log(den), (2 * A_BLK, 2 * A_HD)))

            return [pair(pp) for pp in range(hw // 2)]

        if nsub == 1:
            _for_residues(d, hw, residue)
        else:
            _round_robin([g for u in range(nsub) for r in range(d) for g in residue(r, u)])

    out = pl.BlockSpec((sb, A_HD * hw), lambda hp, n: (n, hp))
    return pl.pallas_call(
        body, name=name,
        grid=(A_HEADS // hw, nsb),
        in_specs=_attn_in_specs(sb, nsb, hw) + [pl.BlockSpec((hw, A_BLK, 2 * A_BLK), lambda hp, n: (hp, 0, 0))],
        out_specs=[out, out],
        out_shape=[jax.ShapeDtypeStruct((S, A_W), F32)] * 2,
        compiler_params=_params(("parallel", "parallel")),
    )(qkv, qkv, qkv, qkv, qkv, bias_p)


def _attn_bwd(qkv, do, lt, delta, bias_p, d, name, prev=(), out_dtype=F32):
    S = qkv.shape[2]
    hw = _attn_heads_per_step(d)
    sub = A_BLK * d
    nsub = max(1, A_TILES // (hw * d))
    sb = sub * nsub
    nsb = S // sb
    done = (nsub - 1) * sub

    def body(*refs):
        q_ref, kp_ref, kc_ref, vp_ref, vc_ref, do_ref, lt_ref, dl_ref, b_ref = refs[:9]
        pq_ref, pk_ref, pv_ref = refs[9:9 + len(prev)] if prev else (None, None, None)
        dq_ref, dk_ref, dv_ref, db_ref, ck, cv = refs[9 + len(prev):]
        n = pl.program_id(1)
        plus = lambda t, p_ref, idx: (t if p_ref is None else t + p_ref[idx]).astype(out_dtype)

        @pl.when(n == 0)
        def _():
            db_ref[...] = jnp.zeros_like(db_ref)
            ck[...] = jnp.zeros_like(ck)
            cv[...] = jnp.zeros_like(cv)

        @pl.when(n < nsb)
        def _():
            kj = lax.broadcasted_iota(jnp.int32, (A_BLK, 2 * A_BLK), 1)
            first = jnp.where((n == 0) & (kj < A_BLK), NEG, 0.0).astype(F32)
            no_first = jnp.zeros((A_BLK, 2 * A_BLK), F32)
            if done:
                dk_ref[0:done, :] = plus(ck[0:done, :], pk_ref, (slice(0, done), slice(None)))
                dv_ref[0:done, :] = plus(cv[0:done, :], pv_ref, (slice(0, done), slice(None)))

            def residue(r, u=0):
                rows = _rows(u * sub + r, d)
                behind = _rows(((nsub if u == 0 else u) - 1) * sub + r, d)

                def pair(pp):
                    lanes = pl.ds(2 * A_HD * pp, 2 * A_HD)
                    lt_r, dl_r = lt_ref[rows, lanes], dl_ref[rows, lanes]
                    kc, vc = kc_ref.at[0, pp], vc_ref.at[0, pp]
                    kb, vb = (kp_ref.at[0, pp], vp_ref.at[0, pp]) if u == 0 else (kc, vc)
                    k = jnp.concatenate([kb[behind, :], kc[rows, :]], axis=0)
                    v = jnp.concatenate([vb[behind, :], vc[rows, :]], axis=0)
                    q2 = _attn_stack(q_ref.at[0, pp][rows, :])
                    do2 = _attn_stack(do_ref[rows, lanes])
                    col = lambda t: jnp.concatenate([t[:, 0:1], t[:, A_HD:A_HD + 1]], axis=0)
                    s = _attn_scores(q_ref.at[0, pp][rows, :], k, b_ref, pp, first if u == 0 else no_first)
                    dp = _mm(do2, v, NT)
                    yield
                    p = jnp.exp(s - col(lt_r))
                    ds = p * (dp - col(dl_r))
                    db_ref[2 * pp] += ds[:A_BLK]
                    db_ref[2 * pp + 1] += ds[A_BLK:]
                    dq = _mm(ds, k)
                    dk = _mm(ds, q2, TN) * (A_HD ** -0.5)
                    dv = _mm(p, do2, TN)
                    yield
                    dq_ref[rows, lanes] = plus(_attn_unstack(dq) * (A_HD ** -0.5), pq_ref, (rows, lanes))
                    if u == 0:
                        dk_ref[behind, lanes] = plus(ck[behind, lanes] + dk[:A_BLK], pk_ref, (behind, lanes))
                        dv_ref[behind, lanes] = plus(cv[behind, lanes] + dv[:A_BLK], pv_ref, (behind, lanes))
                    else:
                        ck[behind, lanes] += dk[:A_BLK]
                        cv[behind, lanes] += dv[:A_BLK]
                    ck[rows, lanes] = dk[A_BLK:]
                    cv[rows, lanes] = dv[A_BLK:]

                return [pair(pp) for pp in range(hw // 2)]

            if nsub == 1:
                _for_residues(d, hw, residue)
            else:
                _round_robin([g for u in range(nsub) for r in range(d) for g in residue(r, u)])

        @pl.when(n == nsb)
        def _():
            dk_ref[...] = plus(ck[...], pk_ref, ...)
            dv_ref[...] = plus(cv[...], pv_ref, ...)

    w = A_HD * hw
    row = pl.BlockSpec((sb, w), lambda hp, n: (jnp.minimum(n, nsb - 1), hp))
    lag = pl.BlockSpec((sb, w), lambda hp, n: (jnp.maximum(n - 1, 0), hp))
    tab = pl.BlockSpec((hw, A_BLK, 2 * A_BLK), lambda hp, n: (hp, 0, 0))
    return pl.pallas_call(
        body, name=name,
        grid=(A_HEADS // hw, nsb + 1),
        in_specs=_attn_in_specs(sb, nsb, hw) + [row, row, row, tab] + ([row, lag, lag] if prev else []),
        out_specs=[row, lag, lag, tab],
        out_shape=[jax.ShapeDtypeStruct((S, A_W), out_dtype)] * 3
                  + [jax.ShapeDtypeStruct((A_HEADS, A_BLK, 2 * A_BLK), F32)],
        scratch_shapes=[pltpu.VMEM((sb, w), F32), pltpu.VMEM((sb, w), F32)],
        compiler_params=_params(("parallel", "arbitrary")),
    )(qkv, qkv, qkv, qkv, qkv, do, lt, delta, bias_p, *prev)


def _rel_bias_grad(dbs, buckets):
    def body(d1, d2, d3, bk_ref, o_ref):
        row = lax.broadcasted_iota(jnp.int32, (A_HEADS, 128), 0)
        lane = lax.broadcasted_iota(jnp.int32, (A_HEADS, 128), 1)
        acc = jnp.zeros((A_HEADS, 128), F32)
        for p, dref in enumerate((d1, d2, d3)):
            bk = bk_ref[p]
            for h in range(A_HEADS):
                ds = dref[h]
                for b in range(N_BUCKETS):
                    s = jnp.sum(jnp.where(bk == b, ds, 0.0), keepdims=True)
                    acc = acc + jnp.where((row == h) & (lane == b), s, 0.0)
        o_ref[...] = acc

    return pl.pallas_call(
        body, name="rel_bias_grad",
        out_shape=jax.ShapeDtypeStruct((A_HEADS, 128), F32),
        compiler_params=_params(),
    )(*dbs, buckets)


def _tri(c):
    t = np.tril(np.ones((c, c), np.float32))
    return jnp.asarray(t), jnp.asarray(t.T.copy())


def _fill_above(ref, x, pad):
    ref[0:G_SUB, :] = jnp.full((G_SUB, x.shape[1]), pad, F32)
    ref[G_SUB:, :] = x


def _fill_below(ref, x, pad):
    ref[0:x.shape[0], :] = x
    ref[x.shape[0]:, :] = jnp.full((G_SUB, x.shape[1]), pad, F32)


def _hgrn_gates(q_ref, f_ref, lbp_ref, tri_ref):
    qraw = q_ref[...]
    sq = _sigmoid(qraw)
    q = qraw * sq
    sg = _sigmoid(f_ref[...])
    lb = _sigmoid(lbp_ref[0:1, :] - lbp_ref[1:2, :])
    f = lb + (1.0 - lb) * sg
    k = 1.0 - f
    b = _mm_exact(tri_ref[...], jnp.log(f))
    return qraw, sq, q, sg, lb, f, k, b


def _hgrn_col(C, base, idx, hps):
    return pl.BlockSpec((C, hps * G_DK), lambda h, n: (idx(n), base * (G_HEADS // hps) + h))


def _round_robin(stages):
    live = list(stages)
    while live:
        nxt = []
        for g in live:
            try:
                next(g)
                nxt.append(g)
            except StopIteration:
                pass
        live = nxt


def _hgrn_levels(C):
    out, m = [], G_SUB
    while 2 * m <= C:
        out.append(m)
        m *= 2
    return out


def _hgrn_level_masks(C):
    ti = np.arange(C)[:, None]
    si = np.arange(C)[None, :]
    return jnp.asarray(np.stack([((ti // (2 * m) == si // (2 * m)) & (ti - si >= G_SUB)).astype(np.float32)
                                 for m in _hgrn_levels(C)]))


def _hgrn_level(b, q, k, C, m):
    zeros = jnp.zeros((m, G_DK), F32)
    eq, ek, qt, kt = [], [], [], []
    for blk in range(0, C // m, 2):
        lo, mid, hi = blk * m, (blk + 1) * m, (blk + 2) * m
        ref = b[mid:mid + 1]
        e_right = jnp.exp(b[mid:hi] - ref)
        e_left = jnp.exp(ref - b[lo:mid])
        eq += [zeros, e_right]
        ek += [e_left, zeros]
        qt += [zeros, q[mid:hi] * e_right]
        kt += [k[lo:mid] * e_left, zeros]
    cat = lambda parts: jnp.concatenate(parts, axis=0)
    return cat(qt), cat(kt), cat(eq), cat(ek)


def _hgrn_fwd(proj, hgrn_lb, onorm_g, C=G_CHUNK):
    S = proj.shape[0]
    nc = S // C
    tri, _ = _tri(C)
    masks = _hgrn_level_masks(C)
    hps = G_HPS_FWD

    def body(q_ref, f_ref, i_ref, z_ref, lbp_ref, go_ref, tri_ref, pm_ref, o_ref, ob_ref, st_ref, St, kp, vp, fp):
        @pl.when(pl.program_id(1) == 0)
        def _():
            St[...] = jnp.zeros_like(St)

        heads = []
        for hh in range(hps):
            ln = pl.ds(G_DK * hh, G_DK)
            heads.append(head(
                q_ref.at[:, ln], f_ref.at[:, ln], i_ref.at[:, ln], z_ref.at[:, ln], lbp_ref.at[:, ln], go_ref,
                tri_ref, pm_ref, o_ref.at[:, ln], ob_ref.at[:, ln], st_ref.at[0, hh], St.at[hh], kp.at[hh], vp.at[hh],
                fp.at[hh]))
        _round_robin(heads)

    def head(q_ref, f_ref, i_ref, z_ref, lbp_ref, go_ref, tri_ref, pm_ref, o_ref, ob_ref, st_ref, St, kp, vp, fp):
        _, _, q, _, _, f, k, b = _hgrn_gates(q_ref, f_ref, lbp_ref, tri_ref)
        v = i_ref[...]
        bC = b[C - 1:C, :]
        S0 = St[...]
        o = _mm(q * jnp.exp(b), S0, NT)
        yield
        _fill_above(kp, k, 0.0)
        _fill_above(vp, v, 0.0)
        _fill_above(fp, f, 1.0)
        near = []
        for r0 in range(0, C, G_RB):
            qb = q[r0:r0 + G_RB]
            acc = e = None
            for l in range(G_SUB):
                rows = pl.ds(G_SUB - l + r0, G_RB)
                if l > 0:
                    fl = fp[pl.ds(G_SUB - l + 1 + r0, G_RB), :]
                    e = fl if e is None else e * fl
                kl = kp[rows, :]
                a = jnp.sum(qb * kl if e is None else qb * kl * e, axis=-1, keepdims=True)
                t = a * vp[rows, :]
                acc = t if acc is None else acc + t
            near.append(acc)
        o = o + jnp.concatenate(near, axis=0)
        yield
        a_off = jnp.zeros((C, C), F32)
        for lv, m in enumerate(_hgrn_levels(C)):
            qt, kt, _, _ = _hgrn_level(b, q, k, C, m)
            prod = _mm_split(_split(qt), _split(kt), NT) if m == G_SUB else _mm(qt, kt, NT)
            a_off = a_off + pm_ref[lv] * prod
        yield
        o = o + _mm(a_off, v)
        S1 = S0 * jnp.exp(bC) + _mm(v, k * jnp.exp(bC - b), TN)
        St[...] = S1
        st_ref[...] = S1
        o_ref[...] = o
        r = lax.rsqrt(jnp.mean(o * o, axis=-1, keepdims=True) + EPS)
        z = z_ref[...]
        ob_ref[...] = (o * r * go_ref[...] * (z * _sigmoid(z))).astype(MXU_DTYPE)

    ident = lambda n: n
    w = hps * G_DK
    out = pl.BlockSpec((C, w), lambda h, n: (n, h))
    return pl.pallas_call(
        body, name="hgrn_fwd",
        grid=(G_HEADS // hps, nc),
        in_specs=[_hgrn_col(C, base, ident, hps) for base in (2, 3, 4, 5)] + [
                  pl.BlockSpec((2, w), lambda h, n: (0, h)),
                  pl.BlockSpec((1, G_DK), lambda h, n: (0, 0)),
                  pl.BlockSpec((C, C), lambda h, n: (0, 0)),
                  pl.BlockSpec(masks.shape, lambda h, n: (0, 0, 0))],
        out_specs=[out, out, pl.BlockSpec((1, hps, G_DK, G_DK), lambda h, n: (n, h, 0, 0))],
        out_shape=[jax.ShapeDtypeStruct((S, G_W), F32), jax.ShapeDtypeStruct((S, G_W), MXU_DTYPE),
                   jax.ShapeDtypeStruct((nc, G_HEADS, G_DK, G_DK), F32)],
        scratch_shapes=[pltpu.VMEM((hps, G_DK, G_DK), F32)] + [pltpu.VMEM((hps, C + G_SUB, G_DK), F32)] * 3,
        compiler_params=_params(("parallel", "arbitrary")),
    )(proj, proj, proj, proj, hgrn_lb, onorm_g, tri, masks)


def _hgrn_bwd(proj, o_raw, dob, states, hgrn_lb, onorm_g, C=G_CHUNK):
    S = proj.shape[0]
    nc = S // C
    tri, triu = _tri(C)
    masks = _hgrn_level_masks(C)
    hps = G_HPS_BWD

    def body(q_ref, f_ref, i_ref, z_ref, o_ref, dob_ref, s0_ref, s1_ref, lbp_ref, go_ref, tri_ref, triu_ref,
             pm_ref, dq_ref, df_ref, di_ref, dz_ref, dlb_ref, dgo_ref, dSt, *shifted):
        @pl.when(pl.program_id(1) == 0)
        def _():
            dSt[...] = jnp.zeros_like(dSt)
            dlb_ref[...] = jnp.zeros_like(dlb_ref)
            dgo_ref[...] = jnp.zeros_like(dgo_ref)

        heads = []
        for hh in range(hps):
            ln = pl.ds(G_DK * hh, G_DK)
            heads.append(head(
                q_ref.at[:, ln], f_ref.at[:, ln], i_ref.at[:, ln], z_ref.at[:, ln], o_ref.at[:, ln],
                dob_ref.at[:, ln], s0_ref.at[0, hh], s1_ref.at[0, hh], lbp_ref.at[:, ln], go_ref, tri_ref, triu_ref,
                pm_ref, dq_ref.at[:, ln], df_ref.at[:, ln], di_ref.at[:, ln], dz_ref.at[:, ln], dlb_ref.at[:, ln],
                dgo_ref.at[pl.ds(8 * hh, 8), :], dSt.at[hh], *[t.at[hh] for t in shifted]))
        _round_robin(heads)

    def head(q_ref, f_ref, i_ref, z_ref, o_ref, dob_ref, s0_ref, s1_ref, lbp_ref, go_ref, tri_ref, triu_ref,
             pm_ref, dq_ref, df_ref, di_ref, dz_ref, dlb_ref, dgo_ref, dSt, kp, vp, fp, qn, dn_, fn, xs, dac):
        cn = nc - 1 - pl.program_id(1)
        qraw, sq, q, sg, lb, f, k, b = _hgrn_gates(q_ref, f_ref, lbp_ref, tri_ref)
        v = i_ref[...]
        bC = b[C - 1:C, :]
        eb = jnp.exp(b)
        ecb = jnp.exp(bC - b)
        o = o_ref[...]
        z = z_ref[...]
        sz = _sigmoid(z)
        go = go_ref[...]
        g_ob = dob_ref[...]
        r = lax.rsqrt(jnp.mean(o * o, axis=-1, keepdims=True) + EPS)
        nh = o * r
        dnrm = g_ob * (z * sz)
        dz_ref[...] = (g_ob * (nh * go) * (sz * (1.0 + z * (1.0 - sz)))).astype(MXU_DTYPE)
        dgo_ref[0:1, :] += jnp.sum(dnrm * nh, axis=0, keepdims=True)
        dn = dnrm * go
        do = r * (dn - nh * jnp.mean(dn * nh, axis=-1, keepdims=True))

        yield
        S0 = jnp.where(cn == 0, 0.0, s0_ref[...])
        S1 = s1_ref[...]
        dS1 = dSt[...]
        dq = eb * _mm(do, S0)
        dk = ecb * _mm(v, dS1)
        dv = _mm(k * ecb, dS1, NT)
        bnd = jnp.sum(dS1 * S1, axis=0, keepdims=True)
        dSt[...] = dS1 * jnp.exp(bC) + _mm(do, q * eb, TN)

        _fill_above(kp, k, 0.0)
        _fill_above(vp, v, 0.0)
        _fill_above(fp, f, 1.0)
        _fill_below(qn, q, 0.0)
        _fill_below(dn_, do, 0.0)
        _fill_below(fn, f, 1.0)
        yield
        for r0 in range(0, C, G_RB):
            do_b = do[r0:r0 + G_RB]
            for l in range(G_SUB):
                xs[pl.ds(l * C + r0, G_RB), :] = (do_b * vp[pl.ds(G_SUB - l + r0, G_RB), :]).astype(MXU_DTYPE)
        dac[0:G_SUB * C, :] = _mm(xs[...], jnp.ones((G_DK, G_DK), MXU_DTYPE))
        dac[G_SUB * C:, :] = jnp.zeros((G_SUB, G_DK), F32)
        yield
        near_q, near_k, near_v = [], [], []
        for r0 in range(0, C, G_RB):
            k_b = k[r0:r0 + G_RB]
            aq = ak = av = e = e2 = None
            for l in range(G_SUB):
                down, up = pl.ds(G_SUB - l + r0, G_RB), pl.ds(l + r0, G_RB)
                if l > 0:
                    fl = fp[pl.ds(G_SUB - l + 1 + r0, G_RB), :]
                    e = fl if e is None else e * fl
                    fu = fn[up, :]
                    e2 = fu if e2 is None else e2 * fu
                kl = kp[down, :]
                t = dac[pl.ds(l * C + r0, G_RB), :] * (kl if e is None else kl * e)
                aq = t if aq is None else aq + t
                qu = qn[up, :]
                qe = qu if e2 is None else qu * e2
                dou = dn_[up, :]
                a2 = jnp.sum(qe * k_b, axis=-1, keepdims=True)
                t = dac[pl.ds(l * C + l + r0, G_RB), :] * qe
                ak = t if ak is None else ak + t
                t = a2 * dou
                av = t if av is None else av + t
            near_q.append(aq)
            near_k.append(ak)
            near_v.append(av)
        dq = dq + jnp.concatenate(near_q, axis=0)
        dk = dk + jnp.concatenate(near_k, axis=0)
        dv = dv + jnp.concatenate(near_v, axis=0)

        yield
        da_all = _mm(do, v, NT)
        a_off = jnp.zeros((C, C), F32)
        for lv, m in enumerate(_hgrn_levels(C)):
            qt, kt, eq, ek = _hgrn_level(b, q, k, C, m)
            da_m = pm_ref[lv] * da_all
            if m == G_SUB:
                qs, ks, das = _split(qt), _split(kt), _split(da_m)
                a_off = a_off + pm_ref[lv] * _mm_split(qs, ks, NT)
                dq = dq + _mm_split(das, ks, NN) * eq
                dk = dk + _mm_split(das, qs, TN) * ek
            else:
                a_off = a_off + pm_ref[lv] * _mm(qt, kt, NT)
                dq = dq + _mm(da_m, kt) * eq
                dk = dk + _mm(da_m, qt, TN) * ek
        dv = dv + _mm(a_off, do, TN)

        yield
        row = lax.broadcasted_iota(jnp.int32, (C, 1), 0)
        db = q * dq - k * dk + jnp.where(row == C - 1, bnd, 0.0)
        dg = _mm_exact(triu_ref[...], db)
        df = dg / f - dk
        df_ref[...] = (df * (1.0 - lb) * (sg * (1.0 - sg))).astype(MXU_DTYPE)
        dlb_ref[0:1, :] += jnp.sum(df * (1.0 - sg), axis=0, keepdims=True)
        dq_ref[...] = (dq * (sq * (1.0 + qraw * (1.0 - sq)))).astype(MXU_DTYPE)
        di_ref[...] = dv.astype(MXU_DTYPE)

    rev = lambda n: nc - 1 - n
    w = hps * G_DK
    blk = pl.BlockSpec((C, w), lambda h, n: (nc - 1 - n, h))
    return pl.pallas_call(
        body, name="hgrn_bwd",
        grid=(G_HEADS // hps, nc),
        in_specs=[_hgrn_col(C, base, rev, hps) for base in (2, 3, 4, 5)] + [
                  blk, blk,
                  pl.BlockSpec((1, hps, G_DK, G_DK), lambda h, n: (jnp.maximum(nc - 2 - n, 0), h, 0, 0)),
                  pl.BlockSpec((1, hps, G_DK, G_DK), lambda h, n: (nc - 1 - n, h, 0, 0)),
                  pl.BlockSpec((2, w), lambda h, n: (0, h)),
                  pl.BlockSpec((1, G_DK), lambda h, n: (0, 0)),
                  pl.BlockSpec((C, C), lambda h, n: (0, 0)),
                  pl.BlockSpec((C, C), lambda h, n: (0, 0)),
                  pl.BlockSpec(masks.shape, lambda h, n: (0, 0, 0))],
        out_specs=[blk, blk, blk, blk,
                   pl.BlockSpec((8, w), lambda h, n: (0, h)),
                   pl.BlockSpec((8 * hps, G_DK), lambda h, n: (h, 0))],
        out_shape=[jax.ShapeDtypeStruct((S, G_W), MXU_DTYPE)] * 4
                  + [jax.ShapeDtypeStruct((8, G_W), F32), jax.ShapeDtypeStruct((8 * G_HEADS, G_DK), F32)],
        scratch_shapes=[pltpu.VMEM((hps, G_DK, G_DK), F32)] + [pltpu.VMEM((hps, C + G_SUB, G_DK), F32)] * 6
                       + [pltpu.VMEM((hps, G_SUB * C, G_DK), MXU_DTYPE),
                          pltpu.VMEM((hps, G_SUB * C + G_SUB, G_DK), F32)],
        compiler_params=_params(("parallel", "arbitrary")),
    )(proj, proj, proj, proj, o_raw, dob, states, states, hgrn_lb, onorm_g, tri, triu, masks)


def _tail(x, target, os, ls, ob, proj, mod3, final_g, wa, wb, wo, tm=256):
    S = x.shape[0]
    nt = S // tm

    def body(x_ref, t_ref, o1, o2, o3, l1, l2, l3, za_ref, ob_ref, ga_ref, gb_ref, mod_ref, fg_ref,
             wa_ref, wb_ref, wo_ref,
             lt_ref, dx2_ref, do_ref, dl_ref, dza_ref, dob_ref, dga_ref, dgb_ref, sums_ref,
             gwa_ref, gwb_ref, gwo_ref, acc_a, acc_b, acc_o):
        i = pl.program_id(0)

        @pl.when(i == 0)
        def _():
            sums_ref[...] = jnp.zeros_like(sums_ref)
            acc_a[...] = jnp.zeros_like(acc_a)
            acc_b[...] = jnp.zeros_like(acc_b)
            acc_o[...] = jnp.zeros_like(acc_o)

        a1, a2, a3 = l1[...], l2[...], l3[...]
        lm = jnp.maximum(jnp.maximum(a1, a2), a3)
        e1, e2, e3 = jnp.exp(a1 - lm), jnp.exp(a2 - lm), jnp.exp(a3 - lm)
        lden = e1 + e2 + e3
        ao = (e1 * o1[...] + e2 * o2[...] + e3 * o3[...]) / lden
        lt_ref[...] = lm + jnp.log(lden)
        za = za_ref[...]
        sza = _sigmoid(za)
        oa_v, ob_v = (ao * (za * sza)).astype(MXU_DTYPE), ob_ref[...]
        pa = _mm(oa_v, wa_ref[...])
        pb = _mm(ob_v, wb_ref[...])
        sa, sb = _sigmoid(ga_ref[...]), _sigmoid(gb_ref[...])
        ym = sa * pa + sb * pb
        u = _mm(ym, wo_ref[...])
        gate = mod_ref[2:3, :]
        fg = fg_ref[...]
        x2 = x_ref[...] + gate * u
        r2 = lax.rsqrt(jnp.mean(x2 * x2, axis=-1, keepdims=True) + EPS)
        xn2 = x2 * r2
        e = xn2 * fg - t_ref[...]
        dy = e * (1.0 / D)
        dn = dy * fg
        dx2 = r2 * (dn - xn2 * jnp.mean(dn * xn2, axis=-1, keepdims=True))
        dx2_ref[...] = dx2
        sums_ref[0:1, :] += jnp.sum(dy * xn2, axis=0, keepdims=True)
        sums_ref[1:2, :] += jnp.sum(dx2 * u, axis=0, keepdims=True)
        sums_ref[2:3, :] += (0.5 / D) * jnp.sum(e * e, axis=0, keepdims=True)
        du = dx2 * gate
        dym = _mm(du, wo_ref[...], NT)
        acc_o[...] += _mm(ym, du, TN)
        dpa, dpb = dym * sa, dym * sb
        dga_ref[...] = (dym * pa * (sa * (1.0 - sa))).astype(MXU_DTYPE)
        dgb_ref[...] = (dym * pb * (sb * (1.0 - sb))).astype(MXU_DTYPE)
        doa = _mm(dpa, wa_ref[...], NT)
        dza_ref[...] = (doa * ao * (sza * (1.0 + za * (1.0 - sza)))).astype(MXU_DTYPE)
        do = doa * (za * sza)
        do_ref[...] = do
        prod = do * ao
        for h in range(A_HEADS):
            sl = slice(A_HD * h, A_HD * (h + 1))
            dl_ref[:, sl] = jnp.broadcast_to(jnp.sum(prod[:, sl], axis=-1, keepdims=True), (tm, A_HD))
        dob_ref[...] = _mm(dpb, wb_ref[...], NT)
        acc_a[...] += _mm(oa_v, dpa, TN)
        acc_b[...] += _mm(ob_v, dpb, TN)

        @pl.when(i == nt - 1)
        def _():
            pltpu.sync_copy(acc_a, gwa_ref)
            pltpu.sync_copy(acc_b, gwb_ref)
            pltpu.sync_copy(acc_o, gwo_ref)

    row = lambda w: pl.BlockSpec((tm, w), lambda i: (i, 0))
    full = lambda a, b: pl.BlockSpec((a, b), lambda i: (0, 0))
    any_spec = pl.BlockSpec(memory_space=pl.ANY)
    return pl.pallas_call(
        body, name="tail",
        grid=(nt,),
        in_specs=[row(D), row(D)] + [row(A_W)] * 6 + [pl.BlockSpec((tm, A_W), lambda i: (i, 3)), row(D),
                  pl.BlockSpec((tm, D), lambda i: (i, 6)), pl.BlockSpec((tm, D), lambda i: (i, 7)),
                  full(8, D), full(1, D), full(A_W, D), full(D, D), full(D, D)],
        out_specs=[row(A_W), row(D), row(A_W), row(A_W), row(A_W), row(D), row(D), row(D), full(8, D),
                   any_spec, any_spec, any_spec],
        out_shape=[jax.ShapeDtypeStruct((S, A_W), F32),
                   jax.ShapeDtypeStruct((S, D), F32), jax.ShapeDtypeStruct((S, A_W), F32),
                   jax.ShapeDtypeStruct((S, A_W), F32), jax.ShapeDtypeStruct((S, A_W), MXU_DTYPE),
                   jax.ShapeDtypeStruct((S, D), F32), jax.ShapeDtypeStruct((S, D), MXU_DTYPE),
                   jax.ShapeDtypeStruct((S, D), MXU_DTYPE), jax.ShapeDtypeStruct((8, D), F32),
                   jax.ShapeDtypeStruct((A_W, D), F32), jax.ShapeDtypeStruct((D, D), F32),
                   jax.ShapeDtypeStruct((D, D), F32)],
        scratch_shapes=[pltpu.VMEM((A_W, D), F32), pltpu.VMEM((D, D), F32), pltpu.VMEM((D, D), F32)],
        compiler_params=_params(("arbitrary",)),
    )(x, target, *os, *ls, proj, ob, proj, proj, mod3, final_g, wa, wb, wo)


def _piece_parts(pieces):
    parts, where = [], []
    for k, piece in enumerate(pieces):
        off = 0
        for part in piece:
            parts.append(part)
            where.append((k, off, part.shape[1]))
            off += part.shape[1]
        assert off == D
    return parts, where


def _dh(pieces, w_in_g, x, dx2, mod3, norm_g, grads, tm=256):
    S = x.shape[0]
    ni = S // tm
    ng = len(grads)
    parts, where = _piece_parts(pieces)
    npart = len(parts)

    def body(*refs):
        p_refs = refs[:npart]
        w_ref, x_ref, dx2_ref, mod_ref, g_ref = refs[npart:npart + 5]
        g_ins = refs[npart + 5:npart + 5 + ng]
        gx_ref, sums_ref = refs[npart + 5 + ng:npart + 7 + ng]
        g_outs = refs[npart + 7 + ng:npart + 7 + 2 * ng]
        w_all, send_sems, recv_sems, local_sems = refs[npart + 7 + 2 * ng:]
        i = pl.program_id(0)
        start, wait = _all_to_all_copies(g_ins, g_outs, send_sems, recv_sems, local_sems)

        @pl.when(i == 0)
        def _():
            start()
            sums_ref[...] = jnp.zeros_like(sums_ref)
            pltpu.sync_copy(w_ref, w_all)

        dh = None
        for p_ref, (k, off, width) in zip(p_refs, where):
            term = _mm(p_ref[...], w_all[k, :, off:off + width], NT)
            dh = term if dh is None else dh + term
        xv = x_ref[...]
        g = g_ref[...]
        sc1 = 1.0 + mod_ref[1:2, :]
        r = lax.rsqrt(jnp.mean(xv * xv, axis=-1, keepdims=True) + EPS)
        xn = xv * r
        sums_ref[0:1, :] += jnp.sum(dh, axis=0, keepdims=True)
        sums_ref[1:2, :] += jnp.sum(dh * (xn * g), axis=0, keepdims=True)
        sums_ref[2:3, :] += jnp.sum(dh * sc1 * xn, axis=0, keepdims=True)
        dxn = dh * sc1 * g
        gx_ref[...] = dx2_ref[...] + r * (dxn - xn * jnp.mean(dxn * xn, axis=-1, keepdims=True))

        @pl.when(i == ni - 1)
        def _():
            wait()

    row = pl.BlockSpec((tm, D), lambda i: (i, 0))
    any_spec = pl.BlockSpec(memory_space=pl.ANY)
    return pl.pallas_call(
        body, name="dh_scatter",
        grid=(ni,),
        in_specs=[pl.BlockSpec((tm, width), lambda i: (i, 0)) for _, _, width in where]
                 + [any_spec, row, row,
                    pl.BlockSpec((8, D), lambda i: (0, 0)),
                    pl.BlockSpec((1, D), lambda i: (0, 0))]
                 + [any_spec] * ng,
        out_specs=[row, pl.BlockSpec((8, D), lambda i: (0, 0))] + [any_spec] * ng,
        out_shape=[jax.ShapeDtypeStruct((S, D), F32), jax.ShapeDtypeStruct((8, D), F32)]
                  + [jax.ShapeDtypeStruct(g.shape, g.dtype) for g in grads],
        scratch_shapes=[pltpu.VMEM(w_in_g.shape, w_in_g.dtype),
                        pltpu.SemaphoreType.DMA((ng, N_DEV - 1)), pltpu.SemaphoreType.DMA((ng, N_DEV - 1)),
                        pltpu.SemaphoreType.DMA((ng,))],
        compiler_params=_params(("arbitrary",)),
    )(*parts, w_in_g, x, dx2, mod3, norm_g, *grads)


def _gw_in(ht, pieces, grads, tm=1024):
    S = ht.shape[1]
    nt = S // tm
    ng = len(grads)
    parts, where = _piece_parts(pieces)
    npart = len(parts)

    def body(*refs):
        h_ref, p_refs = refs[0], refs[1:1 + npart]
        g_ins = refs[1 + npart:1 + npart + ng]
        o_ref = refs[1 + npart + ng]
        g_outs = refs[2 + npart + ng:2 + npart + 2 * ng]
        acc, send_sems, recv_sems, local_sems = refs[2 + npart + 2 * ng:]
        j, i = pl.program_id(0), pl.program_id(1)
        start, wait = _all_to_all_copies(g_ins, g_outs, send_sems, recv_sems, local_sems)

        @pl.when((j == 0) & (i == 0))
        def _():
            start()

        @pl.when(i == 0)
        def _():
            acc[...] = jnp.zeros_like(acc)

        for k in range(N_DEV):
            @pl.when(j == k)
            def _(k=k):
                for p_ref, (kk, off, width) in zip(p_refs, where):
                    if kk == k:
                        acc[:, off:off + width] += _mm(h_ref[...], p_ref[...])

        @pl.when(i == nt - 1)
        def _():
            o_ref[0] = acc[...].astype(XCHG_DTYPE)

        @pl.when((j == N_DEV - 1) & (i == nt - 1))
        def _():
            wait()

    def part_spec(k, width):
        return pl.BlockSpec((tm, width), lambda j, i: (jnp.where(j == k, i, 0), 0))

    any_spec = pl.BlockSpec(memory_space=pl.ANY)
    return pl.pallas_call(
        body, name="gw_in_scatter",
        grid=(N_DEV, nt),
        in_specs=[pl.BlockSpec((D, tm), lambda j, i: (0, i))] + [part_spec(k, width) for k, _, width in where] + [any_spec] * ng,
        out_specs=[pl.BlockSpec((1, D, D), lambda j, i: (j, 0, 0))] + [any_spec] * ng,
        out_shape=[jax.ShapeDtypeStruct((N_DEV, D, D), XCHG_DTYPE)]
                  + [jax.ShapeDtypeStruct(g.shape, g.dtype) for g in grads],
        scratch_shapes=[pltpu.VMEM((D, D), F32),
                        pltpu.SemaphoreType.DMA((ng, N_DEV - 1)), pltpu.SemaphoreType.DMA((ng, N_DEV - 1)),
                        pltpu.SemaphoreType.DMA((ng,))],
        compiler_params=_params(("arbitrary", "arbitrary")),
    )(ht, *parts, *grads)


def _adamw_math(w, g, m, v):
    m = ADAM_B1 * m + (1.0 - ADAM_B1) * g
    v = ADAM_B2 * v + (1.0 - ADAM_B2) * (g * g)
    m_hat = m / (1.0 - ADAM_B1 ** ADAM_STEP)
    v_hat = v / (1.0 - ADAM_B2 ** ADAM_STEP)
    delta = -ADAM_LR * (m_hat / (jnp.sqrt(v_hat) + ADAM_EPS) + ADAM_WD * w)
    return delta, m, v


def _adamw_big(recv, w, m, v, name, tr=128):
    M, N = w.shape
    tr = min(tr, M)

    def body(r_ref, w_ref, m_ref, v_ref, g_ref, d_ref, nm_ref, nv_ref):
        g = r_ref[0].astype(F32)
        for j in range(1, N_DEV):
            g = g + r_ref[j].astype(F32)
        g_ref[...] = g
        d_ref[...], nm_ref[...], nv_ref[...] = _adamw_math(w_ref[...], g, m_ref[...], v_ref[...])

    blk = pl.BlockSpec((tr, N), lambda i: (i, 0))
    return pl.pallas_call(
        body, name=name,
        grid=(M // tr,),
        in_specs=[pl.BlockSpec((N_DEV, tr, N), lambda i: (0, i, 0)), blk, blk, blk],
        out_specs=[blk] * 4,
        out_shape=[jax.ShapeDtypeStruct((M, N), F32)] * 4,
        compiler_params=_params(("parallel",)),
    )(recv, w, m, v)


def _adamw_w_ada(c64, dmod64, w, m, v):
    def body(c_ref, dm_ref, w_ref, m_ref, v_ref, g_ref, d_ref, nm_ref, nv_ref):
        cv = c_ref[...]
        g = _mm(cv * _sigmoid(cv), dm_ref[...], TN)
        g_ref[...] = g
        d_ref[...], nm_ref[...], nv_ref[...] = _adamw_math(w_ref[...], g, m_ref[...], v_ref[...])

    return pl.pallas_call(
        body, name="adamw_w_ada",
        out_shape=[jax.ShapeDtypeStruct(w.shape, F32)] * 4,
        compiler_params=_params(),
    )(c64, dmod64, w, m, v)


P_MOD, P_NORM, P_ONORM, P_RELB, P_LB, P_FINAL, P_LOSS, P_END = (0, 3 * D, 4 * D, 5 * D, 6 * D, 7 * D, 8 * D, 9 * D)


def _adamw_small(packed, b_ada, norm_g, onorm_g, relb, hgrn_lb, final_g, ms, vs):
    def body(pk_ref, b_ref, ng_ref, og_ref, rb_ref, lb_ref, fg_ref,
             mb, mn, mo, mr, ml, mf, vb, vn, vo, vr, vl, vf,
             loss_ref, gb, gn, go, gr, gl, gf, db, dn, do, dr, dl, df,
             nmb, nmn, nmo, nmr, nml, nmf, nvb, nvn, nvo, nvr, nvl, nvf):
        tot = pk_ref[0:1, :]
        for j in range(1, N_DEV):
            tot = tot + pk_ref[8 * j:8 * j + 1, :]
        loss_ref[...] = jnp.broadcast_to(jnp.sum(tot[:, P_LOSS:P_END], axis=-1, keepdims=True), (8, 128))

        def upd(g, w_ref, m_ref, v_ref, g_out, d_out, m_out, v_out):
            g_out[...] = g
            d_out[...], m_out[...], v_out[...] = _adamw_math(w_ref[...], g, m_ref[...], v_ref[...])

        upd(tot[:, P_MOD:P_NORM], b_ref, mb, vb, gb, db, nmb, nvb)
        upd(tot[:, P_NORM:P_ONORM], ng_ref, mn, vn, gn, dn, nmn, nvn)
        g_on = tot[:, P_ONORM:P_ONORM + G_DK]
        for h in range(1, G_HEADS):
            g_on = g_on + tot[:, P_ONORM + G_DK * h:P_ONORM + G_DK * (h + 1)]
        upd(g_on, og_ref, mo, vo, go, do, nmo, nvo)
        upd(tot[:, P_RELB:P_LB], rb_ref, mr, vr, gr, dr, nmr, nvr)
        a = lb_ref[...]
        lb = _sigmoid(a[0:1, :] - a[1:2, :])
        g0 = tot[:, P_LB:P_FINAL] * lb * (1.0 - lb)
        row = lax.broadcasted_iota(jnp.int32, (2, D), 0)
        upd(jnp.where(row == 0, g0, -g0), lb_ref, ml, vl, gl, dl, nml, nvl)
        upd(tot[:, P_FINAL:P_LOSS], fg_ref, mf, vf, gf, df, nmf, nvf)

    shapes = [b_ada.shape, norm_g.shape, onorm_g.shape, relb.shape, hgrn_lb.shape, final_g.shape]
    outs = [jax.ShapeDtypeStruct((8, 128), F32)] + [jax.ShapeDtypeStruct(s, F32) for s in shapes] * 4
    return pl.pallas_call(
        body, name="adamw_small",
        out_shape=outs,
        compiler_params=_params(),
    )(packed, b_ada, norm_g, onorm_g, relb, hgrn_lb, final_g, *ms, *vs)


def _local_step(x, target, mod3, norm_g, w_in_g, onorm_g, wa_blk, wb_blk, wo_blk, rel_bias, hgrn_lb, final_g):
    buckets = jnp.asarray(_bucket_tables())
    bias = _bias_tables(rel_bias, buckets)
    proj, ht, qkv, wa_g, wb_g, wo_g = _inproj(x, mod3, norm_g, w_in_g, [wa_blk, wb_blk, wo_blk])
    wa = wa_g.transpose(1, 0, 2).reshape(A_W, D)
    wb = wb_g.reshape(D, D)
    wo = wo_g.reshape(D, D)
    os, ls = [], []
    for p, (_, d) in enumerate(PATTERNS):
        o, l = _attn_fwd(qkv, bias[p], d, "attn_fwd_d%d" % d)
        os.append(o)
        ls.append(l)
    o_raw, ob, states = _hgrn_fwd(proj, hgrn_lb, onorm_g)
    lt, dx2, do, delta, dza, dob, dga, dgb, tsums, gwa, gwb, gwo = _tail(
        x, target, os, ls, ob, proj, mod3, final_g, wa, wb, wo)
    dbs, acc = [None] * len(PATTERNS), ()
    for p in reversed(range(len(PATTERNS))):
        d = PATTERNS[p][1]
        *acc, dbs[p] = _attn_bwd(qkv, do, lt, delta, bias[p], d, "attn_bwd_d%d" % d, prev=tuple(acc),
                                 out_dtype=MXU_DTYPE if p == 0 else F32)
    dqa, dka, dva = acc
    g_relb = _rel_bias_grad(dbs, buckets)
    dqb, dfb, dib, dzb, dlb, dgo = _hgrn_bwd(proj, o_raw, dob, states, hgrn_lb, onorm_g)
    pieces = [[dqa, dka], [dva, dza], [dqb], [dfb], [dib], [dzb], [dga], [dgb]]
    small = [gwa.astype(XCHG_DTYPE).reshape(A_W, N_DEV, D // N_DEV).transpose(1, 0, 2),
             gwb.astype(XCHG_DTYPE).reshape(N_DEV, D // N_DEV, D),
             gwo.astype(XCHG_DTYPE).reshape(N_DEV, D // N_DEV, D)]
    gw_in, *received_small = _gw_in(ht, pieces, small)
    gx, hsums, received_in = _dh(pieces, w_in_g, x, dx2, mod3, norm_g, [gw_in])
    received = [received_in] + received_small
    row = jnp.concatenate([
        hsums[0], hsums[1], tsums[1],
        hsums[2],
        dgo.reshape(G_HEADS, 8, G_DK)[:, 0].reshape(-1),
        g_relb.reshape(-1),
        dlb[0],
        tsums[0],
        tsums[2],
    ])
    return gx, received, row


def kernel(x, c, w_ada, b_ada, norm_g, w_in, hgrn_onorm_g, w_branch_a, w_branch_b, w_out, rel_bias, hgrn_lb, final_g, loss_target, m_w_ada, m_b_ada, m_norm_g, m_w_in, m_hgrn_onorm_g, m_w_branch_a, m_w_branch_b, m_w_out, m_rel_bias, m_hgrn_lb, m_final_g, v_w_ada, v_b_ada, v_norm_g, v_w_in, v_hgrn_onorm_g, v_w_branch_a, v_w_branch_b, v_w_out, v_rel_bias, v_hgrn_lb, v_final_g):
    me = 4 * lax.axis_index("x") + 2 * lax.axis_index("y") + lax.axis_index("c")
    n_ada = w_ada.shape[2]

    w_in_g, c_all = _all_gather([w_in[0].astype(MXU_DTYPE), jnp.broadcast_to(c, (8, D))], "gather_w_in_c")

    c64 = c_all.reshape(8 * N_DEV, D)
    b_loc = lax.dynamic_slice(b_ada, (0, me * n_ada), (1, n_ada))
    mod_part = _mod_fwd(c64, w_ada[0], b_loc)[::8]
    (mod_all,) = _all_gather([mod_part], "gather_mod")
    mod = lax.dynamic_slice(mod_all, (0, me, 0), (N_DEV, 1, n_ada)).reshape(3, D)
    mod3 = jnp.concatenate([mod, jnp.zeros((5, D), F32)], axis=0)

    onorm_t = hgrn_onorm_g
    gx, (r_in, r_a, r_b, r_o), row = _local_step(
        x[0], loss_target[0], mod3, norm_g, w_in_g, onorm_t, w_branch_a[0].astype(MXU_DTYPE),
        w_branch_b[0].astype(MXU_DTYPE), w_out[0].astype(MXU_DTYPE), rel_bias, hgrn_lb,
        final_g.reshape(1, D))
    packed8 = jnp.concatenate([row[None, :], jnp.zeros((7, P_END), F32)], axis=0)
    (packed,) = _all_gather([packed8], "gather_small")
    packed = packed.reshape(8 * N_DEV, P_END)

    g_in, d_in, nm_in, nv_in = _adamw_big(r_in, w_in[0], m_w_in[0], v_w_in[0], "adamw_w_in")
    g_a, d_a, nm_a, nv_a = _adamw_big(r_a, w_branch_a[0], m_w_branch_a[0], v_w_branch_a[0], "adamw_w_branch_a")
    g_b, d_b, nm_b, nv_b = _adamw_big(r_b, w_branch_b[0], m_w_branch_b[0], v_w_branch_b[0], "adamw_w_branch_b")
    g_o, d_o, nm_o, nv_o = _adamw_big(r_o, w_out[0], m_w_out[0], v_w_out[0], "adamw_w_out")

    dmod64 = lax.dynamic_slice(packed, (0, P_MOD + me * n_ada), (8 * N_DEV, n_ada))
    g_ada, d_ada, nm_ada, nv_ada = _adamw_w_ada(c64, dmod64, w_ada[0], m_w_ada[0], v_w_ada[0])

    def flat_relb(t):
        return jnp.pad(t.T, ((0, 0), (0, 128 - N_BUCKETS))).reshape(1, A_HEADS * 128)

    def unflat_relb(t):
        return t.reshape(A_HEADS, 128)[:, :N_BUCKETS].T

    fg2 = lambda t: t.reshape(1, D)
    smalls = _adamw_small(
        packed, b_ada, norm_g, hgrn_onorm_g, flat_relb(rel_bias), hgrn_lb, fg2(final_g),
        [m_b_ada, m_norm_g, m_hgrn_onorm_g, flat_relb(m_rel_bias), m_hgrn_lb, fg2(m_final_g)],
        [v_b_ada, v_norm_g, v_hgrn_onorm_g, flat_relb(v_rel_bias), v_hgrn_lb, fg2(v_final_g)])
    loss = smalls[0][0, 0]

    def small(kind):
        s = smalls[1 + 6 * kind:7 + 6 * kind]
        return s[0], s[1], s[2], unflat_relb(s[3]), s[4], s[5].reshape(D)

    def leaves(ada, sm, w_in_, wa_, wb_, wo_):
        b_, n_, o_, r_, l_, f_ = sm
        return (ada[None], b_, n_, w_in_[None], o_, wa_[None], wb_[None], wo_[None], r_, l_, f_)

    return (loss, gx[None],
            *leaves(g_ada, small(0), g_in, g_a, g_b, g_o),
            *leaves(d_ada, small(1), d_in, d_a, d_b, d_o),
            *leaves(nm_ada, small(2), nm_in, nm_a, nm_b, nm_o),
            *leaves(nv_ada, small(3), nv_in, nv_a, nv_b, nv_o))
```

```python
import functools
import math

import numpy as np
import jax
import jax.numpy as jnp
from jax import lax
from jax.experimental import pallas as pl
from jax.experimental.pallas import tpu as pltpu

F32 = jnp.float32
BF16 = jnp.bfloat16
MXU_DTYPE = jnp.bfloat16
XCHG_DTYPE = jnp.bfloat16

N_DEV = 8
D = 1024
A_HEADS = 8
A_HD = 64
A_W = A_HEADS * A_HD
A_BLK = 128
PATTERNS = ((128, 1), (512, 4), (2048, 16))
N_BUCKETS = 32
MAX_DISTANCE = 2048
NEG = -1e30
G_HEADS = 8
G_DK = 128
G_W = G_HEADS * G_DK
IN_W = 8 * D
EPS = 1e-6
ADAM_LR = 0.001
ADAM_B1 = 0.9
ADAM_B2 = 0.999
ADAM_EPS = 1e-08
ADAM_WD = 0.01
ADAM_STEP = 10

G_CHUNK = 128
G_SUB = 8
G_HPS_FWD = 8
G_HPS_BWD = 8
G_RB = 16
VMEM_LIMIT = 56 * 1024 * 1024

NN = (((1,), (0,)), ((), ()))
NT = (((1,), (1,)), ((), ()))
TN = (((0,), (0,)), ((), ()))
MESH = pl.DeviceIdType.MESH


def _mm(a, b, dims=NN):
    return lax.dot_general(a.astype(MXU_DTYPE), b.astype(MXU_DTYPE), dims,
                           preferred_element_type=F32)


def _mm_exact(t, x):
    hi = x.astype(BF16)
    r = x - hi.astype(F32)
    mid = r.astype(BF16)
    lo = (r - mid.astype(F32)).astype(BF16)
    tb = t.astype(BF16)
    return sum(lax.dot_general(tb, p, NN, preferred_element_type=F32) for p in (hi, mid, lo))


def _split(x):
    hi = x.astype(BF16)
    return hi, (x - hi.astype(F32)).astype(BF16)


def _mm_split(a, b, dims):
    dot = lambda p, q: lax.dot_general(p, q, dims, preferred_element_type=F32)
    return dot(a[0], b[0]) + dot(a[0], b[1]) + dot(a[1], b[0])


def _sigmoid(x):
    return 0.5 * jnp.tanh(0.5 * x) + 0.5


def _params(sem=None):
    return pltpu.CompilerParams(dimension_semantics=sem, vmem_limit_bytes=VMEM_LIMIT)


def _all_gather(xs, name):
    n = len(xs)

    def body(*refs):
        ins, outs = refs[:n], refs[n:2 * n]
        send_sems, recv_sems, local_sems = refs[2 * n:]
        x, y, c = lax.axis_index("x"), lax.axis_index("y"), lax.axis_index("c")
        me, sibling = (x, y, c), (x, y, 1 - c)
        chips = [(1 - x, y), (x, 1 - y), (1 - x, 1 - y)]

        def slot(ref, dev):
            return ref.at[4 * dev[0] + 2 * dev[1] + dev[2]]

        def copy(a, k, block, to, src=None):
            return pltpu.make_async_remote_copy(
                src_ref=slot(outs[a], block) if src is None else src,
                dst_ref=slot(outs[a], block),
                send_sem=send_sems.at[a, k], recv_sem=recv_sems.at[a, k],
                device_id=to, device_id_type=MESH)

        mine, first, passed = [], [], []
        for a in range(n):
            cp = pltpu.make_async_copy(ins[a], slot(outs[a], me), local_sems.at[a])
            cp.start()
            mine.append(cp)
            first.append(copy(a, 0, me, sibling, src=ins[a]))
            for j, chip in enumerate(chips):
                first.append(copy(a, 1 + j, me, (*chip, c), src=ins[a]))
        for cp in first:
            cp.start()
        for j, chip in enumerate(chips):
            for a in range(n):
                copy(a, 1 + j, (*chip, c), me).wait_recv()
                cp = copy(a, 4 + j, (*chip, c), sibling)
                cp.start()
                passed.append(cp)
        for a in range(n):
            copy(a, 0, sibling, me).wait_recv()
            for j, chip in enumerate(chips):
                copy(a, 4 + j, (*chip, 1 - c), me).wait_recv()
        for cp in first + passed:
            cp.wait_send()
        for cp in mine:
            cp.wait()

    any_spec = pl.BlockSpec(memory_space=pl.ANY)
    return pl.pallas_call(
        body, name=name,
        out_shape=[jax.ShapeDtypeStruct((N_DEV,) + v.shape, v.dtype) for v in xs],
        in_specs=[any_spec] * n, out_specs=[any_spec] * n,
        scratch_shapes=[pltpu.SemaphoreType.DMA((n, 7)), pltpu.SemaphoreType.DMA((n, 7)),
                        pltpu.SemaphoreType.DMA((n,))],
    )(*xs)


def _all_to_all_copies(ins, outs, send_sems, recv_sems, local_sems, gather=False):
    n = len(ins)
    x, y, c = lax.axis_index("x"), lax.axis_index("y"), lax.axis_index("c")
    me = 4 * x + 2 * y + c
    peers = []
    for m in range(1, N_DEV):
        peers.append((1 - x if m & 4 else x, 1 - y if m & 2 else y, 1 - c if m & 1 else c))

    def chunk(a, j):
        return ins[a] if gather else ins[a].at[j]

    def copy(a, k, landing):
        peer = peers[k]
        pid = 4 * peer[0] + 2 * peer[1] + peer[2]
        return pltpu.make_async_remote_copy(
            src_ref=chunk(a, pid), dst_ref=outs[a].at[pid if landing else me],
            send_sem=send_sems.at[a, k], recv_sem=recv_sems.at[a, k],
            device_id=peer, device_id_type=MESH)

    def local(a):
        return pltpu.make_async_copy(chunk(a, me), outs[a].at[me], local_sems.at[a])

    def start():
        for a in range(n):
            local(a).start()
        for k in range(N_DEV - 1):
            for a in range(n):
                copy(a, k, False).start()

    def wait():
        for k in range(N_DEV - 1):
            for a in range(n):
                copy(a, k, True).wait_recv()
        for k in range(N_DEV - 1):
            for a in range(n):
                copy(a, k, False).wait_send()
        for a in range(n):
            local(a).wait()

    return start, wait


def _rotating_exchange(stage, recv, send_sems, recv_sems, local_sem):
    x, y, c = lax.axis_index("x"), lax.axis_index("y"), lax.axis_index("c")
    me = 4 * x + 2 * y + c

    def chunk_of(r):
        return (me + 1 + r) % N_DEV

    def copy(r, landing):
        k = (me - 1 - r) % N_DEV if landing else chunk_of(r)
        return pltpu.make_async_remote_copy(
            src_ref=stage.at[k], dst_ref=recv.at[k if landing else me],
            send_sem=send_sems.at[r], recv_sem=recv_sems.at[r],
            device_id=(k // 4, (k // 2) % 2, k % 2), device_id_type=MESH)

    def own():
        return pltpu.make_async_copy(stage.at[me], recv.at[me], local_sem)

    def wait():
        for r in range(N_DEV - 1):
            copy(r, True).wait_recv()
        for r in range(N_DEV - 1):
            copy(r, False).wait_send()
        own().wait()

    return chunk_of, (lambda r: copy(r, False).start()), (lambda: own().start()), wait


def _mod_fwd(c64, w_ada, b_loc):
    def body(c_ref, w_ref, b_ref, o_ref):
        cv = c_ref[...]
        sc = cv * _sigmoid(cv)
        o_ref[...] = _mm(sc, w_ref[...]) + b_ref[...]

    return pl.pallas_call(
        body, name="mod_fwd",
        out_shape=jax.ShapeDtypeStruct((c64.shape[0], w_ada.shape[1]), F32),
        compiler_params=_params(),
    )(c64, w_ada, b_loc)


def _inproj(x, mod3, norm_g, w_in_g, blocks, tm=256):
    S = x.shape[0]
    ni = S // tm
    nb = len(blocks)

    def body(*refs):
        x_ref, mod_ref, g_ref, w_ref = refs[:4]
        b_ins = refs[4:4 + nb]
        proj_ref, ht_ref, qkv_ref = refs[4 + nb:7 + nb]
        b_outs = refs[7 + nb:7 + 2 * nb]
        w_all, send_sems, recv_sems, local_sems = refs[7 + 2 * nb:]
        i = pl.program_id(0)
        start, wait = _all_to_all_copies(b_ins, b_outs, send_sems, recv_sems, local_sems, gather=True)

        @pl.when(i == 0)
        def _():
            start()
            pltpu.sync_copy(w_ref, w_all)

        xv = x_ref[...]
        r = lax.rsqrt(jnp.mean(xv * xv, axis=-1, keepdims=True) + EPS)
        h = ((xv * r * g_ref[...]) * (1.0 + mod_ref[1:2, :]) + mod_ref[0:1, :]).astype(MXU_DTYPE)
        ht_ref[...] = h.T
        for j in range(N_DEV):
            pj = _mm(h, w_all[j])
            proj_ref[:, j * D:(j + 1) * D] = pj
            for c in range(3):
                if c // 2 == j:
                    for p in range(A_HEADS // 2):
                        lo = (c % 2) * A_W + 2 * A_HD * p
                        qkv_ref[c, p] = pj[:, lo:lo + 2 * A_HD]

        @pl.when(i == ni - 1)
        def _():
            wait()

    any_spec = pl.BlockSpec(memory_space=pl.ANY)
    return pl.pallas_call(
        body, name="inproj_gather",
        grid=(ni,),
        in_specs=[pl.BlockSpec((tm, D), lambda i: (i, 0)),
                  pl.BlockSpec((8, D), lambda i: (0, 0)),
                  pl.BlockSpec((1, D), lambda i: (0, 0)),
                  any_spec] + [any_spec] * nb,
        out_specs=[pl.BlockSpec((tm, IN_W), lambda i: (i, 0)),
                   pl.BlockSpec((D, tm), lambda i: (0, i)),
                   pl.BlockSpec((3, A_HEADS // 2, tm, 2 * A_HD), lambda i: (0, 0, i, 0))] + [any_spec] * nb,
        out_shape=[jax.ShapeDtypeStruct((S, IN_W), F32), jax.ShapeDtypeStruct((D, S), MXU_DTYPE),
                   jax.ShapeDtypeStruct((3, A_HEADS // 2, S, 2 * A_HD), F32)]
                  + [jax.ShapeDtypeStruct((N_DEV,) + b.shape, b.dtype) for b in blocks],
        scratch_shapes=[pltpu.VMEM(w_in_g.shape, w_in_g.dtype),
                        pltpu.SemaphoreType.DMA((nb, N_DEV - 1)), pltpu.SemaphoreType.DMA((nb, N_DEV - 1)),
                        pltpu.SemaphoreType.DMA((nb,))],
        compiler_params=_params(("arbitrary",)),
    )(x, mod3, norm_g, w_in_g, *blocks)


def _bucket_tables():
    qi = np.arange(A_BLK)[:, None]
    kj = np.arange(2 * A_BLK)[None, :]
    delta = qi + A_BLK - kj
    out = []
    for window, dil in PATTERNS:
        span = window // dil
        band = (delta >= 0) & (delta <= span)
        dist = np.clip(delta, 0, None) * dil
        max_exact = N_BUCKETS // 2
        nf = dist.astype(np.float32)
        large = max_exact + (np.log(np.maximum(nf, np.float32(1.0)) / np.float32(max_exact))
                             / np.float32(math.log(MAX_DISTANCE / max_exact))
                             * np.float32(N_BUCKETS - max_exact)).astype(np.int32)
        large = np.minimum(large, N_BUCKETS - 1)
        bucket = np.where(dist < max_exact, dist, large)
        out.append(np.where(band, bucket, -1).astype(np.int32))
    return np.stack(out)


def _bias_tables(rel_bias, buckets):
    def body(rb_ref, bk_ref, o_ref):
        h = pl.program_id(1)
        bk = bk_ref[0]
        acc = jnp.full(bk.shape, NEG, F32)
        for b in range(N_BUCKETS):
            acc = jnp.where(bk == b, rb_ref[b, h], acc)
        o_ref[0, 0] = acc

    return pl.pallas_call(
        body, name="bias_tables",
        grid=(3, A_HEADS),
        in_specs=[pl.BlockSpec(memory_space=pltpu.SMEM),
                  pl.BlockSpec((1, A_BLK, 2 * A_BLK), lambda p, h: (p, 0, 0))],
        out_specs=pl.BlockSpec((1, 1, A_BLK, 2 * A_BLK), lambda p, h: (p, h, 0, 0)),
        out_shape=jax.ShapeDtypeStruct((3, A_HEADS, A_BLK, 2 * A_BLK), F32),
        compiler_params=_params(("arbitrary", "arbitrary")),
    )(rel_bias, buckets)


A_TILES = 32


def _attn_heads_per_step(d):
    return A_HEADS if d == 1 else 2


def _attn_in_specs(sb, nsb, hw):
    blk = (1, hw // 2, sb, 2 * A_HD)

    def cur(c):
        return pl.BlockSpec(blk, lambda hp, n: (c, hp, jnp.minimum(n, nsb - 1), 0))

    def prev(c):
        return pl.BlockSpec(blk, lambda hp, n: (c, hp, jnp.maximum(jnp.minimum(n, nsb - 1) - 1, 0), 0))

    return [cur(0), prev(1), cur(1), prev(2), cur(2)]


def _rows(r, d):
    return pl.ds(r, A_BLK) if d == 1 else pl.ds(r, A_BLK, stride=d)


def _for_residues(d, hw, fn):
    unroll = min(d, max(1, A_TILES // hw))
    if d == unroll:
        _round_robin([g for r in range(d) for g in fn(r)])
    else:
        def group(g, c):
            _round_robin([t for u in range(unroll) for t in fn(g * unroll + u)])
            return c
        lax.fori_loop(0, d // unroll, group, 0)


def _attn_stack(t):
    first_half = lax.broadcasted_iota(jnp.int32, (1, 2 * A_HD), 1) < A_HD
    return jnp.concatenate([jnp.where(first_half, t, 0.0), jnp.where(first_half, 0.0, t)], axis=0)


def _attn_unstack(t2):
    first_half = lax.broadcasted_iota(jnp.int32, (1, 2 * A_HD), 1) < A_HD
    return jnp.where(first_half, t2[:A_BLK], t2[A_BLK:])


def _attn_scores(q, k, b_ref, pp, first):
    bias = jnp.concatenate([b_ref[2 * pp] + first, b_ref[2 * pp + 1] + first], axis=0)
    return _mm(_attn_stack(q), k, NT) * (A_HD ** -0.5) + bias


def _attn_fwd(qkv, bias_p, d, name):
    S = qkv.shape[2]
    hw = _attn_heads_per_step(d)
    sub = A_BLK * d
    nsub = max(1, A_TILES // (hw * d))
    sb = sub * nsub
    nsb = S // sb

    def body(q_ref, kp_ref, kc_ref, vp_ref, vc_ref, b_ref, o_ref, l_ref):
        n = pl.program_id(1)
        kj = lax.broadcasted_iota(jnp.int32, (A_BLK, 2 * A_BLK), 1)
        first = jnp.where((n == 0) & (kj < A_BLK), NEG, 0.0).astype(F32)
        no_first = jnp.zeros((A_BLK, 2 * A_BLK), F32)

        def residue(r, u=0):
            rows = _rows(u * sub + r, d)
            behind = _rows(((nsub if u == 0 else u) - 1) * sub + r, d)

            def pair(pp):
                lanes = pl.ds(2 * A_HD * pp, 2 * A_HD)
                kc, vc = kc_ref.at[0, pp], vc_ref.at[0, pp]
                kb, vb = (kp_ref.at[0, pp], vp_ref.at[0, pp]) if u == 0 else (kc, vc)
                k = jnp.concatenate([kb[behind, :], kc[rows, :]], axis=0)
                v = jnp.concatenate([vb[behind, :], vc[rows, :]], axis=0)
                s = _attn_scores(q_ref.at[0, pp][rows, :], k, b_ref, pp, first if u == 0 else no_first)
                yield
                m = jnp.max(s, axis=-1, keepdims=True)
                p = jnp.exp(s - m)
                den = jnp.sum(p, axis=-1, keepdims=True)
                pv = _mm(p, v)
                yield
                o_ref[rows, lanes] = _attn_unstack(pv / den)
                l_ref[rows, lanes] = _attn_unstack(jnp.broadcast_to(m + jnp.log(den), (2 * A_BLK, 2 * A_HD)))

            return [pair(pp) for pp in range(hw // 2)]

        if nsub == 1:
            _for_residues(d, hw, residue)
        else:
            _round_robin([g for u in range(nsub) for r in range(d) for g in residue(r, u)])

    out = pl.BlockSpec((sb, A_HD * hw), lambda hp, n: (n, hp))
    return pl.pallas_call(
        body, name=name,
        grid=(A_HEADS // hw, nsb),
        in_specs=_attn_in_specs(sb, nsb, hw) + [pl.BlockSpec((hw, A_BLK, 2 * A_BLK), lambda hp, n: (hp, 0, 0))],
        out_specs=[out, out],
        out_shape=[jax.ShapeDtypeStruct((S, A_W), F32)] * 2,
        compiler_params=_params(("parallel", "parallel")),
    )(qkv, qkv, qkv, qkv, qkv, bias_p)


def _attn_bwd(qkv, do, lt, delta, bias_p, d, name, prev=(), out_dtype=F32):
    S = qkv.shape[2]
    hw = _attn_heads_per_step(d)
    sub = A_BLK * d
    nsub = max(1, A_TILES // (hw * d))
    sb = sub * nsub
    nsb = S // sb
    done = (nsub - 1) * sub

    def body(*refs):
        q_ref, kp_ref, kc_ref, vp_ref, vc_ref, do_ref, lt_ref, dl_ref, b_ref = refs[:9]
        pq_ref, pk_ref, pv_ref = refs[9:9 + len(prev)] if prev else (None, None, None)
        dq_ref, dk_ref, dv_ref, db_ref, ck, cv = refs[9 + len(prev):]
        n = pl.program_id(1)
        plus = lambda t, p_ref, idx: (t if p_ref is None else t + p_ref[idx]).astype(out_dtype)

        @pl.when(n == 0)
        def _():
            db_ref[...] = jnp.zeros_like(db_ref)
            ck[...] = jnp.zeros_like(ck)
            cv[...] = jnp.zeros_like(cv)

        @pl.when(n < nsb)
        def _():
            kj = lax.broadcasted_iota(jnp.int32, (A_BLK, 2 * A_BLK), 1)
            first = jnp.where((n == 0) & (kj < A_BLK), NEG, 0.0).astype(F32)
            no_first = jnp.zeros((A_BLK, 2 * A_BLK), F32)
            if done:
                dk_ref[0:done, :] = plus(ck[0:done, :], pk_ref, (slice(0, done), slice(None)))
                dv_ref[0:done, :] = plus(cv[0:done, :], pv_ref, (slice(0, done), slice(None)))

            def residue(r, u=0):
                rows = _rows(u * sub + r, d)
                behind = _rows(((nsub if u == 0 else u) - 1) * sub + r, d)

                def pair(pp):
                    lanes = pl.ds(2 * A_HD * pp, 2 * A_HD)
                    lt_r, dl_r = lt_ref[rows, lanes], dl_ref[rows, lanes]
                    kc, vc = kc_ref.at[0, pp], vc_ref.at[0, pp]
                    kb, vb = (kp_ref.at[0, pp], vp_ref.at[0, pp]) if u == 0 else (kc, vc)
                    k = jnp.concatenate([kb[behind, :], kc[rows, :]], axis=0)
                    v = jnp.concatenate([vb[behind, :], vc[rows, :]], axis=0)
                    q2 = _attn_stack(q_ref.at[0, pp][rows, :])
                    do2 = _attn_stack(do_ref[rows, lanes])
                    col = lambda t: jnp.concatenate([t[:, 0:1], t[:, A_HD:A_HD + 1]], axis=0)
                    s = _attn_scores(q_ref.at[0, pp][rows, :], k, b_ref, pp, first if u == 0 else no_first)
                    dp = _mm(do2, v, NT)
                    yield
                    p = jnp.exp(s - col(lt_r))
                    ds = p * (dp - col(dl_r))
                    db_ref[2 * pp] += ds[:A_BLK]
                    db_ref[2 * pp + 1] += ds[A_BLK:]
                    dq = _mm(ds, k)
                    dk = _mm(ds, q2, TN) * (A_HD ** -0.5)
                    dv = _mm(p, do2, TN)
                    yield
                    dq_ref[rows, lanes] = plus(_attn_unstack(dq) * (A_HD ** -0.5), pq_ref, (rows, lanes))
                    if u == 0:
                        dk_ref[behind, lanes] = plus(ck[behind, lanes] + dk[:A_BLK], pk_ref, (behind, lanes))
                        dv_ref[behind, lanes] = plus(cv[behind, lanes] + dv[:A_BLK], pv_ref, (behind, lanes))
                    else:
                        ck[behind, lanes] += dk[:A_BLK]
                        cv[behind, lanes] += dv[:A_BLK]
                    ck[rows, lanes] = dk[A_BLK:]
                    cv[rows, lanes] = dv[A_BLK:]

                return [pair(pp) for pp in range(hw // 2)]

            if nsub == 1:
                _for_residues(d, hw, residue)
            else:
                _round_robin([g for u in range(nsub) for r in range(d) for g in residue(r, u)])

        @pl.when(n == nsb)
        def _():
            dk_ref[...] = plus(ck[...], pk_ref, ...)
            dv_ref[...] = plus(cv[...], pv_ref, ...)

    w = A_HD * hw
    row = pl.BlockSpec((sb, w), lambda hp, n: (jnp.minimum(n, nsb - 1), hp))
    lag = pl.BlockSpec((sb, w), lambda hp, n: (jnp.maximum(n - 1, 0), hp))
    tab = pl.BlockSpec((hw, A_BLK, 2 * A_BLK), lambda hp, n: (hp, 0, 0))
    return pl.pallas_call(
        body, name=name,
        grid=(A_HEADS // hw, nsb + 1),
        in_specs=_attn_in_specs(sb, nsb, hw) + [row, row, row, tab] + ([row, lag, lag] if prev else []),
        out_specs=[row, lag, lag, tab],
        out_shape=[jax.ShapeDtypeStruct((S, A_W), out_dtype)] * 3
                  + [jax.ShapeDtypeStruct((A_HEADS, A_BLK, 2 * A_BLK), F32)],
        scratch_shapes=[pltpu.VMEM((sb, w), F32), pltpu.VMEM((sb, w), F32)],
        compiler_params=_params(("parallel", "arbitrary")),
    )(qkv, qkv, qkv, qkv, qkv, do, lt, delta, bias_p, *prev)


def _rel_bias_grad(dbs, buckets):
    def body(d1, d2, d3, bk_ref, o_ref):
        row = lax.broadcasted_iota(jnp.int32, (A_HEADS, 128), 0)
        lane = lax.broadcasted_iota(jnp.int32, (A_HEADS, 128), 1)
        acc = jnp.zeros((A_HEADS, 128), F32)
        for p, dref in enumerate((d1, d2, d3)):
            bk = bk_ref[p]
            for h in range(A_HEADS):
                ds = dref[h]
                for b in range(N_BUCKETS):
                    s = jnp.sum(jnp.where(bk == b, ds, 0.0), keepdims=True)
                    acc = acc + jnp.where((row == h) & (lane == b), s, 0.0)
        o_ref[...] = acc

    return pl.pallas_call(
        body, name="rel_bias_grad",
        out_shape=jax.ShapeDtypeStruct((A_HEADS, 128), F32),
        compiler_params=_params(),
    )(*dbs, buckets)


def _tri(c):
    t = np.tril(np.ones((c, c), np.float32))
    return jnp.asarray(t), jnp.asarray(t.T.copy())


def _fill_above(ref, x, pad):
    ref[0:G_SUB, :] = jnp.full((G_SUB, x.shape[1]), pad, F32)
    ref[G_SUB:, :] = x


def _fill_below(ref, x, pad):
    ref[0:x.shape[0], :] = x
    ref[x.shape[0]:, :] = jnp.full((G_SUB, x.shape[1]), pad, F32)


def _hgrn_gates(q_ref, f_ref, lbp_ref, tri_ref):
    qraw = q_ref[...]
    sq = _sigmoid(qraw)
    q = qraw * sq
    sg = _sigmoid(f_ref[...])
    lb = _sigmoid(lbp_ref[0:1, :] - lbp_ref[1:2, :])
    f = lb + (1.0 - lb) * sg
    k = 1.0 - f
    b = _mm_exact(tri_ref[...], jnp.log(f))
    return qraw, sq, q, sg, lb, f, k, b


def _hgrn_col(C, base, idx, hps):
    return pl.BlockSpec((C, hps * G_DK), lambda h, n: (idx(n), base * (G_HEADS // hps) + h))


def _round_robin(stages):
    live = list(stages)
    while live:
        nxt = []
        for g in live:
            try:
                next(g)
                nxt.append(g)
            except StopIteration:
                pass
        live = nxt


def _hgrn_levels(C):
    out, m = [], G_SUB
    while 2 * m <= C:
        out.append(m)
        m *= 2
    return out


def _hgrn_level_masks(C):
    ti = np.arange(C)[:, None]
    si = np.arange(C)[None, :]
    return jnp.asarray(np.stack([((ti // (2 * m) == si // (2 * m)) & (ti - si >= G_SUB)).astype(np.float32)
                                 for m in _hgrn_levels(C)]))


def _hgrn_level(b, q, k, C, m):
    zeros = jnp.zeros((m, G_DK), F32)
    eq, ek, qt, kt = [], [], [], []
    for blk in range(0, C // m, 2):
        lo, mid, hi = blk * m, (blk + 1) * m, (blk + 2) * m
        ref = b[mid:mid + 1]
        e_right = jnp.exp(b[mid:hi] - ref)
        e_left = jnp.exp(ref - b[lo:mid])
        eq += [zeros, e_right]
        ek += [e_left, zeros]
        qt += [zeros, q[mid:hi] * e_right]
        kt += [k[lo:mid] * e_left, zeros]
    cat = lambda parts: jnp.concatenate(parts, axis=0)
    return cat(qt), cat(kt), cat(eq), cat(ek)


def _hgrn_fwd(proj, hgrn_lb, onorm_g, C=G_CHUNK):
    S = proj.shape[0]
    nc = S // C
    tri, _ = _tri(C)
    masks = _hgrn_level_masks(C)
    hps = G_HPS_FWD

    def body(q_ref, f_ref, i_ref, z_ref, lbp_ref, go_ref, tri_ref, pm_ref, o_ref, ob_ref, st_ref, St, kp, vp, fp):
        @pl.when(pl.program_id(1) == 0)
        def _():
            St[...] = jnp.zeros_like(St)

        heads = []
        for hh in range(hps):
            ln = pl.ds(G_DK * hh, G_DK)
            heads.append(head(
                q_ref.at[:, ln], f_ref.at[:, ln], i_ref.at[:, ln], z_ref.at[:, ln], lbp_ref.at[:, ln], go_ref,
                tri_ref, pm_ref, o_ref.at[:, ln], ob_ref.at[:, ln], st_ref.at[0, hh], St.at[hh], kp.at[hh], vp.at[hh],
                fp.at[hh]))
        _round_robin(heads)

    def head(q_ref, f_ref, i_ref, z_ref, lbp_ref, go_ref, tri_ref, pm_ref, o_ref, ob_ref, st_ref, St, kp, vp, fp):
        _, _, q, _, _, f, k, b = _hgrn_gates(q_ref, f_ref, lbp_ref, tri_ref)
        v = i_ref[...]
        bC = b[C - 1:C, :]
        S0 = St[...]
        o = _mm(q * jnp.exp(b), S0, NT)
        yield
        _fill_above(kp, k, 0.0)
        _fill_above(vp, v, 0.0)
        _fill_above(fp, f, 1.0)
        near = []
        for r0 in range(0, C, G_RB):
            qb = q[r0:r0 + G_RB]
            acc = e = None
            for l in range(G_SUB):
                rows = pl.ds(G_SUB - l + r0, G_RB)
                if l > 0:
                    fl = fp[pl.ds(G_SUB - l + 1 + r0, G_RB), :]
                    e = fl if e is None else e * fl
                kl = kp[rows, :]
                a = jnp.sum(qb * kl if e is None else qb * kl * e, axis=-1, keepdims=True)
                t = a * vp[rows, :]
                acc = t if acc is None else acc + t
            near.append(acc)
        o = o + jnp.concatenate(near, axis=0)
        yield
        a_off = jnp.zeros((C, C), F32)
        for lv, m in enumerate(_hgrn_levels(C)):
            qt, kt, _, _ = _hgrn_level(b, q, k, C, m)
            prod = _mm_split(_split(qt), _split(kt), NT) if m == G_SUB else _mm(qt, kt, NT)
            a_off = a_off + pm_ref[lv] * prod
        yield
        o = o + _mm(a_off, v)
        S1 = S0 * jnp.exp(bC) + _mm(v, k * jnp.exp(bC - b), TN)
        St[...] = S1
        st_ref[...] = S1
        o_ref[...] = o
        r = lax.rsqrt(jnp.mean(o * o, axis=-1, keepdims=True) + EPS)
        z = z_ref[...]
        ob_ref[...] = (o * r * go_ref[...] * (z * _sigmoid(z))).astype(MXU_DTYPE)

    ident = lambda n: n
    w = hps * G_DK
    out = pl.BlockSpec((C, w), lambda h, n: (n, h))
    return pl.pallas_call(
        body, name="hgrn_fwd",
        grid=(G_HEADS // hps, nc),
        in_specs=[_hgrn_col(C, base, ident, hps) for base in (2, 3, 4, 5)] + [
                  pl.BlockSpec((2, w), lambda h, n: (0, h)),
                  pl.BlockSpec((1, G_DK), lambda h, n: (0, 0)),
                  pl.BlockSpec((C, C), lambda h, n: (0, 0)),
                  pl.BlockSpec(masks.shape, lambda h, n: (0, 0, 0))],
        out_specs=[out, out, pl.BlockSpec((1, hps, G_DK, G_DK), lambda h, n: (n, h, 0, 0))],
        out_shape=[jax.ShapeDtypeStruct((S, G_W), F32), jax.ShapeDtypeStruct((S, G_W), MXU_DTYPE),
                   jax.ShapeDtypeStruct((nc, G_HEADS, G_DK, G_DK), F32)],
        scratch_shapes=[pltpu.VMEM((hps, G_DK, G_DK), F32)] + [pltpu.VMEM((hps, C + G_SUB, G_DK), F32)] * 3,
        compiler_params=_params(("parallel", "arbitrary")),
    )(proj, proj, proj, proj, hgrn_lb, onorm_g, tri, masks)


def _hgrn_bwd(proj, o_raw, dob, states, hgrn_lb, onorm_g, C=G_CHUNK):
    S = proj.shape[0]
    nc = S // C
    tri, triu = _tri(C)
    masks = _hgrn_level_masks(C)
    hps = G_HPS_BWD

    def body(q_ref, f_ref, i_ref, z_ref, o_ref, dob_ref, s0_ref, s1_ref, lbp_ref, go_ref, tri_ref, triu_ref,
             pm_ref, dq_ref, df_ref, di_ref, dz_ref, dlb_ref, dgo_ref, dSt, *shifted):
        @pl.when(pl.program_id(1) == 0)
        def _():
            dSt[...] = jnp.zeros_like(dSt)
            dlb_ref[...] = jnp.zeros_like(dlb_ref)
            dgo_ref[...] = jnp.zeros_like(dgo_ref)

        heads = []
        for hh in range(hps):
            ln = pl.ds(G_DK * hh, G_DK)
            heads.append(head(
                q_ref.at[:, ln], f_ref.at[:, ln], i_ref.at[:, ln], z_ref.at[:, ln], o_ref.at[:, ln],
                dob_ref.at[:, ln], s0_ref.at[0, hh], s1_ref.at[0, hh], lbp_ref.at[:, ln], go_ref, tri_ref, triu_ref,
                pm_ref, dq_ref.at[:, ln], df_ref.at[:, ln], di_ref.at[:, ln], dz_ref.at[:, ln], dlb_ref.at[:, ln],
                dgo_ref.at[pl.ds(8 * hh, 8), :], dSt.at[hh], *[t.at[hh] for t in shifted]))
        _round_robin(heads)

    def head(q_ref, f_ref, i_ref, z_ref, o_ref, dob_ref, s0_ref, s1_ref, lbp_ref, go_ref, tri_ref, triu_ref,
             pm_ref, dq_ref, df_ref, di_ref, dz_ref, dlb_ref, dgo_ref, dSt, kp, vp, fp, qn, dn_, fn, xs, dac):
        cn = nc - 1 - pl.program_id(1)
        qraw, sq, q, sg, lb, f, k, b = _hgrn_gates(q_ref, f_ref, lbp_ref, tri_ref)
        v = i_ref[...]
        bC = b[C - 1:C, :]
        eb = jnp.exp(b)
        ecb = jnp.exp(bC - b)
        o = o_ref[...]
        z = z_ref[...]
        sz = _sigmoid(z)
        go = go_ref[...]
        g_ob = dob_ref[...]
        r = lax.rsqrt(jnp.mean(o * o, axis=-1, keepdims=True) + EPS)
        nh = o * r
        dnrm = g_ob * (z * sz)
        dz_ref[...] = (g_ob * (nh * go) * (sz * (1.0 + z * (1.0 - sz)))).astype(MXU_DTYPE)
        dgo_ref[0:1, :] += jnp.sum(dnrm * nh, axis=0, keepdims=True)
        dn = dnrm * go
        do = r * (dn - nh * jnp.mean(dn * nh, axis=-1, keepdims=True))

        yield
        S0 = jnp.where(cn == 0, 0.0, s0_ref[...])
        S1 = s1_ref[...]
        dS1 = dSt[...]
        dq = eb * _mm(do, S0)
        dk = ecb * _mm(v, dS1)
        dv = _mm(k * ecb, dS1, NT)
        bnd = jnp.sum(dS1 * S1, axis=0, keepdims=True)
        dSt[...] = dS1 * jnp.exp(bC) + _mm(do, q * eb, TN)

        _fill_above(kp, k, 0.0)
        _fill_above(vp, v, 0.0)
        _fill_above(fp, f, 1.0)
        _fill_below(qn, q, 0.0)
        _fill_below(dn_, do, 0.0)
        _fill_below(fn, f, 1.0)
        yield
        for r0 in range(0, C, G_RB):
            do_b = do[r0:r0 + G_RB]
            for l in range(G_SUB):
                xs[pl.ds(l * C + r0, G_RB), :] = (do_b * vp[pl.ds(G_SUB - l + r0, G_RB), :]).astype(MXU_DTYPE)
        dac[0:G_SUB * C, :] = _mm(xs[...], jnp.ones((G_DK, G_DK), MXU_DTYPE))
        dac[G_SUB * C:, :] = jnp.zeros((G_SUB, G_DK), F32)
        yield
        near_q, near_k, near_v = [], [], []
        for r0 in range(0, C, G_RB):
            k_b = k[r0:r0 + G_RB]
            aq = ak = av = e = e2 = None
            for l in range(G_SUB):
                down, up = pl.ds(G_SUB - l + r0, G_RB), pl.ds(l + r0, G_RB)
                if l > 0:
                    fl = fp[pl.ds(G_SUB - l + 1 + r0, G_RB), :]
                    e = fl if e is None else e * fl
                    fu = fn[up, :]
                    e2 = fu if e2 is None else e2 * fu
                kl = kp[down, :]
                t = dac[pl.ds(l * C + r0, G_RB), :] * (kl if e is None else kl * e)
                aq = t if aq is None else aq + t
                qu = qn[up, :]
                qe = qu if e2 is None else qu * e2
                dou = dn_[up, :]
                a2 = jnp.sum(qe * k_b, axis=-1, keepdims=True)
                t = dac[pl.ds(l * C + l + r0, G_RB), :] * qe
                ak = t if ak is None else ak + t
                t = a2 * dou
                av = t if av is None else av + t
            near_q.append(aq)
            near_k.append(ak)
            near_v.append(av)
        dq = dq + jnp.concatenate(near_q, axis=0)
        dk = dk + jnp.concatenate(near_k, axis=0)
        dv = dv + jnp.concatenate(near_v, axis=0)

        yield
        da_all = _mm(do, v, NT)
        a_off = jnp.zeros((C, C), F32)
        for lv, m in enumerate(_hgrn_levels(C)):
            qt, kt, eq, ek = _hgrn_level(b, q, k, C, m)
            da_m = pm_ref[lv] * da_all
            if m == G_SUB:
                qs, ks, das = _split(qt), _split(kt), _split(da_m)
                a_off = a_off + pm_ref[lv] * _mm_split(qs, ks, NT)
                dq = dq + _mm_split(das, ks, NN) * eq
                dk = dk + _mm_split(das, qs, TN) * ek
            else:
                a_off = a_off + pm_ref[lv] * _mm(qt, kt, NT)
                dq = dq + _mm(da_m, kt) * eq
                dk = dk + _mm(da_m, qt, TN) * ek
        dv = dv + _mm(a_off, do, TN)

        yield
        row = lax.broadcasted_iota(jnp.int32, (C, 1), 0)
        db = q * dq - k * dk + jnp.where(row == C - 1, bnd, 0.0)
        dg = _mm_exact(triu_ref[...], db)
        df = dg / f - dk
        df_ref[...] = (df * (1.0 - lb) * (sg * (1.0 - sg))).astype(MXU_DTYPE)
        dlb_ref[0:1, :] += jnp.sum(df * (1.0 - sg), axis=0, keepdims=True)
        dq_ref[...] = (dq * (sq * (1.0 + qraw * (1.0 - sq)))).astype(MXU_DTYPE)
        di_ref[...] = dv.astype(MXU_DTYPE)

    rev = lambda n: nc - 1 - n
    w = hps * G_DK
    blk = pl.BlockSpec((C, w), lambda h, n: (nc - 1 - n, h))
    return pl.pallas_call(
        body, name="hgrn_bwd",
        grid=(G_HEADS // hps, nc),
        in_specs=[_hgrn_col(C, base, rev, hps) for base in (2, 3, 4, 5)] + [
                  blk, blk,
                  pl.BlockSpec((1, hps, G_DK, G_DK), lambda h, n: (jnp.maximum(nc - 2 - n, 0), h, 0, 0)),
                  pl.BlockSpec((1, hps, G_DK, G_DK), lambda h, n: (nc - 1 - n, h, 0, 0)),
                  pl.BlockSpec((2, w), lambda h, n: (0, h)),
                  pl.BlockSpec((1, G_DK), lambda h, n: (0, 0)),
                  pl.BlockSpec((C, C), lambda h, n: (0, 0)),
                  pl.BlockSpec((C, C), lambda h, n: (0, 0)),
                  pl.BlockSpec(masks.shape, lambda h, n: (0, 0, 0))],
        out_specs=[blk, blk, blk, blk,
                   pl.BlockSpec((8, w), lambda h, n: (0, h)),
                   pl.BlockSpec((8 * hps, G_DK), lambda h, n: (h, 0))],
        out_shape=[jax.ShapeDtypeStruct((S, G_W), MXU_DTYPE)] * 4
                  + [jax.ShapeDtypeStruct((8, G_W), F32), jax.ShapeDtypeStruct((8 * G_HEADS, G_DK), F32)],
        scratch_shapes=[pltpu.VMEM((hps, G_DK, G_DK), F32)] + [pltpu.VMEM((hps, C + G_SUB, G_DK), F32)] * 6
                       + [pltpu.VMEM((hps, G_SUB * C, G_DK), MXU_DTYPE),
                          pltpu.VMEM((hps, G_SUB * C + G_SUB, G_DK), F32)],
        compiler_params=_params(("parallel", "arbitrary")),
    )(proj, proj, proj, proj, o_raw, dob, states, states, hgrn_lb, onorm_g, tri, triu, masks)


def _tail(x, target, os, ls, ob, proj, mod3, final_g, wa, wb, wo, tm=256):
    S = x.shape[0]
    nt = S // tm

    def body(x_ref, t_ref, o1, o2, o3, l1, l2, l3, za_ref, ob_ref, ga_ref, gb_ref, mod_ref, fg_ref,
             wa_ref, wb_ref, wo_ref,
             lt_ref, dx2_ref, do_ref, dl_ref, dza_ref, dob_ref, dga_ref, dgb_ref, sums_ref,
             gwa_ref, gwb_ref, gwo_ref, acc_a, acc_b, acc_o):
        i = pl.program_id(0)

        @pl.when(i == 0)
        def _():
            sums_ref[...] = jnp.zeros_like(sums_ref)
            acc_a[...] = jnp.zeros_like(acc_a)
            acc_b[...] = jnp.zeros_like(acc_b)
            acc_o[...] = jnp.zeros_like(acc_o)

        a1, a2, a3 = l1[...], l2[...], l3[...]
        lm = jnp.maximum(jnp.maximum(a1, a2), a3)
        e1, e2, e3 = jnp.exp(a1 - lm), jnp.exp(a2 - lm), jnp.exp(a3 - lm)
        lden = e1 + e2 + e3
        ao = (e1 * o1[...] + e2 * o2[...] + e3 * o3[...]) / lden
        lt_ref[...] = lm + jnp.log(lden)
        za = za_ref[...]
        sza = _sigmoid(za)
        oa_v, ob_v = (ao * (za * sza)).astype(MXU_DTYPE), ob_ref[...]
        pa = _mm(oa_v, wa_ref[...])
        pb = _mm(ob_v, wb_ref[...])
        sa, sb = _sigmoid(ga_ref[...]), _sigmoid(gb_ref[...])
        ym = sa * pa + sb * pb
        u = _mm(ym, wo_ref[...])
        gate = mod_ref[2:3, :]
        fg = fg_ref[...]
        x2 = x_ref[...] + gate * u
        r2 = lax.rsqrt(jnp.mean(x2 * x2, axis=-1, keepdims=True) + EPS)
        xn2 = x2 * r2
        e = xn2 * fg - t_ref[...]
        dy = e * (1.0 / D)
        dn = dy * fg
        dx2 = r2 * (dn - xn2 * jnp.mean(dn * xn2, axis=-1, keepdims=True))
        dx2_ref[...] = dx2
        sums_ref[0:1, :] += jnp.sum(dy * xn2, axis=0, keepdims=True)
        sums_ref[1:2, :] += jnp.sum(dx2 * u, axis=0, keepdims=True)
        sums_ref[2:3, :] += (0.5 / D) * jnp.sum(e * e, axis=0, keepdims=True)
        du = dx2 * gate
        dym = _mm(du, wo_ref[...], NT)
        acc_o[...] += _mm(ym, du, TN)
        dpa, dpb = dym * sa, dym * sb
        dga_ref[...] = (dym * pa * (sa * (1.0 - sa))).astype(MXU_DTYPE)
        dgb_ref[...] = (dym * pb * (sb * (1.0 - sb))).astype(MXU_DTYPE)
        doa = _mm(dpa, wa_ref[...], NT)
        dza_ref[...] = (doa * ao * (sza * (1.0 + za * (1.0 - sza)))).astype(MXU_DTYPE)
        do = doa * (za * sza)
        do_ref[...] = do
        prod = do * ao
        for h in range(A_HEADS):
            sl = slice(A_HD * h, A_HD * (h + 1))
            dl_ref[:, sl] = jnp.broadcast_to(jnp.sum(prod[:, sl], axis=-1, keepdims=True), (tm, A_HD))
        dob_ref[...] = _mm(dpb, wb_ref[...], NT)
        acc_a[...] += _mm(oa_v, dpa, TN)
        acc_b[...] += _mm(ob_v, dpb, TN)

        @pl.when(i == nt - 1)
        def _():
            pltpu.sync_copy(acc_a, gwa_ref)
            pltpu.sync_copy(acc_b, gwb_ref)
            pltpu.sync_copy(acc_o, gwo_ref)

    row = lambda w: pl.BlockSpec((tm, w), lambda i: (i, 0))
    full = lambda a, b: pl.BlockSpec((a, b), lambda i: (0, 0))
    any_spec = pl.BlockSpec(memory_space=pl.ANY)
    return pl.pallas_call(
        body, name="tail",
        grid=(nt,),
        in_specs=[row(D), row(D)] + [row(A_W)] * 6 + [pl.BlockSpec((tm, A_W), lambda i: (i, 3)), row(D),
                  pl.BlockSpec((tm, D), lambda i: (i, 6)), pl.BlockSpec((tm, D), lambda i: (i, 7)),
                  full(8, D), full(1, D), full(A_W, D), full(D, D), full(D, D)],
        out_specs=[row(A_W), row(D), row(A_W), row(A_W), row(A_W), row(D), row(D), row(D), full(8, D),
                   any_spec, any_spec, any_spec],
        out_shape=[jax.ShapeDtypeStruct((S, A_W), F32),
                   jax.ShapeDtypeStruct((S, D), F32), jax.ShapeDtypeStruct((S, A_W), F32),
                   jax.ShapeDtypeStruct((S, A_W), F32), jax.ShapeDtypeStruct((S, A_W), MXU_DTYPE),
                   jax.ShapeDtypeStruct((S, D), F32), jax.ShapeDtypeStruct((S, D), MXU_DTYPE),
                   jax.ShapeDtypeStruct((S, D), MXU_DTYPE), jax.ShapeDtypeStruct((8, D), F32),
                   jax.ShapeDtypeStruct((A_W, D), F32), jax.ShapeDtypeStruct((D, D), F32),
                   jax.ShapeDtypeStruct((D, D), F32)],
        scratch_shapes=[pltpu.VMEM((A_W, D), F32), pltpu.VMEM((D, D), F32), pltpu.VMEM((D, D), F32)],
        compiler_params=_params(("arbitrary",)),
    )(x, target, *os, *ls, proj, ob, proj, proj, mod3, final_g, wa, wb, wo)


def _piece_parts(pieces):
    parts, where = [], []
    for k, piece in enumerate(pieces):
        off = 0
        for part in piece:
            parts.append(part)
            where.append((k, off, part.shape[1]))
            off += part.shape[1]
        assert off == D
    return parts, where


def _dh(pieces, w_in_g, x, dx2, mod3, norm_g, tm=256):
    S = x.shape[0]
    ni = S // tm
    parts, where = _piece_parts(pieces)
    npart = len(parts)

    def body(*refs):
        p_refs = refs[:npart]
        w_ref, x_ref, dx2_ref, mod_ref, g_ref, gx_ref, sums_ref, w_all = refs[npart:]

        @pl.when(pl.program_id(0) == 0)
        def _():
            sums_ref[...] = jnp.zeros_like(sums_ref)
            pltpu.sync_copy(w_ref, w_all)

        dh = None
        for p_ref, (k, off, width) in zip(p_refs, where):
            term = _mm(p_ref[...], w_all[k, :, off:off + width], NT)
            dh = term if dh is None else dh + term
        xv = x_ref[...]
        g = g_ref[...]
        sc1 = 1.0 + mod_ref[1:2, :]
        r = lax.rsqrt(jnp.mean(xv * xv, axis=-1, keepdims=True) + EPS)
        xn = xv * r
        sums_ref[0:1, :] += jnp.sum(dh, axis=0, keepdims=True)
        sums_ref[1:2, :] += jnp.sum(dh * (xn * g), axis=0, keepdims=True)
        sums_ref[2:3, :] += jnp.sum(dh * sc1 * xn, axis=0, keepdims=True)
        dxn = dh * sc1 * g
        gx_ref[...] = dx2_ref[...] + r * (dxn - xn * jnp.mean(dxn * xn, axis=-1, keepdims=True))

    row = pl.BlockSpec((tm, D), lambda i: (i, 0))
    any_spec = pl.BlockSpec(memory_space=pl.ANY)
    return pl.pallas_call(
        body, name="dh",
        grid=(ni,),
        in_specs=[pl.BlockSpec((tm, width), lambda i: (i, 0)) for _, _, width in where]
                 + [any_spec, row, row,
                    pl.BlockSpec((8, D), lambda i: (0, 0)),
                    pl.BlockSpec((1, D), lambda i: (0, 0))],
        out_specs=[row, pl.BlockSpec((8, D), lambda i: (0, 0))],
        out_shape=[jax.ShapeDtypeStruct((S, D), F32), jax.ShapeDtypeStruct((8, D), F32)],
        scratch_shapes=[pltpu.VMEM(w_in_g.shape, w_in_g.dtype)],
        compiler_params=_params(("arbitrary",)),
    )(*parts, w_in_g, x, dx2, mod3, norm_g)


def _gw_in(ht, pieces, grads, tm=1024):
    S = ht.shape[1]
    nt = S // tm
    ng = len(grads)
    parts, where = _piece_parts(pieces)
    npart = len(parts)

    def round_chunk(r):
        return (4 * lax.axis_index("x") + 2 * lax.axis_index("y") + lax.axis_index("c") + 1 + r) % N_DEV

    def body(*refs):
        h_ref, p_refs = refs[0], refs[1:1 + npart]
        g_ins = refs[1 + npart:1 + npart + ng]
        stage, recv = refs[1 + npart + ng:3 + npart + ng]
        g_outs = refs[3 + npart + ng:3 + npart + 2 * ng]
        acc, xb, send_sems, recv_sems, local_sems, in_send, in_recv, in_local = refs[3 + npart + 2 * ng:]
        r, i = pl.program_id(0), pl.program_id(1)
        start, wait = _all_to_all_copies(g_ins, g_outs, send_sems, recv_sems, local_sems)
        chunk_of, send, keep, wait_in = _rotating_exchange(stage, recv, in_send, in_recv, in_local.at[0])
        k_now = chunk_of(r)

        @pl.when((r == 0) & (i == 0))
        def _():
            start()

        @pl.when(i == 0)
        def _():
            acc[...] = jnp.zeros_like(acc)

        for k in range(N_DEV):
            @pl.when(k_now == k)
            def _(k=k):
                for p_ref, (kk, off, width) in zip(p_refs, where):
                    if kk == k:
                        acc[:, off:off + width] += _mm(h_ref[...], p_ref[...])

        @pl.when(i == nt - 1)
        def _():
            xb[...] = acc[...].astype(XCHG_DTYPE)
            pltpu.sync_copy(xb, stage.at[k_now])

        @pl.when((i == nt - 1) & (r < N_DEV - 1))
        def _():
            send(r)

        @pl.when((i == nt - 1) & (r == N_DEV - 1))
        def _():
            keep()
            wait_in()
            wait()

    def part_spec(k, width):
        return pl.BlockSpec((tm, width), lambda r, i: (jnp.where(round_chunk(r) == k, i, 0), 0))

    any_spec = pl.BlockSpec(memory_space=pl.ANY)
    chunks = jax.ShapeDtypeStruct((N_DEV, D, D), XCHG_DTYPE)
    return pl.pallas_call(
        body, name="gw_in_scatter",
        grid=(N_DEV, nt),
        in_specs=[pl.BlockSpec((D, tm), lambda r, i: (0, i))] + [part_spec(k, width) for k, _, width in where] + [any_spec] * ng,
        out_specs=[any_spec, any_spec] + [any_spec] * ng,
        out_shape=[chunks, chunks] + [jax.ShapeDtypeStruct(g.shape, g.dtype) for g in grads],
        scratch_shapes=[pltpu.VMEM((D, D), F32), pltpu.VMEM((D, D), XCHG_DTYPE),
                        pltpu.SemaphoreType.DMA((ng, N_DEV - 1)), pltpu.SemaphoreType.DMA((ng, N_DEV - 1)),
                        pltpu.SemaphoreType.DMA((ng,)),
                        pltpu.SemaphoreType.DMA((N_DEV - 1,)), pltpu.SemaphoreType.DMA((N_DEV - 1,)),
                        pltpu.SemaphoreType.DMA((1,))],
        compiler_params=_params(("arbitrary", "arbitrary")),
    )(ht, *parts, *grads)[1:]


def _adamw_math(w, g, m, v):
    m = ADAM_B1 * m + (1.0 - ADAM_B1) * g
    v = ADAM_B2 * v + (1.0 - ADAM_B2) * (g * g)
    m_hat = m / (1.0 - ADAM_B1 ** ADAM_STEP)
    v_hat = v / (1.0 - ADAM_B2 ** ADAM_STEP)
    delta = -ADAM_LR * (m_hat / (jnp.sqrt(v_hat) + ADAM_EPS) + ADAM_WD * w)
    return delta, m, v


def _adamw_big(recv, w, m, v, name, tr=128):
    M, N = w.shape
    tr = min(tr, M)

    def body(r_ref, w_ref, m_ref, v_ref, g_ref, d_ref, nm_ref, nv_ref):
        g = r_ref[0].astype(F32)
        for j in range(1, N_DEV):
            g = g + r_ref[j].astype(F32)
        g_ref[...] = g
        d_ref[...], nm_ref[...], nv_ref[...] = _adamw_math(w_ref[...], g, m_ref[...], v_ref[...])

    blk = pl.BlockSpec((tr, N), lambda i: (i, 0))
    return pl.pallas_call(
        body, name=name,
        grid=(M // tr,),
        in_specs=[pl.BlockSpec((N_DEV, tr, N), lambda i: (0, i, 0)), blk, blk, blk],
        out_specs=[blk] * 4,
        out_shape=[jax.ShapeDtypeStruct((M, N), F32)] * 4,
        compiler_params=_params(("parallel",)),
    )(recv, w, m, v)


def _adamw_w_ada(c64, dmod64, w, m, v):
    def body(c_ref, dm_ref, w_ref, m_ref, v_ref, g_ref, d_ref, nm_ref, nv_ref):
        cv = c_ref[...]
        g = _mm(cv * _sigmoid(cv), dm_ref[...], TN)
        g_ref[...] = g
        d_ref[...], nm_ref[...], nv_ref[...] = _adamw_math(w_ref[...], g, m_ref[...], v_ref[...])

    return pl.pallas_call(
        body, name="adamw_w_ada",
        out_shape=[jax.ShapeDtypeStruct(w.shape, F32)] * 4,
        compiler_params=_params(),
    )(c64, dmod64, w, m, v)


P_MOD, P_NORM, P_ONORM, P_RELB, P_LB, P_FINAL, P_LOSS, P_END = (0, 3 * D, 4 * D, 5 * D, 6 * D, 7 * D, 8 * D, 9 * D)


def _adamw_small(packed, b_ada, norm_g, onorm_g, relb, hgrn_lb, final_g, ms, vs):
    def body(pk_ref, b_ref, ng_ref, og_ref, rb_ref, lb_ref, fg_ref,
             mb, mn, mo, mr, ml, mf, vb, vn, vo, vr, vl, vf,
             loss_ref, gb, gn, go, gr, gl, gf, db, dn, do, dr, dl, df,
             nmb, nmn, nmo, nmr, nml, nmf, nvb, nvn, nvo, nvr, nvl, nvf):
        tot = pk_ref[0:1, :]
        for j in range(1, N_DEV):
            tot = tot + pk_ref[8 * j:8 * j + 1, :]
        loss_ref[...] = jnp.broadcast_to(jnp.sum(tot[:, P_LOSS:P_END], axis=-1, keepdims=True), (8, 128))

        def upd(g, w_ref, m_ref, v_ref, g_out, d_out, m_out, v_out):
            g_out[...] = g
            d_out[...], m_out[...], v_out[...] = _adamw_math(w_ref[...], g, m_ref[...], v_ref[...])

        upd(tot[:, P_MOD:P_NORM], b_ref, mb, vb, gb, db, nmb, nvb)
        upd(tot[:, P_NORM:P_ONORM], ng_ref, mn, vn, gn, dn, nmn, nvn)
        g_on = tot[:, P_ONORM:P_ONORM + G_DK]
        for h in range(1, G_HEADS):
            g_on = g_on + tot[:, P_ONORM + G_DK * h:P_ONORM + G_DK * (h + 1)]
        upd(g_on, og_ref, mo, vo, go, do, nmo, nvo)
        upd(tot[:, P_RELB:P_LB], rb_ref, mr, vr, gr, dr, nmr, nvr)
        a = lb_ref[...]
        lb = _sigmoid(a[0:1, :] - a[1:2, :])
        g0 = tot[:, P_LB:P_FINAL] * lb * (1.0 - lb)
        row = lax.broadcasted_iota(jnp.int32, (2, D), 0)
        upd(jnp.where(row == 0, g0, -g0), lb_ref, ml, vl, gl, dl, nml, nvl)
        upd(tot[:, P_FINAL:P_LOSS], fg_ref, mf, vf, gf, df, nmf, nvf)

    shapes = [b_ada.shape, norm_g.shape, onorm_g.shape, relb.shape, hgrn_lb.shape, final_g.shape]
    outs = [jax.ShapeDtypeStruct((8, 128), F32)] + [jax.ShapeDtypeStruct(s, F32) for s in shapes] * 4
    return pl.pallas_call(
        body, name="adamw_small",
        out_shape=outs,
        compiler_params=_params(),
    )(packed, b_ada, norm_g, onorm_g, relb, hgrn_lb, final_g, *ms, *vs)


def _local_step(x, target, mod3, norm_g, w_in_g, onorm_g, wa_blk, wb_blk, wo_blk, rel_bias, hgrn_lb, final_g):
    buckets = jnp.asarray(_bucket_tables())
    bias = _bias_tables(rel_bias, buckets)
    proj, ht, qkv, wa_g, wb_g, wo_g = _inproj(x, mod3, norm_g, w_in_g, [wa_blk, wb_blk, wo_blk])
    wa = wa_g.transpose(1, 0, 2).reshape(A_W, D)
    wb = wb_g.reshape(D, D)
    wo = wo_g.reshape(D, D)
    os, ls = [], []
    for p, (_, d) in enumerate(PATTERNS):
        o, l = _attn_fwd(qkv, bias[p], d, "attn_fwd_d%d" % d)
        os.append(o)
        ls.append(l)
    o_raw, ob, states = _hgrn_fwd(proj, hgrn_lb, onorm_g)
    lt, dx2, do, delta, dza, dob, dga, dgb, tsums, gwa, gwb, gwo = _tail(
        x, target, os, ls, ob, proj, mod3, final_g, wa, wb, wo)
    dbs, acc = [None] * len(PATTERNS), ()
    for p in reversed(range(len(PATTERNS))):
        d = PATTERNS[p][1]
        *acc, dbs[p] = _attn_bwd(qkv, do, lt, delta, bias[p], d, "attn_bwd_d%d" % d, prev=tuple(acc),
                                 out_dtype=MXU_DTYPE if p == 0 else F32)
    dqa, dka, dva = acc
    g_relb = _rel_bias_grad(dbs, buckets)
    dqb, dfb, dib, dzb, dlb, dgo = _hgrn_bwd(proj, o_raw, dob, states, hgrn_lb, onorm_g)
    pieces = [[dqa, dka], [dva, dza], [dqb], [dfb], [dib], [dzb], [dga], [dgb]]
    small = [gwa.astype(XCHG_DTYPE).reshape(A_W, N_DEV, D // N_DEV).transpose(1, 0, 2),
             gwb.astype(XCHG_DTYPE).reshape(N_DEV, D // N_DEV, D),
             gwo.astype(XCHG_DTYPE).reshape(N_DEV, D // N_DEV, D)]
    received = _gw_in(ht, pieces, small)
    gx, hsums = _dh(pieces, w_in_g, x, dx2, mod3, norm_g)
    row = jnp.concatenate([
        hsums[0], hsums[1], tsums[1],
        hsums[2],
        dgo.reshape(G_HEADS, 8, G_DK)[:, 0].reshape(-1),
        g_relb.reshape(-1),
        dlb[0],
        tsums[0],
        tsums[2],
    ])
    return gx, received, row


def kernel(x, c, w_ada, b_ada, norm_g, w_in, hgrn_onorm_g, w_branch_a, w_branch_b, w_out, rel_bias, hgrn_lb, final_g, loss_target, m_w_ada, m_b_ada, m_norm_g, m_w_in, m_hgrn_onorm_g, m_w_branch_a, m_w_branch_b, m_w_out, m_rel_bias, m_hgrn_lb, m_final_g, v_w_ada, v_b_ada, v_norm_g, v_w_in, v_hgrn_onorm_g, v_w_branch_a, v_w_branch_b, v_w_out, v_rel_bias, v_hgrn_lb, v_final_g):
    me = 4 * lax.axis_index("x") + 2 * lax.axis_index("y") + lax.axis_index("c")
    n_ada = w_ada.shape[2]

    w_in_g, c_all = _all_gather([w_in[0].astype(MXU_DTYPE), jnp.broadcast_to(c, (8, D))], "gather_w_in_c")

    c64 = c_all.reshape(8 * N_DEV, D)
    b_loc = lax.dynamic_slice(b_ada, (0, me * n_ada), (1, n_ada))
    mod_part = _mod_fwd(c64, w_ada[0], b_loc)[::8]
    (mod_all,) = _all_gather([mod_part], "gather_mod")
    mod = lax.dynamic_slice(mod_all, (0, me, 0), (N_DEV, 1, n_ada)).reshape(3, D)
    mod3 = jnp.concatenate([mod, jnp.zeros((5, D), F32)], axis=0)

    onorm_t = hgrn_onorm_g
    gx, (r_in, r_a, r_b, r_o), row = _local_step(
        x[0], loss_target[0], mod3, norm_g, w_in_g, onorm_t, w_branch_a[0].astype(MXU_DTYPE),
        w_branch_b[0].astype(MXU_DTYPE), w_out[0].astype(MXU_DTYPE), rel_bias, hgrn_lb,
        final_g.reshape(1, D))
    packed8 = jnp.concatenate([row[None, :], jnp.zeros((7, P_END), F32)], axis=0)
    (packed,) = _all_gather([packed8], "gather_small")
    packed = packed.reshape(8 * N_DEV, P_END)

    g_in, d_in, nm_in, nv_in = _adamw_big(r_in, w_in[0], m_w_in[0], v_w_in[0], "adamw_w_in")
    g_a, d_a, nm_a, nv_a = _adamw_big(r_a, w_branch_a[0], m_w_branch_a[0], v_w_branch_a[0], "adamw_w_branch_a")
    g_b, d_b, nm_b, nv_b = _adamw_big(r_b, w_branch_b[0], m_w_branch_b[0], v_w_branch_b[0], "adamw_w_branch_b")
    g_o, d_o, nm_o, nv_o = _adamw_big(r_o, w_out[0], m_w_out[0], v_w_out[0], "adamw_w_out")

    dmod64 = lax.dynamic_slice(packed, (0, P_MOD + me * n_ada), (8 * N_DEV, n_ada))
    g_ada, d_ada, nm_ada, nv_ada = _adamw_w_ada(c64, dmod64, w_ada[0], m_w_ada[0], v_w_ada[0])

    def flat_relb(t):
        return jnp.pad(t.T, ((0, 0), (0, 128 - N_BUCKETS))).reshape(1, A_HEADS * 128)

    def unflat_relb(t):
        return t.reshape(A_HEADS, 128)[:, :N_BUCKETS].T

    fg2 = lambda t: t.reshape(1, D)
    smalls = _adamw_small(
        packed, b_ada, norm_g, hgrn_onorm_g, flat_relb(rel_bias), hgrn_lb, fg2(final_g),
        [m_b_ada, m_norm_g, m_hgrn_onorm_g, flat_relb(m_rel_bias), m_hgrn_lb, fg2(m_final_g)],
        [v_b_ada, v_norm_g, v_hgrn_onorm_g, flat_relb(v_rel_bias), v_hgrn_lb, fg2(v_final_g)])
    loss = smalls[0][0, 0]

    def small(kind):
        s = smalls[1 + 6 * kind:7 + 6 * kind]
        return s[0], s[1], s[2], unflat_relb(s[3]), s[4], s[5].reshape(D)

    def leaves(ada, sm, w_in_, wa_, wb_, wo_):
        b_, n_, o_, r_, l_, f_ = sm
        return (ada[None], b_, n_, w_in_[None], o_, wa_[None], wb_[None], wo_[None], r_, l_, f_)

    return (loss, gx[None],
            *leaves(g_ada, small(0), g_in, g_a, g_b, g_o),
            *leaves(d_ada, small(1), d_in, d_a, d_b, d_o),
            *leaves(nm_ada, small(2), nm_in, nm_a, nm_b, nm_o),
            *leaves(nv_ada, small(3), nv_in, nv_a, nv_b, nv_o))
```

```python
import functools
import math

import numpy as np
import jax
import jax.numpy as jnp
from jax import lax
from jax.experimental import pallas as pl
from jax.experimental.pallas import tpu as pltpu

F32 = jnp.float32
BF16 = jnp.bfloat16
MXU_DTYPE = jnp.bfloat16
XCHG_DTYPE = jnp.bfloat16

N_DEV = 8
D = 1024
A_HEADS = 8
A_HD = 64
A_W = A_HEADS * A_HD
A_BLK = 128
PATTERNS = ((128, 1), (512, 4), (2048, 16))
N_BUCKETS = 32
MAX_DISTANCE = 2048
NEG = -1e30
G_HEADS = 8
G_DK = 128
G_W = G_HEADS * G_DK
IN_W = 8 * D
EPS = 1e-6
ADAM_LR = 0.001
ADAM_B1 = 0.9
ADAM_B2 = 0.999
ADAM_EPS = 1e-08
ADAM_WD = 0.01
ADAM_STEP = 10

G_CHUNK = 128
G_SUB = 8
G_HPS_FWD = 8
G_HPS_BWD = 8
G_RB = 16
VMEM_LIMIT = 56 * 1024 * 1024

NN = (((1,), (0,)), ((), ()))
NT = (((1,), (1,)), ((), ()))
TN = (((0,), (0,)), ((), ()))
MESH = pl.DeviceIdType.MESH


def _mm(a, b, dims=NN):
    return lax.dot_general(a.astype(MXU_DTYPE), b.astype(MXU_DTYPE), dims,
                           preferred_element_type=F32)


def _mm_exact(t, x):
    hi = x.astype(BF16)
    r = x - hi.astype(F32)
    mid = r.astype(BF16)
    lo = (r - mid.astype(F32)).astype(BF16)
    tb = t.astype(BF16)
    return sum(lax.dot_general(tb, p, NN, preferred_element_type=F32) for p in (hi, mid, lo))


def _split(x):
    hi = x.astype(BF16)
    return hi, (x - hi.astype(F32)).astype(BF16)


def _mm_split(a, b, dims):
    dot = lambda p, q: lax.dot_general(p, q, dims, preferred_element_type=F32)
    return dot(a[0], b[0]) + dot(a[0], b[1]) + dot(a[1], b[0])


def _sigmoid(x):
    return 0.5 * jnp.tanh(0.5 * x) + 0.5


def _params(sem=None):
    return pltpu.CompilerParams(dimension_semantics=sem, vmem_limit_bytes=VMEM_LIMIT)


def _all_gather(xs, name):
    n = len(xs)

    def body(*refs):
        ins, outs = refs[:n], refs[n:2 * n]
        send_sems, recv_sems, local_sems = refs[2 * n:]
        x, y, c = lax.axis_index("x"), lax.axis_index("y"), lax.axis_index("c")
        me, sibling = (x, y, c), (x, y, 1 - c)
        chips = [(1 - x, y), (x, 1 - y), (1 - x, 1 - y)]

        def slot(ref, dev):
            return ref.at[4 * dev[0] + 2 * dev[1] + dev[2]]

        def copy(a, k, block, to, src=None):
            return pltpu.make_async_remote_copy(
                src_ref=slot(outs[a], block) if src is None else src,
                dst_ref=slot(outs[a], block),
                send_sem=send_sems.at[a, k], recv_sem=recv_sems.at[a, k],
                device_id=to, device_id_type=MESH)

        mine, first, passed = [], [], []
        for a in range(n):
            cp = pltpu.make_async_copy(ins[a], slot(outs[a], me), local_sems.at[a])
            cp.start()
            mine.append(cp)
            first.append(copy(a, 0, me, sibling, src=ins[a]))
            for j, chip in enumerate(chips):
                first.append(copy(a, 1 + j, me, (*chip, c), src=ins[a]))
        for cp in first:
            cp.start()
        for j, chip in enumerate(chips):
            for a in range(n):
                copy(a, 1 + j, (*chip, c), me).wait_recv()
                cp = copy(a, 4 + j, (*chip, c), sibling)
                cp.start()
                passed.append(cp)
        for a in range(n):
            copy(a, 0, sibling, me).wait_recv()
            for j, chip in enumerate(chips):
                copy(a, 4 + j, (*chip, 1 - c), me).wait_recv()
        for cp in first + passed:
            cp.wait_send()
        for cp in mine:
            cp.wait()

    any_spec = pl.BlockSpec(memory_space=pl.ANY)
    return pl.pallas_call(
        body, name=name,
        out_shape=[jax.ShapeDtypeStruct((N_DEV,) + v.shape, v.dtype) for v in xs],
        in_specs=[any_spec] * n, out_specs=[any_spec] * n,
        scratch_shapes=[pltpu.SemaphoreType.DMA((n, 7)), pltpu.SemaphoreType.DMA((n, 7)),
                        pltpu.SemaphoreType.DMA((n,))],
    )(*xs)


def _all_to_all_copies(ins, outs, send_sems, recv_sems, local_sems, gather=False):
    n = len(ins)
    x, y, c = lax.axis_index("x"), lax.axis_index("y"), lax.axis_index("c")
    me = 4 * x + 2 * y + c
    peers = []
    for m in range(1, N_DEV):
        peers.append((1 - x if m & 4 else x, 1 - y if m & 2 else y, 1 - c if m & 1 else c))

    def chunk(a, j):
        return ins[a] if gather else ins[a].at[j]

    def copy(a, k, landing):
        peer = peers[k]
        pid = 4 * peer[0] + 2 * peer[1] + peer[2]
        return pltpu.make_async_remote_copy(
            src_ref=chunk(a, pid), dst_ref=outs[a].at[pid if landing else me],
            send_sem=send_sems.at[a, k], recv_sem=recv_sems.at[a, k],
            device_id=peer, device_id_type=MESH)

    def local(a):
        return pltpu.make_async_copy(chunk(a, me), outs[a].at[me], local_sems.at[a])

    def start():
        for a in range(n):
            local(a).start()
        for k in range(N_DEV - 1):
            for a in range(n):
                copy(a, k, False).start()

    def wait():
        for k in range(N_DEV - 1):
            for a in range(n):
                copy(a, k, True).wait_recv()
        for k in range(N_DEV - 1):
            for a in range(n):
                copy(a, k, False).wait_send()
        for a in range(n):
            local(a).wait()

    return start, wait


def _mod_fwd(c64, w_ada, b_loc):
    def body(c_ref, w_ref, b_ref, o_ref):
        cv = c_ref[...]
        sc = cv * _sigmoid(cv)
        o_ref[...] = _mm(sc, w_ref[...]) + b_ref[...]

    return pl.pallas_call(
        body, name="mod_fwd",
        out_shape=jax.ShapeDtypeStruct((c64.shape[0], w_ada.shape[1]), F32),
        compiler_params=_params(),
    )(c64, w_ada, b_loc)


def _inproj(x, mod3, norm_g, w_in_g, blocks, tm=256):
    S = x.shape[0]
    ni = S // tm
    nb = len(blocks)

    def body(*refs):
        x_ref, mod_ref, g_ref, w_ref = refs[:4]
        b_ins = refs[4:4 + nb]
        proj_ref, ht_ref, qkv_ref = refs[4 + nb:7 + nb]
        b_outs = refs[7 + nb:7 + 2 * nb]
        w_all, send_sems, recv_sems, local_sems = refs[7 + 2 * nb:]
        i = pl.program_id(0)
        start, wait = _all_to_all_copies(b_ins, b_outs, send_sems, recv_sems, local_sems, gather=True)

        @pl.when(i == 0)
        def _():
            start()
            pltpu.sync_copy(w_ref, w_all)

        xv = x_ref[...]
        r = lax.rsqrt(jnp.mean(xv * xv, axis=-1, keepdims=True) + EPS)
        h = ((xv * r * g_ref[...]) * (1.0 + mod_ref[1:2, :]) + mod_ref[0:1, :]).astype(MXU_DTYPE)
        ht_ref[...] = h.T
        for j in range(N_DEV):
            pj = _mm(h, w_all[j])
            proj_ref[:, j * D:(j + 1) * D] = pj
            for c in range(3):
                if c // 2 == j:
                    for p in range(A_HEADS // 2):
                        lo = (c % 2) * A_W + 2 * A_HD * p
                        qkv_ref[c, p] = pj[:, lo:lo + 2 * A_HD]

        @pl.when(i == ni - 1)
        def _():
            wait()

    any_spec = pl.BlockSpec(memory_space=pl.ANY)
    return pl.pallas_call(
        body, name="inproj_gather",
        grid=(ni,),
        in_specs=[pl.BlockSpec((tm, D), lambda i: (i, 0)),
                  pl.BlockSpec((8, D), lambda i: (0, 0)),
                  pl.BlockSpec((1, D), lambda i: (0, 0)),
                  any_spec] + [any_spec] * nb,
        out_specs=[pl.BlockSpec((tm, IN_W), lambda i: (i, 0)),
                   pl.BlockSpec((D, tm), lambda i: (0, i)),
                   pl.BlockSpec((3, A_HEADS // 2, tm, 2 * A_HD), lambda i: (0, 0, i, 0))] + [any_spec] * nb,
        out_shape=[jax.ShapeDtypeStruct((S, IN_W), F32), jax.ShapeDtypeStruct((D, S), MXU_DTYPE),
                   jax.ShapeDtypeStruct((3, A_HEADS // 2, S, 2 * A_HD), F32)]
                  + [jax.ShapeDtypeStruct((N_DEV,) + b.shape, b.dtype) for b in blocks],
        scratch_shapes=[pltpu.VMEM(w_in_g.shape, w_in_g.dtype),
                        pltpu.SemaphoreType.DMA((nb, N_DEV - 1)), pltpu.SemaphoreType.DMA((nb, N_DEV - 1)),
                        pltpu.SemaphoreType.DMA((nb,))],
        compiler_params=_params(("arbitrary",)),
    )(x, mod3, norm_g, w_in_g, *blocks)


def _bucket_tables():
    qi = np.arange(A_BLK)[:, None]
    kj = np.arange(2 * A_BLK)[None, :]
    delta = qi + A_BLK - kj
    out = []
    for window, dil in PATTERNS:
        span = window // dil
        band = (delta >= 0) & (delta <= span)
        dist = np.clip(delta, 0, None) * dil
        max_exact = N_BUCKETS // 2
        nf = dist.astype(np.float32)
        large = max_exact + (np.log(np.maximum(nf, np.float32(1.0)) / np.float32(max_exact))
                             / np.float32(math.log(MAX_DISTANCE / max_exact))
                             * np.float32(N_BUCKETS - max_exact)).astype(np.int32)
        large = np.minimum(large, N_BUCKETS - 1)
        bucket = np.where(dist < max_exact, dist, large)
        out.append(np.where(band, bucket, -1).astype(np.int32))
    return np.stack(out)


def _bias_tables(rel_bias, buckets):
    def body(rb_ref, bk_ref, o_ref):
        bk = bk_ref[0]
        for h in range(A_HEADS):
            acc = jnp.full(bk.shape, NEG, F32)
            for b in range(N_BUCKETS):
                acc = jnp.where(bk == b, rb_ref[b, h], acc)
            o_ref[0, h] = acc

    return pl.pallas_call(
        body, name="bias_tables",
        grid=(len(PATTERNS),),
        in_specs=[pl.BlockSpec(memory_space=pltpu.SMEM),
                  pl.BlockSpec((1, A_BLK, 2 * A_BLK), lambda p: (p, 0, 0))],
        out_specs=pl.BlockSpec((1, A_HEADS, A_BLK, 2 * A_BLK), lambda p: (p, 0, 0, 0)),
        out_shape=jax.ShapeDtypeStruct((len(PATTERNS), A_HEADS, A_BLK, 2 * A_BLK), F32),
        compiler_params=_params(("arbitrary",)),
    )(rel_bias, buckets)


A_TILES = 32


def _attn_heads_per_step(d):
    return A_HEADS if d == 1 else 2


def _attn_in_specs(sb, nsb, hw):
    blk = (1, hw // 2, sb, 2 * A_HD)

    def cur(c):
        return pl.BlockSpec(blk, lambda hp, n: (c, hp, jnp.minimum(n, nsb - 1), 0))

    def prev(c):
        return pl.BlockSpec(blk, lambda hp, n: (c, hp, jnp.maximum(jnp.minimum(n, nsb - 1) - 1, 0), 0))

    return [cur(0), prev(1), cur(1), prev(2), cur(2)]


def _rows(r, d):
    return pl.ds(r, A_BLK) if d == 1 else pl.ds(r, A_BLK, stride=d)


def _for_residues(d, hw, fn):
    unroll = min(d, max(1, A_TILES // hw))
    if d == unroll:
        _round_robin([g for r in range(d) for g in fn(r)])
    else:
        def group(g, c):
            _round_robin([t for u in range(unroll) for t in fn(g * unroll + u)])
            return c
        lax.fori_loop(0, d // unroll, group, 0)


def _attn_stack(t):
    first_half = lax.broadcasted_iota(jnp.int32, (1, 2 * A_HD), 1) < A_HD
    return jnp.concatenate([jnp.where(first_half, t, 0.0), jnp.where(first_half, 0.0, t)], axis=0)


def _attn_unstack(t2):
    first_half = lax.broadcasted_iota(jnp.int32, (1, 2 * A_HD), 1) < A_HD
    return jnp.where(first_half, t2[:A_BLK], t2[A_BLK:])


def _attn_scores(q, k, b_ref, pp, first):
    bias = jnp.concatenate([b_ref[2 * pp] + first, b_ref[2 * pp + 1] + first], axis=0)
    return _mm(_attn_stack(q), k, NT) * (A_HD ** -0.5) + bias


def _attn_fwd(qkv, bias_p, d, name):
    S = qkv.shape[2]
    hw = _attn_heads_per_step(d)
    sub = A_BLK * d
    nsub = max(1, A_TILES // (hw * d))
    sb = sub * nsub
    nsb = S // sb

    def body(q_ref, kp_ref, kc_ref, vp_ref, vc_ref, b_ref, o_ref, l_ref):
        n = pl.program_id(1)
        kj = lax.broadcasted_iota(jnp.int32, (A_BLK, 2 * A_BLK), 1)
        first = jnp.where((n == 0) & (kj < A_BLK), NEG, 0.0).astype(F32)
        no_first = jnp.zeros((A_BLK, 2 * A_BLK), F32)

        def residue(r, u=0):
            rows = _rows(u * sub + r, d)
            behind = _rows(((nsub if u == 0 else u) - 1) * sub + r, d)

            def pair(pp):
                lanes = pl.ds(2 * A_HD * pp, 2 * A_HD)
                kc, vc = kc_ref.at[0, pp], vc_ref.at[0, pp]
                kb, vb = (kp_ref.at[0, pp], vp_ref.at[0, pp]) if u == 0 else (kc, vc)
                k = jnp.concatenate([kb[behind, :], kc[rows, :]], axis=0)
                v = jnp.concatenate([vb[behind, :], vc[rows, :]], axis=0)
                s = _attn_scores(q_ref.at[0, pp][rows, :], k, b_ref, pp, first if u == 0 else no_first)
                yield
                m = jnp.max(s, axis=-1, keepdims=True)
                p = jnp.exp(s - m)
                den = jnp.sum(p, axis=-1, keepdims=True)
                pv = _mm(p, v)
                yield
                o_ref[rows, lanes] = _attn_unstack(pv / den)
                l_ref[rows, lanes] = _attn_unstack(jnp.broadcast_to(m + jnp.log(den), (2 * A_BLK, 2 * A_HD)))

            return [pair(pp) for pp in range(hw // 2)]

        if nsub == 1:
            _for_residues(d, hw, residue)
        else:
            _round_robin([g for u in range(nsub) for r in range(d) for g in residue(r, u)])

    out = pl.BlockSpec((sb, A_HD * hw), lambda hp, n: (n, hp))
    return pl.pallas_call(
        body, name=name,
        grid=(A_HEADS // hw, nsb),
        in_specs=_attn_in_specs(sb, nsb, hw) + [pl.BlockSpec((hw, A_BLK, 2 * A_BLK), lambda hp, n: (hp, 0, 0))],
        out_specs=[out, out],
        out_shape=[jax.ShapeDtypeStruct((S, A_W), F32)] * 2,
        compiler_params=_params(("parallel", "parallel")),
    )(qkv, qkv, qkv, qkv, qkv, bias_p)


def _attn_bwd(qkv, do, lt, delta, bias_p, d, name, prev=(), out_dtype=F32):
    S = qkv.shape[2]
    hw = _attn_heads_per_step(d)
    sub = A_BLK * d
    nsub = max(1, A_TILES // (hw * d))
    sb = sub * nsub
    nsb = S // sb
    done = (nsub - 1) * sub

    def body(*refs):
        q_ref, kp_ref, kc_ref, vp_ref, vc_ref, do_ref, lt_ref, dl_ref, b_ref = refs[:9]
        pq_ref, pk_ref, pv_ref = refs[9:9 + len(prev)] if prev else (None, None, None)
        dq_ref, dk_ref, dv_ref, db_ref, ck, cv = refs[9 + len(prev):]
        n = pl.program_id(1)
        plus = lambda t, p_ref, idx: (t if p_ref is None else t + p_ref[idx]).astype(out_dtype)

        @pl.when(n == 0)
        def _():
            db_ref[...] = jnp.zeros_like(db_ref)
            ck[...] = jnp.zeros_like(ck)
            cv[...] = jnp.zeros_like(cv)

        @pl.when(n < nsb)
        def _():
            kj = lax.broadcasted_iota(jnp.int32, (A_BLK, 2 * A_BLK), 1)
            first = jnp.where((n == 0) & (kj < A_BLK), NEG, 0.0).astype(F32)
            no_first = jnp.zeros((A_BLK, 2 * A_BLK), F32)
            if done:
                dk_ref[0:done, :] = plus(ck[0:done, :], pk_ref, (slice(0, done), slice(None)))
                dv_ref[0:done, :] = plus(cv[0:done, :], pv_ref, (slice(0, done), slice(None)))

            def residue(r, u=0):
                rows = _rows(u * sub + r, d)
                behind = _rows(((nsub if u == 0 else u) - 1) * sub + r, d)

                def pair(pp):
                    lanes = pl.ds(2 * A_HD * pp, 2 * A_HD)
                    lt_r, dl_r = lt_ref[rows, lanes], dl_ref[rows, lanes]
                    kc, vc = kc_ref.at[0, pp], vc_ref.at[0, pp]
                    kb, vb = (kp_ref.at[0, pp], vp_ref.at[0, pp]) if u == 0 else (kc, vc)
                    k = jnp.concatenate([kb[behind, :], kc[rows, :]], axis=0)
                    v = jnp.concatenate([vb[behind, :], vc[rows, :]], axis=0)
                    q2 = _attn_stack(q_ref.at[0, pp][rows, :])
                    do2 = _attn_stack(do_ref[rows, lanes])
                    col = lambda t: jnp.concatenate([t[:, 0:1], t[:, A_HD:A_HD + 1]], axis=0)
                    s = _attn_scores(q_ref.at[0, pp][rows, :], k, b_ref, pp, first if u == 0 else no_first)
                    dp = _mm(do2, v, NT)
                    yield
                    p = jnp.exp(s - col(lt_r))
                    ds = p * (dp - col(dl_r))
                    db_ref[2 * pp] += ds[:A_BLK]
                    db_ref[2 * pp + 1] += ds[A_BLK:]
                    dq = _mm(ds, k)
                    dk = _mm(ds, q2, TN) * (A_HD ** -0.5)
                    dv = _mm(p, do2, TN)
                    yield
                    dq_ref[rows, lanes] = plus(_attn_unstack(dq) * (A_HD ** -0.5), pq_ref, (rows, lanes))
                    if u == 0:
                        dk_ref[behind, lanes] = plus(ck[behind, lanes] + dk[:A_BLK], pk_ref, (behind, lanes))
                        dv_ref[behind, lanes] = plus(cv[behind, lanes] + dv[:A_BLK], pv_ref, (behind, lanes))
                    else:
                        ck[behind, lanes] += dk[:A_BLK]
                        cv[behind, lanes] += dv[:A_BLK]
                    ck[rows, lanes] = dk[A_BLK:]
                    cv[rows, lanes] = dv[A_BLK:]

                return [pair(pp) for pp in range(hw // 2)]

            if nsub == 1:
                _for_residues(d, hw, residue)
            else:
                _round_robin([g for u in range(nsub) for r in range(d) for g in residue(r, u)])

        @pl.when(n == nsb)
        def _():
            dk_ref[...] = plus(ck[...], pk_ref, ...)
            dv_ref[...] = plus(cv[...], pv_ref, ...)

    w = A_HD * hw
    row = pl.BlockSpec((sb, w), lambda hp, n: (jnp.minimum(n, nsb - 1), hp))
    lag = pl.BlockSpec((sb, w), lambda hp, n: (jnp.maximum(n - 1, 0), hp))
    tab = pl.BlockSpec((hw, A_BLK, 2 * A_BLK), lambda hp, n: (hp, 0, 0))
    return pl.pallas_call(
        body, name=name,
        grid=(A_HEADS // hw, nsb + 1),
        in_specs=_attn_in_specs(sb, nsb, hw) + [row, row, row, tab] + ([row, lag, lag] if prev else []),
        out_specs=[row, lag, lag, tab],
        out_shape=[jax.ShapeDtypeStruct((S, A_W), out_dtype)] * 3
                  + [jax.ShapeDtypeStruct((A_HEADS, A_BLK, 2 * A_BLK), F32)],
        scratch_shapes=[pltpu.VMEM((sb, w), F32), pltpu.VMEM((sb, w), F32)],
        compiler_params=_params(("parallel", "arbitrary")),
    )(qkv, qkv, qkv, qkv, qkv, do, lt, delta, bias_p, *prev)


def _rel_bias_grad(dbs, buckets):
    def body(d1, d2, d3, bk_ref, o_ref):
        row = lax.broadcasted_iota(jnp.int32, (A_HEADS, 128), 0)
        lane = lax.broadcasted_iota(jnp.int32, (A_HEADS, 128), 1)
        acc = jnp.zeros((A_HEADS, 128), F32)
        for p, dref in enumerate((d1, d2, d3)):
            bk = bk_ref[p]
            for h in range(A_HEADS):
                ds = dref[h]
                for b in range(N_BUCKETS):
                    s = jnp.sum(jnp.where(bk == b, ds, 0.0), keepdims=True)
                    acc = acc + jnp.where((row == h) & (lane == b), s, 0.0)
        o_ref[...] = acc

    return pl.pallas_call(
        body, name="rel_bias_grad",
        out_shape=jax.ShapeDtypeStruct((A_HEADS, 128), F32),
        compiler_params=_params(),
    )(*dbs, buckets)


def _tri(c):
    t = np.tril(np.ones((c, c), np.float32))
    return jnp.asarray(t), jnp.asarray(t.T.copy())


def _fill_above(ref, x, pad):
    ref[0:G_SUB, :] = jnp.full((G_SUB, x.shape[1]), pad, F32)
    ref[G_SUB:, :] = x


def _fill_below(ref, x, pad):
    ref[0:x.shape[0], :] = x
    ref[x.shape[0]:, :] = jnp.full((G_SUB, x.shape[1]), pad, F32)


def _hgrn_gates(q_ref, f_ref, lbp_ref, tri_ref):
    qraw = q_ref[...]
    sq = _sigmoid(qraw)
    q = qraw * sq
    sg = _sigmoid(f_ref[...])
    lb = _sigmoid(lbp_ref[0:1, :] - lbp_ref[1:2, :])
    f = lb + (1.0 - lb) * sg
    k = 1.0 - f
    b = _mm_exact(tri_ref[...], jnp.log(f))
    return qraw, sq, q, sg, lb, f, k, b


def _hgrn_col(C, base, idx, hps):
    return pl.BlockSpec((C, hps * G_DK), lambda h, n: (idx(n), base * (G_HEADS // hps) + h))


def _round_robin(stages):
    live = list(stages)
    while live:
        nxt = []
        for g in live:
            try:
                next(g)
                nxt.append(g)
            except StopIteration:
                pass
        live = nxt


def _hgrn_levels(C):
    out, m = [], G_SUB
    while 2 * m <= C:
        out.append(m)
        m *= 2
    return out


def _hgrn_level_masks(C):
    ti = np.arange(C)[:, None]
    si = np.arange(C)[None, :]
    return jnp.asarray(np.stack([((ti // (2 * m) == si // (2 * m)) & (ti - si >= G_SUB)).astype(np.float32)
                                 for m in _hgrn_levels(C)]))


def _hgrn_level(b, q, k, C, m):
    zeros = jnp.zeros((m, G_DK), F32)
    eq, ek, qt, kt = [], [], [], []
    for blk in range(0, C // m, 2):
        lo, mid, hi = blk * m, (blk + 1) * m, (blk + 2) * m
        ref = b[mid:mid + 1]
        e_right = jnp.exp(b[mid:hi] - ref)
        e_left = jnp.exp(ref - b[lo:mid])
        eq += [zeros, e_right]
        ek += [e_left, zeros]
        qt += [zeros, q[mid:hi] * e_right]
        kt += [k[lo:mid] * e_left, zeros]
    cat = lambda parts: jnp.concatenate(parts, axis=0)
    return cat(qt), cat(kt), cat(eq), cat(ek)


def _hgrn_fwd(proj, hgrn_lb, onorm_g, C=G_CHUNK):
    S = proj.shape[0]
    nc = S // C
    tri, _ = _tri(C)
    masks = _hgrn_level_masks(C)
    hps = G_HPS_FWD

    def body(q_ref, f_ref, i_ref, z_ref, lbp_ref, go_ref, tri_ref, pm_ref, o_ref, ob_ref, st_ref, St, kp, vp, fp):
        @pl.when(pl.program_id(1) == 0)
        def _():
            St[...] = jnp.zeros_like(St)

        heads = []
        for hh in range(hps):
            ln = pl.ds(G_DK * hh, G_DK)
            heads.append(head(
                q_ref.at[:, ln], f_ref.at[:, ln], i_ref.at[:, ln], z_ref.at[:, ln], lbp_ref.at[:, ln], go_ref,
                tri_ref, pm_ref, o_ref.at[:, ln], ob_ref.at[:, ln], st_ref.at[0, hh], St.at[hh], kp.at[hh], vp.at[hh],
                fp.at[hh]))
        _round_robin(heads)

    def head(q_ref, f_ref, i_ref, z_ref, lbp_ref, go_ref, tri_ref, pm_ref, o_ref, ob_ref, st_ref, St, kp, vp, fp):
        _, _, q, _, _, f, k, b = _hgrn_gates(q_ref, f_ref, lbp_ref, tri_ref)
        v = i_ref[...]
        bC = b[C - 1:C, :]
        S0 = St[...]
        o = _mm(q * jnp.exp(b), S0, NT)
        yield
        _fill_above(kp, k, 0.0)
        _fill_above(vp, v, 0.0)
        _fill_above(fp, f, 1.0)
        near = []
        for r0 in range(0, C, G_RB):
            qb = q[r0:r0 + G_RB]
            acc = e = None
            for l in range(G_SUB):
                rows = pl.ds(G_SUB - l + r0, G_RB)
                if l > 0:
                    fl = fp[pl.ds(G_SUB - l + 1 + r0, G_RB), :]
                    e = fl if e is None else e * fl
                kl = kp[rows, :]
                a = jnp.sum(qb * kl if e is None else qb * kl * e, axis=-1, keepdims=True)
                t = a * vp[rows, :]
                acc = t if acc is None else acc + t
            near.append(acc)
        o = o + jnp.concatenate(near, axis=0)
        yield
        a_off = jnp.zeros((C, C), F32)
        for lv, m in enumerate(_hgrn_levels(C)):
            qt, kt, _, _ = _hgrn_level(b, q, k, C, m)
            prod = _mm_split(_split(qt), _split(kt), NT) if m == G_SUB else _mm(qt, kt, NT)
            a_off = a_off + pm_ref[lv] * prod
        yield
        o = o + _mm(a_off, v)
        S1 = S0 * jnp.exp(bC) + _mm(v, k * jnp.exp(bC - b), TN)
        St[...] = S1
        st_ref[...] = S1
        o_ref[...] = o
        r = lax.rsqrt(jnp.mean(o * o, axis=-1, keepdims=True) + EPS)
        z = z_ref[...]
        ob_ref[...] = (o * r * go_ref[...] * (z * _sigmoid(z))).astype(MXU_DTYPE)

    ident = lambda n: n
    w = hps * G_DK
    out = pl.BlockSpec((C, w), lambda h, n: (n, h))
    return pl.pallas_call(
        body, name="hgrn_fwd",
        grid=(G_HEADS // hps, nc),
        in_specs=[_hgrn_col(C, base, ident, hps) for base in (2, 3, 4, 5)] + [
                  pl.BlockSpec((2, w), lambda h, n: (0, h)),
                  pl.BlockSpec((1, G_DK), lambda h, n: (0, 0)),
                  pl.BlockSpec((C, C), lambda h, n: (0, 0)),
                  pl.BlockSpec(masks.shape, lambda h, n: (0, 0, 0))],
        out_specs=[out, out, pl.BlockSpec((1, hps, G_DK, G_DK), lambda h, n: (n, h, 0, 0))],
        out_shape=[jax.ShapeDtypeStruct((S, G_W), F32), jax.ShapeDtypeStruct((S, G_W), MXU_DTYPE),
                   jax.ShapeDtypeStruct((nc, G_HEADS, G_DK, G_DK), F32)],
        scratch_shapes=[pltpu.VMEM((hps, G_DK, G_DK), F32)] + [pltpu.VMEM((hps, C + G_SUB, G_DK), F32)] * 3,
        compiler_params=_params(("parallel", "arbitrary")),
    )(proj, proj, proj, proj, hgrn_lb, onorm_g, tri, masks)


def _hgrn_bwd(proj, o_raw, dob, states, hgrn_lb, onorm_g, C=G_CHUNK):
    S = proj.shape[0]
    nc = S // C
    tri, triu = _tri(C)
    masks = _hgrn_level_masks(C)
    hps = G_HPS_BWD

    def body(q_ref, f_ref, i_ref, z_ref, o_ref, dob_ref, s0_ref, s1_ref, lbp_ref, go_ref, tri_ref, triu_ref,
             pm_ref, dq_ref, df_ref, di_ref, dz_ref, dlb_ref, dgo_ref, dSt, *shifted):
        @pl.when(pl.program_id(1) == 0)
        def _():
            dSt[...] = jnp.zeros_like(dSt)
            dlb_ref[...] = jnp.zeros_like(dlb_ref)
            dgo_ref[...] = jnp.zeros_like(dgo_ref)

        heads = []
        for hh in range(hps):
            ln = pl.ds(G_DK * hh, G_DK)
            heads.append(head(
                q_ref.at[:, ln], f_ref.at[:, ln], i_ref.at[:, ln], z_ref.at[:, ln], o_ref.at[:, ln],
                dob_ref.at[:, ln], s0_ref.at[0, hh], s1_ref.at[0, hh], lbp_ref.at[:, ln], go_ref, tri_ref, triu_ref,
                pm_ref, dq_ref.at[:, ln], df_ref.at[:, ln], di_ref.at[:, ln], dz_ref.at[:, ln], dlb_ref.at[:, ln],
                dgo_ref.at[pl.ds(8 * hh, 8), :], dSt.at[hh], *[t.at[hh] for t in shifted]))
        _round_robin(heads)

    def head(q_ref, f_ref, i_ref, z_ref, o_ref, dob_ref, s0_ref, s1_ref, lbp_ref, go_ref, tri_ref, triu_ref,
             pm_ref, dq_ref, df_ref, di_ref, dz_ref, dlb_ref, dgo_ref, dSt, kp, vp, fp, qn, dn_, fn, xs, dac):
        cn = nc - 1 - pl.program_id(1)
        qraw, sq, q, sg, lb, f, k, b = _hgrn_gates(q_ref, f_ref, lbp_ref, tri_ref)
        v = i_ref[...]
        bC = b[C - 1:C, :]
        eb = jnp.exp(b)
        ecb = jnp.exp(bC - b)
        o = o_ref[...]
        z = z_ref[...]
        sz = _sigmoid(z)
        go = go_ref[...]
        g_ob = dob_ref[...]
        r = lax.rsqrt(jnp.mean(o * o, axis=-1, keepdims=True) + EPS)
        nh = o * r
        dnrm = g_ob * (z * sz)
        dz_ref[...] = (g_ob * (nh * go) * (sz * (1.0 + z * (1.0 - sz)))).astype(MXU_DTYPE)
        dgo_ref[0:1, :] += jnp.sum(dnrm * nh, axis=0, keepdims=True)
        dn = dnrm * go
        do = r * (dn - nh * jnp.mean(dn * nh, axis=-1, keepdims=True))

        yield
        S0 = jnp.where(cn == 0, 0.0, s0_ref[...])
        S1 = s1_ref[...]
        dS1 = dSt[...]
        dq = eb * _mm(do, S0)
        dk = ecb * _mm(v, dS1)
        dv = _mm(k * ecb, dS1, NT)
        bnd = jnp.sum(dS1 * S1, axis=0, keepdims=True)
        dSt[...] = dS1 * jnp.exp(bC) + _mm(do, q * eb, TN)

        _fill_above(kp, k, 0.0)
        _fill_above(vp, v, 0.0)
        _fill_above(fp, f, 1.0)
        _fill_below(qn, q, 0.0)
        _fill_below(dn_, do, 0.0)
        _fill_below(fn, f, 1.0)
        yield
        for r0 in range(0, C, G_RB):
            do_b = do[r0:r0 + G_RB]
            for l in range(G_SUB):
                xs[pl.ds(l * C + r0, G_RB), :] = (do_b * vp[pl.ds(G_SUB - l + r0, G_RB), :]).astype(MXU_DTYPE)
        dac[0:G_SUB * C, :] = _mm(xs[...], jnp.ones((G_DK, G_DK), MXU_DTYPE))
        dac[G_SUB * C:, :] = jnp.zeros((G_SUB, G_DK), F32)
        yield
        near_q, near_k, near_v = [], [], []
        for r0 in range(0, C, G_RB):
            k_b = k[r0:r0 + G_RB]
            aq = ak = av = e = e2 = None
            for l in range(G_SUB):
                down, up = pl.ds(G_SUB - l + r0, G_RB), pl.ds(l + r0, G_RB)
                if l > 0:
                    fl = fp[pl.ds(G_SUB - l + 1 + r0, G_RB), :]
                    e = fl if e is None else e * fl
                    fu = fn[up, :]
                    e2 = fu if e2 is None else e2 * fu
                kl = kp[down, :]
                t = dac[pl.ds(l * C + r0, G_RB), :] * (kl if e is None else kl * e)
                aq = t if aq is None else aq + t
                qu = qn[up, :]
                qe = qu if e2 is None else qu * e2
                dou = dn_[up, :]
                a2 = jnp.sum(qe * k_b, axis=-1, keepdims=True)
                t = dac[pl.ds(l * C + l + r0, G_RB), :] * qe
                ak = t if ak is None else ak + t
                t = a2 * dou
                av = t if av is None else av + t
            near_q.append(aq)
            near_k.append(ak)
            near_v.append(av)
        dq = dq + jnp.concatenate(near_q, axis=0)
        dk = dk + jnp.concatenate(near_k, axis=0)
        dv = dv + jnp.concatenate(near_v, axis=0)

        yield
        da_all = _mm(do, v, NT)
        a_off = jnp.zeros((C, C), F32)
        for lv, m in enumerate(_hgrn_levels(C)):
            qt, kt, eq, ek = _hgrn_level(b, q, k, C, m)
            da_m = pm_ref[lv] * da_all
            if m == G_SUB:
                qs, ks, das = _split(qt), _split(kt), _split(da_m)
                a_off = a_off + pm_ref[lv] * _mm_split(qs, ks, NT)
                dq = dq + _mm_split(das, ks, NN) * eq
                dk = dk + _mm_split(das, qs, TN) * ek
            else:
                a_off = a_off + pm_ref[lv] * _mm(qt, kt, NT)
                dq = dq + _mm(da_m, kt) * eq
                dk = dk + _mm(da_m, qt, TN) * ek
        dv = dv + _mm(a_off, do, TN)

        yield
        row = lax.broadcasted_iota(jnp.int32, (C, 1), 0)
        db = q * dq - k * dk + jnp.where(row == C - 1, bnd, 0.0)
        dg = _mm_exact(triu_ref[...], db)
        df = dg / f - dk
        df_ref[...] = (df * (1.0 - lb) * (sg * (1.0 - sg))).astype(MXU_DTYPE)
        dlb_ref[0:1, :] += jnp.sum(df * (1.0 - sg), axis=0, keepdims=True)
        dq_ref[...] = (dq * (sq * (1.0 + qraw * (1.0 - sq)))).astype(MXU_DTYPE)
        di_ref[...] = dv.astype(MXU_DTYPE)

    rev = lambda n: nc - 1 - n
    w = hps * G_DK
    blk = pl.BlockSpec((C, w), lambda h, n: (nc - 1 - n, h))
    return pl.pallas_call(
        body, name="hgrn_bwd",
        grid=(G_HEADS // hps, nc),
        in_specs=[_hgrn_col(C, base, rev, hps) for base in (2, 3, 4, 5)] + [
                  blk, blk,
                  pl.BlockSpec((1, hps, G_DK, G_DK), lambda h, n: (jnp.maximum(nc - 2 - n, 0), h, 0, 0)),
                  pl.BlockSpec((1, hps, G_DK, G_DK), lambda h, n: (nc - 1 - n, h, 0, 0)),
                  pl.BlockSpec((2, w), lambda h, n: (0, h)),
                  pl.BlockSpec((1, G_DK), lambda h, n: (0, 0)),
                  pl.BlockSpec((C, C), lambda h, n: (0, 0)),
                  pl.BlockSpec((C, C), lambda h, n: (0, 0)),
                  pl.BlockSpec(masks.shape, lambda h, n: (0, 0, 0))],
        out_specs=[blk, blk, blk, blk,
                   pl.BlockSpec((8, w), lambda h, n: (0, h)),
                   pl.BlockSpec((8 * hps, G_DK), lambda h, n: (h, 0))],
        out_shape=[jax.ShapeDtypeStruct((S, G_W), MXU_DTYPE)] * 4
                  + [jax.ShapeDtypeStruct((8, G_W), F32), jax.ShapeDtypeStruct((8 * G_HEADS, G_DK), F32)],
        scratch_shapes=[pltpu.VMEM((hps, G_DK, G_DK), F32)] + [pltpu.VMEM((hps, C + G_SUB, G_DK), F32)] * 6
                       + [pltpu.VMEM((hps, G_SUB * C, G_DK), MXU_DTYPE),
                          pltpu.VMEM((hps, G_SUB * C + G_SUB, G_DK), F32)],
        compiler_params=_params(("parallel", "arbitrary")),
    )(proj, proj, proj, proj, o_raw, dob, states, states, hgrn_lb, onorm_g, tri, triu, masks)


def _tail(x, target, os, ls, ob, proj, mod3, final_g, wa, wb, wo, tm=256):
    S = x.shape[0]
    nt = S // tm

    def body(x_ref, t_ref, o1, o2, o3, l1, l2, l3, za_ref, ob_ref, ga_ref, gb_ref, mod_ref, fg_ref,
             wa_ref, wb_ref, wo_ref,
             lt_ref, dx2_ref, do_ref, dl_ref, dza_ref, dob_ref, dga_ref, dgb_ref, sums_ref,
             gwa_ref, gwb_ref, gwo_ref, acc_a, acc_b, acc_o):
        i = pl.program_id(0)

        @pl.when(i == 0)
        def _():
            sums_ref[...] = jnp.zeros_like(sums_ref)
            acc_a[...] = jnp.zeros_like(acc_a)
            acc_b[...] = jnp.zeros_like(acc_b)
            acc_o[...] = jnp.zeros_like(acc_o)

        a1, a2, a3 = l1[...], l2[...], l3[...]
        lm = jnp.maximum(jnp.maximum(a1, a2), a3)
        e1, e2, e3 = jnp.exp(a1 - lm), jnp.exp(a2 - lm), jnp.exp(a3 - lm)
        lden = e1 + e2 + e3
        ao = (e1 * o1[...] + e2 * o2[...] + e3 * o3[...]) / lden
        lt_ref[...] = lm + jnp.log(lden)
        za = za_ref[...]
        sza = _sigmoid(za)
        oa_v, ob_v = (ao * (za * sza)).astype(MXU_DTYPE), ob_ref[...]
        pa = _mm(oa_v, wa_ref[...])
        pb = _mm(ob_v, wb_ref[...])
        sa, sb = _sigmoid(ga_ref[...]), _sigmoid(gb_ref[...])
        ym = sa * pa + sb * pb
        u = _mm(ym, wo_ref[...])
        gate = mod_ref[2:3, :]
        fg = fg_ref[...]
        x2 = x_ref[...] + gate * u
        r2 = lax.rsqrt(jnp.mean(x2 * x2, axis=-1, keepdims=True) + EPS)
        xn2 = x2 * r2
        e = xn2 * fg - t_ref[...]
        dy = e * (1.0 / D)
        dn = dy * fg
        dx2 = r2 * (dn - xn2 * jnp.mean(dn * xn2, axis=-1, keepdims=True))
        dx2_ref[...] = dx2
        sums_ref[0:1, :] += jnp.sum(dy * xn2, axis=0, keepdims=True)
        sums_ref[1:2, :] += jnp.sum(dx2 * u, axis=0, keepdims=True)
        sums_ref[2:3, :] += (0.5 / D) * jnp.sum(e * e, axis=0, keepdims=True)
        du = dx2 * gate
        dym = _mm(du, wo_ref[...], NT)
        acc_o[...] += _mm(ym, du, TN)
        dpa, dpb = dym * sa, dym * sb
        dga_ref[...] = (dym * pa * (sa * (1.0 - sa))).astype(MXU_DTYPE)
        dgb_ref[...] = (dym * pb * (sb * (1.0 - sb))).astype(MXU_DTYPE)
        doa = _mm(dpa, wa_ref[...], NT)
        dza_ref[...] = (doa * ao * (sza * (1.0 + za * (1.0 - sza)))).astype(MXU_DTYPE)
        do = doa * (za * sza)
        do_ref[...] = do
        prod = do * ao
        for h in range(A_HEADS):
            sl = slice(A_HD * h, A_HD * (h + 1))
            dl_ref[:, sl] = jnp.broadcast_to(jnp.sum(prod[:, sl], axis=-1, keepdims=True), (tm, A_HD))
        dob_ref[...] = _mm(dpb, wb_ref[...], NT)
        acc_a[...] += _mm(oa_v, dpa, TN)
        acc_b[...] += _mm(ob_v, dpb, TN)

        @pl.when(i == nt - 1)
        def _():
            pltpu.sync_copy(acc_a, gwa_ref)
            pltpu.sync_copy(acc_b, gwb_ref)
            pltpu.sync_copy(acc_o, gwo_ref)

    row = lambda w: pl.BlockSpec((tm, w), lambda i: (i, 0))
    full = lambda a, b: pl.BlockSpec((a, b), lambda i: (0, 0))
    any_spec = pl.BlockSpec(memory_space=pl.ANY)
    return pl.pallas_call(
        body, name="tail",
        grid=(nt,),
        in_specs=[row(D), row(D)] + [row(A_W)] * 6 + [pl.BlockSpec((tm, A_W), lambda i: (i, 3)), row(D),
                  pl.BlockSpec((tm, D), lambda i: (i, 6)), pl.BlockSpec((tm, D), lambda i: (i, 7)),
                  full(8, D), full(1, D), full(A_W, D), full(D, D), full(D, D)],
        out_specs=[row(A_W), row(D), row(A_W), row(A_W), row(A_W), row(D), row(D), row(D), full(8, D),
                   any_spec, any_spec, any_spec],
        out_shape=[jax.ShapeDtypeStruct((S, A_W), F32),
                   jax.ShapeDtypeStruct((S, D), F32), jax.ShapeDtypeStruct((S, A_W), F32),
                   jax.ShapeDtypeStruct((S, A_W), F32), jax.ShapeDtypeStruct((S, A_W), MXU_DTYPE),
                   jax.ShapeDtypeStruct((S, D), F32), jax.ShapeDtypeStruct((S, D), MXU_DTYPE),
                   jax.ShapeDtypeStruct((S, D), MXU_DTYPE), jax.ShapeDtypeStruct((8, D), F32),
                   jax.ShapeDtypeStruct((A_W, D), F32), jax.ShapeDtypeStruct((D, D), F32),
                   jax.ShapeDtypeStruct((D, D), F32)],
        scratch_shapes=[pltpu.VMEM((A_W, D), F32), pltpu.VMEM((D, D), F32), pltpu.VMEM((D, D), F32)],
        compiler_params=_params(("arbitrary",)),
    )(x, target, *os, *ls, proj, ob, proj, proj, mod3, final_g, wa, wb, wo)


def _piece_parts(pieces):
    parts, where = [], []
    for k, piece in enumerate(pieces):
        off = 0
        for part in piece:
            parts.append(part)
            where.append((k, off, part.shape[1]))
            off += part.shape[1]
        assert off == D
    return parts, where


def _dh(pieces, w_in_g, x, dx2, mod3, norm_g, grads, tm=512):
    S = x.shape[0]
    ni = S // tm
    ng = len(grads)
    parts, where = _piece_parts(pieces)
    npart = len(parts)

    def body(*refs):
        p_refs = refs[:npart]
        w_ref, x_ref, dx2_ref, mod_ref, g_ref = refs[npart:npart + 5]
        g_ins = refs[npart + 5:npart + 5 + ng]
        gx_ref, sums_ref = refs[npart + 5 + ng:npart + 7 + ng]
        g_outs = refs[npart + 7 + ng:npart + 7 + 2 * ng]
        w_all, send_sems, recv_sems, local_sems = refs[npart + 7 + 2 * ng:]
        i = pl.program_id(0)
        start, wait = _all_to_all_copies(g_ins, g_outs, send_sems, recv_sems, local_sems)

        @pl.when(i == 0)
        def _():
            start()
            sums_ref[...] = jnp.zeros_like(sums_ref)
            pltpu.sync_copy(w_ref, w_all)

        dh = None
        for p_ref, (k, off, width) in zip(p_refs, where):
            term = _mm(p_ref[...], w_all[k, :, off:off + width], NT)
            dh = term if dh is None else dh + term
        xv = x_ref[...]
        g = g_ref[...]
        sc1 = 1.0 + mod_ref[1:2, :]
        r = lax.rsqrt(jnp.mean(xv * xv, axis=-1, keepdims=True) + EPS)
        xn = xv * r
        sums_ref[0:1, :] += jnp.sum(dh, axis=0, keepdims=True)
        sums_ref[1:2, :] += jnp.sum(dh * (xn * g), axis=0, keepdims=True)
        sums_ref[2:3, :] += jnp.sum(dh * sc1 * xn, axis=0, keepdims=True)
        dxn = dh * sc1 * g
        gx_ref[...] = dx2_ref[...] + r * (dxn - xn * jnp.mean(dxn * xn, axis=-1, keepdims=True))

        @pl.when(i == ni - 1)
        def _():
            wait()

    row = pl.BlockSpec((tm, D), lambda i: (i, 0))
    any_spec = pl.BlockSpec(memory_space=pl.ANY)
    return pl.pallas_call(
        body, name="dh_scatter",
        grid=(ni,),
        in_specs=[pl.BlockSpec((tm, width), lambda i: (i, 0)) for _, _, width in where]
                 + [any_spec, row, row,
                    pl.BlockSpec((8, D), lambda i: (0, 0)),
                    pl.BlockSpec((1, D), lambda i: (0, 0))]
                 + [any_spec] * ng,
        out_specs=[row, pl.BlockSpec((8, D), lambda i: (0, 0))] + [any_spec] * ng,
        out_shape=[jax.ShapeDtypeStruct((S, D), F32), jax.ShapeDtypeStruct((8, D), F32)]
                  + [jax.ShapeDtypeStruct(g.shape, g.dtype) for g in grads],
        scratch_shapes=[pltpu.VMEM(w_in_g.shape, w_in_g.dtype),
                        pltpu.SemaphoreType.DMA((ng, N_DEV - 1)), pltpu.SemaphoreType.DMA((ng, N_DEV - 1)),
                        pltpu.SemaphoreType.DMA((ng,))],
        compiler_params=_params(("arbitrary",)),
    )(*parts, w_in_g, x, dx2, mod3, norm_g, *grads)


def _gw_in(ht, pieces, grads, tm=1024):
    S = ht.shape[1]
    nt = S // tm
    ng = len(grads)
    parts, where = _piece_parts(pieces)
    npart = len(parts)

    def body(*refs):
        h_ref, p_refs = refs[0], refs[1:1 + npart]
        g_ins = refs[1 + npart:1 + npart + ng]
        o_ref = refs[1 + npart + ng]
        g_outs = refs[2 + npart + ng:2 + npart + 2 * ng]
        acc, send_sems, recv_sems, local_sems = refs[2 + npart + 2 * ng:]
        j, i = pl.program_id(0), pl.program_id(1)
        start, wait = _all_to_all_copies(g_ins, g_outs, send_sems, recv_sems, local_sems)

        @pl.when((j == 0) & (i == 0))
        def _():
            start()

        @pl.when(i == 0)
        def _():
            acc[...] = jnp.zeros_like(acc)

        for k in range(N_DEV):
            @pl.when(j == k)
            def _(k=k):
                for p_ref, (kk, off, width) in zip(p_refs, where):
                    if kk == k:
                        acc[:, off:off + width] += _mm(h_ref[...], p_ref[...])

        @pl.when(i == nt - 1)
        def _():
            o_ref[0] = acc[...].astype(XCHG_DTYPE)

        @pl.when((j == N_DEV - 1) & (i == nt - 1))
        def _():
            wait()

    def part_spec(k, width):
        return pl.BlockSpec((tm, width), lambda j, i: (jnp.where(j == k, i, 0), 0))

    any_spec = pl.BlockSpec(memory_space=pl.ANY)
    return pl.pallas_call(
        body, name="gw_in_scatter",
        grid=(N_DEV, nt),
        in_specs=[pl.BlockSpec((D, tm), lambda j, i: (0, i))] + [part_spec(k, width) for k, _, width in where] + [any_spec] * ng,
        out_specs=[pl.BlockSpec((1, D, D), lambda j, i: (j, 0, 0))] + [any_spec] * ng,
        out_shape=[jax.ShapeDtypeStruct((N_DEV, D, D), XCHG_DTYPE)]
                  + [jax.ShapeDtypeStruct(g.shape, g.dtype) for g in grads],
        scratch_shapes=[pltpu.VMEM((D, D), F32),
                        pltpu.SemaphoreType.DMA((ng, N_DEV - 1)), pltpu.SemaphoreType.DMA((ng, N_DEV - 1)),
                        pltpu.SemaphoreType.DMA((ng,))],
        compiler_params=_params(("arbitrary", "arbitrary")),
    )(ht, *parts, *grads)


def _adamw_math(w, g, m, v):
    m = ADAM_B1 * m + (1.0 - ADAM_B1) * g
    v = ADAM_B2 * v + (1.0 - ADAM_B2) * (g * g)
    m_hat = m / (1.0 - ADAM_B1 ** ADAM_STEP)
    v_hat = v / (1.0 - ADAM_B2 ** ADAM_STEP)
    delta = -ADAM_LR * (m_hat / (jnp.sqrt(v_hat) + ADAM_EPS) + ADAM_WD * w)
    return delta, m, v


def _adamw_big(recv, w, m, v, name, tr=128):
    M, N = w.shape
    tr = min(tr, M)

    def body(r_ref, w_ref, m_ref, v_ref, g_ref, d_ref, nm_ref, nv_ref):
        g = r_ref[0].astype(F32)
        for j in range(1, N_DEV):
            g = g + r_ref[j].astype(F32)
        g_ref[...] = g
        d_ref[...], nm_ref[...], nv_ref[...] = _adamw_math(w_ref[...], g, m_ref[...], v_ref[...])

    blk = pl.BlockSpec((tr, N), lambda i: (i, 0))
    return pl.pallas_call(
        body, name=name,
        grid=(M // tr,),
        in_specs=[pl.BlockSpec((N_DEV, tr, N), lambda i: (0, i, 0)), blk, blk, blk],
        out_specs=[blk] * 4,
        out_shape=[jax.ShapeDtypeStruct((M, N), F32)] * 4,
        compiler_params=_params(("parallel",)),
    )(recv, w, m, v)


def _adamw_w_ada(c64, dmod64, w, m, v):
    def body(c_ref, dm_ref, w_ref, m_ref, v_ref, g_ref, d_ref, nm_ref, nv_ref):
        cv = c_ref[...]
        g = _mm(cv * _sigmoid(cv), dm_ref[...], TN)
        g_ref[...] = g
        d_ref[...], nm_ref[...], nv_ref[...] = _adamw_math(w_ref[...], g, m_ref[...], v_ref[...])

    return pl.pallas_call(
        body, name="adamw_w_ada",
        out_shape=[jax.ShapeDtypeStruct(w.shape, F32)] * 4,
        compiler_params=_params(),
    )(c64, dmod64, w, m, v)


P_MOD, P_NORM, P_ONORM, P_RELB, P_LB, P_FINAL, P_LOSS, P_END = (0, 3 * D, 4 * D, 5 * D, 6 * D, 7 * D, 8 * D, 9 * D)


def _adamw_small(packed, b_ada, norm_g, onorm_g, relb, hgrn_lb, final_g, ms, vs):
    def body(pk_ref, b_ref, ng_ref, og_ref, rb_ref, lb_ref, fg_ref,
             mb, mn, mo, mr, ml, mf, vb, vn, vo, vr, vl, vf,
             loss_ref, gb, gn, go, gr, gl, gf, db, dn, do, dr, dl, df,
             nmb, nmn, nmo, nmr, nml, nmf, nvb, nvn, nvo, nvr, nvl, nvf):
        tot = pk_ref[0:1, :]
        for j in range(1, N_DEV):
            tot = tot + pk_ref[8 * j:8 * j + 1, :]
        loss_ref[...] = jnp.broadcast_to(jnp.sum(tot[:, P_LOSS:P_END], axis=-1, keepdims=True), (8, 128))

        def upd(g, w_ref, m_ref, v_ref, g_out, d_out, m_out, v_out):
            g_out[...] = g
            d_out[...], m_out[...], v_out[...] = _adamw_math(w_ref[...], g, m_ref[...], v_ref[...])

        upd(tot[:, P_MOD:P_NORM], b_ref, mb, vb, gb, db, nmb, nvb)
        upd(tot[:, P_NORM:P_ONORM], ng_ref, mn, vn, gn, dn, nmn, nvn)
        g_on = tot[:, P_ONORM:P_ONORM + G_DK]
        for h in range(1, G_HEADS):
            g_on = g_on + tot[:, P_ONORM + G_DK * h:P_ONORM + G_DK * (h + 1)]
        upd(g_on, og_ref, mo, vo, go, do, nmo, nvo)
        upd(tot[:, P_RELB:P_LB], rb_ref, mr, vr, gr, dr, nmr, nvr)
        a = lb_ref[...]
        lb = _sigmoid(a[0:1, :] - a[1:2, :])
        g0 = tot[:, P_LB:P_FINAL] * lb * (1.0 - lb)
        row = lax.broadcasted_iota(jnp.int32, (2, D), 0)
        upd(jnp.where(row == 0, g0, -g0), lb_ref, ml, vl, gl, dl, nml, nvl)
        upd(tot[:, P_FINAL:P_LOSS], fg_ref, mf, vf, gf, df, nmf, nvf)

    shapes = [b_ada.shape, norm_g.shape, onorm_g.shape, relb.shape, hgrn_lb.shape, final_g.shape]
    outs = [jax.ShapeDtypeStruct((8, 128), F32)] + [jax.ShapeDtypeStruct(s, F32) for s in shapes] * 4
    return pl.pallas_call(
        body, name="adamw_small",
        out_shape=outs,
        compiler_params=_params(),
    )(packed, b_ada, norm_g, onorm_g, relb, hgrn_lb, final_g, *ms, *vs)


def _local_step(x, target, mod3, norm_g, w_in_g, onorm_g, wa_blk, wb_blk, wo_blk, rel_bias, hgrn_lb, final_g):
    buckets = jnp.asarray(_bucket_tables())
    bias = _bias_tables(rel_bias, buckets)
    proj, ht, qkv, wa_g, wb_g, wo_g = _inproj(x, mod3, norm_g, w_in_g, [wa_blk, wb_blk, wo_blk])
    wa = wa_g.transpose(1, 0, 2).reshape(A_W, D)
    wb = wb_g.reshape(D, D)
    wo = wo_g.reshape(D, D)
    os, ls = [], []
    for p, (_, d) in enumerate(PATTERNS):
        o, l = _attn_fwd(qkv, bias[p], d, "attn_fwd_d%d" % d)
        os.append(o)
        ls.append(l)
    o_raw, ob, states = _hgrn_fwd(proj, hgrn_lb, onorm_g)
    lt, dx2, do, delta, dza, dob, dga, dgb, tsums, gwa, gwb, gwo = _tail(
        x, target, os, ls, ob, proj, mod3, final_g, wa, wb, wo)
    dbs, acc = [None] * len(PATTERNS), ()
    for p in reversed(range(len(PATTERNS))):
        d = PATTERNS[p][1]
        *acc, dbs[p] = _attn_bwd(qkv, do, lt, delta, bias[p], d, "attn_bwd_d%d" % d, prev=tuple(acc),
                                 out_dtype=MXU_DTYPE if p == 0 else F32)
    dqa, dka, dva = acc
    g_relb = _rel_bias_grad(dbs, buckets)
    dqb, dfb, dib, dzb, dlb, dgo = _hgrn_bwd(proj, o_raw, dob, states, hgrn_lb, onorm_g)
    pieces = [[dqa, dka], [dva, dza], [dqb], [dfb], [dib], [dzb], [dga], [dgb]]
    small = [gwa.astype(XCHG_DTYPE).reshape(A_W, N_DEV, D // N_DEV).transpose(1, 0, 2),
             gwb.astype(XCHG_DTYPE).reshape(N_DEV, D // N_DEV, D),
             gwo.astype(XCHG_DTYPE).reshape(N_DEV, D // N_DEV, D)]
    gw_in, *received_small = _gw_in(ht, pieces, small)
    gx, hsums, received_in = _dh(pieces, w_in_g, x, dx2, mod3, norm_g, [gw_in])
    received = [received_in] + received_small
    row = jnp.concatenate([
        hsums[0], hsums[1], tsums[1],
        hsums[2],
        dgo.reshape(G_HEADS, 8, G_DK)[:, 0].reshape(-1),
        g_relb.reshape(-1),
        dlb[0],
        tsums[0],
        tsums[2],
    ])
    return gx, received, row


def kernel(x, c, w_ada, b_ada, norm_g, w_in, hgrn_onorm_g, w_branch_a, w_branch_b, w_out, rel_bias, hgrn_lb, final_g, loss_target, m_w_ada, m_b_ada, m_norm_g, m_w_in, m_hgrn_onorm_g, m_w_branch_a, m_w_branch_b, m_w_out, m_rel_bias, m_hgrn_lb, m_final_g, v_w_ada, v_b_ada, v_norm_g, v_w_in, v_hgrn_onorm_g, v_w_branch_a, v_w_branch_b, v_w_out, v_rel_bias, v_hgrn_lb, v_final_g):
    me = 4 * lax.axis_index("x") + 2 * lax.axis_index("y") + lax.axis_index("c")
    n_ada = w_ada.shape[2]

    w_in_g, c_all = _all_gather([w_in[0].astype(MXU_DTYPE), jnp.broadcast_to(c, (8, D))], "gather_w_in_c")

    c64 = c_all.reshape(8 * N_DEV, D)
    b_loc = lax.dynamic_slice(b_ada, (0, me * n_ada), (1, n_ada))
    mod_part = _mod_fwd(c64, w_ada[0], b_loc)[::8]
    (mod_all,) = _all_gather([mod_part], "gather_mod")
    mod = lax.dynamic_slice(mod_all, (0, me, 0), (N_DEV, 1, n_ada)).reshape(3, D)
    mod3 = jnp.concatenate([mod, jnp.zeros((5, D), F32)], axis=0)

    onorm_t = hgrn_onorm_g
    gx, (r_in, r_a, r_b, r_o), row = _local_step(
        x[0], loss_target[0], mod3, norm_g, w_in_g, onorm_t, w_branch_a[0].astype(MXU_DTYPE),
        w_branch_b[0].astype(MXU_DTYPE), w_out[0].astype(MXU_DTYPE), rel_bias, hgrn_lb,
        final_g.reshape(1, D))
    packed8 = jnp.concatenate([row[None, :], jnp.zeros((7, P_END), F32)], axis=0)
    (packed,) = _all_gather([packed8], "gather_small")
    packed = packed.reshape(8 * N_DEV, P_END)

    g_in, d_in, nm_in, nv_in = _adamw_big(r_in, w_in[0], m_w_in[0], v_w_in[0], "adamw_w_in")
    g_a, d_a, nm_a, nv_a = _adamw_big(r_a, w_branch_a[0], m_w_branch_a[0], v_w_branch_a[0], "adamw_w_branch_a")
    g_b, d_b, nm_b, nv_b = _adamw_big(r_b, w_branch_b[0], m_w_branch_b[0], v_w_branch_b[0], "adamw_w_branch_b")
    g_o, d_o, nm_o, nv_o = _adamw_big(r_o, w_out[0], m_w_out[0], v_w_out[0], "adamw_w_out")

    dmod64 = lax.dynamic_slice(packed, (0, P_MOD + me * n_ada), (8 * N_DEV, n_ada))
    g_ada, d_ada, nm_ada, nv_ada = _adamw_w_ada(c64, dmod64, w_ada[0], m_w_ada[0], v_w_ada[0])

    def flat_relb(t):
        return jnp.pad(t.T, ((0, 0), (0, 128 - N_BUCKETS))).reshape(1, A_HEADS * 128)

    def unflat_relb(t):
        return t.reshape(A_HEADS, 128)[:, :N_BUCKETS].T

    fg2 = lambda t: t.reshape(1, D)
    smalls = _adamw_small(
        packed, b_ada, norm_g, hgrn_onorm_g, flat_relb(rel_bias), hgrn_lb, fg2(final_g),
        [m_b_ada, m_norm_g, m_hgrn_onorm_g, flat_relb(m_rel_bias), m_hgrn_lb, fg2(m_final_g)],
        [v_b_ada, v_norm_g, v_hgrn_onorm_g, flat_relb(v_rel_bias), v_hgrn_lb, fg2(v_final_g)])
    loss = smalls[0][0, 0]

    def small(kind):
        s = smalls[1 + 6 * kind:7 + 6 * kind]
        return s[0], s[1], s[2], unflat_relb(s[3]), s[4], s[5].reshape(D)

    def leaves(ada, sm, w_in_, wa_, wb_, wo_):
        b_, n_, o_, r_, l_, f_ = sm
        return (ada[None], b_, n_, w_in_[None], o_, wa_[None], wb_[None], wo_[None], r_, l_, f_)

    return (loss, gx[None],
            *leaves(g_ada, small(0), g_in, g_a, g_b, g_o),
            *leaves(d_ada, small(1), d_in, d_a, d_b, d_o),
            *leaves(nm_ada, small(2), nm_in, nm_a, nm_b, nm_o),
            *leaves(nv_ada, small(3), nv_in, nv_a, nv_b, nv_o))
```

```python
import functools
import math

import numpy as np
import jax
import jax.numpy as jnp
from jax import lax
from jax.experimental import pallas as pl
from jax.experimental.pallas import tpu as pltpu

F32 = jnp.float32
BF16 = jnp.bfloat16
MXU_DTYPE = jnp.bfloat16
XCHG_DTYPE = jnp.bfloat16

N_DEV = 8
D = 1024
A_HEADS = 8
A_HD = 64
A_W = A_HEADS * A_HD
A_BLK = 128
PATTERNS = ((128, 1), (512, 4), (2048, 16))
N_BUCKETS = 32
MAX_DISTANCE = 2048
NEG = -1e30
G_HEADS = 8
G_DK = 128
G_W = G_HEADS * G_DK
IN_W = 8 * D
EPS = 1e-6
ADAM_LR = 0.001
ADAM_B1 = 0.9
ADAM_B2 = 0.999
ADAM_EPS = 1e-08
ADAM_WD = 0.01
ADAM_STEP = 10

G_CHUNK = 128
G_SUB = 8
G_HPS_FWD = 8
G_HPS_BWD = 8
G_RB = 16
VMEM_LIMIT = 56 * 1024 * 1024

NN = (((1,), (0,)), ((), ()))
NT = (((1,), (1,)), ((), ()))
TN = (((0,), (0,)), ((), ()))
MESH = pl.DeviceIdType.MESH


def _mm(a, b, dims=NN):
    return lax.dot_general(a.astype(MXU_DTYPE), b.astype(MXU_DTYPE), dims,
                           preferred_element_type=F32)


def _mm_exact(t, x):
    hi = x.astype(BF16)
    r = x - hi.astype(F32)
    mid = r.astype(BF16)
    lo = (r - mid.astype(F32)).astype(BF16)
    tb = t.astype(BF16)
    return sum(lax.dot_general(tb, p, NN, preferred_element_type=F32) for p in (hi, mid, lo))


def _split(x):
    hi = x.astype(BF16)
    return hi, (x - hi.astype(F32)).astype(BF16)


def _mm_split(a, b, dims):
    dot = lambda p, q: lax.dot_general(p, q, dims, preferred_element_type=F32)
    return dot(a[0], b[0]) + dot(a[0], b[1]) + dot(a[1], b[0])


def _sigmoid(x):
    return 0.5 * jnp.tanh(0.5 * x) + 0.5


def _params(sem=None):
    return pltpu.CompilerParams(dimension_semantics=sem, vmem_limit_bytes=VMEM_LIMIT)


def _all_gather(xs, name):
    n = len(xs)

    def body(*refs):
        ins, outs = refs[:n], refs[n:2 * n]
        send_sems, recv_sems, local_sems = refs[2 * n:]
        x, y, c = lax.axis_index("x"), lax.axis_index("y"), lax.axis_index("c")
        me, sibling = (x, y, c), (x, y, 1 - c)
        chips = [(1 - x, y), (x, 1 - y), (1 - x, 1 - y)]

        def slot(ref, dev):
            return ref.at[4 * dev[0] + 2 * dev[1] + dev[2]]

        def copy(a, k, block, to, src=None):
            return pltpu.make_async_remote_copy(
                src_ref=slot(outs[a], block) if src is None else src,
                dst_ref=slot(outs[a], block),
                send_sem=send_sems.at[a, k], recv_sem=recv_sems.at[a, k],
                device_id=to, device_id_type=MESH)

        mine, first, passed = [], [], []
        for a in range(n):
            cp = pltpu.make_async_copy(ins[a], slot(outs[a], me), local_sems.at[a])
            cp.start()
            mine.append(cp)
            first.append(copy(a, 0, me, sibling, src=ins[a]))
            for j, chip in enumerate(chips):
                first.append(copy(a, 1 + j, me, (*chip, c), src=ins[a]))
        for cp in first:
            cp.start()
        for j, chip in enumerate(chips):
            for a in range(n):
                copy(a, 1 + j, (*chip, c), me).wait_recv()
                cp = copy(a, 4 + j, (*chip, c), sibling)
                cp.start()
                passed.append(cp)
        for a in range(n):
            copy(a, 0, sibling, me).wait_recv()
            for j, chip in enumerate(chips):
                copy(a, 4 + j, (*chip, 1 - c), me).wait_recv()
        for cp in first + passed:
            cp.wait_send()
        for cp in mine:
            cp.wait()

    any_spec = pl.BlockSpec(memory_space=pl.ANY)
    return pl.pallas_call(
        body, name=name,
        out_shape=[jax.ShapeDtypeStruct((N_DEV,) + v.shape, v.dtype) for v in xs],
        in_specs=[any_spec] * n, out_specs=[any_spec] * n,
        scratch_shapes=[pltpu.SemaphoreType.DMA((n, 7)), pltpu.SemaphoreType.DMA((n, 7)),
                        pltpu.SemaphoreType.DMA((n,))],
    )(*xs)


def _all_to_all_copies(ins, outs, send_sems, recv_sems, local_sems, gather=False):
    n = len(ins)
    x, y, c = lax.axis_index("x"), lax.axis_index("y"), lax.axis_index("c")
    me = 4 * x + 2 * y + c
    peers = []
    for m in range(1, N_DEV):
        peers.append((1 - x if m & 4 else x, 1 - y if m & 2 else y, 1 - c if m & 1 else c))

    def chunk(a, j):
        return ins[a] if gather else ins[a].at[j]

    def copy(a, k, landing):
        peer = peers[k]
        pid = 4 * peer[0] + 2 * peer[1] + peer[2]
        return pltpu.make_async_remote_copy(
            src_ref=chunk(a, pid), dst_ref=outs[a].at[pid if landing else me],
            send_sem=send_sems.at[a, k], recv_sem=recv_sems.at[a, k],
            device_id=peer, device_id_type=MESH)

    def local(a):
        return pltpu.make_async_copy(chunk(a, me), outs[a].at[me], local_sems.at[a])

    def start():
        for a in range(n):
            local(a).start()
        for k in range(N_DEV - 1):
            for a in range(n):
                copy(a, k, False).start()

    def wait():
        for k in range(N_DEV - 1):
            for a in range(n):
                copy(a, k, True).wait_recv()
        for k in range(N_DEV - 1):
            for a in range(n):
                copy(a, k, False).wait_send()
        for a in range(n):
            local(a).wait()

    return start, wait


def _mod_fwd(c64, w_ada, b_loc):
    def body(c_ref, w_ref, b_ref, o_ref):
        cv = c_ref[...]
        sc = cv * _sigmoid(cv)
        o_ref[...] = _mm(sc, w_ref[...]) + b_ref[...]

    return pl.pallas_call(
        body, name="mod_fwd",
        out_shape=jax.ShapeDtypeStruct((c64.shape[0], w_ada.shape[1]), F32),
        compiler_params=_params(),
    )(c64, w_ada, b_loc)


def _inproj(x, mod3, norm_g, w_in_g, blocks, tm=256):
    S = x.shape[0]
    ni = S // tm
    nb = len(blocks)

    def body(*refs):
        x_ref, mod_ref, g_ref, w_ref = refs[:4]
        b_ins = refs[4:4 + nb]
        proj_ref, ht_ref, qkv_ref = refs[4 + nb:7 + nb]
        b_outs = refs[7 + nb:7 + 2 * nb]
        w_all, send_sems, recv_sems, local_sems = refs[7 + 2 * nb:]
        i = pl.program_id(0)
        start, wait = _all_to_all_copies(b_ins, b_outs, send_sems, recv_sems, local_sems, gather=True)

        @pl.when(i == 0)
        def _():
            start()
            pltpu.sync_copy(w_ref, w_all)

        xv = x_ref[...]
        r = lax.rsqrt(jnp.mean(xv * xv, axis=-1, keepdims=True) + EPS)
        h = ((xv * r * g_ref[...]) * (1.0 + mod_ref[1:2, :]) + mod_ref[0:1, :]).astype(MXU_DTYPE)
        ht_ref[...] = h.T
        for j in range(N_DEV):
            pj = _mm(h, w_all[j])
            proj_ref[:, j * D:(j + 1) * D] = pj
            for c in range(3):
                if c // 2 == j:
                    for p in range(A_HEADS // 2):
                        lo = (c % 2) * A_W + 2 * A_HD * p
                        qkv_ref[c, p] = pj[:, lo:lo + 2 * A_HD]

        @pl.when(i == ni - 1)
        def _():
            wait()

    any_spec = pl.BlockSpec(memory_space=pl.ANY)
    return pl.pallas_call(
        body, name="inproj_gather",
        grid=(ni,),
        in_specs=[pl.BlockSpec((tm, D), lambda i: (i, 0)),
                  pl.BlockSpec((8, D), lambda i: (0, 0)),
                  pl.BlockSpec((1, D), lambda i: (0, 0)),
                  any_spec] + [any_spec] * nb,
        out_specs=[pl.BlockSpec((tm, IN_W), lambda i: (i, 0)),
                   pl.BlockSpec((D, tm), lambda i: (0, i)),
                   pl.BlockSpec((3, A_HEADS // 2, tm, 2 * A_HD), lambda i: (0, 0, i, 0))] + [any_spec] * nb,
        out_shape=[jax.ShapeDtypeStruct((S, IN_W), F32), jax.ShapeDtypeStruct((D, S), MXU_DTYPE),
                   jax.ShapeDtypeStruct((3, A_HEADS // 2, S, 2 * A_HD), F32)]
                  + [jax.ShapeDtypeStruct((N_DEV,) + b.shape, b.dtype) for b in blocks],
        scratch_shapes=[pltpu.VMEM(w_in_g.shape, w_in_g.dtype),
                        pltpu.SemaphoreType.DMA((nb, N_DEV - 1)), pltpu.SemaphoreType.DMA((nb, N_DEV - 1)),
                        pltpu.SemaphoreType.DMA((nb,))],
        compiler_params=_params(("arbitrary",)),
    )(x, mod3, norm_g, w_in_g, *blocks)


def _bucket_tables():
    qi = np.arange(A_BLK)[:, None]
    kj = np.arange(2 * A_BLK)[None, :]
    delta = qi + A_BLK - kj
    out = []
    for window, dil in PATTERNS:
        span = window // dil
        band = (delta >= 0) & (delta <= span)
        dist = np.clip(delta, 0, None) * dil
        max_exact = N_BUCKETS // 2
        nf = dist.astype(np.float32)
        large = max_exact + (np.log(np.maximum(nf, np.float32(1.0)) / np.float32(max_exact))
                             / np.float32(math.log(MAX_DISTANCE / max_exact))
                             * np.float32(N_BUCKETS - max_exact)).astype(np.int32)
        large = np.minimum(large, N_BUCKETS - 1)
        bucket = np.where(dist < max_exact, dist, large)
        out.append(np.where(band, bucket, -1).astype(np.int32))
    return np.stack(out)


def _bias_tables(rel_bias, buckets):
    def body(rb_ref, bk_ref, o_ref):
        bk = bk_ref[0]
        for h in range(A_HEADS):
            acc = jnp.full(bk.shape, NEG, F32)
            for b in range(N_BUCKETS):
                acc = jnp.where(bk == b, rb_ref[b, h], acc)
            o_ref[0, h] = acc

    return pl.pallas_call(
        body, name="bias_tables",
        grid=(len(PATTERNS),),
        in_specs=[pl.BlockSpec(memory_space=pltpu.SMEM),
                  pl.BlockSpec((1, A_BLK, 2 * A_BLK), lambda p: (p, 0, 0))],
        out_specs=pl.BlockSpec((1, A_HEADS, A_BLK, 2 * A_BLK), lambda p: (p, 0, 0, 0)),
        out_shape=jax.ShapeDtypeStruct((len(PATTERNS), A_HEADS, A_BLK, 2 * A_BLK), F32),
        compiler_params=_params(("arbitrary",)),
    )(rel_bias, buckets)


A_TILES = 32


def _attn_heads_per_step(d):
    return A_HEADS if d == 1 else 2


def _attn_in_specs(sb, nsb, hw):
    blk = (1, hw // 2, sb, 2 * A_HD)

    def cur(c):
        return pl.BlockSpec(blk, lambda hp, n: (c, hp, jnp.minimum(n, nsb - 1), 0))

    def prev(c):
        return pl.BlockSpec(blk, lambda hp, n: (c, hp, jnp.maximum(jnp.minimum(n, nsb - 1) - 1, 0), 0))

    return [cur(0), prev(1), cur(1), prev(2), cur(2)]


def _rows(r, d):
    return pl.ds(r, A_BLK) if d == 1 else pl.ds(r, A_BLK, stride=d)


def _for_residues(d, hw, fn):
    unroll = min(d, max(1, A_TILES // hw))
    if d == unroll:
        _round_robin([g for r in range(d) for g in fn(r)])
    else:
        def group(g, c):
            _round_robin([t for u in range(unroll) for t in fn(g * unroll + u)])
            return c
        lax.fori_loop(0, d // unroll, group, 0)


def _attn_stack(t):
    first_half = lax.broadcasted_iota(jnp.int32, (1, 2 * A_HD), 1) < A_HD
    return jnp.concatenate([jnp.where(first_half, t, 0.0), jnp.where(first_half, 0.0, t)], axis=0)


def _attn_unstack(t2):
    first_half = lax.broadcasted_iota(jnp.int32, (1, 2 * A_HD), 1) < A_HD
    return jnp.where(first_half, t2[:A_BLK], t2[A_BLK:])


def _attn_scores(q, k, b_ref, pp, first):
    bias = jnp.concatenate([b_ref[2 * pp] + first, b_ref[2 * pp + 1] + first], axis=0)
    return _mm(_attn_stack(q), k, NT) * (A_HD ** -0.5) + bias


def _attn_fwd(qkv, bias_p, d, name):
    S = qkv.shape[2]
    hw = _attn_heads_per_step(d)
    sub = A_BLK * d
    nsub = max(1, A_TILES // (hw * d))
    sb = sub * nsub
    nsb = S // sb

    def body(q_ref, kp_ref, kc_ref, vp_ref, vc_ref, b_ref, o_ref, l_ref):
        n = pl.program_id(1)
        kj = lax.broadcasted_iota(jnp.int32, (A_BLK, 2 * A_BLK), 1)
        first = jnp.where((n == 0) & (kj < A_BLK), NEG, 0.0).astype(F32)
        no_first = jnp.zeros((A_BLK, 2 * A_BLK), F32)

        def residue(r, u=0):
            rows = _rows(u * sub + r, d)
            behind = _rows(((nsub if u == 0 else u) - 1) * sub + r, d)

            def pair(pp):
                lanes = pl.ds(2 * A_HD * pp, 2 * A_HD)
                kc, vc = kc_ref.at[0, pp], vc_ref.at[0, pp]
                kb, vb = (kp_ref.at[0, pp], vp_ref.at[0, pp]) if u == 0 else (kc, vc)
                k = jnp.concatenate([kb[behind, :], kc[rows, :]], axis=0)
                v = jnp.concatenate([vb[behind, :], vc[rows, :]], axis=0)
                s = _attn_scores(q_ref.at[0, pp][rows, :], k, b_ref, pp, first if u == 0 else no_first)
                yield
                m = jnp.max(s, axis=-1, keepdims=True)
                p = jnp.exp(s - m)
                den = jnp.sum(p, axis=-1, keepdims=True)
                pv = _mm(p, v)
                yield
                o_ref[rows, lanes] = _attn_unstack(pv / den)
                l_ref[rows, lanes] = _attn_unstack(jnp.broadcast_to(m + jnp.log(den), (2 * A_BLK, 2 * A_HD)))

            return [pair(pp) for pp in range(hw // 2)]

        if nsub == 1:
            _for_residues(d, hw, residue)
        else:
            _round_robin([g for u in range(nsub) for r in range(d) for g in residue(r, u)])

    out = pl.BlockSpec((sb, A_HD * hw), lambda hp, n: (n, hp))
    return pl.pallas_call(
        body, name=name,
        grid=(A_HEADS // hw, nsb),
        in_specs=_attn_in_specs(sb, nsb, hw) + [pl.BlockSpec((hw, A_BLK, 2 * A_BLK), lambda hp, n: (hp, 0, 0))],
        out_specs=[out, out],
        out_shape=[jax.ShapeDtypeStruct((S, A_W), F32)] * 2,
        compiler_params=_params(("parallel", "parallel")),
    )(qkv, qkv, qkv, qkv, qkv, bias_p)


def _attn_bwd(qkv, do, lt, delta, bias_p, d, name, prev=(), out_dtype=F32):
    S = qkv.shape[2]
    hw = _attn_heads_per_step(d)
    sub = A_BLK * d
    nsub = max(1, A_TILES // (hw * d))
    sb = sub * nsub
    nsb = S // sb
    done = (nsub - 1) * sub

    def body(*refs):
        q_ref, kp_ref, kc_ref, vp_ref, vc_ref, do_ref, lt_ref, dl_ref, b_ref = refs[:9]
        pq_ref, pk_ref, pv_ref = refs[9:9 + len(prev)] if prev else (None, None, None)
        dq_ref, dk_ref, dv_ref, db_ref, ck, cv = refs[9 + len(prev):]
        n = pl.program_id(1)
        plus = lambda t, p_ref, idx: (t if p_ref is None else t + p_ref[idx]).astype(out_dtype)

        @pl.when(n == 0)
        def _():
            db_ref[...] = jnp.zeros_like(db_ref)
            ck[...] = jnp.zeros_like(ck)
            cv[...] = jnp.zeros_like(cv)

        @pl.when(n < nsb)
        def _():
            kj = lax.broadcasted_iota(jnp.int32, (A_BLK, 2 * A_BLK), 1)
            first = jnp.where((n == 0) & (kj < A_BLK), NEG, 0.0).astype(F32)
            no_first = jnp.zeros((A_BLK, 2 * A_BLK), F32)
            if done:
                dk_ref[0:done, :] = plus(ck[0:done, :], pk_ref, (slice(0, done), slice(None)))
                dv_ref[0:done, :] = plus(cv[0:done, :], pv_ref, (slice(0, done), slice(None)))

            def residue(r, u=0):
                rows = _rows(u * sub + r, d)
                behind = _rows(((nsub if u == 0 else u) - 1) * sub + r, d)

                def pair(pp):
                    lanes = pl.ds(2 * A_HD * pp, 2 * A_HD)
                    lt_r, dl_r = lt_ref[rows, lanes], dl_ref[rows, lanes]
                    kc, vc = kc_ref.at[0, pp], vc_ref.at[0, pp]
                    kb, vb = (kp_ref.at[0, pp], vp_ref.at[0, pp]) if u == 0 else (kc, vc)
                    k = jnp.concatenate([kb[behind, :], kc[rows, :]], axis=0)
                    v = jnp.concatenate([vb[behind, :], vc[rows, :]], axis=0)
                    q2 = _attn_stack(q_ref.at[0, pp][rows, :])
                    do2 = _attn_stack(do_ref[rows, lanes])
                    col = lambda t: jnp.concatenate([t[:, 0:1], t[:, A_HD:A_HD + 1]], axis=0)
                    s = _attn_scores(q_ref.at[0, pp][rows, :], k, b_ref, pp, first if u == 0 else no_first)
                    dp = _mm(do2, v, NT)
                    yield
                    p = jnp.exp(s - col(lt_r))
                    ds = p * (dp - col(dl_r))
                    db_ref[2 * pp] += ds[:A_BLK]
                    db_ref[2 * pp + 1] += ds[A_BLK:]
                    dq = _mm(ds, k)
                    dk = _mm(ds, q2, TN) * (A_HD ** -0.5)
                    dv = _mm(p, do2, TN)
                    yield
                    dq_ref[rows, lanes] = plus(_attn_unstack(dq) * (A_HD ** -0.5), pq_ref, (rows, lanes))
                    if u == 0:
                        dk_ref[behind, lanes] = plus(ck[behind, lanes] + dk[:A_BLK], pk_ref, (behind, lanes))
                        dv_ref[behind, lanes] = plus(cv[behind, lanes] + dv[:A_BLK], pv_ref, (behind, lanes))
                    else:
                        ck[behind, lanes] += dk[:A_BLK]
                        cv[behind, lanes] += dv[:A_BLK]
                    ck[rows, lanes] = dk[A_BLK:]
                    cv[rows, lanes] = dv[A_BLK:]

                return [pair(pp) for pp in range(hw // 2)]

            if nsub == 1:
                _for_residues(d, hw, residue)
            else:
                _round_robin([g for u in range(nsub) for r in range(d) for g in residue(r, u)])

        @pl.when(n == nsb)
        def _():
            dk_ref[...] = plus(ck[...], pk_ref, ...)
            dv_ref[...] = plus(cv[...], pv_ref, ...)

    w = A_HD * hw
    row = pl.BlockSpec((sb, w), lambda hp, n: (jnp.minimum(n, nsb - 1), hp))
    lag = pl.BlockSpec((sb, w), lambda hp, n: (jnp.maximum(n - 1, 0), hp))
    tab = pl.BlockSpec((hw, A_BLK, 2 * A_BLK), lambda hp, n: (hp, 0, 0))
    return pl.pallas_call(
        body, name=name,
        grid=(A_HEADS // hw, nsb + 1),
        in_specs=_attn_in_specs(sb, nsb, hw) + [row, row, row, tab] + ([row, lag, lag] if prev else []),
        out_specs=[row, lag, lag, tab],
        out_shape=[jax.ShapeDtypeStruct((S, A_W), out_dtype)] * 3
                  + [jax.ShapeDtypeStruct((A_HEADS, A_BLK, 2 * A_BLK), F32)],
        scratch_shapes=[pltpu.VMEM((sb, w), F32), pltpu.VMEM((sb, w), F32)],
        compiler_params=_params(("parallel", "arbitrary")),
    )(qkv, qkv, qkv, qkv, qkv, do, lt, delta, bias_p, *prev)


def _rel_bias_grad(dbs, buckets):
    def body(d1, d2, d3, bk_ref, o_ref):
        row = lax.broadcasted_iota(jnp.int32, (A_HEADS, 128), 0)
        lane = lax.broadcasted_iota(jnp.int32, (A_HEADS, 128), 1)
        acc = jnp.zeros((A_HEADS, 128), F32)
        for p, dref in enumerate((d1, d2, d3)):
            bk = bk_ref[p]
            for h in range(A_HEADS):
                ds = dref[h]
                for b in range(N_BUCKETS):
                    s = jnp.sum(jnp.where(bk == b, ds, 0.0), keepdims=True)
                    acc = acc + jnp.where((row == h) & (lane == b), s, 0.0)
        o_ref[...] = acc

    return pl.pallas_call(
        body, name="rel_bias_grad",
        out_shape=jax.ShapeDtypeStruct((A_HEADS, 128), F32),
        compiler_params=_params(),
    )(*dbs, buckets)


def _tri(c):
    t = np.tril(np.ones((c, c), np.float32))
    return jnp.asarray(t), jnp.asarray(t.T.copy())


def _fill_above(ref, x, pad):
    ref[0:G_SUB, :] = jnp.full((G_SUB, x.shape[1]), pad, F32)
    ref[G_SUB:, :] = x


def _fill_below(ref, x, pad):
    ref[0:x.shape[0], :] = x
    ref[x.shape[0]:, :] = jnp.full((G_SUB, x.shape[1]), pad, F32)


def _hgrn_gates(q_ref, f_ref, lbp_ref, tri_ref):
    qraw = q_ref[...]
    sq = _sigmoid(qraw)
    q = qraw * sq
    sg = _sigmoid(f_ref[...])
    lb = _sigmoid(lbp_ref[0:1, :] - lbp_ref[1:2, :])
    f = lb + (1.0 - lb) * sg
    k = 1.0 - f
    b = _mm_exact(tri_ref[...], jnp.log(f))
    return qraw, sq, q, sg, lb, f, k, b


def _hgrn_col(C, base, idx, hps):
    return pl.BlockSpec((C, hps * G_DK), lambda h, n: (idx(n), base * (G_HEADS // hps) + h))


def _round_robin(stages):
    live = list(stages)
    while live:
        nxt = []
        for g in live:
            try:
                next(g)
                nxt.append(g)
            except StopIteration:
                pass
        live = nxt


def _hgrn_levels(C):
    out, m = [], G_SUB
    while 2 * m <= C:
        out.append(m)
        m *= 2
    return out


def _hgrn_level_masks(C):
    ti = np.arange(C)[:, None]
    si = np.arange(C)[None, :]
    return jnp.asarray(np.stack([((ti // (2 * m) == si // (2 * m)) & (ti - si >= G_SUB)).astype(np.float32)
                                 for m in _hgrn_levels(C)]))


def _hgrn_level(b, q, k, C, m):
    zeros = jnp.zeros((m, G_DK), F32)
    eq, ek, qt, kt = [], [], [], []
    for blk in range(0, C // m, 2):
        lo, mid, hi = blk * m, (blk + 1) * m, (blk + 2) * m
        ref = b[mid:mid + 1]
        e_right = jnp.exp(b[mid:hi] - ref)
        e_left = jnp.exp(ref - b[lo:mid])
        eq += [zeros, e_right]
        ek += [e_left, zeros]
        qt += [zeros, q[mid:hi] * e_right]
        kt += [k[lo:mid] * e_left, zeros]
    cat = lambda parts: jnp.concatenate(parts, axis=0)
    return cat(qt), cat(kt), cat(eq), cat(ek)


def _hgrn_fwd(proj, hgrn_lb, onorm_g, C=G_CHUNK):
    S = proj.shape[0]
    nc = S // C
    tri, _ = _tri(C)
    masks = _hgrn_level_masks(C)
    hps = G_HPS_FWD

    def body(q_ref, f_ref, i_ref, z_ref, lbp_ref, go_ref, tri_ref, pm_ref, o_ref, ob_ref, st_ref, St, kp, vp, fp):
        @pl.when(pl.program_id(1) == 0)
        def _():
            St[...] = jnp.zeros_like(St)

        heads = []
        for hh in range(hps):
            ln = pl.ds(G_DK * hh, G_DK)
            heads.append(head(
                q_ref.at[:, ln], f_ref.at[:, ln], i_ref.at[:, ln], z_ref.at[:, ln], lbp_ref.at[:, ln], go_ref,
                tri_ref, pm_ref, o_ref.at[:, ln], ob_ref.at[:, ln], st_ref.at[0, hh], St.at[hh], kp.at[hh], vp.at[hh],
                fp.at[hh]))
        _round_robin(heads)

    def head(q_ref, f_ref, i_ref, z_ref, lbp_ref, go_ref, tri_ref, pm_ref, o_ref, ob_ref, st_ref, St, kp, vp, fp):
        _, _, q, _, _, f, k, b = _hgrn_gates(q_ref, f_ref, lbp_ref, tri_ref)
        v = i_ref[...]
        bC = b[C - 1:C, :]
        S0 = St[...]
        o = _mm(q * jnp.exp(b), S0, NT)
        yield
        _fill_above(kp, k, 0.0)
        _fill_above(vp, v, 0.0)
        _fill_above(fp, f, 1.0)
        near = []
        for r0 in range(0, C, G_RB):
            qb = q[r0:r0 + G_RB]
            acc = e = None
            for l in range(G_SUB):
                rows = pl.ds(G_SUB - l + r0, G_RB)
                if l > 0:
                    fl = fp[pl.ds(G_SUB - l + 1 + r0, G_RB), :]
                    e = fl if e is None else e * fl
                kl = kp[rows, :]
                a = jnp.sum(qb * kl if e is None else qb * kl * e, axis=-1, keepdims=True)
                t = a * vp[rows, :]
                acc = t if acc is None else acc + t
            near.append(acc)
        o = o + jnp.concatenate(near, axis=0)
        yield
        a_off = jnp.zeros((C, C), F32)
        for lv, m in enumerate(_hgrn_levels(C)):
            qt, kt, _, _ = _hgrn_level(b, q, k, C, m)
            prod = _mm_split(_split(qt), _split(kt), NT) if m == G_SUB else _mm(qt, kt, NT)
            a_off = a_off + pm_ref[lv] * prod
        yield
        o = o + _mm(a_off, v)
        S1 = S0 * jnp.exp(bC) + _mm(v, k * jnp.exp(bC - b), TN)
        St[...] = S1
        st_ref[...] = S1
        o_ref[...] = o
        r = lax.rsqrt(jnp.mean(o * o, axis=-1, keepdims=True) + EPS)
        z = z_ref[...]
        ob_ref[...] = (o * r * go_ref[...] * (z * _sigmoid(z))).astype(MXU_DTYPE)

    ident = lambda n: n
    w = hps * G_DK
    out = pl.BlockSpec((C, w), lambda h, n: (n, h))
    return pl.pallas_call(
        body, name="hgrn_fwd",
        grid=(G_HEADS // hps, nc),
        in_specs=[_hgrn_col(C, base, ident, hps) for base in (2, 3, 4, 5)] + [
                  pl.BlockSpec((2, w), lambda h, n: (0, h)),
                  pl.BlockSpec((1, G_DK), lambda h, n: (0, 0)),
                  pl.BlockSpec((C, C), lambda h, n: (0, 0)),
                  pl.BlockSpec(masks.shape, lambda h, n: (0, 0, 0))],
        out_specs=[out, out, pl.BlockSpec((1, hps, G_DK, G_DK), lambda h, n: (n, h, 0, 0))],
        out_shape=[jax.ShapeDtypeStruct((S, G_W), F32), jax.ShapeDtypeStruct((S, G_W), MXU_DTYPE),
                   jax.ShapeDtypeStruct((nc, G_HEADS, G_DK, G_DK), F32)],
        scratch_shapes=[pltpu.VMEM((hps, G_DK, G_DK), F32)] + [pltpu.VMEM((hps, C + G_SUB, G_DK), F32)] * 3,
        compiler_params=_params(("parallel", "arbitrary")),
    )(proj, proj, proj, proj, hgrn_lb, onorm_g, tri, masks)


def _hgrn_bwd(proj, o_raw, dob, states, hgrn_lb, onorm_g, C=G_CHUNK):
    S = proj.shape[0]
    nc = S // C
    tri, triu = _tri(C)
    masks = _hgrn_level_masks(C)
    hps = G_HPS_BWD

    def body(q_ref, f_ref, i_ref, z_ref, o_ref, dob_ref, s0_ref, s1_ref, lbp_ref, go_ref, tri_ref, triu_ref,
             pm_ref, dq_ref, df_ref, di_ref, dz_ref, dlb_ref, dgo_ref, dSt, *shifted):
        @pl.when(pl.program_id(1) == 0)
        def _():
            dSt[...] = jnp.zeros_like(dSt)
            dlb_ref[...] = jnp.zeros_like(dlb_ref)
            dgo_ref[...] = jnp.zeros_like(dgo_ref)

        heads = []
        for hh in range(hps):
            ln = pl.ds(G_DK * hh, G_DK)
            heads.append(head(
                q_ref.at[:, ln], f_ref.at[:, ln], i_ref.at[:, ln], z_ref.at[:, ln], o_ref.at[:, ln],
                dob_ref.at[:, ln], s0_ref.at[0, hh], s1_ref.at[0, hh], lbp_ref.at[:, ln], go_ref, tri_ref, triu_ref,
                pm_ref, dq_ref.at[:, ln], df_ref.at[:, ln], di_ref.at[:, ln], dz_ref.at[:, ln], dlb_ref.at[:, ln],
                dgo_ref.at[pl.ds(8 * hh, 8), :], dSt.at[hh], *[t.at[hh] for t in shifted]))
        _round_robin(heads)

    def head(q_ref, f_ref, i_ref, z_ref, o_ref, dob_ref, s0_ref, s1_ref, lbp_ref, go_ref, tri_ref, triu_ref,
             pm_ref, dq_ref, df_ref, di_ref, dz_ref, dlb_ref, dgo_ref, dSt, kp, vp, fp, qn, dn_, fn, xs, dac):
        cn = nc - 1 - pl.program_id(1)
        qraw, sq, q, sg, lb, f, k, b = _hgrn_gates(q_ref, f_ref, lbp_ref, tri_ref)
        v = i_ref[...]
        bC = b[C - 1:C, :]
        eb = jnp.exp(b)
        ecb = jnp.exp(bC - b)
        o = o_ref[...]
        z = z_ref[...]
        sz = _sigmoid(z)
        go = go_ref[...]
        g_ob = dob_ref[...]
        r = lax.rsqrt(jnp.mean(o * o, axis=-1, keepdims=True) + EPS)
        nh = o * r
        dnrm = g_ob * (z * sz)
        dz_ref[...] = (g_ob * (nh * go) * (sz * (1.0 + z * (1.0 - sz)))).astype(MXU_DTYPE)
        dgo_ref[0:1, :] += jnp.sum(dnrm * nh, axis=0, keepdims=True)
        dn = dnrm * go
        do = r * (dn - nh * jnp.mean(dn * nh, axis=-1, keepdims=True))

        yield
        S0 = jnp.where(cn == 0, 0.0, s0_ref[...])
        S1 = s1_ref[...]
        dS1 = dSt[...]
        dq = eb * _mm(do, S0)
        dk = ecb * _mm(v, dS1)
        dv = _mm(k * ecb, dS1, NT)
        bnd = jnp.sum(dS1 * S1, axis=0, keepdims=True)
        dSt[...] = dS1 * jnp.exp(bC) + _mm(do, q * eb, TN)

        _fill_above(kp, k, 0.0)
        _fill_above(vp, v, 0.0)
        _fill_above(fp, f, 1.0)
        _fill_below(qn, q, 0.0)
        _fill_below(dn_, do, 0.0)
        _fill_below(fn, f, 1.0)
        yield
        for r0 in range(0, C, G_RB):
            do_b = do[r0:r0 + G_RB]
            for l in range(G_SUB):
                xs[pl.ds(l * C + r0, G_RB), :] = (do_b * vp[pl.ds(G_SUB - l + r0, G_RB), :]).astype(MXU_DTYPE)
        dac[0:G_SUB * C, :] = _mm(xs[...], jnp.ones((G_DK, G_DK), MXU_DTYPE))
        dac[G_SUB * C:, :] = jnp.zeros((G_SUB, G_DK), F32)
        yield
        near_q, near_k, near_v = [], [], []
        for r0 in range(0, C, G_RB):
            k_b = k[r0:r0 + G_RB]
            aq = ak = av = e = e2 = None
            for l in range(G_SUB):
                down, up = pl.ds(G_SUB - l + r0, G_RB), pl.ds(l + r0, G_RB)
                if l > 0:
                    fl = fp[pl.ds(G_SUB - l + 1 + r0, G_RB), :]
                    e = fl if e is None else e * fl
                    fu = fn[up, :]
                    e2 = fu if e2 is None else e2 * fu
                kl = kp[down, :]
                t = dac[pl.ds(l * C + r0, G_RB), :] * (kl if e is None else kl * e)
                aq = t if aq is None else aq + t
                qu = qn[up, :]
                qe = qu if e2 is None else qu * e2
                dou = dn_[up, :]
                a2 = jnp.sum(qe * k_b, axis=-1, keepdims=True)
                t = dac[pl.ds(l * C + l + r0, G_RB), :] * qe
                ak = t if ak is None else ak + t
                t = a2 * dou
                av = t if av is None else av + t
            near_q.append(aq)
            near_k.append(ak)
            near_v.append(av)
        dq = dq + jnp.concatenate(near_q, axis=0)
        dk = dk + jnp.concatenate(near_k, axis=0)
        dv = dv + jnp.concatenate(near_v, axis=0)

        yield
        da_all = _mm(do, v, NT)
        a_off = jnp.zeros((C, C), F32)
        for lv, m in enumerate(_hgrn_levels(C)):
            qt, kt, eq, ek = _hgrn_level(b, q, k, C, m)
            da_m = pm_ref[lv] * da_all
            if m == G_SUB:
                qs, ks, das = _split(qt), _split(kt), _split(da_m)
                a_off = a_off + pm_ref[lv] * _mm_split(qs, ks, NT)
                dq = dq + _mm_split(das, ks, NN) * eq
                dk = dk + _mm_split(das, qs, TN) * ek
            else:
                a_off = a_off + pm_ref[lv] * _mm(qt, kt, NT)
                dq = dq + _mm(da_m, kt) * eq
                dk = dk + _mm(da_m, qt, TN) * ek
        dv = dv + _mm(a_off, do, TN)

        yield
        row = lax.broadcasted_iota(jnp.int32, (C, 1), 0)
        db = q * dq - k * dk + jnp.where(row == C - 1, bnd, 0.0)
        dg = _mm_exact(triu_ref[...], db)
        df = dg / f - dk
        df_ref[...] = (df * (1.0 - lb) * (sg * (1.0 - sg))).astype(MXU_DTYPE)
        dlb_ref[0:1, :] += jnp.sum(df * (1.0 - sg), axis=0, keepdims=True)
        dq_ref[...] = (dq * (sq * (1.0 + qraw * (1.0 - sq)))).astype(MXU_DTYPE)
        di_ref[...] = dv.astype(MXU_DTYPE)

    rev = lambda n: nc - 1 - n
    w = hps * G_DK
    blk = pl.BlockSpec((C, w), lambda h, n: (nc - 1 - n, h))
    return pl.pallas_call(
        body, name="hgrn_bwd",
        grid=(G_HEADS // hps, nc),
        in_specs=[_hgrn_col(C, base, rev, hps) for base in (2, 3, 4, 5)] + [
                  blk, blk,
                  pl.BlockSpec((1, hps, G_DK, G_DK), lambda h, n: (jnp.maximum(nc - 2 - n, 0), h, 0, 0)),
                  pl.BlockSpec((1, hps, G_DK, G_DK), lambda h, n: (nc - 1 - n, h, 0, 0)),
                  pl.BlockSpec((2, w), lambda h, n: (0, h)),
                  pl.BlockSpec((1, G_DK), lambda h, n: (0, 0)),
                  pl.BlockSpec((C, C), lambda h, n: (0, 0)),
                  pl.BlockSpec((C, C), lambda h, n: (0, 0)),
                  pl.BlockSpec(masks.shape, lambda h, n: (0, 0, 0))],
        out_specs=[blk, blk, blk, blk,
                   pl.BlockSpec((8, w), lambda h, n: (0, h)),
                   pl.BlockSpec((8 * hps, G_DK), lambda h, n: (h, 0))],
        out_shape=[jax.ShapeDtypeStruct((S, G_W), MXU_DTYPE)] * 4
                  + [jax.ShapeDtypeStruct((8, G_W), F32), jax.ShapeDtypeStruct((8 * G_HEADS, G_DK), F32)],
        scratch_shapes=[pltpu.VMEM((hps, G_DK, G_DK), F32)] + [pltpu.VMEM((hps, C + G_SUB, G_DK), F32)] * 6
                       + [pltpu.VMEM((hps, G_SUB * C, G_DK), MXU_DTYPE),
                          pltpu.VMEM((hps, G_SUB * C + G_SUB, G_DK), F32)],
        compiler_params=_params(("parallel", "arbitrary")),
    )(proj, proj, proj, proj, o_raw, dob, states, states, hgrn_lb, onorm_g, tri, triu, masks)


def _tail(x, target, os, ls, ob, proj, mod3, final_g, wa, wb, wo, tm=256):
    S = x.shape[0]
    nt = S // tm

    def body(x_ref, t_ref, o1, o2, o3, l1, l2, l3, za_ref, ob_ref, ga_ref, gb_ref, mod_ref, fg_ref,
             wa_ref, wb_ref, wo_ref,
             lt_ref, dx2_ref, do_ref, dl_ref, dza_ref, dob_ref, dga_ref, dgb_ref, sums_ref,
             gwa_ref, gwb_ref, gwo_ref, acc_a, acc_b, acc_o):
        i = pl.program_id(0)

        @pl.when(i == 0)
        def _():
            sums_ref[...] = jnp.zeros_like(sums_ref)
            acc_a[...] = jnp.zeros_like(acc_a)
            acc_b[...] = jnp.zeros_like(acc_b)
            acc_o[...] = jnp.zeros_like(acc_o)

        a1, a2, a3 = l1[...], l2[...], l3[...]
        lm = jnp.maximum(jnp.maximum(a1, a2), a3)
        e1, e2, e3 = jnp.exp(a1 - lm), jnp.exp(a2 - lm), jnp.exp(a3 - lm)
        lden = e1 + e2 + e3
        ao = (e1 * o1[...] + e2 * o2[...] + e3 * o3[...]) / lden
        lt_ref[...] = lm + jnp.log(lden)
        za = za_ref[...]
        sza = _sigmoid(za)
        oa_v, ob_v = (ao * (za * sza)).astype(MXU_DTYPE), ob_ref[...]
        pa = _mm(oa_v, wa_ref[...])
        pb = _mm(ob_v, wb_ref[...])
        sa, sb = _sigmoid(ga_ref[...]), _sigmoid(gb_ref[...])
        ym = sa * pa + sb * pb
        u = _mm(ym, wo_ref[...])
        gate = mod_ref[2:3, :]
        fg = fg_ref[...]
        x2 = x_ref[...] + gate * u
        r2 = lax.rsqrt(jnp.mean(x2 * x2, axis=-1, keepdims=True) + EPS)
        xn2 = x2 * r2
        e = xn2 * fg - t_ref[...]
        dy = e * (1.0 / D)
        dn = dy * fg
        dx2 = r2 * (dn - xn2 * jnp.mean(dn * xn2, axis=-1, keepdims=True))
        dx2_ref[...] = dx2
        sums_ref[0:1, :] += jnp.sum(dy * xn2, axis=0, keepdims=True)
        sums_ref[1:2, :] += jnp.sum(dx2 * u, axis=0, keepdims=True)
        sums_ref[2:3, :] += (0.5 / D) * jnp.sum(e * e, axis=0, keepdims=True)
        du = dx2 * gate
        dym = _mm(du, wo_ref[...], NT)
        acc_o[...] += _mm(ym, du, TN)
        dpa, dpb = dym * sa, dym * sb
        dga_ref[...] = (dym * pa * (sa * (1.0 - sa))).astype(MXU_DTYPE)
        dgb_ref[...] = (dym * pb * (sb * (1.0 - sb))).astype(MXU_DTYPE)
        doa = _mm(dpa, wa_ref[...], NT)
        dza_ref[...] = (doa * ao * (sza * (1.0 + za * (1.0 - sza)))).astype(MXU_DTYPE)
        do = doa * (za * sza)
        do_ref[...] = do
        prod = do * ao
        for h in range(A_HEADS):
            sl = slice(A_HD * h, A_HD * (h + 1))
            dl_ref[:, sl] = jnp.broadcast_to(jnp.sum(prod[:, sl], axis=-1, keepdims=True), (tm, A_HD))
        dob_ref[...] = _mm(dpb, wb_ref[...], NT)
        acc_a[...] += _mm(oa_v, dpa, TN)
        acc_b[...] += _mm(ob_v, dpb, TN)

        @pl.when(i == nt - 1)
        def _():
            pltpu.sync_copy(acc_a, gwa_ref)
            pltpu.sync_copy(acc_b, gwb_ref)
            pltpu.sync_copy(acc_o, gwo_ref)

    row = lambda w: pl.BlockSpec((tm, w), lambda i: (i, 0))
    full = lambda a, b: pl.BlockSpec((a, b), lambda i: (0, 0))
    any_spec = pl.BlockSpec(memory_space=pl.ANY)
    return pl.pallas_call(
        body, name="tail",
        grid=(nt,),
        in_specs=[row(D), row(D)] + [row(A_W)] * 6 + [pl.BlockSpec((tm, A_W), lambda i: (i, 3)), row(D),
                  pl.BlockSpec((tm, D), lambda i: (i, 6)), pl.BlockSpec((tm, D), lambda i: (i, 7)),
                  full(8, D), full(1, D), full(A_W, D), full(D, D), full(D, D)],
        out_specs=[row(A_W), row(D), row(A_W), row(A_W), row(A_W), row(D), row(D), row(D), full(8, D),
                   any_spec, any_spec, any_spec],
        out_shape=[jax.ShapeDtypeStruct((S, A_W), F32),
                   jax.ShapeDtypeStruct((S, D), F32), jax.ShapeDtypeStruct((S, A_W), F32),
                   jax.ShapeDtypeStruct((S, A_W), F32), jax.ShapeDtypeStruct((S, A_W), MXU_DTYPE),
                   jax.ShapeDtypeStruct((S, D), F32), jax.ShapeDtypeStruct((S, D), MXU_DTYPE),
                   jax.ShapeDtypeStruct((S, D), MXU_DTYPE), jax.ShapeDtypeStruct((8, D), F32),
                   jax.ShapeDtypeStruct((A_W, D), F32), jax.ShapeDtypeStruct((D, D), F32),
                   jax.ShapeDtypeStruct((D, D), F32)],
        scratch_shapes=[pltpu.VMEM((A_W, D), F32), pltpu.VMEM((D, D), F32), pltpu.VMEM((D, D), F32)],
        compiler_params=_params(("arbitrary",)),
    )(x, target, *os, *ls, proj, ob, proj, proj, mod3, final_g, wa, wb, wo)


def _piece_parts(pieces):
    parts, where = [], []
    for k, piece in enumerate(pieces):
        off = 0
        for part in piece:
            parts.append(part)
            where.append((k, off, part.shape[1]))
            off += part.shape[1]
        assert off == D
    return parts, where


def _dh(pieces, w_in_g, x, dx2, mod3, norm_g, grads, tm=256):
    S = x.shape[0]
    ni = S // tm
    ng = len(grads)
    parts, where = _piece_parts(pieces)
    npart = len(parts)

    def body(*refs):
        p_refs = refs[:npart]
        w_ref, x_ref, dx2_ref, mod_ref, g_ref = refs[npart:npart + 5]
        g_ins = refs[npart + 5:npart + 5 + ng]
        gx_ref, sums_ref = refs[npart + 5 + ng:npart + 7 + ng]
        g_outs = refs[npart + 7 + ng:npart + 7 + 2 * ng]
        w_all, send_sems, recv_sems, local_sems = refs[npart + 7 + 2 * ng:]
        i = pl.program_id(0)
        start, wait = _all_to_all_copies(g_ins, g_outs, send_sems, recv_sems, local_sems)

        @pl.when(i == 0)
        def _():
            start()
            sums_ref[...] = jnp.zeros_like(sums_ref)
            pltpu.sync_copy(w_ref, w_all)

        dh = None
        for p_ref, (k, off, width) in zip(p_refs, where):
            term = _mm(p_ref[...], w_all[k, :, off:off + width], NT)
            dh = term if dh is None else dh + term
        xv = x_ref[...]
        g = g_ref[...]
        sc1 = 1.0 + mod_ref[1:2, :]
        r = lax.rsqrt(jnp.mean(xv * xv, axis=-1, keepdims=True) + EPS)
        xn = xv * r
        sums_ref[0:1, :] += jnp.sum(dh, axis=0, keepdims=True)
        sums_ref[1:2, :] += jnp.sum(dh * (xn * g), axis=0, keepdims=True)
        sums_ref[2:3, :] += jnp.sum(dh * sc1 * xn, axis=0, keepdims=True)
        dxn = dh * sc1 * g
        gx_ref[...] = dx2_ref[...] + r * (dxn - xn * jnp.mean(dxn * xn, axis=-1, keepdims=True))

        @pl.when(i == ni - 1)
        def _():
            wait()

    row = pl.BlockSpec((tm, D), lambda i: (i, 0))
    any_spec = pl.BlockSpec(memory_space=pl.ANY)
    return pl.pallas_call(
        body, name="dh_scatter",
        grid=(ni,),
        in_specs=[pl.BlockSpec((tm, width), lambda i: (i, 0)) for _, _, width in where]
                 + [any_spec, row, row,
                    pl.BlockSpec((8, D), lambda i: (0, 0)),
                    pl.BlockSpec((1, D), lambda i: (0, 0))]
                 + [any_spec] * ng,
        out_specs=[row, pl.BlockSpec((8, D), lambda i: (0, 0))] + [any_spec] * ng,
        out_shape=[jax.ShapeDtypeStruct((S, D), F32), jax.ShapeDtypeStruct((8, D), F32)]
                  + [jax.ShapeDtypeStruct(g.shape, g.dtype) for g in grads],
        scratch_shapes=[pltpu.VMEM(w_in_g.shape, w_in_g.dtype),
                        pltpu.SemaphoreType.DMA((ng, N_DEV - 1)), pltpu.SemaphoreType.DMA((ng, N_DEV - 1)),
                        pltpu.SemaphoreType.DMA((ng,))],
        compiler_params=_params(("arbitrary",)),
    )(*parts, w_in_g, x, dx2, mod3, norm_g, *grads)


def _gw_in(ht, pieces, grads, tm=1024):
    S = ht.shape[1]
    nt = S // tm
    ng = len(grads)
    parts, where = _piece_parts(pieces)
    npart = len(parts)

    def body(*refs):
        h_ref, p_refs = refs[0], refs[1:1 + npart]
        g_ins = refs[1 + npart:1 + npart + ng]
        o_ref = refs[1 + npart + ng]
        g_outs = refs[2 + npart + ng:2 + npart + 2 * ng]
        acc, send_sems, recv_sems, local_sems = refs[2 + npart + 2 * ng:]
        j, i = pl.program_id(0), pl.program_id(1)
        start, wait = _all_to_all_copies(g_ins, g_outs, send_sems, recv_sems, local_sems)

        @pl.when((j == 0) & (i == 0))
        def _():
            start()

        @pl.when(i == 0)
        def _():
            acc[...] = jnp.zeros_like(acc)

        for k in range(N_DEV):
            @pl.when(j == k)
            def _(k=k):
                for p_ref, (kk, off, width) in zip(p_refs, where):
                    if kk == k:
                        acc[:, off:off + width] += _mm(h_ref[...], p_ref[...])

        @pl.when(i == nt - 1)
        def _():
            o_ref[0] = acc[...].astype(XCHG_DTYPE)

        @pl.when((j == N_DEV - 1) & (i == nt - 1))
        def _():
            wait()

    def part_spec(k, width):
        return pl.BlockSpec((tm, width), lambda j, i: (jnp.where(j == k, i, 0), 0))

    any_spec = pl.BlockSpec(memory_space=pl.ANY)
    return pl.pallas_call(
        body, name="gw_in_scatter",
        grid=(N_DEV, nt),
        in_specs=[pl.BlockSpec((D, tm), lambda j, i: (0, i))] + [part_spec(k, width) for k, _, width in where] + [any_spec] * ng,
        out_specs=[pl.BlockSpec((1, D, D), lambda j, i: (j, 0, 0))] + [any_spec] * ng,
        out_shape=[jax.ShapeDtypeStruct((N_DEV, D, D), XCHG_DTYPE)]
                  + [jax.ShapeDtypeStruct(g.shape, g.dtype) for g in grads],
        scratch_shapes=[pltpu.VMEM((D, D), F32),
                        pltpu.SemaphoreType.DMA((ng, N_DEV - 1)), pltpu.SemaphoreType.DMA((ng, N_DEV - 1)),
                        pltpu.SemaphoreType.DMA((ng,))],
        compiler_params=_params(("arbitrary", "arbitrary")),
    )(ht, *parts, *grads)


def _adamw_math(w, g, m, v):
    m = ADAM_B1 * m + (1.0 - ADAM_B1) * g
    v = ADAM_B2 * v + (1.0 - ADAM_B2) * (g * g)
    m_hat = m / (1.0 - ADAM_B1 ** ADAM_STEP)
    v_hat = v / (1.0 - ADAM_B2 ** ADAM_STEP)
    delta = -ADAM_LR * (m_hat / (jnp.sqrt(v_hat) + ADAM_EPS) + ADAM_WD * w)
    return delta, m, v


def _adamw_big(recv, w, m, v, name, tr=128):
    M, N = w.shape
    tr = min(tr, M)

    def body(r_ref, w_ref, m_ref, v_ref, g_ref, d_ref, nm_ref, nv_ref):
        g = r_ref[0].astype(F32)
        for j in range(1, N_DEV):
            g = g + r_ref[j].astype(F32)
        g_ref[...] = g
        d_ref[...], nm_ref[...], nv_ref[...] = _adamw_math(w_ref[...], g, m_ref[...], v_ref[...])

    blk = pl.BlockSpec((tr, N), lambda i: (i, 0))
    return pl.pallas_call(
        body, name=name,
        grid=(M // tr,),
        in_specs=[pl.BlockSpec((N_DEV, tr, N), lambda i: (0, i, 0)), blk, blk, blk],
        out_specs=[blk] * 4,
        out_shape=[jax.ShapeDtypeStruct((M, N), F32)] * 4,
        compiler_params=_params(("parallel",)),
    )(recv, w, m, v)


def _adamw_w_ada(c64, dmod64, w, m, v):
    def body(c_ref, dm_ref, w_ref, m_ref, v_ref, g_ref, d_ref, nm_ref, nv_ref):
        cv = c_ref[...]
        g = _mm(cv * _sigmoid(cv), dm_ref[...], TN)
        g_ref[...] = g
        d_ref[...], nm_ref[...], nv_ref[...] = _adamw_math(w_ref[...], g, m_ref[...], v_ref[...])

    return pl.pallas_call(
        body, name="adamw_w_ada",
        out_shape=[jax.ShapeDtypeStruct(w.shape, F32)] * 4,
        compiler_params=_params(),
    )(c64, dmod64, w, m, v)


P_MOD, P_NORM, P_ONORM, P_RELB, P_LB, P_FINAL, P_LOSS, P_END = (0, 3 * D, 4 * D, 5 * D, 6 * D, 7 * D, 8 * D, 9 * D)


def _adamw_small(packed, b_ada, norm_g, onorm_g, relb, hgrn_lb, final_g, ms, vs):
    def body(pk_ref, b_ref, ng_ref, og_ref, rb_ref, lb_ref, fg_ref,
             mb, mn, mo, mr, ml, mf, vb, vn, vo, vr, vl, vf,
             loss_ref, gb, gn, go, gr, gl, gf, db, dn, do, dr, dl, df,
             nmb, nmn, nmo, nmr, nml, nmf, nvb, nvn, nvo, nvr, nvl, nvf):
        tot = pk_ref[0:1, :]
        for j in range(1, N_DEV):
            tot = tot + pk_ref[8 * j:8 * j + 1, :]
        loss_ref[...] = jnp.broadcast_to(jnp.sum(tot[:, P_LOSS:P_END], axis=-1, keepdims=True), (8, 128))

        def upd(g, w_ref, m_ref, v_ref, g_out, d_out, m_out, v_out):
            g_out[...] = g
            d_out[...], m_out[...], v_out[...] = _adamw_math(w_ref[...], g, m_ref[...], v_ref[...])

        upd(tot[:, P_MOD:P_NORM], b_ref, mb, vb, gb, db, nmb, nvb)
        upd(tot[:, P_NORM:P_ONORM], ng_ref, mn, vn, gn, dn, nmn, nvn)
        g_on = tot[:, P_ONORM:P_ONORM + G_DK]
        for h in range(1, G_HEADS):
            g_on = g_on + tot[:, P_ONORM + G_DK * h:P_ONORM + G_DK * (h + 1)]
        upd(g_on, og_ref, mo, vo, go, do, nmo, nvo)
        upd(tot[:, P_RELB:P_LB], rb_ref, mr, vr, gr, dr, nmr, nvr)
        a = lb_ref[...]
        lb = _sigmoid(a[0:1, :] - a[1:2, :])
        g0 = tot[:, P_LB:P_FINAL] * lb * (1.0 - lb)
        row = lax.broadcasted_iota(jnp.int32, (2, D), 0)
        upd(jnp.where(row == 0, g0, -g0), lb_ref, ml, vl, gl, dl, nml, nvl)
        upd(tot[:, P_FINAL:P_LOSS], fg_ref, mf, vf, gf, df, nmf, nvf)

    shapes = [b_ada.shape, norm_g.shape, onorm_g.shape, relb.shape, hgrn_lb.shape, final_g.shape]
    outs = [jax.ShapeDtypeStruct((8, 128), F32)] + [jax.ShapeDtypeStruct(s, F32) for s in shapes] * 4
    return pl.pallas_call(
        body, name="adamw_small",
        out_shape=outs,
        compiler_params=_params(),
    )(packed, b_ada, norm_g, onorm_g, relb, hgrn_lb, final_g, *ms, *vs)


def _local_step(x, target, mod3, norm_g, w_in_g, onorm_g, wa_blk, wb_blk, wo_blk, rel_bias, hgrn_lb, final_g):
    buckets = jnp.asarray(_bucket_tables())
    bias = _bias_tables(rel_bias, buckets)
    proj, ht, qkv, wa_g, wb_g, wo_g = _inproj(x, mod3, norm_g, w_in_g, [wa_blk, wb_blk, wo_blk])
    wa = wa_g.transpose(1, 0, 2).reshape(A_W, D)
    wb = wb_g.reshape(D, D)
    wo = wo_g.reshape(D, D)
    os, ls = [], []
    for p, (_, d) in enumerate(PATTERNS):
        o, l = _attn_fwd(qkv, bias[p], d, "attn_fwd_d%d" % d)
        os.append(o)
        ls.append(l)
    o_raw, ob, states = _hgrn_fwd(proj, hgrn_lb, onorm_g)
    lt, dx2, do, delta, dza, dob, dga, dgb, tsums, gwa, gwb, gwo = _tail(
        x, target, os, ls, ob, proj, mod3, final_g, wa, wb, wo)
    dbs, acc = [None] * len(PATTERNS), ()
    for p in reversed(range(len(PATTERNS))):
        d = PATTERNS[p][1]
        *acc, dbs[p] = _attn_bwd(qkv, do, lt, delta, bias[p], d, "attn_bwd_d%d" % d, prev=tuple(acc),
                                 out_dtype=MXU_DTYPE if p == 0 else F32)
    dqa, dka, dva = acc
    g_relb = _rel_bias_grad(dbs, buckets)
    dqb, dfb, dib, dzb, dlb, dgo = _hgrn_bwd(proj, o_raw, dob, states, hgrn_lb, onorm_g)
    pieces = [[dqa, dka], [dva, dza], [dqb], [dfb], [dib], [dzb], [dga], [dgb]]
    small = [gwa.astype(XCHG_DTYPE).reshape(A_W, N_DEV, D // N_DEV).transpose(1, 0, 2),
             gwb.astype(XCHG_DTYPE).reshape(N_DEV, D // N_DEV, D),
             gwo.astype(XCHG_DTYPE).reshape(N_DEV, D // N_DEV, D)]
    gw_in, *received_small = _gw_in(ht, pieces, small)
    gx, hsums, received_in = _dh(pieces, w_in_g, x, dx2, mod3, norm_g, [gw_in])
    received = [received_in] + received_small
    row = jnp.concatenate([
        hsums[0], hsums[1], tsums[1],
        hsums[2],
        dgo.reshape(G_HEADS, 8, G_DK)[:, 0].reshape(-1),
        g_relb.reshape(-1),
        dlb[0],
        tsums[0],
        tsums[2],
    ])
    return gx, received, row


def kernel(x, c, w_ada, b_ada, norm_g, w_in, hgrn_onorm_g, w_branch_a, w_branch_b, w_out, rel_bias, hgrn_lb, final_g, loss_target, m_w_ada, m_b_ada, m_norm_g, m_w_in, m_hgrn_onorm_g, m_w_branch_a, m_w_branch_b, m_w_out, m_rel_bias, m_hgrn_lb, m_final_g, v_w_ada, v_b_ada, v_norm_g, v_w_in, v_hgrn_onorm_g, v_w_branch_a, v_w_branch_b, v_w_out, v_rel_bias, v_hgrn_lb, v_final_g):
    me = 4 * lax.axis_index("x") + 2 * lax.axis_index("y") + lax.axis_index("c")
    n_ada = w_ada.shape[2]

    w_in_g, c_all = _all_gather([w_in[0].astype(MXU_DTYPE), jnp.broadcast_to(c, (8, D))], "gather_w_in_c")

    c64 = c_all.reshape(8 * N_DEV, D)
    b_loc = lax.dynamic_slice(b_ada, (0, me * n_ada), (1, n_ada))
    mod_part = _mod_fwd(c64, w_ada[0], b_loc)[::8]
    (mod_all,) = _all_gather([mod_part], "gather_mod")
    mod = lax.dynamic_slice(mod_all, (0, me, 0), (N_DEV, 1, n_ada)).reshape(3, D)
    mod3 = jnp.concatenate([mod, jnp.zeros((5, D), F32)], axis=0)

    onorm_t = hgrn_onorm_g
    gx, (r_in, r_a, r_b, r_o), row = _local_step(
        x[0], loss_target[0], mod3, norm_g, w_in_g, onorm_t, w_branch_a[0].astype(MXU_DTYPE),
        w_branch_b[0].astype(MXU_DTYPE), w_out[0].astype(MXU_DTYPE), rel_bias, hgrn_lb,
        final_g.reshape(1, D))
    packed8 = jnp.concatenate([row[None, :], jnp.zeros((7, P_END), F32)], axis=0)
    (packed,) = _all_gather([packed8], "gather_small")
    packed = packed.reshape(8 * N_DEV, P_END)

    g_in, d_in, nm_in, nv_in = _adamw_big(r_in, w_in[0], m_w_in[0], v_w_in[0], "adamw_w_in")
    g_a, d_a, nm_a, nv_a = _adamw_big(r_a, w_branch_a[0], m_w_branch_a[0], v_w_branch_a[0], "adamw_w_branch_a")
    g_b, d_b, nm_b, nv_b = _adamw_big(r_b, w_branch_b[0], m_w_branch_b[0], v_w_branch_b[0], "adamw_w_branch_b")
    g_o, d_o, nm_o, nv_o = _adamw_big(r_o, w_out[0], m_w_out[0], v_w_out[0], "adamw_w_out")

    dmod64 = lax.dynamic_slice(packed, (0, P_MOD + me * n_ada), (8 * N_DEV, n_ada))
    g_ada, d_ada, nm_ada, nv_ada = _adamw_w_ada(c64, dmod64, w_ada[0], m_w_ada[0], v_w_ada[0])

    def flat_relb(t):
        return jnp.pad(t.T, ((0, 0), (0, 128 - N_BUCKETS))).reshape(1, A_HEADS * 128)

    def unflat_relb(t):
        return t.reshape(A_HEADS, 128)[:, :N_BUCKETS].T

    fg2 = lambda t: t.reshape(1, D)
    smalls = _adamw_small(
        packed, b_ada, norm_g, hgrn_onorm_g, flat_relb(rel_bias), hgrn_lb, fg2(final_g),
        [m_b_ada, m_norm_g, m_hgrn_onorm_g, flat_relb(m_rel_bias), m_hgrn_lb, fg2(m_final_g)],
        [v_b_ada, v_norm_g, v_hgrn_onorm_g, flat_relb(v_rel_bias), v_hgrn_lb, fg2(v_final_g)])
    loss = smalls[0][0, 0]

    def small(kind):
        s = smalls[1 + 6 * kind:7 + 6 * kind]
        return s[0], s[1], s[2], unflat_relb(s[3]), s[4], s[5].reshape(D)

    def leaves(ada, sm, w_in_, wa_, wb_, wo_):
        b_, n_, o_, r_, l_, f_ = sm
        return (ada[None], b_, n_, w_in_[None], o_, wa_[None], wb_[None], wo_[None], r_, l_, f_)

    return (loss, gx[None],
            *leaves(g_ada, small(0), g_in, g_a, g_b, g_o),
            *leaves(d_ada, small(1), d_in, d_a, d_b, d_o),
            *leaves(nm_ada, small(2), nm_in, nm_a, nm_b, nm_o),
            *leaves(nv_ada, small(3), nv_in, nv_a, nv_b, nv_o))
```

```python
import functools
import math

import numpy as np
import jax
import jax.numpy as jnp
from jax import lax
from jax.experimental import pallas as pl
from jax.experimental.pallas import tpu as pltpu

F32 = jnp.float32
BF16 = jnp.bfloat16
MXU_DTYPE = jnp.bfloat16
XCHG_DTYPE = jnp.bfloat16

N_DEV = 8
D = 1024
A_HEADS = 8
A_HD = 64
A_W = A_HEADS * A_HD
A_BLK = 128
PATTERNS = ((128, 1), (512, 4), (2048, 16))
N_BUCKETS = 32
MAX_DISTANCE = 2048
NEG = -1e30
G_HEADS = 8
G_DK = 128
G_W = G_HEADS * G_DK
IN_W = 8 * D
EPS = 1e-6
ADAM_LR = 0.001
ADAM_B1 = 0.9
ADAM_B2 = 0.999
ADAM_EPS = 1e-08
ADAM_WD = 0.01
ADAM_STEP = 10

G_CHUNK = 128
G_SUB = 8
G_HPS_FWD = 8
G_HPS_BWD = 8
G_RB = 16
VMEM_LIMIT = 56 * 1024 * 1024

NN = (((1,), (0,)), ((), ()))
NT = (((1,), (1,)), ((), ()))
TN = (((0,), (0,)), ((), ()))
MESH = pl.DeviceIdType.MESH


def _mm(a, b, dims=NN):
    return lax.dot_general(a.astype(MXU_DTYPE), b.astype(MXU_DTYPE), dims,
                           preferred_element_type=F32)


def _mm_exact(t, x):
    hi = x.astype(BF16)
    r = x - hi.astype(F32)
    mid = r.astype(BF16)
    lo = (r - mid.astype(F32)).astype(BF16)
    tb = t.astype(BF16)
    return sum(lax.dot_general(tb, p, NN, preferred_element_type=F32) for p in (hi, mid, lo))


def _split(x):
    hi = x.astype(BF16)
    return hi, (x - hi.astype(F32)).astype(BF16)


def _mm_split(a, b, dims):
    dot = lambda p, q: lax.dot_general(p, q, dims, preferred_element_type=F32)
    return dot(a[0], b[0]) + dot(a[0], b[1]) + dot(a[1], b[0])


def _sigmoid(x):
    return 0.5 * jnp.tanh(0.5 * x) + 0.5


def _params(sem=None):
    return pltpu.CompilerParams(dimension_semantics=sem, vmem_limit_bytes=VMEM_LIMIT)


def _all_gather(xs, name):
    n = len(xs)

    def body(*refs):
        ins, outs = refs[:n], refs[n:2 * n]
        send_sems, recv_sems, local_sems = refs[2 * n:]
        x, y, c = lax.axis_index("x"), lax.axis_index("y"), lax.axis_index("c")
        me, sibling = (x, y, c), (x, y, 1 - c)
        chips = [(1 - x, y), (x, 1 - y), (1 - x, 1 - y)]

        def slot(ref, dev):
            return ref.at[4 * dev[0] + 2 * dev[1] + dev[2]]

        def copy(a, k, block, to, src=None):
            return pltpu.make_async_remote_copy(
                src_ref=slot(outs[a], block) if src is None else src,
                dst_ref=slot(outs[a], block),
                send_sem=send_sems.at[a, k], recv_sem=recv_sems.at[a, k],
                device_id=to, device_id_type=MESH)

        mine, first, passed = [], [], []
        for a in range(n):
            cp = pltpu.make_async_copy(ins[a], slot(outs[a], me), local_sems.at[a])
            cp.start()
            mine.append(cp)
            first.append(copy(a, 0, me, sibling, src=ins[a]))
            for j, chip in enumerate(chips):
                first.append(copy(a, 1 + j, me, (*chip, c), src=ins[a]))
        for cp in first:
            cp.start()
        for j, chip in enumerate(chips):
            for a in range(n):
                copy(a, 1 + j, (*chip, c), me).wait_recv()
                cp = copy(a, 4 + j, (*chip, c), sibling)
                cp.start()
                passed.append(cp)
        for a in range(n):
            copy(a, 0, sibling, me).wait_recv()
            for j, chip in enumerate(chips):
                copy(a, 4 + j, (*chip, 1 - c), me).wait_recv()
        for cp in first + passed:
            cp.wait_send()
        for cp in mine:
            cp.wait()

    any_spec = pl.BlockSpec(memory_space=pl.ANY)
    return pl.pallas_call(
        body, name=name,
        out_shape=[jax.ShapeDtypeStruct((N_DEV,) + v.shape, v.dtype) for v in xs],
        in_specs=[any_spec] * n, out_specs=[any_spec] * n,
        scratch_shapes=[pltpu.SemaphoreType.DMA((n, 7)), pltpu.SemaphoreType.DMA((n, 7)),
                        pltpu.SemaphoreType.DMA((n,))],
    )(*xs)


def _all_to_all_copies(ins, outs, send_sems, recv_sems, local_sems, gather=False):
    n = len(ins)
    x, y, c = lax.axis_index("x"), lax.axis_index("y"), lax.axis_index("c")
    me = 4 * x + 2 * y + c
    peers = []
    for m in range(1, N_DEV):
        peers.append((1 - x if m & 4 else x, 1 - y if m & 2 else y, 1 - c if m & 1 else c))

    def chunk(a, j):
        return ins[a] if gather else ins[a].at[j]

    def copy(a, k, landing):
        peer = peers[k]
        pid = 4 * peer[0] + 2 * peer[1] + peer[2]
        return pltpu.make_async_remote_copy(
            src_ref=chunk(a, pid), dst_ref=outs[a].at[pid if landing else me],
            send_sem=send_sems.at[a, k], recv_sem=recv_sems.at[a, k],
            device_id=peer, device_id_type=MESH)

    def local(a):
        return pltpu.make_async_copy(chunk(a, me), outs[a].at[me], local_sems.at[a])

    def start():
        for a in range(n):
            local(a).start()
        for k in range(N_DEV - 1):
            for a in range(n):
                copy(a, k, False).start()

    def wait():
        for k in range(N_DEV - 1):
            for a in range(n):
                copy(a, k, True).wait_recv()
        for k in range(N_DEV - 1):
            for a in range(n):
                copy(a, k, False).wait_send()
        for a in range(n):
            local(a).wait()

    return start, wait


def _mod_fwd(c64, w_ada, b_loc):
    def body(c_ref, w_ref, b_ref, o_ref):
        cv = c_ref[...]
        sc = cv * _sigmoid(cv)
        o_ref[...] = _mm(sc, w_ref[...]) + b_ref[...]

    return pl.pallas_call(
        body, name="mod_fwd",
        out_shape=jax.ShapeDtypeStruct((c64.shape[0], w_ada.shape[1]), F32),
        compiler_params=_params(),
    )(c64, w_ada, b_loc)


def _inproj(x, mod3, norm_g, w_in_g, blocks, tm=256):
    S = x.shape[0]
    ni = S // tm
    nb = len(blocks)

    def body(*refs):
        x_ref, mod_ref, g_ref, w_ref = refs[:4]
        b_ins = refs[4:4 + nb]
        proj_ref, ht_ref, qkv_ref = refs[4 + nb:7 + nb]
        b_outs = refs[7 + nb:7 + 2 * nb]
        w_all, send_sems, recv_sems, local_sems = refs[7 + 2 * nb:]
        i = pl.program_id(0)
        start, wait = _all_to_all_copies(b_ins, b_outs, send_sems, recv_sems, local_sems, gather=True)

        @pl.when(i == 0)
        def _():
            start()
            pltpu.sync_copy(w_ref, w_all)

        xv = x_ref[...]
        r = lax.rsqrt(jnp.mean(xv * xv, axis=-1, keepdims=True) + EPS)
        h = ((xv * r * g_ref[...]) * (1.0 + mod_ref[1:2, :]) + mod_ref[0:1, :]).astype(MXU_DTYPE)
        ht_ref[...] = h.T
        for j in range(N_DEV):
            pj = _mm(h, w_all[j])
            proj_ref[:, j * D:(j + 1) * D] = pj
            for c in range(3):
                if c // 2 == j:
                    for p in range(A_HEADS // 2):
                        lo = (c % 2) * A_W + 2 * A_HD * p
                        qkv_ref[c, p] = pj[:, lo:lo + 2 * A_HD]

        @pl.when(i == ni - 1)
        def _():
            wait()

    any_spec = pl.BlockSpec(memory_space=pl.ANY)
    return pl.pallas_call(
        body, name="inproj_gather",
        grid=(ni,),
        in_specs=[pl.BlockSpec((tm, D), lambda i: (i, 0)),
                  pl.BlockSpec((8, D), lambda i: (0, 0)),
                  pl.BlockSpec((1, D), lambda i: (0, 0)),
                  any_spec] + [any_spec] * nb,
        out_specs=[pl.BlockSpec((tm, IN_W), lambda i: (i, 0)),
                   pl.BlockSpec((D, tm), lambda i: (0, i)),
                   pl.BlockSpec((3, A_HEADS // 2, tm, 2 * A_HD), lambda i: (0, 0, i, 0))] + [any_spec] * nb,
        out_shape=[jax.ShapeDtypeStruct((S, IN_W), F32), jax.ShapeDtypeStruct((D, S), MXU_DTYPE),
                   jax.ShapeDtypeStruct((3, A_HEADS // 2, S, 2 * A_HD), F32)]
                  + [jax.ShapeDtypeStruct((N_DEV,) + b.shape, b.dtype) for b in blocks],
        scratch_shapes=[pltpu.VMEM(w_in_g.shape, w_in_g.dtype),
                        pltpu.SemaphoreType.DMA((nb, N_DEV - 1)), pltpu.SemaphoreType.DMA((nb, N_DEV - 1)),
                        pltpu.SemaphoreType.DMA((nb,))],
        compiler_params=_params(("arbitrary",)),
    )(x, mod3, norm_g, w_in_g, *blocks)


def _bucket_tables():
    qi = np.arange(A_BLK)[:, None]
    kj = np.arange(2 * A_BLK)[None, :]
    delta = qi + A_BLK - kj
    out = []
    for window, dil in PATTERNS:
        span = window // dil
        band = (delta >= 0) & (delta <= span)
        dist = np.clip(delta, 0, None) * dil
        max_exact = N_BUCKETS // 2
        nf = dist.astype(np.float32)
        large = max_exact + (np.log(np.maximum(nf, np.float32(1.0)) / np.float32(max_exact))
                             / np.float32(math.log(MAX_DISTANCE / max_exact))
                             * np.float32(N_BUCKETS - max_exact)).astype(np.int32)
        large = np.minimum(large, N_BUCKETS - 1)
        bucket = np.where(dist < max_exact, dist, large)
        out.append(np.where(band, bucket, -1).astype(np.int32))
    return np.stack(out)


def _bias_tables(rel_bias, buckets):
    def body(rb_ref, bk_ref, o_ref):
        bk = bk_ref[0]
        for h in range(A_HEADS):
            acc = jnp.full(bk.shape, NEG, F32)
            for b in range(N_BUCKETS):
                acc = jnp.where(bk == b, rb_ref[b, h], acc)
            o_ref[0, h] = acc

    return pl.pallas_call(
        body, name="bias_tables",
        grid=(len(PATTERNS),),
        in_specs=[pl.BlockSpec(memory_space=pltpu.SMEM),
                  pl.BlockSpec((1, A_BLK, 2 * A_BLK), lambda p: (p, 0, 0))],
        out_specs=pl.BlockSpec((1, A_HEADS, A_BLK, 2 * A_BLK), lambda p: (p, 0, 0, 0)),
        out_shape=jax.ShapeDtypeStruct((len(PATTERNS), A_HEADS, A_BLK, 2 * A_BLK), F32),
        compiler_params=_params(("arbitrary",)),
    )(rel_bias, buckets)


A_TILES = 32


def _attn_heads_per_step(d):
    return A_HEADS if d == 1 else 2


def _attn_in_specs(sb, nsb, hw):
    blk = (1, hw // 2, sb, 2 * A_HD)

    def cur(c):
        return pl.BlockSpec(blk, lambda hp, n: (c, hp, jnp.minimum(n, nsb - 1), 0))

    def prev(c):
        return pl.BlockSpec(blk, lambda hp, n: (c, hp, jnp.maximum(jnp.minimum(n, nsb - 1) - 1, 0), 0))

    return [cur(0), prev(1), cur(1), prev(2), cur(2)]


def _rows(r, d):
    return pl.ds(r, A_BLK) if d == 1 else pl.ds(r, A_BLK, stride=d)


def _for_residues(d, hw, fn):
    unroll = min(d, max(1, A_TILES // hw))
    if d == unroll:
        _round_robin([g for r in range(d) for g in fn(r)])
    else:
        def group(g, c):
            _round_robin([t for u in range(unroll) for t in fn(g * unroll + u)])
            return c
        lax.fori_loop(0, d // unroll, group, 0)


def _attn_stack(t):
    first_half = lax.broadcasted_iota(jnp.int32, (1, 2 * A_HD), 1) < A_HD
    return jnp.concatenate([jnp.where(first_half, t, 0.0), jnp.where(first_half, 0.0, t)], axis=0)


def _attn_unstack(t2):
    first_half = lax.broadcasted_iota(jnp.int32, (1, 2 * A_HD), 1) < A_HD
    return jnp.where(first_half, t2[:A_BLK], t2[A_BLK:])


def _attn_scores(q, k, b_ref, pp, first):
    bias = jnp.concatenate([b_ref[2 * pp] + first, b_ref[2 * pp + 1] + first], axis=0)
    return _mm(_attn_stack(q), k, NT) * (A_HD ** -0.5) + bias


def _attn_fwd(qkv, bias_p, d, name):
    S = qkv.shape[2]
    hw = _attn_heads_per_step(d)
    sub = A_BLK * d
    nsub = max(1, 2 * A_TILES // (hw * d))
    sb = sub * nsub
    nsb = S // sb

    def body(q_ref, kp_ref, kc_ref, vp_ref, vc_ref, b_ref, o_ref, l_ref):
        n = pl.program_id(1)
        kj = lax.broadcasted_iota(jnp.int32, (A_BLK, 2 * A_BLK), 1)
        first = jnp.where((n == 0) & (kj < A_BLK), NEG, 0.0).astype(F32)
        no_first = jnp.zeros((A_BLK, 2 * A_BLK), F32)

        def residue(r, u=0):
            rows = _rows(u * sub + r, d)
            behind = _rows(((nsub if u == 0 else u) - 1) * sub + r, d)

            def pair(pp):
                lanes = pl.ds(2 * A_HD * pp, 2 * A_HD)
                kc, vc = kc_ref.at[0, pp], vc_ref.at[0, pp]
                kb, vb = (kp_ref.at[0, pp], vp_ref.at[0, pp]) if u == 0 else (kc, vc)
                k = jnp.concatenate([kb[behind, :], kc[rows, :]], axis=0)
                v = jnp.concatenate([vb[behind, :], vc[rows, :]], axis=0)
                s = _attn_scores(q_ref.at[0, pp][rows, :], k, b_ref, pp, first if u == 0 else no_first)
                yield
                m = jnp.max(s, axis=-1, keepdims=True)
                p = jnp.exp(s - m)
                den = jnp.sum(p, axis=-1, keepdims=True)
                pv = _mm(p, v)
                yield
                o_ref[rows, lanes] = _attn_unstack(pv / den)
                l_ref[rows, lanes] = _attn_unstack(jnp.broadcast_to(m + jnp.log(den), (2 * A_BLK, 2 * A_HD)))

            return [pair(pp) for pp in range(hw // 2)]

        if nsub == 1:
            _for_residues(d, hw, residue)
        else:
            _round_robin([g for u in range(nsub) for r in range(d) for g in residue(r, u)])

    out = pl.BlockSpec((sb, A_HD * hw), lambda hp, n: (n, hp))
    return pl.pallas_call(
        body, name=name,
        grid=(A_HEADS // hw, nsb),
        in_specs=_attn_in_specs(sb, nsb, hw) + [pl.BlockSpec((hw, A_BLK, 2 * A_BLK), lambda hp, n: (hp, 0, 0))],
        out_specs=[out, out],
        out_shape=[jax.ShapeDtypeStruct((S, A_W), F32)] * 2,
        compiler_params=_params(("parallel", "parallel")),
    )(qkv, qkv, qkv, qkv, qkv, bias_p)


def _attn_bwd(qkv, do, lt, delta, bias_p, d, name, prev=(), out_dtype=F32):
    S = qkv.shape[2]
    hw = _attn_heads_per_step(d)
    sub = A_BLK * d
    nsub = max(1, A_TILES // (hw * d))
    sb = sub * nsub
    nsb = S // sb
    done = (nsub - 1) * sub

    def body(*refs):
        q_ref, kp_ref, kc_ref, vp_ref, vc_ref, do_ref, lt_ref, dl_ref, b_ref = refs[:9]
        pq_ref, pk_ref, pv_ref = refs[9:9 + len(prev)] if prev else (None, None, None)
        dq_ref, dk_ref, dv_ref, db_ref, ck, cv = refs[9 + len(prev):]
        n = pl.program_id(1)
        plus = lambda t, p_ref, idx: (t if p_ref is None else t + p_ref[idx]).astype(out_dtype)

        @pl.when(n == 0)
        def _():
            db_ref[...] = jnp.zeros_like(db_ref)
            ck[...] = jnp.zeros_like(ck)
            cv[...] = jnp.zeros_like(cv)

        @pl.when(n < nsb)
        def _():
            kj = lax.broadcasted_iota(jnp.int32, (A_BLK, 2 * A_BLK), 1)
            first = jnp.where((n == 0) & (kj < A_BLK), NEG, 0.0).astype(F32)
            no_first = jnp.zeros((A_BLK, 2 * A_BLK), F32)
            if done:
                dk_ref[0:done, :] = plus(ck[0:done, :], pk_ref, (slice(0, done), slice(None)))
                dv_ref[0:done, :] = plus(cv[0:done, :], pv_ref, (slice(0, done), slice(None)))

            def residue(r, u=0):
                rows = _rows(u * sub + r, d)
                behind = _rows(((nsub if u == 0 else u) - 1) * sub + r, d)

                def pair(pp):
                    lanes = pl.ds(2 * A_HD * pp, 2 * A_HD)
                    lt_r, dl_r = lt_ref[rows, lanes], dl_ref[rows, lanes]
                    kc, vc = kc_ref.at[0, pp], vc_ref.at[0, pp]
                    kb, vb = (kp_ref.at[0, pp], vp_ref.at[0, pp]) if u == 0 else (kc, vc)
                    k = jnp.concatenate([kb[behind, :], kc[rows, :]], axis=0)
                    v = jnp.concatenate([vb[behind, :], vc[rows, :]], axis=0)
                    q2 = _attn_stack(q_ref.at[0, pp][rows, :])
                    do2 = _attn_stack(do_ref[rows, lanes])
                    col = lambda t: jnp.concatenate([t[:, 0:1], t[:, A_HD:A_HD + 1]], axis=0)
                    s = _attn_scores(q_ref.at[0, pp][rows, :], k, b_ref, pp, first if u == 0 else no_first)
                    dp = _mm(do2, v, NT)
                    yield
                    p = jnp.exp(s - col(lt_r))
                    ds = p * (dp - col(dl_r))
                    db_ref[2 * pp] += ds[:A_BLK]
                    db_ref[2 * pp + 1] += ds[A_BLK:]
                    dq = _mm(ds, k)
                    dk = _mm(ds, q2, TN) * (A_HD ** -0.5)
                    dv = _mm(p, do2, TN)
                    yield
                    dq_ref[rows, lanes] = plus(_attn_unstack(dq) * (A_HD ** -0.5), pq_ref, (rows, lanes))
                    if u == 0:
                        dk_ref[behind, lanes] = plus(ck[behind, lanes] + dk[:A_BLK], pk_ref, (behind, lanes))
                        dv_ref[behind, lanes] = plus(cv[behind, lanes] + dv[:A_BLK], pv_ref, (behind, lanes))
                    else:
                        ck[behind, lanes] += dk[:A_BLK]
                        cv[behind, lanes] += dv[:A_BLK]
                    ck[rows, lanes] = dk[A_BLK:]
                    cv[rows, lanes] = dv[A_BLK:]

                return [pair(pp) for pp in range(hw // 2)]

            if nsub == 1:
                _for_residues(d, hw, residue)
            else:
                _round_robin([g for u in range(nsub) for r in range(d) for g in residue(r, u)])

        @pl.when(n == nsb)
        def _():
            dk_ref[...] = plus(ck[...], pk_ref, ...)
            dv_ref[...] = plus(cv[...], pv_ref, ...)

    w = A_HD * hw
    row = pl.BlockSpec((sb, w), lambda hp, n: (jnp.minimum(n, nsb - 1), hp))
    lag = pl.BlockSpec((sb, w), lambda hp, n: (jnp.maximum(n - 1, 0), hp))
    tab = pl.BlockSpec((hw, A_BLK, 2 * A_BLK), lambda hp, n: (hp, 0, 0))
    return pl.pallas_call(
        body, name=name,
        grid=(A_HEADS // hw, nsb + 1),
        in_specs=_attn_in_specs(sb, nsb, hw) + [row, row, row, tab] + ([row, lag, lag] if prev else []),
        out_specs=[row, lag, lag, tab],
        out_shape=[jax.ShapeDtypeStruct((S, A_W), out_dtype)] * 3
                  + [jax.ShapeDtypeStruct((A_HEADS, A_BLK, 2 * A_BLK), F32)],
        scratch_shapes=[pltpu.VMEM((sb, w), F32), pltpu.VMEM((sb, w), F32)],
        compiler_params=_params(("parallel", "arbitrary")),
    )(qkv, qkv, qkv, qkv, qkv, do, lt, delta, bias_p, *prev)


def _rel_bias_grad(dbs, buckets):
    def body(d1, d2, d3, bk_ref, o_ref):
        row = lax.broadcasted_iota(jnp.int32, (A_HEADS, 128), 0)
        lane = lax.broadcasted_iota(jnp.int32, (A_HEADS, 128), 1)
        acc = jnp.zeros((A_HEADS, 128), F32)
        for p, dref in enumerate((d1, d2, d3)):
            bk = bk_ref[p]
            for h in range(A_HEADS):
                ds = dref[h]
                for b in range(N_BUCKETS):
                    s = jnp.sum(jnp.where(bk == b, ds, 0.0), keepdims=True)
                    acc = acc + jnp.where((row == h) & (lane == b), s, 0.0)
        o_ref[...] = acc

    return pl.pallas_call(
        body, name="rel_bias_grad",
        out_shape=jax.ShapeDtypeStruct((A_HEADS, 128), F32),
        compiler_params=_params(),
    )(*dbs, buckets)


def _tri(c):
    t = np.tril(np.ones((c, c), np.float32))
    return jnp.asarray(t), jnp.asarray(t.T.copy())


def _fill_above(ref, x, pad):
    ref[0:G_SUB, :] = jnp.full((G_SUB, x.shape[1]), pad, F32)
    ref[G_SUB:, :] = x


def _fill_below(ref, x, pad):
    ref[0:x.shape[0], :] = x
    ref[x.shape[0]:, :] = jnp.full((G_SUB, x.shape[1]), pad, F32)


def _hgrn_gates(q_ref, f_ref, lbp_ref, tri_ref):
    qraw = q_ref[...]
    sq = _sigmoid(qraw)
    q = qraw * sq
    sg = _sigmoid(f_ref[...])
    lb = _sigmoid(lbp_ref[0:1, :] - lbp_ref[1:2, :])
    f = lb + (1.0 - lb) * sg
    k = 1.0 - f
    b = _mm_exact(tri_ref[...], jnp.log(f))
    return qraw, sq, q, sg, lb, f, k, b


def _hgrn_col(C, base, idx, hps):
    return pl.BlockSpec((C, hps * G_DK), lambda h, n: (idx(n), base * (G_HEADS // hps) + h))


def _round_robin(stages):
    live = list(stages)
    while live:
        nxt = []
        for g in live:
            try:
                next(g)
                nxt.append(g)
            except StopIteration:
                pass
        live = nxt


def _hgrn_levels(C):
    out, m = [], G_SUB
    while 2 * m <= C:
        out.append(m)
        m *= 2
    return out


def _hgrn_level_masks(C):
    ti = np.arange(C)[:, None]
    si = np.arange(C)[None, :]
    return jnp.asarray(np.stack([((ti // (2 * m) == si // (2 * m)) & (ti - si >= G_SUB)).astype(np.float32)
                                 for m in _hgrn_levels(C)]))


def _hgrn_level(b, q, k, C, m):
    zeros = jnp.zeros((m, G_DK), F32)
    eq, ek, qt, kt = [], [], [], []
    for blk in range(0, C // m, 2):
        lo, mid, hi = blk * m, (blk + 1) * m, (blk + 2) * m
        ref = b[mid:mid + 1]
        e_right = jnp.exp(b[mid:hi] - ref)
        e_left = jnp.exp(ref - b[lo:mid])
        eq += [zeros, e_right]
        ek += [e_left, zeros]
        qt += [zeros, q[mid:hi] * e_right]
        kt += [k[lo:mid] * e_left, zeros]
    cat = lambda parts: jnp.concatenate(parts, axis=0)
    return cat(qt), cat(kt), cat(eq), cat(ek)


def _hgrn_fwd(proj, hgrn_lb, onorm_g, C=G_CHUNK):
    S = proj.shape[0]
    nc = S // C
    tri, _ = _tri(C)
    masks = _hgrn_level_masks(C)
    hps = G_HPS_FWD

    def body(q_ref, f_ref, i_ref, z_ref, lbp_ref, go_ref, tri_ref, pm_ref, o_ref, ob_ref, st_ref, St, kp, vp, fp):
        @pl.when(pl.program_id(1) == 0)
        def _():
            St[...] = jnp.zeros_like(St)

        heads = []
        for hh in range(hps):
            ln = pl.ds(G_DK * hh, G_DK)
            heads.append(head(
                q_ref.at[:, ln], f_ref.at[:, ln], i_ref.at[:, ln], z_ref.at[:, ln], lbp_ref.at[:, ln], go_ref,
                tri_ref, pm_ref, o_ref.at[:, ln], ob_ref.at[:, ln], st_ref.at[0, hh], St.at[hh], kp.at[hh], vp.at[hh],
                fp.at[hh]))
        _round_robin(heads)

    def head(q_ref, f_ref, i_ref, z_ref, lbp_ref, go_ref, tri_ref, pm_ref, o_ref, ob_ref, st_ref, St, kp, vp, fp):
        _, _, q, _, _, f, k, b = _hgrn_gates(q_ref, f_ref, lbp_ref, tri_ref)
        v = i_ref[...]
        bC = b[C - 1:C, :]
        S0 = St[...]
        o = _mm(q * jnp.exp(b), S0, NT)
        yield
        _fill_above(kp, k, 0.0)
        _fill_above(vp, v, 0.0)
        _fill_above(fp, f, 1.0)
        near = []
        for r0 in range(0, C, G_RB):
            qb = q[r0:r0 + G_RB]
            acc = e = None
            for l in range(G_SUB):
                rows = pl.ds(G_SUB - l + r0, G_RB)
                if l > 0:
                    fl = fp[pl.ds(G_SUB - l + 1 + r0, G_RB), :]
                    e = fl if e is None else e * fl
                kl = kp[rows, :]
                a = jnp.sum(qb * kl if e is None else qb * kl * e, axis=-1, keepdims=True)
                t = a * vp[rows, :]
                acc = t if acc is None else acc + t
            near.append(acc)
        o = o + jnp.concatenate(near, axis=0)
        yield
        a_off = jnp.zeros((C, C), F32)
        for lv, m in enumerate(_hgrn_levels(C)):
            qt, kt, _, _ = _hgrn_level(b, q, k, C, m)
            prod = _mm_split(_split(qt), _split(kt), NT) if m == G_SUB else _mm(qt, kt, NT)
            a_off = a_off + pm_ref[lv] * prod
        yield
        o = o + _mm(a_off, v)
        S1 = S0 * jnp.exp(bC) + _mm(v, k * jnp.exp(bC - b), TN)
        St[...] = S1
        st_ref[...] = S1
        o_ref[...] = o
        r = lax.rsqrt(jnp.mean(o * o, axis=-1, keepdims=True) + EPS)
        z = z_ref[...]
        ob_ref[...] = (o * r * go_ref[...] * (z * _sigmoid(z))).astype(MXU_DTYPE)

    ident = lambda n: n
    w = hps * G_DK
    out = pl.BlockSpec((C, w), lambda h, n: (n, h))
    return pl.pallas_call(
        body, name="hgrn_fwd",
        grid=(G_HEADS // hps, nc),
        in_specs=[_hgrn_col(C, base, ident, hps) for base in (2, 3, 4, 5)] + [
                  pl.BlockSpec((2, w), lambda h, n: (0, h)),
                  pl.BlockSpec((1, G_DK), lambda h, n: (0, 0)),
                  pl.BlockSpec((C, C), lambda h, n: (0, 0)),
                  pl.BlockSpec(masks.shape, lambda h, n: (0, 0, 0))],
        out_specs=[out, out, pl.BlockSpec((1, hps, G_DK, G_DK), lambda h, n: (n, h, 0, 0))],
        out_shape=[jax.ShapeDtypeStruct((S, G_W), F32), jax.ShapeDtypeStruct((S, G_W), MXU_DTYPE),
                   jax.ShapeDtypeStruct((nc, G_HEADS, G_DK, G_DK), F32)],
        scratch_shapes=[pltpu.VMEM((hps, G_DK, G_DK), F32)] + [pltpu.VMEM((hps, C + G_SUB, G_DK), F32)] * 3,
        compiler_params=_params(("parallel", "arbitrary")),
    )(proj, proj, proj, proj, hgrn_lb, onorm_g, tri, masks)


def _hgrn_bwd(proj, o_raw, dob, states, hgrn_lb, onorm_g, C=G_CHUNK):
    S = proj.shape[0]
    nc = S // C
    tri, triu = _tri(C)
    masks = _hgrn_level_masks(C)
    hps = G_HPS_BWD

    def body(q_ref, f_ref, i_ref, z_ref, o_ref, dob_ref, s0_ref, s1_ref, lbp_ref, go_ref, tri_ref, triu_ref,
             pm_ref, dq_ref, df_ref, di_ref, dz_ref, dlb_ref, dgo_ref, dSt, *shifted):
        @pl.when(pl.program_id(1) == 0)
        def _():
            dSt[...] = jnp.zeros_like(dSt)
            dlb_ref[...] = jnp.zeros_like(dlb_ref)
            dgo_ref[...] = jnp.zeros_like(dgo_ref)

        heads = []
        for hh in range(hps):
            ln = pl.ds(G_DK * hh, G_DK)
            heads.append(head(
                q_ref.at[:, ln], f_ref.at[:, ln], i_ref.at[:, ln], z_ref.at[:, ln], o_ref.at[:, ln],
                dob_ref.at[:, ln], s0_ref.at[0, hh], s1_ref.at[0, hh], lbp_ref.at[:, ln], go_ref, tri_ref, triu_ref,
                pm_ref, dq_ref.at[:, ln], df_ref.at[:, ln], di_ref.at[:, ln], dz_ref.at[:, ln], dlb_ref.at[:, ln],
                dgo_ref.at[pl.ds(8 * hh, 8), :], dSt.at[hh], *[t.at[hh] for t in shifted]))
        _round_robin(heads)

    def head(q_ref, f_ref, i_ref, z_ref, o_ref, dob_ref, s0_ref, s1_ref, lbp_ref, go_ref, tri_ref, triu_ref,
             pm_ref, dq_ref, df_ref, di_ref, dz_ref, dlb_ref, dgo_ref, dSt, kp, vp, fp, qn, dn_, fn, xs, dac):
        cn = nc - 1 - pl.program_id(1)
        qraw, sq, q, sg, lb, f, k, b = _hgrn_gates(q_ref, f_ref, lbp_ref, tri_ref)
        v = i_ref[...]
        bC = b[C - 1:C, :]
        eb = jnp.exp(b)
        ecb = jnp.exp(bC - b)
        o = o_ref[...]
        z = z_ref[...]
        sz = _sigmoid(z)
        go = go_ref[...]
        g_ob = dob_ref[...]
        r = lax.rsqrt(jnp.mean(o * o, axis=-1, keepdims=True) + EPS)
        nh = o * r
        dnrm = g_ob * (z * sz)
        dz_ref[...] = (g_ob * (nh * go) * (sz * (1.0 + z * (1.0 - sz)))).astype(MXU_DTYPE)
        dgo_ref[0:1, :] += jnp.sum(dnrm * nh, axis=0, keepdims=True)
        dn = dnrm * go
        do = r * (dn - nh * jnp.mean(dn * nh, axis=-1, keepdims=True))

        yield
        S0 = jnp.where(cn == 0, 0.0, s0_ref[...])
        S1 = s1_ref[...]
        dS1 = dSt[...]
        dq = eb * _mm(do, S0)
        dk = ecb * _mm(v, dS1)
        dv = _mm(k * ecb, dS1, NT)
        bnd = jnp.sum(dS1 * S1, axis=0, keepdims=True)
        dSt[...] = dS1 * jnp.exp(bC) + _mm(do, q * eb, TN)

        _fill_above(kp, k, 0.0)
        _fill_above(vp, v, 0.0)
        _fill_above(fp, f, 1.0)
        _fill_below(qn, q, 0.0)
        _fill_below(dn_, do, 0.0)
        _fill_below(fn, f, 1.0)
        yield
        for r0 in range(0, C, G_RB):
            do_b = do[r0:r0 + G_RB]
            for l in range(G_SUB):
                xs[pl.ds(l * C + r0, G_RB), :] = (do_b * vp[pl.ds(G_SUB - l + r0, G_RB), :]).astype(MXU_DTYPE)
        dac[0:G_SUB * C, :] = _mm(xs[...], jnp.ones((G_DK, G_DK), MXU_DTYPE))
        dac[G_SUB * C:, :] = jnp.zeros((G_SUB, G_DK), F32)
        yield
        near_q, near_k, near_v = [], [], []
        for r0 in range(0, C, G_RB):
            k_b = k[r0:r0 + G_RB]
            aq = ak = av = e = e2 = None
            for l in range(G_SUB):
                down, up = pl.ds(G_SUB - l + r0, G_RB), pl.ds(l + r0, G_RB)
                if l > 0:
                    fl = fp[pl.ds(G_SUB - l + 1 + r0, G_RB), :]
                    e = fl if e is None else e * fl
                    fu = fn[up, :]
                    e2 = fu if e2 is None else e2 * fu
                kl = kp[down, :]
                t = dac[pl.ds(l * C + r0, G_RB), :] * (kl if e is None else kl * e)
                aq = t if aq is None else aq + t
                qu = qn[up, :]
                qe = qu if e2 is None else qu * e2
                dou = dn_[up, :]
                a2 = jnp.sum(qe * k_b, axis=-1, keepdims=True)
                t = dac[pl.ds(l * C + l + r0, G_RB), :] * qe
                ak = t if ak is None else ak + t
                t = a2 * dou
                av = t if av is None else av + t
            near_q.append(aq)
            near_k.append(ak)
            near_v.append(av)
        dq = dq + jnp.concatenate(near_q, axis=0)
        dk = dk + jnp.concatenate(near_k, axis=0)
        dv = dv + jnp.concatenate(near_v, axis=0)

        yield
        da_all = _mm(do, v, NT)
        a_off = jnp.zeros((C, C), F32)
        for lv, m in enumerate(_hgrn_levels(C)):
            qt, kt, eq, ek = _hgrn_level(b, q, k, C, m)
            da_m = pm_ref[lv] * da_all
            if m == G_SUB:
                qs, ks, das = _split(qt), _split(kt), _split(da_m)
                a_off = a_off + pm_ref[lv] * _mm_split(qs, ks, NT)
                dq = dq + _mm_split(das, ks, NN) * eq
                dk = dk + _mm_split(das, qs, TN) * ek
            else:
                a_off = a_off + pm_ref[lv] * _mm(qt, kt, NT)
                dq = dq + _mm(da_m, kt) * eq
                dk = dk + _mm(da_m, qt, TN) * ek
        dv = dv + _mm(a_off, do, TN)

        yield
        row = lax.broadcasted_iota(jnp.int32, (C, 1), 0)
        db = q * dq - k * dk + jnp.where(row == C - 1, bnd, 0.0)
        dg = _mm_exact(triu_ref[...], db)
        df = dg / f - dk
        df_ref[...] = (df * (1.0 - lb) * (sg * (1.0 - sg))).astype(MXU_DTYPE)
        dlb_ref[0:1, :] += jnp.sum(df * (1.0 - sg), axis=0, keepdims=True)
        dq_ref[...] = (dq * (sq * (1.0 + qraw * (1.0 - sq)))).astype(MXU_DTYPE)
        di_ref[...] = dv.astype(MXU_DTYPE)

    rev = lambda n: nc - 1 - n
    w = hps * G_DK
    blk = pl.BlockSpec((C, w), lambda h, n: (nc - 1 - n, h))
    return pl.pallas_call(
        body, name="hgrn_bwd",
        grid=(G_HEADS // hps, nc),
        in_specs=[_hgrn_col(C, base, rev, hps) for base in (2, 3, 4, 5)] + [
                  blk, blk,
                  pl.BlockSpec((1, hps, G_DK, G_DK), lambda h, n: (jnp.maximum(nc - 2 - n, 0), h, 0, 0)),
                  pl.BlockSpec((1, hps, G_DK, G_DK), lambda h, n: (nc - 1 - n, h, 0, 0)),
                  pl.BlockSpec((2, w), lambda h, n: (0, h)),
                  pl.BlockSpec((1, G_DK), lambda h, n: (0, 0)),
                  pl.BlockSpec((C, C), lambda h, n: (0, 0)),
                  pl.BlockSpec((C, C), lambda h, n: (0, 0)),
                  pl.BlockSpec(masks.shape, lambda h, n: (0, 0, 0))],
        out_specs=[blk, blk, blk, blk,
                   pl.BlockSpec((8, w), lambda h, n: (0, h)),
                   pl.BlockSpec((8 * hps, G_DK), lambda h, n: (h, 0))],
        out_shape=[jax.ShapeDtypeStruct((S, G_W), MXU_DTYPE)] * 4
                  + [jax.ShapeDtypeStruct((8, G_W), F32), jax.ShapeDtypeStruct((8 * G_HEADS, G_DK), F32)],
        scratch_shapes=[pltpu.VMEM((hps, G_DK, G_DK), F32)] + [pltpu.VMEM((hps, C + G_SUB, G_DK), F32)] * 6
                       + [pltpu.VMEM((hps, G_SUB * C, G_DK), MXU_DTYPE),
                          pltpu.VMEM((hps, G_SUB * C + G_SUB, G_DK), F32)],
        compiler_params=_params(("parallel", "arbitrary")),
    )(proj, proj, proj, proj, o_raw, dob, states, states, hgrn_lb, onorm_g, tri, triu, masks)


def _tail(x, target, os, ls, ob, proj, mod3, final_g, wa, wb, wo, tm=256):
    S = x.shape[0]
    nt = S // tm

    def body(x_ref, t_ref, o1, o2, o3, l1, l2, l3, za_ref, ob_ref, ga_ref, gb_ref, mod_ref, fg_ref,
             wa_ref, wb_ref, wo_ref,
             lt_ref, dx2_ref, do_ref, dl_ref, dza_ref, dob_ref, dga_ref, dgb_ref, sums_ref,
             gwa_ref, gwb_ref, gwo_ref, acc_a, acc_b, acc_o):
        i = pl.program_id(0)

        @pl.when(i == 0)
        def _():
            sums_ref[...] = jnp.zeros_like(sums_ref)
            acc_a[...] = jnp.zeros_like(acc_a)
            acc_b[...] = jnp.zeros_like(acc_b)
            acc_o[...] = jnp.zeros_like(acc_o)

        a1, a2, a3 = l1[...], l2[...], l3[...]
        lm = jnp.maximum(jnp.maximum(a1, a2), a3)
        e1, e2, e3 = jnp.exp(a1 - lm), jnp.exp(a2 - lm), jnp.exp(a3 - lm)
        lden = e1 + e2 + e3
        ao = (e1 * o1[...] + e2 * o2[...] + e3 * o3[...]) / lden
        lt_ref[...] = lm + jnp.log(lden)
        za = za_ref[...]
        sza = _sigmoid(za)
        oa_v, ob_v = (ao * (za * sza)).astype(MXU_DTYPE), ob_ref[...]
        pa = _mm(oa_v, wa_ref[...])
        pb = _mm(ob_v, wb_ref[...])
        sa, sb = _sigmoid(ga_ref[...]), _sigmoid(gb_ref[...])
        ym = sa * pa + sb * pb
        u = _mm(ym, wo_ref[...])
        gate = mod_ref[2:3, :]
        fg = fg_ref[...]
        x2 = x_ref[...] + gate * u
        r2 = lax.rsqrt(jnp.mean(x2 * x2, axis=-1, keepdims=True) + EPS)
        xn2 = x2 * r2
        e = xn2 * fg - t_ref[...]
        dy = e * (1.0 / D)
        dn = dy * fg
        dx2 = r2 * (dn - xn2 * jnp.mean(dn * xn2, axis=-1, keepdims=True))
        dx2_ref[...] = dx2
        sums_ref[0:1, :] += jnp.sum(dy * xn2, axis=0, keepdims=True)
        sums_ref[1:2, :] += jnp.sum(dx2 * u, axis=0, keepdims=True)
        sums_ref[2:3, :] += (0.5 / D) * jnp.sum(e * e, axis=0, keepdims=True)
        du = dx2 * gate
        dym = _mm(du, wo_ref[...], NT)
        acc_o[...] += _mm(ym, du, TN)
        dpa, dpb = dym * sa, dym * sb
        dga_ref[...] = (dym * pa * (sa * (1.0 - sa))).astype(MXU_DTYPE)
        dgb_ref[...] = (dym * pb * (sb * (1.0 - sb))).astype(MXU_DTYPE)
        doa = _mm(dpa, wa_ref[...], NT)
        dza_ref[...] = (doa * ao * (sza * (1.0 + za * (1.0 - sza)))).astype(MXU_DTYPE)
        do = doa * (za * sza)
        do_ref[...] = do
        prod = do * ao
        for h in range(A_HEADS):
            sl = slice(A_HD * h, A_HD * (h + 1))
            dl_ref[:, sl] = jnp.broadcast_to(jnp.sum(prod[:, sl], axis=-1, keepdims=True), (tm, A_HD))
        dob_ref[...] = _mm(dpb, wb_ref[...], NT)
        acc_a[...] += _mm(oa_v, dpa, TN)
        acc_b[...] += _mm(ob_v, dpb, TN)

        @pl.when(i == nt - 1)
        def _():
            pltpu.sync_copy(acc_a, gwa_ref)
            pltpu.sync_copy(acc_b, gwb_ref)
            pltpu.sync_copy(acc_o, gwo_ref)

    row = lambda w: pl.BlockSpec((tm, w), lambda i: (i, 0))
    full = lambda a, b: pl.BlockSpec((a, b), lambda i: (0, 0))
    any_spec = pl.BlockSpec(memory_space=pl.ANY)
    return pl.pallas_call(
        body, name="tail",
        grid=(nt,),
        in_specs=[row(D), row(D)] + [row(A_W)] * 6 + [pl.BlockSpec((tm, A_W), lambda i: (i, 3)), row(D),
                  pl.BlockSpec((tm, D), lambda i: (i, 6)), pl.BlockSpec((tm, D), lambda i: (i, 7)),
                  full(8, D), full(1, D), full(A_W, D), full(D, D), full(D, D)],
        out_specs=[row(A_W), row(D), row(A_W), row(A_W), row(A_W), row(D), row(D), row(D), full(8, D),
                   any_spec, any_spec, any_spec],
        out_shape=[jax.ShapeDtypeStruct((S, A_W), F32),
                   jax.ShapeDtypeStruct((S, D), F32), jax.ShapeDtypeStruct((S, A_W), F32),
                   jax.ShapeDtypeStruct((S, A_W), F32), jax.ShapeDtypeStruct((S, A_W), MXU_DTYPE),
                   jax.ShapeDtypeStruct((S, D), F32), jax.ShapeDtypeStruct((S, D), MXU_DTYPE),
                   jax.ShapeDtypeStruct((S, D), MXU_DTYPE), jax.ShapeDtypeStruct((8, D), F32),
                   jax.ShapeDtypeStruct((A_W, D), F32), jax.ShapeDtypeStruct((D, D), F32),
                   jax.ShapeDtypeStruct((D, D), F32)],
        scratch_shapes=[pltpu.VMEM((A_W, D), F32), pltpu.VMEM((D, D), F32), pltpu.VMEM((D, D), F32)],
        compiler_params=_params(("arbitrary",)),
    )(x, target, *os, *ls, proj, ob, proj, proj, mod3, final_g, wa, wb, wo)


def _piece_parts(pieces):
    parts, where = [], []
    for k, piece in enumerate(pieces):
        off = 0
        for part in piece:
            parts.append(part)
            where.append((k, off, part.shape[1]))
            off += part.shape[1]
        assert off == D
    return parts, where


def _dh(pieces, w_in_g, x, dx2, mod3, norm_g, grads, tm=256):
    S = x.shape[0]
    ni = S // tm
    ng = len(grads)
    parts, where = _piece_parts(pieces)
    npart = len(parts)

    def body(*refs):
        p_refs = refs[:npart]
        w_ref, x_ref, dx2_ref, mod_ref, g_ref = refs[npart:npart + 5]
        g_ins = refs[npart + 5:npart + 5 + ng]
        gx_ref, sums_ref = refs[npart + 5 + ng:npart + 7 + ng]
        g_outs = refs[npart + 7 + ng:npart + 7 + 2 * ng]
        w_all, send_sems, recv_sems, local_sems = refs[npart + 7 + 2 * ng:]
        i = pl.program_id(0)
        start, wait = _all_to_all_copies(g_ins, g_outs, send_sems, recv_sems, local_sems)

        @pl.when(i == 0)
        def _():
            start()
            sums_ref[...] = jnp.zeros_like(sums_ref)
            pltpu.sync_copy(w_ref, w_all)

        dh = None
        for p_ref, (k, off, width) in zip(p_refs, where):
            term = _mm(p_ref[...], w_all[k, :, off:off + width], NT)
            dh = term if dh is None else dh + term
        xv = x_ref[...]
        g = g_ref[...]
        sc1 = 1.0 + mod_ref[1:2, :]
        r = lax.rsqrt(jnp.mean(xv * xv, axis=-1, keepdims=True) + EPS)
        xn = xv * r
        sums_ref[0:1, :] += jnp.sum(dh, axis=0, keepdims=True)
        sums_ref[1:2, :] += jnp.sum(dh * (xn * g), axis=0, keepdims=True)
        sums_ref[2:3, :] += jnp.sum(dh * sc1 * xn, axis=0, keepdims=True)
        dxn = dh * sc1 * g
        gx_ref[...] = dx2_ref[...] + r * (dxn - xn * jnp.mean(dxn * xn, axis=-1, keepdims=True))

        @pl.when(i == ni - 1)
        def _():
            wait()

    row = pl.BlockSpec((tm, D), lambda i: (i, 0))
    any_spec = pl.BlockSpec(memory_space=pl.ANY)
    return pl.pallas_call(
        body, name="dh_scatter",
        grid=(ni,),
        in_specs=[pl.BlockSpec((tm, width), lambda i: (i, 0)) for _, _, width in where]
                 + [any_spec, row, row,
                    pl.BlockSpec((8, D), lambda i: (0, 0)),
                    pl.BlockSpec((1, D), lambda i: (0, 0))]
                 + [any_spec] * ng,
        out_specs=[row, pl.BlockSpec((8, D), lambda i: (0, 0))] + [any_spec] * ng,
        out_shape=[jax.ShapeDtypeStruct((S, D), F32), jax.ShapeDtypeStruct((8, D), F32)]
                  + [jax.ShapeDtypeStruct(g.shape, g.dtype) for g in grads],
        scratch_shapes=[pltpu.VMEM(w_in_g.shape, w_in_g.dtype),
                        pltpu.SemaphoreType.DMA((ng, N_DEV - 1)), pltpu.SemaphoreType.DMA((ng, N_DEV - 1)),
                        pltpu.SemaphoreType.DMA((ng,))],
        compiler_params=_params(("arbitrary",)),
    )(*parts, w_in_g, x, dx2, mod3, norm_g, *grads)


def _gw_in(ht, pieces, grads, tm=1024):
    S = ht.shape[1]
    nt = S // tm
    ng = len(grads)
    parts, where = _piece_parts(pieces)
    npart = len(parts)

    def body(*refs):
        h_ref, p_refs = refs[0], refs[1:1 + npart]
        g_ins = refs[1 + npart:1 + npart + ng]
        o_ref = refs[1 + npart + ng]
        g_outs = refs[2 + npart + ng:2 + npart + 2 * ng]
        acc, send_sems, recv_sems, local_sems = refs[2 + npart + 2 * ng:]
        j, i = pl.program_id(0), pl.program_id(1)
        start, wait = _all_to_all_copies(g_ins, g_outs, send_sems, recv_sems, local_sems)

        @pl.when((j == 0) & (i == 0))
        def _():
            start()

        @pl.when(i == 0)
        def _():
            acc[...] = jnp.zeros_like(acc)

        for k in range(N_DEV):
            @pl.when(j == k)
            def _(k=k):
                for p_ref, (kk, off, width) in zip(p_refs, where):
                    if kk == k:
                        acc[:, off:off + width] += _mm(h_ref[...], p_ref[...])

        @pl.when(i == nt - 1)
        def _():
            o_ref[0] = acc[...].astype(XCHG_DTYPE)

        @pl.when((j == N_DEV - 1) & (i == nt - 1))
        def _():
            wait()

    def part_spec(k, width):
        return pl.BlockSpec((tm, width), lambda j, i: (jnp.where(j == k, i, 0), 0))

    any_spec = pl.BlockSpec(memory_space=pl.ANY)
    return pl.pallas_call(
        body, name="gw_in_scatter",
        grid=(N_DEV, nt),
        in_specs=[pl.BlockSpec((D, tm), lambda j, i: (0, i))] + [part_spec(k, width) for k, _, width in where] + [any_spec] * ng,
        out_specs=[pl.BlockSpec((1, D, D), lambda j, i: (j, 0, 0))] + [any_spec] * ng,
        out_shape=[jax.ShapeDtypeStruct((N_DEV, D, D), XCHG_DTYPE)]
                  + [jax.ShapeDtypeStruct(g.shape, g.dtype) for g in grads],
        scratch_shapes=[pltpu.VMEM((D, D), F32),
                        pltpu.SemaphoreType.DMA((ng, N_DEV - 1)), pltpu.SemaphoreType.DMA((ng, N_DEV - 1)),
                        pltpu.SemaphoreType.DMA((ng,))],
        compiler_params=_params(("arbitrary", "arbitrary")),
    )(ht, *parts, *grads)


def _adamw_math(w, g, m, v):
    m = ADAM_B1 * m + (1.0 - ADAM_B1) * g
    v = ADAM_B2 * v + (1.0 - ADAM_B2) * (g * g)
    m_hat = m / (1.0 - ADAM_B1 ** ADAM_STEP)
    v_hat = v / (1.0 - ADAM_B2 ** ADAM_STEP)
    delta = -ADAM_LR * (m_hat / (jnp.sqrt(v_hat) + ADAM_EPS) + ADAM_WD * w)
    return delta, m, v


def _adamw_big(recv, w, m, v, name, tr=128):
    M, N = w.shape
    tr = min(tr, M)

    def body(r_ref, w_ref, m_ref, v_ref, g_ref, d_ref, nm_ref, nv_ref):
        g = r_ref[0].astype(F32)
        for j in range(1, N_DEV):
            g = g + r_ref[j].astype(F32)
        g_ref[...] = g
        d_ref[...], nm_ref[...], nv_ref[...] = _adamw_math(w_ref[...], g, m_ref[...], v_ref[...])

    blk = pl.BlockSpec((tr, N), lambda i: (i, 0))
    return pl.pallas_call(
        body, name=name,
        grid=(M // tr,),
        in_specs=[pl.BlockSpec((N_DEV, tr, N), lambda i: (0, i, 0)), blk, blk, blk],
        out_specs=[blk] * 4,
        out_shape=[jax.ShapeDtypeStruct((M, N), F32)] * 4,
        compiler_params=_params(("parallel",)),
    )(recv, w, m, v)


def _adamw_w_ada(c64, dmod64, w, m, v):
    def body(c_ref, dm_ref, w_ref, m_ref, v_ref, g_ref, d_ref, nm_ref, nv_ref):
        cv = c_ref[...]
        g = _mm(cv * _sigmoid(cv), dm_ref[...], TN)
        g_ref[...] = g
        d_ref[...], nm_ref[...], nv_ref[...] = _adamw_math(w_ref[...], g, m_ref[...], v_ref[...])

    return pl.pallas_call(
        body, name="adamw_w_ada",
        out_shape=[jax.ShapeDtypeStruct(w.shape, F32)] * 4,
        compiler_params=_params(),
    )(c64, dmod64, w, m, v)


P_MOD, P_NORM, P_ONORM, P_RELB, P_LB, P_FINAL, P_LOSS, P_END = (0, 3 * D, 4 * D, 5 * D, 6 * D, 7 * D, 8 * D, 9 * D)


def _adamw_small(packed, b_ada, norm_g, onorm_g, relb, hgrn_lb, final_g, ms, vs):
    def body(pk_ref, b_ref, ng_ref, og_ref, rb_ref, lb_ref, fg_ref,
             mb, mn, mo, mr, ml, mf, vb, vn, vo, vr, vl, vf,
             loss_ref, gb, gn, go, gr, gl, gf, db, dn, do, dr, dl, df,
             nmb, nmn, nmo, nmr, nml, nmf, nvb, nvn, nvo, nvr, nvl, nvf):
        tot = pk_ref[0:1, :]
        for j in range(1, N_DEV):
            tot = tot + pk_ref[8 * j:8 * j + 1, :]
        loss_ref[...] = jnp.broadcast_to(jnp.sum(tot[:, P_LOSS:P_END], axis=-1, keepdims=True), (8, 128))

        def upd(g, w_ref, m_ref, v_ref, g_out, d_out, m_out, v_out):
            g_out[...] = g
            d_out[...], m_out[...], v_out[...] = _adamw_math(w_ref[...], g, m_ref[...], v_ref[...])

        upd(tot[:, P_MOD:P_NORM], b_ref, mb, vb, gb, db, nmb, nvb)
        upd(tot[:, P_NORM:P_ONORM], ng_ref, mn, vn, gn, dn, nmn, nvn)
        g_on = tot[:, P_ONORM:P_ONORM + G_DK]
        for h in range(1, G_HEADS):
            g_on = g_on + tot[:, P_ONORM + G_DK * h:P_ONORM + G_DK * (h + 1)]
        upd(g_on, og_ref, mo, vo, go, do, nmo, nvo)
        upd(tot[:, P_RELB:P_LB], rb_ref, mr, vr, gr, dr, nmr, nvr)
        a = lb_ref[...]
        lb = _sigmoid(a[0:1, :] - a[1:2, :])
        g0 = tot[:, P_LB:P_FINAL] * lb * (1.0 - lb)
        row = lax.broadcasted_iota(jnp.int32, (2, D), 0)
        upd(jnp.where(row == 0, g0, -g0), lb_ref, ml, vl, gl, dl, nml, nvl)
        upd(tot[:, P_FINAL:P_LOSS], fg_ref, mf, vf, gf, df, nmf, nvf)

    shapes = [b_ada.shape, norm_g.shape, onorm_g.shape, relb.shape, hgrn_lb.shape, final_g.shape]
    outs = [jax.ShapeDtypeStruct((8, 128), F32)] + [jax.ShapeDtypeStruct(s, F32) for s in shapes] * 4
    return pl.pallas_call(
        body, name="adamw_small",
        out_shape=outs,
        compiler_params=_params(),
    )(packed, b_ada, norm_g, onorm_g, relb, hgrn_lb, final_g, *ms, *vs)


def _local_step(x, target, mod3, norm_g, w_in_g, onorm_g, wa_blk, wb_blk, wo_blk, rel_bias, hgrn_lb, final_g):
    buckets = jnp.asarray(_bucket_tables())
    bias = _bias_tables(rel_bias, buckets)
    proj, ht, qkv, wa_g, wb_g, wo_g = _inproj(x, mod3, norm_g, w_in_g, [wa_blk, wb_blk, wo_blk])
    wa = wa_g.transpose(1, 0, 2).reshape(A_W, D)
    wb = wb_g.reshape(D, D)
    wo = wo_g.reshape(D, D)
    os, ls = [], []
    for p, (_, d) in enumerate(PATTERNS):
        o, l = _attn_fwd(qkv, bias[p], d, "attn_fwd_d%d" % d)
        os.append(o)
        ls.append(l)
    o_raw, ob, states = _hgrn_fwd(proj, hgrn_lb, onorm_g)
    lt, dx2, do, delta, dza, dob, dga, dgb, tsums, gwa, gwb, gwo = _tail(
        x, target, os, ls, ob, proj, mod3, final_g, wa, wb, wo)
    dbs, acc = [None] * len(PATTERNS), ()
    for p in reversed(range(len(PATTERNS))):
        d = PATTERNS[p][1]
        *acc, dbs[p] = _attn_bwd(qkv, do, lt, delta, bias[p], d, "attn_bwd_d%d" % d, prev=tuple(acc),
                                 out_dtype=MXU_DTYPE if p == 0 else F32)
    dqa, dka, dva = acc
    g_relb = _rel_bias_grad(dbs, buckets)
    dqb, dfb, dib, dzb, dlb, dgo = _hgrn_bwd(proj, o_raw, dob, states, hgrn_lb, onorm_g)
    pieces = [[dqa, dka], [dva, dza], [dqb], [dfb], [dib], [dzb], [dga], [dgb]]
    small = [gwa.astype(XCHG_DTYPE).reshape(A_W, N_DEV, D // N_DEV).transpose(1, 0, 2),
             gwb.astype(XCHG_DTYPE).reshape(N_DEV, D // N_DEV, D),
             gwo.astype(XCHG_DTYPE).reshape(N_DEV, D // N_DEV, D)]
    gw_in, *received_small = _gw_in(ht, pieces, small)
    gx, hsums, received_in = _dh(pieces, w_in_g, x, dx2, mod3, norm_g, [gw_in])
    received = [received_in] + received_small
    row = jnp.concatenate([
        hsums[0], hsums[1], tsums[1],
        hsums[2],
        dgo.reshape(G_HEADS, 8, G_DK)[:, 0].reshape(-1),
        g_relb.reshape(-1),
        dlb[0],
        tsums[0],
        tsums[2],
    ])
    return gx, received, row


def kernel(x, c, w_ada, b_ada, norm_g, w_in, hgrn_onorm_g, w_branch_a, w_branch_b, w_out, rel_bias, hgrn_lb, final_g, loss_target, m_w_ada, m_b_ada, m_norm_g, m_w_in, m_hgrn_onorm_g, m_w_branch_a, m_w_branch_b, m_w_out, m_rel_bias, m_hgrn_lb, m_final_g, v_w_ada, v_b_ada, v_norm_g, v_w_in, v_hgrn_onorm_g, v_w_branch_a, v_w_branch_b, v_w_out, v_rel_bias, v_hgrn_lb, v_final_g):
    me = 4 * lax.axis_index("x") + 2 * lax.axis_index("y") + lax.axis_index("c")
    n_ada = w_ada.shape[2]

    w_in_g, c_all = _all_gather([w_in[0].astype(MXU_DTYPE), jnp.broadcast_to(c, (8, D))], "gather_w_in_c")

    c64 = c_all.reshape(8 * N_DEV, D)
    b_loc = lax.dynamic_slice(b_ada, (0, me * n_ada), (1, n_ada))
    mod_part = _mod_fwd(c64, w_ada[0], b_loc)[::8]
    (mod_all,) = _all_gather([mod_part], "gather_mod")
    mod = lax.dynamic_slice(mod_all, (0, me, 0), (N_DEV, 1, n_ada)).reshape(3, D)
    mod3 = jnp.concatenate([mod, jnp.zeros((5, D), F32)], axis=0)

    onorm_t = hgrn_onorm_g
    gx, (r_in, r_a, r_b, r_o), row = _local_step(
        x[0], loss_target[0], mod3, norm_g, w_in_g, onorm_t, w_branch_a[0].astype(MXU_DTYPE),
        w_branch_b[0].astype(MXU_DTYPE), w_out[0].astype(MXU_DTYPE), rel_bias, hgrn_lb,
        final_g.reshape(1, D))
    packed8 = jnp.concatenate([row[None, :], jnp.zeros((7, P_END), F32)], axis=0)
    (packed,) = _all_gather([packed8], "gather_small")
    packed = packed.reshape(8 * N_DEV, P_END)

    g_in, d_in, nm_in, nv_in = _adamw_big(r_in, w_in[0], m_w_in[0], v_w_in[0], "adamw_w_in")
    g_a, d_a, nm_a, nv_a = _adamw_big(r_a, w_branch_a[0], m_w_branch_a[0], v_w_branch_a[0], "adamw_w_branch_a")
    g_b, d_b, nm_b, nv_b = _adamw_big(r_b, w_branch_b[0], m_w_branch_b[0], v_w_branch_b[0], "adamw_w_branch_b")
    g_o, d_o, nm_o, nv_o = _adamw_big(r_o, w_out[0], m_w_out[0], v_w_out[0], "adamw_w_out")

    dmod64 = lax.dynamic_slice(packed, (0, P_MOD + me * n_ada), (8 * N_DEV, n_ada))
    g_ada, d_ada, nm_ada, nv_ada = _adamw_w_ada(c64, dmod64, w_ada[0], m_w_ada[0], v_w_ada[0])

    def flat_relb(t):
        return jnp.pad(t.T, ((0, 0), (0, 128 - N_BUCKETS))).reshape(1, A_HEADS * 128)

    def unflat_relb(t):
        return t.reshape(A_HEADS, 128)[:, :N_BUCKETS].T

    fg2 = lambda t: t.reshape(1, D)
    smalls = _adamw_small(
        packed, b_ada, norm_g, hgrn_onorm_g, flat_relb(rel_bias), hgrn_lb, fg2(final_g),
        [m_b_ada, m_norm_g, m_hgrn_onorm_g, flat_relb(m_rel_bias), m_hgrn_lb, fg2(m_final_g)],
        [v_b_ada, v_norm_g, v_hgrn_onorm_g, flat_relb(v_rel_bias), v_hgrn_lb, fg2(v_final_g)])
    loss = smalls[0][0, 0]

    def small(kind):
        s = smalls[1 + 6 * kind:7 + 6 * kind]
        return s[0], s[1], s[2], unflat_relb(s[3]), s[4], s[5].reshape(D)

    def leaves(ada, sm, w_in_, wa_, wb_, wo_):
        b_, n_, o_, r_, l_, f_ = sm
        return (ada[None], b_, n_, w_in_[None], o_, wa_[None], wb_[None], wo_[None], r_, l_, f_)

    return (loss, gx[None],
            *leaves(g_ada, small(0), g_in, g_a, g_b, g_o),
            *leaves(d_ada, small(1), d_in, d_a, d_b, d_o),
            *leaves(nm_ada, small(2), nm_in, nm_a, nm_b, nm_o),
            *leaves(nv_ada, small(3), nv_in, nv_a, nv_b, nv_o))
```

```python
import functools
import math

import numpy as np
import jax
import jax.numpy as jnp
from jax import lax
from jax.experimental import pallas as pl
from jax.experimental.pallas import tpu as pltpu

F32 = jnp.float32
BF16 = jnp.bfloat16
MXU_DTYPE = jnp.bfloat16
XCHG_DTYPE = jnp.bfloat16

N_DEV = 8
D = 1024
A_HEADS = 8
A_HD = 64
A_W = A_HEADS * A_HD
A_BLK = 128
PATTERNS = ((128, 1), (512, 4), (2048, 16))
N_BUCKETS = 32
MAX_DISTANCE = 2048
NEG = -1e30
G_HEADS = 8
G_DK = 128
G_W = G_HEADS * G_DK
IN_W = 8 * D
EPS = 1e-6
ADAM_LR = 0.001
ADAM_B1 = 0.9
ADAM_B2 = 0.999
ADAM_EPS = 1e-08
ADAM_WD = 0.01
ADAM_STEP = 10

G_CHUNK = 128
G_SUB = 8
G_HPS_FWD = 8
G_HPS_BWD = 8
G_RB = 16
VMEM_LIMIT = 56 * 1024 * 1024

NN = (((1,), (0,)), ((), ()))
NT = (((1,), (1,)), ((), ()))
TN = (((0,), (0,)), ((), ()))
MESH = pl.DeviceIdType.MESH


def _mm(a, b, dims=NN):
    return lax.dot_general(a.astype(MXU_DTYPE), b.astype(MXU_DTYPE), dims,
                           preferred_element_type=F32)


def _mm_exact(t, x):
    hi = x.astype(BF16)
    r = x - hi.astype(F32)
    mid = r.astype(BF16)
    lo = (r - mid.astype(F32)).astype(BF16)
    tb = t.astype(BF16)
    return sum(lax.dot_general(tb, p, NN, preferred_element_type=F32) for p in (hi, mid, lo))


def _split(x):
    hi = x.astype(BF16)
    return hi, (x - hi.astype(F32)).astype(BF16)


def _mm_split(a, b, dims):
    dot = lambda p, q: lax.dot_general(p, q, dims, preferred_element_type=F32)
    return dot(a[0], b[0]) + dot(a[0], b[1]) + dot(a[1], b[0])


def _sigmoid(x):
    return 0.5 * jnp.tanh(0.5 * x) + 0.5


def _params(sem=None):
    return pltpu.CompilerParams(dimension_semantics=sem, vmem_limit_bytes=VMEM_LIMIT)


def _all_gather(xs, name):
    n = len(xs)

    def body(*refs):
        ins, outs = refs[:n], refs[n:2 * n]
        send_sems, recv_sems, local_sems = refs[2 * n:]
        x, y, c = lax.axis_index("x"), lax.axis_index("y"), lax.axis_index("c")
        me, sibling = (x, y, c), (x, y, 1 - c)
        chips = [(1 - x, y), (x, 1 - y), (1 - x, 1 - y)]

        def slot(ref, dev):
            return ref.at[4 * dev[0] + 2 * dev[1] + dev[2]]

        def copy(a, k, block, to, src=None):
            return pltpu.make_async_remote_copy(
                src_ref=slot(outs[a], block) if src is None else src,
                dst_ref=slot(outs[a], block),
                send_sem=send_sems.at[a, k], recv_sem=recv_sems.at[a, k],
                device_id=to, device_id_type=MESH)

        mine, first, passed = [], [], []
        for a in range(n):
            cp = pltpu.make_async_copy(ins[a], slot(outs[a], me), local_sems.at[a])
            cp.start()
            mine.append(cp)
            first.append(copy(a, 0, me, sibling, src=ins[a]))
            for j, chip in enumerate(chips):
                first.append(copy(a, 1 + j, me, (*chip, c), src=ins[a]))
        for cp in first:
            cp.start()
        for j, chip in enumerate(chips):
            for a in range(n):
                copy(a, 1 + j, (*chip, c), me).wait_recv()
                cp = copy(a, 4 + j, (*chip, c), sibling)
                cp.start()
                passed.append(cp)
        for a in range(n):
            copy(a, 0, sibling, me).wait_recv()
            for j, chip in enumerate(chips):
                copy(a, 4 + j, (*chip, 1 - c), me).wait_recv()
        for cp in first + passed:
            cp.wait_send()
        for cp in mine:
            cp.wait()

    any_spec = pl.BlockSpec(memory_space=pl.ANY)
    return pl.pallas_call(
        body, name=name,
        out_shape=[jax.ShapeDtypeStruct((N_DEV,) + v.shape, v.dtype) for v in xs],
        in_specs=[any_spec] * n, out_specs=[any_spec] * n,
        scratch_shapes=[pltpu.SemaphoreType.DMA((n, 7)), pltpu.SemaphoreType.DMA((n, 7)),
                        pltpu.SemaphoreType.DMA((n,))],
    )(*xs)


def _all_to_all_copies(ins, outs, send_sems, recv_sems, local_sems, gather=False):
    n = len(ins)
    x, y, c = lax.axis_index("x"), lax.axis_index("y"), lax.axis_index("c")
    me = 4 * x + 2 * y + c
    peers = []
    for m in range(1, N_DEV):
        peers.append((1 - x if m & 4 else x, 1 - y if m & 2 else y, 1 - c if m & 1 else c))

    def chunk(a, j):
        return ins[a] if gather else ins[a].at[j]

    def copy(a, k, landing):
        peer = peers[k]
        pid = 4 * peer[0] + 2 * peer[1] + peer[2]
        return pltpu.make_async_remote_copy(
            src_ref=chunk(a, pid), dst_ref=outs[a].at[pid if landing else me],
            send_sem=send_sems.at[a, k], recv_sem=recv_sems.at[a, k],
            device_id=peer, device_id_type=MESH)

    def local(a):
        return pltpu.make_async_copy(chunk(a, me), outs[a].at[me], local_sems.at[a])

    def start():
        for a in range(n):
            local(a).start()
        for k in range(N_DEV - 1):
            for a in range(n):
                copy(a, k, False).start()

    def wait():
        for k in range(N_DEV - 1):
            for a in range(n):
                copy(a, k, True).wait_recv()
        for k in range(N_DEV - 1):
            for a in range(n):
                copy(a, k, False).wait_send()
        for a in range(n):
            local(a).wait()

    return start, wait


def _all_gather_small(xs, name):
    n = len(xs)

    def body(*refs):
        start, wait = _all_to_all_copies(refs[:n], refs[n:2 * n], *refs[2 * n:], gather=True)
        start()
        wait()

    any_spec = pl.BlockSpec(memory_space=pl.ANY)
    return pl.pallas_call(
        body, name=name,
        out_shape=[jax.ShapeDtypeStruct((N_DEV,) + v.shape, v.dtype) for v in xs],
        in_specs=[any_spec] * n, out_specs=[any_spec] * n,
        scratch_shapes=[pltpu.SemaphoreType.DMA((n, N_DEV - 1)), pltpu.SemaphoreType.DMA((n, N_DEV - 1)),
                        pltpu.SemaphoreType.DMA((n,))],
    )(*xs)


def _mod_fwd(c64, w_ada, b_loc):
    def body(c_ref, w_ref, b_ref, o_ref):
        cv = c_ref[...]
        sc = cv * _sigmoid(cv)
        o_ref[...] = _mm(sc, w_ref[...]) + b_ref[...]

    return pl.pallas_call(
        body, name="mod_fwd",
        out_shape=jax.ShapeDtypeStruct((c64.shape[0], w_ada.shape[1]), F32),
        compiler_params=_params(),
    )(c64, w_ada, b_loc)


def _inproj(x, mod3, norm_g, w_in_g, blocks, tm=256):
    S = x.shape[0]
    ni = S // tm
    nb = len(blocks)

    def body(*refs):
        x_ref, mod_ref, g_ref, w_ref = refs[:4]
        b_ins = refs[4:4 + nb]
        proj_ref, ht_ref, qkv_ref = refs[4 + nb:7 + nb]
        b_outs = refs[7 + nb:7 + 2 * nb]
        w_all, send_sems, recv_sems, local_sems = refs[7 + 2 * nb:]
        i = pl.program_id(0)
        start, wait = _all_to_all_copies(b_ins, b_outs, send_sems, recv_sems, local_sems, gather=True)

        @pl.when(i == 0)
        def _():
            start()
            pltpu.sync_copy(w_ref, w_all)

        xv = x_ref[...]
        r = lax.rsqrt(jnp.mean(xv * xv, axis=-1, keepdims=True) + EPS)
        h = ((xv * r * g_ref[...]) * (1.0 + mod_ref[1:2, :]) + mod_ref[0:1, :]).astype(MXU_DTYPE)
        ht_ref[...] = h.T
        for j in range(N_DEV):
            pj = _mm(h, w_all[j])
            proj_ref[:, j * D:(j + 1) * D] = pj
            for c in range(3):
                if c // 2 == j:
                    for p in range(A_HEADS // 2):
                        lo = (c % 2) * A_W + 2 * A_HD * p
                        qkv_ref[c, p] = pj[:, lo:lo + 2 * A_HD]

        @pl.when(i == ni - 1)
        def _():
            wait()

    any_spec = pl.BlockSpec(memory_space=pl.ANY)
    return pl.pallas_call(
        body, name="inproj_gather",
        grid=(ni,),
        in_specs=[pl.BlockSpec((tm, D), lambda i: (i, 0)),
                  pl.BlockSpec((8, D), lambda i: (0, 0)),
                  pl.BlockSpec((1, D), lambda i: (0, 0)),
                  any_spec] + [any_spec] * nb,
        out_specs=[pl.BlockSpec((tm, IN_W), lambda i: (i, 0)),
                   pl.BlockSpec((D, tm), lambda i: (0, i)),
                   pl.BlockSpec((3, A_HEADS // 2, tm, 2 * A_HD), lambda i: (0, 0, i, 0))] + [any_spec] * nb,
        out_shape=[jax.ShapeDtypeStruct((S, IN_W), F32), jax.ShapeDtypeStruct((D, S), MXU_DTYPE),
                   jax.ShapeDtypeStruct((3, A_HEADS // 2, S, 2 * A_HD), F32)]
                  + [jax.ShapeDtypeStruct((N_DEV,) + b.shape, b.dtype) for b in blocks],
        scratch_shapes=[pltpu.VMEM(w_in_g.shape, w_in_g.dtype),
                        pltpu.SemaphoreType.DMA((nb, N_DEV - 1)), pltpu.SemaphoreType.DMA((nb, N_DEV - 1)),
                        pltpu.SemaphoreType.DMA((nb,))],
        compiler_params=_params(("arbitrary",)),
    )(x, mod3, norm_g, w_in_g, *blocks)


def _bucket_tables():
    qi = np.arange(A_BLK)[:, None]
    kj = np.arange(2 * A_BLK)[None, :]
    delta = qi + A_BLK - kj
    out = []
    for window, dil in PATTERNS:
        span = window // dil
        band = (delta >= 0) & (delta <= span)
        dist = np.clip(delta, 0, None) * dil
        max_exact = N_BUCKETS // 2
        nf = dist.astype(np.float32)
        large = max_exact + (np.log(np.maximum(nf, np.float32(1.0)) / np.float32(max_exact))
                             / np.float32(math.log(MAX_DISTANCE / max_exact))
                             * np.float32(N_BUCKETS - max_exact)).astype(np.int32)
        large = np.minimum(large, N_BUCKETS - 1)
        bucket = np.where(dist < max_exact, dist, large)
        out.append(np.where(band, bucket, -1).astype(np.int32))
    return np.stack(out)


def _bias_tables(rel_bias, buckets):
    def body(rb_ref, bk_ref, o_ref):
        bk = bk_ref[0]
        for h in range(A_HEADS):
            acc = jnp.full(bk.shape, NEG, F32)
            for b in range(N_BUCKETS):
                acc = jnp.where(bk == b, rb_ref[b, h], acc)
            o_ref[0, h] = acc

    return pl.pallas_call(
        body, name="bias_tables",
        grid=(len(PATTERNS),),
        in_specs=[pl.BlockSpec(memory_space=pltpu.SMEM),
                  pl.BlockSpec((1, A_BLK, 2 * A_BLK), lambda p: (p, 0, 0))],
        out_specs=pl.BlockSpec((1, A_HEADS, A_BLK, 2 * A_BLK), lambda p: (p, 0, 0, 0)),
        out_shape=jax.ShapeDtypeStruct((len(PATTERNS), A_HEADS, A_BLK, 2 * A_BLK), F32),
        compiler_params=_params(("arbitrary",)),
    )(rel_bias, buckets)


A_TILES = 32


def _attn_heads_per_step(d):
    return A_HEADS if d == 1 else 2


def _attn_in_specs(sb, nsb, hw):
    blk = (1, hw // 2, sb, 2 * A_HD)

    def cur(c):
        return pl.BlockSpec(blk, lambda hp, n: (c, hp, jnp.minimum(n, nsb - 1), 0))

    def prev(c):
        return pl.BlockSpec(blk, lambda hp, n: (c, hp, jnp.maximum(jnp.minimum(n, nsb - 1) - 1, 0), 0))

    return [cur(0), prev(1), cur(1), prev(2), cur(2)]


def _rows(r, d):
    return pl.ds(r, A_BLK) if d == 1 else pl.ds(r, A_BLK, stride=d)


def _for_residues(d, hw, fn):
    unroll = min(d, max(1, A_TILES // hw))
    if d == unroll:
        _round_robin([g for r in range(d) for g in fn(r)])
    else:
        def group(g, c):
            _round_robin([t for u in range(unroll) for t in fn(g * unroll + u)])
            return c
        lax.fori_loop(0, d // unroll, group, 0)


def _attn_stack(t):
    first_half = lax.broadcasted_iota(jnp.int32, (1, 2 * A_HD), 1) < A_HD
    return jnp.concatenate([jnp.where(first_half, t, 0.0), jnp.where(first_half, 0.0, t)], axis=0)


def _attn_unstack(t2):
    first_half = lax.broadcasted_iota(jnp.int32, (1, 2 * A_HD), 1) < A_HD
    return jnp.where(first_half, t2[:A_BLK], t2[A_BLK:])


def _attn_scores(q, k, b_ref, pp, first):
    bias = jnp.concatenate([b_ref[2 * pp] + first, b_ref[2 * pp + 1] + first], axis=0)
    return _mm(_attn_stack(q), k, NT) * (A_HD ** -0.5) + bias


def _attn_fwd(qkv, bias_p, d, name):
    S = qkv.shape[2]
    hw = _attn_heads_per_step(d)
    sub = A_BLK * d
    nsub = max(1, 2 * A_TILES // (hw * d))
    sb = sub * nsub
    nsb = S // sb

    def body(q_ref, kp_ref, kc_ref, vp_ref, vc_ref, b_ref, o_ref, l_ref):
        n = pl.program_id(1)
        kj = lax.broadcasted_iota(jnp.int32, (A_BLK, 2 * A_BLK), 1)
        first = jnp.where((n == 0) & (kj < A_BLK), NEG, 0.0).astype(F32)
        no_first = jnp.zeros((A_BLK, 2 * A_BLK), F32)

        def residue(r, u=0):
            rows = _rows(u * sub + r, d)
            behind = _rows(((nsub if u == 0 else u) - 1) * sub + r, d)

            def pair(pp):
                lanes = pl.ds(2 * A_HD * pp, 2 * A_HD)
                kc, vc = kc_ref.at[0, pp], vc_ref.at[0, pp]
                kb, vb = (kp_ref.at[0, pp], vp_ref.at[0, pp]) if u == 0 else (kc, vc)
                k = jnp.concatenate([kb[behind, :], kc[rows, :]], axis=0)
                v = jnp.concatenate([vb[behind, :], vc[rows, :]], axis=0)
                s = _attn_scores(q_ref.at[0, pp][rows, :], k, b_ref, pp, first if u == 0 else no_first)
                yield
                m = jnp.max(s, axis=-1, keepdims=True)
                p = jnp.exp(s - m)
                den = jnp.sum(p, axis=-1, keepdims=True)
                pv = _mm(p, v)
                yield
                o_ref[rows, lanes] = _attn_unstack(pv / den)
                l_ref[rows, lanes] = _attn_unstack(jnp.broadcast_to(m + jnp.log(den), (2 * A_BLK, 2 * A_HD)))

            return [pair(pp) for pp in range(hw // 2)]

        if nsub == 1:
            _for_residues(d, hw, residue)
        else:
            _round_robin([g for u in range(nsub) for r in range(d) for g in residue(r, u)])

    out = pl.BlockSpec((sb, A_HD * hw), lambda hp, n: (n, hp))
    return pl.pallas_call(
        body, name=name,
        grid=(A_HEADS // hw, nsb),
        in_specs=_attn_in_specs(sb, nsb, hw) + [pl.BlockSpec((hw, A_BLK, 2 * A_BLK), lambda hp, n: (hp, 0, 0))],
        out_specs=[out, out],
        out_shape=[jax.ShapeDtypeStruct((S, A_W), F32)] * 2,
        compiler_params=_params(("parallel", "parallel")),
    )(qkv, qkv, qkv, qkv, qkv, bias_p)


def _attn_bwd(qkv, do, lt, delta, bias_p, d, name, prev=(), out_dtype=F32):
    S = qkv.shape[2]
    hw = _attn_heads_per_step(d)
    sub = A_BLK * d
    nsub = max(1, A_TILES // (hw * d))
    sb = sub * nsub
    nsb = S // sb
    done = (nsub - 1) * sub

    def body(*refs):
        q_ref, kp_ref, kc_ref, vp_ref, vc_ref, do_ref, lt_ref, dl_ref, b_ref = refs[:9]
        pq_ref, pk_ref, pv_ref = refs[9:9 + len(prev)] if prev else (None, None, None)
        dq_ref, dk_ref, dv_ref, db_ref, ck, cv = refs[9 + len(prev):]
        n = pl.program_id(1)
        plus = lambda t, p_ref, idx: (t if p_ref is None else t + p_ref[idx]).astype(out_dtype)

        @pl.when(n == 0)
        def _():
            db_ref[...] = jnp.zeros_like(db_ref)
            ck[...] = jnp.zeros_like(ck)
            cv[...] = jnp.zeros_like(cv)

        @pl.when(n < nsb)
        def _():
            kj = lax.broadcasted_iota(jnp.int32, (A_BLK, 2 * A_BLK), 1)
            first = jnp.where((n == 0) & (kj < A_BLK), NEG, 0.0).astype(F32)
            no_first = jnp.zeros((A_BLK, 2 * A_BLK), F32)
            if done:
                dk_ref[0:done, :] = plus(ck[0:done, :], pk_ref, (slice(0, done), slice(None)))
                dv_ref[0:done, :] = plus(cv[0:done, :], pv_ref, (slice(0, done), slice(None)))

            def residue(r, u=0):
                rows = _rows(u * sub + r, d)
                behind = _rows(((nsub if u == 0 else u) - 1) * sub + r, d)

                def pair(pp):
                    lanes = pl.ds(2 * A_HD * pp, 2 * A_HD)
                    lt_r, dl_r = lt_ref[rows, lanes], dl_ref[rows, lanes]
                    kc, vc = kc_ref.at[0, pp], vc_ref.at[0, pp]
                    kb, vb = (kp_ref.at[0, pp], vp_ref.at[0, pp]) if u == 0 else (kc, vc)
                    k = jnp.concatenate([kb[behind, :], kc[rows, :]], axis=0)
                    v = jnp.concatenate([vb[behind, :], vc[rows, :]], axis=0)
                    q2 = _attn_stack(q_ref.at[0, pp][rows, :])
                    do2 = _attn_stack(do_ref[rows, lanes])
                    col = lambda t: jnp.concatenate([t[:, 0:1], t[:, A_HD:A_HD + 1]], axis=0)
                    s = _attn_scores(q_ref.at[0, pp][rows, :], k, b_ref, pp, first if u == 0 else no_first)
                    dp = _mm(do2, v, NT)
                    yield
                    p = jnp.exp(s - col(lt_r))
                    ds = p * (dp - col(dl_r))
                    db_ref[2 * pp] += ds[:A_BLK]
                    db_ref[2 * pp + 1] += ds[A_BLK:]
                    dq = _mm(ds, k)
                    dk = _mm(ds, q2, TN) * (A_HD ** -0.5)
                    dv = _mm(p, do2, TN)
                    yield
                    dq_ref[rows, lanes] = plus(_attn_unstack(dq) * (A_HD ** -0.5), pq_ref, (rows, lanes))
                    if u == 0:
                        dk_ref[behind, lanes] = plus(ck[behind, lanes] + dk[:A_BLK], pk_ref, (behind, lanes))
                        dv_ref[behind, lanes] = plus(cv[behind, lanes] + dv[:A_BLK], pv_ref, (behind, lanes))
                    else:
                        ck[behind, lanes] += dk[:A_BLK]
                        cv[behind, lanes] += dv[:A_BLK]
                    ck[rows, lanes] = dk[A_BLK:]
                    cv[rows, lanes] = dv[A_BLK:]

                return [pair(pp) for pp in range(hw // 2)]

            if nsub == 1:
                _for_residues(d, hw, residue)
            else:
                _round_robin([g for u in range(nsub) for r in range(d) for g in residue(r, u)])

        @pl.when(n == nsb)
        def _():
            dk_ref[...] = plus(ck[...], pk_ref, ...)
            dv_ref[...] = plus(cv[...], pv_ref, ...)

    w = A_HD * hw
    row = pl.BlockSpec((sb, w), lambda hp, n: (jnp.minimum(n, nsb - 1), hp))
    lag = pl.BlockSpec((sb, w), lambda hp, n: (jnp.maximum(n - 1, 0), hp))
    tab = pl.BlockSpec((hw, A_BLK, 2 * A_BLK), lambda hp, n: (hp, 0, 0))
    return pl.pallas_call(
        body, name=name,
        grid=(A_HEADS // hw, nsb + 1),
        in_specs=_attn_in_specs(sb, nsb, hw) + [row, row, row, tab] + ([row, lag, lag] if prev else []),
        out_specs=[row, lag, lag, tab],
        out_shape=[jax.ShapeDtypeStruct((S, A_W), out_dtype)] * 3
                  + [jax.ShapeDtypeStruct((A_HEADS, A_BLK, 2 * A_BLK), F32)],
        scratch_shapes=[pltpu.VMEM((sb, w), F32), pltpu.VMEM((sb, w), F32)],
        compiler_params=_params(("parallel", "arbitrary")),
    )(qkv, qkv, qkv, qkv, qkv, do, lt, delta, bias_p, *prev)


def _rel_bias_grad(dbs, buckets):
    def body(d1, d2, d3, bk_ref, o_ref):
        row = lax.broadcasted_iota(jnp.int32, (A_HEADS, 128), 0)
        lane = lax.broadcasted_iota(jnp.int32, (A_HEADS, 128), 1)
        acc = jnp.zeros((A_HEADS, 128), F32)
        for p, dref in enumerate((d1, d2, d3)):
            bk = bk_ref[p]
            for h in range(A_HEADS):
                ds = dref[h]
                for b in range(N_BUCKETS):
                    s = jnp.sum(jnp.where(bk == b, ds, 0.0), keepdims=True)
                    acc = acc + jnp.where((row == h) & (lane == b), s, 0.0)
        o_ref[...] = acc

    return pl.pallas_call(
        body, name="rel_bias_grad",
        out_shape=jax.ShapeDtypeStruct((A_HEADS, 128), F32),
        compiler_params=_params(),
    )(*dbs, buckets)


def _tri(c):
    t = np.tril(np.ones((c, c), np.float32))
    return jnp.asarray(t), jnp.asarray(t.T.copy())


def _fill_above(ref, x, pad):
    ref[0:G_SUB, :] = jnp.full((G_SUB, x.shape[1]), pad, F32)
    ref[G_SUB:, :] = x


def _fill_below(ref, x, pad):
    ref[0:x.shape[0], :] = x
    ref[x.shape[0]:, :] = jnp.full((G_SUB, x.shape[1]), pad, F32)


def _hgrn_gates(q_ref, f_ref, lbp_ref, tri_ref):
    qraw = q_ref[...]
    sq = _sigmoid(qraw)
    q = qraw * sq
    sg = _sigmoid(f_ref[...])
    lb = _sigmoid(lbp_ref[0:1, :] - lbp_ref[1:2, :])
    f = lb + (1.0 - lb) * sg
    k = 1.0 - f
    b = _mm_exact(tri_ref[...], jnp.log(f))
    return qraw, sq, q, sg, lb, f, k, b


def _hgrn_col(C, base, idx, hps):
    return pl.BlockSpec((C, hps * G_DK), lambda h, n: (idx(n), base * (G_HEADS // hps) + h))


def _round_robin(stages):
    live = list(stages)
    while live:
        nxt = []
        for g in live:
            try:
                next(g)
                nxt.append(g)
            except StopIteration:
                pass
        live = nxt


def _hgrn_levels(C):
    out, m = [], G_SUB
    while 2 * m <= C:
        out.append(m)
        m *= 2
    return out


def _hgrn_level_masks(C):
    ti = np.arange(C)[:, None]
    si = np.arange(C)[None, :]
    return jnp.asarray(np.stack([((ti // (2 * m) == si // (2 * m)) & (ti - si >= G_SUB)).astype(np.float32)
                                 for m in _hgrn_levels(C)]))


def _hgrn_level(b, q, k, C, m):
    zeros = jnp.zeros((m, G_DK), F32)
    eq, ek, qt, kt = [], [], [], []
    for blk in range(0, C // m, 2):
        lo, mid, hi = blk * m, (blk + 1) * m, (blk + 2) * m
        ref = b[mid:mid + 1]
        e_right = jnp.exp(b[mid:hi] - ref)
        e_left = jnp.exp(ref - b[lo:mid])
        eq += [zeros, e_right]
        ek += [e_left, zeros]
        qt += [zeros, q[mid:hi] * e_right]
        kt += [k[lo:mid] * e_left, zeros]
    cat = lambda parts: jnp.concatenate(parts, axis=0)
    return cat(qt), cat(kt), cat(eq), cat(ek)


def _hgrn_fwd(proj, hgrn_lb, onorm_g, C=G_CHUNK):
    S = proj.shape[0]
    nc = S // C
    tri, _ = _tri(C)
    masks = _hgrn_level_masks(C)
    hps = G_HPS_FWD

    def body(q_ref, f_ref, i_ref, z_ref, lbp_ref, go_ref, tri_ref, pm_ref, o_ref, ob_ref, st_ref, St, kp, vp, fp):
        @pl.when(pl.program_id(1) == 0)
        def _():
            St[...] = jnp.zeros_like(St)

        heads = []
        for hh in range(hps):
            ln = pl.ds(G_DK * hh, G_DK)
            heads.append(head(
                q_ref.at[:, ln], f_ref.at[:, ln], i_ref.at[:, ln], z_ref.at[:, ln], lbp_ref.at[:, ln], go_ref,
                tri_ref, pm_ref, o_ref.at[:, ln], ob_ref.at[:, ln], st_ref.at[0, hh], St.at[hh], kp.at[hh], vp.at[hh],
                fp.at[hh]))
        _round_robin(heads)

    def head(q_ref, f_ref, i_ref, z_ref, lbp_ref, go_ref, tri_ref, pm_ref, o_ref, ob_ref, st_ref, St, kp, vp, fp):
        _, _, q, _, _, f, k, b = _hgrn_gates(q_ref, f_ref, lbp_ref, tri_ref)
        v = i_ref[...]
        bC = b[C - 1:C, :]
        S0 = St[...]
        o = _mm(q * jnp.exp(b), S0, NT)
        yield
        _fill_above(kp, k, 0.0)
        _fill_above(vp, v, 0.0)
        _fill_above(fp, f, 1.0)
        near = []
        for r0 in range(0, C, G_RB):
            qb = q[r0:r0 + G_RB]
            acc = e = None
            for l in range(G_SUB):
                rows = pl.ds(G_SUB - l + r0, G_RB)
                if l > 0:
                    fl = fp[pl.ds(G_SUB - l + 1 + r0, G_RB), :]
                    e = fl if e is None else e * fl
                kl = kp[rows, :]
                a = jnp.sum(qb * kl if e is None else qb * kl * e, axis=-1, keepdims=True)
                t = a * vp[rows, :]
                acc = t if acc is None else acc + t
            near.append(acc)
        o = o + jnp.concatenate(near, axis=0)
        yield
        a_off = jnp.zeros((C, C), F32)
        for lv, m in enumerate(_hgrn_levels(C)):
            qt, kt, _, _ = _hgrn_level(b, q, k, C, m)
            prod = _mm_split(_split(qt), _split(kt), NT) if m == G_SUB else _mm(qt, kt, NT)
            a_off = a_off + pm_ref[lv] * prod
        yield
        o = o + _mm(a_off, v)
        S1 = S0 * jnp.exp(bC) + _mm(v, k * jnp.exp(bC - b), TN)
        St[...] = S1
        st_ref[...] = S1
        o_ref[...] = o
        r = lax.rsqrt(jnp.mean(o * o, axis=-1, keepdims=True) + EPS)
        z = z_ref[...]
        ob_ref[...] = (o * r * go_ref[...] * (z * _sigmoid(z))).astype(MXU_DTYPE)

    ident = lambda n: n
    w = hps * G_DK
    out = pl.BlockSpec((C, w), lambda h, n: (n, h))
    return pl.pallas_call(
        body, name="hgrn_fwd",
        grid=(G_HEADS // hps, nc),
        in_specs=[_hgrn_col(C, base, ident, hps) for base in (2, 3, 4, 5)] + [
                  pl.BlockSpec((2, w), lambda h, n: (0, h)),
                  pl.BlockSpec((1, G_DK), lambda h, n: (0, 0)),
                  pl.BlockSpec((C, C), lambda h, n: (0, 0)),
                  pl.BlockSpec(masks.shape, lambda h, n: (0, 0, 0))],
        out_specs=[out, out, pl.BlockSpec((1, hps, G_DK, G_DK), lambda h, n: (n, h, 0, 0))],
        out_shape=[jax.ShapeDtypeStruct((S, G_W), F32), jax.ShapeDtypeStruct((S, G_W), MXU_DTYPE),
                   jax.ShapeDtypeStruct((nc, G_HEADS, G_DK, G_DK), F32)],
        scratch_shapes=[pltpu.VMEM((hps, G_DK, G_DK), F32)] + [pltpu.VMEM((hps, C + G_SUB, G_DK), F32)] * 3,
        compiler_params=_params(("parallel", "arbitrary")),
    )(proj, proj, proj, proj, hgrn_lb, onorm_g, tri, masks)


def _hgrn_bwd(proj, o_raw, dob, states, hgrn_lb, onorm_g, C=G_CHUNK):
    S = proj.shape[0]
    nc = S // C
    tri, triu = _tri(C)
    masks = _hgrn_level_masks(C)
    hps = G_HPS_BWD

    def body(q_ref, f_ref, i_ref, z_ref, o_ref, dob_ref, s0_ref, s1_ref, lbp_ref, go_ref, tri_ref, triu_ref,
             pm_ref, dq_ref, df_ref, di_ref, dz_ref, dlb_ref, dgo_ref, dSt, *shifted):
        @pl.when(pl.program_id(1) == 0)
        def _():
            dSt[...] = jnp.zeros_like(dSt)
            dlb_ref[...] = jnp.zeros_like(dlb_ref)
            dgo_ref[...] = jnp.zeros_like(dgo_ref)

        heads = []
        for hh in range(hps):
            ln = pl.ds(G_DK * hh, G_DK)
            heads.append(head(
                q_ref.at[:, ln], f_ref.at[:, ln], i_ref.at[:, ln], z_ref.at[:, ln], o_ref.at[:, ln],
                dob_ref.at[:, ln], s0_ref.at[0, hh], s1_ref.at[0, hh], lbp_ref.at[:, ln], go_ref, tri_ref, triu_ref,
                pm_ref, dq_ref.at[:, ln], df_ref.at[:, ln], di_ref.at[:, ln], dz_ref.at[:, ln], dlb_ref.at[:, ln],
                dgo_ref.at[pl.ds(8 * hh, 8), :], dSt.at[hh], *[t.at[hh] for t in shifted]))
        _round_robin(heads)

    def head(q_ref, f_ref, i_ref, z_ref, o_ref, dob_ref, s0_ref, s1_ref, lbp_ref, go_ref, tri_ref, triu_ref,
             pm_ref, dq_ref, df_ref, di_ref, dz_ref, dlb_ref, dgo_ref, dSt, kp, vp, fp, qn, dn_, fn, xs, dac):
        cn = nc - 1 - pl.program_id(1)
        qraw, sq, q, sg, lb, f, k, b = _hgrn_gates(q_ref, f_ref, lbp_ref, tri_ref)
        v = i_ref[...]
        bC = b[C - 1:C, :]
        eb = jnp.exp(b)
        ecb = jnp.exp(bC - b)
        o = o_ref[...]
        z = z_ref[...]
        sz = _sigmoid(z)
        go = go_ref[...]
        g_ob = dob_ref[...]
        r = lax.rsqrt(jnp.mean(o * o, axis=-1, keepdims=True) + EPS)
        nh = o * r
        dnrm = g_ob * (z * sz)
        dz_ref[...] = (g_ob * (nh * go) * (sz * (1.0 + z * (1.0 - sz)))).astype(MXU_DTYPE)
        dgo_ref[0:1, :] += jnp.sum(dnrm * nh, axis=0, keepdims=True)
        dn = dnrm * go
        do = r * (dn - nh * jnp.mean(dn * nh, axis=-1, keepdims=True))

        yield
        S0 = jnp.where(cn == 0, 0.0, s0_ref[...])
        S1 = s1_ref[...]
        dS1 = dSt[...]
        dq = eb * _mm(do, S0)
        dk = ecb * _mm(v, dS1)
        dv = _mm(k * ecb, dS1, NT)
        bnd = jnp.sum(dS1 * S1, axis=0, keepdims=True)
        dSt[...] = dS1 * jnp.exp(bC) + _mm(do, q * eb, TN)

        _fill_above(kp, k, 0.0)
        _fill_above(vp, v, 0.0)
        _fill_above(fp, f, 1.0)
        _fill_below(qn, q, 0.0)
        _fill_below(dn_, do, 0.0)
        _fill_below(fn, f, 1.0)
        yield
        for r0 in range(0, C, G_RB):
            do_b = do[r0:r0 + G_RB]
            for l in range(G_SUB):
                xs[pl.ds(l * C + r0, G_RB), :] = (do_b * vp[pl.ds(G_SUB - l + r0, G_RB), :]).astype(MXU_DTYPE)
        dac[0:G_SUB * C, :] = _mm(xs[...], jnp.ones((G_DK, G_DK), MXU_DTYPE))
        dac[G_SUB * C:, :] = jnp.zeros((G_SUB, G_DK), F32)
        yield
        near_q, near_k, near_v = [], [], []
        for r0 in range(0, C, G_RB):
            k_b = k[r0:r0 + G_RB]
            aq = ak = av = e = e2 = None
            for l in range(G_SUB):
                down, up = pl.ds(G_SUB - l + r0, G_RB), pl.ds(l + r0, G_RB)
                if l > 0:
                    fl = fp[pl.ds(G_SUB - l + 1 + r0, G_RB), :]
                    e = fl if e is None else e * fl
                    fu = fn[up, :]
                    e2 = fu if e2 is None else e2 * fu
                kl = kp[down, :]
                t = dac[pl.ds(l * C + r0, G_RB), :] * (kl if e is None else kl * e)
                aq = t if aq is None else aq + t
                qu = qn[up, :]
                qe = qu if e2 is None else qu * e2
                dou = dn_[up, :]
                a2 = jnp.sum(qe * k_b, axis=-1, keepdims=True)
                t = dac[pl.ds(l * C + l + r0, G_RB), :] * qe
                ak = t if ak is None else ak + t
                t = a2 * dou
                av = t if av is None else av + t
            near_q.append(aq)
            near_k.append(ak)
            near_v.append(av)
        dq = dq + jnp.concatenate(near_q, axis=0)
        dk = dk + jnp.concatenate(near_k, axis=0)
        dv = dv + jnp.concatenate(near_v, axis=0)

        yield
        da_all = _mm(do, v, NT)
        a_off = jnp.zeros((C, C), F32)
        for lv, m in enumerate(_hgrn_levels(C)):
            qt, kt, eq, ek = _hgrn_level(b, q, k, C, m)
            da_m = pm_ref[lv] * da_all
            if m == G_SUB:
                qs, ks, das = _split(qt), _split(kt), _split(da_m)
                a_off = a_off + pm_ref[lv] * _mm_split(qs, ks, NT)
                dq = dq + _mm_split(das, ks, NN) * eq
                dk = dk + _mm_split(das, qs, TN) * ek
            else:
                a_off = a_off + pm_ref[lv] * _mm(qt, kt, NT)
                dq = dq + _mm(da_m, kt) * eq
                dk = dk + _mm(da_m, qt, TN) * ek
        dv = dv + _mm(a_off, do, TN)

        yield
        row = lax.broadcasted_iota(jnp.int32, (C, 1), 0)
        db = q * dq - k * dk + jnp.where(row == C - 1, bnd, 0.0)
        dg = _mm_exact(triu_ref[...], db)
        df = dg / f - dk
        df_ref[...] = (df * (1.0 - lb) * (sg * (1.0 - sg))).astype(MXU_DTYPE)
        dlb_ref[0:1, :] += jnp.sum(df * (1.0 - sg), axis=0, keepdims=True)
        dq_ref[...] = (dq * (sq * (1.0 + qraw * (1.0 - sq)))).astype(MXU_DTYPE)
        di_ref[...] = dv.astype(MXU_DTYPE)

    rev = lambda n: nc - 1 - n
    w = hps * G_DK
    blk = pl.BlockSpec((C, w), lambda h, n: (nc - 1 - n, h))
    return pl.pallas_call(
        body, name="hgrn_bwd",
        grid=(G_HEADS // hps, nc),
        in_specs=[_hgrn_col(C, base, rev, hps) for base in (2, 3, 4, 5)] + [
                  blk, blk,
                  pl.BlockSpec((1, hps, G_DK, G_DK), lambda h, n: (jnp.maximum(nc - 2 - n, 0), h, 0, 0)),
                  pl.BlockSpec((1, hps, G_DK, G_DK), lambda h, n: (nc - 1 - n, h, 0, 0)),
                  pl.BlockSpec((2, w), lambda h, n: (0, h)),
                  pl.BlockSpec((1, G_DK), lambda h, n: (0, 0)),
                  pl.BlockSpec((C, C), lambda h, n: (0, 0)),
                  pl.BlockSpec((C, C), lambda h, n: (0, 0)),
                  pl.BlockSpec(masks.shape, lambda h, n: (0, 0, 0))],
        out_specs=[blk, blk, blk, blk,
                   pl.BlockSpec((8, w), lambda h, n: (0, h)),
                   pl.BlockSpec((8 * hps, G_DK), lambda h, n: (h, 0))],
        out_shape=[jax.ShapeDtypeStruct((S, G_W), MXU_DTYPE)] * 4
                  + [jax.ShapeDtypeStruct((8, G_W), F32), jax.ShapeDtypeStruct((8 * G_HEADS, G_DK), F32)],
        scratch_shapes=[pltpu.VMEM((hps, G_DK, G_DK), F32)] + [pltpu.VMEM((hps, C + G_SUB, G_DK), F32)] * 6
                       + [pltpu.VMEM((hps, G_SUB * C, G_DK), MXU_DTYPE),
                          pltpu.VMEM((hps, G_SUB * C + G_SUB, G_DK), F32)],
        compiler_params=_params(("parallel", "arbitrary")),
    )(proj, proj, proj, proj, o_raw, dob, states, states, hgrn_lb, onorm_g, tri, triu, masks)


def _tail(x, target, os, ls, ob, proj, mod3, final_g, wa, wb, wo, tm=256):
    S = x.shape[0]
    nt = S // tm

    def body(x_ref, t_ref, o1, o2, o3, l1, l2, l3, za_ref, ob_ref, ga_ref, gb_ref, mod_ref, fg_ref,
             wa_ref, wb_ref, wo_ref,
             lt_ref, dx2_ref, do_ref, dl_ref, dza_ref, dob_ref, dga_ref, dgb_ref, sums_ref,
             gwa_ref, gwb_ref, gwo_ref, acc_a, acc_b, acc_o):
        i = pl.program_id(0)

        @pl.when(i == 0)
        def _():
            sums_ref[...] = jnp.zeros_like(sums_ref)
            acc_a[...] = jnp.zeros_like(acc_a)
            acc_b[...] = jnp.zeros_like(acc_b)
            acc_o[...] = jnp.zeros_like(acc_o)

        a1, a2, a3 = l1[...], l2[...], l3[...]
        lm = jnp.maximum(jnp.maximum(a1, a2), a3)
        e1, e2, e3 = jnp.exp(a1 - lm), jnp.exp(a2 - lm), jnp.exp(a3 - lm)
        lden = e1 + e2 + e3
        ao = (e1 * o1[...] + e2 * o2[...] + e3 * o3[...]) / lden
        lt_ref[...] = lm + jnp.log(lden)
        za = za_ref[...]
        sza = _sigmoid(za)
        oa_v, ob_v = (ao * (za * sza)).astype(MXU_DTYPE), ob_ref[...]
        pa = _mm(oa_v, wa_ref[...])
        pb = _mm(ob_v, wb_ref[...])
        sa, sb = _sigmoid(ga_ref[...]), _sigmoid(gb_ref[...])
        ym = sa * pa + sb * pb
        u = _mm(ym, wo_ref[...])
        gate = mod_ref[2:3, :]
        fg = fg_ref[...]
        x2 = x_ref[...] + gate * u
        r2 = lax.rsqrt(jnp.mean(x2 * x2, axis=-1, keepdims=True) + EPS)
        xn2 = x2 * r2
        e = xn2 * fg - t_ref[...]
        dy = e * (1.0 / D)
        dn = dy * fg
        dx2 = r2 * (dn - xn2 * jnp.mean(dn * xn2, axis=-1, keepdims=True))
        dx2_ref[...] = dx2
        sums_ref[0:1, :] += jnp.sum(dy * xn2, axis=0, keepdims=True)
        sums_ref[1:2, :] += jnp.sum(dx2 * u, axis=0, keepdims=True)
        sums_ref[2:3, :] += (0.5 / D) * jnp.sum(e * e, axis=0, keepdims=True)
        du = dx2 * gate
        dym = _mm(du, wo_ref[...], NT)
        acc_o[...] += _mm(ym, du, TN)
        dpa, dpb = dym * sa, dym * sb
        dga_ref[...] = (dym * pa * (sa * (1.0 - sa))).astype(MXU_DTYPE)
        dgb_ref[...] = (dym * pb * (sb * (1.0 - sb))).astype(MXU_DTYPE)
        doa = _mm(dpa, wa_ref[...], NT)
        dza_ref[...] = (doa * ao * (sza * (1.0 + za * (1.0 - sza)))).astype(MXU_DTYPE)
        do = doa * (za * sza)
        do_ref[...] = do
        prod = do * ao
        for h in range(A_HEADS):
            sl = slice(A_HD * h, A_HD * (h + 1))
            dl_ref[:, sl] = jnp.broadcast_to(jnp.sum(prod[:, sl], axis=-1, keepdims=True), (tm, A_HD))
        dob_ref[...] = _mm(dpb, wb_ref[...], NT)
        acc_a[...] += _mm(oa_v, dpa, TN)
        acc_b[...] += _mm(ob_v, dpb, TN)

        @pl.when(i == nt - 1)
        def _():
            pltpu.sync_copy(acc_a, gwa_ref)
            pltpu.sync_copy(acc_b, gwb_ref)
            pltpu.sync_copy(acc_o, gwo_ref)

    row = lambda w: pl.BlockSpec((tm, w), lambda i: (i, 0))
    full = lambda a, b: pl.BlockSpec((a, b), lambda i: (0, 0))
    any_spec = pl.BlockSpec(memory_space=pl.ANY)
    return pl.pallas_call(
        body, name="tail",
        grid=(nt,),
        in_specs=[row(D), row(D)] + [row(A_W)] * 6 + [pl.BlockSpec((tm, A_W), lambda i: (i, 3)), row(D),
                  pl.BlockSpec((tm, D), lambda i: (i, 6)), pl.BlockSpec((tm, D), lambda i: (i, 7)),
                  full(8, D), full(1, D), full(A_W, D), full(D, D), full(D, D)],
        out_specs=[row(A_W), row(D), row(A_W), row(A_W), row(A_W), row(D), row(D), row(D), full(8, D),
                   any_spec, any_spec, any_spec],
        out_shape=[jax.ShapeDtypeStruct((S, A_W), F32),
                   jax.ShapeDtypeStruct((S, D), F32), jax.ShapeDtypeStruct((S, A_W), F32),
                   jax.ShapeDtypeStruct((S, A_W), F32), jax.ShapeDtypeStruct((S, A_W), MXU_DTYPE),
                   jax.ShapeDtypeStruct((S, D), F32), jax.ShapeDtypeStruct((S, D), MXU_DTYPE),
                   jax.ShapeDtypeStruct((S, D), MXU_DTYPE), jax.ShapeDtypeStruct((8, D), F32),
                   jax.ShapeDtypeStruct((A_W, D), F32), jax.ShapeDtypeStruct((D, D), F32),
                   jax.ShapeDtypeStruct((D, D), F32)],
        scratch_shapes=[pltpu.VMEM((A_W, D), F32), pltpu.VMEM((D, D), F32), pltpu.VMEM((D, D), F32)],
        compiler_params=_params(("arbitrary",)),
    )(x, target, *os, *ls, proj, ob, proj, proj, mod3, final_g, wa, wb, wo)


def _piece_parts(pieces):
    parts, where = [], []
    for k, piece in enumerate(pieces):
        off = 0
        for part in piece:
            parts.append(part)
            where.append((k, off, part.shape[1]))
            off += part.shape[1]
        assert off == D
    return parts, where


def _dh(pieces, w_in_g, x, dx2, mod3, norm_g, grads, tm=256):
    S = x.shape[0]
    ni = S // tm
    ng = len(grads)
    parts, where = _piece_parts(pieces)
    npart = len(parts)

    def body(*refs):
        p_refs = refs[:npart]
        w_ref, x_ref, dx2_ref, mod_ref, g_ref = refs[npart:npart + 5]
        g_ins = refs[npart + 5:npart + 5 + ng]
        gx_ref, sums_ref = refs[npart + 5 + ng:npart + 7 + ng]
        g_outs = refs[npart + 7 + ng:npart + 7 + 2 * ng]
        w_all, send_sems, recv_sems, local_sems = refs[npart + 7 + 2 * ng:]
        i = pl.program_id(0)
        start, wait = _all_to_all_copies(g_ins, g_outs, send_sems, recv_sems, local_sems)

        @pl.when(i == 0)
        def _():
            start()
            sums_ref[...] = jnp.zeros_like(sums_ref)
            pltpu.sync_copy(w_ref, w_all)

        dh = None
        for p_ref, (k, off, width) in zip(p_refs, where):
            term = _mm(p_ref[...], w_all[k, :, off:off + width], NT)
            dh = term if dh is None else dh + term
        xv = x_ref[...]
        g = g_ref[...]
        sc1 = 1.0 + mod_ref[1:2, :]
        r = lax.rsqrt(jnp.mean(xv * xv, axis=-1, keepdims=True) + EPS)
        xn = xv * r
        sums_ref[0:1, :] += jnp.sum(dh, axis=0, keepdims=True)
        sums_ref[1:2, :] += jnp.sum(dh * (xn * g), axis=0, keepdims=True)
        sums_ref[2:3, :] += jnp.sum(dh * sc1 * xn, axis=0, keepdims=True)
        dxn = dh * sc1 * g
        gx_ref[...] = dx2_ref[...] + r * (dxn - xn * jnp.mean(dxn * xn, axis=-1, keepdims=True))

        @pl.when(i == ni - 1)
        def _():
            wait()

    row = pl.BlockSpec((tm, D), lambda i: (i, 0))
    any_spec = pl.BlockSpec(memory_space=pl.ANY)
    return pl.pallas_call(
        body, name="dh_scatter",
        grid=(ni,),
        in_specs=[pl.BlockSpec((tm, width), lambda i: (i, 0)) for _, _, width in where]
                 + [any_spec, row, row,
                    pl.BlockSpec((8, D), lambda i: (0, 0)),
                    pl.BlockSpec((1, D), lambda i: (0, 0))]
                 + [any_spec] * ng,
        out_specs=[row, pl.BlockSpec((8, D), lambda i: (0, 0))] + [any_spec] * ng,
        out_shape=[jax.ShapeDtypeStruct((S, D), F32), jax.ShapeDtypeStruct((8, D), F32)]
                  + [jax.ShapeDtypeStruct(g.shape, g.dtype) for g in grads],
        scratch_shapes=[pltpu.VMEM(w_in_g.shape, w_in_g.dtype),
                        pltpu.SemaphoreType.DMA((ng, N_DEV - 1)), pltpu.SemaphoreType.DMA((ng, N_DEV - 1)),
                        pltpu.SemaphoreType.DMA((ng,))],
        compiler_params=_params(("arbitrary",)),
    )(*parts, w_in_g, x, dx2, mod3, norm_g, *grads)


def _gw_in(ht, pieces, grads, tm=1024):
    S = ht.shape[1]
    nt = S // tm
    ng = len(grads)
    parts, where = _piece_parts(pieces)
    npart = len(parts)

    def body(*refs):
        h_ref, p_refs = refs[0], refs[1:1 + npart]
        g_ins = refs[1 + npart:1 + npart + ng]
        o_ref = refs[1 + npart + ng]
        g_outs = refs[2 + npart + ng:2 + npart + 2 * ng]
        acc, send_sems, recv_sems, local_sems = refs[2 + npart + 2 * ng:]
        j, i = pl.program_id(0), pl.program_id(1)
        start, wait = _all_to_all_copies(g_ins, g_outs, send_sems, recv_sems, local_sems)

        @pl.when((j == 0) & (i == 0))
        def _():
            start()

        @pl.when(i == 0)
        def _():
            acc[...] = jnp.zeros_like(acc)

        for k in range(N_DEV):
            @pl.when(j == k)
            def _(k=k):
                for p_ref, (kk, off, width) in zip(p_refs, where):
                    if kk == k:
                        acc[:, off:off + width] += _mm(h_ref[...], p_ref[...])

        @pl.when(i == nt - 1)
        def _():
            o_ref[0] = acc[...].astype(XCHG_DTYPE)

        @pl.when((j == N_DEV - 1) & (i == nt - 1))
        def _():
            wait()

    def part_spec(k, width):
        return pl.BlockSpec((tm, width), lambda j, i: (jnp.where(j == k, i, 0), 0))

    any_spec = pl.BlockSpec(memory_space=pl.ANY)
    return pl.pallas_call(
        body, name="gw_in_scatter",
        grid=(N_DEV, nt),
        in_specs=[pl.BlockSpec((D, tm), lambda j, i: (0, i))] + [part_spec(k, width) for k, _, width in where] + [any_spec] * ng,
        out_specs=[pl.BlockSpec((1, D, D), lambda j, i: (j, 0, 0))] + [any_spec] * ng,
        out_shape=[jax.ShapeDtypeStruct((N_DEV, D, D), XCHG_DTYPE)]
                  + [jax.ShapeDtypeStruct(g.shape, g.dtype) for g in grads],
        scratch_shapes=[pltpu.VMEM((D, D), F32),
                        pltpu.SemaphoreType.DMA((ng, N_DEV - 1)), pltpu.SemaphoreType.DMA((ng, N_DEV - 1)),
                        pltpu.SemaphoreType.DMA((ng,))],
        compiler_params=_params(("arbitrary", "arbitrary")),
    )(ht, *parts, *grads)


def _adamw_math(w, g, m, v):
    m = ADAM_B1 * m + (1.0 - ADAM_B1) * g
    v = ADAM_B2 * v + (1.0 - ADAM_B2) * (g * g)
    m_hat = m / (1.0 - ADAM_B1 ** ADAM_STEP)
    v_hat = v / (1.0 - ADAM_B2 ** ADAM_STEP)
    delta = -ADAM_LR * (m_hat / (jnp.sqrt(v_hat) + ADAM_EPS) + ADAM_WD * w)
    return delta, m, v


def _adamw_big(recv, w, m, v, name, tr=128):
    M, N = w.shape
    tr = min(tr, M)

    def body(r_ref, w_ref, m_ref, v_ref, g_ref, d_ref, nm_ref, nv_ref):
        g = r_ref[0].astype(F32)
        for j in range(1, N_DEV):
            g = g + r_ref[j].astype(F32)
        g_ref[...] = g
        d_ref[...], nm_ref[...], nv_ref[...] = _adamw_math(w_ref[...], g, m_ref[...], v_ref[...])

    blk = pl.BlockSpec((tr, N), lambda i: (i, 0))
    return pl.pallas_call(
        body, name=name,
        grid=(M // tr,),
        in_specs=[pl.BlockSpec((N_DEV, tr, N), lambda i: (0, i, 0)), blk, blk, blk],
        out_specs=[blk] * 4,
        out_shape=[jax.ShapeDtypeStruct((M, N), F32)] * 4,
        compiler_params=_params(("parallel",)),
    )(recv, w, m, v)


def _adamw_w_ada(c64, dmod64, w, m, v):
    def body(c_ref, dm_ref, w_ref, m_ref, v_ref, g_ref, d_ref, nm_ref, nv_ref):
        cv = c_ref[...]
        g = _mm(cv * _sigmoid(cv), dm_ref[...], TN)
        g_ref[...] = g
        d_ref[...], nm_ref[...], nv_ref[...] = _adamw_math(w_ref[...], g, m_ref[...], v_ref[...])

    return pl.pallas_call(
        body, name="adamw_w_ada",
        out_shape=[jax.ShapeDtypeStruct(w.shape, F32)] * 4,
        compiler_params=_params(),
    )(c64, dmod64, w, m, v)


P_MOD, P_NORM, P_ONORM, P_RELB, P_LB, P_FINAL, P_LOSS, P_END = (0, 3 * D, 4 * D, 5 * D, 6 * D, 7 * D, 8 * D, 9 * D)


def _adamw_small(packed, b_ada, norm_g, onorm_g, relb, hgrn_lb, final_g, ms, vs):
    def body(pk_ref, b_ref, ng_ref, og_ref, rb_ref, lb_ref, fg_ref,
             mb, mn, mo, mr, ml, mf, vb, vn, vo, vr, vl, vf,
             loss_ref, gb, gn, go, gr, gl, gf, db, dn, do, dr, dl, df,
             nmb, nmn, nmo, nmr, nml, nmf, nvb, nvn, nvo, nvr, nvl, nvf):
        tot = pk_ref[0:1, :]
        for j in range(1, N_DEV):
            tot = tot + pk_ref[8 * j:8 * j + 1, :]
        loss_ref[...] = jnp.broadcast_to(jnp.sum(tot[:, P_LOSS:P_END], axis=-1, keepdims=True), (8, 128))

        def upd(g, w_ref, m_ref, v_ref, g_out, d_out, m_out, v_out):
            g_out[...] = g
            d_out[...], m_out[...], v_out[...] = _adamw_math(w_ref[...], g, m_ref[...], v_ref[...])

        upd(tot[:, P_MOD:P_NORM], b_ref, mb, vb, gb, db, nmb, nvb)
        upd(tot[:, P_NORM:P_ONORM], ng_ref, mn, vn, gn, dn, nmn, nvn)
        g_on = tot[:, P_ONORM:P_ONORM + G_DK]
        for h in range(1, G_HEADS):
            g_on = g_on + tot[:, P_ONORM + G_DK * h:P_ONORM + G_DK * (h + 1)]
        upd(g_on, og_ref, mo, vo, go, do, nmo, nvo)
        upd(tot[:, P_RELB:P_LB], rb_ref, mr, vr, gr, dr, nmr, nvr)
        a = lb_ref[...]
        lb = _sigmoid(a[0:1, :] - a[1:2, :])
        g0 = tot[:, P_LB:P_FINAL] * lb * (1.0 - lb)
        row = lax.broadcasted_iota(jnp.int32, (2, D), 0)
        upd(jnp.where(row == 0, g0, -g0), lb_ref, ml, vl, gl, dl, nml, nvl)
        upd(tot[:, P_FINAL:P_LOSS], fg_ref, mf, vf, gf, df, nmf, nvf)

    shapes = [b_ada.shape, norm_g.shape, onorm_g.shape, relb.shape, hgrn_lb.shape, final_g.shape]
    outs = [jax.ShapeDtypeStruct((8, 128), F32)] + [jax.ShapeDtypeStruct(s, F32) for s in shapes] * 4
    return pl.pallas_call(
        body, name="adamw_small",
        out_shape=outs,
        compiler_params=_params(),
    )(packed, b_ada, norm_g, onorm_g, relb, hgrn_lb, final_g, *ms, *vs)


def _local_step(x, target, mod3, norm_g, w_in_g, onorm_g, wa_blk, wb_blk, wo_blk, rel_bias, hgrn_lb, final_g):
    buckets = jnp.asarray(_bucket_tables())
    bias = _bias_tables(rel_bias, buckets)
    proj, ht, qkv, wa_g, wb_g, wo_g = _inproj(x, mod3, norm_g, w_in_g, [wa_blk, wb_blk, wo_blk])
    wa = wa_g.transpose(1, 0, 2).reshape(A_W, D)
    wb = wb_g.reshape(D, D)
    wo = wo_g.reshape(D, D)
    os, ls = [], []
    for p, (_, d) in enumerate(PATTERNS):
        o, l = _attn_fwd(qkv, bias[p], d, "attn_fwd_d%d" % d)
        os.append(o)
        ls.append(l)
    o_raw, ob, states = _hgrn_fwd(proj, hgrn_lb, onorm_g)
    lt, dx2, do, delta, dza, dob, dga, dgb, tsums, gwa, gwb, gwo = _tail(
        x, target, os, ls, ob, proj, mod3, final_g, wa, wb, wo)
    dbs, acc = [None] * len(PATTERNS), ()
    for p in reversed(range(len(PATTERNS))):
        d = PATTERNS[p][1]
        *acc, dbs[p] = _attn_bwd(qkv, do, lt, delta, bias[p], d, "attn_bwd_d%d" % d, prev=tuple(acc),
                                 out_dtype=MXU_DTYPE if p == 0 else F32)
    dqa, dka, dva = acc
    g_relb = _rel_bias_grad(dbs, buckets)
    dqb, dfb, dib, dzb, dlb, dgo = _hgrn_bwd(proj, o_raw, dob, states, hgrn_lb, onorm_g)
    pieces = [[dqa, dka], [dva, dza], [dqb], [dfb], [dib], [dzb], [dga], [dgb]]
    small = [gwa.astype(XCHG_DTYPE).reshape(A_W, N_DEV, D // N_DEV).transpose(1, 0, 2),
             gwb.astype(XCHG_DTYPE).reshape(N_DEV, D // N_DEV, D),
             gwo.astype(XCHG_DTYPE).reshape(N_DEV, D // N_DEV, D)]
    gw_in, *received_small = _gw_in(ht, pieces, small)
    gx, hsums, received_in = _dh(pieces, w_in_g, x, dx2, mod3, norm_g, [gw_in])
    received = [received_in] + received_small
    row = jnp.concatenate([
        hsums[0], hsums[1], tsums[1],
        hsums[2],
        dgo.reshape(G_HEADS, 8, G_DK)[:, 0].reshape(-1),
        g_relb.reshape(-1),
        dlb[0],
        tsums[0],
        tsums[2],
    ])
    return gx, received, row


def kernel(x, c, w_ada, b_ada, norm_g, w_in, hgrn_onorm_g, w_branch_a, w_branch_b, w_out, rel_bias, hgrn_lb, final_g, loss_target, m_w_ada, m_b_ada, m_norm_g, m_w_in, m_hgrn_onorm_g, m_w_branch_a, m_w_branch_b, m_w_out, m_rel_bias, m_hgrn_lb, m_final_g, v_w_ada, v_b_ada, v_norm_g, v_w_in, v_hgrn_onorm_g, v_w_branch_a, v_w_branch_b, v_w_out, v_rel_bias, v_hgrn_lb, v_final_g):
    me = 4 * lax.axis_index("x") + 2 * lax.axis_index("y") + lax.axis_index("c")
    n_ada = w_ada.shape[2]

    w_in_g, c_all = _all_gather([w_in[0].astype(MXU_DTYPE), jnp.broadcast_to(c, (8, D))], "gather_w_in_c")

    c64 = c_all.reshape(8 * N_DEV, D)
    b_loc = lax.dynamic_slice(b_ada, (0, me * n_ada), (1, n_ada))
    mod_part = _mod_fwd(c64, w_ada[0], b_loc)[::8]
    (mod_all,) = _all_gather_small([mod_part], "gather_mod")
    mod = lax.dynamic_slice(mod_all, (0, me, 0), (N_DEV, 1, n_ada)).reshape(3, D)
    mod3 = jnp.concatenate([mod, jnp.zeros((5, D), F32)], axis=0)

    onorm_t = hgrn_onorm_g
    gx, (r_in, r_a, r_b, r_o), row = _local_step(
        x[0], loss_target[0], mod3, norm_g, w_in_g, onorm_t, w_branch_a[0].astype(MXU_DTYPE),
        w_branch_b[0].astype(MXU_DTYPE), w_out[0].astype(MXU_DTYPE), rel_bias, hgrn_lb,
        final_g.reshape(1, D))
    packed8 = jnp.concatenate([row[None, :], jnp.zeros((7, P_END), F32)], axis=0)
    (packed,) = _all_gather_small([packed8], "gather_small")
    packed = packed.reshape(8 * N_DEV, P_END)

    g_in, d_in, nm_in, nv_in = _adamw_big(r_in, w_in[0], m_w_in[0], v_w_in[0], "adamw_w_in")
    g_a, d_a, nm_a, nv_a = _adamw_big(r_a, w_branch_a[0], m_w_branch_a[0], v_w_branch_a[0], "adamw_w_branch_a")
    g_b, d_b, nm_b, nv_b = _adamw_big(r_b, w_branch_b[0], m_w_branch_b[0], v_w_branch_b[0], "adamw_w_branch_b")
    g_o, d_o, nm_o, nv_o = _adamw_big(r_o, w_out[0], m_w_out[0], v_w_out[0], "adamw_w_out")

    dmod64 = lax.dynamic_slice(packed, (0, P_MOD + me * n_ada), (8 * N_DEV, n_ada))
    g_ada, d_ada, nm_ada, nv_ada = _adamw_w_ada(c64, dmod64, w_ada[0], m_w_ada[0], v_w_ada[0])

    def flat_relb(t):
        return jnp.pad(t.T, ((0, 0), (0, 128 - N_BUCKETS))).reshape(1, A_HEADS * 128)

    def unflat_relb(t):
        return t.reshape(A_HEADS, 128)[:, :N_BUCKETS].T

    fg2 = lambda t: t.reshape(1, D)
    smalls = _adamw_small(
        packed, b_ada, norm_g, hgrn_onorm_g, flat_relb(rel_bias), hgrn_lb, fg2(final_g),
        [m_b_ada, m_norm_g, m_hgrn_onorm_g, flat_relb(m_rel_bias), m_hgrn_lb, fg2(m_final_g)],
        [v_b_ada, v_norm_g, v_hgrn_onorm_g, flat_relb(v_rel_bias), v_hgrn_lb, fg2(v_final_g)])
    loss = smalls[0][0, 0]

    def small(kind):
        s = smalls[1 + 6 * kind:7 + 6 * kind]
        return s[0], s[1], s[2], unflat_relb(s[3]), s[4], s[5].reshape(D)

    def leaves(ada, sm, w_in_, wa_, wb_, wo_):
        b_, n_, o_, r_, l_, f_ = sm
        return (ada[None], b_, n_, w_in_[None], o_, wa_[None], wb_[None], wo_[None], r_, l_, f_)

    return (loss, gx[None],
            *leaves(g_ada, small(0), g_in, g_a, g_b, g_o),
            *leaves(d_ada, small(1), d_in, d_a, d_b, d_o),
            *leaves(nm_ada, small(2), nm_in, nm_a, nm_b, nm_o),
            *leaves(nv_ada, small(3), nv_in, nv_a, nv_b, nv_o))
```
